```python
import math
import jax
import jax.numpy as jnp
from jax import lax
import numpy as np

D_MODEL = 1024
BATCH = 8
SEQ = 2048
DEPTH = 4

HEAD_DIM = 64
H_SB = 4
H_DIFF = 4
DIFF_QK_DIM = HEAD_DIM // 2
DIFF_V_DIM = HEAD_DIM
H_DSA = 4
IDX_HEADS = 8
IDX_DIM = 64
DSA_TOPK_MAX = 256
H_MOBA = 4
MOBA_BLOCK = 256
MOBA_TOPK = 3
N_BRANCH = 4
N_SOFTMAX_HEADS = H_DIFF + H_DSA + H_MOBA
N_BUCKETS = 32
MAX_DISTANCE = 128
Q_BLOCK = 128
D_FF = ((8 * D_MODEL + 3 * 256 - 1) // (3 * 256)) * 256
NORM_EPS = 1e-6
N_IN = (3 * H_SB * HEAD_DIM + H_DIFF * (4 * DIFF_QK_DIM + DIFF_V_DIM) + 3 * H_DSA * HEAD_DIM + IDX_HEADS * IDX_DIM + IDX_DIM + IDX_HEADS + 3 * H_MOBA * HEAD_DIM + N_BRANCH * D_MODEL)

kernel_name = 'hybrid_gated_sparse_trunk'


def _rms_norm(x, g):
    xf = x.astype(jnp.float32)
    y = xf * lax.rsqrt(jnp.mean(xf * xf, axis=-1, keepdims=True) + NORM_EPS)
    return (y * g.astype(jnp.float32)).astype(x.dtype)


def _split_sizes():
    hd = HEAD_DIM
    return ((H_SB * hd,) * 3
            + (H_DIFF * DIFF_QK_DIM,) * 4 + (H_DIFF * DIFF_V_DIM,)
            + (H_DSA * hd,) * 3 + (IDX_HEADS * IDX_DIM, IDX_DIM, IDX_HEADS)
            + (H_MOBA * hd,) * 3
            + (N_BRANCH * D_MODEL,))


def _split_columns(z):
    points, acc = [], 0
    for sz in _split_sizes()[:-1]:
        acc += sz
        points.append(acc)
    return jnp.split(z, points, axis=-1)


def _seq_blocks(t):
    b, s = t.shape[:2]
    t = t.reshape((b, s // Q_BLOCK, Q_BLOCK) + t.shape[2:])
    return jnp.moveaxis(t, 1, 0)


def _merge_blocks(t):
    t = jnp.moveaxis(t, 0, 1)
    return t.reshape((t.shape[0], t.shape[1] * t.shape[2]) + t.shape[3:])


def _t5_bucket(dist):
    max_exact = N_BUCKETS // 2
    d = jnp.maximum(dist, 0)
    log_ratio = jnp.log(jnp.maximum(d, 1).astype(jnp.float32) / max_exact) / math.log(MAX_DISTANCE / max_exact)
    large = jnp.minimum(max_exact + (log_ratio * (N_BUCKETS - max_exact)).astype(jnp.int32), N_BUCKETS - 1)
    return jnp.where(d < max_exact, d, large)


def _t5_bias(table, dist):
    return jnp.moveaxis(table.astype(jnp.float32)[_t5_bucket(dist)], -1, 0)


def _stick_breaking_attention(q, k, v):
    s = q.shape[1]
    pos = jnp.arange(s, dtype=jnp.int32)
    scale = HEAD_DIM ** -0.5

    def block(args):
        qb, qpos = args
        z = jnp.einsum('bqhd,bkhd->bhqk', qb, k).astype(jnp.float32) * scale
        strict = pos[None, :] < qpos[:, None]
        sp = jnp.where(strict, jax.nn.softplus(z), 0.0)
        tail = lax.cumsum(sp, axis=3, reverse=True) - sp
        a = jnp.where(strict, jnp.exp(jax.nn.log_sigmoid(z) - tail), 0.0)
        return jnp.einsum('bhqk,bkhd->bqhd', a.astype(v.dtype), v)

    return _merge_blocks(lax.map(block, (_seq_blocks(q), pos.reshape(-1, Q_BLOCK))))


def _differential_attention(q1, q2, k1, k2, v, lam, bias_table):
    s = q1.shape[1]
    pos = jnp.arange(s, dtype=jnp.int32)
    scale = DIFF_QK_DIM ** -0.5

    def block(args):
        qb1, qb2, qpos = args
        causal = pos[None, :] <= qpos[:, None]
        bias = _t5_bias(bias_table, qpos[:, None] - pos[None, :])

        def probs(qb, kk):
            lg = jnp.einsum('bqhd,bkhd->bhqk', qb, kk).astype(jnp.float32) * scale + bias
            return jax.nn.softmax(jnp.where(causal, lg, -jnp.inf), axis=-1)

        w = probs(qb1, k1) - lam * probs(qb2, k2)
        return jnp.einsum('bhqk,bkhd->bqhd', w.astype(v.dtype), v)

    return _merge_blocks(lax.map(block, (_seq_blocks(q1), _seq_blocks(q2), pos.reshape(-1, Q_BLOCK))))


def _dsa_attention(q, k, v, q_idx, k_idx, w_idx, bias_table):
    s = q.shape[1]
    pos = jnp.arange(s, dtype=jnp.int32)
    topk = min(DSA_TOPK_MAX, s // 4)
    scale = HEAD_DIM ** -0.5
    gather = jax.vmap(lambda t, i: t[i])

    def block(args):
        qb, qib, wb, qpos = args
        rel = jax.nn.relu(jnp.einsum('bqhd,bkd->bqhk', qib, k_idx).astype(jnp.float32) * IDX_DIM ** -0.5)
        score = jnp.einsum('bqh,bqhk->bqk', wb.astype(jnp.float32), rel)
        score = jnp.where((pos[None, :] <= qpos[:, None])[None], score, -jnp.inf)
        top_score, sel = lax.top_k(score, topk)
        valid = (top_score > -jnp.inf)[:, None]
        kg = gather(k, sel)
        vg = gather(v, sel)
        bias = jnp.moveaxis(bias_table.astype(jnp.float32)[_t5_bucket(qpos[None, :, None] - sel)], -1, 1)
        lg = jnp.einsum('bqhd,bqkhd->bhqk', qb, kg).astype(jnp.float32) * scale + bias
        p = jax.nn.softmax(jnp.where(valid, lg, -jnp.inf), axis=-1)
        return jnp.einsum('bhqk,bqkhd->bqhd', p.astype(v.dtype), vg)

    xs = (_seq_blocks(q), _seq_blocks(q_idx), _seq_blocks(w_idx), pos.reshape(-1, Q_BLOCK))
    return _merge_blocks(lax.map(block, xs))


def _moba_attention(q, k, v, bias_table):
    b, s, h, d = q.shape
    pos = jnp.arange(s, dtype=jnp.int32)
    nblk = -(-s // MOBA_BLOCK)
    pad = nblk * MOBA_BLOCK - s

    def to_blocks(t):
        t = jnp.pad(jnp.swapaxes(t, 1, 2), ((0, 0), (0, 0), (0, pad), (0, 0)))
        return t.reshape(b, h, nblk, MOBA_BLOCK, d)

    kb, vb = to_blocks(k), to_blocks(v)
    kmean = jnp.mean(kb, axis=3)
    topb = min(MOBA_TOPK, nblk - 1)
    head_ids = jnp.arange(h)
    in_blk = jnp.arange(MOBA_BLOCK, dtype=jnp.int32)
    scale = d ** -0.5
    gather = jax.vmap(jax.vmap(lambda t, i: t[i]))
    table_t = bias_table.astype(jnp.float32).T

    def block(args):
        qb, qpos = args
        own = qpos[0] // MOBA_BLOCK
        k_own = lax.dynamic_index_in_dim(kb, own, axis=2, keepdims=False)
        v_own = lax.dynamic_index_in_dim(vb, own, axis=2, keepdims=False)
        own_pos = own * MOBA_BLOCK + in_blk
        lg_own = (jnp.einsum('bqhd,bhkd->bhqk', qb, k_own).astype(jnp.float32) * scale
                  + _t5_bias(bias_table, qpos[:, None] - own_pos[None, :]))
        lg_own = jnp.where(own_pos[None, :] <= qpos[:, None], lg_own, -jnp.inf)
        if topb == 0:
            p = jax.nn.softmax(lg_own, axis=-1)
            return jnp.einsum('bhqk,bhkd->bqhd', p.astype(v.dtype), v_own)
        gate = jnp.einsum('bqhd,bhnd->bhqn', qb, kmean).astype(jnp.float32)
        gate = jnp.where(jnp.arange(nblk) < own, gate, -jnp.inf)
        g_score, sel = lax.top_k(gate, topb)
        valid = (g_score > -jnp.inf)[..., None]
        kg = gather(kb, sel)
        vg = gather(vb, sel)
        past_pos = sel[..., None] * MOBA_BLOCK + in_blk
        bias_past = table_t[head_ids[None, :, None, None, None], _t5_bucket(qpos[None, None, :, None, None] - past_pos)]
        lg_past = jnp.einsum('bqhd,bhqnkd->bhqnk', qb, kg).astype(jnp.float32) * scale + bias_past
        lg_past = jnp.where(valid, lg_past, -jnp.inf)
        nq = qb.shape[1]
        p = jax.nn.softmax(jnp.concatenate([lg_past.reshape(b, h, nq, topb * MOBA_BLOCK), lg_own], axis=-1), axis=-1)
        p_past = p[..., :topb * MOBA_BLOCK].reshape(b, h, nq, topb, MOBA_BLOCK).astype(v.dtype)
        p_own = p[..., topb * MOBA_BLOCK:].astype(v.dtype)
        return (jnp.einsum('bhqnk,bhqnkd->bqhd', p_past, vg)
                + jnp.einsum('bhqk,bhkd->bqhd', p_own, v_own))

    return _merge_blocks(lax.map(block, (_seq_blocks(q), pos.reshape(-1, Q_BLOCK))))


def _token_mixers(h, w_in, w_br_sb, w_br_diff, w_br_dsa, w_br_moba, w_out, lam, lam_init, diff_subln_g, rel_bias):
    b, s, _ = h.shape
    (q_sb, k_sb, v_sb, q1, q2, k1, k2, v_df, q_ds, k_ds, v_ds, qi_ds, ki_ds, wi_ds,
     q_mb, k_mb, v_mb, gate_logits) = _split_columns(h @ w_in)

    def heads(t, n):
        return t.reshape(b, s, n, -1)

    def flat(t):
        return t.reshape(b, s, -1)

    bias_df = rel_bias[:, :H_DIFF]
    bias_ds = rel_bias[:, H_DIFF:H_DIFF + H_DSA]
    bias_mb = rel_bias[:, H_DIFF + H_DSA:]
    o_sb = _stick_breaking_attention(heads(q_sb, H_SB), heads(k_sb, H_SB), heads(v_sb, H_SB))
    o_df = _differential_attention(heads(q1, H_DIFF), heads(q2, H_DIFF), heads(k1, H_DIFF), heads(k2, H_DIFF),
                                   heads(v_df, H_DIFF), lam, bias_df)
    o_df = _rms_norm(o_df, diff_subln_g) * (1.0 - lam_init)
    o_ds = _dsa_attention(heads(q_ds, H_DSA), heads(k_ds, H_DSA), heads(v_ds, H_DSA),
                          heads(qi_ds, IDX_HEADS), ki_ds, wi_ds * IDX_HEADS ** -0.5, bias_ds)
    o_mb = _moba_attention(heads(q_mb, H_MOBA), heads(k_mb, H_MOBA), heads(v_mb, H_MOBA), bias_mb)
    g = jax.nn.sigmoid(gate_logits.astype(jnp.float32)).astype(h.dtype).reshape(b, s, N_BRANCH, D_MODEL)
    merged = (g[:, :, 0] * (flat(o_sb) @ w_br_sb)
              + g[:, :, 1] * (flat(o_df) @ w_br_diff)
              + g[:, :, 2] * (flat(o_ds) @ w_br_dsa)
              + g[:, :, 3] * (flat(o_mb) @ w_br_moba))
    return merged @ w_out


def _swiglu(h, w_ffn_in, w_ffn_out):
    gate, up = jnp.split(h @ w_ffn_in, 2, axis=-1)
    return (jax.nn.silu(gate) * up) @ w_ffn_out


def setup_inputs(seed: int = 0) -> dict:
    key = jax.random.key(seed)
    ks = jax.random.split(key, 19)
    f32 = jnp.float32

    def nrm(k, shape, fan_in):
        return jax.random.normal(k, shape, f32) * fan_in ** -0.5

    def gain(k, shape):
        return 1.0 + 0.05 * jax.random.normal(k, shape, f32)

    br_in = HEAD_DIM * 4
    return {
        'x': jax.random.normal(ks[0], (BATCH, SEQ, D_MODEL), f32),
        'w_in': nrm(ks[1], (DEPTH, D_MODEL, N_IN), D_MODEL),
        'w_br_sb': nrm(ks[2], (DEPTH, H_SB * HEAD_DIM, D_MODEL), br_in),
        'w_br_diff': nrm(ks[3], (DEPTH, H_DIFF * DIFF_V_DIM, D_MODEL), br_in),
        'w_br_dsa': nrm(ks[4], (DEPTH, H_DSA * HEAD_DIM, D_MODEL), br_in),
        'w_br_moba': nrm(ks[5], (DEPTH, H_MOBA * HEAD_DIM, D_MODEL), br_in),
        'w_out': nrm(ks[6], (DEPTH, D_MODEL, D_MODEL), D_MODEL),
        'lambda_q1': 0.1 * jax.random.normal(ks[7], (DEPTH, DIFF_QK_DIM), f32),
        'lambda_k1': 0.1 * jax.random.normal(ks[8], (DEPTH, DIFF_QK_DIM), f32),
        'lambda_q2': 0.1 * jax.random.normal(ks[9], (DEPTH, DIFF_QK_DIM), f32),
        'lambda_k2': 0.1 * jax.random.normal(ks[10], (DEPTH, DIFF_QK_DIM), f32),
        'diff_subln_g': gain(ks[11], (DEPTH, DIFF_V_DIM)),
        'rel_bias': 0.1 * jax.random.normal(ks[12], (N_BUCKETS, N_SOFTMAX_HEADS), f32),
        'w_ffn_in': nrm(ks[13], (DEPTH, D_MODEL, 2 * D_FF), D_MODEL),
        'w_ffn_out': nrm(ks[14], (DEPTH, D_FF, D_MODEL), D_FF),
        'g_pre_mix': gain(ks[15], (DEPTH, D_MODEL)),
        'g_post_mix': gain(ks[16], (DEPTH, D_MODEL)),
        'g_pre_ffn': gain(ks[17], (DEPTH, D_MODEL)),
        'g_post_ffn': gain(ks[18], (DEPTH, D_MODEL)),
    }


def reference(x, w_in, w_br_sb, w_br_diff, w_br_dsa, w_br_moba, w_out, lambda_q1, lambda_k1, lambda_q2, lambda_k2,
              diff_subln_g, rel_bias, w_ffn_in, w_ffn_out, g_pre_mix, g_post_mix, g_pre_ffn, g_post_ffn):
    f32 = jnp.float32
    for l in range(DEPTH):
        lam_init = 0.8 - 0.6 * math.exp(-0.3 * l)
        lam = (jnp.exp(jnp.sum(lambda_q1[l].astype(f32) * lambda_k1[l].astype(f32)))
               - jnp.exp(jnp.sum(lambda_q2[l].astype(f32) * lambda_k2[l].astype(f32))) + lam_init)
        h = _rms_norm(x, g_pre_mix[l])
        y = _token_mixers(h, w_in[l], w_br_sb[l], w_br_diff[l], w_br_dsa[l], w_br_moba[l], w_out[l],
                          lam, lam_init, diff_subln_g[l], rel_bias)
        x = x + _rms_norm(y, g_post_mix[l])
        h = _rms_norm(x, g_pre_ffn[l])
        x = x + _rms_norm(_swiglu(h, w_ffn_in[l], w_ffn_out[l]), g_post_ffn[l])
    return x
```

```python
import functools
import math

import numpy as np
import jax
import jax.numpy as jnp
from jax import lax
from jax.experimental import pallas as pl
from jax.experimental.pallas import tpu as pltpu

F32 = jnp.float32
BF16 = jnp.bfloat16

D_MODEL = 1024
HEAD_DIM = 64
N_HEADS = 4
DIFF_QK_DIM = 32
IDX_HEADS = 8
DSA_TOPK_MAX = 256
MOBA_BLOCK = 256
MOBA_TOPK = 3
N_BUCKETS = 32
MAX_DISTANCE = 128
D_FF = 2816
NORM_EPS = 1e-6

TQ = 256
TK = 256
NEG = -1e30
HALF_NEG = -0.5e30
BIG = 3e38
N_BISECT = 32
TM_PROJ = 256
TM_MERGE = 256
TM_FFN = 256
N_SLAB = 60
N_PACK = N_SLAB * HEAD_DIM
VMEM_LIMIT = 56 * 1024 * 1024

G_SB_Q, G_SB_K, G_SB_V = 0, 1, 2
G_DF_Q, G_DF_K, G_DF_V = 3, 4, 5
G_DS_Q, G_DS_K, G_DS_V = 6, 7, 8
G_QI_A, G_QI_B = 9, 10
G_MB_Q, G_MB_K, G_MB_V = 11, 12, 13
G_IDX = 14
S_KIDX = 56


def _pack_layout():
    off = {}
    acc = 0
    for name, sz in (("q_sb", 256), ("k_sb", 256), ("v_sb", 256), ("q1", 128), ("q2", 128), ("k1", 128),
                     ("k2", 128), ("v_df", 256), ("q_ds", 256), ("k_ds", 256), ("v_ds", 256), ("qi", 512),
                     ("ki", 64), ("wi", 8), ("q_mb", 256), ("k_mb", 256), ("v_mb", 256), ("gate", 4096)):
        off[name] = acc
        acc += sz
    cols, scale = [], []

    def add(start, n, s=1.0):
        cols.extend(range(start, start + n))
        scale.extend([s] * n)

    hd = HEAD_DIM ** -0.5
    add(off["q_sb"], 256, hd); add(off["k_sb"], 256); add(off["v_sb"], 256)
    for h in range(N_HEADS):
        add(off["q1"] + h * 32, 32, DIFF_QK_DIM ** -0.5); add(off["q2"] + h * 32, 32, DIFF_QK_DIM ** -0.5)
    for h in range(N_HEADS):
        add(off["k1"] + h * 32, 32); add(off["k2"] + h * 32, 32)
    add(off["v_df"], 256)
    add(off["q_ds"], 256, hd); add(off["k_ds"], 256); add(off["v_ds"], 256)
    add(off["qi"], 512, HEAD_DIM ** -0.5)
    add(off["q_mb"], 256, hd); add(off["k_mb"], 256); add(off["v_mb"], 256)
    add(off["ki"], 64); add(off["wi"], 8, IDX_HEADS ** -0.5)
    cols.extend([-1] * 184); scale.extend([1.0] * 184)
    assert len(cols) == N_PACK
    return np.asarray(cols, np.int32), np.asarray(scale, np.float32), off["gate"]


_PACK_SRC, _PACK_SCALE, _GATE_OFF = _pack_layout()


def _dot(a, b):
    return jnp.dot(a, b, preferred_element_type=F32)


def _dot_nt(a, b):
    return lax.dot_general(a, b, (((1,), (1,)), ((), ())), preferred_element_type=F32)


def _rms(x, g):
    return x * lax.rsqrt(jnp.mean(x * x, axis=-1, keepdims=True) + NORM_EPS) * g


def _cparams(n_axes):
    return pltpu.CompilerParams(dimension_semantics=("arbitrary",) * n_axes, vmem_limit_bytes=VMEM_LIMIT)


def _const_spec(shape):
    nd = len(shape)
    return pl.BlockSpec(shape, lambda *_: (0,) * nd, pipeline_mode=pl.Buffered(1))


def _proj_kernel(x_ref, g_ref, w_ref, cs_ref, zz_ref, aux_ref):
    h = _rms(x_ref[...], g_ref[...]).astype(BF16)
    for c in range(N_SLAB // 4):
        r = _dot(h, w_ref[:, c * 256:(c + 1) * 256]) * cs_ref[:, c * 256:(c + 1) * 256]
        for s in range(4):
            zz_ref[4 * c + s] = r[:, s * HEAD_DIM:(s + 1) * HEAD_DIM].astype(BF16)
        if c == G_IDX:
            aux_ref[...] = r[:, 0:128]


def _proj(x, g, w, cs):
    t = x.shape[0]
    tm = TM_PROJ
    return pl.pallas_call(
        _proj_kernel,
        grid=(t // tm,),
        in_specs=[pl.BlockSpec((tm, D_MODEL), lambda i: (i, 0)),
                  _const_spec((1, D_MODEL)),
                  _const_spec((D_MODEL, N_PACK)),
                  _const_spec((1, N_PACK))],
        out_specs=[pl.BlockSpec((N_SLAB, tm, HEAD_DIM), lambda i: (0, i, 0)),
                   pl.BlockSpec((tm, 128), lambda i: (i, 0))],
        out_shape=[jax.ShapeDtypeStruct((N_SLAB, t, HEAD_DIM), BF16),
                   jax.ShapeDtypeStruct((t, 128), F32)],
        compiler_params=_cparams(1),
        name="proj",
    )(x, g, w, cs)


def _kv_block(ref, h, j):
    return ref[h, pl.ds(pl.multiple_of(j * TK, TK), TK), :]


def _softmax_step(lg, vj, carry):
    m, l, acc = carry
    m_new = jnp.maximum(m, jnp.max(lg, axis=-1, keepdims=True))
    alpha = jnp.exp(m - m_new)
    p = jnp.exp(lg - m_new)
    l = alpha * l + jnp.sum(p, axis=-1, keepdims=True)
    acc = alpha * acc + _dot(p.astype(BF16), vj)
    return m_new, l, acc


def _softmax_init():
    return (jnp.full((TQ, 1), NEG, F32), jnp.zeros((TQ, 1), F32), jnp.zeros((TQ, HEAD_DIM), F32))


def _attn_specs(nq, gq, gk, gv, s):
    return [pl.BlockSpec((4, TQ, HEAD_DIM), lambda b, i: (gq, b * nq + i, 0)),
            pl.BlockSpec((4, s, HEAD_DIM), lambda b, i: (gk, b, 0)),
            pl.BlockSpec((4, s, HEAD_DIM), lambda b, i: (gv, b, 0))]


def _out_spec(nq):
    return pl.BlockSpec((TQ, N_HEADS * HEAD_DIM), lambda b, i: (b * nq + i, 0))


def _sb_kernel(q_ref, k_ref, v_ref, tri_ref, mask_ref, o_ref):
    i = pl.program_id(1)
    tri = tri_ref[...]
    outs = []
    for h in range(N_HEADS):
        q = q_ref[h]

        def block(j, carry, masked, q=q, h=h):
            tail, acc = carry
            z = _dot_nt(q, _kv_block(k_ref, h, j))
            if masked:
                z = z + mask_ref[...]
            sp = jnp.maximum(z, 0.0) + jnp.log1p(jnp.exp(-jnp.abs(z)))
            sp_hi = sp.astype(BF16)
            sp_lo = (sp - sp_hi.astype(F32)).astype(BF16)
            c = _dot(sp_hi, tri) + _dot(sp_lo, tri) + tail
            a = jnp.exp(z - c)
            acc = acc + _dot(a.astype(BF16), _kv_block(v_ref, h, j))
            return c[:, 0:1], acc

        carry = block(i, (jnp.zeros((TQ, 1), F32), jnp.zeros((TQ, HEAD_DIM), F32)), True)
        carry = lax.fori_loop(0, i, lambda t, c, block=block: block(i - 1 - t, c, False), carry)
        outs.append(carry[1])
    o_ref[...] = jnp.concatenate(outs, axis=-1).astype(BF16)


def _sb_attention(zz, b, s, tri, mask):
    nq = s // TQ
    return pl.pallas_call(
        _sb_kernel,
        grid=(b, nq),
        in_specs=_attn_specs(nq, G_SB_Q, G_SB_K, G_SB_V, s) + [_const_spec((TK, TK)), _const_spec((TQ, TK))],
        out_specs=_out_spec(nq),
        out_shape=jax.ShapeDtypeStruct((b * s, N_HEADS * HEAD_DIM), BF16),
        compiler_params=_cparams(2),
        name="sb_attn",
    )(zz, zz, zz, tri, mask)


def _diff_kernel(q_ref, k_ref, v_ref, bt_ref, lam_ref, cst_ref, g_ref, o_ref):
    i = pl.program_id(1)
    lp = lam_ref[...]
    lam_init = cst_ref[:, 0:1]
    lam = (jnp.exp(jnp.sum(lp[0:1] * lp[1:2], axis=-1, keepdims=True))
           - jnp.exp(jnp.sum(lp[2:3] * lp[3:4], axis=-1, keepdims=True)) + lam_init)
    lane = lax.broadcasted_iota(jnp.int32, (TQ, HEAD_DIM), 1)
    outs = []
    for h in range(N_HEADS):
        q = q_ref[h]
        q1 = jnp.where(lane < DIFF_QK_DIM, q, jnp.zeros_like(q))
        q2 = jnp.where(lane >= DIFF_QK_DIM, q, jnp.zeros_like(q))

        def block(j, carry, q1=q1, q2=q2, h=h):
            c1, c2 = carry
            kj = _kv_block(k_ref, h, j)
            vj = _kv_block(v_ref, h, j)
            bias = bt_ref[h, jnp.minimum(i - j, 2)]
            c1 = _softmax_step(_dot_nt(q1, kj) + bias, vj, c1)
            c2 = _softmax_step(_dot_nt(q2, kj) + bias, vj, c2)
            return c1, c2

        (_, l1, a1), (_, l2, a2) = lax.fori_loop(0, i + 1, block, (_softmax_init(), _softmax_init()))
        o = a1 / l1 - lam * (a2 / l2)
        outs.append(_rms(o, g_ref[...]) * (1.0 - lam_init))
    o_ref[...] = jnp.concatenate(outs, axis=-1).astype(BF16)


def _diff_attention(zz, b, s, bt, lamp, cst, g):
    nq = s // TQ
    return pl.pallas_call(
        _diff_kernel,
        grid=(b, nq),
        in_specs=_attn_specs(nq, G_DF_Q, G_DF_K, G_DF_V, s) + [
            _const_spec((N_HEADS, 3, TQ, TK)), _const_spec((4, DIFF_QK_DIM)), _const_spec((1, 128)),
            _const_spec((1, HEAD_DIM))],
        out_specs=_out_spec(nq),
        out_shape=jax.ShapeDtypeStruct((b * s, N_HEADS * HEAD_DIM), BF16),
        compiler_params=_cparams(2),
        name="diff_attn",
    )(zz, zz, zz, bt, lamp, cst, g)


def _dsa_kernel(q_ref, k_ref, v_ref, qia_ref, qib_ref, ki_ref, aux_ref, bt_ref, o_ref, sc_ref, *, topk, seq):
    i = pl.program_id(1)
    nb = i + 1
    kf = float(topk)
    w = aux_ref[:, 64:64 + IDX_HEADS]
    row = lax.broadcasted_iota(jnp.int32, (TQ, TK), 0)
    col = lax.broadcasted_iota(jnp.int32, (TQ, TK), 1)

    def score_block(j, _):
        kij = _kv_block(ki_ref, 0, j)
        sc = jnp.zeros((TQ, TK), F32)
        for hh in range(IDX_HEADS):
            qi = (qia_ref if hh < 4 else qib_ref)[hh % 4]
            sc = sc + w[:, hh:hh + 1] * jnp.maximum(_dot_nt(qi, kij), 0.0)
        sc_ref[j] = sc
        return 0

    lax.fori_loop(0, nb, score_block, 0)
    sc_ref[i] = jnp.where(row >= col, sc_ref[i], NEG)

    def reduce_blocks(fn, init):
        return lax.fori_loop(0, nb, lambda j, c: fn(sc_ref[j], j, c), init)

    def count_ge(t):
        return reduce_blocks(
            lambda x, j, c: c + jnp.sum(jnp.where(x >= t, 1.0, 0.0), axis=-1, keepdims=True),
            jnp.zeros((TQ, 1), F32))

    def minmax(x, j, c):
        return (jnp.minimum(c[0], jnp.min(jnp.where(x > HALF_NEG, x, BIG), axis=-1, keepdims=True)),
                jnp.maximum(c[1], jnp.max(x, axis=-1, keepdims=True)))

    lo, hi = reduce_blocks(minmax, (jnp.full((TQ, 1), BIG, F32), jnp.full((TQ, 1), NEG, F32)))

    def bisect(_, c):
        lo, hi = c
        mid = 0.5 * lo + 0.5 * hi
        ge = count_ge(mid) >= kf
        return jnp.where(ge, mid, lo), jnp.where(ge, hi, mid)

    lo, hi = lax.fori_loop(0, N_BISECT, bisect, (lo, hi))
    c_lo = count_ge(lo)
    c_hi = count_ge(hi)
    hi_ok = c_hi < kf
    c_above = jnp.where(hi_ok, c_hi, 0.0)
    hi_sel = jnp.where(hi_ok, hi, BIG)
    need = kf - c_above
    n_valid = i * TQ + lax.broadcasted_iota(jnp.int32, (TQ, 1), 0) + 1
    take_all = n_valid <= topk
    tied = jnp.max(jnp.where(take_all, 0.0, (c_lo - c_above) - need)) > 0.0

    def index_bound():
        def count_band_le(mb):
            def f(x, j, c):
                idx = j * TK + col
                inb = jnp.where(x >= lo, jnp.where(x < hi_sel, jnp.where(idx <= mb, 1.0, 0.0), 0.0), 0.0)
                return c + jnp.sum(inb, axis=-1, keepdims=True)
            return reduce_blocks(f, jnp.zeros((TQ, 1), F32))

        def step(_, c):
            lo_i, hi_i = c
            mid = lax.shift_right_arithmetic(lo_i + hi_i, 1)
            ge = count_band_le(mid) >= need
            return jnp.where(ge, lo_i, mid), jnp.where(ge, mid, hi_i)

        n_steps = int(math.ceil(math.log2(seq))) + 1
        _, hi_i = lax.fori_loop(0, n_steps, step,
                                (jnp.full((TQ, 1), -1, jnp.int32), jnp.full((TQ, 1), seq - 1, jnp.int32)))
        return hi_i

    mb = lax.cond(tied, index_bound, lambda: jnp.full((TQ, 1), seq, jnp.int32))
    lo_sel = jnp.where(take_all, HALF_NEG, lo)
    mb_sel = jnp.where(take_all, seq, mb)

    def write_mask(j, _):
        x = sc_ref[j]
        idx = j * TK + col
        sel = (x >= lo_sel) & ((x >= hi_sel) | (idx <= mb_sel))
        sc_ref[j] = jnp.where(sel, 0.0, NEG)
        return 0

    lax.fori_loop(0, nb, write_mask, 0)

    outs = []
    for h in range(N_HEADS):
        q = q_ref[h]

        def block(j, carry, q=q, h=h):
            lg = _dot_nt(q, _kv_block(k_ref, h, j)) + bt_ref[h, jnp.minimum(i - j, 2)] + sc_ref[j]
            return _softmax_step(lg, _kv_block(v_ref, h, j), carry)

        _, l, acc = lax.fori_loop(0, nb, block, _softmax_init())
        outs.append(acc / l)
    o_ref[...] = jnp.concatenate(outs, axis=-1).astype(BF16)


def _dsa_attention(zz, aux, b, s, bt):
    nq = s // TQ
    topk = min(DSA_TOPK_MAX, s // 4)
    return pl.pallas_call(
        functools.partial(_dsa_kernel, topk=topk, seq=s),
        grid=(b, nq),
        in_specs=_attn_specs(nq, G_DS_Q, G_DS_K, G_DS_V, s) + [
            pl.BlockSpec((4, TQ, HEAD_DIM), lambda b_, i: (G_QI_A, b_ * nq + i, 0)),
            pl.BlockSpec((4, TQ, HEAD_DIM), lambda b_, i: (G_QI_B, b_ * nq + i, 0)),
            pl.BlockSpec((1, s, HEAD_DIM), lambda b_, i: (S_KIDX, b_, 0)),
            pl.BlockSpec((TQ, 128), lambda b_, i: (b_ * nq + i, 0)),
            _const_spec((N_HEADS, 3, TQ, TK))],
        out_specs=_out_spec(nq),
        out_shape=jax.ShapeDtypeStruct((b * s, N_HEADS * HEAD_DIM), BF16),
        scratch_shapes=[pltpu.VMEM((nq, TQ, TK), F32)],
        compiler_params=_cparams(2),
        name="dsa_attn",
    )(zz, zz, zz, zz, zz, zz, aux, bt)


def _moba_kernel(q_ref, k_ref, v_ref, bt_ref, o_ref, km_ref, *, nblk, topb):
    i = pl.program_id(1)
    nrow = km_ref.shape[1]

    @pl.when(i == 0)
    def _():
        km_ref[...] = jnp.zeros_like(km_ref)
        for h in range(N_HEADS):
            for n in range(nblk):
                kb = k_ref[h, n * MOBA_BLOCK:(n + 1) * MOBA_BLOCK, :].astype(F32)
                km_ref[h, n:n + 1, :] = jnp.mean(kb, axis=0, keepdims=True)

    blk = lax.broadcasted_iota(jnp.int32, (nrow, TQ), 0)
    past = blk < i
    bits_rows = []
    for h in range(N_HEADS):
        gate = _dot_nt(km_ref[h].astype(BF16), q_ref[h])
        bits = jnp.zeros((1, TQ), F32)
        for n in range(nblk):
            gn = gate[n:n + 1, :]
            beats = jnp.where(past, jnp.where(gate > gn, 1.0, jnp.where(gate == gn, jnp.where(blk < n, 1.0, 0.0), 0.0)), 0.0)
            rank = jnp.sum(beats, axis=0, keepdims=True)
            chosen = jnp.where(rank < float(topb), jnp.where(n < i, float(2 ** n), 0.0), 0.0)
            bits = bits + chosen
        bits_rows.append(bits)
    bits_t = jnp.concatenate(bits_rows + [jnp.zeros((8 - N_HEADS, TQ), F32)], axis=0)
    bits_q = jnp.transpose(bits_t).astype(jnp.int32)

    outs = []
    for h in range(N_HEADS):
        q = q_ref[h]
        hbits = bits_q[:, h:h + 1]
        own = _softmax_step(_dot_nt(q, _kv_block(k_ref, h, i)) + bt_ref[h, 0], _kv_block(v_ref, h, i),
                            _softmax_init())

        def block(n, carry, q=q, h=h, hbits=hbits):
            picked = (lax.shift_right_logical(hbits, jnp.full_like(hbits, n)) & 1) == 1
            lg = (_dot_nt(q, _kv_block(k_ref, h, n)) + bt_ref[h, jnp.minimum(i - n, 2)]
                  + jnp.where(picked, 0.0, NEG))
            return _softmax_step(lg, _kv_block(v_ref, h, n), carry)

        _, l, acc = lax.fori_loop(0, i, block, own)
        outs.append(acc / l)
    o_ref[...] = jnp.concatenate(outs, axis=-1).astype(BF16)


def _moba_attention(zz, b, s, bt):
    nq = s // TQ
    nblk = s // MOBA_BLOCK
    topb = min(MOBA_TOPK, nblk - 1)
    return pl.pallas_call(
        functools.partial(_moba_kernel, nblk=nblk, topb=topb),
        grid=(b, nq),
        in_specs=_attn_specs(nq, G_MB_Q, G_MB_K, G_MB_V, s) + [_const_spec((N_HEADS, 3, TQ, TK))],
        out_specs=_out_spec(nq),
        out_shape=jax.ShapeDtypeStruct((b * s, N_HEADS * HEAD_DIM), BF16),
        scratch_shapes=[pltpu.VMEM((N_HEADS, max(8, nblk), HEAD_DIM), F32)],
        compiler_params=_cparams(2),
        name="moba_attn",
    )(zz, zz, zz, bt)


def _merge_kernel(x_ref, osb_ref, odf_ref, ods_ref, omb_ref, gpre_ref, wg_ref, wbr_ref, wout_ref, gpost_ref, o_ref):
    x = x_ref[...]
    h = _rms(x, gpre_ref[...]).astype(BF16)
    y = jnp.zeros((x.shape[0], D_MODEL), F32)
    for r, o_r in enumerate((osb_ref, odf_ref, ods_ref, omb_ref)):
        gate = jax.nn.sigmoid(_dot(h, wg_ref[:, r * D_MODEL:(r + 1) * D_MODEL]))
        y = y + gate * _dot(o_r[...], wbr_ref[r])
    o_ref[...] = x + _rms(_dot(y.astype(BF16), wout_ref[...]), gpost_ref[...])


def _merge(x, o_sb, o_df, o_ds, o_mb, g_pre, w_gate, w_br, w_out, g_post):
    t = x.shape[0]
    tm = TM_MERGE
    tok = lambda width: pl.BlockSpec((tm, width), lambda i: (i, 0))
    return pl.pallas_call(
        _merge_kernel,
        grid=(t // tm,),
        in_specs=[tok(D_MODEL), tok(256), tok(256), tok(256), tok(256),
                  _const_spec((1, D_MODEL)), _const_spec((D_MODEL, 4 * D_MODEL)),
                  _const_spec((4, 256, D_MODEL)), _const_spec((D_MODEL, D_MODEL)), _const_spec((1, D_MODEL))],
        out_specs=tok(D_MODEL),
        out_shape=jax.ShapeDtypeStruct((t, D_MODEL), F32),
        compiler_params=_cparams(1),
        name="merge",
    )(x, o_sb, o_df, o_ds, o_mb, g_pre, w_gate, w_br, w_out, g_post)


def _ffn_kernel(x_ref, gpre_ref, win_ref, wout_ref, gpost_ref, o_ref):
    x = x_ref[...]
    h = _rms(x, gpre_ref[...]).astype(BF16)
    gate = _dot(h, win_ref[:, 0:D_FF])
    up = _dot(h, win_ref[:, D_FF:2 * D_FF])
    act = (gate * jax.nn.sigmoid(gate) * up).astype(BF16)
    o_ref[...] = x + _rms(_dot(act, wout_ref[...]), gpost_ref[...])


def _ffn(x, g_pre, w_in, w_out, g_post):
    t = x.shape[0]
    tm = TM_FFN
    return pl.pallas_call(
        _ffn_kernel,
        grid=(t // tm,),
        in_specs=[pl.BlockSpec((tm, D_MODEL), lambda i: (i, 0)), _const_spec((1, D_MODEL)),
                  _const_spec((D_MODEL, 2 * D_FF)), _const_spec((D_FF, D_MODEL)), _const_spec((1, D_MODEL))],
        out_specs=pl.BlockSpec((tm, D_MODEL), lambda i: (i, 0)),
        out_shape=jax.ShapeDtypeStruct((t, D_MODEL), F32),
        compiler_params=_cparams(1),
        name="ffn",
    )(x, g_pre, w_in, w_out, g_post)


def _t5_bucket(dist):
    max_exact = N_BUCKETS // 2
    d = jnp.maximum(dist, 0)
    log_ratio = jnp.log(jnp.maximum(d, 1).astype(F32) / max_exact) / math.log(MAX_DISTANCE / max_exact)
    large = jnp.minimum(max_exact + (log_ratio * (N_BUCKETS - max_exact)).astype(jnp.int32), N_BUCKETS - 1)
    return jnp.where(d < max_exact, d, large)


def _bias_tiles(rel_bias):
    r = np.arange(TQ)[:, None]
    c = np.arange(TK)[None, :]
    dist = jnp.asarray(np.stack([o * TK + r - c for o in range(3)]), jnp.int32)
    bias = jnp.moveaxis(rel_bias.astype(F32)[_t5_bucket(dist)], -1, 0)
    return jnp.where(dist[None] >= 0, bias, NEG)


def _pack_weights(w_in):
    parts, start = [], 0
    for e in range(1, N_PACK + 1):
        if e == N_PACK or _PACK_SRC[e] != _PACK_SRC[e - 1] + (1 if _PACK_SRC[e - 1] >= 0 else 0):
            a = int(_PACK_SRC[start])
            n = e - start
            parts.append(w_in[:, :, a:a + n] if a >= 0 else jnp.zeros(w_in.shape[:2] + (n,), w_in.dtype))
            start = e
    return jnp.concatenate(parts, axis=2).astype(BF16)


def kernel(x, w_in, w_br_sb, w_br_diff, w_br_dsa, w_br_moba, w_out, lambda_q1, lambda_k1, lambda_q2, lambda_k2,
           diff_subln_g, rel_bias, w_ffn_in, w_ffn_out, g_pre_mix, g_post_mix, g_pre_ffn, g_post_ffn):
    b, s, d = x.shape
    depth = w_in.shape[0]
    assert d == D_MODEL and s % TQ == 0 and s // MOBA_BLOCK >= 2
    t = b * s

    w_pack = _pack_weights(w_in)
    w_gate = w_in[:, :, _GATE_OFF:].astype(BF16)
    w_br = jnp.stack([w_br_sb, w_br_diff, w_br_dsa, w_br_moba], axis=1).astype(BF16)
    w_o = w_out.astype(BF16)
    w_f1 = w_ffn_in.astype(BF16)
    w_f2 = w_ffn_out.astype(BF16)
    cs = jnp.asarray(_PACK_SCALE)[None, :]

    bt = _bias_tiles(rel_bias)
    bt_df, bt_ds, bt_mb = bt[0:4], bt[4:8], bt[8:12]
    r = np.arange(TQ)[:, None]
    c = np.arange(TK)[None, :]
    tri = jnp.asarray(r >= c, BF16)
    sb_mask = jnp.asarray(np.where(r > c, 0.0, NEG), F32)

    xf = x.reshape(t, d)
    for l in range(depth):
        lam_init = 0.8 - 0.6 * math.exp(-0.3 * l)
        lamp = jnp.stack([lambda_q1[l], lambda_k1[l], lambda_q2[l], lambda_k2[l]]).astype(F32)
        cst = jnp.full((1, 128), lam_init, F32)
        zz, aux = _proj(xf, g_pre_mix[l][None, :], w_pack[l], cs)
        o_sb = _sb_attention(zz, b, s, tri, sb_mask)
        o_df = _diff_attention(zz, b, s, bt_df, lamp, cst, diff_subln_g[l][None, :])
        o_ds = _dsa_attention(zz, aux, b, s, bt_ds)
        o_mb = _moba_attention(zz, b, s, bt_mb)
        xf = _merge(xf, o_sb, o_df, o_ds, o_mb, g_pre_mix[l][None, :], w_gate[l], w_br[l], w_o[l],
                    g_post_mix[l][None, :])
        xf = _ffn(xf, g_pre_ffn[l][None, :], w_f1[l], w_f2[l], g_post_ffn[l][None, :])
    return xf.reshape(b, s, d)
```

```python
import functools
import math

import numpy as np
import jax
import jax.numpy as jnp
from jax import lax
from jax.experimental import pallas as pl
from jax.experimental.pallas import tpu as pltpu

F32 = jnp.float32
BF16 = jnp.bfloat16

D_MODEL = 1024
HEAD_DIM = 64
N_HEADS = 4
DIFF_QK_DIM = 32
IDX_HEADS = 8
DSA_TOPK_MAX = 256
MOBA_BLOCK = 256
MOBA_TOPK = 3
N_BUCKETS = 32
MAX_DISTANCE = 128
D_FF = 2816
NORM_EPS = 1e-6

TQ = 256
TK = 256
NEG = -1e30
HALF_NEG = -0.5e30
BIG = 3e38
N_BISECT = 32
TM_PROJ = 256
TM_MERGE = 256
TM_FFN = 256
N_SLAB = 60
N_PACK = N_SLAB * HEAD_DIM
VMEM_LIMIT = 56 * 1024 * 1024

G_SB_Q, G_SB_K, G_SB_V = 0, 1, 2
G_DF_Q, G_DF_K, G_DF_V = 3, 4, 5
G_DS_Q, G_DS_K, G_DS_V = 6, 7, 8
G_QI_A, G_QI_B = 9, 10
G_MB_Q, G_MB_K, G_MB_V = 11, 12, 13
G_IDX = 14
S_KIDX = 56


def _pack_layout():
    off = {}
    acc = 0
    for name, sz in (("q_sb", 256), ("k_sb", 256), ("v_sb", 256), ("q1", 128), ("q2", 128), ("k1", 128),
                     ("k2", 128), ("v_df", 256), ("q_ds", 256), ("k_ds", 256), ("v_ds", 256), ("qi", 512),
                     ("ki", 64), ("wi", 8), ("q_mb", 256), ("k_mb", 256), ("v_mb", 256), ("gate", 4096)):
        off[name] = acc
        acc += sz
    cols, scale = [], []

    def add(start, n, s=1.0):
        cols.extend(range(start, start + n))
        scale.extend([s] * n)

    hd = HEAD_DIM ** -0.5
    add(off["q_sb"], 256, hd); add(off["k_sb"], 256); add(off["v_sb"], 256)
    for h in range(N_HEADS):
        add(off["q1"] + h * 32, 32, DIFF_QK_DIM ** -0.5); add(off["q2"] + h * 32, 32, DIFF_QK_DIM ** -0.5)
    for h in range(N_HEADS):
        add(off["k1"] + h * 32, 32); add(off["k2"] + h * 32, 32)
    add(off["v_df"], 256)
    add(off["q_ds"], 256, hd); add(off["k_ds"], 256); add(off["v_ds"], 256)
    add(off["qi"], 512, HEAD_DIM ** -0.5)
    add(off["q_mb"], 256, hd); add(off["k_mb"], 256); add(off["v_mb"], 256)
    add(off["ki"], 64); add(off["wi"], 8, IDX_HEADS ** -0.5)
    cols.extend([-1] * 184); scale.extend([1.0] * 184)
    assert len(cols) == N_PACK
    return np.asarray(cols, np.int32), np.asarray(scale, np.float32), off["gate"]


_PACK_SRC, _PACK_SCALE, _GATE_OFF = _pack_layout()


def _dot(a, b):
    return jnp.dot(a, b, preferred_element_type=F32)


def _dot_nt(a, b):
    return lax.dot_general(a, b, (((1,), (1,)), ((), ())), preferred_element_type=F32)


def _rms(x, g):
    return x * lax.rsqrt(jnp.mean(x * x, axis=-1, keepdims=True) + NORM_EPS) * g


def _cparams(n_axes):
    return pltpu.CompilerParams(dimension_semantics=("arbitrary",) * n_axes, vmem_limit_bytes=VMEM_LIMIT)


def _const_spec(shape):
    nd = len(shape)
    return pl.BlockSpec(shape, lambda *_: (0,) * nd, pipeline_mode=pl.Buffered(1))


def _proj_kernel(x_ref, g_ref, w_ref, cs_ref, zz_ref, aux_ref):
    h = _rms(x_ref[...], g_ref[...]).astype(BF16)
    for c in range(N_SLAB // 4):
        r = _dot(h, w_ref[:, c * 256:(c + 1) * 256]) * cs_ref[:, c * 256:(c + 1) * 256]
        for s in range(4):
            zz_ref[4 * c + s] = r[:, s * HEAD_DIM:(s + 1) * HEAD_DIM].astype(BF16)
        if c == G_IDX:
            aux_ref[...] = r[:, 0:128]


def _proj(x, g, w, cs):
    t = x.shape[0]
    tm = TM_PROJ
    return pl.pallas_call(
        _proj_kernel,
        grid=(t // tm,),
        in_specs=[pl.BlockSpec((tm, D_MODEL), lambda i: (i, 0)),
                  _const_spec((1, D_MODEL)),
                  _const_spec((D_MODEL, N_PACK)),
                  _const_spec((1, N_PACK))],
        out_specs=[pl.BlockSpec((N_SLAB, tm, HEAD_DIM), lambda i: (0, i, 0)),
                   pl.BlockSpec((tm, 128), lambda i: (i, 0))],
        out_shape=[jax.ShapeDtypeStruct((N_SLAB, t, HEAD_DIM), BF16),
                   jax.ShapeDtypeStruct((t, 128), F32)],
        compiler_params=_cparams(1),
        name="proj",
    )(x, g, w, cs)


def _kv_block(ref, h, j):
    return ref[h, pl.ds(pl.multiple_of(j * TK, TK), TK), :]


def _softmax_step(lg, vj, carry):
    m, l, acc = carry
    m_new = jnp.maximum(m, jnp.max(lg, axis=-1, keepdims=True))
    alpha = jnp.exp(m - m_new)
    p = jnp.exp(lg - m_new)
    l = alpha * l + jnp.sum(p, axis=-1, keepdims=True)
    acc = alpha * acc + _dot(p.astype(BF16), vj)
    return m_new, l, acc


def _softmax_init():
    return (jnp.full((TQ, 1), NEG, F32), jnp.zeros((TQ, 1), F32), jnp.zeros((TQ, HEAD_DIM), F32))


def _attn_specs(nq, gq, gk, gv, s):
    return [pl.BlockSpec((4, TQ, HEAD_DIM), lambda b, i: (gq, b * nq + i, 0)),
            pl.BlockSpec((4, s, HEAD_DIM), lambda b, i: (gk, b, 0)),
            pl.BlockSpec((4, s, HEAD_DIM), lambda b, i: (gv, b, 0))]


def _out_spec(nq):
    return pl.BlockSpec((TQ, N_HEADS * HEAD_DIM), lambda b, i: (b * nq + i, 0))


def _sb_kernel(q_ref, k_ref, v_ref, tri_ref, mask_ref, o_ref):
    i = pl.program_id(1)
    tri = tri_ref[...]

    def head_step(h, j, carry, masked):
        tail, acc = carry
        z = _dot_nt(q_ref[h], _kv_block(k_ref, h, j))
        if masked:
            z = z + mask_ref[...]
        sp = jnp.maximum(z, 0.0) + jnp.log1p(jnp.exp(-jnp.abs(z)))
        sp_hi = sp.astype(BF16)
        sp_lo = (sp - sp_hi.astype(F32)).astype(BF16)
        c = _dot(sp_hi, tri) + _dot(sp_lo, tri) + tail
        a = jnp.exp(z - c)
        acc = acc + _dot(a.astype(BF16), _kv_block(v_ref, h, j))
        return c[:, 0:1], acc

    def block(j, carry, masked):
        return tuple(head_step(h, j, carry[h], masked) for h in range(N_HEADS))

    init = tuple((jnp.zeros((TQ, 1), F32), jnp.zeros((TQ, HEAD_DIM), F32)) for _ in range(N_HEADS))
    carry = block(i, init, True)
    carry = lax.fori_loop(0, i, lambda t, c: block(i - 1 - t, c, False), carry)
    o_ref[...] = jnp.concatenate([c[1] for c in carry], axis=-1).astype(BF16)


def _sb_attention(zz, b, s, tri, mask):
    nq = s // TQ
    return pl.pallas_call(
        _sb_kernel,
        grid=(b, nq),
        in_specs=_attn_specs(nq, G_SB_Q, G_SB_K, G_SB_V, s) + [_const_spec((TK, TK)), _const_spec((TQ, TK))],
        out_specs=_out_spec(nq),
        out_shape=jax.ShapeDtypeStruct((b * s, N_HEADS * HEAD_DIM), BF16),
        compiler_params=_cparams(2),
        name="sb_attn",
    )(zz, zz, zz, tri, mask)


def _diff_kernel(q_ref, k_ref, v_ref, bt_ref, lam_ref, cst_ref, g_ref, o_ref):
    i = pl.program_id(1)
    lp = lam_ref[...]
    lam_init = cst_ref[:, 0:1]
    lam = (jnp.exp(jnp.sum(lp[0:1] * lp[1:2], axis=-1, keepdims=True))
           - jnp.exp(jnp.sum(lp[2:3] * lp[3:4], axis=-1, keepdims=True)) + lam_init)
    lane = lax.broadcasted_iota(jnp.int32, (TQ, HEAD_DIM), 1)
    qs = []
    for h in range(N_HEADS):
        q = q_ref[h]
        qs.append((jnp.where(lane < DIFF_QK_DIM, q, jnp.zeros_like(q)),
                   jnp.where(lane >= DIFF_QK_DIM, q, jnp.zeros_like(q))))

    def block(j, carry):
        new = []
        for h in range(N_HEADS):
            kj = _kv_block(k_ref, h, j)
            vj = _kv_block(v_ref, h, j)
            bias = bt_ref[h, jnp.minimum(i - j, 2)]
            new.append(_softmax_step(_dot_nt(qs[h][0], kj) + bias, vj, carry[2 * h]))
            new.append(_softmax_step(_dot_nt(qs[h][1], kj) + bias, vj, carry[2 * h + 1]))
        return tuple(new)

    carry = lax.fori_loop(0, i + 1, block, tuple(_softmax_init() for _ in range(2 * N_HEADS)))
    outs = []
    for h in range(N_HEADS):
        (_, l1, a1), (_, l2, a2) = carry[2 * h], carry[2 * h + 1]
        o = a1 / l1 - lam * (a2 / l2)
        outs.append(_rms(o, g_ref[...]) * (1.0 - lam_init))
    o_ref[...] = jnp.concatenate(outs, axis=-1).astype(BF16)


def _diff_attention(zz, b, s, bt, lamp, cst, g):
    nq = s // TQ
    return pl.pallas_call(
        _diff_kernel,
        grid=(b, nq),
        in_specs=_attn_specs(nq, G_DF_Q, G_DF_K, G_DF_V, s) + [
            _const_spec((N_HEADS, 3, TQ, TK)), _const_spec((4, DIFF_QK_DIM)), _const_spec((1, 128)),
            _const_spec((1, HEAD_DIM))],
        out_specs=_out_spec(nq),
        out_shape=jax.ShapeDtypeStruct((b * s, N_HEADS * HEAD_DIM), BF16),
        compiler_params=_cparams(2),
        name="diff_attn",
    )(zz, zz, zz, bt, lamp, cst, g)


def _dsa_kernel(q_ref, k_ref, v_ref, qia_ref, qib_ref, ki_ref, aux_ref, bt_ref, o_ref, sc_ref, *, topk, seq):
    i = pl.program_id(1)
    nb = i + 1
    kf = float(topk)
    w = aux_ref[:, 64:64 + IDX_HEADS]
    row = lax.broadcasted_iota(jnp.int32, (TQ, TK), 0)
    col = lax.broadcasted_iota(jnp.int32, (TQ, TK), 1)

    def score_block(j, _):
        kij = _kv_block(ki_ref, 0, j)
        sc = jnp.zeros((TQ, TK), F32)
        for hh in range(IDX_HEADS):
            qi = (qia_ref if hh < 4 else qib_ref)[hh % 4]
            sc = sc + w[:, hh:hh + 1] * jnp.maximum(_dot_nt(qi, kij), 0.0)
        sc_ref[j] = sc
        return 0

    lax.fori_loop(0, nb, score_block, 0)
    sc_ref[i] = jnp.where(row >= col, sc_ref[i], NEG)

    def reduce_blocks(fn, init):
        return lax.fori_loop(0, nb, lambda j, c: fn(sc_ref[j], j, c), init)

    def fold(a, op=jnp.add):
        return op(a[:, :TK // 2], a[:, TK // 2:])

    def lane_sum(a):
        return jnp.sum(a, axis=-1, keepdims=True)

    zero_part = jnp.zeros((TQ, TK // 2), F32)

    def count_ge(t):
        tb = jnp.broadcast_to(t, (TQ, TK))
        return lane_sum(reduce_blocks(lambda x, j, c: c + fold(jnp.where(x >= tb, 1.0, 0.0)), zero_part))

    def minmax(x, j, c):
        return (jnp.minimum(c[0], fold(jnp.where(x > HALF_NEG, x, BIG), jnp.minimum)),
                jnp.maximum(c[1], fold(x, jnp.maximum)))

    lo, hi = reduce_blocks(minmax, (jnp.full((TQ, TK // 2), BIG, F32), jnp.full((TQ, TK // 2), NEG, F32)))
    lo = jnp.min(lo, axis=-1, keepdims=True)
    hi = jnp.max(hi, axis=-1, keepdims=True)

    def bisect(_, c):
        lo, hi = c
        mid = 0.5 * lo + 0.5 * hi
        ge = count_ge(mid) >= kf
        return jnp.where(ge, mid, lo), jnp.where(ge, hi, mid)

    lo, hi = lax.fori_loop(0, N_BISECT, bisect, (lo, hi))
    c_lo = count_ge(lo)
    c_hi = count_ge(hi)
    hi_ok = c_hi < kf
    c_above = jnp.where(hi_ok, c_hi, 0.0)
    hi_sel = jnp.where(hi_ok, hi, BIG)
    need = kf - c_above
    n_valid = i * TQ + lax.broadcasted_iota(jnp.int32, (TQ, 1), 0) + 1
    take_all = n_valid <= topk
    tied = jnp.max(jnp.where(take_all, 0.0, (c_lo - c_above) - need)) > 0.0

    def index_bound():
        def count_band_le(mb):
            def f(x, j, c):
                idx = j * TK + col
                inb = jnp.where(x >= lo, jnp.where(x < hi_sel, jnp.where(idx <= mb, 1.0, 0.0), 0.0), 0.0)
                return c + fold(inb)
            return lane_sum(reduce_blocks(f, zero_part))

        def step(_, c):
            lo_i, hi_i = c
            mid = lax.shift_right_arithmetic(lo_i + hi_i, 1)
            ge = count_band_le(mid) >= need
            return jnp.where(ge, lo_i, mid), jnp.where(ge, mid, hi_i)

        n_steps = int(math.ceil(math.log2(seq))) + 1
        _, hi_i = lax.fori_loop(0, n_steps, step,
                                (jnp.full((TQ, 1), -1, jnp.int32), jnp.full((TQ, 1), seq - 1, jnp.int32)))
        return hi_i

    mb = lax.cond(tied, index_bound, lambda: jnp.full((TQ, 1), seq, jnp.int32))
    lo_sel = jnp.where(take_all, HALF_NEG, lo)
    mb_sel = jnp.where(take_all, seq, mb)

    def write_mask(j, _):
        x = sc_ref[j]
        idx = j * TK + col
        sel = (x >= lo_sel) & ((x >= hi_sel) | (idx <= mb_sel))
        sc_ref[j] = jnp.where(sel, 0.0, NEG)
        return 0

    lax.fori_loop(0, nb, write_mask, 0)

    def block(j, carry):
        new = []
        for h in range(N_HEADS):
            lg = _dot_nt(q_ref[h], _kv_block(k_ref, h, j)) + bt_ref[h, jnp.minimum(i - j, 2)] + sc_ref[j]
            new.append(_softmax_step(lg, _kv_block(v_ref, h, j), carry[h]))
        return tuple(new)

    carry = lax.fori_loop(0, nb, block, tuple(_softmax_init() for _ in range(N_HEADS)))
    o_ref[...] = jnp.concatenate([acc / l for _, l, acc in carry], axis=-1).astype(BF16)


def _dsa_attention(zz, aux, b, s, bt):
    nq = s // TQ
    topk = min(DSA_TOPK_MAX, s // 4)
    return pl.pallas_call(
        functools.partial(_dsa_kernel, topk=topk, seq=s),
        grid=(b, nq),
        in_specs=_attn_specs(nq, G_DS_Q, G_DS_K, G_DS_V, s) + [
            pl.BlockSpec((4, TQ, HEAD_DIM), lambda b_, i: (G_QI_A, b_ * nq + i, 0)),
            pl.BlockSpec((4, TQ, HEAD_DIM), lambda b_, i: (G_QI_B, b_ * nq + i, 0)),
            pl.BlockSpec((1, s, HEAD_DIM), lambda b_, i: (S_KIDX, b_, 0)),
            pl.BlockSpec((TQ, 128), lambda b_, i: (b_ * nq + i, 0)),
            _const_spec((N_HEADS, 3, TQ, TK))],
        out_specs=_out_spec(nq),
        out_shape=jax.ShapeDtypeStruct((b * s, N_HEADS * HEAD_DIM), BF16),
        scratch_shapes=[pltpu.VMEM((nq, TQ, TK), F32)],
        compiler_params=_cparams(2),
        name="dsa_attn",
    )(zz, zz, zz, zz, zz, zz, aux, bt)


def _moba_kernel(q_ref, k_ref, v_ref, bt_ref, o_ref, km_ref, *, nblk, topb):
    i = pl.program_id(1)
    nrow = km_ref.shape[1]

    @pl.when(i == 0)
    def _():
        km_ref[...] = jnp.zeros_like(km_ref)
        for h in range(N_HEADS):
            for n in range(nblk):
                kb = k_ref[h, n * MOBA_BLOCK:(n + 1) * MOBA_BLOCK, :].astype(F32)
                km_ref[h, n:n + 1, :] = jnp.mean(kb, axis=0, keepdims=True)

    blk = lax.broadcasted_iota(jnp.int32, (nrow, TQ), 0)
    past = blk < i
    bits_rows = []
    for h in range(N_HEADS):
        gate = _dot_nt(km_ref[h].astype(BF16), q_ref[h])
        bits = jnp.zeros((1, TQ), F32)
        for n in range(nblk):
            gn = gate[n:n + 1, :]
            beats = jnp.where(past, jnp.where(gate > gn, 1.0, jnp.where(gate == gn, jnp.where(blk < n, 1.0, 0.0), 0.0)), 0.0)
            rank = jnp.sum(beats, axis=0, keepdims=True)
            chosen = jnp.where(rank < float(topb), jnp.where(n < i, float(2 ** n), 0.0), 0.0)
            bits = bits + chosen
        bits_rows.append(bits)
    bits_t = jnp.concatenate(bits_rows + [jnp.zeros((8 - N_HEADS, TQ), F32)], axis=0)
    bits_q = jnp.transpose(bits_t).astype(jnp.int32)

    own = tuple(_softmax_step(_dot_nt(q_ref[h], _kv_block(k_ref, h, i)) + bt_ref[h, 0], _kv_block(v_ref, h, i),
                              _softmax_init()) for h in range(N_HEADS))

    def block(n, carry):
        new = []
        for h in range(N_HEADS):
            hbits = bits_q[:, h:h + 1]
            picked = (lax.shift_right_logical(hbits, jnp.full_like(hbits, n)) & 1) == 1
            lg = (_dot_nt(q_ref[h], _kv_block(k_ref, h, n)) + bt_ref[h, jnp.minimum(i - n, 2)]
                  + jnp.where(picked, 0.0, NEG))
            new.append(_softmax_step(lg, _kv_block(v_ref, h, n), carry[h]))
        return tuple(new)

    carry = lax.fori_loop(0, i, block, own)
    o_ref[...] = jnp.concatenate([acc / l for _, l, acc in carry], axis=-1).astype(BF16)


def _moba_attention(zz, b, s, bt):
    nq = s // TQ
    nblk = s // MOBA_BLOCK
    topb = min(MOBA_TOPK, nblk - 1)
    return pl.pallas_call(
        functools.partial(_moba_kernel, nblk=nblk, topb=topb),
        grid=(b, nq),
        in_specs=_attn_specs(nq, G_MB_Q, G_MB_K, G_MB_V, s) + [_const_spec((N_HEADS, 3, TQ, TK))],
        out_specs=_out_spec(nq),
        out_shape=jax.ShapeDtypeStruct((b * s, N_HEADS * HEAD_DIM), BF16),
        scratch_shapes=[pltpu.VMEM((N_HEADS, max(8, nblk), HEAD_DIM), F32)],
        compiler_params=_cparams(2),
        name="moba_attn",
    )(zz, zz, zz, bt)


def _merge_kernel(x_ref, osb_ref, odf_ref, ods_ref, omb_ref, gpre_ref, wg_ref, wbr_ref, wout_ref, gpost_ref, o_ref):
    x = x_ref[...]
    h = _rms(x, gpre_ref[...]).astype(BF16)
    y = jnp.zeros((x.shape[0], D_MODEL), F32)
    for r, o_r in enumerate((osb_ref, odf_ref, ods_ref, omb_ref)):
        gate = jax.nn.sigmoid(_dot(h, wg_ref[:, r * D_MODEL:(r + 1) * D_MODEL]))
        y = y + gate * _dot(o_r[...], wbr_ref[r])
    o_ref[...] = x + _rms(_dot(y.astype(BF16), wout_ref[...]), gpost_ref[...])


def _merge(x, o_sb, o_df, o_ds, o_mb, g_pre, w_gate, w_br, w_out, g_post):
    t = x.shape[0]
    tm = TM_MERGE
    tok = lambda width: pl.BlockSpec((tm, width), lambda i: (i, 0))
    return pl.pallas_call(
        _merge_kernel,
        grid=(t // tm,),
        in_specs=[tok(D_MODEL), tok(256), tok(256), tok(256), tok(256),
                  _const_spec((1, D_MODEL)), _const_spec((D_MODEL, 4 * D_MODEL)),
                  _const_spec((4, 256, D_MODEL)), _const_spec((D_MODEL, D_MODEL)), _const_spec((1, D_MODEL))],
        out_specs=tok(D_MODEL),
        out_shape=jax.ShapeDtypeStruct((t, D_MODEL), F32),
        compiler_params=_cparams(1),
        name="merge",
    )(x, o_sb, o_df, o_ds, o_mb, g_pre, w_gate, w_br, w_out, g_post)


def _ffn_kernel(x_ref, gpre_ref, win_ref, wout_ref, gpost_ref, o_ref):
    x = x_ref[...]
    h = _rms(x, gpre_ref[...]).astype(BF16)
    gate = _dot(h, win_ref[:, 0:D_FF])
    up = _dot(h, win_ref[:, D_FF:2 * D_FF])
    act = (gate * jax.nn.sigmoid(gate) * up).astype(BF16)
    o_ref[...] = x + _rms(_dot(act, wout_ref[...]), gpost_ref[...])


def _ffn(x, g_pre, w_in, w_out, g_post):
    t = x.shape[0]
    tm = TM_FFN
    return pl.pallas_call(
        _ffn_kernel,
        grid=(t // tm,),
        in_specs=[pl.BlockSpec((tm, D_MODEL), lambda i: (i, 0)), _const_spec((1, D_MODEL)),
                  _const_spec((D_MODEL, 2 * D_FF)), _const_spec((D_FF, D_MODEL)), _const_spec((1, D_MODEL))],
        out_specs=pl.BlockSpec((tm, D_MODEL), lambda i: (i, 0)),
        out_shape=jax.ShapeDtypeStruct((t, D_MODEL), F32),
        compiler_params=_cparams(1),
        name="ffn",
    )(x, g_pre, w_in, w_out, g_post)


def _t5_bucket(dist):
    max_exact = N_BUCKETS // 2
    d = jnp.maximum(dist, 0)
    log_ratio = jnp.log(jnp.maximum(d, 1).astype(F32) / max_exact) / math.log(MAX_DISTANCE / max_exact)
    large = jnp.minimum(max_exact + (log_ratio * (N_BUCKETS - max_exact)).astype(jnp.int32), N_BUCKETS - 1)
    return jnp.where(d < max_exact, d, large)


def _bias_tiles(rel_bias):
    assert TQ == TK
    n = TK
    nh = rel_bias.shape[1]
    d = np.arange(-(n - 1), 3 * n)
    by_dist = rel_bias.astype(F32).T[:, _t5_bucket(jnp.asarray(np.maximum(d, 0), jnp.int32))]
    by_dist = jnp.where(jnp.asarray(d >= 0)[None, :], by_dist, NEG)
    tiles = []
    for o in range(3):
        w = by_dist[:, o * n:o * n + 2 * n - 1][:, ::-1]
        w = jnp.concatenate([w, jnp.zeros((nh, 1), F32)], axis=1)
        flat = jnp.broadcast_to(w[:, None, :], (nh, n, 2 * n)).reshape(nh, 2 * n * n)
        tiles.append(flat[:, :n * (2 * n - 1)].reshape(nh, n, 2 * n - 1)[:, :, n - 1:])
    return jnp.stack(tiles, axis=1)


def _pack_weights(w_in):
    parts, start = [], 0
    for e in range(1, N_PACK + 1):
        if e == N_PACK or _PACK_SRC[e] != _PACK_SRC[e - 1] + (1 if _PACK_SRC[e - 1] >= 0 else 0):
            a = int(_PACK_SRC[start])
            n = e - start
            parts.append(w_in[:, :, a:a + n] if a >= 0 else jnp.zeros(w_in.shape[:2] + (n,), w_in.dtype))
            start = e
    return jnp.concatenate(parts, axis=2).astype(BF16)


def kernel(x, w_in, w_br_sb, w_br_diff, w_br_dsa, w_br_moba, w_out, lambda_q1, lambda_k1, lambda_q2, lambda_k2,
           diff_subln_g, rel_bias, w_ffn_in, w_ffn_out, g_pre_mix, g_post_mix, g_pre_ffn, g_post_ffn):
    b, s, d = x.shape
    depth = w_in.shape[0]
    assert d == D_MODEL and s % TQ == 0 and s // MOBA_BLOCK >= 2
    t = b * s

    w_pack = _pack_weights(w_in)
    w_gate = w_in[:, :, _GATE_OFF:].astype(BF16)
    w_br = jnp.stack([w_br_sb, w_br_diff, w_br_dsa, w_br_moba], axis=1).astype(BF16)
    w_o = w_out.astype(BF16)
    w_f1 = w_ffn_in.astype(BF16)
    w_f2 = w_ffn_out.astype(BF16)
    cs = jnp.asarray(_PACK_SCALE)[None, :]

    bt = _bias_tiles(rel_bias)
    bt_df, bt_ds, bt_mb = bt[0:4], bt[4:8], bt[8:12]
    r = np.arange(TQ)[:, None]
    c = np.arange(TK)[None, :]
    tri = jnp.asarray(r >= c, BF16)
    sb_mask = jnp.asarray(np.where(r > c, 0.0, NEG), F32)

    xf = x.reshape(t, d)
    for l in range(depth):
        lam_init = 0.8 - 0.6 * math.exp(-0.3 * l)
        lamp = jnp.stack([lambda_q1[l], lambda_k1[l], lambda_q2[l], lambda_k2[l]]).astype(F32)
        cst = jnp.full((1, 128), lam_init, F32)
        zz, aux = _proj(xf, g_pre_mix[l][None, :], w_pack[l], cs)
        o_sb = _sb_attention(zz, b, s, tri, sb_mask)
        o_df = _diff_attention(zz, b, s, bt_df, lamp, cst, diff_subln_g[l][None, :])
        o_ds = _dsa_attention(zz, aux, b, s, bt_ds)
        o_mb = _moba_attention(zz, b, s, bt_mb)
        xf = _merge(xf, o_sb, o_df, o_ds, o_mb, g_pre_mix[l][None, :], w_gate[l], w_br[l], w_o[l],
                    g_post_mix[l][None, :])
        xf = _ffn(xf, g_pre_ffn[l][None, :], w_f1[l], w_f2[l], g_post_ffn[l][None, :])
    return xf.reshape(b, s, d)
```

```python
import functools
import math

import numpy as np
import jax
import jax.numpy as jnp
from jax import lax
from jax.experimental import pallas as pl
from jax.experimental.pallas import tpu as pltpu

F32 = jnp.float32
BF16 = jnp.bfloat16

D_MODEL = 1024
HEAD_DIM = 64
N_HEADS = 4
N_MIXERS = 4
DIFF_QK_DIM = 32
IDX_HEADS = 8
DSA_TOPK_MAX = 256
MOBA_BLOCK = 256
MOBA_TOPK = 3
N_BUCKETS = 32
MAX_DISTANCE = 128
D_FF = 2816
NORM_EPS = 1e-6

TQ = 256
TK = 256
NEG = -1e30
HALF_NEG = -0.5e30
BIG = 3e38
BISECT_WARMUP = 12
BISECT_TRIP = 4
BISECT_MAX_TRIPS = 70
TM_MERGE = 256
TM_FFN = 256
N_GROUP = 11
N_SLAB = 4 * N_GROUP
N_PACK = N_SLAB * HEAD_DIM
VMEM_LIMIT = 56 * 1024 * 1024

G_SB_Q, G_SB_K, G_DF_Q, G_DF_K, G_DS_Q, G_DS_K, G_QI_A, G_QI_B, G_MB_Q, G_MB_K, G_KIDX = range(N_GROUP)
S_KIDX = 4 * G_KIDX
M_SB, M_DF, M_DS, M_MB = range(N_MIXERS)


def _layout():
    off = {}
    acc = 0
    for name, sz in (("q_sb", 256), ("k_sb", 256), ("v_sb", 256), ("q1", 128), ("q2", 128), ("k1", 128),
                     ("k2", 128), ("v_df", 256), ("q_ds", 256), ("k_ds", 256), ("v_ds", 256), ("qi", 512),
                     ("ki", 64), ("wi", 8), ("q_mb", 256), ("k_mb", 256), ("v_mb", 256), ("gate", 4096)):
        off[name] = acc
        acc += sz
    return off


_OFF = _layout()


def _pack_layout():
    off = _OFF
    cols, scale = [], []

    def add(start, n, s=1.0):
        cols.extend(range(start, start + n))
        scale.extend([s] * n)

    hd = HEAD_DIM ** -0.5
    add(off["q_sb"], 256, hd); add(off["k_sb"], 256)
    for h in range(N_HEADS):
        add(off["q1"] + h * 32, 32, DIFF_QK_DIM ** -0.5); add(off["q2"] + h * 32, 32, DIFF_QK_DIM ** -0.5)
    for h in range(N_HEADS):
        add(off["k1"] + h * 32, 32); add(off["k2"] + h * 32, 32)
    add(off["q_ds"], 256, hd); add(off["k_ds"], 256)
    add(off["qi"], 512, HEAD_DIM ** -0.5)
    add(off["q_mb"], 256, hd); add(off["k_mb"], 256)
    add(off["ki"], 64)
    cols.extend([-1] * 192); scale.extend([1.0] * 192)
    assert len(cols) == N_PACK
    return np.asarray(cols, np.int32), np.asarray(scale, np.float32)


_PACK_SRC, _PACK_SCALE = _pack_layout()


def _dot(a, b):
    return jnp.dot(a, b, preferred_element_type=F32)


def _dot_nt(a, b):
    return lax.dot_general(a, b, (((1,), (1,)), ((), ())), preferred_element_type=F32)


def _rms(x, g):
    return x * lax.rsqrt(jnp.mean(x * x, axis=-1, keepdims=True) + NORM_EPS) * g


def _cparams(n_axes):
    return pltpu.CompilerParams(dimension_semantics=("arbitrary",) * n_axes, vmem_limit_bytes=VMEM_LIMIT)


def _const_spec(shape):
    nd = len(shape)
    return pl.BlockSpec(shape, lambda *_: (0,) * nd, pipeline_mode=pl.Buffered(1))


def _proj_kernel(x_ref, g_ref, w_ref, cs_ref, wvt_ref, wit_ref, zz_ref, vt_ref, wt_ref):
    h = _rms(x_ref[...], g_ref[...]).astype(BF16)
    for c in range(N_GROUP):
        r = _dot(h, w_ref[:, c * 256:(c + 1) * 256]) * cs_ref[:, c * 256:(c + 1) * 256]
        for s in range(4):
            zz_ref[4 * c + s] = r[:, s * HEAD_DIM:(s + 1) * HEAD_DIM].astype(BF16)
    for m in range(N_MIXERS):
        vt_ref[m, 0] = _dot_nt(wvt_ref[m], h).astype(BF16)
    wt_ref[...] = _dot_nt(wit_ref[...], h) * IDX_HEADS ** -0.5


def _proj(x, g, w, cs, wvt, wit):
    t = x.shape[0]
    tm = TQ
    return pl.pallas_call(
        _proj_kernel,
        grid=(t // tm,),
        in_specs=[pl.BlockSpec((tm, D_MODEL), lambda i: (i, 0)),
                  _const_spec((1, D_MODEL)),
                  _const_spec((D_MODEL, N_PACK)),
                  _const_spec((1, N_PACK)),
                  _const_spec((N_MIXERS, 256, D_MODEL)),
                  _const_spec((IDX_HEADS, D_MODEL))],
        out_specs=[pl.BlockSpec((N_SLAB, tm, HEAD_DIM), lambda i: (0, i, 0)),
                   pl.BlockSpec((N_MIXERS, 1, 256, tm), lambda i: (0, i, 0, 0)),
                   pl.BlockSpec((IDX_HEADS, tm), lambda i: (0, i))],
        out_shape=[jax.ShapeDtypeStruct((N_SLAB, t, HEAD_DIM), BF16),
                   jax.ShapeDtypeStruct((N_MIXERS, t // tm, 256, tm), BF16),
                   jax.ShapeDtypeStruct((IDX_HEADS, t), F32)],
        compiler_params=_cparams(1),
        name="proj",
    )(x, g, w, cs, wvt, wit)


def _k_block(ref, h, j):
    return ref[h, pl.ds(pl.multiple_of(j * TK, TK), TK), :]


def _vt_block(ref, h, j):
    return ref[0, j, h * HEAD_DIM:(h + 1) * HEAD_DIM, :]


def _softmax_block(s_list, vt_list, carry):
    ms = [jnp.maximum(c[0], jnp.max(s, axis=0, keepdims=True)) for s, c in zip(s_list, carry)]
    ps = [jnp.exp(s - m) for s, m in zip(s_list, ms)]
    pvs = [_dot(vt, p.astype(BF16)) for vt, p in zip(vt_list, ps)]
    new = []
    for (m, l, acc), m_new, p, pv in zip(carry, ms, ps, pvs):
        alpha = jnp.exp(m - m_new)
        new.append((m_new, alpha * l + jnp.sum(p, axis=0, keepdims=True), alpha * acc + pv))
    return tuple(new)


def _softmax_init():
    return (jnp.full((1, TQ), NEG, F32), jnp.zeros((1, TQ), F32), jnp.zeros((HEAD_DIM, TQ), F32))


def _store_heads(o_ref, heads_t):
    o_ref[...] = jnp.transpose(jnp.concatenate(heads_t, axis=0)).astype(BF16)


def _attn_specs(nq, gq, gk, mixer, s):
    return [pl.BlockSpec((4, TQ, HEAD_DIM), lambda b, i: (gq, b * nq + i, 0)),
            pl.BlockSpec((4, s, HEAD_DIM), lambda b, i: (gk, b, 0)),
            pl.BlockSpec((1, s // TK, 256, TK), lambda b, i: (mixer, b, 0, 0))]


def _out_spec(nq):
    return pl.BlockSpec((TQ, N_HEADS * HEAD_DIM), lambda b, i: (b * nq + i, 0))


def _sb_kernel(q_ref, k_ref, vt_ref, tri_ref, mask_ref, o_ref):
    i = pl.program_id(1)
    tri = tri_ref[...]

    def block(j, carry, masked):
        heads = range(N_HEADS)
        zs = [_dot_nt(_k_block(k_ref, h, j), q_ref[h]) for h in heads]
        if masked:
            zs = [z + mask_ref[...] for z in zs]
        sps = [jnp.maximum(z, 0.0) + jnp.log1p(jnp.exp(-jnp.abs(z))) for z in zs]
        his = [sp.astype(BF16) for sp in sps]
        los = [(sp - hi.astype(F32)).astype(BF16) for sp, hi in zip(sps, his)]
        cs = [_dot(tri, hi) + _dot(tri, lo) + c[0] for hi, lo, c in zip(his, los, carry)]
        avs = [_dot(_vt_block(vt_ref, h, j), jnp.exp(z - c).astype(BF16)) for h, z, c in zip(heads, zs, cs)]
        return tuple((c[0:1, :], old[1] + av) for c, old, av in zip(cs, carry, avs))

    init = tuple((jnp.zeros((1, TQ), F32), jnp.zeros((HEAD_DIM, TQ), F32)) for _ in range(N_HEADS))
    carry = block(i, init, True)
    carry = lax.fori_loop(0, i, lambda t, c: block(i - 1 - t, c, False), carry)
    _store_heads(o_ref, [c[1] for c in carry])


def _sb_attention(zz, vt, b, s, tri, mask):
    nq = s // TQ
    return pl.pallas_call(
        _sb_kernel,
        grid=(b, nq),
        in_specs=_attn_specs(nq, G_SB_Q, G_SB_K, M_SB, s) + [_const_spec((TK, TK)), _const_spec((TK, TQ))],
        out_specs=_out_spec(nq),
        out_shape=jax.ShapeDtypeStruct((b * s, N_HEADS * HEAD_DIM), BF16),
        compiler_params=_cparams(2),
        name="sb_attn",
    )(zz, zz, vt, tri, mask)


def _diff_kernel(q_ref, k_ref, vt_ref, bt_ref, lam_ref, cst_ref, g_ref, o_ref):
    i = pl.program_id(1)
    lp = lam_ref[...]
    lam_init = cst_ref[:, 0:1]
    lam = (jnp.exp(jnp.sum(lp[0:1] * lp[1:2], axis=-1, keepdims=True))
           - jnp.exp(jnp.sum(lp[2:3] * lp[3:4], axis=-1, keepdims=True)) + lam_init)
    lane = lax.broadcasted_iota(jnp.int32, (TQ, HEAD_DIM), 1)
    qs = []
    for h in range(N_HEADS):
        q = q_ref[h]
        qs.append((jnp.where(lane < DIFF_QK_DIM, q, jnp.zeros_like(q)),
                   jnp.where(lane >= DIFF_QK_DIM, q, jnp.zeros_like(q))))

    def block(j, carry):
        s_list, vt_list = [], []
        for h in range(N_HEADS):
            kj = _k_block(k_ref, h, j)
            bias = bt_ref[h, jnp.minimum(i - j, 2)]
            s_list += [_dot_nt(kj, qs[h][0]) + bias, _dot_nt(kj, qs[h][1]) + bias]
            vt_list += [_vt_block(vt_ref, h, j)] * 2
        return _softmax_block(s_list, vt_list, carry)

    carry = lax.fori_loop(0, i + 1, block, tuple(_softmax_init() for _ in range(2 * N_HEADS)))
    outs = []
    for h in range(N_HEADS):
        (_, l1, a1), (_, l2, a2) = carry[2 * h], carry[2 * h + 1]
        o = a1 / l1 - lam * (a2 / l2)
        o = o * lax.rsqrt(jnp.mean(o * o, axis=0, keepdims=True) + NORM_EPS) * g_ref[...]
        outs.append(o * (1.0 - lam_init))
    _store_heads(o_ref, outs)


def _diff_attention(zz, vt, b, s, bt, lamp, cst, g):
    nq = s // TQ
    return pl.pallas_call(
        _diff_kernel,
        grid=(b, nq),
        in_specs=_attn_specs(nq, G_DF_Q, G_DF_K, M_DF, s) + [
            _const_spec((N_HEADS, 3, TK, TQ)), _const_spec((4, DIFF_QK_DIM)), _const_spec((1, 128)),
            _const_spec((HEAD_DIM, 1))],
        out_specs=_out_spec(nq),
        out_shape=jax.ShapeDtypeStruct((b * s, N_HEADS * HEAD_DIM), BF16),
        compiler_params=_cparams(2),
        name="diff_attn",
    )(zz, zz, vt, bt, lamp, cst, g)


def _dsa_kernel(q_ref, k_ref, vt_ref, qia_ref, qib_ref, ki_ref, wt_ref, bt_ref, o_ref, sc_ref, *, topk, seq):
    i = pl.program_id(1)
    nb = i + 1
    kf = float(topk)
    w = wt_ref[...]
    key = lax.broadcasted_iota(jnp.int32, (TK, TQ), 0)
    qry = lax.broadcasted_iota(jnp.int32, (TK, TQ), 1)

    def score_block(j, _):
        kij = _k_block(ki_ref, 0, j)
        sc = jnp.zeros((TK, TQ), F32)
        for hh in range(IDX_HEADS):
            qi = (qia_ref if hh < 4 else qib_ref)[hh % 4]
            sc = sc + w[hh:hh + 1, :] * jnp.maximum(_dot_nt(kij, qi), 0.0)
        sc_ref[j] = sc
        return 0

    lax.fori_loop(0, nb, score_block, 0)
    sc_ref[i] = jnp.where(key <= qry, sc_ref[i], NEG)

    def reduce_blocks(fn, init):
        return lax.fori_loop(0, nb, lambda j, c: fn(sc_ref[j], j, c), init)

    def count_ge(t):
        return reduce_blocks(
            lambda x, j, c: c + jnp.sum(jnp.where(x >= t, 1.0, 0.0), axis=0, keepdims=True),
            jnp.zeros((1, TQ), F32))

    def minmax(x, j, c):
        return (jnp.minimum(c[0], jnp.min(jnp.where(x > HALF_NEG, x, BIG), axis=0, keepdims=True)),
                jnp.maximum(c[1], jnp.max(x, axis=0, keepdims=True)))

    n_valid = i * TQ + lax.broadcasted_iota(jnp.int32, (1, TQ), 1) + 1
    take_all = n_valid <= topk
    lo, hi = reduce_blocks(minmax, (jnp.full((1, TQ), BIG, F32), jnp.full((1, TQ), NEG, F32)))
    c_max = count_ge(hi)
    at_max = c_max >= kf
    state = (jnp.where(at_max, hi, lo), hi, jnp.where(at_max, c_max, n_valid.astype(F32)), c_max)

    def bisect(_, state):
        lo, hi, c_lo, c_hi = state
        mid = 0.5 * lo + 0.5 * hi
        c = count_ge(mid)
        ge = c >= kf
        return jnp.where(ge, mid, lo), jnp.where(ge, hi, mid), jnp.where(ge, c, c_lo), jnp.where(ge, c_hi, c)

    def unsettled(state):
        lo, hi, c_lo, _ = state
        open_q = jnp.where(take_all, 0.0, jnp.where(c_lo != kf, jnp.where(lo < hi, 1.0, 0.0), 0.0))

        def band_spread():
            def f(x, j, c):
                inb = (x >= lo) & (x < hi)
                return (jnp.minimum(c[0], jnp.min(jnp.where(inb, x, BIG), axis=0, keepdims=True)),
                        jnp.maximum(c[1], jnp.max(jnp.where(inb, x, -BIG), axis=0, keepdims=True)))
            b_min, b_max = reduce_blocks(f, (jnp.full((1, TQ), BIG, F32), jnp.full((1, TQ), -BIG, F32)))
            return (jnp.max(jnp.where(b_max != b_min, open_q, 0.0)) > 0.0).astype(jnp.int32)

        return lax.cond(jnp.max(open_q) > 0.0, band_spread, lambda: jnp.int32(0))

    state = lax.fori_loop(0, BISECT_WARMUP, bisect, state)

    def trip(c):
        n, state, _ = c
        state = lax.fori_loop(0, BISECT_TRIP, bisect, state)
        return n + 1, state, unsettled(state)

    _, state, _ = lax.while_loop(lambda c: (c[2] > 0) & (c[0] < BISECT_MAX_TRIPS), trip,
                                 (jnp.int32(0), state, unsettled(state)))
    lo, hi, c_lo, c_hi = state
    hi_ok = lo < hi
    c_above = jnp.where(hi_ok, c_hi, 0.0)
    hi_sel = jnp.where(hi_ok, hi, BIG)
    need = kf - c_above
    tied = jnp.max(jnp.where(take_all, 0.0, c_lo - kf)) > 0.0

    def index_bound():
        def count_band_le(mb):
            def f(x, j, c):
                idx = j * TK + key
                inb = jnp.where(x >= lo, jnp.where(x < hi_sel, jnp.where(idx <= mb, 1.0, 0.0), 0.0), 0.0)
                return c + jnp.sum(inb, axis=0, keepdims=True)
            return reduce_blocks(f, jnp.zeros((1, TQ), F32))

        def step(_, c):
            lo_i, hi_i = c
            mid = lax.shift_right_arithmetic(lo_i + hi_i, 1)
            ge = count_band_le(mid) >= need
            return jnp.where(ge, lo_i, mid), jnp.where(ge, mid, hi_i)

        n_steps = int(math.ceil(math.log2(seq))) + 1
        _, hi_i = lax.fori_loop(0, n_steps, step,
                                (jnp.full((1, TQ), -1, jnp.int32), jnp.full((1, TQ), seq - 1, jnp.int32)))
        return hi_i

    mb = lax.cond(tied, index_bound, lambda: jnp.full((1, TQ), seq, jnp.int32))
    lo_sel = jnp.where(take_all, HALF_NEG, lo)
    mb_sel = jnp.where(take_all, seq, mb)

    def write_mask(j, _):
        x = sc_ref[j]
        idx = j * TK + key
        sel = (x >= lo_sel) & ((x >= hi_sel) | (idx <= mb_sel))
        sc_ref[j] = jnp.where(sel, 0.0, NEG)
        return 0

    lax.fori_loop(0, nb, write_mask, 0)

    def block(j, carry):
        s_list = [_dot_nt(_k_block(k_ref, h, j), q_ref[h]) + bt_ref[h, jnp.minimum(i - j, 2)] + sc_ref[j]
                  for h in range(N_HEADS)]
        return _softmax_block(s_list, [_vt_block(vt_ref, h, j) for h in range(N_HEADS)], carry)

    carry = lax.fori_loop(0, nb, block, tuple(_softmax_init() for _ in range(N_HEADS)))
    _store_heads(o_ref, [acc / l for _, l, acc in carry])


def _dsa_attention(zz, vt, wt, b, s, bt):
    nq = s // TQ
    topk = min(DSA_TOPK_MAX, s // 4)
    return pl.pallas_call(
        functools.partial(_dsa_kernel, topk=topk, seq=s),
        grid=(b, nq),
        in_specs=_attn_specs(nq, G_DS_Q, G_DS_K, M_DS, s) + [
            pl.BlockSpec((4, TQ, HEAD_DIM), lambda b_, i: (G_QI_A, b_ * nq + i, 0)),
            pl.BlockSpec((4, TQ, HEAD_DIM), lambda b_, i: (G_QI_B, b_ * nq + i, 0)),
            pl.BlockSpec((1, s, HEAD_DIM), lambda b_, i: (S_KIDX, b_, 0)),
            pl.BlockSpec((IDX_HEADS, TQ), lambda b_, i: (0, b_ * nq + i)),
            _const_spec((N_HEADS, 3, TK, TQ))],
        out_specs=_out_spec(nq),
        out_shape=jax.ShapeDtypeStruct((b * s, N_HEADS * HEAD_DIM), BF16),
        scratch_shapes=[pltpu.VMEM((nq, TK, TQ), F32)],
        compiler_params=_cparams(2),
        name="dsa_attn",
    )(zz, zz, vt, zz, zz, zz, wt, bt)


def _moba_kernel(q_ref, k_ref, vt_ref, bt_ref, o_ref, km_ref, *, nblk, topb):
    i = pl.program_id(1)
    nrow = km_ref.shape[1]

    @pl.when(i == 0)
    def _():
        km_ref[...] = jnp.zeros_like(km_ref)
        for h in range(N_HEADS):
            for n in range(nblk):
                kb = k_ref[h, n * MOBA_BLOCK:(n + 1) * MOBA_BLOCK, :].astype(F32)
                km_ref[h, n:n + 1, :] = jnp.mean(kb, axis=0, keepdims=True)

    blk = lax.broadcasted_iota(jnp.int32, (nrow, TQ), 0)
    past = blk < i
    head_bits = []
    for h in range(N_HEADS):
        gate = _dot_nt(km_ref[h].astype(BF16), q_ref[h])
        bits = jnp.zeros((1, TQ), F32)
        for n in range(nblk):
            gn = gate[n:n + 1, :]
            beats = jnp.where(past, jnp.where(gate > gn, 1.0, jnp.where(gate == gn, jnp.where(blk < n, 1.0, 0.0), 0.0)), 0.0)
            rank = jnp.sum(beats, axis=0, keepdims=True)
            bits = bits + jnp.where(rank < float(topb), jnp.where(n < i, float(2 ** n), 0.0), 0.0)
        head_bits.append(bits.astype(jnp.int32))

    def values(n):
        return [_vt_block(vt_ref, h, n) for h in range(N_HEADS)]

    own = _softmax_block([_dot_nt(_k_block(k_ref, h, i), q_ref[h]) + bt_ref[h, 0] for h in range(N_HEADS)],
                         values(i), tuple(_softmax_init() for _ in range(N_HEADS)))

    def block(n, carry):
        s_list = []
        for h in range(N_HEADS):
            picked = (lax.shift_right_logical(head_bits[h], jnp.full_like(head_bits[h], n)) & 1) == 1
            s_list.append(_dot_nt(_k_block(k_ref, h, n), q_ref[h]) + bt_ref[h, jnp.minimum(i - n, 2)]
                          + jnp.where(picked, 0.0, NEG))
        return _softmax_block(s_list, values(n), carry)

    carry = lax.fori_loop(0, i, block, own)
    _store_heads(o_ref, [acc / l for _, l, acc in carry])


def _moba_attention(zz, vt, b, s, bt):
    nq = s // TQ
    nblk = s // MOBA_BLOCK
    topb = min(MOBA_TOPK, nblk - 1)
    return pl.pallas_call(
        functools.partial(_moba_kernel, nblk=nblk, topb=topb),
        grid=(b, nq),
        in_specs=_attn_specs(nq, G_MB_Q, G_MB_K, M_MB, s) + [_const_spec((N_HEADS, 3, TK, TQ))],
        out_specs=_out_spec(nq),
        out_shape=jax.ShapeDtypeStruct((b * s, N_HEADS * HEAD_DIM), BF16),
        scratch_shapes=[pltpu.VMEM((N_HEADS, max(8, nblk), HEAD_DIM), F32)],
        compiler_params=_cparams(2),
        name="moba_attn",
    )(zz, zz, vt, bt)


def _merge_kernel(x_ref, osb_ref, odf_ref, ods_ref, omb_ref, gpre_ref, wg_ref, wbr_ref, wout_ref, gpost_ref, o_ref):
    x = x_ref[...]
    h = _rms(x, gpre_ref[...]).astype(BF16)
    y = jnp.zeros((x.shape[0], D_MODEL), F32)
    for r, o_r in enumerate((osb_ref, odf_ref, ods_ref, omb_ref)):
        gate = jax.nn.sigmoid(_dot(h, wg_ref[:, r * D_MODEL:(r + 1) * D_MODEL]))
        y = y + gate * _dot(o_r[...], wbr_ref[r])
    o_ref[...] = x + _rms(_dot(y.astype(BF16), wout_ref[...]), gpost_ref[...])


def _merge(x, o_sb, o_df, o_ds, o_mb, g_pre, w_gate, w_br, w_out, g_post):
    t = x.shape[0]
    tm = TM_MERGE
    tok = lambda width: pl.BlockSpec((tm, width), lambda i: (i, 0))
    return pl.pallas_call(
        _merge_kernel,
        grid=(t // tm,),
        in_specs=[tok(D_MODEL), tok(256), tok(256), tok(256), tok(256),
                  _const_spec((1, D_MODEL)), _const_spec((D_MODEL, 4 * D_MODEL)),
                  _const_spec((4, 256, D_MODEL)), _const_spec((D_MODEL, D_MODEL)), _const_spec((1, D_MODEL))],
        out_specs=tok(D_MODEL),
        out_shape=jax.ShapeDtypeStruct((t, D_MODEL), F32),
        compiler_params=_cparams(1),
        name="merge",
    )(x, o_sb, o_df, o_ds, o_mb, g_pre, w_gate, w_br, w_out, g_post)


def _ffn_kernel(x_ref, gpre_ref, win_ref, wout_ref, gpost_ref, o_ref):
    x = x_ref[...]
    h = _rms(x, gpre_ref[...]).astype(BF16)
    gate = _dot(h, win_ref[:, 0:D_FF])
    up = _dot(h, win_ref[:, D_FF:2 * D_FF])
    act = (gate * jax.nn.sigmoid(gate) * up).astype(BF16)
    o_ref[...] = x + _rms(_dot(act, wout_ref[...]), gpost_ref[...])


def _ffn(x, g_pre, w_in, w_out, g_post):
    t = x.shape[0]
    tm = TM_FFN
    return pl.pallas_call(
        _ffn_kernel,
        grid=(t // tm,),
        in_specs=[pl.BlockSpec((tm, D_MODEL), lambda i: (i, 0)), _const_spec((1, D_MODEL)),
                  _const_spec((D_MODEL, 2 * D_FF)), _const_spec((D_FF, D_MODEL)), _const_spec((1, D_MODEL))],
        out_specs=pl.BlockSpec((tm, D_MODEL), lambda i: (i, 0)),
        out_shape=jax.ShapeDtypeStruct((t, D_MODEL), F32),
        compiler_params=_cparams(1),
        name="ffn",
    )(x, g_pre, w_in, w_out, g_post)


def _t5_bucket(dist):
    max_exact = N_BUCKETS // 2
    d = jnp.maximum(dist, 0)
    log_ratio = jnp.log(jnp.maximum(d, 1).astype(F32) / max_exact) / math.log(MAX_DISTANCE / max_exact)
    large = jnp.minimum(max_exact + (log_ratio * (N_BUCKETS - max_exact)).astype(jnp.int32), N_BUCKETS - 1)
    return jnp.where(d < max_exact, d, large)


def _bias_tiles(rel_bias):
    assert TQ == TK
    n = TK
    nh = rel_bias.shape[1]
    d = np.arange(-(n - 1), 3 * n)
    by_dist = rel_bias.astype(F32).T[:, _t5_bucket(jnp.asarray(np.maximum(d, 0), jnp.int32))]
    by_dist = jnp.where(jnp.asarray(d >= 0)[None, :], by_dist, NEG)
    tiles = []
    for o in range(3):
        w = by_dist[:, o * n:o * n + 2 * n - 1]
        w = jnp.concatenate([w, jnp.zeros((nh, 1), F32)], axis=1)
        flat = jnp.broadcast_to(w[:, None, :], (nh, n, 2 * n)).reshape(nh, 2 * n * n)
        tiles.append(flat[:, :n * (2 * n - 1)].reshape(nh, n, 2 * n - 1)[:, :, n - 1:])
    return jnp.stack(tiles, axis=1)


def _pack_weights(w_in):
    parts, start = [], 0
    for e in range(1, N_PACK + 1):
        if e == N_PACK or _PACK_SRC[e] != _PACK_SRC[e - 1] + (1 if _PACK_SRC[e - 1] >= 0 else 0):
            a = int(_PACK_SRC[start])
            n = e - start
            parts.append(w_in[:, :, a:a + n] if a >= 0 else jnp.zeros(w_in.shape[:2] + (n,), w_in.dtype))
            start = e
    return jnp.concatenate(parts, axis=2).astype(BF16)


def _transposed_weights(w_in):
    wv = jnp.stack([w_in[:, :, _OFF[n]:_OFF[n] + 256] for n in ("v_sb", "v_df", "v_ds", "v_mb")], axis=1)
    wi = w_in[:, :, _OFF["wi"]:_OFF["wi"] + IDX_HEADS]
    return jnp.swapaxes(wv, 2, 3).astype(BF16), jnp.swapaxes(wi, 1, 2).astype(BF16)


def kernel(x, w_in, w_br_sb, w_br_diff, w_br_dsa, w_br_moba, w_out, lambda_q1, lambda_k1, lambda_q2, lambda_k2,
           diff_subln_g, rel_bias, w_ffn_in, w_ffn_out, g_pre_mix, g_post_mix, g_pre_ffn, g_post_ffn):
    b, s, d = x.shape
    depth = w_in.shape[0]
    assert d == D_MODEL and s % TQ == 0 and s // MOBA_BLOCK >= 2
    t = b * s

    w_pack = _pack_weights(w_in)
    w_vt, w_it = _transposed_weights(w_in)
    w_gate = w_in[:, :, _OFF["gate"]:].astype(BF16)
    w_br = jnp.stack([w_br_sb, w_br_diff, w_br_dsa, w_br_moba], axis=1).astype(BF16)
    w_o = w_out.astype(BF16)
    w_f1 = w_ffn_in.astype(BF16)
    w_f2 = w_ffn_out.astype(BF16)
    cs = jnp.asarray(_PACK_SCALE)[None, :]

    bt = _bias_tiles(rel_bias)
    bt_df, bt_ds, bt_mb = bt[0:4], bt[4:8], bt[8:12]
    key = np.arange(TK)[:, None]
    qry = np.arange(TQ)[None, :]
    tri = jnp.asarray(key <= np.arange(TK)[None, :], BF16)
    sb_mask = jnp.asarray(np.where(key < qry, 0.0, NEG), F32)

    xf = x.reshape(t, d)
    for l in range(depth):
        lam_init = 0.8 - 0.6 * math.exp(-0.3 * l)
        lamp = jnp.stack([lambda_q1[l], lambda_k1[l], lambda_q2[l], lambda_k2[l]]).astype(F32)
        cst = jnp.full((1, 128), lam_init, F32)
        zz, vt, wt = _proj(xf, g_pre_mix[l][None, :], w_pack[l], cs, w_vt[l], w_it[l])
        o_sb = _sb_attention(zz, vt, b, s, tri, sb_mask)
        o_df = _diff_attention(zz, vt, b, s, bt_df, lamp, cst, diff_subln_g[l][:, None])
        o_ds = _dsa_attention(zz, vt, wt, b, s, bt_ds)
        o_mb = _moba_attention(zz, vt, b, s, bt_mb)
        xf = _merge(xf, o_sb, o_df, o_ds, o_mb, g_pre_mix[l][None, :], w_gate[l], w_br[l], w_o[l],
                    g_post_mix[l][None, :])
        xf = _ffn(xf, g_pre_ffn[l][None, :], w_f1[l], w_f2[l], g_post_ffn[l][None, :])
    return xf.reshape(b, s, d)
```

```python
import functools
import math

import numpy as np
import jax
import jax.numpy as jnp
from jax import lax
from jax.experimental import pallas as pl
from jax.experimental.pallas import tpu as pltpu

F32 = jnp.float32
BF16 = jnp.bfloat16

D_MODEL = 1024
HEAD_DIM = 64
N_HEADS = 4
N_MIXERS = 4
DIFF_QK_DIM = 32
IDX_HEADS = 8
DSA_TOPK_MAX = 256
MOBA_BLOCK = 256
MOBA_TOPK = 3
N_BUCKETS = 32
MAX_DISTANCE = 128
D_FF = 2816
NORM_EPS = 1e-6

TQ = 256
TK = 256
NEG = -1e30
HALF_NEG = -0.5e30
BIG = 3e38
BISECT_WARMUP = 12
BISECT_TRIP = 4
BISECT_MAX_TRIPS = 70
SB_TAIL_CUTOFF = 120.0
LOG2E = math.log2(math.e)
V_ROWS = 80
TM_MERGE = 256
TM_FFN = 256
N_GROUP = 11
N_SLAB = 4 * N_GROUP
N_PACK = N_SLAB * HEAD_DIM
VMEM_LIMIT = 56 * 1024 * 1024

G_SB_Q, G_SB_K, G_DF_Q, G_DF_K, G_DS_Q, G_DS_K, G_QI_A, G_QI_B, G_MB_Q, G_MB_K, G_KIDX = range(N_GROUP)
S_KIDX = 4 * G_KIDX
M_SB, M_DF, M_DS, M_MB = range(N_MIXERS)


def _layout():
    off = {}
    acc = 0
    for name, sz in (("q_sb", 256), ("k_sb", 256), ("v_sb", 256), ("q1", 128), ("q2", 128), ("k1", 128),
                     ("k2", 128), ("v_df", 256), ("q_ds", 256), ("k_ds", 256), ("v_ds", 256), ("qi", 512),
                     ("ki", 64), ("wi", 8), ("q_mb", 256), ("k_mb", 256), ("v_mb", 256), ("gate", 4096)):
        off[name] = acc
        acc += sz
    return off


_OFF = _layout()


def _pack_layout():
    off = _OFF
    cols, scale = [], []

    def add(start, n, s=1.0):
        cols.extend(range(start, start + n))
        scale.extend([s] * n)

    hd = HEAD_DIM ** -0.5
    hd2 = hd * LOG2E
    df2 = DIFF_QK_DIM ** -0.5 * LOG2E
    add(off["q_sb"], 256, hd); add(off["k_sb"], 256)
    for h in range(N_HEADS):
        add(off["q1"] + h * 32, 32, df2); add(off["q2"] + h * 32, 32, df2)
    for h in range(N_HEADS):
        add(off["k1"] + h * 32, 32); add(off["k2"] + h * 32, 32)
    add(off["q_ds"], 256, hd2); add(off["k_ds"], 256)
    add(off["qi"], 512, HEAD_DIM ** -0.5)
    add(off["q_mb"], 256, hd2); add(off["k_mb"], 256)
    add(off["ki"], 64)
    cols.extend([-1] * 192); scale.extend([1.0] * 192)
    assert len(cols) == N_PACK
    return np.asarray(cols, np.int32), np.asarray(scale, np.float32)


_PACK_SRC, _PACK_SCALE = _pack_layout()


def _dot(a, b):
    return jnp.dot(a, b, preferred_element_type=F32)


def _dot_nt(a, b):
    return lax.dot_general(a, b, (((1,), (1,)), ((), ())), preferred_element_type=F32)


def _rms(x, g):
    return x * lax.rsqrt(jnp.mean(x * x, axis=-1, keepdims=True) + NORM_EPS) * g


def _cparams(n_axes):
    return pltpu.CompilerParams(dimension_semantics=("arbitrary",) * n_axes, vmem_limit_bytes=VMEM_LIMIT)


def _const_spec(shape):
    nd = len(shape)
    return pl.BlockSpec(shape, lambda *_: (0,) * nd, pipeline_mode=pl.Buffered(1))


def _proj_kernel(x_ref, g_ref, w_ref, cs_ref, wvt_ref, wit_ref, zz_ref, vt_ref, wt_ref):
    h = _rms(x_ref[...], g_ref[...]).astype(BF16)
    for c in range(N_GROUP):
        r = _dot(h, w_ref[:, c * 256:(c + 1) * 256]) * cs_ref[:, c * 256:(c + 1) * 256]
        for s in range(4):
            zz_ref[4 * c + s] = r[:, s * HEAD_DIM:(s + 1) * HEAD_DIM].astype(BF16)
    for m in range(N_MIXERS):
        vt = _dot_nt(wvt_ref[m], h).astype(BF16)
        for hh in range(N_HEADS):
            vt_ref[m, 0, hh * V_ROWS:hh * V_ROWS + HEAD_DIM, :] = vt[hh * HEAD_DIM:(hh + 1) * HEAD_DIM]
            vt_ref[m, 0, hh * V_ROWS + HEAD_DIM:(hh + 1) * V_ROWS, :] = jnp.ones((V_ROWS - HEAD_DIM, TQ), BF16)
    wt_ref[...] = _dot_nt(wit_ref[...], h) * IDX_HEADS ** -0.5


def _proj(x, g, w, cs, wvt, wit):
    t = x.shape[0]
    tm = TQ
    return pl.pallas_call(
        _proj_kernel,
        grid=(t // tm,),
        in_specs=[pl.BlockSpec((tm, D_MODEL), lambda i: (i, 0)),
                  _const_spec((1, D_MODEL)),
                  _const_spec((D_MODEL, N_PACK)),
                  _const_spec((1, N_PACK)),
                  _const_spec((N_MIXERS, 256, D_MODEL)),
                  _const_spec((IDX_HEADS, D_MODEL))],
        out_specs=[pl.BlockSpec((N_SLAB, tm, HEAD_DIM), lambda i: (0, i, 0)),
                   pl.BlockSpec((N_MIXERS, 1, N_HEADS * V_ROWS, tm), lambda i: (0, i, 0, 0)),
                   pl.BlockSpec((IDX_HEADS, tm), lambda i: (0, i))],
        out_shape=[jax.ShapeDtypeStruct((N_SLAB, t, HEAD_DIM), BF16),
                   jax.ShapeDtypeStruct((N_MIXERS, t // tm, N_HEADS * V_ROWS, tm), BF16),
                   jax.ShapeDtypeStruct((IDX_HEADS, t), F32)],
        compiler_params=_cparams(1),
        name="proj",
    )(x, g, w, cs, wvt, wit)


def _k_block(ref, h, j):
    return ref[h, pl.ds(pl.multiple_of(j * TK, TK), TK), :]


def _vt_block(ref, h, j, rows=V_ROWS):
    return ref[0, j, h * V_ROWS:h * V_ROWS + rows, :]


def _softmax_block(s_list, vt_list, carry):
    ms = [jnp.maximum(c[0], jnp.max(s, axis=0, keepdims=True)) for s, c in zip(s_list, carry)]
    pvs = [_dot(vt, jnp.exp2(s - m).astype(BF16)) for vt, s, m in zip(vt_list, s_list, ms)]
    return tuple((m_new, jnp.exp2(m - m_new) * acc + pv) for (m, acc), m_new, pv in zip(carry, ms, pvs))


def _softmax_init():
    return (jnp.full((1, TQ), NEG, F32), jnp.zeros((V_ROWS, TQ), F32))


def _softmax_out(carry):
    _, acc = carry
    return acc[:HEAD_DIM] / acc[HEAD_DIM:HEAD_DIM + 1]


def _store_heads(o_ref, heads_t):
    o_ref[...] = jnp.transpose(jnp.concatenate(heads_t, axis=0)).astype(BF16)


def _attn_specs(nq, gq, gk, mixer, s):
    return [pl.BlockSpec((4, TQ, HEAD_DIM), lambda b, i: (gq, b * nq + i, 0)),
            pl.BlockSpec((4, s, HEAD_DIM), lambda b, i: (gk, b, 0)),
            pl.BlockSpec((1, s // TK, N_HEADS * V_ROWS, TK), lambda b, i: (mixer, b, 0, 0))]


def _out_spec(nq):
    return pl.BlockSpec((TQ, N_HEADS * HEAD_DIM), lambda b, i: (b * nq + i, 0))


def _sb_kernel(q_ref, k_ref, vt_ref, tri_ref, mask_ref, o_ref):
    i = pl.program_id(1)
    tri = tri_ref[...]

    def block(j, carry, masked):
        heads = range(N_HEADS)
        zs = [_dot_nt(_k_block(k_ref, h, j), q_ref[h]) for h in heads]
        if masked:
            zs = [z + mask_ref[...] for z in zs]
        sps = [jnp.maximum(z, 0.0) + jnp.log(1.0 + jnp.exp(-jnp.abs(z))) for z in zs]
        his = [sp.astype(BF16) for sp in sps]
        los = [(sp - hi.astype(F32)).astype(BF16) for sp, hi in zip(sps, his)]
        cs = [_dot(tri, hi) + _dot(tri, lo) + c[0] for hi, lo, c in zip(his, los, carry)]
        avs = [_dot(_vt_block(vt_ref, h, j, HEAD_DIM), jnp.exp(z - c).astype(BF16))
               for h, z, c in zip(heads, zs, cs)]
        return tuple((c[0:1, :], old[1] + av) for c, old, av in zip(cs, carry, avs))

    init = tuple((jnp.zeros((1, TQ), F32), jnp.zeros((HEAD_DIM, TQ), F32)) for _ in range(N_HEADS))
    carry = block(i, init, True)

    def weights_left(carry):
        tail = functools.reduce(jnp.minimum, [c[0] for c in carry])
        return (jnp.min(tail) < SB_TAIL_CUTOFF).astype(jnp.int32)

    def earlier_block(c):
        j, carry, _ = c
        carry = block(j, carry, False)
        return j - 1, carry, weights_left(carry)

    _, carry, _ = lax.while_loop(lambda c: (c[0] >= 0) & (c[2] > 0), earlier_block, (i - 1, carry, weights_left(carry)))
    _store_heads(o_ref, [c[1] for c in carry])


def _sb_attention(zz, vt, b, s, tri, mask):
    nq = s // TQ
    return pl.pallas_call(
        _sb_kernel,
        grid=(b, nq),
        in_specs=_attn_specs(nq, G_SB_Q, G_SB_K, M_SB, s) + [_const_spec((TK, TK)), _const_spec((TK, TQ))],
        out_specs=_out_spec(nq),
        out_shape=jax.ShapeDtypeStruct((b * s, N_HEADS * HEAD_DIM), BF16),
        compiler_params=_cparams(2),
        name="sb_attn",
    )(zz, zz, vt, tri, mask)


def _diff_kernel(q_ref, k_ref, vt_ref, bt_ref, lam_ref, cst_ref, g_ref, o_ref):
    i = pl.program_id(1)
    lp = lam_ref[...]
    lam_init = cst_ref[:, 0:1]
    lam = (jnp.exp(jnp.sum(lp[0:1] * lp[1:2], axis=-1, keepdims=True))
           - jnp.exp(jnp.sum(lp[2:3] * lp[3:4], axis=-1, keepdims=True)) + lam_init)
    lane = lax.broadcasted_iota(jnp.int32, (TQ, HEAD_DIM), 1)
    qs = []
    for h in range(N_HEADS):
        q = q_ref[h]
        qs.append((jnp.where(lane < DIFF_QK_DIM, q, jnp.zeros_like(q)),
                   jnp.where(lane >= DIFF_QK_DIM, q, jnp.zeros_like(q))))

    def block(j, carry):
        s_list, vt_list = [], []
        for h in range(N_HEADS):
            kj = _k_block(k_ref, h, j)
            bias = bt_ref[h, jnp.minimum(i - j, 2)]
            s_list += [_dot_nt(kj, qs[h][0]) + bias, _dot_nt(kj, qs[h][1]) + bias]
            vt_list += [_vt_block(vt_ref, h, j)] * 2
        return _softmax_block(s_list, vt_list, carry)

    carry = lax.fori_loop(0, i + 1, block, tuple(_softmax_init() for _ in range(2 * N_HEADS)))
    outs = []
    for h in range(N_HEADS):
        o = _softmax_out(carry[2 * h]) - lam * _softmax_out(carry[2 * h + 1])
        o = o * lax.rsqrt(jnp.mean(o * o, axis=0, keepdims=True) + NORM_EPS) * g_ref[...]
        outs.append(o * (1.0 - lam_init))
    _store_heads(o_ref, outs)


def _diff_attention(zz, vt, b, s, bt, lamp, cst, g):
    nq = s // TQ
    return pl.pallas_call(
        _diff_kernel,
        grid=(b, nq),
        in_specs=_attn_specs(nq, G_DF_Q, G_DF_K, M_DF, s) + [
            _const_spec((N_HEADS, 3, TK, TQ)), _const_spec((4, DIFF_QK_DIM)), _const_spec((1, 128)),
            _const_spec((HEAD_DIM, 1))],
        out_specs=_out_spec(nq),
        out_shape=jax.ShapeDtypeStruct((b * s, N_HEADS * HEAD_DIM), BF16),
        compiler_params=_cparams(2),
        name="diff_attn",
    )(zz, zz, vt, bt, lamp, cst, g)


def _dsa_kernel(q_ref, k_ref, vt_ref, qia_ref, qib_ref, ki_ref, wt_ref, bt_ref, o_ref, sc_ref, *, topk, seq):
    i = pl.program_id(1)
    nb = i + 1
    kf = float(topk)
    w = wt_ref[...]
    key = lax.broadcasted_iota(jnp.int32, (TK, TQ), 0)
    qry = lax.broadcasted_iota(jnp.int32, (TK, TQ), 1)

    def score_block(j, _):
        kij = _k_block(ki_ref, 0, j)
        sc = jnp.zeros((TK, TQ), F32)
        for hh in range(IDX_HEADS):
            qi = (qia_ref if hh < 4 else qib_ref)[hh % 4]
            sc = sc + w[hh:hh + 1, :] * jnp.maximum(_dot_nt(kij, qi), 0.0)
        sc_ref[j] = sc
        return 0

    lax.fori_loop(0, nb, score_block, 0)
    sc_ref[i] = jnp.where(key <= qry, sc_ref[i], NEG)

    def reduce_blocks(fn, init):
        return lax.fori_loop(0, nb, lambda j, c: fn(sc_ref[j], j, c), init)

    def count_ge(t):
        return reduce_blocks(
            lambda x, j, c: c + jnp.sum(jnp.where(x >= t, 1.0, 0.0), axis=0, keepdims=True),
            jnp.zeros((1, TQ), F32))

    def minmax(x, j, c):
        return (jnp.minimum(c[0], jnp.min(jnp.where(x > HALF_NEG, x, BIG), axis=0, keepdims=True)),
                jnp.maximum(c[1], jnp.max(x, axis=0, keepdims=True)))

    n_valid = i * TQ + lax.broadcasted_iota(jnp.int32, (1, TQ), 1) + 1
    take_all = n_valid <= topk
    lo, hi = reduce_blocks(minmax, (jnp.full((1, TQ), BIG, F32), jnp.full((1, TQ), NEG, F32)))
    c_max = count_ge(hi)
    at_max = c_max >= kf
    state = (jnp.where(at_max, hi, lo), hi, jnp.where(at_max, c_max, n_valid.astype(F32)), c_max)

    def bisect(_, state):
        lo, hi, c_lo, c_hi = state
        mid = 0.5 * lo + 0.5 * hi
        c = count_ge(mid)
        ge = c >= kf
        return jnp.where(ge, mid, lo), jnp.where(ge, hi, mid), jnp.where(ge, c, c_lo), jnp.where(ge, c_hi, c)

    def unsettled(state):
        lo, hi, c_lo, _ = state
        open_q = jnp.where(take_all, 0.0, jnp.where(c_lo != kf, jnp.where(lo < hi, 1.0, 0.0), 0.0))

        def band_spread():
            def f(x, j, c):
                inb = (x >= lo) & (x < hi)
                return (jnp.minimum(c[0], jnp.min(jnp.where(inb, x, BIG), axis=0, keepdims=True)),
                        jnp.maximum(c[1], jnp.max(jnp.where(inb, x, -BIG), axis=0, keepdims=True)))
            b_min, b_max = reduce_blocks(f, (jnp.full((1, TQ), BIG, F32), jnp.full((1, TQ), -BIG, F32)))
            return (jnp.max(jnp.where(b_max != b_min, open_q, 0.0)) > 0.0).astype(jnp.int32)

        return lax.cond(jnp.max(open_q) > 0.0, band_spread, lambda: jnp.int32(0))

    state = lax.fori_loop(0, BISECT_WARMUP, bisect, state)

    def trip(c):
        n, state, _ = c
        state = lax.fori_loop(0, BISECT_TRIP, bisect, state)
        return n + 1, state, unsettled(state)

    _, state, _ = lax.while_loop(lambda c: (c[2] > 0) & (c[0] < BISECT_MAX_TRIPS), trip,
                                 (jnp.int32(0), state, unsettled(state)))
    lo, hi, c_lo, c_hi = state
    hi_ok = lo < hi
    c_above = jnp.where(hi_ok, c_hi, 0.0)
    hi_sel = jnp.where(hi_ok, hi, BIG)
    need = kf - c_above
    tied = jnp.max(jnp.where(take_all, 0.0, c_lo - kf)) > 0.0

    def index_bound():
        def count_band_le(mb):
            def f(x, j, c):
                idx = j * TK + key
                inb = jnp.where(x >= lo, jnp.where(x < hi_sel, jnp.where(idx <= mb, 1.0, 0.0), 0.0), 0.0)
                return c + jnp.sum(inb, axis=0, keepdims=True)
            return reduce_blocks(f, jnp.zeros((1, TQ), F32))

        def step(_, c):
            lo_i, hi_i = c
            mid = lax.shift_right_arithmetic(lo_i + hi_i, 1)
            ge = count_band_le(mid) >= need
            return jnp.where(ge, lo_i, mid), jnp.where(ge, mid, hi_i)

        n_steps = int(math.ceil(math.log2(seq))) + 1
        _, hi_i = lax.fori_loop(0, n_steps, step,
                                (jnp.full((1, TQ), -1, jnp.int32), jnp.full((1, TQ), seq - 1, jnp.int32)))
        return hi_i

    mb = lax.cond(tied, index_bound, lambda: jnp.full((1, TQ), seq, jnp.int32))
    lo_sel = jnp.where(take_all, HALF_NEG, lo)
    mb_sel = jnp.where(take_all, seq, mb)

    def write_mask(j, _):
        x = sc_ref[j]
        idx = j * TK + key
        sel = (x >= lo_sel) & ((x >= hi_sel) | (idx <= mb_sel))
        sc_ref[j] = jnp.where(sel, 0.0, NEG)
        return 0

    lax.fori_loop(0, nb, write_mask, 0)

    def block(j, carry):
        s_list = [_dot_nt(_k_block(k_ref, h, j), q_ref[h]) + bt_ref[h, jnp.minimum(i - j, 2)] + sc_ref[j]
                  for h in range(N_HEADS)]
        return _softmax_block(s_list, [_vt_block(vt_ref, h, j) for h in range(N_HEADS)], carry)

    carry = lax.fori_loop(0, nb, block, tuple(_softmax_init() for _ in range(N_HEADS)))
    _store_heads(o_ref, [_softmax_out(c) for c in carry])


def _dsa_attention(zz, vt, wt, b, s, bt):
    nq = s // TQ
    topk = min(DSA_TOPK_MAX, s // 4)
    return pl.pallas_call(
        functools.partial(_dsa_kernel, topk=topk, seq=s),
        grid=(b, nq),
        in_specs=_attn_specs(nq, G_DS_Q, G_DS_K, M_DS, s) + [
            pl.BlockSpec((4, TQ, HEAD_DIM), lambda b_, i: (G_QI_A, b_ * nq + i, 0)),
            pl.BlockSpec((4, TQ, HEAD_DIM), lambda b_, i: (G_QI_B, b_ * nq + i, 0)),
            pl.BlockSpec((1, s, HEAD_DIM), lambda b_, i: (S_KIDX, b_, 0)),
            pl.BlockSpec((IDX_HEADS, TQ), lambda b_, i: (0, b_ * nq + i)),
            _const_spec((N_HEADS, 3, TK, TQ))],
        out_specs=_out_spec(nq),
        out_shape=jax.ShapeDtypeStruct((b * s, N_HEADS * HEAD_DIM), BF16),
        scratch_shapes=[pltpu.VMEM((nq, TK, TQ), F32)],
        compiler_params=_cparams(2),
        name="dsa_attn",
    )(zz, zz, vt, zz, zz, zz, wt, bt)


def _moba_kernel(q_ref, k_ref, vt_ref, bt_ref, o_ref, km_ref, *, nblk, topb):
    i = pl.program_id(1)
    nrow = km_ref.shape[1]

    @pl.when(i == 0)
    def _():
        km_ref[...] = jnp.zeros_like(km_ref)
        for h in range(N_HEADS):
            for n in range(nblk):
                kb = k_ref[h, n * MOBA_BLOCK:(n + 1) * MOBA_BLOCK, :].astype(F32)
                km_ref[h, n:n + 1, :] = jnp.mean(kb, axis=0, keepdims=True)

    blk = lax.broadcasted_iota(jnp.int32, (nrow, TQ), 0)
    past = blk < i
    head_bits = []
    for h in range(N_HEADS):
        gate = _dot_nt(km_ref[h].astype(BF16), q_ref[h])
        bits = jnp.zeros((1, TQ), F32)
        for n in range(nblk):
            gn = gate[n:n + 1, :]
            beats = jnp.where(past, jnp.where(gate > gn, 1.0, jnp.where(gate == gn, jnp.where(blk < n, 1.0, 0.0), 0.0)), 0.0)
            rank = jnp.sum(beats, axis=0, keepdims=True)
            bits = bits + jnp.where(rank < float(topb), jnp.where(n < i, float(2 ** n), 0.0), 0.0)
        head_bits.append(bits.astype(jnp.int32))

    def values(n):
        return [_vt_block(vt_ref, h, n) for h in range(N_HEADS)]

    own = _softmax_block([_dot_nt(_k_block(k_ref, h, i), q_ref[h]) + bt_ref[h, 0] for h in range(N_HEADS)],
                         values(i), tuple(_softmax_init() for _ in range(N_HEADS)))

    def block(n, carry):
        s_list = []
        for h in range(N_HEADS):
            picked = (lax.shift_right_logical(head_bits[h], jnp.full_like(head_bits[h], n)) & 1) == 1
            s_list.append(_dot_nt(_k_block(k_ref, h, n), q_ref[h]) + bt_ref[h, jnp.minimum(i - n, 2)]
                          + jnp.where(picked, 0.0, NEG))
        return _softmax_block(s_list, values(n), carry)

    carry = lax.fori_loop(0, i, block, own)
    _store_heads(o_ref, [_softmax_out(c) for c in carry])


def _moba_attention(zz, vt, b, s, bt):
    nq = s // TQ
    nblk = s // MOBA_BLOCK
    topb = min(MOBA_TOPK, nblk - 1)
    return pl.pallas_call(
        functools.partial(_moba_kernel, nblk=nblk, topb=topb),
        grid=(b, nq),
        in_specs=_attn_specs(nq, G_MB_Q, G_MB_K, M_MB, s) + [_const_spec((N_HEADS, 3, TK, TQ))],
        out_specs=_out_spec(nq),
        out_shape=jax.ShapeDtypeStruct((b * s, N_HEADS * HEAD_DIM), BF16),
        scratch_shapes=[pltpu.VMEM((N_HEADS, max(8, nblk), HEAD_DIM), F32)],
        compiler_params=_cparams(2),
        name="moba_attn",
    )(zz, zz, vt, bt)


def _merge_kernel(x_ref, osb_ref, odf_ref, ods_ref, omb_ref, gpre_ref, wg_ref, wbr_ref, wout_ref, gpost_ref, o_ref):
    x = x_ref[...]
    h = _rms(x, gpre_ref[...]).astype(BF16)
    y = jnp.zeros((x.shape[0], D_MODEL), F32)
    for r, o_r in enumerate((osb_ref, odf_ref, ods_ref, omb_ref)):
        gate = jax.nn.sigmoid(_dot(h, wg_ref[:, r * D_MODEL:(r + 1) * D_MODEL]))
        y = y + gate * _dot(o_r[...], wbr_ref[r])
    o_ref[...] = x + _rms(_dot(y.astype(BF16), wout_ref[...]), gpost_ref[...])


def _merge(x, o_sb, o_df, o_ds, o_mb, g_pre, w_gate, w_br, w_out, g_post):
    t = x.shape[0]
    tm = TM_MERGE
    tok = lambda width: pl.BlockSpec((tm, width), lambda i: (i, 0))
    return pl.pallas_call(
        _merge_kernel,
        grid=(t // tm,),
        in_specs=[tok(D_MODEL), tok(256), tok(256), tok(256), tok(256),
                  _const_spec((1, D_MODEL)), _const_spec((D_MODEL, 4 * D_MODEL)),
                  _const_spec((4, 256, D_MODEL)), _const_spec((D_MODEL, D_MODEL)), _const_spec((1, D_MODEL))],
        out_specs=tok(D_MODEL),
        out_shape=jax.ShapeDtypeStruct((t, D_MODEL), F32),
        compiler_params=_cparams(1),
        name="merge",
    )(x, o_sb, o_df, o_ds, o_mb, g_pre, w_gate, w_br, w_out, g_post)


def _ffn_kernel(x_ref, gpre_ref, win_ref, wout_ref, gpost_ref, o_ref):
    x = x_ref[...]
    h = _rms(x, gpre_ref[...]).astype(BF16)
    gate = _dot(h, win_ref[:, 0:D_FF])
    up = _dot(h, win_ref[:, D_FF:2 * D_FF])
    act = (gate * jax.nn.sigmoid(gate) * up).astype(BF16)
    o_ref[...] = x + _rms(_dot(act, wout_ref[...]), gpost_ref[...])


def _ffn(x, g_pre, w_in, w_out, g_post):
    t = x.shape[0]
    tm = TM_FFN
    return pl.pallas_call(
        _ffn_kernel,
        grid=(t // tm,),
        in_specs=[pl.BlockSpec((tm, D_MODEL), lambda i: (i, 0)), _const_spec((1, D_MODEL)),
                  _const_spec((D_MODEL, 2 * D_FF)), _const_spec((D_FF, D_MODEL)), _const_spec((1, D_MODEL))],
        out_specs=pl.BlockSpec((tm, D_MODEL), lambda i: (i, 0)),
        out_shape=jax.ShapeDtypeStruct((t, D_MODEL), F32),
        compiler_params=_cparams(1),
        name="ffn",
    )(x, g_pre, w_in, w_out, g_post)


def _t5_bucket(dist):
    max_exact = N_BUCKETS // 2
    d = jnp.maximum(dist, 0)
    log_ratio = jnp.log(jnp.maximum(d, 1).astype(F32) / max_exact) / math.log(MAX_DISTANCE / max_exact)
    large = jnp.minimum(max_exact + (log_ratio * (N_BUCKETS - max_exact)).astype(jnp.int32), N_BUCKETS - 1)
    return jnp.where(d < max_exact, d, large)


def _bias_tiles(rel_bias):
    assert TQ == TK
    n = TK
    nh = rel_bias.shape[1]
    d = np.arange(-(n - 1), 3 * n)
    by_dist = rel_bias.astype(F32).T[:, _t5_bucket(jnp.asarray(np.maximum(d, 0), jnp.int32))]
    by_dist = jnp.where(jnp.asarray(d >= 0)[None, :], by_dist * LOG2E, NEG)
    tiles = []
    for o in range(3):
        w = by_dist[:, o * n:o * n + 2 * n - 1]
        w = jnp.concatenate([w, jnp.zeros((nh, 1), F32)], axis=1)
        flat = jnp.broadcast_to(w[:, None, :], (nh, n, 2 * n)).reshape(nh, 2 * n * n)
        tiles.append(flat[:, :n * (2 * n - 1)].reshape(nh, n, 2 * n - 1)[:, :, n - 1:])
    return jnp.stack(tiles, axis=1)


def _pack_weights(w_in):
    parts, start = [], 0
    for e in range(1, N_PACK + 1):
        if e == N_PACK or _PACK_SRC[e] != _PACK_SRC[e - 1] + (1 if _PACK_SRC[e - 1] >= 0 else 0):
            a = int(_PACK_SRC[start])
            n = e - start
            parts.append(w_in[:, :, a:a + n] if a >= 0 else jnp.zeros(w_in.shape[:2] + (n,), w_in.dtype))
            start = e
    return jnp.concatenate(parts, axis=2).astype(BF16)


def _transposed_weights(w_in):
    wv = jnp.stack([w_in[:, :, _OFF[n]:_OFF[n] + 256] for n in ("v_sb", "v_df", "v_ds", "v_mb")], axis=1)
    wi = w_in[:, :, _OFF["wi"]:_OFF["wi"] + IDX_HEADS]
    return jnp.swapaxes(wv, 2, 3).astype(BF16), jnp.swapaxes(wi, 1, 2).astype(BF16)


def kernel(x, w_in, w_br_sb, w_br_diff, w_br_dsa, w_br_moba, w_out, lambda_q1, lambda_k1, lambda_q2, lambda_k2,
           diff_subln_g, rel_bias, w_ffn_in, w_ffn_out, g_pre_mix, g_post_mix, g_pre_ffn, g_post_ffn):
    b, s, d = x.shape
    depth = w_in.shape[0]
    assert d == D_MODEL and s % TQ == 0 and s // MOBA_BLOCK >= 2
    t = b * s

    w_pack = _pack_weights(w_in)
    w_vt, w_it = _transposed_weights(w_in)
    w_gate = w_in[:, :, _OFF["gate"]:].astype(BF16)
    w_br = jnp.stack([w_br_sb, w_br_diff, w_br_dsa, w_br_moba], axis=1).astype(BF16)
    w_o = w_out.astype(BF16)
    w_f1 = w_ffn_in.astype(BF16)
    w_f2 = w_ffn_out.astype(BF16)
    cs = jnp.asarray(_PACK_SCALE)[None, :]

    bt = _bias_tiles(rel_bias)
    bt_df, bt_ds, bt_mb = bt[0:4], bt[4:8], bt[8:12]
    key = np.arange(TK)[:, None]
    qry = np.arange(TQ)[None, :]
    tri = jnp.asarray(key <= np.arange(TK)[None, :], BF16)
    sb_mask = jnp.asarray(np.where(key < qry, 0.0, NEG), F32)

    xf = x.reshape(t, d)
    for l in range(depth):
        lam_init = 0.8 - 0.6 * math.exp(-0.3 * l)
        lamp = jnp.stack([lambda_q1[l], lambda_k1[l], lambda_q2[l], lambda_k2[l]]).astype(F32)
        cst = jnp.full((1, 128), lam_init, F32)
        zz, vt, wt = _proj(xf, g_pre_mix[l][None, :], w_pack[l], cs, w_vt[l], w_it[l])
        o_sb = _sb_attention(zz, vt, b, s, tri, sb_mask)
        o_df = _diff_attention(zz, vt, b, s, bt_df, lamp, cst, diff_subln_g[l][:, None])
        o_ds = _dsa_attention(zz, vt, wt, b, s, bt_ds)
        o_mb = _moba_attention(zz, vt, b, s, bt_mb)
        xf = _merge(xf, o_sb, o_df, o_ds, o_mb, g_pre_mix[l][None, :], w_gate[l], w_br[l], w_o[l],
                    g_post_mix[l][None, :])
        xf = _ffn(xf, g_pre_ffn[l][None, :], w_f1[l], w_f2[l], g_post_ffn[l][None, :])
    return xf.reshape(b, s, d)
```

```python
import functools
import math

import numpy as np
import jax
import jax.numpy as jnp
from jax import lax
from jax.experimental import pallas as pl
from jax.experimental.pallas import tpu as pltpu

F32 = jnp.float32
BF16 = jnp.bfloat16

D_MODEL = 1024
HEAD_DIM = 64
N_HEADS = 4
N_MIXERS = 4
DIFF_QK_DIM = 32
IDX_HEADS = 8
DSA_TOPK_MAX = 256
MOBA_BLOCK = 256
MOBA_TOPK = 3
N_BUCKETS = 32
MAX_DISTANCE = 128
D_FF = 2816
NORM_EPS = 1e-6

TQ = 256
TK = 256
NEG = -1e30
HALF_NEG = -0.5e30
BIG = 3e38
BISECT_WARMUP = 16
BISECT_TRIP = 4
BISECT_MAX_TRIPS = 70
SB_TAIL_CUTOFF = 120.0
LOG2E = math.log2(math.e)
V_ROWS = 80
TM_MERGE = 256
TM_FFN = 256
N_GROUP = 11
N_SLAB = 4 * N_GROUP
N_PACK = N_SLAB * HEAD_DIM
VMEM_LIMIT = 56 * 1024 * 1024

G_SB_Q, G_SB_K, G_DF_Q, G_DF_K, G_DS_Q, G_DS_K, G_QI_A, G_QI_B, G_MB_Q, G_MB_K, G_KIDX = range(N_GROUP)
S_KIDX = 4 * G_KIDX
M_SB, M_DF, M_DS, M_MB = range(N_MIXERS)


def _layout():
    off = {}
    acc = 0
    for name, sz in (("q_sb", 256), ("k_sb", 256), ("v_sb", 256), ("q1", 128), ("q2", 128), ("k1", 128),
                     ("k2", 128), ("v_df", 256), ("q_ds", 256), ("k_ds", 256), ("v_ds", 256), ("qi", 512),
                     ("ki", 64), ("wi", 8), ("q_mb", 256), ("k_mb", 256), ("v_mb", 256), ("gate", 4096)):
        off[name] = acc
        acc += sz
    return off


_OFF = _layout()


def _pack_layout():
    off = _OFF
    cols, scale = [], []

    def add(start, n, s=1.0):
        cols.extend(range(start, start + n))
        scale.extend([s] * n)

    hd = HEAD_DIM ** -0.5
    hd2 = hd * LOG2E
    df2 = DIFF_QK_DIM ** -0.5 * LOG2E
    add(off["q_sb"], 256, hd); add(off["k_sb"], 256)
    for h in range(N_HEADS):
        add(off["q1"] + h * 32, 32, df2); add(off["q2"] + h * 32, 32, df2)
    for h in range(N_HEADS):
        add(off["k1"] + h * 32, 32); add(off["k2"] + h * 32, 32)
    add(off["q_ds"], 256, hd2); add(off["k_ds"], 256)
    add(off["qi"], 512, HEAD_DIM ** -0.5)
    add(off["q_mb"], 256, hd2); add(off["k_mb"], 256)
    add(off["ki"], 64)
    cols.extend([-1] * 192); scale.extend([1.0] * 192)
    assert len(cols) == N_PACK
    return np.asarray(cols, np.int32), np.asarray(scale, np.float32)


_PACK_SRC, _PACK_SCALE = _pack_layout()


def _dot(a, b):
    return jnp.dot(a, b, preferred_element_type=F32)


def _dot_nt(a, b):
    return lax.dot_general(a, b, (((1,), (1,)), ((), ())), preferred_element_type=F32)


def _rms(x, g):
    return x * lax.rsqrt(jnp.mean(x * x, axis=-1, keepdims=True) + NORM_EPS) * g


def _cparams(n_axes):
    return pltpu.CompilerParams(dimension_semantics=("arbitrary",) * n_axes, vmem_limit_bytes=VMEM_LIMIT)


def _const_spec(shape):
    nd = len(shape)
    return pl.BlockSpec(shape, lambda *_: (0,) * nd, pipeline_mode=pl.Buffered(1))


def _proj_kernel(x_ref, g_ref, w_ref, cs_ref, wvt_ref, wit_ref, zz_ref, vt_ref, wt_ref):
    h = _rms(x_ref[...], g_ref[...]).astype(BF16)
    for c in range(N_GROUP):
        r = _dot(h, w_ref[:, c * 256:(c + 1) * 256]) * cs_ref[:, c * 256:(c + 1) * 256]
        for s in range(4):
            zz_ref[4 * c + s] = r[:, s * HEAD_DIM:(s + 1) * HEAD_DIM].astype(BF16)
    for m in range(N_MIXERS):
        vt = _dot_nt(wvt_ref[m], h).astype(BF16)
        for hh in range(N_HEADS):
            vt_ref[m, 0, hh * V_ROWS:hh * V_ROWS + HEAD_DIM, :] = vt[hh * HEAD_DIM:(hh + 1) * HEAD_DIM]
            vt_ref[m, 0, hh * V_ROWS + HEAD_DIM:(hh + 1) * V_ROWS, :] = jnp.ones((V_ROWS - HEAD_DIM, TQ), BF16)
    wt_ref[...] = _dot_nt(wit_ref[...], h) * IDX_HEADS ** -0.5


def _proj(x, g, w, cs, wvt, wit):
    t = x.shape[0]
    tm = TQ
    return pl.pallas_call(
        _proj_kernel,
        grid=(t // tm,),
        in_specs=[pl.BlockSpec((tm, D_MODEL), lambda i: (i, 0)),
                  _const_spec((1, D_MODEL)),
                  _const_spec((D_MODEL, N_PACK)),
                  _const_spec((1, N_PACK)),
                  _const_spec((N_MIXERS, 256, D_MODEL)),
                  _const_spec((IDX_HEADS, D_MODEL))],
        out_specs=[pl.BlockSpec((N_SLAB, tm, HEAD_DIM), lambda i: (0, i, 0)),
                   pl.BlockSpec((N_MIXERS, 1, N_HEADS * V_ROWS, tm), lambda i: (0, i, 0, 0)),
                   pl.BlockSpec((IDX_HEADS, tm), lambda i: (0, i))],
        out_shape=[jax.ShapeDtypeStruct((N_SLAB, t, HEAD_DIM), BF16),
                   jax.ShapeDtypeStruct((N_MIXERS, t // tm, N_HEADS * V_ROWS, tm), BF16),
                   jax.ShapeDtypeStruct((IDX_HEADS, t), F32)],
        compiler_params=_cparams(1),
        name="proj",
    )(x, g, w, cs, wvt, wit)


def _k_block(ref, h, j):
    return ref[h, pl.ds(pl.multiple_of(j * TK, TK), TK), :]


def _vt_block(ref, h, j, rows=V_ROWS):
    return ref[0, j, h * V_ROWS:h * V_ROWS + rows, :]


def _softmax_block(s_list, vt_list, carry):
    ms = [jnp.maximum(c[0], jnp.max(s, axis=0, keepdims=True)) for s, c in zip(s_list, carry)]
    pvs = [_dot(vt, jnp.exp2(s - m).astype(BF16)) for vt, s, m in zip(vt_list, s_list, ms)]
    return tuple((m_new, jnp.exp2(m - m_new) * acc + pv) for (m, acc), m_new, pv in zip(carry, ms, pvs))


def _softmax_loop(n_blocks, logits, values, carry):
    def pair(t, carry):
        j = 2 * t
        s_a, s_b = logits(j), logits(j + 1)
        return _softmax_block(s_b, values(j + 1), _softmax_block(s_a, values(j), carry))

    carry = lax.fori_loop(0, n_blocks // 2, pair, carry)
    last = n_blocks - 1
    return lax.cond(n_blocks % 2 == 1, lambda c: _softmax_block(logits(last), values(last), c), lambda c: c, carry)


def _softmax_init():
    return (jnp.full((1, TQ), NEG, F32), jnp.zeros((V_ROWS, TQ), F32))


def _softmax_out(carry):
    _, acc = carry
    return acc[:HEAD_DIM] / acc[HEAD_DIM:HEAD_DIM + 1]


def _store_heads(o_ref, heads_t):
    o_ref[...] = jnp.transpose(jnp.concatenate(heads_t, axis=0)).astype(BF16)


def _attn_specs(nq, gq, gk, mixer, s):
    return [pl.BlockSpec((4, TQ, HEAD_DIM), lambda b, i: (gq, b * nq + i, 0)),
            pl.BlockSpec((4, s, HEAD_DIM), lambda b, i: (gk, b, 0)),
            pl.BlockSpec((1, s // TK, N_HEADS * V_ROWS, TK), lambda b, i: (mixer, b, 0, 0))]


def _out_spec(nq):
    return pl.BlockSpec((TQ, N_HEADS * HEAD_DIM), lambda b, i: (b * nq + i, 0))


def _sb_kernel(q_ref, k_ref, vt_ref, tri_ref, mask_ref, o_ref):
    i = pl.program_id(1)
    tri = tri_ref[...]

    def block(j, carry, masked):
        heads = range(N_HEADS)
        zs = [_dot_nt(_k_block(k_ref, h, j), q_ref[h]) for h in heads]
        if masked:
            zs = [z + mask_ref[...] for z in zs]
        sps = [jnp.maximum(z, 0.0) + jnp.log(1.0 + jnp.exp(-jnp.abs(z))) for z in zs]
        his = [sp.astype(BF16) for sp in sps]
        los = [(sp - hi.astype(F32)).astype(BF16) for sp, hi in zip(sps, his)]
        cs = [_dot(tri, hi) + _dot(tri, lo) + c[0] for hi, lo, c in zip(his, los, carry)]
        avs = [_dot(_vt_block(vt_ref, h, j, HEAD_DIM), jnp.exp(z - c).astype(BF16))
               for h, z, c in zip(heads, zs, cs)]
        return tuple((c[0:1, :], old[1] + av) for c, old, av in zip(cs, carry, avs))

    init = tuple((jnp.zeros((1, TQ), F32), jnp.zeros((HEAD_DIM, TQ), F32)) for _ in range(N_HEADS))
    carry = block(i, init, True)

    def weights_left(carry):
        tail = functools.reduce(jnp.minimum, [c[0] for c in carry])
        return (jnp.min(tail) < SB_TAIL_CUTOFF).astype(jnp.int32)

    def earlier_block(c):
        j, carry, _ = c
        carry = block(j, carry, False)
        return j - 1, carry, weights_left(carry)

    _, carry, _ = lax.while_loop(lambda c: (c[0] >= 0) & (c[2] > 0), earlier_block, (i - 1, carry, weights_left(carry)))
    _store_heads(o_ref, [c[1] for c in carry])


def _sb_attention(zz, vt, b, s, tri, mask):
    nq = s // TQ
    return pl.pallas_call(
        _sb_kernel,
        grid=(b, nq),
        in_specs=_attn_specs(nq, G_SB_Q, G_SB_K, M_SB, s) + [_const_spec((TK, TK)), _const_spec((TK, TQ))],
        out_specs=_out_spec(nq),
        out_shape=jax.ShapeDtypeStruct((b * s, N_HEADS * HEAD_DIM), BF16),
        compiler_params=_cparams(2),
        name="sb_attn",
    )(zz, zz, vt, tri, mask)


def _diff_kernel(q_ref, k_ref, vt_ref, bt_ref, lam_ref, cst_ref, g_ref, o_ref):
    i = pl.program_id(1)
    lp = lam_ref[...]
    lam_init = cst_ref[:, 0:1]
    lam = (jnp.exp(jnp.sum(lp[0:1] * lp[1:2], axis=-1, keepdims=True))
           - jnp.exp(jnp.sum(lp[2:3] * lp[3:4], axis=-1, keepdims=True)) + lam_init)
    lane = lax.broadcasted_iota(jnp.int32, (TQ, HEAD_DIM), 1)
    qs = []
    for h in range(N_HEADS):
        q = q_ref[h]
        qs.append((jnp.where(lane < DIFF_QK_DIM, q, jnp.zeros_like(q)),
                   jnp.where(lane >= DIFF_QK_DIM, q, jnp.zeros_like(q))))

    def logits(j):
        s_list = []
        for h in range(N_HEADS):
            kj = _k_block(k_ref, h, j)
            bias = bt_ref[h, jnp.minimum(i - j, 2)]
            s_list += [_dot_nt(kj, qs[h][0]) + bias, _dot_nt(kj, qs[h][1]) + bias]
        return s_list

    def values(j):
        return [_vt_block(vt_ref, h, j) for h in range(N_HEADS) for _ in range(2)]

    carry = _softmax_loop(i + 1, logits, values, tuple(_softmax_init() for _ in range(2 * N_HEADS)))
    outs = []
    for h in range(N_HEADS):
        o = _softmax_out(carry[2 * h]) - lam * _softmax_out(carry[2 * h + 1])
        o = o * lax.rsqrt(jnp.mean(o * o, axis=0, keepdims=True) + NORM_EPS) * g_ref[...]
        outs.append(o * (1.0 - lam_init))
    _store_heads(o_ref, outs)


def _diff_attention(zz, vt, b, s, bt, lamp, cst, g):
    nq = s // TQ
    return pl.pallas_call(
        _diff_kernel,
        grid=(b, nq),
        in_specs=_attn_specs(nq, G_DF_Q, G_DF_K, M_DF, s) + [
            _const_spec((N_HEADS, 3, TK, TQ)), _const_spec((4, DIFF_QK_DIM)), _const_spec((1, 128)),
            _const_spec((HEAD_DIM, 1))],
        out_specs=_out_spec(nq),
        out_shape=jax.ShapeDtypeStruct((b * s, N_HEADS * HEAD_DIM), BF16),
        compiler_params=_cparams(2),
        name="diff_attn",
    )(zz, zz, vt, bt, lamp, cst, g)


def _dsa_kernel(q_ref, k_ref, vt_ref, qia_ref, qib_ref, ki_ref, wt_ref, bt_ref, tril_ref, o_ref, sc_ref, *, topk):
    i = pl.program_id(1)
    nb = i + 1
    kf = float(topk)
    w = wt_ref[...]
    key = lax.broadcasted_iota(jnp.int32, (TK, TQ), 0)
    qry = lax.broadcasted_iota(jnp.int32, (TK, TQ), 1)

    def score_block(j, _):
        kij = _k_block(ki_ref, 0, j)
        sc = jnp.zeros((TK, TQ), F32)
        for hh in range(IDX_HEADS):
            qi = (qia_ref if hh < 4 else qib_ref)[hh % 4]
            sc = sc + w[hh:hh + 1, :] * jnp.maximum(_dot_nt(kij, qi), 0.0)
        sc_ref[j] = sc
        return 0

    lax.fori_loop(0, nb, score_block, 0)
    sc_ref[i] = jnp.where(key <= qry, sc_ref[i], NEG)

    def reduce_blocks(fn, init):
        return lax.fori_loop(0, nb, lambda j, c: fn(sc_ref[j], j, c), init)

    def count_ge(t):
        return reduce_blocks(
            lambda x, j, c: c + jnp.sum(jnp.where(x >= t, 1.0, 0.0), axis=0, keepdims=True),
            jnp.zeros((1, TQ), F32))

    def minmax(x, j, c):
        return (jnp.minimum(c[0], jnp.min(jnp.where(x > HALF_NEG, x, BIG), axis=0, keepdims=True)),
                jnp.maximum(c[1], jnp.max(x, axis=0, keepdims=True)))

    n_valid = i * TQ + lax.broadcasted_iota(jnp.int32, (1, TQ), 1) + 1
    take_all = n_valid <= topk
    lo, hi = reduce_blocks(minmax, (jnp.full((1, TQ), BIG, F32), jnp.full((1, TQ), NEG, F32)))
    c_max = count_ge(hi)
    at_max = c_max >= kf
    state = (jnp.where(at_max, hi, lo), hi, jnp.where(at_max, c_max, n_valid.astype(F32)), c_max)

    def bisect(_, state):
        lo, hi, c_lo, c_hi = state
        mid = 0.5 * lo + 0.5 * hi
        c = count_ge(mid)
        ge = c >= kf
        return jnp.where(ge, mid, lo), jnp.where(ge, hi, mid), jnp.where(ge, c, c_lo), jnp.where(ge, c_hi, c)

    def unsettled(state):
        lo, hi, c_lo, _ = state
        open_q = jnp.where(take_all, 0.0, jnp.where(c_lo != kf, jnp.where(lo < hi, 1.0, 0.0), 0.0))

        def band_spread():
            def f(x, j, c):
                inb = (x >= lo) & (x < hi)
                return (jnp.minimum(c[0], jnp.min(jnp.where(inb, x, BIG), axis=0, keepdims=True)),
                        jnp.maximum(c[1], jnp.max(jnp.where(inb, x, -BIG), axis=0, keepdims=True)))
            b_min, b_max = reduce_blocks(f, (jnp.full((1, TQ), BIG, F32), jnp.full((1, TQ), -BIG, F32)))
            return (jnp.max(jnp.where(b_max != b_min, open_q, 0.0)) > 0.0).astype(jnp.int32)

        return lax.cond(jnp.max(open_q) > 0.0, band_spread, lambda: jnp.int32(0))

    state = lax.fori_loop(0, BISECT_WARMUP, bisect, state)

    def trip(c):
        n, state, _ = c
        state = lax.fori_loop(0, BISECT_TRIP, bisect, state)
        return n + 1, state, unsettled(state)

    _, state, _ = lax.while_loop(lambda c: (c[2] > 0) & (c[0] < BISECT_MAX_TRIPS), trip,
                                 (jnp.int32(0), state, unsettled(state)))
    lo, hi, c_lo, c_hi = state
    hi_ok = lo < hi
    c_above = jnp.where(hi_ok, c_hi, 0.0)
    hi_sel = jnp.where(hi_ok, hi, BIG)
    need = jnp.where(take_all, BIG, kf - c_above)
    lo_sel = jnp.where(take_all, HALF_NEG, lo)

    tril = tril_ref[...]

    def write_mask(j, taken):
        x = sc_ref[j]
        band = jnp.where(x >= lo_sel, jnp.where(x < hi_sel, 1.0, 0.0), 0.0)
        rank = _dot(tril, band.astype(BF16)) + taken
        sc_ref[j] = jnp.where(x >= hi_sel, 0.0, jnp.where(band * rank > 0.0, jnp.where(rank <= need, 0.0, NEG), NEG))
        return rank[TK - 1:TK, :]

    lax.fori_loop(0, nb, write_mask, jnp.zeros((1, TQ), F32))

    def logits(j):
        return [_dot_nt(_k_block(k_ref, h, j), q_ref[h]) + bt_ref[h, jnp.minimum(i - j, 2)] + sc_ref[j]
                for h in range(N_HEADS)]

    carry = _softmax_loop(nb, logits, lambda j: [_vt_block(vt_ref, h, j) for h in range(N_HEADS)],
                          tuple(_softmax_init() for _ in range(N_HEADS)))
    _store_heads(o_ref, [_softmax_out(c) for c in carry])


def _dsa_attention(zz, vt, wt, b, s, bt, tril):
    nq = s // TQ
    topk = min(DSA_TOPK_MAX, s // 4)
    return pl.pallas_call(
        functools.partial(_dsa_kernel, topk=topk),
        grid=(b, nq),
        in_specs=_attn_specs(nq, G_DS_Q, G_DS_K, M_DS, s) + [
            pl.BlockSpec((4, TQ, HEAD_DIM), lambda b_, i: (G_QI_A, b_ * nq + i, 0)),
            pl.BlockSpec((4, TQ, HEAD_DIM), lambda b_, i: (G_QI_B, b_ * nq + i, 0)),
            pl.BlockSpec((1, s, HEAD_DIM), lambda b_, i: (S_KIDX, b_, 0)),
            pl.BlockSpec((IDX_HEADS, TQ), lambda b_, i: (0, b_ * nq + i)),
            _const_spec((N_HEADS, 3, TK, TQ)), _const_spec((TK, TK))],
        out_specs=_out_spec(nq),
        out_shape=jax.ShapeDtypeStruct((b * s, N_HEADS * HEAD_DIM), BF16),
        scratch_shapes=[pltpu.VMEM((nq, TK, TQ), F32)],
        compiler_params=_cparams(2),
        name="dsa_attn",
    )(zz, zz, vt, zz, zz, zz, wt, bt, tril)


def _moba_kernel(q_ref, k_ref, vt_ref, bt_ref, o_ref, km_ref, *, nblk, topb):
    i = pl.program_id(1)
    nrow = km_ref.shape[1]

    @pl.when(i == 0)
    def _():
        km_ref[...] = jnp.zeros_like(km_ref)
        for h in range(N_HEADS):
            for n in range(nblk):
                kb = k_ref[h, n * MOBA_BLOCK:(n + 1) * MOBA_BLOCK, :].astype(F32)
                km_ref[h, n:n + 1, :] = jnp.mean(kb, axis=0, keepdims=True)

    blk = lax.broadcasted_iota(jnp.int32, (nrow, TQ), 0)
    past = blk < i
    head_bits = []
    for h in range(N_HEADS):
        gate = _dot_nt(km_ref[h].astype(BF16), q_ref[h])
        bits = jnp.zeros((1, TQ), F32)
        for n in range(nblk):
            gn = gate[n:n + 1, :]
            beats = jnp.where(past, jnp.where(gate > gn, 1.0, jnp.where(gate == gn, jnp.where(blk < n, 1.0, 0.0), 0.0)), 0.0)
            rank = jnp.sum(beats, axis=0, keepdims=True)
            bits = bits + jnp.where(rank < float(topb), jnp.where(n < i, float(2 ** n), 0.0), 0.0)
        head_bits.append(bits.astype(jnp.int32))

    def values(n):
        return [_vt_block(vt_ref, h, n) for h in range(N_HEADS)]

    own = _softmax_block([_dot_nt(_k_block(k_ref, h, i), q_ref[h]) + bt_ref[h, 0] for h in range(N_HEADS)],
                         values(i), tuple(_softmax_init() for _ in range(N_HEADS)))

    def logits(n):
        s_list = []
        for h in range(N_HEADS):
            picked = (lax.shift_right_logical(head_bits[h], jnp.full_like(head_bits[h], n)) & 1) == 1
            s_list.append(_dot_nt(_k_block(k_ref, h, n), q_ref[h]) + bt_ref[h, jnp.minimum(i - n, 2)]
                          + jnp.where(picked, 0.0, NEG))
        return s_list

    carry = _softmax_loop(i, logits, values, own)
    _store_heads(o_ref, [_softmax_out(c) for c in carry])


def _moba_attention(zz, vt, b, s, bt):
    nq = s // TQ
    nblk = s // MOBA_BLOCK
    topb = min(MOBA_TOPK, nblk - 1)
    return pl.pallas_call(
        functools.partial(_moba_kernel, nblk=nblk, topb=topb),
        grid=(b, nq),
        in_specs=_attn_specs(nq, G_MB_Q, G_MB_K, M_MB, s) + [_const_spec((N_HEADS, 3, TK, TQ))],
        out_specs=_out_spec(nq),
        out_shape=jax.ShapeDtypeStruct((b * s, N_HEADS * HEAD_DIM), BF16),
        scratch_shapes=[pltpu.VMEM((N_HEADS, max(8, nblk), HEAD_DIM), F32)],
        compiler_params=_cparams(2),
        name="moba_attn",
    )(zz, zz, vt, bt)


def _merge_kernel(x_ref, osb_ref, odf_ref, ods_ref, omb_ref, gpre_ref, wg_ref, wbr_ref, wout_ref, gpost_ref, o_ref):
    x = x_ref[...]
    h = _rms(x, gpre_ref[...]).astype(BF16)
    y = jnp.zeros((x.shape[0], D_MODEL), F32)
    for r, o_r in enumerate((osb_ref, odf_ref, ods_ref, omb_ref)):
        gate = jax.nn.sigmoid(_dot(h, wg_ref[:, r * D_MODEL:(r + 1) * D_MODEL]))
        y = y + gate * _dot(o_r[...], wbr_ref[r])
    o_ref[...] = x + _rms(_dot(y.astype(BF16), wout_ref[...]), gpost_ref[...])


def _merge(x, o_sb, o_df, o_ds, o_mb, g_pre, w_gate, w_br, w_out, g_post):
    t = x.shape[0]
    tm = TM_MERGE
    tok = lambda width: pl.BlockSpec((tm, width), lambda i: (i, 0))
    return pl.pallas_call(
        _merge_kernel,
        grid=(t // tm,),
        in_specs=[tok(D_MODEL), tok(256), tok(256), tok(256), tok(256),
                  _const_spec((1, D_MODEL)), _const_spec((D_MODEL, 4 * D_MODEL)),
                  _const_spec((4, 256, D_MODEL)), _const_spec((D_MODEL, D_MODEL)), _const_spec((1, D_MODEL))],
        out_specs=tok(D_MODEL),
        out_shape=jax.ShapeDtypeStruct((t, D_MODEL), F32),
        compiler_params=_cparams(1),
        name="merge",
    )(x, o_sb, o_df, o_ds, o_mb, g_pre, w_gate, w_br, w_out, g_post)


def _ffn_kernel(x_ref, gpre_ref, win_ref, wout_ref, gpost_ref, o_ref):
    x = x_ref[...]
    h = _rms(x, gpre_ref[...]).astype(BF16)
    gate = _dot(h, win_ref[:, 0:D_FF])
    up = _dot(h, win_ref[:, D_FF:2 * D_FF])
    act = (gate * jax.nn.sigmoid(gate) * up).astype(BF16)
    o_ref[...] = x + _rms(_dot(act, wout_ref[...]), gpost_ref[...])


def _ffn(x, g_pre, w_in, w_out, g_post):
    t = x.shape[0]
    tm = TM_FFN
    return pl.pallas_call(
        _ffn_kernel,
        grid=(t // tm,),
        in_specs=[pl.BlockSpec((tm, D_MODEL), lambda i: (i, 0)), _const_spec((1, D_MODEL)),
                  _const_spec((D_MODEL, 2 * D_FF)), _const_spec((D_FF, D_MODEL)), _const_spec((1, D_MODEL))],
        out_specs=pl.BlockSpec((tm, D_MODEL), lambda i: (i, 0)),
        out_shape=jax.ShapeDtypeStruct((t, D_MODEL), F32),
        compiler_params=_cparams(1),
        name="ffn",
    )(x, g_pre, w_in, w_out, g_post)


def _t5_bucket(dist):
    max_exact = N_BUCKETS // 2
    d = jnp.maximum(dist, 0)
    log_ratio = jnp.log(jnp.maximum(d, 1).astype(F32) / max_exact) / math.log(MAX_DISTANCE / max_exact)
    large = jnp.minimum(max_exact + (log_ratio * (N_BUCKETS - max_exact)).astype(jnp.int32), N_BUCKETS - 1)
    return jnp.where(d < max_exact, d, large)


def _bias_tiles(rel_bias):
    assert TQ == TK
    n = TK
    nh = rel_bias.shape[1]
    d = np.arange(-(n - 1), 3 * n)
    by_dist = rel_bias.astype(F32).T[:, _t5_bucket(jnp.asarray(np.maximum(d, 0), jnp.int32))]
    by_dist = jnp.where(jnp.asarray(d >= 0)[None, :], by_dist * LOG2E, NEG)
    tiles = []
    for o in range(3):
        w = by_dist[:, o * n:o * n + 2 * n - 1]
        w = jnp.concatenate([w, jnp.zeros((nh, 1), F32)], axis=1)
        flat = jnp.broadcast_to(w[:, None, :], (nh, n, 2 * n)).reshape(nh, 2 * n * n)
        tiles.append(flat[:, :n * (2 * n - 1)].reshape(nh, n, 2 * n - 1)[:, :, n - 1:])
    return jnp.stack(tiles, axis=1)


def _pack_weights(w_in):
    parts, start = [], 0
    for e in range(1, N_PACK + 1):
        if e == N_PACK or _PACK_SRC[e] != _PACK_SRC[e - 1] + (1 if _PACK_SRC[e - 1] >= 0 else 0):
            a = int(_PACK_SRC[start])
            n = e - start
            parts.append(w_in[:, :, a:a + n] if a >= 0 else jnp.zeros(w_in.shape[:2] + (n,), w_in.dtype))
            start = e
    return jnp.concatenate(parts, axis=2).astype(BF16)


def _transposed_weights(w_in):
    wv = jnp.stack([w_in[:, :, _OFF[n]:_OFF[n] + 256] for n in ("v_sb", "v_df", "v_ds", "v_mb")], axis=1)
    wi = w_in[:, :, _OFF["wi"]:_OFF["wi"] + IDX_HEADS]
    return jnp.swapaxes(wv, 2, 3).astype(BF16), jnp.swapaxes(wi, 1, 2).astype(BF16)


def kernel(x, w_in, w_br_sb, w_br_diff, w_br_dsa, w_br_moba, w_out, lambda_q1, lambda_k1, lambda_q2, lambda_k2,
           diff_subln_g, rel_bias, w_ffn_in, w_ffn_out, g_pre_mix, g_post_mix, g_pre_ffn, g_post_ffn):
    b, s, d = x.shape
    depth = w_in.shape[0]
    assert d == D_MODEL and s % TQ == 0 and s // MOBA_BLOCK >= 2
    t = b * s

    w_pack = _pack_weights(w_in)
    w_vt, w_it = _transposed_weights(w_in)
    w_gate = w_in[:, :, _OFF["gate"]:].astype(BF16)
    w_br = jnp.stack([w_br_sb, w_br_diff, w_br_dsa, w_br_moba], axis=1).astype(BF16)
    w_o = w_out.astype(BF16)
    w_f1 = w_ffn_in.astype(BF16)
    w_f2 = w_ffn_out.astype(BF16)
    cs = jnp.asarray(_PACK_SCALE)[None, :]

    bt = _bias_tiles(rel_bias)
    bt_df, bt_ds, bt_mb = bt[0:4], bt[4:8], bt[8:12]
    key = np.arange(TK)[:, None]
    qry = np.arange(TQ)[None, :]
    tri = jnp.asarray(key <= np.arange(TK)[None, :], BF16)
    tril = jnp.asarray(key >= np.arange(TK)[None, :], BF16)
    sb_mask = jnp.asarray(np.where(key < qry, 0.0, NEG), F32)

    xf = x.reshape(t, d)
    for l in range(depth):
        lam_init = 0.8 - 0.6 * math.exp(-0.3 * l)
        lamp = jnp.stack([lambda_q1[l], lambda_k1[l], lambda_q2[l], lambda_k2[l]]).astype(F32)
        cst = jnp.full((1, 128), lam_init, F32)
        zz, vt, wt = _proj(xf, g_pre_mix[l][None, :], w_pack[l], cs, w_vt[l], w_it[l])
        o_sb = _sb_attention(zz, vt, b, s, tri, sb_mask)
        o_df = _diff_attention(zz, vt, b, s, bt_df, lamp, cst, diff_subln_g[l][:, None])
        o_ds = _dsa_attention(zz, vt, wt, b, s, bt_ds, tril)
        o_mb = _moba_attention(zz, vt, b, s, bt_mb)
        xf = _merge(xf, o_sb, o_df, o_ds, o_mb, g_pre_mix[l][None, :], w_gate[l], w_br[l], w_o[l],
                    g_post_mix[l][None, :])
        xf = _ffn(xf, g_pre_ffn[l][None, :], w_f1[l], w_f2[l], g_post_ffn[l][None, :])
    return xf.reshape(b, s, d)
```

```python
import functools
import math

import numpy as np
import jax
import jax.numpy as jnp
from jax import lax
from jax.experimental import pallas as pl
from jax.experimental.pallas import tpu as pltpu

F32 = jnp.float32
BF16 = jnp.bfloat16

D_MODEL = 1024
HEAD_DIM = 64
N_HEADS = 4
N_MIXERS = 4
DIFF_QK_DIM = 32
IDX_HEADS = 8
DSA_TOPK_MAX = 256
MOBA_BLOCK = 256
MOBA_TOPK = 3
N_BUCKETS = 32
MAX_DISTANCE = 128
D_FF = 2816
NORM_EPS = 1e-6

TQ = 256
TK = 256
NEG = -1e30
HALF_NEG = -0.5e30
BIG = 3e38
BISECT_WARMUP = 16
BISECT_TRIP = 4
BISECT_MAX_TRIPS = 70
SB_TAIL_CUTOFF = 120.0
LOG2E = math.log2(math.e)
V_ROWS = 80
TM_MERGE = 256
TM_FFN = 256
N_GROUP = 11
N_SLAB = 4 * N_GROUP
N_PACK = N_SLAB * HEAD_DIM
VMEM_LIMIT = 56 * 1024 * 1024

G_SB_Q, G_SB_K, G_DF_Q, G_DF_K, G_DS_Q, G_DS_K, G_QI_A, G_QI_B, G_MB_Q, G_MB_K, G_KIDX = range(N_GROUP)
S_KIDX = 4 * G_KIDX
M_SB, M_DF, M_DS, M_MB = range(N_MIXERS)


def _layout():
    off = {}
    acc = 0
    for name, sz in (("q_sb", 256), ("k_sb", 256), ("v_sb", 256), ("q1", 128), ("q2", 128), ("k1", 128),
                     ("k2", 128), ("v_df", 256), ("q_ds", 256), ("k_ds", 256), ("v_ds", 256), ("qi", 512),
                     ("ki", 64), ("wi", 8), ("q_mb", 256), ("k_mb", 256), ("v_mb", 256), ("gate", 4096)):
        off[name] = acc
        acc += sz
    return off


_OFF = _layout()


def _pack_layout():
    off = _OFF
    cols, scale = [], []

    def add(start, n, s=1.0):
        cols.extend(range(start, start + n))
        scale.extend([s] * n)

    hd = HEAD_DIM ** -0.5
    hd2 = hd * LOG2E
    df2 = DIFF_QK_DIM ** -0.5 * LOG2E
    add(off["q_sb"], 256, hd); add(off["k_sb"], 256)
    for h in range(N_HEADS):
        add(off["q1"] + h * 32, 32, df2); add(off["q2"] + h * 32, 32, df2)
    for h in range(N_HEADS):
        add(off["k1"] + h * 32, 32); add(off["k2"] + h * 32, 32)
    add(off["q_ds"], 256, hd2); add(off["k_ds"], 256)
    add(off["qi"], 512, HEAD_DIM ** -0.5)
    add(off["q_mb"], 256, hd2); add(off["k_mb"], 256)
    add(off["ki"], 64)
    cols.extend([-1] * 192); scale.extend([1.0] * 192)
    assert len(cols) == N_PACK
    return np.asarray(cols, np.int32), np.asarray(scale, np.float32)


_PACK_SRC, _PACK_SCALE = _pack_layout()


def _dot(a, b):
    return jnp.dot(a, b, preferred_element_type=F32)


def _dot_nt(a, b):
    return lax.dot_general(a, b, (((1,), (1,)), ((), ())), preferred_element_type=F32)


def _rms(x, g):
    return x * lax.rsqrt(jnp.mean(x * x, axis=-1, keepdims=True) + NORM_EPS) * g


def _cparams(n_axes):
    return pltpu.CompilerParams(dimension_semantics=("arbitrary",) * n_axes, vmem_limit_bytes=VMEM_LIMIT)


def _const_spec(shape):
    nd = len(shape)
    return pl.BlockSpec(shape, lambda *_: (0,) * nd, pipeline_mode=pl.Buffered(1))


def _proj_kernel(x_ref, g_ref, w_ref, cs_ref, wvt_ref, wit_ref, zz_ref, vt_ref, wt_ref):
    h = _rms(x_ref[...], g_ref[...]).astype(BF16)
    for c in range(N_GROUP):
        r = _dot(h, w_ref[:, c * 256:(c + 1) * 256]) * cs_ref[:, c * 256:(c + 1) * 256]
        for s in range(4):
            zz_ref[4 * c + s] = r[:, s * HEAD_DIM:(s + 1) * HEAD_DIM].astype(BF16)
    for m in range(N_MIXERS):
        vt = _dot_nt(wvt_ref[m], h).astype(BF16)
        for hh in range(N_HEADS):
            vt_ref[m, 0, hh * V_ROWS:hh * V_ROWS + HEAD_DIM, :] = vt[hh * HEAD_DIM:(hh + 1) * HEAD_DIM]
            vt_ref[m, 0, hh * V_ROWS + HEAD_DIM:(hh + 1) * V_ROWS, :] = jnp.ones((V_ROWS - HEAD_DIM, TQ), BF16)
    wt_ref[...] = _dot_nt(wit_ref[...], h) * IDX_HEADS ** -0.5


def _proj(x, g, w, cs, wvt, wit):
    t = x.shape[0]
    tm = TQ
    return pl.pallas_call(
        _proj_kernel,
        grid=(t // tm,),
        in_specs=[pl.BlockSpec((tm, D_MODEL), lambda i: (i, 0)),
                  _const_spec((1, D_MODEL)),
                  _const_spec((D_MODEL, N_PACK)),
                  _const_spec((1, N_PACK)),
                  _const_spec((N_MIXERS, 256, D_MODEL)),
                  _const_spec((IDX_HEADS, D_MODEL))],
        out_specs=[pl.BlockSpec((N_SLAB, tm, HEAD_DIM), lambda i: (0, i, 0)),
                   pl.BlockSpec((N_MIXERS, 1, N_HEADS * V_ROWS, tm), lambda i: (0, i, 0, 0)),
                   pl.BlockSpec((IDX_HEADS, tm), lambda i: (0, i))],
        out_shape=[jax.ShapeDtypeStruct((N_SLAB, t, HEAD_DIM), BF16),
                   jax.ShapeDtypeStruct((N_MIXERS, t // tm, N_HEADS * V_ROWS, tm), BF16),
                   jax.ShapeDtypeStruct((IDX_HEADS, t), F32)],
        compiler_params=_cparams(1),
        name="proj",
    )(x, g, w, cs, wvt, wit)


def _k_block(ref, h, j):
    return ref[h, pl.ds(pl.multiple_of(j * TK, TK), TK), :]


def _fold_keys(a, op):
    n = a.shape[0]
    while n > 8:
        n //= 2
        a = op(a[:n], a[n:2 * n])
    return a


def _vt_block(ref, h, j, rows=V_ROWS):
    return ref[0, j, h * V_ROWS:h * V_ROWS + rows, :]


def _softmax_block(s_list, vt_list, carry):
    ms = [jnp.maximum(c[0], jnp.max(_fold_keys(s, jnp.maximum), axis=0, keepdims=True))
          for s, c in zip(s_list, carry)]
    pvs = [_dot(vt, jnp.exp2(s - m).astype(BF16)) for vt, s, m in zip(vt_list, s_list, ms)]
    return tuple((m_new, jnp.exp2(m - m_new) * acc + pv) for (m, acc), m_new, pv in zip(carry, ms, pvs))


def _softmax_loop(n_blocks, logits, values, carry):
    def pair(t, carry):
        j = 2 * t
        s_a, s_b = logits(j), logits(j + 1)
        return _softmax_block(s_b, values(j + 1), _softmax_block(s_a, values(j), carry))

    def single(j, carry):
        return _softmax_block(logits(j), values(j), carry)

    n_pairs = n_blocks // 2
    carry = lax.fori_loop(0, n_pairs, pair, carry)
    return lax.fori_loop(2 * n_pairs, n_blocks, single, carry)


def _softmax_init():
    return (jnp.full((1, TQ), NEG, F32), jnp.zeros((V_ROWS, TQ), F32))


def _softmax_out(carry):
    _, acc = carry
    return acc[:HEAD_DIM] / acc[HEAD_DIM:HEAD_DIM + 1]


def _store_heads(o_ref, heads_t):
    o_ref[...] = jnp.transpose(jnp.concatenate(heads_t, axis=0)).astype(BF16)


def _attn_specs(nq, gq, gk, mixer, s):
    return [pl.BlockSpec((4, TQ, HEAD_DIM), lambda b, i: (gq, b * nq + i, 0)),
            pl.BlockSpec((4, s, HEAD_DIM), lambda b, i: (gk, b, 0)),
            pl.BlockSpec((1, s // TK, N_HEADS * V_ROWS, TK), lambda b, i: (mixer, b, 0, 0))]


def _out_spec(nq):
    return pl.BlockSpec((TQ, N_HEADS * HEAD_DIM), lambda b, i: (b * nq + i, 0))


def _sb_kernel(q_ref, k_ref, vt_ref, tri_ref, mask_ref, o_ref):
    i = pl.program_id(1)
    tri = tri_ref[...]

    def block(j, carry, masked):
        heads = range(N_HEADS)
        zs = [_dot_nt(_k_block(k_ref, h, j), q_ref[h]) for h in heads]
        if masked:
            zs = [z + mask_ref[...] for z in zs]
        sps = [jnp.maximum(z, 0.0) + jnp.log(1.0 + jnp.exp(-jnp.abs(z))) for z in zs]
        his = [sp.astype(BF16) for sp in sps]
        los = [(sp - hi.astype(F32)).astype(BF16) for sp, hi in zip(sps, his)]
        cs = [_dot(tri, hi) + _dot(tri, lo) + c[0] for hi, lo, c in zip(his, los, carry)]
        avs = [_dot(_vt_block(vt_ref, h, j, HEAD_DIM), jnp.exp(z - c).astype(BF16))
               for h, z, c in zip(heads, zs, cs)]
        return tuple((c[0:1, :], old[1] + av) for c, old, av in zip(cs, carry, avs))

    init = tuple((jnp.zeros((1, TQ), F32), jnp.zeros((HEAD_DIM, TQ), F32)) for _ in range(N_HEADS))
    carry = block(i, init, True)

    def weights_left(carry):
        tail = functools.reduce(jnp.minimum, [c[0] for c in carry])
        return (jnp.min(tail) < SB_TAIL_CUTOFF).astype(jnp.int32)

    def earlier_block(c):
        j, carry, _ = c
        carry = block(j, carry, False)
        return j - 1, carry, weights_left(carry)

    _, carry, _ = lax.while_loop(lambda c: (c[0] >= 0) & (c[2] > 0), earlier_block, (i - 1, carry, weights_left(carry)))
    _store_heads(o_ref, [c[1] for c in carry])


def _sb_attention(zz, vt, b, s, tri, mask):
    nq = s // TQ
    return pl.pallas_call(
        _sb_kernel,
        grid=(b, nq),
        in_specs=_attn_specs(nq, G_SB_Q, G_SB_K, M_SB, s) + [_const_spec((TK, TK)), _const_spec((TK, TQ))],
        out_specs=_out_spec(nq),
        out_shape=jax.ShapeDtypeStruct((b * s, N_HEADS * HEAD_DIM), BF16),
        compiler_params=_cparams(2),
        name="sb_attn",
    )(zz, zz, vt, tri, mask)


def _diff_kernel(q_ref, k_ref, vt_ref, bt_ref, lam_ref, cst_ref, g_ref, o_ref):
    i = pl.program_id(1)
    lp = lam_ref[...]
    lam_init = cst_ref[:, 0:1]
    lam = (jnp.exp(jnp.sum(lp[0:1] * lp[1:2], axis=-1, keepdims=True))
           - jnp.exp(jnp.sum(lp[2:3] * lp[3:4], axis=-1, keepdims=True)) + lam_init)
    lane = lax.broadcasted_iota(jnp.int32, (TQ, HEAD_DIM), 1)
    qs = []
    for h in range(N_HEADS):
        q = q_ref[h]
        qs.append((jnp.where(lane < DIFF_QK_DIM, q, jnp.zeros_like(q)),
                   jnp.where(lane >= DIFF_QK_DIM, q, jnp.zeros_like(q))))

    def logits(j):
        s_list = []
        for h in range(N_HEADS):
            kj = _k_block(k_ref, h, j)
            bias = bt_ref[h, jnp.minimum(i - j, 2)]
            s_list += [_dot_nt(kj, qs[h][0]) + bias, _dot_nt(kj, qs[h][1]) + bias]
        return s_list

    def values(j):
        return [_vt_block(vt_ref, h, j) for h in range(N_HEADS) for _ in range(2)]

    carry = _softmax_loop(i + 1, logits, values, tuple(_softmax_init() for _ in range(2 * N_HEADS)))
    outs = []
    for h in range(N_HEADS):
        o = _softmax_out(carry[2 * h]) - lam * _softmax_out(carry[2 * h + 1])
        o = o * lax.rsqrt(jnp.mean(o * o, axis=0, keepdims=True) + NORM_EPS) * g_ref[...]
        outs.append(o * (1.0 - lam_init))
    _store_heads(o_ref, outs)


def _diff_attention(zz, vt, b, s, bt, lamp, cst, g):
    nq = s // TQ
    return pl.pallas_call(
        _diff_kernel,
        grid=(b, nq),
        in_specs=_attn_specs(nq, G_DF_Q, G_DF_K, M_DF, s) + [
            _const_spec((N_HEADS, 3, TK, TQ)), _const_spec((4, DIFF_QK_DIM)), _const_spec((1, 128)),
            _const_spec((HEAD_DIM, 1))],
        out_specs=_out_spec(nq),
        out_shape=jax.ShapeDtypeStruct((b * s, N_HEADS * HEAD_DIM), BF16),
        compiler_params=_cparams(2),
        name="diff_attn",
    )(zz, zz, vt, bt, lamp, cst, g)


def _dsa_kernel(q_ref, k_ref, vt_ref, qia_ref, qib_ref, ki_ref, wt_ref, bt_ref, tril_ref, o_ref, sc_ref, *, topk):
    i = pl.program_id(1)
    nb = i + 1
    kf = float(topk)
    w = wt_ref[...]
    key = lax.broadcasted_iota(jnp.int32, (TK, TQ), 0)
    qry = lax.broadcasted_iota(jnp.int32, (TK, TQ), 1)

    def score_block(j, _):
        kij = _k_block(ki_ref, 0, j)
        sc = jnp.zeros((TK, TQ), F32)
        for hh in range(IDX_HEADS):
            qi = (qia_ref if hh < 4 else qib_ref)[hh % 4]
            sc = sc + w[hh:hh + 1, :] * jnp.maximum(_dot_nt(kij, qi), 0.0)
        sc_ref[j] = sc
        return 0

    lax.fori_loop(0, nb, score_block, 0)
    sc_ref[i] = jnp.where(key <= qry, sc_ref[i], NEG)

    def reduce_blocks(fn, init):
        return lax.fori_loop(0, nb, lambda j, c: fn(sc_ref[j], j, c), init)

    def count_ge(t):
        part = reduce_blocks(lambda x, j, c: c + _fold_keys(jnp.where(x >= t, 1.0, 0.0), jnp.add),
                             jnp.zeros((8, TQ), F32))
        return jnp.sum(part, axis=0, keepdims=True)

    def minmax_blocks(lo_of, hi_of):
        def f(x, j, c):
            return (jnp.minimum(c[0], _fold_keys(lo_of(x), jnp.minimum)),
                    jnp.maximum(c[1], _fold_keys(hi_of(x), jnp.maximum)))
        lo_part, hi_part = reduce_blocks(f, (jnp.full((8, TQ), BIG, F32), jnp.full((8, TQ), -BIG, F32)))
        return jnp.min(lo_part, axis=0, keepdims=True), jnp.max(hi_part, axis=0, keepdims=True)

    n_valid = i * TQ + lax.broadcasted_iota(jnp.int32, (1, TQ), 1) + 1
    take_all = n_valid <= topk
    lo, hi = minmax_blocks(lambda x: jnp.where(x > HALF_NEG, x, BIG), lambda x: x)
    c_max = count_ge(hi)
    at_max = c_max >= kf
    state = (jnp.where(at_max, hi, lo), hi, jnp.where(at_max, c_max, n_valid.astype(F32)), c_max)

    def bisect(_, state):
        lo, hi, c_lo, c_hi = state
        mid = 0.5 * lo + 0.5 * hi
        c = count_ge(mid)
        ge = c >= kf
        return jnp.where(ge, mid, lo), jnp.where(ge, hi, mid), jnp.where(ge, c, c_lo), jnp.where(ge, c_hi, c)

    def unsettled(state):
        lo, hi, c_lo, _ = state
        open_q = jnp.where(take_all, 0.0, jnp.where(c_lo != kf, jnp.where(lo < hi, 1.0, 0.0), 0.0))

        def band_spread():
            b_min, b_max = minmax_blocks(lambda x: jnp.where(x >= lo, jnp.where(x < hi, x, BIG), BIG),
                                         lambda x: jnp.where(x >= lo, jnp.where(x < hi, x, -BIG), -BIG))
            return (jnp.max(jnp.where(b_max != b_min, open_q, 0.0)) > 0.0).astype(jnp.int32)

        return lax.cond(jnp.max(open_q) > 0.0, band_spread, lambda: jnp.int32(0))

    state = lax.fori_loop(0, BISECT_WARMUP, bisect, state)

    def trip(c):
        n, state, _ = c
        state = lax.fori_loop(0, BISECT_TRIP, bisect, state)
        return n + 1, state, unsettled(state)

    _, state, _ = lax.while_loop(lambda c: (c[2] > 0) & (c[0] < BISECT_MAX_TRIPS), trip,
                                 (jnp.int32(0), state, unsettled(state)))
    lo, hi, c_lo, c_hi = state
    hi_ok = lo < hi
    c_above = jnp.where(hi_ok, c_hi, 0.0)
    hi_sel = jnp.where(hi_ok, hi, BIG)
    need = jnp.where(take_all, BIG, kf - c_above)
    lo_sel = jnp.where(take_all, HALF_NEG, lo)

    tied = jnp.max(jnp.where(take_all, 0.0, c_lo - kf)) > 0.0

    @pl.when(tied)
    def _():
        tril = tril_ref[...]

        def write_mask(j, taken):
            x = sc_ref[j]
            band = jnp.where(x >= lo_sel, jnp.where(x < hi_sel, 1.0, 0.0), 0.0)
            rank = _dot(tril, band.astype(BF16)) + taken
            sc_ref[j] = jnp.where(x >= hi_sel, 0.0,
                                  jnp.where(band * rank > 0.0, jnp.where(rank <= need, 0.0, NEG), NEG))
            return rank[TK - 1:TK, :]

        lax.fori_loop(0, nb, write_mask, jnp.zeros((1, TQ), F32))

    @pl.when(jnp.logical_not(tied))
    def _():
        def write_mask(j, _):
            sc_ref[j] = jnp.where(sc_ref[j] >= lo_sel, 0.0, NEG)
            return 0

        lax.fori_loop(0, nb, write_mask, 0)

    def logits(j):
        return [_dot_nt(_k_block(k_ref, h, j), q_ref[h]) + bt_ref[h, jnp.minimum(i - j, 2)] + sc_ref[j]
                for h in range(N_HEADS)]

    carry = _softmax_loop(nb, logits, lambda j: [_vt_block(vt_ref, h, j) for h in range(N_HEADS)],
                          tuple(_softmax_init() for _ in range(N_HEADS)))
    _store_heads(o_ref, [_softmax_out(c) for c in carry])


def _dsa_attention(zz, vt, wt, b, s, bt, tril):
    nq = s // TQ
    topk = min(DSA_TOPK_MAX, s // 4)
    return pl.pallas_call(
        functools.partial(_dsa_kernel, topk=topk),
        grid=(b, nq),
        in_specs=_attn_specs(nq, G_DS_Q, G_DS_K, M_DS, s) + [
            pl.BlockSpec((4, TQ, HEAD_DIM), lambda b_, i: (G_QI_A, b_ * nq + i, 0)),
            pl.BlockSpec((4, TQ, HEAD_DIM), lambda b_, i: (G_QI_B, b_ * nq + i, 0)),
            pl.BlockSpec((1, s, HEAD_DIM), lambda b_, i: (S_KIDX, b_, 0)),
            pl.BlockSpec((IDX_HEADS, TQ), lambda b_, i: (0, b_ * nq + i)),
            _const_spec((N_HEADS, 3, TK, TQ)), _const_spec((TK, TK))],
        out_specs=_out_spec(nq),
        out_shape=jax.ShapeDtypeStruct((b * s, N_HEADS * HEAD_DIM), BF16),
        scratch_shapes=[pltpu.VMEM((nq, TK, TQ), F32)],
        compiler_params=_cparams(2),
        name="dsa_attn",
    )(zz, zz, vt, zz, zz, zz, wt, bt, tril)


def _moba_kernel(q_ref, k_ref, vt_ref, bt_ref, o_ref, km_ref, *, nblk, topb):
    i = pl.program_id(1)
    nrow = km_ref.shape[1]

    @pl.when(i == 0)
    def _():
        km_ref[...] = jnp.zeros_like(km_ref)
        for h in range(N_HEADS):
            for n in range(nblk):
                kb = k_ref[h, n * MOBA_BLOCK:(n + 1) * MOBA_BLOCK, :].astype(F32)
                km_ref[h, n:n + 1, :] = jnp.mean(kb, axis=0, keepdims=True)

    blk = lax.broadcasted_iota(jnp.int32, (nrow, TQ), 0)
    past = blk < i
    head_bits = []
    for h in range(N_HEADS):
        gate = _dot_nt(km_ref[h].astype(BF16), q_ref[h])
        bits = jnp.zeros((1, TQ), F32)
        for n in range(nblk):
            gn = gate[n:n + 1, :]
            beats = jnp.where(past, jnp.where(gate > gn, 1.0, jnp.where(gate == gn, jnp.where(blk < n, 1.0, 0.0), 0.0)), 0.0)
            rank = jnp.sum(beats, axis=0, keepdims=True)
            bits = bits + jnp.where(rank < float(topb), jnp.where(n < i, float(2 ** n), 0.0), 0.0)
        head_bits.append(bits.astype(jnp.int32))

    def values(n):
        return [_vt_block(vt_ref, h, n) for h in range(N_HEADS)]

    own = _softmax_block([_dot_nt(_k_block(k_ref, h, i), q_ref[h]) + bt_ref[h, 0] for h in range(N_HEADS)],
                         values(i), tuple(_softmax_init() for _ in range(N_HEADS)))

    def logits(n):
        s_list = []
        for h in range(N_HEADS):
            picked = (lax.shift_right_logical(head_bits[h], jnp.full_like(head_bits[h], n)) & 1) == 1
            s_list.append(_dot_nt(_k_block(k_ref, h, n), q_ref[h]) + bt_ref[h, jnp.minimum(i - n, 2)]
                          + jnp.where(picked, 0.0, NEG))
        return s_list

    carry = _softmax_loop(i, logits, values, own)
    _store_heads(o_ref, [_softmax_out(c) for c in carry])


def _moba_attention(zz, vt, b, s, bt):
    nq = s // TQ
    nblk = s // MOBA_BLOCK
    topb = min(MOBA_TOPK, nblk - 1)
    return pl.pallas_call(
        functools.partial(_moba_kernel, nblk=nblk, topb=topb),
        grid=(b, nq),
        in_specs=_attn_specs(nq, G_MB_Q, G_MB_K, M_MB, s) + [_const_spec((N_HEADS, 3, TK, TQ))],
        out_specs=_out_spec(nq),
        out_shape=jax.ShapeDtypeStruct((b * s, N_HEADS * HEAD_DIM), BF16),
        scratch_shapes=[pltpu.VMEM((N_HEADS, max(8, nblk), HEAD_DIM), F32)],
        compiler_params=_cparams(2),
        name="moba_attn",
    )(zz, zz, vt, bt)


def _merge_kernel(x_ref, osb_ref, odf_ref, ods_ref, omb_ref, gpre_ref, wg_ref, wbr_ref, wout_ref, gpost_ref, o_ref):
    x = x_ref[...]
    h = _rms(x, gpre_ref[...]).astype(BF16)
    y = jnp.zeros((x.shape[0], D_MODEL), F32)
    for r, o_r in enumerate((osb_ref, odf_ref, ods_ref, omb_ref)):
        gate = jax.nn.sigmoid(_dot(h, wg_ref[:, r * D_MODEL:(r + 1) * D_MODEL]))
        y = y + gate * _dot(o_r[...], wbr_ref[r])
    o_ref[...] = x + _rms(_dot(y.astype(BF16), wout_ref[...]), gpost_ref[...])


def _merge(x, o_sb, o_df, o_ds, o_mb, g_pre, w_gate, w_br, w_out, g_post):
    t = x.shape[0]
    tm = TM_MERGE
    tok = lambda width: pl.BlockSpec((tm, width), lambda i: (i, 0))
    return pl.pallas_call(
        _merge_kernel,
        grid=(t // tm,),
        in_specs=[tok(D_MODEL), tok(256), tok(256), tok(256), tok(256),
                  _const_spec((1, D_MODEL)), _const_spec((D_MODEL, 4 * D_MODEL)),
                  _const_spec((4, 256, D_MODEL)), _const_spec((D_MODEL, D_MODEL)), _const_spec((1, D_MODEL))],
        out_specs=tok(D_MODEL),
        out_shape=jax.ShapeDtypeStruct((t, D_MODEL), F32),
        compiler_params=_cparams(1),
        name="merge",
    )(x, o_sb, o_df, o_ds, o_mb, g_pre, w_gate, w_br, w_out, g_post)


def _ffn_kernel(x_ref, gpre_ref, win_ref, wout_ref, gpost_ref, o_ref):
    x = x_ref[...]
    h = _rms(x, gpre_ref[...]).astype(BF16)
    gate = _dot(h, win_ref[:, 0:D_FF])
    up = _dot(h, win_ref[:, D_FF:2 * D_FF])
    act = (gate * jax.nn.sigmoid(gate) * up).astype(BF16)
    o_ref[...] = x + _rms(_dot(act, wout_ref[...]), gpost_ref[...])


def _ffn(x, g_pre, w_in, w_out, g_post):
    t = x.shape[0]
    tm = TM_FFN
    return pl.pallas_call(
        _ffn_kernel,
        grid=(t // tm,),
        in_specs=[pl.BlockSpec((tm, D_MODEL), lambda i: (i, 0)), _const_spec((1, D_MODEL)),
                  _const_spec((D_MODEL, 2 * D_FF)), _const_spec((D_FF, D_MODEL)), _const_spec((1, D_MODEL))],
        out_specs=pl.BlockSpec((tm, D_MODEL), lambda i: (i, 0)),
        out_shape=jax.ShapeDtypeStruct((t, D_MODEL), F32),
        compiler_params=_cparams(1),
        name="ffn",
    )(x, g_pre, w_in, w_out, g_post)


def _t5_bucket(dist):
    max_exact = N_BUCKETS // 2
    d = jnp.maximum(dist, 0)
    log_ratio = jnp.log(jnp.maximum(d, 1).astype(F32) / max_exact) / math.log(MAX_DISTANCE / max_exact)
    large = jnp.minimum(max_exact + (log_ratio * (N_BUCKETS - max_exact)).astype(jnp.int32), N_BUCKETS - 1)
    return jnp.where(d < max_exact, d, large)


def _bias_tiles(rel_bias):
    assert TQ == TK
    n = TK
    nh = rel_bias.shape[1]
    d = np.arange(-(n - 1), 3 * n)
    by_dist = rel_bias.astype(F32).T[:, _t5_bucket(jnp.asarray(np.maximum(d, 0), jnp.int32))]
    by_dist = jnp.where(jnp.asarray(d >= 0)[None, :], by_dist * LOG2E, NEG)
    tiles = []
    for o in range(3):
        w = by_dist[:, o * n:o * n + 2 * n - 1]
        w = jnp.concatenate([w, jnp.zeros((nh, 1), F32)], axis=1)
        flat = jnp.broadcast_to(w[:, None, :], (nh, n, 2 * n)).reshape(nh, 2 * n * n)
        tiles.append(flat[:, :n * (2 * n - 1)].reshape(nh, n, 2 * n - 1)[:, :, n - 1:])
    return jnp.stack(tiles, axis=1)


def _pack_weights(w_in):
    parts, start = [], 0
    for e in range(1, N_PACK + 1):
        if e == N_PACK or _PACK_SRC[e] != _PACK_SRC[e - 1] + (1 if _PACK_SRC[e - 1] >= 0 else 0):
            a = int(_PACK_SRC[start])
            n = e - start
            parts.append(w_in[:, :, a:a + n] if a >= 0 else jnp.zeros(w_in.shape[:2] + (n,), w_in.dtype))
            start = e
    return jnp.concatenate(parts, axis=2).astype(BF16)


def _transposed_weights(w_in):
    wv = jnp.stack([w_in[:, :, _OFF[n]:_OFF[n] + 256] for n in ("v_sb", "v_df", "v_ds", "v_mb")], axis=1)
    wi = w_in[:, :, _OFF["wi"]:_OFF["wi"] + IDX_HEADS]
    return jnp.swapaxes(wv, 2, 3).astype(BF16), jnp.swapaxes(wi, 1, 2).astype(BF16)


def kernel(x, w_in, w_br_sb, w_br_diff, w_br_dsa, w_br_moba, w_out, lambda_q1, lambda_k1, lambda_q2, lambda_k2,
           diff_subln_g, rel_bias, w_ffn_in, w_ffn_out, g_pre_mix, g_post_mix, g_pre_ffn, g_post_ffn):
    b, s, d = x.shape
    depth = w_in.shape[0]
    assert d == D_MODEL and s % TQ == 0 and s // MOBA_BLOCK >= 2
    t = b * s

    w_pack = _pack_weights(w_in)
    w_vt, w_it = _transposed_weights(w_in)
    w_gate = w_in[:, :, _OFF["gate"]:].astype(BF16)
    w_br = jnp.stack([w_br_sb, w_br_diff, w_br_dsa, w_br_moba], axis=1).astype(BF16)
    w_o = w_out.astype(BF16)
    w_f1 = w_ffn_in.astype(BF16)
    w_f2 = w_ffn_out.astype(BF16)
    cs = jnp.asarray(_PACK_SCALE)[None, :]

    bt = _bias_tiles(rel_bias)
    bt_df, bt_ds, bt_mb = bt[0:4], bt[4:8], bt[8:12]
    key = np.arange(TK)[:, None]
    qry = np.arange(TQ)[None, :]
    tri = jnp.asarray(key <= np.arange(TK)[None, :], BF16)
    tril = jnp.asarray(key >= np.arange(TK)[None, :], BF16)
    sb_mask = jnp.asarray(np.where(key < qry, 0.0, NEG), F32)

    xf = x.reshape(t, d)
    for l in range(depth):
        lam_init = 0.8 - 0.6 * math.exp(-0.3 * l)
        lamp = jnp.stack([lambda_q1[l], lambda_k1[l], lambda_q2[l], lambda_k2[l]]).astype(F32)
        cst = jnp.full((1, 128), lam_init, F32)
        zz, vt, wt = _proj(xf, g_pre_mix[l][None, :], w_pack[l], cs, w_vt[l], w_it[l])
        o_sb = _sb_attention(zz, vt, b, s, tri, sb_mask)
        o_df = _diff_attention(zz, vt, b, s, bt_df, lamp, cst, diff_subln_g[l][:, None])
        o_ds = _dsa_attention(zz, vt, wt, b, s, bt_ds, tril)
        o_mb = _moba_attention(zz, vt, b, s, bt_mb)
        xf = _merge(xf, o_sb, o_df, o_ds, o_mb, g_pre_mix[l][None, :], w_gate[l], w_br[l], w_o[l],
                    g_post_mix[l][None, :])
        xf = _ffn(xf, g_pre_ffn[l][None, :], w_f1[l], w_f2[l], g_post_ffn[l][None, :])
    return xf.reshape(b, s, d)
```

```python
import functools
import math

import numpy as np
import jax
import jax.numpy as jnp
from jax import lax
from jax.experimental import pallas as pl
from jax.experimental.pallas import tpu as pltpu

F32 = jnp.float32
BF16 = jnp.bfloat16

D_MODEL = 1024
HEAD_DIM = 64
N_HEADS = 4
N_MIXERS = 4
DIFF_QK_DIM = 32
IDX_HEADS = 8
DSA_TOPK_MAX = 256
MOBA_BLOCK = 256
MOBA_TOPK = 3
N_BUCKETS = 32
MAX_DISTANCE = 128
D_FF = 2816
NORM_EPS = 1e-6

TQ = 256
TK = 256
NEG = -1e30
HALF_NEG = -0.5e30
BIG = 3e38
BISECT_WARMUP = 20
BISECT_TRIP = 2
BISECT_MAX_TRIPS = 134
SB_TAIL_CUTOFF = 120.0
LOG2E = math.log2(math.e)
V_ROWS = 80
TM_MERGE = 512
TM_FFN = 512
N_GROUP = 11
N_SLAB = 4 * N_GROUP
N_PACK = N_SLAB * HEAD_DIM
VMEM_LIMIT = 56 * 1024 * 1024

G_SB_Q, G_SB_K, G_DF_Q, G_DF_K, G_DS_Q, G_DS_K, G_QI_A, G_QI_B, G_MB_Q, G_MB_K, G_KIDX = range(N_GROUP)
S_KIDX = 4 * G_KIDX
M_SB, M_DF, M_DS, M_MB = range(N_MIXERS)


def _layout():
    off = {}
    acc = 0
    for name, sz in (("q_sb", 256), ("k_sb", 256), ("v_sb", 256), ("q1", 128), ("q2", 128), ("k1", 128),
                     ("k2", 128), ("v_df", 256), ("q_ds", 256), ("k_ds", 256), ("v_ds", 256), ("qi", 512),
                     ("ki", 64), ("wi", 8), ("q_mb", 256), ("k_mb", 256), ("v_mb", 256), ("gate", 4096)):
        off[name] = acc
        acc += sz
    return off


_OFF = _layout()


def _pack_layout():
    off = _OFF
    cols, scale = [], []

    def add(start, n, s=1.0):
        cols.extend(range(start, start + n))
        scale.extend([s] * n)

    hd = HEAD_DIM ** -0.5
    hd2 = hd * LOG2E
    df2 = DIFF_QK_DIM ** -0.5 * LOG2E
    add(off["q_sb"], 256, hd); add(off["k_sb"], 256)
    for h in range(N_HEADS):
        add(off["q1"] + h * 32, 32, df2); add(off["q2"] + h * 32, 32, df2)
    for h in range(N_HEADS):
        add(off["k1"] + h * 32, 32); add(off["k2"] + h * 32, 32)
    add(off["q_ds"], 256, hd2); add(off["k_ds"], 256)
    add(off["qi"], 512, HEAD_DIM ** -0.5)
    add(off["q_mb"], 256, hd2); add(off["k_mb"], 256)
    add(off["ki"], 64)
    cols.extend([-1] * 192); scale.extend([1.0] * 192)
    assert len(cols) == N_PACK
    return np.asarray(cols, np.int32), np.asarray(scale, np.float32)


_PACK_SRC, _PACK_SCALE = _pack_layout()


def _dot(a, b):
    return jnp.dot(a, b, preferred_element_type=F32)


def _dot_nt(a, b):
    return lax.dot_general(a, b, (((1,), (1,)), ((), ())), preferred_element_type=F32)


def _rms(x, g):
    return x * lax.rsqrt(jnp.mean(x * x, axis=-1, keepdims=True) + NORM_EPS) * g


def _cparams(n_axes):
    return pltpu.CompilerParams(dimension_semantics=("arbitrary",) * n_axes, vmem_limit_bytes=VMEM_LIMIT)


def _const_spec(shape):
    nd = len(shape)
    return pl.BlockSpec(shape, lambda *_: (0,) * nd, pipeline_mode=pl.Buffered(1))


def _proj_kernel(x_ref, g_ref, w_ref, cs_ref, wvt_ref, wit_ref, zz_ref, vt_ref, wt_ref):
    h = _rms(x_ref[...], g_ref[...]).astype(BF16)
    for c in range(N_GROUP):
        r = _dot(h, w_ref[:, c * 256:(c + 1) * 256]) * cs_ref[:, c * 256:(c + 1) * 256]
        for s in range(4):
            zz_ref[4 * c + s] = r[:, s * HEAD_DIM:(s + 1) * HEAD_DIM].astype(BF16)
    for m in range(N_MIXERS):
        vt = _dot_nt(wvt_ref[m], h).astype(BF16)
        for hh in range(N_HEADS):
            vt_ref[m, 0, hh * V_ROWS:hh * V_ROWS + HEAD_DIM, :] = vt[hh * HEAD_DIM:(hh + 1) * HEAD_DIM]
            vt_ref[m, 0, hh * V_ROWS + HEAD_DIM:(hh + 1) * V_ROWS, :] = jnp.ones((V_ROWS - HEAD_DIM, TQ), BF16)
    wt_ref[...] = _dot_nt(wit_ref[...], h) * IDX_HEADS ** -0.5


def _proj(x, g, w, cs, wvt, wit):
    t = x.shape[0]
    tm = TQ
    return pl.pallas_call(
        _proj_kernel,
        grid=(t // tm,),
        in_specs=[pl.BlockSpec((tm, D_MODEL), lambda i: (i, 0)),
                  _const_spec((1, D_MODEL)),
                  _const_spec((D_MODEL, N_PACK)),
                  _const_spec((1, N_PACK)),
                  _const_spec((N_MIXERS, 256, D_MODEL)),
                  _const_spec((IDX_HEADS, D_MODEL))],
        out_specs=[pl.BlockSpec((N_SLAB, tm, HEAD_DIM), lambda i: (0, i, 0)),
                   pl.BlockSpec((N_MIXERS, 1, N_HEADS * V_ROWS, tm), lambda i: (0, i, 0, 0)),
                   pl.BlockSpec((IDX_HEADS, tm), lambda i: (0, i))],
        out_shape=[jax.ShapeDtypeStruct((N_SLAB, t, HEAD_DIM), BF16),
                   jax.ShapeDtypeStruct((N_MIXERS, t // tm, N_HEADS * V_ROWS, tm), BF16),
                   jax.ShapeDtypeStruct((IDX_HEADS, t), F32)],
        compiler_params=_cparams(1),
        name="proj",
    )(x, g, w, cs, wvt, wit)


def _k_block(ref, h, j):
    return ref[h, pl.ds(pl.multiple_of(j * TK, TK), TK), :]


def _fold_keys(a, op):
    n = a.shape[0]
    while n > 8:
        n //= 2
        a = op(a[:n], a[n:2 * n])
    return a


def _vt_block(ref, h, j, rows=V_ROWS):
    return ref[0, j, h * V_ROWS:h * V_ROWS + rows, :]


def _softmax_block(s_list, vt_list, carry):
    ms = [jnp.maximum(c[0], jnp.max(_fold_keys(s, jnp.maximum), axis=0, keepdims=True))
          for s, c in zip(s_list, carry)]
    pvs = [_dot(vt, jnp.exp2(s - m).astype(BF16)) for vt, s, m in zip(vt_list, s_list, ms)]
    return tuple((m_new, jnp.exp2(m - m_new) * acc + pv) for (m, acc), m_new, pv in zip(carry, ms, pvs))


def _softmax_loop(n_blocks, logits, values, carry):
    def pair(t, carry):
        j = 2 * t
        s_a, s_b = logits(j), logits(j + 1)
        return _softmax_block(s_b, values(j + 1), _softmax_block(s_a, values(j), carry))

    def single(j, carry):
        return _softmax_block(logits(j), values(j), carry)

    n_pairs = n_blocks // 2
    carry = lax.fori_loop(0, n_pairs, pair, carry)
    return lax.fori_loop(2 * n_pairs, n_blocks, single, carry)


def _softmax_init():
    return (jnp.full((1, TQ), NEG, F32), jnp.zeros((V_ROWS, TQ), F32))


def _softmax_out(carry):
    _, acc = carry
    return acc[:HEAD_DIM] / acc[HEAD_DIM:HEAD_DIM + 1]


def _store_heads(o_ref, heads_t):
    o_ref[...] = jnp.transpose(jnp.concatenate(heads_t, axis=0)).astype(BF16)


def _attn_specs(nq, gq, gk, mixer, s):
    return [pl.BlockSpec((4, TQ, HEAD_DIM), lambda b, i: (gq, b * nq + i, 0)),
            pl.BlockSpec((4, s, HEAD_DIM), lambda b, i: (gk, b, 0)),
            pl.BlockSpec((1, s // TK, N_HEADS * V_ROWS, TK), lambda b, i: (mixer, b, 0, 0))]


def _out_spec(nq):
    return pl.BlockSpec((TQ, N_HEADS * HEAD_DIM), lambda b, i: (b * nq + i, 0))


def _sb_kernel(q_ref, k_ref, vt_ref, tri_ref, mask_ref, o_ref):
    i = pl.program_id(1)
    tri = tri_ref[...]

    def block(j, carry, masked):
        heads = range(N_HEADS)
        zs = [_dot_nt(_k_block(k_ref, h, j), q_ref[h]) for h in heads]
        if masked:
            zs = [z + mask_ref[...] for z in zs]
        sps = [jnp.maximum(z, 0.0) + jnp.log(1.0 + jnp.exp(-jnp.abs(z))) for z in zs]
        his = [sp.astype(BF16) for sp in sps]
        los = [(sp - hi.astype(F32)).astype(BF16) for sp, hi in zip(sps, his)]
        cs = [_dot(tri, hi) + _dot(tri, lo) + c[0] for hi, lo, c in zip(his, los, carry)]
        avs = [_dot(_vt_block(vt_ref, h, j, HEAD_DIM), jnp.exp(z - c).astype(BF16))
               for h, z, c in zip(heads, zs, cs)]
        return tuple((c[0:1, :], old[1] + av) for c, old, av in zip(cs, carry, avs))

    init = tuple((jnp.zeros((1, TQ), F32), jnp.zeros((HEAD_DIM, TQ), F32)) for _ in range(N_HEADS))
    carry = block(i, init, True)

    def weights_left(carry):
        tail = functools.reduce(jnp.minimum, [c[0] for c in carry])
        return (jnp.min(tail) < SB_TAIL_CUTOFF).astype(jnp.int32)

    def earlier_block(c):
        j, carry, _ = c
        carry = block(j, carry, False)
        return j - 1, carry, weights_left(carry)

    _, carry, _ = lax.while_loop(lambda c: (c[0] >= 0) & (c[2] > 0), earlier_block, (i - 1, carry, weights_left(carry)))
    _store_heads(o_ref, [c[1] for c in carry])


def _sb_attention(zz, vt, b, s, tri, mask):
    nq = s // TQ
    return pl.pallas_call(
        _sb_kernel,
        grid=(b, nq),
        in_specs=_attn_specs(nq, G_SB_Q, G_SB_K, M_SB, s) + [_const_spec((TK, TK)), _const_spec((TK, TQ))],
        out_specs=_out_spec(nq),
        out_shape=jax.ShapeDtypeStruct((b * s, N_HEADS * HEAD_DIM), BF16),
        compiler_params=_cparams(2),
        name="sb_attn",
    )(zz, zz, vt, tri, mask)


def _diff_kernel(q_ref, k_ref, vt_ref, bt_ref, lam_ref, cst_ref, g_ref, o_ref):
    i = pl.program_id(1)
    lp = lam_ref[...]
    lam_init = cst_ref[:, 0:1]
    lam = (jnp.exp(jnp.sum(lp[0:1] * lp[1:2], axis=-1, keepdims=True))
           - jnp.exp(jnp.sum(lp[2:3] * lp[3:4], axis=-1, keepdims=True)) + lam_init)
    lane = lax.broadcasted_iota(jnp.int32, (TQ, HEAD_DIM), 1)
    qs = []
    for h in range(N_HEADS):
        q = q_ref[h]
        qs.append((jnp.where(lane < DIFF_QK_DIM, q, jnp.zeros_like(q)),
                   jnp.where(lane >= DIFF_QK_DIM, q, jnp.zeros_like(q))))

    def logits(j):
        s_list = []
        for h in range(N_HEADS):
            kj = _k_block(k_ref, h, j)
            bias = bt_ref[h, jnp.minimum(i - j, 2)]
            s_list += [_dot_nt(kj, qs[h][0]) + bias, _dot_nt(kj, qs[h][1]) + bias]
        return s_list

    def values(j):
        return [_vt_block(vt_ref, h, j) for h in range(N_HEADS) for _ in range(2)]

    carry = _softmax_loop(i + 1, logits, values, tuple(_softmax_init() for _ in range(2 * N_HEADS)))
    outs = []
    for h in range(N_HEADS):
        o = _softmax_out(carry[2 * h]) - lam * _softmax_out(carry[2 * h + 1])
        o = o * lax.rsqrt(jnp.mean(o * o, axis=0, keepdims=True) + NORM_EPS) * g_ref[...]
        outs.append(o * (1.0 - lam_init))
    _store_heads(o_ref, outs)


def _diff_attention(zz, vt, b, s, bt, lamp, cst, g):
    nq = s // TQ
    return pl.pallas_call(
        _diff_kernel,
        grid=(b, nq),
        in_specs=_attn_specs(nq, G_DF_Q, G_DF_K, M_DF, s) + [
            _const_spec((N_HEADS, 3, TK, TQ)), _const_spec((4, DIFF_QK_DIM)), _const_spec((1, 128)),
            _const_spec((HEAD_DIM, 1))],
        out_specs=_out_spec(nq),
        out_shape=jax.ShapeDtypeStruct((b * s, N_HEADS * HEAD_DIM), BF16),
        compiler_params=_cparams(2),
        name="diff_attn",
    )(zz, zz, vt, bt, lamp, cst, g)


def _dsa_kernel(q_ref, k_ref, vt_ref, qia_ref, qib_ref, ki_ref, wt_ref, bt_ref, tril_ref, o_ref, sc_ref, *, topk):
    i = pl.program_id(1)
    nb = i + 1
    kf = float(topk)
    w = wt_ref[...]
    key = lax.broadcasted_iota(jnp.int32, (TK, TQ), 0)
    qry = lax.broadcasted_iota(jnp.int32, (TK, TQ), 1)

    def score_block(j, _):
        kij = _k_block(ki_ref, 0, j)
        sc = jnp.zeros((TK, TQ), F32)
        for hh in range(IDX_HEADS):
            qi = (qia_ref if hh < 4 else qib_ref)[hh % 4]
            sc = sc + w[hh:hh + 1, :] * jnp.maximum(_dot_nt(kij, qi), 0.0)
        sc_ref[j] = sc
        return 0

    lax.fori_loop(0, nb, score_block, 0)
    sc_ref[i] = jnp.where(key <= qry, sc_ref[i], NEG)

    def reduce_blocks(fn, init):
        return lax.fori_loop(0, nb, lambda j, c: fn(sc_ref[j], j, c), init)

    def count_ge(t):
        part = reduce_blocks(lambda x, j, c: c + _fold_keys(jnp.where(x >= t, 1.0, 0.0), jnp.add),
                             jnp.zeros((8, TQ), F32))
        return jnp.sum(part, axis=0, keepdims=True)

    def minmax_blocks(lo_of, hi_of):
        def f(x, j, c):
            return (jnp.minimum(c[0], _fold_keys(lo_of(x), jnp.minimum)),
                    jnp.maximum(c[1], _fold_keys(hi_of(x), jnp.maximum)))
        lo_part, hi_part = reduce_blocks(f, (jnp.full((8, TQ), BIG, F32), jnp.full((8, TQ), -BIG, F32)))
        return jnp.min(lo_part, axis=0, keepdims=True), jnp.max(hi_part, axis=0, keepdims=True)

    n_valid = i * TQ + lax.broadcasted_iota(jnp.int32, (1, TQ), 1) + 1
    take_all = n_valid <= topk
    lo, hi = minmax_blocks(lambda x: jnp.where(x > HALF_NEG, x, BIG), lambda x: x)
    c_max = count_ge(hi)
    at_max = c_max >= kf
    state = (jnp.where(at_max, hi, lo), hi, jnp.where(at_max, c_max, n_valid.astype(F32)), c_max)

    def bisect(_, state):
        lo, hi, c_lo, c_hi = state
        mid = 0.5 * lo + 0.5 * hi
        c = count_ge(mid)
        ge = c >= kf
        return jnp.where(ge, mid, lo), jnp.where(ge, hi, mid), jnp.where(ge, c, c_lo), jnp.where(ge, c_hi, c)

    def unsettled(state):
        lo, hi, c_lo, _ = state
        open_q = jnp.where(take_all, 0.0, jnp.where(c_lo != kf, jnp.where(lo < hi, 1.0, 0.0), 0.0))

        def band_spread():
            b_min, b_max = minmax_blocks(lambda x: jnp.where(x >= lo, jnp.where(x < hi, x, BIG), BIG),
                                         lambda x: jnp.where(x >= lo, jnp.where(x < hi, x, -BIG), -BIG))
            return (jnp.max(jnp.where(b_max != b_min, open_q, 0.0)) > 0.0).astype(jnp.int32)

        return lax.cond(jnp.max(open_q) > 0.0, band_spread, lambda: jnp.int32(0))

    state = lax.fori_loop(0, jnp.where((i + 1) * TQ <= topk, 0, BISECT_WARMUP), bisect, state)

    def trip(c):
        n, state, _ = c
        state = lax.fori_loop(0, BISECT_TRIP, bisect, state)
        return n + 1, state, unsettled(state)

    _, state, _ = lax.while_loop(lambda c: (c[2] > 0) & (c[0] < BISECT_MAX_TRIPS), trip,
                                 (jnp.int32(0), state, unsettled(state)))
    lo, hi, c_lo, c_hi = state
    hi_ok = lo < hi
    c_above = jnp.where(hi_ok, c_hi, 0.0)
    hi_sel = jnp.where(hi_ok, hi, BIG)
    need = jnp.where(take_all, BIG, kf - c_above)
    lo_sel = jnp.where(take_all, HALF_NEG, lo)

    tied = jnp.max(jnp.where(take_all, 0.0, c_lo - kf)) > 0.0

    @pl.when(tied)
    def _():
        tril = tril_ref[...]

        def write_mask(j, taken):
            x = sc_ref[j]
            band = jnp.where(x >= lo_sel, jnp.where(x < hi_sel, 1.0, 0.0), 0.0)
            rank = _dot(tril, band.astype(BF16)) + taken
            sc_ref[j] = jnp.where(x >= hi_sel, 0.0,
                                  jnp.where(band * rank > 0.0, jnp.where(rank <= need, 0.0, NEG), NEG))
            return rank[TK - 1:TK, :]

        lax.fori_loop(0, nb, write_mask, jnp.zeros((1, TQ), F32))

    @pl.when(jnp.logical_not(tied))
    def _():
        def write_mask(j, _):
            sc_ref[j] = jnp.where(sc_ref[j] >= lo_sel, 0.0, NEG)
            return 0

        lax.fori_loop(0, nb, write_mask, 0)

    def logits(j):
        return [_dot_nt(_k_block(k_ref, h, j), q_ref[h]) + bt_ref[h, jnp.minimum(i - j, 2)] + sc_ref[j]
                for h in range(N_HEADS)]

    carry = _softmax_loop(nb, logits, lambda j: [_vt_block(vt_ref, h, j) for h in range(N_HEADS)],
                          tuple(_softmax_init() for _ in range(N_HEADS)))
    _store_heads(o_ref, [_softmax_out(c) for c in carry])


def _dsa_attention(zz, vt, wt, b, s, bt, tril):
    nq = s // TQ
    topk = min(DSA_TOPK_MAX, s // 4)
    return pl.pallas_call(
        functools.partial(_dsa_kernel, topk=topk),
        grid=(b, nq),
        in_specs=_attn_specs(nq, G_DS_Q, G_DS_K, M_DS, s) + [
            pl.BlockSpec((4, TQ, HEAD_DIM), lambda b_, i: (G_QI_A, b_ * nq + i, 0)),
            pl.BlockSpec((4, TQ, HEAD_DIM), lambda b_, i: (G_QI_B, b_ * nq + i, 0)),
            pl.BlockSpec((1, s, HEAD_DIM), lambda b_, i: (S_KIDX, b_, 0)),
            pl.BlockSpec((IDX_HEADS, TQ), lambda b_, i: (0, b_ * nq + i)),
            _const_spec((N_HEADS, 3, TK, TQ)), _const_spec((TK, TK))],
        out_specs=_out_spec(nq),
        out_shape=jax.ShapeDtypeStruct((b * s, N_HEADS * HEAD_DIM), BF16),
        scratch_shapes=[pltpu.VMEM((nq, TK, TQ), F32)],
        compiler_params=_cparams(2),
        name="dsa_attn",
    )(zz, zz, vt, zz, zz, zz, wt, bt, tril)


def _moba_kernel(q_ref, k_ref, vt_ref, bt_ref, o_ref, km_ref, *, nblk, topb):
    i = pl.program_id(1)
    nrow = km_ref.shape[1]

    @pl.when(i == 0)
    def _():
        km_ref[...] = jnp.zeros_like(km_ref)
        for h in range(N_HEADS):
            for n in range(nblk):
                kb = k_ref[h, n * MOBA_BLOCK:(n + 1) * MOBA_BLOCK, :].astype(F32)
                km_ref[h, n:n + 1, :] = jnp.mean(kb, axis=0, keepdims=True)

    blk = lax.broadcasted_iota(jnp.int32, (nrow, TQ), 0)
    past = blk < i
    head_bits = []
    for h in range(N_HEADS):
        gate = _dot_nt(km_ref[h].astype(BF16), q_ref[h])
        bits = jnp.zeros((1, TQ), F32)
        for n in range(nblk):
            gn = gate[n:n + 1, :]
            beats = jnp.where(past, jnp.where(gate > gn, 1.0, jnp.where(gate == gn, jnp.where(blk < n, 1.0, 0.0), 0.0)), 0.0)
            rank = jnp.sum(beats, axis=0, keepdims=True)
            bits = bits + jnp.where(rank < float(topb), jnp.where(n < i, float(2 ** n), 0.0), 0.0)
        head_bits.append(bits.astype(jnp.int32))

    def values(n):
        return [_vt_block(vt_ref, h, n) for h in range(N_HEADS)]

    own = _softmax_block([_dot_nt(_k_block(k_ref, h, i), q_ref[h]) + bt_ref[h, 0] for h in range(N_HEADS)],
                         values(i), tuple(_softmax_init() for _ in range(N_HEADS)))

    def logits(n):
        s_list = []
        for h in range(N_HEADS):
            picked = (lax.shift_right_logical(head_bits[h], jnp.full_like(head_bits[h], n)) & 1) == 1
            s_list.append(_dot_nt(_k_block(k_ref, h, n), q_ref[h]) + bt_ref[h, jnp.minimum(i - n, 2)]
                          + jnp.where(picked, 0.0, NEG))
        return s_list

    carry = _softmax_loop(i, logits, values, own)
    _store_heads(o_ref, [_softmax_out(c) for c in carry])


def _moba_attention(zz, vt, b, s, bt):
    nq = s // TQ
    nblk = s // MOBA_BLOCK
    topb = min(MOBA_TOPK, nblk - 1)
    return pl.pallas_call(
        functools.partial(_moba_kernel, nblk=nblk, topb=topb),
        grid=(b, nq),
        in_specs=_attn_specs(nq, G_MB_Q, G_MB_K, M_MB, s) + [_const_spec((N_HEADS, 3, TK, TQ))],
        out_specs=_out_spec(nq),
        out_shape=jax.ShapeDtypeStruct((b * s, N_HEADS * HEAD_DIM), BF16),
        scratch_shapes=[pltpu.VMEM((N_HEADS, max(8, nblk), HEAD_DIM), F32)],
        compiler_params=_cparams(2),
        name="moba_attn",
    )(zz, zz, vt, bt)


def _merge_kernel(x_ref, osb_ref, odf_ref, ods_ref, omb_ref, gpre_ref, wg_ref, wbr_ref, wout_ref, gpost_ref, o_ref):
    x = x_ref[...]
    h = _rms(x, gpre_ref[...]).astype(BF16)
    y = jnp.zeros((x.shape[0], D_MODEL), F32)
    for r, o_r in enumerate((osb_ref, odf_ref, ods_ref, omb_ref)):
        gate = jax.nn.sigmoid(_dot(h, wg_ref[:, r * D_MODEL:(r + 1) * D_MODEL]))
        y = y + gate * _dot(o_r[...], wbr_ref[r])
    o_ref[...] = x + _rms(_dot(y.astype(BF16), wout_ref[...]), gpost_ref[...])


def _merge(x, o_sb, o_df, o_ds, o_mb, g_pre, w_gate, w_br, w_out, g_post):
    t = x.shape[0]
    tm = TM_MERGE
    tok = lambda width: pl.BlockSpec((tm, width), lambda i: (i, 0))
    return pl.pallas_call(
        _merge_kernel,
        grid=(t // tm,),
        in_specs=[tok(D_MODEL), tok(256), tok(256), tok(256), tok(256),
                  _const_spec((1, D_MODEL)), _const_spec((D_MODEL, 4 * D_MODEL)),
                  _const_spec((4, 256, D_MODEL)), _const_spec((D_MODEL, D_MODEL)), _const_spec((1, D_MODEL))],
        out_specs=tok(D_MODEL),
        out_shape=jax.ShapeDtypeStruct((t, D_MODEL), F32),
        compiler_params=_cparams(1),
        name="merge",
    )(x, o_sb, o_df, o_ds, o_mb, g_pre, w_gate, w_br, w_out, g_post)


def _ffn_kernel(x_ref, gpre_ref, win_ref, wout_ref, gpost_ref, o_ref):
    x = x_ref[...]
    h = _rms(x, gpre_ref[...]).astype(BF16)
    gate = _dot(h, win_ref[:, 0:D_FF])
    up = _dot(h, win_ref[:, D_FF:2 * D_FF])
    act = (gate * jax.nn.sigmoid(gate) * up).astype(BF16)
    o_ref[...] = x + _rms(_dot(act, wout_ref[...]), gpost_ref[...])


def _ffn(x, g_pre, w_in, w_out, g_post):
    t = x.shape[0]
    tm = TM_FFN
    return pl.pallas_call(
        _ffn_kernel,
        grid=(t // tm,),
        in_specs=[pl.BlockSpec((tm, D_MODEL), lambda i: (i, 0)), _const_spec((1, D_MODEL)),
                  _const_spec((D_MODEL, 2 * D_FF)), _const_spec((D_FF, D_MODEL)), _const_spec((1, D_MODEL))],
        out_specs=pl.BlockSpec((tm, D_MODEL), lambda i: (i, 0)),
        out_shape=jax.ShapeDtypeStruct((t, D_MODEL), F32),
        compiler_params=_cparams(1),
        name="ffn",
    )(x, g_pre, w_in, w_out, g_post)


def _t5_bucket(dist):
    max_exact = N_BUCKETS // 2
    d = jnp.maximum(dist, 0)
    log_ratio = jnp.log(jnp.maximum(d, 1).astype(F32) / max_exact) / math.log(MAX_DISTANCE / max_exact)
    large = jnp.minimum(max_exact + (log_ratio * (N_BUCKETS - max_exact)).astype(jnp.int32), N_BUCKETS - 1)
    return jnp.where(d < max_exact, d, large)


def _bias_tiles(rel_bias):
    assert TQ == TK
    n = TK
    nh = rel_bias.shape[1]
    d = np.arange(-(n - 1), 3 * n)
    by_dist = rel_bias.astype(F32).T[:, _t5_bucket(jnp.asarray(np.maximum(d, 0), jnp.int32))]
    by_dist = jnp.where(jnp.asarray(d >= 0)[None, :], by_dist * LOG2E, NEG)
    tiles = []
    for o in range(3):
        w = by_dist[:, o * n:o * n + 2 * n - 1]
        w = jnp.concatenate([w, jnp.zeros((nh, 1), F32)], axis=1)
        flat = jnp.broadcast_to(w[:, None, :], (nh, n, 2 * n)).reshape(nh, 2 * n * n)
        tiles.append(flat[:, :n * (2 * n - 1)].reshape(nh, n, 2 * n - 1)[:, :, n - 1:])
    return jnp.stack(tiles, axis=1)


def _pack_weights(w_in):
    parts, start = [], 0
    for e in range(1, N_PACK + 1):
        if e == N_PACK or _PACK_SRC[e] != _PACK_SRC[e - 1] + (1 if _PACK_SRC[e - 1] >= 0 else 0):
            a = int(_PACK_SRC[start])
            n = e - start
            parts.append(w_in[:, a:a + n] if a >= 0 else jnp.zeros((w_in.shape[0], n), w_in.dtype))
            start = e
    return jnp.concatenate(parts, axis=1).astype(BF16)


def _transposed_weights(w_in):
    wv = jnp.stack([w_in[:, _OFF[n]:_OFF[n] + 256].T for n in ("v_sb", "v_df", "v_ds", "v_mb")])
    wi = w_in[:, _OFF["wi"]:_OFF["wi"] + IDX_HEADS].T
    return wv.astype(BF16), wi.astype(BF16)


def kernel(x, w_in, w_br_sb, w_br_diff, w_br_dsa, w_br_moba, w_out, lambda_q1, lambda_k1, lambda_q2, lambda_k2,
           diff_subln_g, rel_bias, w_ffn_in, w_ffn_out, g_pre_mix, g_post_mix, g_pre_ffn, g_post_ffn):
    b, s, d = x.shape
    depth = w_in.shape[0]
    assert d == D_MODEL and s % TQ == 0 and s // MOBA_BLOCK >= 2
    t = b * s

    cs = jnp.asarray(_PACK_SCALE)[None, :]

    bt = _bias_tiles(rel_bias)
    bt_df, bt_ds, bt_mb = bt[0:4], bt[4:8], bt[8:12]
    key = np.arange(TK)[:, None]
    qry = np.arange(TQ)[None, :]
    tri = jnp.asarray(key <= np.arange(TK)[None, :], BF16)
    tril = jnp.asarray(key >= np.arange(TK)[None, :], BF16)
    sb_mask = jnp.asarray(np.where(key < qry, 0.0, NEG), F32)

    xf = x.reshape(t, d)
    for l in range(depth):
        lam_init = 0.8 - 0.6 * math.exp(-0.3 * l)
        lamp = jnp.stack([lambda_q1[l], lambda_k1[l], lambda_q2[l], lambda_k2[l]]).astype(F32)
        cst = jnp.full((1, 128), lam_init, F32)
        w_l = w_in[l]
        w_vt, w_it = _transposed_weights(w_l)
        w_br = jnp.stack([w_br_sb[l], w_br_diff[l], w_br_dsa[l], w_br_moba[l]]).astype(BF16)
        zz, vt, wt = _proj(xf, g_pre_mix[l][None, :], _pack_weights(w_l), cs, w_vt, w_it)
        o_sb = _sb_attention(zz, vt, b, s, tri, sb_mask)
        o_df = _diff_attention(zz, vt, b, s, bt_df, lamp, cst, diff_subln_g[l][:, None])
        o_ds = _dsa_attention(zz, vt, wt, b, s, bt_ds, tril)
        o_mb = _moba_attention(zz, vt, b, s, bt_mb)
        xf = _merge(xf, o_sb, o_df, o_ds, o_mb, g_pre_mix[l][None, :], w_l[:, _OFF["gate"]:].astype(BF16), w_br,
                    w_out[l].astype(BF16), g_post_mix[l][None, :])
        xf = _ffn(xf, g_pre_ffn[l][None, :], w_ffn_in[l].astype(BF16), w_ffn_out[l].astype(BF16),
                  g_post_ffn[l][None, :])
    return xf.reshape(b, s, d)
```

```python
import functools
import math

import numpy as np
import jax
import jax.numpy as jnp
from jax import lax
from jax.experimental import pallas as pl
from jax.experimental.pallas import tpu as pltpu

F32 = jnp.float32
BF16 = jnp.bfloat16

D_MODEL = 1024
HEAD_DIM = 64
N_HEADS = 4
N_MIXERS = 4
DIFF_QK_DIM = 32
IDX_HEADS = 8
DSA_TOPK_MAX = 256
MOBA_BLOCK = 256
MOBA_TOPK = 3
N_BUCKETS = 32
MAX_DISTANCE = 128
D_FF = 2816
NORM_EPS = 1e-6

TQ = 256
TK = 256
NEG = -1e30
HALF_NEG = -0.5e30
BIG = 3e38
BISECT_WARMUP = 20
BISECT_TRIP = 2
BISECT_MAX_TRIPS = 134
SB_TAIL_CUTOFF = 120.0
LOG2E = math.log2(math.e)
V_ROWS = 80
TM_MERGE = 512
TM_FFN = 512
N_GROUP = 11
N_SLAB = 4 * N_GROUP
N_PACK = N_SLAB * HEAD_DIM
VMEM_LIMIT = 56 * 1024 * 1024

G_SB_Q, G_SB_K, G_DF_Q, G_DF_K, G_DS_Q, G_DS_K, G_QI_A, G_QI_B, G_MB_Q, G_MB_K, G_KIDX = range(N_GROUP)
S_KIDX = 4 * G_KIDX
M_SB, M_DF, M_DS, M_MB = range(N_MIXERS)


def _layout():
    off = {}
    acc = 0
    for name, sz in (("q_sb", 256), ("k_sb", 256), ("v_sb", 256), ("q1", 128), ("q2", 128), ("k1", 128),
                     ("k2", 128), ("v_df", 256), ("q_ds", 256), ("k_ds", 256), ("v_ds", 256), ("qi", 512),
                     ("ki", 64), ("wi", 8), ("q_mb", 256), ("k_mb", 256), ("v_mb", 256), ("gate", 4096)):
        off[name] = acc
        acc += sz
    return off


_OFF = _layout()


def _pack_layout():
    off = _OFF
    cols, scale = [], []

    def add(start, n, s=1.0):
        cols.extend(range(start, start + n))
        scale.extend([s] * n)

    hd = HEAD_DIM ** -0.5
    hd2 = hd * LOG2E
    df2 = DIFF_QK_DIM ** -0.5 * LOG2E
    add(off["q_sb"], 256, hd); add(off["k_sb"], 256)
    for h in range(N_HEADS):
        add(off["q1"] + h * 32, 32, df2); add(off["q2"] + h * 32, 32, df2)
    for h in range(N_HEADS):
        add(off["k1"] + h * 32, 32); add(off["k2"] + h * 32, 32)
    add(off["q_ds"], 256, hd2); add(off["k_ds"], 256)
    add(off["qi"], 512, HEAD_DIM ** -0.5)
    add(off["q_mb"], 256, hd2); add(off["k_mb"], 256)
    add(off["ki"], 64)
    cols.extend([-1] * 192); scale.extend([1.0] * 192)
    assert len(cols) == N_PACK
    return np.asarray(cols, np.int32), np.asarray(scale, np.float32)


_PACK_SRC, _PACK_SCALE = _pack_layout()


def _dot(a, b):
    return jnp.dot(a, b, preferred_element_type=F32)


def _dot_nt(a, b):
    return lax.dot_general(a, b, (((1,), (1,)), ((), ())), preferred_element_type=F32)


def _rms(x, g):
    return x * lax.rsqrt(jnp.mean(x * x, axis=-1, keepdims=True) + NORM_EPS) * g


def _cparams(n_axes):
    return pltpu.CompilerParams(dimension_semantics=("arbitrary",) * n_axes, vmem_limit_bytes=VMEM_LIMIT)


def _const_spec(shape):
    nd = len(shape)
    return pl.BlockSpec(shape, lambda *_: (0,) * nd, pipeline_mode=pl.Buffered(1))


def _proj_kernel(x_ref, g_ref, w_ref, cs_ref, wvt_ref, wit_ref, zz_ref, vt_ref, wt_ref):
    h = _rms(x_ref[...], g_ref[...]).astype(BF16)
    for c in range(N_GROUP):
        r = _dot(h, w_ref[:, c * 256:(c + 1) * 256]) * cs_ref[:, c * 256:(c + 1) * 256]
        for s in range(4):
            zz_ref[4 * c + s] = r[:, s * HEAD_DIM:(s + 1) * HEAD_DIM].astype(BF16)
    for m in range(N_MIXERS):
        vt = _dot_nt(wvt_ref[m], h).astype(BF16)
        for hh in range(N_HEADS):
            vt_ref[m, 0, hh * V_ROWS:hh * V_ROWS + HEAD_DIM, :] = vt[hh * HEAD_DIM:(hh + 1) * HEAD_DIM]
            vt_ref[m, 0, hh * V_ROWS + HEAD_DIM:(hh + 1) * V_ROWS, :] = jnp.ones((V_ROWS - HEAD_DIM, TQ), BF16)
    wt_ref[...] = _dot_nt(wit_ref[...], h) * IDX_HEADS ** -0.5


def _proj(x, g, w, cs, wvt, wit):
    t = x.shape[0]
    tm = TQ
    return pl.pallas_call(
        _proj_kernel,
        grid=(t // tm,),
        in_specs=[pl.BlockSpec((tm, D_MODEL), lambda i: (i, 0)),
                  _const_spec((1, D_MODEL)),
                  _const_spec((D_MODEL, N_PACK)),
                  _const_spec((1, N_PACK)),
                  _const_spec((N_MIXERS, 256, D_MODEL)),
                  _const_spec((IDX_HEADS, D_MODEL))],
        out_specs=[pl.BlockSpec((N_SLAB, tm, HEAD_DIM), lambda i: (0, i, 0)),
                   pl.BlockSpec((N_MIXERS, 1, N_HEADS * V_ROWS, tm), lambda i: (0, i, 0, 0)),
                   pl.BlockSpec((IDX_HEADS, tm), lambda i: (0, i))],
        out_shape=[jax.ShapeDtypeStruct((N_SLAB, t, HEAD_DIM), BF16),
                   jax.ShapeDtypeStruct((N_MIXERS, t // tm, N_HEADS * V_ROWS, tm), BF16),
                   jax.ShapeDtypeStruct((IDX_HEADS, t), F32)],
        compiler_params=_cparams(1),
        name="proj",
    )(x, g, w, cs, wvt, wit)


def _k_block(ref, h, j):
    return ref[h, pl.ds(pl.multiple_of(j * TK, TK), TK), :]


def _fold_keys(a, op):
    n = a.shape[0]
    while n > 8:
        n //= 2
        a = op(a[:n], a[n:2 * n])
    return a


def _vt_block(ref, h, j, rows=V_ROWS):
    return ref[0, j, h * V_ROWS:h * V_ROWS + rows, :]


def _softmax_block(s_list, vt_list, carry):
    ms = [jnp.maximum(c[0], jnp.max(_fold_keys(s, jnp.maximum), axis=0, keepdims=True))
          for s, c in zip(s_list, carry)]
    pvs = [_dot(vt, jnp.exp2(s - m).astype(BF16)) for vt, s, m in zip(vt_list, s_list, ms)]
    return tuple((m_new, jnp.exp2(m - m_new) * acc + pv) for (m, acc), m_new, pv in zip(carry, ms, pvs))


def _softmax_loop(n_blocks, logits, values, carry):
    def pair(t, carry):
        j = 2 * t
        s_a, s_b = logits(j), logits(j + 1)
        return _softmax_block(s_b, values(j + 1), _softmax_block(s_a, values(j), carry))

    def single(j, carry):
        return _softmax_block(logits(j), values(j), carry)

    n_pairs = n_blocks // 2
    carry = lax.fori_loop(0, n_pairs, pair, carry)
    return lax.fori_loop(2 * n_pairs, n_blocks, single, carry)


def _softmax_init():
    return (jnp.full((1, TQ), NEG, F32), jnp.zeros((V_ROWS, TQ), F32))


def _softmax_out(carry):
    _, acc = carry
    return acc[:HEAD_DIM] / acc[HEAD_DIM:HEAD_DIM + 1]


def _store_heads(o_ref, heads_t):
    o_ref[...] = jnp.transpose(jnp.concatenate(heads_t, axis=0)).astype(BF16)


def _attn_specs(nq, gq, gk, mixer, s):
    return [pl.BlockSpec((4, TQ, HEAD_DIM), lambda b, i: (gq, b * nq + i, 0)),
            pl.BlockSpec((4, s, HEAD_DIM), lambda b, i: (gk, b, 0)),
            pl.BlockSpec((1, s // TK, N_HEADS * V_ROWS, TK), lambda b, i: (mixer, b, 0, 0))]


def _out_spec(nq):
    return pl.BlockSpec((TQ, N_HEADS * HEAD_DIM), lambda b, i: (b * nq + i, 0))


def _sb_kernel(q_ref, k_ref, vt_ref, tri_ref, mask_ref, o_ref):
    i = pl.program_id(1)
    tri = tri_ref[...]

    def block(j, carry, masked):
        heads = range(N_HEADS)
        zs = [_dot_nt(_k_block(k_ref, h, j), q_ref[h]) for h in heads]
        if masked:
            zs = [z + mask_ref[...] for z in zs]
        sps = [jnp.maximum(z, 0.0) + jnp.log(1.0 + jnp.exp(-jnp.abs(z))) for z in zs]
        his = [sp.astype(BF16) for sp in sps]
        los = [(sp - hi.astype(F32)).astype(BF16) for sp, hi in zip(sps, his)]
        cs = [_dot(tri, hi) + _dot(tri, lo) + c[0] for hi, lo, c in zip(his, los, carry)]
        avs = [_dot(_vt_block(vt_ref, h, j, HEAD_DIM), jnp.exp(z - c).astype(BF16))
               for h, z, c in zip(heads, zs, cs)]
        return tuple((c[0:1, :], old[1] + av) for c, old, av in zip(cs, carry, avs))

    init = tuple((jnp.zeros((1, TQ), F32), jnp.zeros((HEAD_DIM, TQ), F32)) for _ in range(N_HEADS))
    carry = block(i, init, True)

    def weights_left(carry):
        tail = functools.reduce(jnp.minimum, [c[0] for c in carry])
        return (jnp.min(tail) < SB_TAIL_CUTOFF).astype(jnp.int32)

    def earlier_block(c):
        j, carry, _ = c
        carry = block(j, carry, False)
        return j - 1, carry, weights_left(carry)

    _, carry, _ = lax.while_loop(lambda c: (c[0] >= 0) & (c[2] > 0), earlier_block, (i - 1, carry, weights_left(carry)))
    _store_heads(o_ref, [c[1] for c in carry])


def _sb_attention(zz, vt, b, s, tri, mask):
    nq = s // TQ
    return pl.pallas_call(
        _sb_kernel,
        grid=(b, nq),
        in_specs=_attn_specs(nq, G_SB_Q, G_SB_K, M_SB, s) + [_const_spec((TK, TK)), _const_spec((TK, TQ))],
        out_specs=_out_spec(nq),
        out_shape=jax.ShapeDtypeStruct((b * s, N_HEADS * HEAD_DIM), BF16),
        compiler_params=_cparams(2),
        name="sb_attn",
    )(zz, zz, vt, tri, mask)


def _diff_kernel(q_ref, k_ref, vt_ref, bt_ref, lam_ref, cst_ref, g_ref, o_ref):
    i = pl.program_id(1)
    lp = lam_ref[...]
    lam_init = cst_ref[:, 0:1]
    lam = (jnp.exp(jnp.sum(lp[0:1] * lp[1:2], axis=-1, keepdims=True))
           - jnp.exp(jnp.sum(lp[2:3] * lp[3:4], axis=-1, keepdims=True)) + lam_init)
    lane = lax.broadcasted_iota(jnp.int32, (TQ, HEAD_DIM), 1)
    qs = []
    for h in range(N_HEADS):
        q = q_ref[h]
        qs.append((jnp.where(lane < DIFF_QK_DIM, q, jnp.zeros_like(q)),
                   jnp.where(lane >= DIFF_QK_DIM, q, jnp.zeros_like(q))))

    def logits(j):
        s_list = []
        for h in range(N_HEADS):
            kj = _k_block(k_ref, h, j)
            bias = bt_ref[h, jnp.minimum(i - j, 2)]
            s_list += [_dot_nt(kj, qs[h][0]) + bias, _dot_nt(kj, qs[h][1]) + bias]
        return s_list

    def values(j):
        return [_vt_block(vt_ref, h, j) for h in range(N_HEADS) for _ in range(2)]

    carry = _softmax_loop(i + 1, logits, values, tuple(_softmax_init() for _ in range(2 * N_HEADS)))
    outs = []
    for h in range(N_HEADS):
        o = _softmax_out(carry[2 * h]) - lam * _softmax_out(carry[2 * h + 1])
        o = o * lax.rsqrt(jnp.mean(o * o, axis=0, keepdims=True) + NORM_EPS) * g_ref[...]
        outs.append(o * (1.0 - lam_init))
    _store_heads(o_ref, outs)


def _diff_attention(zz, vt, b, s, bt, lamp, cst, g):
    nq = s // TQ
    return pl.pallas_call(
        _diff_kernel,
        grid=(b, nq),
        in_specs=_attn_specs(nq, G_DF_Q, G_DF_K, M_DF, s) + [
            _const_spec((N_HEADS, 3, TK, TQ)), _const_spec((4, DIFF_QK_DIM)), _const_spec((1, 128)),
            _const_spec((HEAD_DIM, 1))],
        out_specs=_out_spec(nq),
        out_shape=jax.ShapeDtypeStruct((b * s, N_HEADS * HEAD_DIM), BF16),
        compiler_params=_cparams(2),
        name="diff_attn",
    )(zz, zz, vt, bt, lamp, cst, g)


def _dsa_kernel(q_ref, k_ref, vt_ref, qia_ref, qib_ref, ki_ref, wt_ref, bt_ref, tril_ref, o_ref, sc_ref, *, topk):
    i = pl.program_id(1)
    nb = i + 1
    kf = float(topk)
    w = wt_ref[...]
    key = lax.broadcasted_iota(jnp.int32, (TK, TQ), 0)
    qry = lax.broadcasted_iota(jnp.int32, (TK, TQ), 1)

    def score(j):
        kij = _k_block(ki_ref, 0, j)
        sc = jnp.zeros((TK, TQ), F32)
        for hh in range(IDX_HEADS):
            qi = (qia_ref if hh < 4 else qib_ref)[hh % 4]
            sc = sc + w[hh:hh + 1, :] * jnp.maximum(_dot_nt(kij, qi), 0.0)
        return sc

    def extend(lo_src, hi_src, c):
        return (jnp.minimum(c[0], _fold_keys(lo_src, jnp.minimum)), jnp.maximum(c[1], _fold_keys(hi_src, jnp.maximum)))

    def earlier_block(j, c):
        sc = score(j)
        sc_ref[j] = sc
        return extend(sc, sc, c)

    def earlier_pair(t, c):
        sc_a, sc_b = score(2 * t), score(2 * t + 1)
        sc_ref[2 * t] = sc_a
        sc_ref[2 * t + 1] = sc_b
        return extend(sc_b, sc_b, extend(sc_a, sc_a, c))

    lo_part, hi_part = lax.fori_loop(0, i // 2, earlier_pair,
                                     (jnp.full((8, TQ), BIG, F32), jnp.full((8, TQ), -BIG, F32)))
    lo_part, hi_part = lax.fori_loop(2 * (i // 2), i, earlier_block, (lo_part, hi_part))
    sc = score(i)
    causal = key <= qry
    sc_ref[i] = jnp.where(causal, sc, NEG)
    lo_part, hi_part = extend(jnp.where(causal, sc, BIG), jnp.where(causal, sc, NEG), (lo_part, hi_part))

    def reduce_blocks(fn, init):
        return lax.fori_loop(0, nb, lambda j, c: fn(sc_ref[j], j, c), init)

    def count_ge(t):
        part = reduce_blocks(lambda x, j, c: c + _fold_keys(jnp.where(x >= t, 1.0, 0.0), jnp.add),
                             jnp.zeros((8, TQ), F32))
        return jnp.sum(part, axis=0, keepdims=True)

    def minmax_blocks(lo_of, hi_of):
        def f(x, j, c):
            return (jnp.minimum(c[0], _fold_keys(lo_of(x), jnp.minimum)),
                    jnp.maximum(c[1], _fold_keys(hi_of(x), jnp.maximum)))
        lo_part, hi_part = reduce_blocks(f, (jnp.full((8, TQ), BIG, F32), jnp.full((8, TQ), -BIG, F32)))
        return jnp.min(lo_part, axis=0, keepdims=True), jnp.max(hi_part, axis=0, keepdims=True)

    n_valid = i * TQ + lax.broadcasted_iota(jnp.int32, (1, TQ), 1) + 1
    take_all = n_valid <= topk
    lo = jnp.min(lo_part, axis=0, keepdims=True)
    hi = jnp.max(hi_part, axis=0, keepdims=True)
    c_max = count_ge(hi)
    at_max = c_max >= kf
    state = (jnp.where(at_max, hi, lo), hi, jnp.where(at_max, c_max, n_valid.astype(F32)), c_max)

    def bisect(_, state):
        lo, hi, c_lo, c_hi = state
        mid = 0.5 * lo + 0.5 * hi
        c = count_ge(mid)
        ge = c >= kf
        return jnp.where(ge, mid, lo), jnp.where(ge, hi, mid), jnp.where(ge, c, c_lo), jnp.where(ge, c_hi, c)

    def unsettled(state):
        lo, hi, c_lo, _ = state
        open_q = jnp.where(take_all, 0.0, jnp.where(c_lo != kf, jnp.where(lo < hi, 1.0, 0.0), 0.0))

        def band_spread():
            b_min, b_max = minmax_blocks(lambda x: jnp.where(x >= lo, jnp.where(x < hi, x, BIG), BIG),
                                         lambda x: jnp.where(x >= lo, jnp.where(x < hi, x, -BIG), -BIG))
            return (jnp.max(jnp.where(b_max != b_min, open_q, 0.0)) > 0.0).astype(jnp.int32)

        return lax.cond(jnp.max(open_q) > 0.0, band_spread, lambda: jnp.int32(0))

    state = lax.fori_loop(0, jnp.where((i + 1) * TQ <= topk, 0, BISECT_WARMUP), bisect, state)

    def trip(c):
        n, state, _ = c
        state = lax.fori_loop(0, BISECT_TRIP, bisect, state)
        return n + 1, state, unsettled(state)

    _, state, _ = lax.while_loop(lambda c: (c[2] > 0) & (c[0] < BISECT_MAX_TRIPS), trip,
                                 (jnp.int32(0), state, unsettled(state)))
    lo, hi, c_lo, c_hi = state
    hi_ok = lo < hi
    c_above = jnp.where(hi_ok, c_hi, 0.0)
    hi_sel = jnp.where(hi_ok, hi, BIG)
    need = jnp.where(take_all, BIG, kf - c_above)
    lo_sel = jnp.where(take_all, HALF_NEG, lo)

    tied = jnp.max(jnp.where(take_all, 0.0, c_lo - kf)) > 0.0

    @pl.when(tied)
    def _():
        tril = tril_ref[...]

        def write_mask(j, taken):
            x = sc_ref[j]
            band = jnp.where(x >= lo_sel, jnp.where(x < hi_sel, 1.0, 0.0), 0.0)
            rank = _dot(tril, band.astype(BF16)) + taken
            sc_ref[j] = jnp.where(x >= hi_sel, 0.0,
                                  jnp.where(band * rank > 0.0, jnp.where(rank <= need, 0.0, NEG), NEG))
            return rank[TK - 1:TK, :]

        lax.fori_loop(0, nb, write_mask, jnp.zeros((1, TQ), F32))

    @pl.when(jnp.logical_not(tied))
    def _():
        def write_mask(j, _):
            sc_ref[j] = jnp.where(sc_ref[j] >= lo_sel, 0.0, NEG)
            return 0

        lax.fori_loop(0, nb, write_mask, 0)

    def logits(j):
        return [_dot_nt(_k_block(k_ref, h, j), q_ref[h]) + bt_ref[h, jnp.minimum(i - j, 2)] + sc_ref[j]
                for h in range(N_HEADS)]

    carry = _softmax_loop(nb, logits, lambda j: [_vt_block(vt_ref, h, j) for h in range(N_HEADS)],
                          tuple(_softmax_init() for _ in range(N_HEADS)))
    _store_heads(o_ref, [_softmax_out(c) for c in carry])


def _dsa_attention(zz, vt, wt, b, s, bt, tril):
    nq = s // TQ
    topk = min(DSA_TOPK_MAX, s // 4)
    return pl.pallas_call(
        functools.partial(_dsa_kernel, topk=topk),
        grid=(b, nq),
        in_specs=_attn_specs(nq, G_DS_Q, G_DS_K, M_DS, s) + [
            pl.BlockSpec((4, TQ, HEAD_DIM), lambda b_, i: (G_QI_A, b_ * nq + i, 0)),
            pl.BlockSpec((4, TQ, HEAD_DIM), lambda b_, i: (G_QI_B, b_ * nq + i, 0)),
            pl.BlockSpec((1, s, HEAD_DIM), lambda b_, i: (S_KIDX, b_, 0)),
            pl.BlockSpec((IDX_HEADS, TQ), lambda b_, i: (0, b_ * nq + i)),
            _const_spec((N_HEADS, 3, TK, TQ)), _const_spec((TK, TK))],
        out_specs=_out_spec(nq),
        out_shape=jax.ShapeDtypeStruct((b * s, N_HEADS * HEAD_DIM), BF16),
        scratch_shapes=[pltpu.VMEM((nq, TK, TQ), F32)],
        compiler_params=_cparams(2),
        name="dsa_attn",
    )(zz, zz, vt, zz, zz, zz, wt, bt, tril)


def _moba_kernel(q_ref, k_ref, vt_ref, bt_ref, o_ref, km_ref, *, nblk, topb):
    i = pl.program_id(1)
    nrow = km_ref.shape[1]

    @pl.when(i == 0)
    def _():
        km_ref[...] = jnp.zeros_like(km_ref)
        for h in range(N_HEADS):
            for n in range(nblk):
                kb = k_ref[h, n * MOBA_BLOCK:(n + 1) * MOBA_BLOCK, :].astype(F32)
                km_ref[h, n:n + 1, :] = jnp.mean(kb, axis=0, keepdims=True)

    blk = lax.broadcasted_iota(jnp.int32, (nrow, TQ), 0)
    past = blk < i
    head_bits = []
    for h in range(N_HEADS):
        gate = _dot_nt(km_ref[h].astype(BF16), q_ref[h])
        bits = jnp.zeros((1, TQ), F32)
        for n in range(nblk):
            gn = gate[n:n + 1, :]
            beats = jnp.where(past, jnp.where(gate > gn, 1.0, jnp.where(gate == gn, jnp.where(blk < n, 1.0, 0.0), 0.0)), 0.0)
            rank = jnp.sum(beats, axis=0, keepdims=True)
            bits = bits + jnp.where(rank < float(topb), jnp.where(n < i, float(2 ** n), 0.0), 0.0)
        head_bits.append(bits.astype(jnp.int32))

    def values(n):
        return [_vt_block(vt_ref, h, n) for h in range(N_HEADS)]

    own = _softmax_block([_dot_nt(_k_block(k_ref, h, i), q_ref[h]) + bt_ref[h, 0] for h in range(N_HEADS)],
                         values(i), tuple(_softmax_init() for _ in range(N_HEADS)))

    def logits(n):
        s_list = []
        for h in range(N_HEADS):
            picked = (lax.shift_right_logical(head_bits[h], jnp.full_like(head_bits[h], n)) & 1) == 1
            s_list.append(_dot_nt(_k_block(k_ref, h, n), q_ref[h]) + bt_ref[h, jnp.minimum(i - n, 2)]
                          + jnp.where(picked, 0.0, NEG))
        return s_list

    carry = _softmax_loop(i, logits, values, own)
    _store_heads(o_ref, [_softmax_out(c) for c in carry])


def _moba_attention(zz, vt, b, s, bt):
    nq = s // TQ
    nblk = s // MOBA_BLOCK
    topb = min(MOBA_TOPK, nblk - 1)
    return pl.pallas_call(
        functools.partial(_moba_kernel, nblk=nblk, topb=topb),
        grid=(b, nq),
        in_specs=_attn_specs(nq, G_MB_Q, G_MB_K, M_MB, s) + [_const_spec((N_HEADS, 3, TK, TQ))],
        out_specs=_out_spec(nq),
        out_shape=jax.ShapeDtypeStruct((b * s, N_HEADS * HEAD_DIM), BF16),
        scratch_shapes=[pltpu.VMEM((N_HEADS, max(8, nblk), HEAD_DIM), F32)],
        compiler_params=_cparams(2),
        name="moba_attn",
    )(zz, zz, vt, bt)


def _merge_kernel(x_ref, osb_ref, odf_ref, ods_ref, omb_ref, gpre_ref, wg_ref, wbr_ref, wout_ref, gpost_ref, o_ref):
    x = x_ref[...]
    h = _rms(x, gpre_ref[...]).astype(BF16)
    y = jnp.zeros((x.shape[0], D_MODEL), F32)
    for r, o_r in enumerate((osb_ref, odf_ref, ods_ref, omb_ref)):
        gate = jax.nn.sigmoid(_dot(h, wg_ref[:, r * D_MODEL:(r + 1) * D_MODEL]))
        y = y + gate * _dot(o_r[...], wbr_ref[r])
    o_ref[...] = x + _rms(_dot(y.astype(BF16), wout_ref[...]), gpost_ref[...])


def _merge(x, o_sb, o_df, o_ds, o_mb, g_pre, w_gate, w_br, w_out, g_post):
    t = x.shape[0]
    tm = TM_MERGE
    tok = lambda width: pl.BlockSpec((tm, width), lambda i: (i, 0))
    return pl.pallas_call(
        _merge_kernel,
        grid=(t // tm,),
        in_specs=[tok(D_MODEL), tok(256), tok(256), tok(256), tok(256),
                  _const_spec((1, D_MODEL)), _const_spec((D_MODEL, 4 * D_MODEL)),
                  _const_spec((4, 256, D_MODEL)), _const_spec((D_MODEL, D_MODEL)), _const_spec((1, D_MODEL))],
        out_specs=tok(D_MODEL),
        out_shape=jax.ShapeDtypeStruct((t, D_MODEL), F32),
        compiler_params=_cparams(1),
        name="merge",
    )(x, o_sb, o_df, o_ds, o_mb, g_pre, w_gate, w_br, w_out, g_post)


def _ffn_kernel(x_ref, gpre_ref, win_ref, wout_ref, gpost_ref, o_ref):
    x = x_ref[...]
    h = _rms(x, gpre_ref[...]).astype(BF16)
    gate = _dot(h, win_ref[:, 0:D_FF])
    up = _dot(h, win_ref[:, D_FF:2 * D_FF])
    act = (gate * jax.nn.sigmoid(gate) * up).astype(BF16)
    o_ref[...] = x + _rms(_dot(act, wout_ref[...]), gpost_ref[...])


def _ffn(x, g_pre, w_in, w_out, g_post):
    t = x.shape[0]
    tm = TM_FFN
    return pl.pallas_call(
        _ffn_kernel,
        grid=(t // tm,),
        in_specs=[pl.BlockSpec((tm, D_MODEL), lambda i: (i, 0)), _const_spec((1, D_MODEL)),
                  _const_spec((D_MODEL, 2 * D_FF)), _const_spec((D_FF, D_MODEL)), _const_spec((1, D_MODEL))],
        out_specs=pl.BlockSpec((tm, D_MODEL), lambda i: (i, 0)),
        out_shape=jax.ShapeDtypeStruct((t, D_MODEL), F32),
        compiler_params=_cparams(1),
        name="ffn",
    )(x, g_pre, w_in, w_out, g_post)


def _t5_bucket(dist):
    max_exact = N_BUCKETS // 2
    d = jnp.maximum(dist, 0)
    log_ratio = jnp.log(jnp.maximum(d, 1).astype(F32) / max_exact) / math.log(MAX_DISTANCE / max_exact)
    large = jnp.minimum(max_exact + (log_ratio * (N_BUCKETS - max_exact)).astype(jnp.int32), N_BUCKETS - 1)
    return jnp.where(d < max_exact, d, large)


def _bias_tiles(rel_bias):
    assert TQ == TK
    n = TK
    nh = rel_bias.shape[1]
    d = np.arange(-(n - 1), 3 * n)
    by_dist = rel_bias.astype(F32).T[:, _t5_bucket(jnp.asarray(np.maximum(d, 0), jnp.int32))]
    by_dist = jnp.where(jnp.asarray(d >= 0)[None, :], by_dist * LOG2E, NEG)
    tiles = []
    for o in range(3):
        w = by_dist[:, o * n:o * n + 2 * n - 1]
        w = jnp.concatenate([w, jnp.zeros((nh, 1), F32)], axis=1)
        flat = jnp.broadcast_to(w[:, None, :], (nh, n, 2 * n)).reshape(nh, 2 * n * n)
        tiles.append(flat[:, :n * (2 * n - 1)].reshape(nh, n, 2 * n - 1)[:, :, n - 1:])
    return jnp.stack(tiles, axis=1)


def _pack_weights(w_in):
    parts, start = [], 0
    for e in range(1, N_PACK + 1):
        if e == N_PACK or _PACK_SRC[e] != _PACK_SRC[e - 1] + (1 if _PACK_SRC[e - 1] >= 0 else 0):
            a = int(_PACK_SRC[start])
            n = e - start
            parts.append(w_in[:, a:a + n] if a >= 0 else jnp.zeros((w_in.shape[0], n), w_in.dtype))
            start = e
    return jnp.concatenate(parts, axis=1).astype(BF16)


def _transposed_weights(w_in):
    wv = jnp.stack([w_in[:, _OFF[n]:_OFF[n] + 256].T for n in ("v_sb", "v_df", "v_ds", "v_mb")])
    wi = w_in[:, _OFF["wi"]:_OFF["wi"] + IDX_HEADS].T
    return wv.astype(BF16), wi.astype(BF16)


def kernel(x, w_in, w_br_sb, w_br_diff, w_br_dsa, w_br_moba, w_out, lambda_q1, lambda_k1, lambda_q2, lambda_k2,
           diff_subln_g, rel_bias, w_ffn_in, w_ffn_out, g_pre_mix, g_post_mix, g_pre_ffn, g_post_ffn):
    b, s, d = x.shape
    depth = w_in.shape[0]
    assert d == D_MODEL and s % TQ == 0 and s // MOBA_BLOCK >= 2
    t = b * s

    cs = jnp.asarray(_PACK_SCALE)[None, :]

    bt = _bias_tiles(rel_bias)
    bt_df, bt_ds, bt_mb = bt[0:4], bt[4:8], bt[8:12]
    key = np.arange(TK)[:, None]
    qry = np.arange(TQ)[None, :]
    tri = jnp.asarray(key <= np.arange(TK)[None, :], BF16)
    tril = jnp.asarray(key >= np.arange(TK)[None, :], BF16)
    sb_mask = jnp.asarray(np.where(key < qry, 0.0, NEG), F32)

    xf = x.reshape(t, d)
    for l in range(depth):
        lam_init = 0.8 - 0.6 * math.exp(-0.3 * l)
        lamp = jnp.stack([lambda_q1[l], lambda_k1[l], lambda_q2[l], lambda_k2[l]]).astype(F32)
        cst = jnp.full((1, 128), lam_init, F32)
        w_l = w_in[l]
        w_vt, w_it = _transposed_weights(w_l)
        w_br = jnp.stack([w_br_sb[l], w_br_diff[l], w_br_dsa[l], w_br_moba[l]]).astype(BF16)
        zz, vt, wt = _proj(xf, g_pre_mix[l][None, :], _pack_weights(w_l), cs, w_vt, w_it)
        o_sb = _sb_attention(zz, vt, b, s, tri, sb_mask)
        o_df = _diff_attention(zz, vt, b, s, bt_df, lamp, cst, diff_subln_g[l][:, None])
        o_ds = _dsa_attention(zz, vt, wt, b, s, bt_ds, tril)
        o_mb = _moba_attention(zz, vt, b, s, bt_mb)
        xf = _merge(xf, o_sb, o_df, o_ds, o_mb, g_pre_mix[l][None, :], w_l[:, _OFF["gate"]:].astype(BF16), w_br,
                    w_out[l].astype(BF16), g_post_mix[l][None, :])
        xf = _ffn(xf, g_pre_ffn[l][None, :], w_ffn_in[l].astype(BF16), w_ffn_out[l].astype(BF16),
                  g_post_ffn[l][None, :])
    return xf.reshape(b, s, d)
```

```python
import functools
import math

import numpy as np
import jax
import jax.numpy as jnp
from jax import lax
from jax.experimental import pallas as pl
from jax.experimental.pallas import tpu as pltpu

F32 = jnp.float32
BF16 = jnp.bfloat16

D_MODEL = 1024
HEAD_DIM = 64
N_HEADS = 4
N_MIXERS = 4
MIXER_WIDTH = N_HEADS * HEAD_DIM
DIFF_QK_DIM = 32
IDX_HEADS = 8
DSA_TOPK_MAX = 256
MOBA_BLOCK = 256
MOBA_TOPK = 3
N_BUCKETS = 32
MAX_DISTANCE = 128
D_FF = 2816
NORM_EPS = 1e-6

TQ = 256
TK = 256
NEG = -1e30
HALF_NEG = -0.5e30
BIG = 3e38
BISECT_WARMUP = 20
BISECT_TRIP = 2
BISECT_MAX_TRIPS = 134
SB_TAIL_CUTOFF = 120.0
LOG2E = math.log2(math.e)
V_ROWS = 80
TM_MERGE = 512
TM_FFN = 512
N_GROUP = 11
N_SLAB = 4 * N_GROUP
N_PACK = N_SLAB * HEAD_DIM
VMEM_LIMIT = 56 * 1024 * 1024

G_SB_Q, G_SB_K, G_DF_Q, G_DF_K, G_DS_Q, G_DS_K, G_QI_A, G_QI_B, G_MB_Q, G_MB_K, G_KIDX = range(N_GROUP)
S_KIDX = 4 * G_KIDX
M_SB, M_DF, M_DS, M_MB = range(N_MIXERS)


def _layout():
    off = {}
    acc = 0
    for name, sz in (("q_sb", 256), ("k_sb", 256), ("v_sb", 256), ("q1", 128), ("q2", 128), ("k1", 128),
                     ("k2", 128), ("v_df", 256), ("q_ds", 256), ("k_ds", 256), ("v_ds", 256), ("qi", 512),
                     ("ki", 64), ("wi", 8), ("q_mb", 256), ("k_mb", 256), ("v_mb", 256), ("gate", 4096)):
        off[name] = acc
        acc += sz
    return off


_OFF = _layout()


def _pack_layout():
    off = _OFF
    cols, scale = [], []

    def add(start, n, s=1.0):
        cols.extend(range(start, start + n))
        scale.extend([s] * n)

    hd = HEAD_DIM ** -0.5
    hd2 = hd * LOG2E
    df2 = DIFF_QK_DIM ** -0.5 * LOG2E
    add(off["q_sb"], 256, hd); add(off["k_sb"], 256)
    for h in range(N_HEADS):
        add(off["q1"] + h * 32, 32, df2); add(off["q2"] + h * 32, 32, df2)
    for h in range(N_HEADS):
        add(off["k1"] + h * 32, 32); add(off["k2"] + h * 32, 32)
    add(off["q_ds"], 256, hd2); add(off["k_ds"], 256)
    add(off["qi"], 512, HEAD_DIM ** -0.5)
    add(off["q_mb"], 256, hd2); add(off["k_mb"], 256)
    add(off["ki"], 64)
    cols.extend([-1] * 192); scale.extend([1.0] * 192)
    assert len(cols) == N_PACK
    return np.asarray(cols, np.int32), np.asarray(scale, np.float32)


_PACK_SRC, _PACK_SCALE = _pack_layout()


def _dot(a, b):
    return jnp.dot(a, b, preferred_element_type=F32)


def _dot_nt(a, b):
    return lax.dot_general(a, b, (((1,), (1,)), ((), ())), preferred_element_type=F32)


def _rms(x, g):
    return x * lax.rsqrt(jnp.mean(x * x, axis=-1, keepdims=True) + NORM_EPS) * g


def _cparams(n_axes):
    return pltpu.CompilerParams(dimension_semantics=("arbitrary",) * n_axes, vmem_limit_bytes=VMEM_LIMIT)


def _const_spec(shape):
    nd = len(shape)
    return pl.BlockSpec(shape, lambda *_: (0,) * nd, pipeline_mode=pl.Buffered(1))


def _layer_spec(shape, layer):
    nd = len(shape)
    return pl.BlockSpec((pl.Squeezed(),) + tuple(shape), lambda *_: (layer,) + (0,) * nd, pipeline_mode=pl.Buffered(1))


def _proj_kernel(x_ref, g_ref, w_ref, cs_ref, wvt_ref, wit_ref, zz_ref, vt_ref, wt_ref):
    h = _rms(x_ref[...], g_ref[...]).astype(BF16)
    for c in range(N_GROUP):
        cols = slice(c * MIXER_WIDTH, (c + 1) * MIXER_WIDTH)
        r = _dot(h, w_ref[:, cols]) * cs_ref[:, cols]
        for s in range(4):
            zz_ref[4 * c + s] = r[:, s * HEAD_DIM:(s + 1) * HEAD_DIM].astype(BF16)
    for m in range(N_MIXERS):
        vt = _dot_nt(wvt_ref[m], h).astype(BF16)
        for hh in range(N_HEADS):
            vt_ref[m, 0, hh * V_ROWS:hh * V_ROWS + HEAD_DIM, :] = vt[hh * HEAD_DIM:(hh + 1) * HEAD_DIM]
            vt_ref[m, 0, hh * V_ROWS + HEAD_DIM:(hh + 1) * V_ROWS, :] = jnp.ones((V_ROWS - HEAD_DIM, TQ), BF16)
    wt_ref[...] = _dot_nt(wit_ref[...], h) * IDX_HEADS ** -0.5


def _proj(x, g, w, cs, wvt, wit, layer):
    t = x.shape[0]
    tm = TQ
    return pl.pallas_call(
        _proj_kernel,
        grid=(t // tm,),
        in_specs=[pl.BlockSpec((tm, D_MODEL), lambda i: (i, 0)),
                  _const_spec((1, D_MODEL)),
                  _layer_spec((D_MODEL, N_PACK), layer),
                  _const_spec((1, N_PACK)),
                  _layer_spec((N_MIXERS, MIXER_WIDTH, D_MODEL), layer),
                  _layer_spec((IDX_HEADS, D_MODEL), layer)],
        out_specs=[pl.BlockSpec((N_SLAB, tm, HEAD_DIM), lambda i: (0, i, 0)),
                   pl.BlockSpec((N_MIXERS, 1, N_HEADS * V_ROWS, tm), lambda i: (0, i, 0, 0)),
                   pl.BlockSpec((IDX_HEADS, tm), lambda i: (0, i))],
        out_shape=[jax.ShapeDtypeStruct((N_SLAB, t, HEAD_DIM), BF16),
                   jax.ShapeDtypeStruct((N_MIXERS, t // tm, N_HEADS * V_ROWS, tm), BF16),
                   jax.ShapeDtypeStruct((IDX_HEADS, t), F32)],
        compiler_params=_cparams(1),
        name="proj",
    )(x, g, w, cs, wvt, wit)


def _k_block(ref, h, j):
    return ref[h, pl.ds(pl.multiple_of(j * TK, TK), TK), :]


def _fold_keys(a, op):
    n = a.shape[0]
    while n > 8:
        n //= 2
        a = op(a[:n], a[n:2 * n])
    return a


def _vt_block(ref, h, j, rows=V_ROWS):
    return ref[0, j, h * V_ROWS:h * V_ROWS + rows, :]


def _softmax_block(s_list, vt_list, carry):
    ms = [jnp.maximum(c[0], jnp.max(_fold_keys(s, jnp.maximum), axis=0, keepdims=True))
          for s, c in zip(s_list, carry)]
    pvs = [_dot(vt, jnp.exp2(s - m).astype(BF16)) for vt, s, m in zip(vt_list, s_list, ms)]
    return tuple((m_new, jnp.exp2(m - m_new) * acc + pv) for (m, acc), m_new, pv in zip(carry, ms, pvs))


def _softmax_loop(n_blocks, logits, values, carry):
    def pair(t, carry):
        j = 2 * t
        s_a, s_b = logits(j), logits(j + 1)
        return _softmax_block(s_b, values(j + 1), _softmax_block(s_a, values(j), carry))

    def single(j, carry):
        return _softmax_block(logits(j), values(j), carry)

    n_pairs = n_blocks // 2
    carry = lax.fori_loop(0, n_pairs, pair, carry)
    return lax.fori_loop(2 * n_pairs, n_blocks, single, carry)


def _softmax_init():
    return (jnp.full((1, TQ), NEG, F32), jnp.zeros((V_ROWS, TQ), F32))


def _softmax_out(carry):
    _, acc = carry
    return acc[:HEAD_DIM] / acc[HEAD_DIM:HEAD_DIM + 1]


def _store_heads(o_ref, heads_t):
    o_ref[...] = jnp.transpose(jnp.concatenate(heads_t, axis=0)).astype(BF16)


def _attn_specs(nq, gq, gk, mixer, s):
    return [pl.BlockSpec((4, TQ, HEAD_DIM), lambda b, i: (gq, b * nq + i, 0)),
            pl.BlockSpec((4, s, HEAD_DIM), lambda b, i: (gk, b, 0)),
            pl.BlockSpec((1, s // TK, N_HEADS * V_ROWS, TK), lambda b, i: (mixer, b, 0, 0))]


def _out_spec(nq):
    return pl.BlockSpec((TQ, N_HEADS * HEAD_DIM), lambda b, i: (b * nq + i, 0))


def _sb_kernel(q_ref, k_ref, vt_ref, tri_ref, mask_ref, o_ref):
    i = pl.program_id(1)
    tri = tri_ref[...]

    def block(j, carry, masked):
        heads = range(N_HEADS)
        zs = [_dot_nt(_k_block(k_ref, h, j), q_ref[h]) for h in heads]
        if masked:
            zs = [z + mask_ref[...] for z in zs]
        sps = [jnp.maximum(z, 0.0) + jnp.log(1.0 + jnp.exp(-jnp.abs(z))) for z in zs]
        his = [sp.astype(BF16) for sp in sps]
        los = [(sp - hi.astype(F32)).astype(BF16) for sp, hi in zip(sps, his)]
        cs = [_dot(tri, hi) + _dot(tri, lo) + c[0] for hi, lo, c in zip(his, los, carry)]
        avs = [_dot(_vt_block(vt_ref, h, j, HEAD_DIM), jnp.exp(z - c).astype(BF16))
               for h, z, c in zip(heads, zs, cs)]
        return tuple((c[0:1, :], old[1] + av) for c, old, av in zip(cs, carry, avs))

    init = tuple((jnp.zeros((1, TQ), F32), jnp.zeros((HEAD_DIM, TQ), F32)) for _ in range(N_HEADS))
    carry = block(i, init, True)

    def weights_left(carry):
        tail = functools.reduce(jnp.minimum, [c[0] for c in carry])
        return (jnp.min(tail) < SB_TAIL_CUTOFF).astype(jnp.int32)

    def earlier_block(c):
        j, carry, _ = c
        carry = block(j, carry, False)
        return j - 1, carry, weights_left(carry)

    _, carry, _ = lax.while_loop(lambda c: (c[0] >= 0) & (c[2] > 0), earlier_block, (i - 1, carry, weights_left(carry)))
    _store_heads(o_ref, [c[1] for c in carry])


def _sb_attention(zz, vt, b, s, tri, mask):
    nq = s // TQ
    return pl.pallas_call(
        _sb_kernel,
        grid=(b, nq),
        in_specs=_attn_specs(nq, G_SB_Q, G_SB_K, M_SB, s) + [_const_spec((TK, TK)), _const_spec((TK, TQ))],
        out_specs=_out_spec(nq),
        out_shape=jax.ShapeDtypeStruct((b * s, N_HEADS * HEAD_DIM), BF16),
        compiler_params=_cparams(2),
        name="sb_attn",
    )(zz, zz, vt, tri, mask)


def _diff_kernel(q_ref, k_ref, vt_ref, bt_ref, lam_ref, cst_ref, g_ref, o_ref):
    i = pl.program_id(1)
    lp = lam_ref[...]
    lam_init = cst_ref[:, 0:1]
    lam = (jnp.exp(jnp.sum(lp[0:1] * lp[1:2], axis=-1, keepdims=True))
           - jnp.exp(jnp.sum(lp[2:3] * lp[3:4], axis=-1, keepdims=True)) + lam_init)
    lane = lax.broadcasted_iota(jnp.int32, (TQ, HEAD_DIM), 1)
    qs = []
    for h in range(N_HEADS):
        q = q_ref[h]
        qs.append((jnp.where(lane < DIFF_QK_DIM, q, jnp.zeros_like(q)),
                   jnp.where(lane >= DIFF_QK_DIM, q, jnp.zeros_like(q))))

    def logits(j):
        s_list = []
        for h in range(N_HEADS):
            kj = _k_block(k_ref, h, j)
            bias = bt_ref[h, jnp.minimum(i - j, 2)]
            s_list += [_dot_nt(kj, qs[h][0]) + bias, _dot_nt(kj, qs[h][1]) + bias]
        return s_list

    def values(j):
        return [_vt_block(vt_ref, h, j) for h in range(N_HEADS) for _ in range(2)]

    carry = _softmax_loop(i + 1, logits, values, tuple(_softmax_init() for _ in range(2 * N_HEADS)))
    outs = []
    for h in range(N_HEADS):
        o = _softmax_out(carry[2 * h]) - lam * _softmax_out(carry[2 * h + 1])
        o = o * lax.rsqrt(jnp.mean(o * o, axis=0, keepdims=True) + NORM_EPS) * g_ref[...]
        outs.append(o * (1.0 - lam_init))
    _store_heads(o_ref, outs)


def _diff_attention(zz, vt, b, s, bt, lamp, cst, g):
    nq = s // TQ
    return pl.pallas_call(
        _diff_kernel,
        grid=(b, nq),
        in_specs=_attn_specs(nq, G_DF_Q, G_DF_K, M_DF, s) + [
            _const_spec((N_HEADS, 3, TK, TQ)), _const_spec((4, DIFF_QK_DIM)), _const_spec((1, 128)),
            _const_spec((HEAD_DIM, 1))],
        out_specs=_out_spec(nq),
        out_shape=jax.ShapeDtypeStruct((b * s, N_HEADS * HEAD_DIM), BF16),
        compiler_params=_cparams(2),
        name="diff_attn",
    )(zz, zz, vt, bt, lamp, cst, g)


def _dsa_kernel(q_ref, k_ref, vt_ref, qia_ref, qib_ref, ki_ref, wt_ref, bt_ref, tril_ref, o_ref, sc_ref, *, topk):
    i = pl.program_id(1)
    nb = i + 1
    kf = float(topk)
    w = wt_ref[...]
    key = lax.broadcasted_iota(jnp.int32, (TK, TQ), 0)
    qry = lax.broadcasted_iota(jnp.int32, (TK, TQ), 1)

    def score(j):
        kij = _k_block(ki_ref, 0, j)
        sc = jnp.zeros((TK, TQ), F32)
        for hh in range(IDX_HEADS):
            qi = (qia_ref if hh < 4 else qib_ref)[hh % 4]
            sc = sc + w[hh:hh + 1, :] * jnp.maximum(_dot_nt(kij, qi), 0.0)
        return sc

    def extend(lo_src, hi_src, c):
        return (jnp.minimum(c[0], _fold_keys(lo_src, jnp.minimum)), jnp.maximum(c[1], _fold_keys(hi_src, jnp.maximum)))

    def earlier_block(j, c):
        sc = score(j)
        sc_ref[j] = sc
        return extend(sc, sc, c)

    def earlier_pair(t, c):
        sc_a, sc_b = score(2 * t), score(2 * t + 1)
        sc_ref[2 * t] = sc_a
        sc_ref[2 * t + 1] = sc_b
        return extend(sc_b, sc_b, extend(sc_a, sc_a, c))

    lo_part, hi_part = lax.fori_loop(0, i // 2, earlier_pair,
                                     (jnp.full((8, TQ), BIG, F32), jnp.full((8, TQ), -BIG, F32)))
    lo_part, hi_part = lax.fori_loop(2 * (i // 2), i, earlier_block, (lo_part, hi_part))
    sc = score(i)
    causal = key <= qry
    sc_ref[i] = jnp.where(causal, sc, NEG)
    lo_part, hi_part = extend(jnp.where(causal, sc, BIG), jnp.where(causal, sc, NEG), (lo_part, hi_part))

    def reduce_blocks(fn, init):
        return lax.fori_loop(0, nb, lambda j, c: fn(sc_ref[j], j, c), init)

    def count_ge(t):
        part = reduce_blocks(lambda x, j, c: c + _fold_keys(jnp.where(x >= t, 1.0, 0.0), jnp.add),
                             jnp.zeros((8, TQ), F32))
        return jnp.sum(part, axis=0, keepdims=True)

    def minmax_blocks(lo_of, hi_of):
        def f(x, j, c):
            return (jnp.minimum(c[0], _fold_keys(lo_of(x), jnp.minimum)),
                    jnp.maximum(c[1], _fold_keys(hi_of(x), jnp.maximum)))
        lo_part, hi_part = reduce_blocks(f, (jnp.full((8, TQ), BIG, F32), jnp.full((8, TQ), -BIG, F32)))
        return jnp.min(lo_part, axis=0, keepdims=True), jnp.max(hi_part, axis=0, keepdims=True)

    n_valid = i * TQ + lax.broadcasted_iota(jnp.int32, (1, TQ), 1) + 1
    take_all = n_valid <= topk
    lo = jnp.min(lo_part, axis=0, keepdims=True)
    hi = jnp.max(hi_part, axis=0, keepdims=True)
    c_max = count_ge(hi)
    at_max = c_max >= kf
    state = (jnp.where(at_max, hi, lo), hi, jnp.where(at_max, c_max, n_valid.astype(F32)), c_max)

    def bisect(_, state):
        lo, hi, c_lo, c_hi = state
        mid = 0.5 * lo + 0.5 * hi
        c = count_ge(mid)
        ge = c >= kf
        return jnp.where(ge, mid, lo), jnp.where(ge, hi, mid), jnp.where(ge, c, c_lo), jnp.where(ge, c_hi, c)

    def unsettled(state):
        lo, hi, c_lo, _ = state
        open_q = jnp.where(take_all, 0.0, jnp.where(c_lo != kf, jnp.where(lo < hi, 1.0, 0.0), 0.0))

        def band_spread():
            b_min, b_max = minmax_blocks(lambda x: jnp.where(x >= lo, jnp.where(x < hi, x, BIG), BIG),
                                         lambda x: jnp.where(x >= lo, jnp.where(x < hi, x, -BIG), -BIG))
            return (jnp.max(jnp.where(b_max != b_min, open_q, 0.0)) > 0.0).astype(jnp.int32)

        return lax.cond(jnp.max(open_q) > 0.0, band_spread, lambda: jnp.int32(0))

    state = lax.fori_loop(0, jnp.where((i + 1) * TQ <= topk, 0, BISECT_WARMUP), bisect, state)

    def trip(c):
        n, state, _ = c
        state = lax.fori_loop(0, BISECT_TRIP, bisect, state)
        return n + 1, state, unsettled(state)

    _, state, _ = lax.while_loop(lambda c: (c[2] > 0) & (c[0] < BISECT_MAX_TRIPS), trip,
                                 (jnp.int32(0), state, unsettled(state)))
    lo, hi, c_lo, c_hi = state
    hi_ok = lo < hi
    c_above = jnp.where(hi_ok, c_hi, 0.0)
    hi_sel = jnp.where(hi_ok, hi, BIG)
    need = jnp.where(take_all, BIG, kf - c_above)
    lo_sel = jnp.where(take_all, HALF_NEG, lo)

    tied = jnp.max(jnp.where(take_all, 0.0, c_lo - kf)) > 0.0

    @pl.when(tied)
    def _():
        tril = tril_ref[...]

        def write_mask(j, taken):
            x = sc_ref[j]
            band = jnp.where(x >= lo_sel, jnp.where(x < hi_sel, 1.0, 0.0), 0.0)
            rank = _dot(tril, band.astype(BF16)) + taken
            sc_ref[j] = jnp.where(x >= hi_sel, 0.0,
                                  jnp.where(band * rank > 0.0, jnp.where(rank <= need, 0.0, NEG), NEG))
            return rank[TK - 1:TK, :]

        lax.fori_loop(0, nb, write_mask, jnp.zeros((1, TQ), F32))

    @pl.when(jnp.logical_not(tied))
    def _():
        def write_mask(j, _):
            sc_ref[j] = jnp.where(sc_ref[j] >= lo_sel, 0.0, NEG)
            return 0

        lax.fori_loop(0, nb, write_mask, 0)

    def logits(j):
        return [_dot_nt(_k_block(k_ref, h, j), q_ref[h]) + bt_ref[h, jnp.minimum(i - j, 2)] + sc_ref[j]
                for h in range(N_HEADS)]

    carry = _softmax_loop(nb, logits, lambda j: [_vt_block(vt_ref, h, j) for h in range(N_HEADS)],
                          tuple(_softmax_init() for _ in range(N_HEADS)))
    _store_heads(o_ref, [_softmax_out(c) for c in carry])


def _dsa_attention(zz, vt, wt, b, s, bt, tril):
    nq = s // TQ
    topk = min(DSA_TOPK_MAX, s // 4)
    return pl.pallas_call(
        functools.partial(_dsa_kernel, topk=topk),
        grid=(b, nq),
        in_specs=_attn_specs(nq, G_DS_Q, G_DS_K, M_DS, s) + [
            pl.BlockSpec((4, TQ, HEAD_DIM), lambda b_, i: (G_QI_A, b_ * nq + i, 0)),
            pl.BlockSpec((4, TQ, HEAD_DIM), lambda b_, i: (G_QI_B, b_ * nq + i, 0)),
            pl.BlockSpec((1, s, HEAD_DIM), lambda b_, i: (S_KIDX, b_, 0)),
            pl.BlockSpec((IDX_HEADS, TQ), lambda b_, i: (0, b_ * nq + i)),
            _const_spec((N_HEADS, 3, TK, TQ)), _const_spec((TK, TK))],
        out_specs=_out_spec(nq),
        out_shape=jax.ShapeDtypeStruct((b * s, N_HEADS * HEAD_DIM), BF16),
        scratch_shapes=[pltpu.VMEM((nq, TK, TQ), F32)],
        compiler_params=_cparams(2),
        name="dsa_attn",
    )(zz, zz, vt, zz, zz, zz, wt, bt, tril)


def _moba_kernel(q_ref, k_ref, vt_ref, bt_ref, o_ref, km_ref, *, nblk, topb):
    i = pl.program_id(1)
    nrow = km_ref.shape[1]

    @pl.when(i == 0)
    def _():
        km_ref[...] = jnp.zeros_like(km_ref)
        for h in range(N_HEADS):
            for n in range(nblk):
                kb = k_ref[h, n * MOBA_BLOCK:(n + 1) * MOBA_BLOCK, :].astype(F32)
                km_ref[h, n:n + 1, :] = jnp.mean(kb, axis=0, keepdims=True)

    blk = lax.broadcasted_iota(jnp.int32, (nrow, TQ), 0)
    past = blk < i
    head_bits = []
    for h in range(N_HEADS):
        gate = _dot_nt(km_ref[h].astype(BF16), q_ref[h])
        bits = jnp.zeros((1, TQ), F32)
        for n in range(nblk):
            gn = gate[n:n + 1, :]
            beats = jnp.where(past, jnp.where(gate > gn, 1.0, jnp.where(gate == gn, jnp.where(blk < n, 1.0, 0.0), 0.0)), 0.0)
            rank = jnp.sum(beats, axis=0, keepdims=True)
            bits = bits + jnp.where(rank < float(topb), jnp.where(n < i, float(2 ** n), 0.0), 0.0)
        head_bits.append(bits.astype(jnp.int32))

    def values(n):
        return [_vt_block(vt_ref, h, n) for h in range(N_HEADS)]

    own = _softmax_block([_dot_nt(_k_block(k_ref, h, i), q_ref[h]) + bt_ref[h, 0] for h in range(N_HEADS)],
                         values(i), tuple(_softmax_init() for _ in range(N_HEADS)))

    def logits(n):
        s_list = []
        for h in range(N_HEADS):
            picked = (lax.shift_right_logical(head_bits[h], jnp.full_like(head_bits[h], n)) & 1) == 1
            s_list.append(_dot_nt(_k_block(k_ref, h, n), q_ref[h]) + bt_ref[h, jnp.minimum(i - n, 2)]
                          + jnp.where(picked, 0.0, NEG))
        return s_list

    carry = _softmax_loop(i, logits, values, own)
    _store_heads(o_ref, [_softmax_out(c) for c in carry])


def _moba_attention(zz, vt, b, s, bt):
    nq = s // TQ
    nblk = s // MOBA_BLOCK
    topb = min(MOBA_TOPK, nblk - 1)
    return pl.pallas_call(
        functools.partial(_moba_kernel, nblk=nblk, topb=topb),
        grid=(b, nq),
        in_specs=_attn_specs(nq, G_MB_Q, G_MB_K, M_MB, s) + [_const_spec((N_HEADS, 3, TK, TQ))],
        out_specs=_out_spec(nq),
        out_shape=jax.ShapeDtypeStruct((b * s, N_HEADS * HEAD_DIM), BF16),
        scratch_shapes=[pltpu.VMEM((N_HEADS, max(8, nblk), HEAD_DIM), F32)],
        compiler_params=_cparams(2),
        name="moba_attn",
    )(zz, zz, vt, bt)


def _merge_kernel(x_ref, osb_ref, odf_ref, ods_ref, omb_ref, gpre_ref, wg_ref, wbr_ref, wout_ref, gpost_ref, o_ref):
    x = x_ref[...]
    h = _rms(x, gpre_ref[...]).astype(BF16)
    y = jnp.zeros((x.shape[0], D_MODEL), F32)
    for r, o_r in enumerate((osb_ref, odf_ref, ods_ref, omb_ref)):
        gate = jax.nn.sigmoid(_dot(h, wg_ref[:, r * D_MODEL:(r + 1) * D_MODEL]))
        y = y + gate * _dot(o_r[...], wbr_ref[r])
    o_ref[...] = x + _rms(_dot(y.astype(BF16), wout_ref[...]), gpost_ref[...])


def _merge(x, o_sb, o_df, o_ds, o_mb, g_pre, w_gate, w_br, w_out, g_post, layer):
    t = x.shape[0]
    tm = TM_MERGE
    tok = lambda width: pl.BlockSpec((tm, width), lambda i: (i, 0))
    return pl.pallas_call(
        _merge_kernel,
        grid=(t // tm,),
        in_specs=[tok(D_MODEL)] + [tok(MIXER_WIDTH)] * N_MIXERS + [
            _const_spec((1, D_MODEL)), _layer_spec((D_MODEL, N_MIXERS * D_MODEL), layer),
            _layer_spec((N_MIXERS, MIXER_WIDTH, D_MODEL), layer), _layer_spec((D_MODEL, D_MODEL), layer),
            _const_spec((1, D_MODEL))],
        out_specs=tok(D_MODEL),
        out_shape=jax.ShapeDtypeStruct((t, D_MODEL), F32),
        compiler_params=_cparams(1),
        name="merge",
    )(x, o_sb, o_df, o_ds, o_mb, g_pre, w_gate, w_br, w_out, g_post)


def _ffn_kernel(x_ref, gpre_ref, win_ref, wout_ref, gpost_ref, o_ref):
    x = x_ref[...]
    h = _rms(x, gpre_ref[...]).astype(BF16)
    gate = _dot(h, win_ref[:, 0:D_FF])
    up = _dot(h, win_ref[:, D_FF:2 * D_FF])
    act = (gate * jax.nn.sigmoid(gate) * up).astype(BF16)
    o_ref[...] = x + _rms(_dot(act, wout_ref[...]), gpost_ref[...])


def _ffn(x, g_pre, w_in, w_out, g_post, layer):
    t = x.shape[0]
    tm = TM_FFN
    return pl.pallas_call(
        _ffn_kernel,
        grid=(t // tm,),
        in_specs=[pl.BlockSpec((tm, D_MODEL), lambda i: (i, 0)), _const_spec((1, D_MODEL)),
                  _layer_spec((D_MODEL, 2 * D_FF), layer), _layer_spec((D_FF, D_MODEL), layer),
                  _const_spec((1, D_MODEL))],
        out_specs=pl.BlockSpec((tm, D_MODEL), lambda i: (i, 0)),
        out_shape=jax.ShapeDtypeStruct((t, D_MODEL), F32),
        compiler_params=_cparams(1),
        name="ffn",
    )(x, g_pre, w_in, w_out, g_post)


def _t5_bucket(dist):
    max_exact = N_BUCKETS // 2
    d = jnp.maximum(dist, 0)
    log_ratio = jnp.log(jnp.maximum(d, 1).astype(F32) / max_exact) / math.log(MAX_DISTANCE / max_exact)
    large = jnp.minimum(max_exact + (log_ratio * (N_BUCKETS - max_exact)).astype(jnp.int32), N_BUCKETS - 1)
    return jnp.where(d < max_exact, d, large)


def _bias_tiles(rel_bias):
    assert TQ == TK
    n = TK
    nh = rel_bias.shape[1]
    d = np.arange(-(n - 1), 3 * n)
    by_dist = rel_bias.astype(F32).T[:, _t5_bucket(jnp.asarray(np.maximum(d, 0), jnp.int32))]
    by_dist = jnp.where(jnp.asarray(d >= 0)[None, :], by_dist * LOG2E, NEG)
    tiles = []
    for o in range(3):
        w = by_dist[:, o * n:o * n + 2 * n - 1]
        w = jnp.concatenate([w, jnp.zeros((nh, 1), F32)], axis=1)
        flat = jnp.broadcast_to(w[:, None, :], (nh, n, 2 * n)).reshape(nh, 2 * n * n)
        tiles.append(flat[:, :n * (2 * n - 1)].reshape(nh, n, 2 * n - 1)[:, :, n - 1:])
    return jnp.stack(tiles, axis=1)


def _pack_weights(w_in):
    parts, start = [], 0
    for e in range(1, N_PACK + 1):
        if e == N_PACK or _PACK_SRC[e] != _PACK_SRC[e - 1] + (1 if _PACK_SRC[e - 1] >= 0 else 0):
            a = int(_PACK_SRC[start])
            n = e - start
            parts.append(w_in[:, :, a:a + n] if a >= 0 else jnp.zeros(w_in.shape[:2] + (n,), w_in.dtype))
            start = e
    return jnp.concatenate(parts, axis=2).astype(BF16)


def _transposed_weights(w_in):
    wv = jnp.stack([w_in[:, :, _OFF[n]:_OFF[n] + MIXER_WIDTH] for n in ("v_sb", "v_df", "v_ds", "v_mb")], axis=1)
    wi = w_in[:, :, _OFF["wi"]:_OFF["wi"] + IDX_HEADS]
    return jnp.swapaxes(wv, 2, 3).astype(BF16), jnp.swapaxes(wi, 1, 2).astype(BF16)


def kernel(x, w_in, w_br_sb, w_br_diff, w_br_dsa, w_br_moba, w_out, lambda_q1, lambda_k1, lambda_q2, lambda_k2,
           diff_subln_g, rel_bias, w_ffn_in, w_ffn_out, g_pre_mix, g_post_mix, g_pre_ffn, g_post_ffn):
    b, s, d = x.shape
    depth = w_in.shape[0]
    assert d == D_MODEL and s % TQ == 0 and s // MOBA_BLOCK >= 2
    t = b * s

    w_pack = _pack_weights(w_in)
    w_vt, w_it = _transposed_weights(w_in)
    w_gate = w_in[:, :, _OFF["gate"]:].astype(BF16)
    w_br = jnp.stack([w_br_sb, w_br_diff, w_br_dsa, w_br_moba], axis=1).astype(BF16)
    w_o = w_out.astype(BF16)
    w_f1 = w_ffn_in.astype(BF16)
    w_f2 = w_ffn_out.astype(BF16)
    cs = jnp.asarray(_PACK_SCALE)[None, :]

    bt = _bias_tiles(rel_bias)
    bt_df, bt_ds, bt_mb = bt[0:4], bt[4:8], bt[8:12]
    key = np.arange(TK)[:, None]
    qry = np.arange(TQ)[None, :]
    tri = jnp.asarray(key <= np.arange(TK)[None, :], BF16)
    tril = jnp.asarray(key >= np.arange(TK)[None, :], BF16)
    sb_mask = jnp.asarray(np.where(key < qry, 0.0, NEG), F32)

    xf = x.reshape(t, d)
    for l in range(depth):
        lam_init = 0.8 - 0.6 * math.exp(-0.3 * l)
        lamp = jnp.stack([lambda_q1[l], lambda_k1[l], lambda_q2[l], lambda_k2[l]]).astype(F32)
        cst = jnp.full((1, 128), lam_init, F32)
        zz, vt, wt = _proj(xf, g_pre_mix[l][None, :], w_pack, cs, w_vt, w_it, l)
        o_sb = _sb_attention(zz, vt, b, s, tri, sb_mask)
        o_df = _diff_attention(zz, vt, b, s, bt_df, lamp, cst, diff_subln_g[l][:, None])
        o_ds = _dsa_attention(zz, vt, wt, b, s, bt_ds, tril)
        o_mb = _moba_attention(zz, vt, b, s, bt_mb)
        xf = _merge(xf, o_sb, o_df, o_ds, o_mb, g_pre_mix[l][None, :], w_gate, w_br, w_o, g_post_mix[l][None, :], l)
        xf = _ffn(xf, g_pre_ffn[l][None, :], w_f1, w_f2, g_post_ffn[l][None, :], l)
    return xf.reshape(b, s, d)
```

```python
import functools
import math

import numpy as np
import jax
import jax.numpy as jnp
from jax import lax
from jax.experimental import pallas as pl
from jax.experimental.pallas import tpu as pltpu

F32 = jnp.float32
BF16 = jnp.bfloat16

D_MODEL = 1024
HEAD_DIM = 64
N_HEADS = 4
N_MIXERS = 4
MIXER_WIDTH = N_HEADS * HEAD_DIM
DIFF_QK_DIM = 32
IDX_HEADS = 8
DSA_TOPK_MAX = 256
MOBA_BLOCK = 256
MOBA_TOPK = 3
N_BUCKETS = 32
MAX_DISTANCE = 128
D_FF = 2816
NORM_EPS = 1e-6

TQ = 256
TK = 256
NEG = -1e30
HALF_NEG = -0.5e30
BIG = 3e38
BISECT_WARMUP = 20
BISECT_TRIP = 2
BISECT_MAX_TRIPS = 134
SB_TAIL_CUTOFF = 120.0
LOG2E = math.log2(math.e)
V_ROWS = 80
TM_MERGE = 512
TM_FFN = 512
N_GROUP = 11
N_SLAB = 4 * N_GROUP
N_PACK = N_SLAB * HEAD_DIM
VMEM_LIMIT = 56 * 1024 * 1024

G_SB_Q, G_SB_K, G_DF_Q, G_DF_K, G_DS_Q, G_DS_K, G_QI_A, G_QI_B, G_MB_Q, G_MB_K, G_KIDX = range(N_GROUP)
S_KIDX = 4 * G_KIDX
M_SB, M_DF, M_DS, M_MB = range(N_MIXERS)


def _layout():
    off = {}
    acc = 0
    for name, sz in (("q_sb", 256), ("k_sb", 256), ("v_sb", 256), ("q1", 128), ("q2", 128), ("k1", 128),
                     ("k2", 128), ("v_df", 256), ("q_ds", 256), ("k_ds", 256), ("v_ds", 256), ("qi", 512),
                     ("ki", 64), ("wi", 8), ("q_mb", 256), ("k_mb", 256), ("v_mb", 256), ("gate", 4096)):
        off[name] = acc
        acc += sz
    return off


_OFF = _layout()


def _pack_layout():
    off = _OFF
    cols, scale = [], []

    def add(start, n, s=1.0):
        cols.extend(range(start, start + n))
        scale.extend([s] * n)

    hd = HEAD_DIM ** -0.5
    hd2 = hd * LOG2E
    df2 = DIFF_QK_DIM ** -0.5 * LOG2E
    add(off["q_sb"], 256, hd); add(off["k_sb"], 256)
    for h in range(N_HEADS):
        add(off["q1"] + h * 32, 32, df2); add(off["q2"] + h * 32, 32, df2)
    for h in range(N_HEADS):
        add(off["k1"] + h * 32, 32); add(off["k2"] + h * 32, 32)
    add(off["q_ds"], 256, hd2); add(off["k_ds"], 256)
    add(off["qi"], 512, HEAD_DIM ** -0.5)
    add(off["q_mb"], 256, hd2); add(off["k_mb"], 256)
    add(off["ki"], 64)
    cols.extend([-1] * 192); scale.extend([1.0] * 192)
    assert len(cols) == N_PACK
    return np.asarray(cols, np.int32), np.asarray(scale, np.float32)


_PACK_SRC, _PACK_SCALE = _pack_layout()


def _dot(a, b):
    return jnp.dot(a, b, preferred_element_type=F32)


def _dot_nt(a, b):
    return lax.dot_general(a, b, (((1,), (1,)), ((), ())), preferred_element_type=F32)


def _rms(x, g):
    return x * lax.rsqrt(jnp.mean(x * x, axis=-1, keepdims=True) + NORM_EPS) * g


def _cparams(n_axes):
    return pltpu.CompilerParams(dimension_semantics=("arbitrary",) * n_axes, vmem_limit_bytes=VMEM_LIMIT)


def _const_spec(shape):
    nd = len(shape)
    return pl.BlockSpec(shape, lambda *_: (0,) * nd, pipeline_mode=pl.Buffered(1))


def _layer_spec(shape, layer):
    nd = len(shape)
    return pl.BlockSpec((pl.Squeezed(),) + tuple(shape), lambda *_: (layer,) + (0,) * nd, pipeline_mode=pl.Buffered(1))


def _proj_kernel(x_ref, g_ref, w_ref, cs_ref, wvt_ref, wit_ref, zz_ref, vt_ref, wt_ref):
    h = _rms(x_ref[...], g_ref[...]).astype(BF16)
    for c in range(N_GROUP):
        cols = slice(c * MIXER_WIDTH, (c + 1) * MIXER_WIDTH)
        r = _dot(h, w_ref[:, cols]) * cs_ref[:, cols]
        for s in range(4):
            zz_ref[4 * c + s] = r[:, s * HEAD_DIM:(s + 1) * HEAD_DIM].astype(BF16)
    for m in range(N_MIXERS):
        vt = _dot_nt(wvt_ref[m], h).astype(BF16)
        for hh in range(N_HEADS):
            vt_ref[m, 0, hh * V_ROWS:hh * V_ROWS + HEAD_DIM, :] = vt[hh * HEAD_DIM:(hh + 1) * HEAD_DIM]
            vt_ref[m, 0, hh * V_ROWS + HEAD_DIM:(hh + 1) * V_ROWS, :] = jnp.ones((V_ROWS - HEAD_DIM, TQ), BF16)
    wt_ref[...] = _dot_nt(wit_ref[...], h) * IDX_HEADS ** -0.5


def _proj(x, g, w, cs, wvt, wit, layer):
    t = x.shape[0]
    tm = TQ
    return pl.pallas_call(
        _proj_kernel,
        grid=(t // tm,),
        in_specs=[pl.BlockSpec((tm, D_MODEL), lambda i: (i, 0)),
                  _const_spec((1, D_MODEL)),
                  _layer_spec((D_MODEL, N_PACK), layer),
                  _const_spec((1, N_PACK)),
                  _layer_spec((N_MIXERS, MIXER_WIDTH, D_MODEL), layer),
                  _layer_spec((IDX_HEADS, D_MODEL), layer)],
        out_specs=[pl.BlockSpec((N_SLAB, tm, HEAD_DIM), lambda i: (0, i, 0)),
                   pl.BlockSpec((N_MIXERS, 1, N_HEADS * V_ROWS, tm), lambda i: (0, i, 0, 0)),
                   pl.BlockSpec((IDX_HEADS, tm), lambda i: (0, i))],
        out_shape=[jax.ShapeDtypeStruct((N_SLAB, t, HEAD_DIM), BF16),
                   jax.ShapeDtypeStruct((N_MIXERS, t // tm, N_HEADS * V_ROWS, tm), BF16),
                   jax.ShapeDtypeStruct((IDX_HEADS, t), F32)],
        compiler_params=_cparams(1),
        name="proj",
    )(x, g, w, cs, wvt, wit)


def _k_block(ref, h, j):
    return ref[h, pl.ds(pl.multiple_of(j * TK, TK), TK), :]


def _fold_keys(a, op):
    n = a.shape[0]
    while n > 8:
        n //= 2
        a = op(a[:n], a[n:2 * n])
    return a


def _vt_block(ref, h, j, rows=V_ROWS):
    return ref[0, j, h * V_ROWS:h * V_ROWS + rows, :]


def _softmax_block(s_list, vt_list, carry):
    ms = [jnp.maximum(c[0], jnp.max(_fold_keys(s, jnp.maximum), axis=0, keepdims=True))
          for s, c in zip(s_list, carry)]
    pvs = [_dot(vt, jnp.exp2(s - m).astype(BF16)) for vt, s, m in zip(vt_list, s_list, ms)]
    return tuple((m_new, jnp.exp2(m - m_new) * acc + pv) for (m, acc), m_new, pv in zip(carry, ms, pvs))


def _softmax_loop(n_blocks, logits, values, carry):
    def pair(t, carry):
        j = 2 * t
        s_a, s_b = logits(j), logits(j + 1)
        return _softmax_block(s_b, values(j + 1), _softmax_block(s_a, values(j), carry))

    def single(j, carry):
        return _softmax_block(logits(j), values(j), carry)

    n_pairs = n_blocks // 2
    carry = lax.fori_loop(0, n_pairs, pair, carry)
    return lax.fori_loop(2 * n_pairs, n_blocks, single, carry)


def _softmax_init():
    return (jnp.full((1, TQ), NEG, F32), jnp.zeros((V_ROWS, TQ), F32))


def _softmax_out(carry):
    _, acc = carry
    return acc[:HEAD_DIM] / acc[HEAD_DIM:HEAD_DIM + 1]


def _store_heads(o_ref, heads_t):
    o_ref[...] = jnp.transpose(jnp.concatenate(heads_t, axis=0)).astype(BF16)


def _attn_specs(nq, gq, gk, mixer, s):
    return [pl.BlockSpec((4, TQ, HEAD_DIM), lambda b, i: (gq, b * nq + i, 0)),
            pl.BlockSpec((4, s, HEAD_DIM), lambda b, i: (gk, b, 0)),
            pl.BlockSpec((1, s // TK, N_HEADS * V_ROWS, TK), lambda b, i: (mixer, b, 0, 0))]


def _out_spec(nq):
    return pl.BlockSpec((TQ, N_HEADS * HEAD_DIM), lambda b, i: (b * nq + i, 0))


def _sb_kernel(q_ref, k_ref, vt_ref, tri_ref, mask_ref, o_ref):
    i = pl.program_id(1)
    tri = tri_ref[...]

    def block(j, carry, masked):
        heads = range(N_HEADS)
        zs = [_dot_nt(_k_block(k_ref, h, j), q_ref[h]) for h in heads]
        if masked:
            zs = [z + mask_ref[...] for z in zs]
        sps = [jnp.maximum(z, 0.0) + jnp.log(1.0 + jnp.exp(-jnp.abs(z))) for z in zs]
        his = [sp.astype(BF16) for sp in sps]
        los = [(sp - hi.astype(F32)).astype(BF16) for sp, hi in zip(sps, his)]
        cs = [_dot(tri, hi) + _dot(tri, lo) + c[0] for hi, lo, c in zip(his, los, carry)]
        avs = [_dot(_vt_block(vt_ref, h, j, HEAD_DIM), jnp.exp(z - c).astype(BF16))
               for h, z, c in zip(heads, zs, cs)]
        return tuple((c[0:1, :], old[1] + av) for c, old, av in zip(cs, carry, avs))

    init = tuple((jnp.zeros((1, TQ), F32), jnp.zeros((HEAD_DIM, TQ), F32)) for _ in range(N_HEADS))
    carry = block(i, init, True)

    def weights_left(carry):
        tail = functools.reduce(jnp.minimum, [c[0] for c in carry])
        return (jnp.min(tail) < SB_TAIL_CUTOFF).astype(jnp.int32)

    def earlier_block(c):
        j, carry, _ = c
        carry = block(j, carry, False)
        return j - 1, carry, weights_left(carry)

    _, carry, _ = lax.while_loop(lambda c: (c[0] >= 0) & (c[2] > 0), earlier_block, (i - 1, carry, weights_left(carry)))
    _store_heads(o_ref, [c[1] for c in carry])


def _sb_attention(zz, vt, b, s, tri, mask):
    nq = s // TQ
    return pl.pallas_call(
        _sb_kernel,
        grid=(b, nq),
        in_specs=_attn_specs(nq, G_SB_Q, G_SB_K, M_SB, s) + [_const_spec((TK, TK)), _const_spec((TK, TQ))],
        out_specs=_out_spec(nq),
        out_shape=jax.ShapeDtypeStruct((b * s, N_HEADS * HEAD_DIM), BF16),
        compiler_params=_cparams(2),
        name="sb_attn",
    )(zz, zz, vt, tri, mask)


def _diff_kernel(q_ref, k_ref, vt_ref, bt_ref, lam_ref, cst_ref, g_ref, o_ref):
    i = pl.program_id(1)
    lp = lam_ref[...]
    lam_init = cst_ref[:, 0:1]
    lam = (jnp.exp(jnp.sum(lp[0:1] * lp[1:2], axis=-1, keepdims=True))
           - jnp.exp(jnp.sum(lp[2:3] * lp[3:4], axis=-1, keepdims=True)) + lam_init)
    lane = lax.broadcasted_iota(jnp.int32, (TQ, HEAD_DIM), 1)
    qs = []
    for h in range(N_HEADS):
        q = q_ref[h]
        qs.append((jnp.where(lane < DIFF_QK_DIM, q, jnp.zeros_like(q)),
                   jnp.where(lane >= DIFF_QK_DIM, q, jnp.zeros_like(q))))

    def logits(j):
        s_list = []
        for h in range(N_HEADS):
            kj = _k_block(k_ref, h, j)
            bias = bt_ref[h, jnp.minimum(i - j, 2)]
            s_list += [_dot_nt(kj, qs[h][0]) + bias, _dot_nt(kj, qs[h][1]) + bias]
        return s_list

    def values(j):
        return [_vt_block(vt_ref, h, j) for h in range(N_HEADS) for _ in range(2)]

    carry = _softmax_loop(i + 1, logits, values, tuple(_softmax_init() for _ in range(2 * N_HEADS)))
    outs = []
    for h in range(N_HEADS):
        o = _softmax_out(carry[2 * h]) - lam * _softmax_out(carry[2 * h + 1])
        o = o * lax.rsqrt(jnp.mean(o * o, axis=0, keepdims=True) + NORM_EPS) * g_ref[...]
        outs.append(o * (1.0 - lam_init))
    _store_heads(o_ref, outs)


def _diff_attention(zz, vt, b, s, bt, lamp, cst, g):
    nq = s // TQ
    return pl.pallas_call(
        _diff_kernel,
        grid=(b, nq),
        in_specs=_attn_specs(nq, G_DF_Q, G_DF_K, M_DF, s) + [
            _const_spec((N_HEADS, 3, TK, TQ)), _const_spec((4, DIFF_QK_DIM)), _const_spec((1, 128)),
            _const_spec((HEAD_DIM, 1))],
        out_specs=_out_spec(nq),
        out_shape=jax.ShapeDtypeStruct((b * s, N_HEADS * HEAD_DIM), BF16),
        compiler_params=_cparams(2),
        name="diff_attn",
    )(zz, zz, vt, bt, lamp, cst, g)


def _dsa_kernel(q_ref, k_ref, vt_ref, qia_ref, qib_ref, ki_ref, wt_ref, bt_ref, tril_ref, o_ref, sc_ref, *, topk):
    i = pl.program_id(1)
    nb = i + 1
    kf = float(topk)
    w = wt_ref[...]
    key = lax.broadcasted_iota(jnp.int32, (TK, TQ), 0)
    qry = lax.broadcasted_iota(jnp.int32, (TK, TQ), 1)

    def score(j):
        kij = _k_block(ki_ref, 0, j)
        sc = jnp.zeros((TK, TQ), F32)
        for hh in range(IDX_HEADS):
            qi = (qia_ref if hh < 4 else qib_ref)[hh % 4]
            sc = sc + w[hh:hh + 1, :] * jnp.maximum(_dot_nt(kij, qi), 0.0)
        return sc

    def extend(lo_src, hi_src, c):
        return (jnp.minimum(c[0], _fold_keys(lo_src, jnp.minimum)), jnp.maximum(c[1], _fold_keys(hi_src, jnp.maximum)))

    def earlier_block(j, c):
        sc = score(j)
        sc_ref[j] = sc
        return extend(sc, sc, c)

    def earlier_pair(t, c):
        sc_a, sc_b = score(2 * t), score(2 * t + 1)
        sc_ref[2 * t] = sc_a
        sc_ref[2 * t + 1] = sc_b
        return extend(sc_b, sc_b, extend(sc_a, sc_a, c))

    lo_part, hi_part = lax.fori_loop(0, i // 2, earlier_pair,
                                     (jnp.full((8, TQ), BIG, F32), jnp.full((8, TQ), -BIG, F32)))
    lo_part, hi_part = lax.fori_loop(2 * (i // 2), i, earlier_block, (lo_part, hi_part))
    sc = score(i)
    causal = key <= qry
    sc_ref[i] = jnp.where(causal, sc, NEG)
    lo_part, hi_part = extend(jnp.where(causal, sc, BIG), jnp.where(causal, sc, NEG), (lo_part, hi_part))

    def reduce_blocks(fn, init):
        def pair(t, c):
            return fn(sc_ref[2 * t + 1], 2 * t + 1, fn(sc_ref[2 * t], 2 * t, c))
        c = lax.fori_loop(0, nb // 2, pair, init)
        return lax.fori_loop(2 * (nb // 2), nb, lambda j, c: fn(sc_ref[j], j, c), c)

    def count_ge(t):
        part = reduce_blocks(lambda x, j, c: c + _fold_keys(jnp.where(x >= t, 1.0, 0.0), jnp.add),
                             jnp.zeros((8, TQ), F32))
        return jnp.sum(part, axis=0, keepdims=True)

    def minmax_blocks(lo_of, hi_of):
        def f(x, j, c):
            return (jnp.minimum(c[0], _fold_keys(lo_of(x), jnp.minimum)),
                    jnp.maximum(c[1], _fold_keys(hi_of(x), jnp.maximum)))
        lo_part, hi_part = reduce_blocks(f, (jnp.full((8, TQ), BIG, F32), jnp.full((8, TQ), -BIG, F32)))
        return jnp.min(lo_part, axis=0, keepdims=True), jnp.max(hi_part, axis=0, keepdims=True)

    n_valid = i * TQ + lax.broadcasted_iota(jnp.int32, (1, TQ), 1) + 1
    take_all = n_valid <= topk
    lo = jnp.min(lo_part, axis=0, keepdims=True)
    hi = jnp.max(hi_part, axis=0, keepdims=True)
    c_max = count_ge(hi)
    at_max = c_max >= kf
    state = (jnp.where(at_max, hi, lo), hi, jnp.where(at_max, c_max, n_valid.astype(F32)), c_max)

    def bisect(_, state):
        lo, hi, c_lo, c_hi = state
        mid = 0.5 * lo + 0.5 * hi
        c = count_ge(mid)
        ge = c >= kf
        return jnp.where(ge, mid, lo), jnp.where(ge, hi, mid), jnp.where(ge, c, c_lo), jnp.where(ge, c_hi, c)

    def unsettled(state):
        lo, hi, c_lo, _ = state
        open_q = jnp.where(take_all, 0.0, jnp.where(c_lo != kf, jnp.where(lo < hi, 1.0, 0.0), 0.0))

        def band_spread():
            b_min, b_max = minmax_blocks(lambda x: jnp.where(x >= lo, jnp.where(x < hi, x, BIG), BIG),
                                         lambda x: jnp.where(x >= lo, jnp.where(x < hi, x, -BIG), -BIG))
            return (jnp.max(jnp.where(b_max != b_min, open_q, 0.0)) > 0.0).astype(jnp.int32)

        return lax.cond(jnp.max(open_q) > 0.0, band_spread, lambda: jnp.int32(0))

    state = lax.fori_loop(0, jnp.where((i + 1) * TQ <= topk, 0, BISECT_WARMUP), bisect, state)

    def trip(c):
        n, state, _ = c
        state = lax.fori_loop(0, BISECT_TRIP, bisect, state)
        return n + 1, state, unsettled(state)

    _, state, _ = lax.while_loop(lambda c: (c[2] > 0) & (c[0] < BISECT_MAX_TRIPS), trip,
                                 (jnp.int32(0), state, unsettled(state)))
    lo, hi, c_lo, c_hi = state
    hi_ok = lo < hi
    c_above = jnp.where(hi_ok, c_hi, 0.0)
    hi_sel = jnp.where(hi_ok, hi, BIG)
    need = jnp.where(take_all, BIG, kf - c_above)
    lo_sel = jnp.where(take_all, HALF_NEG, lo)

    tied = jnp.max(jnp.where(take_all, 0.0, c_lo - kf)) > 0.0

    @pl.when(tied)
    def _():
        tril = tril_ref[...]

        def band_of(j):
            x = sc_ref[j]
            return x, jnp.where(x >= lo_sel, jnp.where(x < hi_sel, 1.0, 0.0), 0.0)

        def write(j, x, band, rank):
            sc_ref[j] = jnp.where(x >= hi_sel, 0.0,
                                  jnp.where(band * rank > 0.0, jnp.where(rank <= need, 0.0, NEG), NEG))
            return rank[TK - 1:TK, :]

        def write_mask(j, taken):
            x, band = band_of(j)
            return write(j, x, band, _dot(tril, band.astype(BF16)) + taken)

        def write_mask_pair(t, taken):
            (x_a, band_a), (x_b, band_b) = band_of(2 * t), band_of(2 * t + 1)
            in_a, in_b = _dot(tril, band_a.astype(BF16)), _dot(tril, band_b.astype(BF16))
            taken = write(2 * t, x_a, band_a, in_a + taken)
            return write(2 * t + 1, x_b, band_b, in_b + taken)

        taken = lax.fori_loop(0, nb // 2, write_mask_pair, jnp.zeros((1, TQ), F32))
        lax.fori_loop(2 * (nb // 2), nb, write_mask, taken)

    @pl.when(jnp.logical_not(tied))
    def _():
        def write_mask(j, _):
            sc_ref[j] = jnp.where(sc_ref[j] >= lo_sel, 0.0, NEG)
            return 0

        lax.fori_loop(0, nb, write_mask, 0)

    def logits(j):
        return [_dot_nt(_k_block(k_ref, h, j), q_ref[h]) + bt_ref[h, jnp.minimum(i - j, 2)] + sc_ref[j]
                for h in range(N_HEADS)]

    carry = _softmax_loop(nb, logits, lambda j: [_vt_block(vt_ref, h, j) for h in range(N_HEADS)],
                          tuple(_softmax_init() for _ in range(N_HEADS)))
    _store_heads(o_ref, [_softmax_out(c) for c in carry])


def _dsa_attention(zz, vt, wt, b, s, bt, tril):
    nq = s // TQ
    topk = min(DSA_TOPK_MAX, s // 4)
    return pl.pallas_call(
        functools.partial(_dsa_kernel, topk=topk),
        grid=(b, nq),
        in_specs=_attn_specs(nq, G_DS_Q, G_DS_K, M_DS, s) + [
            pl.BlockSpec((4, TQ, HEAD_DIM), lambda b_, i: (G_QI_A, b_ * nq + i, 0)),
            pl.BlockSpec((4, TQ, HEAD_DIM), lambda b_, i: (G_QI_B, b_ * nq + i, 0)),
            pl.BlockSpec((1, s, HEAD_DIM), lambda b_, i: (S_KIDX, b_, 0)),
            pl.BlockSpec((IDX_HEADS, TQ), lambda b_, i: (0, b_ * nq + i)),
            _const_spec((N_HEADS, 3, TK, TQ)), _const_spec((TK, TK))],
        out_specs=_out_spec(nq),
        out_shape=jax.ShapeDtypeStruct((b * s, N_HEADS * HEAD_DIM), BF16),
        scratch_shapes=[pltpu.VMEM((nq, TK, TQ), F32)],
        compiler_params=_cparams(2),
        name="dsa_attn",
    )(zz, zz, vt, zz, zz, zz, wt, bt, tril)


def _moba_kernel(q_ref, k_ref, vt_ref, bt_ref, o_ref, km_ref, *, nblk, topb):
    i = pl.program_id(1)
    nrow = km_ref.shape[1]

    @pl.when(i == 0)
    def _():
        km_ref[...] = jnp.zeros_like(km_ref)
        for h in range(N_HEADS):
            for n in range(nblk):
                kb = k_ref[h, n * MOBA_BLOCK:(n + 1) * MOBA_BLOCK, :].astype(F32)
                km_ref[h, n:n + 1, :] = jnp.mean(kb, axis=0, keepdims=True)

    blk = lax.broadcasted_iota(jnp.int32, (nrow, TQ), 0)
    past = blk < i
    head_bits = []
    for h in range(N_HEADS):
        gate = _dot_nt(km_ref[h].astype(BF16), q_ref[h])
        bits = jnp.zeros((1, TQ), F32)
        for n in range(nblk):
            gn = gate[n:n + 1, :]
            beats = jnp.where(past, jnp.where(gate > gn, 1.0, jnp.where(gate == gn, jnp.where(blk < n, 1.0, 0.0), 0.0)), 0.0)
            rank = jnp.sum(beats, axis=0, keepdims=True)
            bits = bits + jnp.where(rank < float(topb), jnp.where(n < i, float(2 ** n), 0.0), 0.0)
        head_bits.append(bits.astype(jnp.int32) | lax.shift_left(jnp.int32(1), i))

    def values(n):
        return [_vt_block(vt_ref, h, n) for h in range(N_HEADS)]

    def logits(n):
        s_list = []
        for h in range(N_HEADS):
            picked = (lax.shift_right_logical(head_bits[h], jnp.full_like(head_bits[h], n)) & 1) == 1
            s_list.append(_dot_nt(_k_block(k_ref, h, n), q_ref[h]) + bt_ref[h, jnp.minimum(i - n, 2)]
                          + jnp.where(picked, 0.0, NEG))
        return s_list

    carry = _softmax_loop(i + 1, logits, values, tuple(_softmax_init() for _ in range(N_HEADS)))
    _store_heads(o_ref, [_softmax_out(c) for c in carry])


def _moba_attention(zz, vt, b, s, bt):
    nq = s // TQ
    nblk = s // MOBA_BLOCK
    topb = min(MOBA_TOPK, nblk - 1)
    return pl.pallas_call(
        functools.partial(_moba_kernel, nblk=nblk, topb=topb),
        grid=(b, nq),
        in_specs=_attn_specs(nq, G_MB_Q, G_MB_K, M_MB, s) + [_const_spec((N_HEADS, 3, TK, TQ))],
        out_specs=_out_spec(nq),
        out_shape=jax.ShapeDtypeStruct((b * s, N_HEADS * HEAD_DIM), BF16),
        scratch_shapes=[pltpu.VMEM((N_HEADS, max(8, nblk), HEAD_DIM), F32)],
        compiler_params=_cparams(2),
        name="moba_attn",
    )(zz, zz, vt, bt)


def _merge_kernel(x_ref, osb_ref, odf_ref, ods_ref, omb_ref, gpre_ref, wg_ref, wbr_ref, wout_ref, gpost_ref, o_ref):
    x = x_ref[...]
    h = _rms(x, gpre_ref[...]).astype(BF16)
    y = jnp.zeros((x.shape[0], D_MODEL), F32)
    for r, o_r in enumerate((osb_ref, odf_ref, ods_ref, omb_ref)):
        gate = jax.nn.sigmoid(_dot(h, wg_ref[:, r * D_MODEL:(r + 1) * D_MODEL]))
        y = y + gate * _dot(o_r[...], wbr_ref[r])
    o_ref[...] = x + _rms(_dot(y.astype(BF16), wout_ref[...]), gpost_ref[...])


def _merge(x, o_sb, o_df, o_ds, o_mb, g_pre, w_gate, w_br, w_out, g_post, layer):
    t = x.shape[0]
    tm = TM_MERGE
    tok = lambda width: pl.BlockSpec((tm, width), lambda i: (i, 0))
    return pl.pallas_call(
        _merge_kernel,
        grid=(t // tm,),
        in_specs=[tok(D_MODEL)] + [tok(MIXER_WIDTH)] * N_MIXERS + [
            _const_spec((1, D_MODEL)), _layer_spec((D_MODEL, N_MIXERS * D_MODEL), layer),
            _layer_spec((N_MIXERS, MIXER_WIDTH, D_MODEL), layer), _layer_spec((D_MODEL, D_MODEL), layer),
            _const_spec((1, D_MODEL))],
        out_specs=tok(D_MODEL),
        out_shape=jax.ShapeDtypeStruct((t, D_MODEL), F32),
        compiler_params=_cparams(1),
        name="merge",
    )(x, o_sb, o_df, o_ds, o_mb, g_pre, w_gate, w_br, w_out, g_post)


def _ffn_kernel(x_ref, gpre_ref, win_ref, wout_ref, gpost_ref, o_ref):
    x = x_ref[...]
    h = _rms(x, gpre_ref[...]).astype(BF16)
    gate = _dot(h, win_ref[:, 0:D_FF])
    up = _dot(h, win_ref[:, D_FF:2 * D_FF])
    act = (gate * jax.nn.sigmoid(gate) * up).astype(BF16)
    o_ref[...] = x + _rms(_dot(act, wout_ref[...]), gpost_ref[...])


def _ffn(x, g_pre, w_in, w_out, g_post, layer):
    t = x.shape[0]
    tm = TM_FFN
    return pl.pallas_call(
        _ffn_kernel,
        grid=(t // tm,),
        in_specs=[pl.BlockSpec((tm, D_MODEL), lambda i: (i, 0)), _const_spec((1, D_MODEL)),
                  _layer_spec((D_MODEL, 2 * D_FF), layer), _layer_spec((D_FF, D_MODEL), layer),
                  _const_spec((1, D_MODEL))],
        out_specs=pl.BlockSpec((tm, D_MODEL), lambda i: (i, 0)),
        out_shape=jax.ShapeDtypeStruct((t, D_MODEL), F32),
        compiler_params=_cparams(1),
        name="ffn",
    )(x, g_pre, w_in, w_out, g_post)


def _t5_bucket(dist):
    max_exact = N_BUCKETS // 2
    d = jnp.maximum(dist, 0)
    log_ratio = jnp.log(jnp.maximum(d, 1).astype(F32) / max_exact) / math.log(MAX_DISTANCE / max_exact)
    large = jnp.minimum(max_exact + (log_ratio * (N_BUCKETS - max_exact)).astype(jnp.int32), N_BUCKETS - 1)
    return jnp.where(d < max_exact, d, large)


def _bias_tiles(rel_bias):
    assert TQ == TK
    n = TK
    nh = rel_bias.shape[1]
    d = np.arange(-(n - 1), 3 * n)
    by_dist = rel_bias.astype(F32).T[:, _t5_bucket(jnp.asarray(np.maximum(d, 0), jnp.int32))]
    by_dist = jnp.where(jnp.asarray(d >= 0)[None, :], by_dist * LOG2E, NEG)
    tiles = []
    for o in range(3):
        w = by_dist[:, o * n:o * n + 2 * n - 1]
        w = jnp.concatenate([w, jnp.zeros((nh, 1), F32)], axis=1)
        flat = jnp.broadcast_to(w[:, None, :], (nh, n, 2 * n)).reshape(nh, 2 * n * n)
        tiles.append(flat[:, :n * (2 * n - 1)].reshape(nh, n, 2 * n - 1)[:, :, n - 1:])
    return jnp.stack(tiles, axis=1)


def _pack_weights(w_in):
    parts, start = [], 0
    for e in range(1, N_PACK + 1):
        if e == N_PACK or _PACK_SRC[e] != _PACK_SRC[e - 1] + (1 if _PACK_SRC[e - 1] >= 0 else 0):
            a = int(_PACK_SRC[start])
            n = e - start
            parts.append(w_in[:, :, a:a + n] if a >= 0 else jnp.zeros(w_in.shape[:2] + (n,), w_in.dtype))
            start = e
    return jnp.concatenate(parts, axis=2).astype(BF16)


def _transposed_weights(w_in):
    wv = jnp.stack([w_in[:, :, _OFF[n]:_OFF[n] + MIXER_WIDTH] for n in ("v_sb", "v_df", "v_ds", "v_mb")], axis=1)
    wi = w_in[:, :, _OFF["wi"]:_OFF["wi"] + IDX_HEADS]
    return jnp.swapaxes(wv, 2, 3).astype(BF16), jnp.swapaxes(wi, 1, 2).astype(BF16)


def kernel(x, w_in, w_br_sb, w_br_diff, w_br_dsa, w_br_moba, w_out, lambda_q1, lambda_k1, lambda_q2, lambda_k2,
           diff_subln_g, rel_bias, w_ffn_in, w_ffn_out, g_pre_mix, g_post_mix, g_pre_ffn, g_post_ffn):
    b, s, d = x.shape
    depth = w_in.shape[0]
    assert d == D_MODEL and s % TQ == 0 and s // MOBA_BLOCK >= 2
    t = b * s

    w_pack = _pack_weights(w_in)
    w_vt, w_it = _transposed_weights(w_in)
    w_gate = w_in.astype(BF16)[:, :, _OFF["gate"]:]
    w_br = jnp.stack([w_br_sb, w_br_diff, w_br_dsa, w_br_moba], axis=1).astype(BF16)
    w_o = w_out.astype(BF16)
    w_f1 = w_ffn_in.astype(BF16)
    w_f2 = w_ffn_out.astype(BF16)
    cs = jnp.asarray(_PACK_SCALE)[None, :]

    bt = _bias_tiles(rel_bias)
    bt_df, bt_ds, bt_mb = bt[0:4], bt[4:8], bt[8:12]
    key = np.arange(TK)[:, None]
    qry = np.arange(TQ)[None, :]
    tri = jnp.asarray(key <= np.arange(TK)[None, :], BF16)
    tril = jnp.asarray(key >= np.arange(TK)[None, :], BF16)
    sb_mask = jnp.asarray(np.where(key < qry, 0.0, NEG), F32)

    xf = x.reshape(t, d)
    for l in range(depth):
        lam_init = 0.8 - 0.6 * math.exp(-0.3 * l)
        lamp = jnp.stack([lambda_q1[l], lambda_k1[l], lambda_q2[l], lambda_k2[l]]).astype(F32)
        cst = jnp.full((1, 128), lam_init, F32)
        zz, vt, wt = _proj(xf, g_pre_mix[l][None, :], w_pack, cs, w_vt, w_it, l)
        o_sb = _sb_attention(zz, vt, b, s, tri, sb_mask)
        o_df = _diff_attention(zz, vt, b, s, bt_df, lamp, cst, diff_subln_g[l][:, None])
        o_ds = _dsa_attention(zz, vt, wt, b, s, bt_ds, tril)
        o_mb = _moba_attention(zz, vt, b, s, bt_mb)
        xf = _merge(xf, o_sb, o_df, o_ds, o_mb, g_pre_mix[l][None, :], w_gate, w_br, w_o, g_post_mix[l][None, :], l)
        xf = _ffn(xf, g_pre_ffn[l][None, :], w_f1, w_f2, g_post_ffn[l][None, :], l)
    return xf.reshape(b, s, d)
```

```python
import functools
import math

import numpy as np
import jax
import jax.numpy as jnp
from jax import lax
from jax.experimental import pallas as pl
from jax.experimental.pallas import tpu as pltpu

F32 = jnp.float32
BF16 = jnp.bfloat16

D_MODEL = 1024
HEAD_DIM = 64
N_HEADS = 4
N_MIXERS = 4
MIXER_WIDTH = N_HEADS * HEAD_DIM
DIFF_QK_DIM = 32
IDX_HEADS = 8
DSA_TOPK_MAX = 256
MOBA_BLOCK = 256
MOBA_TOPK = 3
N_BUCKETS = 32
MAX_DISTANCE = 128
D_FF = 2816
NORM_EPS = 1e-6

TQ = 256
TK = 256
NEG = -1e30
HALF_NEG = -0.5e30
BIG = 3e38
BISECT_WARMUP = 20
BISECT_TRIP = 2
BISECT_MAX_TRIPS = 134
SB_TAIL_CUTOFF = 120.0
LOG2E = math.log2(math.e)
V_ROWS = 80
TM_MERGE = 512
TM_FFN = 512
N_GROUP = 11
N_SLAB = 4 * N_GROUP
N_PACK = N_SLAB * HEAD_DIM
VMEM_LIMIT = 56 * 1024 * 1024

G_SB_Q, G_SB_K, G_DF_Q, G_DF_K, G_DS_Q, G_DS_K, G_QI_A, G_QI_B, G_MB_Q, G_MB_K, G_KIDX = range(N_GROUP)
S_KIDX = 4 * G_KIDX
M_SB, M_DF, M_DS, M_MB = range(N_MIXERS)


def _layout():
    off = {}
    acc = 0
    for name, sz in (("q_sb", 256), ("k_sb", 256), ("v_sb", 256), ("q1", 128), ("q2", 128), ("k1", 128),
                     ("k2", 128), ("v_df", 256), ("q_ds", 256), ("k_ds", 256), ("v_ds", 256), ("qi", 512),
                     ("ki", 64), ("wi", 8), ("q_mb", 256), ("k_mb", 256), ("v_mb", 256), ("gate", 4096)):
        off[name] = acc
        acc += sz
    return off


_OFF = _layout()


def _pack_layout():
    off = _OFF
    cols, scale = [], []

    def add(start, n, s=1.0):
        cols.extend(range(start, start + n))
        scale.extend([s] * n)

    hd = HEAD_DIM ** -0.5
    hd2 = hd * LOG2E
    df2 = DIFF_QK_DIM ** -0.5 * LOG2E
    add(off["q_sb"], 256, hd); add(off["k_sb"], 256)
    for h in range(N_HEADS):
        add(off["q1"] + h * 32, 32, df2); add(off["q2"] + h * 32, 32, df2)
    for h in range(N_HEADS):
        add(off["k1"] + h * 32, 32); add(off["k2"] + h * 32, 32)
    add(off["q_ds"], 256, hd2); add(off["k_ds"], 256)
    add(off["qi"], 512, HEAD_DIM ** -0.5)
    add(off["q_mb"], 256, hd2); add(off["k_mb"], 256)
    add(off["ki"], 64)
    cols.extend([-1] * 192); scale.extend([1.0] * 192)
    assert len(cols) == N_PACK
    return np.asarray(cols, np.int32), np.asarray(scale, np.float32)


_PACK_SRC, _PACK_SCALE = _pack_layout()


def _dot(a, b):
    return jnp.dot(a, b, preferred_element_type=F32)


def _dot_nt(a, b):
    return lax.dot_general(a, b, (((1,), (1,)), ((), ())), preferred_element_type=F32)


def _rms(x, g):
    return x * lax.rsqrt(jnp.mean(x * x, axis=-1, keepdims=True) + NORM_EPS) * g


def _cparams(n_axes):
    return pltpu.CompilerParams(dimension_semantics=("arbitrary",) * n_axes, vmem_limit_bytes=VMEM_LIMIT)


def _const_spec(shape):
    nd = len(shape)
    return pl.BlockSpec(shape, lambda *_: (0,) * nd, pipeline_mode=pl.Buffered(1))


def _layer_spec(shape, layer):
    nd = len(shape)
    return pl.BlockSpec((pl.Squeezed(),) + tuple(shape), lambda *_: (layer,) + (0,) * nd, pipeline_mode=pl.Buffered(1))


def _proj_kernel(x_ref, g_ref, w_ref, cs_ref, wvt_ref, wit_ref, zz_ref, vt_ref, wt_ref):
    h = _rms(x_ref[...], g_ref[...]).astype(BF16)
    for c in range(N_GROUP):
        cols = slice(c * MIXER_WIDTH, (c + 1) * MIXER_WIDTH)
        r = _dot(h, w_ref[:, cols]) * cs_ref[:, cols]
        for s in range(4):
            zz_ref[4 * c + s] = r[:, s * HEAD_DIM:(s + 1) * HEAD_DIM].astype(BF16)
    for m in range(N_MIXERS):
        vt = _dot_nt(wvt_ref[m], h).astype(BF16)
        for hh in range(N_HEADS):
            vt_ref[m, 0, hh * V_ROWS:hh * V_ROWS + HEAD_DIM, :] = vt[hh * HEAD_DIM:(hh + 1) * HEAD_DIM]
            vt_ref[m, 0, hh * V_ROWS + HEAD_DIM:(hh + 1) * V_ROWS, :] = jnp.ones((V_ROWS - HEAD_DIM, TQ), BF16)
    wt_ref[...] = _dot_nt(wit_ref[...], h) * IDX_HEADS ** -0.5


def _proj(x, g, w, cs, wvt, wit, layer):
    t = x.shape[0]
    tm = TQ
    return pl.pallas_call(
        _proj_kernel,
        grid=(t // tm,),
        in_specs=[pl.BlockSpec((tm, D_MODEL), lambda i: (i, 0)),
                  _const_spec((1, D_MODEL)),
                  _layer_spec((D_MODEL, N_PACK), layer),
                  _const_spec((1, N_PACK)),
                  _layer_spec((N_MIXERS, MIXER_WIDTH, D_MODEL), layer),
                  _layer_spec((IDX_HEADS, D_MODEL), layer)],
        out_specs=[pl.BlockSpec((N_SLAB, tm, HEAD_DIM), lambda i: (0, i, 0)),
                   pl.BlockSpec((N_MIXERS, 1, N_HEADS * V_ROWS, tm), lambda i: (0, i, 0, 0)),
                   pl.BlockSpec((IDX_HEADS, tm), lambda i: (0, i))],
        out_shape=[jax.ShapeDtypeStruct((N_SLAB, t, HEAD_DIM), BF16),
                   jax.ShapeDtypeStruct((N_MIXERS, t // tm, N_HEADS * V_ROWS, tm), BF16),
                   jax.ShapeDtypeStruct((IDX_HEADS, t), F32)],
        compiler_params=_cparams(1),
        name="proj",
    )(x, g, w, cs, wvt, wit)


def _k_block(ref, h, j):
    return ref[h, pl.ds(pl.multiple_of(j * TK, TK), TK), :]


def _fold_keys(a, op):
    n = a.shape[0]
    while n > 8:
        n //= 2
        a = op(a[:n], a[n:2 * n])
    return a


def _vt_block(ref, h, j, rows=V_ROWS):
    return ref[0, j, h * V_ROWS:h * V_ROWS + rows, :]


def _softmax_block(s_list, vt_list, carry):
    ms = [jnp.maximum(c[0], jnp.max(_fold_keys(s, jnp.maximum), axis=0, keepdims=True))
          for s, c in zip(s_list, carry)]
    pvs = [_dot(vt, jnp.exp2(s - m).astype(BF16)) for vt, s, m in zip(vt_list, s_list, ms)]
    return tuple((m_new, jnp.exp2(m - m_new) * acc + pv) for (m, acc), m_new, pv in zip(carry, ms, pvs))


def _softmax_loop(n_blocks, logits, values, carry):
    def pair(t, carry):
        j = 2 * t
        s_a, s_b = logits(j), logits(j + 1)
        return _softmax_block(s_b, values(j + 1), _softmax_block(s_a, values(j), carry))

    def single(j, carry):
        return _softmax_block(logits(j), values(j), carry)

    n_pairs = n_blocks // 2
    carry = lax.fori_loop(0, n_pairs, pair, carry)
    return lax.fori_loop(2 * n_pairs, n_blocks, single, carry)


def _softmax_init():
    return (jnp.full((1, TQ), NEG, F32), jnp.zeros((V_ROWS, TQ), F32))


def _softmax_out(carry):
    _, acc = carry
    return acc[:HEAD_DIM] / acc[HEAD_DIM:HEAD_DIM + 1]


def _store_heads(o_ref, heads_t):
    o_ref[...] = jnp.transpose(jnp.concatenate(heads_t, axis=0)).astype(BF16)


BIAS_ROWS_SHAPE = (N_HEADS, 3, 1, 2 * TQ)
BIAS_TILES_SHAPE = (N_HEADS, 3, TK, TQ)


def _fill_bias_tiles(bw_ref, bt_ref):
    @pl.when((pl.program_id(0) == 0) & (pl.program_id(1) == 0))
    def _():
        for h in range(N_HEADS):
            for o in range(3):
                rows = jnp.broadcast_to(bw_ref[h, o], (TK, 2 * TQ))
                bt_ref[h, o] = pltpu.roll(rows, TQ + 1, 1, stride=1, stride_axis=0)[:, :TQ]


def _attn_specs(nq, gq, gk, mixer, s):
    return [pl.BlockSpec((4, TQ, HEAD_DIM), lambda b, i: (gq, b * nq + i, 0)),
            pl.BlockSpec((4, s, HEAD_DIM), lambda b, i: (gk, b, 0)),
            pl.BlockSpec((1, s // TK, N_HEADS * V_ROWS, TK), lambda b, i: (mixer, b, 0, 0))]


def _out_spec(nq):
    return pl.BlockSpec((TQ, N_HEADS * HEAD_DIM), lambda b, i: (b * nq + i, 0))


def _sb_kernel(q_ref, k_ref, vt_ref, tri_ref, mask_ref, o_ref):
    i = pl.program_id(1)
    tri = tri_ref[...]

    def block(j, carry, masked):
        heads = range(N_HEADS)
        zs = [_dot_nt(_k_block(k_ref, h, j), q_ref[h]) for h in heads]
        if masked:
            zs = [z + mask_ref[...] for z in zs]
        sps = [jnp.maximum(z, 0.0) + jnp.log(1.0 + jnp.exp(-jnp.abs(z))) for z in zs]
        his = [sp.astype(BF16) for sp in sps]
        los = [(sp - hi.astype(F32)).astype(BF16) for sp, hi in zip(sps, his)]
        cs = [_dot(tri, hi) + _dot(tri, lo) + c[0] for hi, lo, c in zip(his, los, carry)]
        avs = [_dot(_vt_block(vt_ref, h, j, HEAD_DIM), jnp.exp(z - c).astype(BF16))
               for h, z, c in zip(heads, zs, cs)]
        return tuple((c[0:1, :], old[1] + av) for c, old, av in zip(cs, carry, avs))

    init = tuple((jnp.zeros((1, TQ), F32), jnp.zeros((HEAD_DIM, TQ), F32)) for _ in range(N_HEADS))
    carry = block(i, init, True)

    def weights_left(carry):
        tail = functools.reduce(jnp.minimum, [c[0] for c in carry])
        return (jnp.min(tail) < SB_TAIL_CUTOFF).astype(jnp.int32)

    def earlier_block(c):
        j, carry, _ = c
        carry = block(j, carry, False)
        return j - 1, carry, weights_left(carry)

    _, carry, _ = lax.while_loop(lambda c: (c[0] >= 0) & (c[2] > 0), earlier_block, (i - 1, carry, weights_left(carry)))
    _store_heads(o_ref, [c[1] for c in carry])


def _sb_attention(zz, vt, b, s, tri, mask):
    nq = s // TQ
    return pl.pallas_call(
        _sb_kernel,
        grid=(b, nq),
        in_specs=_attn_specs(nq, G_SB_Q, G_SB_K, M_SB, s) + [_const_spec((TK, TK)), _const_spec((TK, TQ))],
        out_specs=_out_spec(nq),
        out_shape=jax.ShapeDtypeStruct((b * s, N_HEADS * HEAD_DIM), BF16),
        compiler_params=_cparams(2),
        name="sb_attn",
    )(zz, zz, vt, tri, mask)


def _diff_kernel(q_ref, k_ref, vt_ref, bw_ref, lam_ref, cst_ref, g_ref, o_ref, bt_ref):
    i = pl.program_id(1)
    _fill_bias_tiles(bw_ref, bt_ref)
    lp = lam_ref[...]
    lam_init = cst_ref[:, 0:1]
    lam = (jnp.exp(jnp.sum(lp[0:1] * lp[1:2], axis=-1, keepdims=True))
           - jnp.exp(jnp.sum(lp[2:3] * lp[3:4], axis=-1, keepdims=True)) + lam_init)
    lane = lax.broadcasted_iota(jnp.int32, (TQ, HEAD_DIM), 1)
    qs = []
    for h in range(N_HEADS):
        q = q_ref[h]
        qs.append((jnp.where(lane < DIFF_QK_DIM, q, jnp.zeros_like(q)),
                   jnp.where(lane >= DIFF_QK_DIM, q, jnp.zeros_like(q))))

    def logits(j):
        s_list = []
        for h in range(N_HEADS):
            kj = _k_block(k_ref, h, j)
            bias = bt_ref[h, jnp.minimum(i - j, 2)]
            s_list += [_dot_nt(kj, qs[h][0]) + bias, _dot_nt(kj, qs[h][1]) + bias]
        return s_list

    def values(j):
        return [_vt_block(vt_ref, h, j) for h in range(N_HEADS) for _ in range(2)]

    carry = _softmax_loop(i + 1, logits, values, tuple(_softmax_init() for _ in range(2 * N_HEADS)))
    outs = []
    for h in range(N_HEADS):
        o = _softmax_out(carry[2 * h]) - lam * _softmax_out(carry[2 * h + 1])
        o = o * lax.rsqrt(jnp.mean(o * o, axis=0, keepdims=True) + NORM_EPS) * g_ref[...]
        outs.append(o * (1.0 - lam_init))
    _store_heads(o_ref, outs)


def _diff_attention(zz, vt, b, s, bw, lamp, cst, g):
    nq = s // TQ
    return pl.pallas_call(
        _diff_kernel,
        grid=(b, nq),
        in_specs=_attn_specs(nq, G_DF_Q, G_DF_K, M_DF, s) + [
            _const_spec(BIAS_ROWS_SHAPE), _const_spec((4, DIFF_QK_DIM)), _const_spec((1, 128)),
            _const_spec((HEAD_DIM, 1))],
        out_specs=_out_spec(nq),
        out_shape=jax.ShapeDtypeStruct((b * s, N_HEADS * HEAD_DIM), BF16),
        scratch_shapes=[pltpu.VMEM(BIAS_TILES_SHAPE, F32)],
        compiler_params=_cparams(2),
        name="diff_attn",
    )(zz, zz, vt, bw, lamp, cst, g)


def _dsa_kernel(q_ref, k_ref, vt_ref, qia_ref, qib_ref, ki_ref, wt_ref, bw_ref, tril_ref, o_ref, sc_ref, bt_ref, *,
                topk):
    i = pl.program_id(1)
    _fill_bias_tiles(bw_ref, bt_ref)
    nb = i + 1
    kf = float(topk)
    w = wt_ref[...]
    key = lax.broadcasted_iota(jnp.int32, (TK, TQ), 0)
    qry = lax.broadcasted_iota(jnp.int32, (TK, TQ), 1)

    def score(j):
        kij = _k_block(ki_ref, 0, j)
        sc = jnp.zeros((TK, TQ), F32)
        for hh in range(IDX_HEADS):
            qi = (qia_ref if hh < 4 else qib_ref)[hh % 4]
            sc = sc + w[hh:hh + 1, :] * jnp.maximum(_dot_nt(kij, qi), 0.0)
        return sc

    def extend(lo_src, hi_src, c):
        return (jnp.minimum(c[0], _fold_keys(lo_src, jnp.minimum)), jnp.maximum(c[1], _fold_keys(hi_src, jnp.maximum)))

    def earlier_block(j, c):
        sc = score(j)
        sc_ref[j] = sc
        return extend(sc, sc, c)

    def earlier_pair(t, c):
        sc_a, sc_b = score(2 * t), score(2 * t + 1)
        sc_ref[2 * t] = sc_a
        sc_ref[2 * t + 1] = sc_b
        return extend(sc_b, sc_b, extend(sc_a, sc_a, c))

    lo_part, hi_part = lax.fori_loop(0, i // 2, earlier_pair,
                                     (jnp.full((8, TQ), BIG, F32), jnp.full((8, TQ), -BIG, F32)))
    lo_part, hi_part = lax.fori_loop(2 * (i // 2), i, earlier_block, (lo_part, hi_part))
    sc = score(i)
    causal = key <= qry
    sc_ref[i] = jnp.where(causal, sc, NEG)
    lo_part, hi_part = extend(jnp.where(causal, sc, BIG), jnp.where(causal, sc, NEG), (lo_part, hi_part))

    def reduce_blocks(fn, init):
        def pair(t, c):
            return fn(sc_ref[2 * t + 1], 2 * t + 1, fn(sc_ref[2 * t], 2 * t, c))
        c = lax.fori_loop(0, nb // 2, pair, init)
        return lax.fori_loop(2 * (nb // 2), nb, lambda j, c: fn(sc_ref[j], j, c), c)

    def count_ge(t):
        part = reduce_blocks(lambda x, j, c: c + _fold_keys(jnp.where(x >= t, 1.0, 0.0), jnp.add),
                             jnp.zeros((8, TQ), F32))
        return jnp.sum(part, axis=0, keepdims=True)

    def minmax_blocks(lo_of, hi_of):
        def f(x, j, c):
            return (jnp.minimum(c[0], _fold_keys(lo_of(x), jnp.minimum)),
                    jnp.maximum(c[1], _fold_keys(hi_of(x), jnp.maximum)))
        lo_part, hi_part = reduce_blocks(f, (jnp.full((8, TQ), BIG, F32), jnp.full((8, TQ), -BIG, F32)))
        return jnp.min(lo_part, axis=0, keepdims=True), jnp.max(hi_part, axis=0, keepdims=True)

    n_valid = i * TQ + lax.broadcasted_iota(jnp.int32, (1, TQ), 1) + 1
    take_all = n_valid <= topk
    lo = jnp.min(lo_part, axis=0, keepdims=True)
    hi = jnp.max(hi_part, axis=0, keepdims=True)
    c_max = count_ge(hi)
    at_max = c_max >= kf
    state = (jnp.where(at_max, hi, lo), hi, jnp.where(at_max, c_max, n_valid.astype(F32)), c_max)

    def bisect(_, state):
        lo, hi, c_lo, c_hi = state
        mid = 0.5 * lo + 0.5 * hi
        c = count_ge(mid)
        ge = c >= kf
        return jnp.where(ge, mid, lo), jnp.where(ge, hi, mid), jnp.where(ge, c, c_lo), jnp.where(ge, c_hi, c)

    def unsettled(state):
        lo, hi, c_lo, _ = state
        open_q = jnp.where(take_all, 0.0, jnp.where(c_lo != kf, jnp.where(lo < hi, 1.0, 0.0), 0.0))

        def band_spread():
            b_min, b_max = minmax_blocks(lambda x: jnp.where(x >= lo, jnp.where(x < hi, x, BIG), BIG),
                                         lambda x: jnp.where(x >= lo, jnp.where(x < hi, x, -BIG), -BIG))
            return (jnp.max(jnp.where(b_max != b_min, open_q, 0.0)) > 0.0).astype(jnp.int32)

        return lax.cond(jnp.max(open_q) > 0.0, band_spread, lambda: jnp.int32(0))

    state = lax.fori_loop(0, jnp.where((i + 1) * TQ <= topk, 0, BISECT_WARMUP), bisect, state)

    def trip(c):
        n, state, _ = c
        state = lax.fori_loop(0, BISECT_TRIP, bisect, state)
        return n + 1, state, unsettled(state)

    _, state, _ = lax.while_loop(lambda c: (c[2] > 0) & (c[0] < BISECT_MAX_TRIPS), trip,
                                 (jnp.int32(0), state, unsettled(state)))
    lo, hi, c_lo, c_hi = state
    hi_ok = lo < hi
    c_above = jnp.where(hi_ok, c_hi, 0.0)
    hi_sel = jnp.where(hi_ok, hi, BIG)
    need = jnp.where(take_all, BIG, kf - c_above)
    lo_sel = jnp.where(take_all, HALF_NEG, lo)

    tied = jnp.max(jnp.where(take_all, 0.0, c_lo - kf)) > 0.0

    @pl.when(tied)
    def _():
        tril = tril_ref[...]

        def band_of(j):
            x = sc_ref[j]
            return x, jnp.where(x >= lo_sel, jnp.where(x < hi_sel, 1.0, 0.0), 0.0)

        def write(j, x, band, rank):
            sc_ref[j] = jnp.where(x >= hi_sel, 0.0,
                                  jnp.where(band * rank > 0.0, jnp.where(rank <= need, 0.0, NEG), NEG))
            return rank[TK - 1:TK, :]

        def write_mask(j, taken):
            x, band = band_of(j)
            return write(j, x, band, _dot(tril, band.astype(BF16)) + taken)

        def write_mask_pair(t, taken):
            (x_a, band_a), (x_b, band_b) = band_of(2 * t), band_of(2 * t + 1)
            in_a, in_b = _dot(tril, band_a.astype(BF16)), _dot(tril, band_b.astype(BF16))
            taken = write(2 * t, x_a, band_a, in_a + taken)
            return write(2 * t + 1, x_b, band_b, in_b + taken)

        taken = lax.fori_loop(0, nb // 2, write_mask_pair, jnp.zeros((1, TQ), F32))
        lax.fori_loop(2 * (nb // 2), nb, write_mask, taken)

    @pl.when(jnp.logical_not(tied))
    def _():
        def write_mask(j, _):
            sc_ref[j] = jnp.where(sc_ref[j] >= lo_sel, 0.0, NEG)
            return 0

        lax.fori_loop(0, nb, write_mask, 0)

    def logits(j):
        return [_dot_nt(_k_block(k_ref, h, j), q_ref[h]) + bt_ref[h, jnp.minimum(i - j, 2)] + sc_ref[j]
                for h in range(N_HEADS)]

    carry = _softmax_loop(nb, logits, lambda j: [_vt_block(vt_ref, h, j) for h in range(N_HEADS)],
                          tuple(_softmax_init() for _ in range(N_HEADS)))
    _store_heads(o_ref, [_softmax_out(c) for c in carry])


def _dsa_attention(zz, vt, wt, b, s, bw, tril):
    nq = s // TQ
    topk = min(DSA_TOPK_MAX, s // 4)
    return pl.pallas_call(
        functools.partial(_dsa_kernel, topk=topk),
        grid=(b, nq),
        in_specs=_attn_specs(nq, G_DS_Q, G_DS_K, M_DS, s) + [
            pl.BlockSpec((4, TQ, HEAD_DIM), lambda b_, i: (G_QI_A, b_ * nq + i, 0)),
            pl.BlockSpec((4, TQ, HEAD_DIM), lambda b_, i: (G_QI_B, b_ * nq + i, 0)),
            pl.BlockSpec((1, s, HEAD_DIM), lambda b_, i: (S_KIDX, b_, 0)),
            pl.BlockSpec((IDX_HEADS, TQ), lambda b_, i: (0, b_ * nq + i)),
            _const_spec(BIAS_ROWS_SHAPE), _const_spec((TK, TK))],
        out_specs=_out_spec(nq),
        out_shape=jax.ShapeDtypeStruct((b * s, N_HEADS * HEAD_DIM), BF16),
        scratch_shapes=[pltpu.VMEM((nq, TK, TQ), F32), pltpu.VMEM(BIAS_TILES_SHAPE, F32)],
        compiler_params=_cparams(2),
        name="dsa_attn",
    )(zz, zz, vt, zz, zz, zz, wt, bw, tril)


def _moba_kernel(q_ref, k_ref, vt_ref, bw_ref, o_ref, km_ref, bt_ref, *, nblk, topb):
    i = pl.program_id(1)
    _fill_bias_tiles(bw_ref, bt_ref)
    nrow = km_ref.shape[1]

    @pl.when(i == 0)
    def _():
        km_ref[...] = jnp.zeros_like(km_ref)
        for h in range(N_HEADS):
            for n in range(nblk):
                kb = k_ref[h, n * MOBA_BLOCK:(n + 1) * MOBA_BLOCK, :].astype(F32)
                km_ref[h, n:n + 1, :] = jnp.mean(kb, axis=0, keepdims=True)

    blk = lax.broadcasted_iota(jnp.int32, (nrow, TQ), 0)
    past = blk < i
    head_bits = []
    for h in range(N_HEADS):
        gate = _dot_nt(km_ref[h].astype(BF16), q_ref[h])
        bits = jnp.zeros((1, TQ), F32)
        for n in range(nblk):
            gn = gate[n:n + 1, :]
            beats = jnp.where(past, jnp.where(gate > gn, 1.0, jnp.where(gate == gn, jnp.where(blk < n, 1.0, 0.0), 0.0)), 0.0)
            rank = jnp.sum(beats, axis=0, keepdims=True)
            bits = bits + jnp.where(rank < float(topb), jnp.where(n < i, float(2 ** n), 0.0), 0.0)
        head_bits.append(bits.astype(jnp.int32) | lax.shift_left(jnp.int32(1), i))

    def values(n):
        return [_vt_block(vt_ref, h, n) for h in range(N_HEADS)]

    def logits(n):
        s_list = []
        for h in range(N_HEADS):
            picked = (lax.shift_right_logical(head_bits[h], jnp.full_like(head_bits[h], n)) & 1) == 1
            s_list.append(_dot_nt(_k_block(k_ref, h, n), q_ref[h]) + bt_ref[h, jnp.minimum(i - n, 2)]
                          + jnp.where(picked, 0.0, NEG))
        return s_list

    carry = _softmax_loop(i + 1, logits, values, tuple(_softmax_init() for _ in range(N_HEADS)))
    _store_heads(o_ref, [_softmax_out(c) for c in carry])


def _moba_attention(zz, vt, b, s, bw):
    nq = s // TQ
    nblk = s // MOBA_BLOCK
    topb = min(MOBA_TOPK, nblk - 1)
    return pl.pallas_call(
        functools.partial(_moba_kernel, nblk=nblk, topb=topb),
        grid=(b, nq),
        in_specs=_attn_specs(nq, G_MB_Q, G_MB_K, M_MB, s) + [_const_spec(BIAS_ROWS_SHAPE)],
        out_specs=_out_spec(nq),
        out_shape=jax.ShapeDtypeStruct((b * s, N_HEADS * HEAD_DIM), BF16),
        scratch_shapes=[pltpu.VMEM((N_HEADS, max(8, nblk), HEAD_DIM), F32), pltpu.VMEM(BIAS_TILES_SHAPE, F32)],
        compiler_params=_cparams(2),
        name="moba_attn",
    )(zz, zz, vt, bw)


def _merge_kernel(x_ref, osb_ref, odf_ref, ods_ref, omb_ref, gpre_ref, wg_ref, wbr_ref, wout_ref, gpost_ref, o_ref):
    x = x_ref[...]
    h = _rms(x, gpre_ref[...]).astype(BF16)
    y = jnp.zeros((x.shape[0], D_MODEL), F32)
    for r, o_r in enumerate((osb_ref, odf_ref, ods_ref, omb_ref)):
        gate = jax.nn.sigmoid(_dot(h, wg_ref[:, r * D_MODEL:(r + 1) * D_MODEL]))
        y = y + gate * _dot(o_r[...], wbr_ref[r])
    o_ref[...] = x + _rms(_dot(y.astype(BF16), wout_ref[...]), gpost_ref[...])


def _merge(x, o_sb, o_df, o_ds, o_mb, g_pre, w_gate, w_br, w_out, g_post, layer):
    t = x.shape[0]
    tm = TM_MERGE
    tok = lambda width: pl.BlockSpec((tm, width), lambda i: (i, 0))
    return pl.pallas_call(
        _merge_kernel,
        grid=(t // tm,),
        in_specs=[tok(D_MODEL)] + [tok(MIXER_WIDTH)] * N_MIXERS + [
            _const_spec((1, D_MODEL)), _layer_spec((D_MODEL, N_MIXERS * D_MODEL), layer),
            _layer_spec((N_MIXERS, MIXER_WIDTH, D_MODEL), layer), _layer_spec((D_MODEL, D_MODEL), layer),
            _const_spec((1, D_MODEL))],
        out_specs=tok(D_MODEL),
        out_shape=jax.ShapeDtypeStruct((t, D_MODEL), F32),
        compiler_params=_cparams(1),
        name="merge",
    )(x, o_sb, o_df, o_ds, o_mb, g_pre, w_gate, w_br, w_out, g_post)


def _ffn_kernel(x_ref, gpre_ref, win_ref, wout_ref, gpost_ref, o_ref):
    x = x_ref[...]
    h = _rms(x, gpre_ref[...]).astype(BF16)
    gate = _dot(h, win_ref[:, 0:D_FF])
    up = _dot(h, win_ref[:, D_FF:2 * D_FF])
    act = (gate * jax.nn.sigmoid(gate) * up).astype(BF16)
    o_ref[...] = x + _rms(_dot(act, wout_ref[...]), gpost_ref[...])


def _ffn(x, g_pre, w_in, w_out, g_post, layer):
    t = x.shape[0]
    tm = TM_FFN
    return pl.pallas_call(
        _ffn_kernel,
        grid=(t // tm,),
        in_specs=[pl.BlockSpec((tm, D_MODEL), lambda i: (i, 0)), _const_spec((1, D_MODEL)),
                  _layer_spec((D_MODEL, 2 * D_FF), layer), _layer_spec((D_FF, D_MODEL), layer),
                  _const_spec((1, D_MODEL))],
        out_specs=pl.BlockSpec((tm, D_MODEL), lambda i: (i, 0)),
        out_shape=jax.ShapeDtypeStruct((t, D_MODEL), F32),
        compiler_params=_cparams(1),
        name="ffn",
    )(x, g_pre, w_in, w_out, g_post)


def _t5_bucket(dist):
    max_exact = N_BUCKETS // 2
    d = jnp.maximum(dist, 0)
    log_ratio = jnp.log(jnp.maximum(d, 1).astype(F32) / max_exact) / math.log(MAX_DISTANCE / max_exact)
    large = jnp.minimum(max_exact + (log_ratio * (N_BUCKETS - max_exact)).astype(jnp.int32), N_BUCKETS - 1)
    return jnp.where(d < max_exact, d, large)


def _bias_rows(rel_bias):
    assert TQ == TK
    n = TK
    d = np.arange(-(n - 1), 3 * n + 1)
    by_dist = rel_bias.astype(F32).T[:, _t5_bucket(jnp.asarray(np.maximum(d, 0), jnp.int32))]
    by_dist = jnp.where(jnp.asarray(d >= 0)[None, :], by_dist * LOG2E, NEG)
    return jnp.stack([by_dist[:, o * n:o * n + 2 * n] for o in range(3)], axis=1)[:, :, None, :]


def _pack_weights(w_in):
    parts, start = [], 0
    for e in range(1, N_PACK + 1):
        if e == N_PACK or _PACK_SRC[e] != _PACK_SRC[e - 1] + (1 if _PACK_SRC[e - 1] >= 0 else 0):
            a = int(_PACK_SRC[start])
            n = e - start
            parts.append(w_in[:, :, a:a + n] if a >= 0 else jnp.zeros(w_in.shape[:2] + (n,), w_in.dtype))
            start = e
    return jnp.concatenate(parts, axis=2).astype(BF16)


def _transposed_weights(w_in):
    wv = jnp.stack([w_in[:, :, _OFF[n]:_OFF[n] + MIXER_WIDTH] for n in ("v_sb", "v_df", "v_ds", "v_mb")], axis=1)
    wi = w_in[:, :, _OFF["wi"]:_OFF["wi"] + IDX_HEADS]
    return jnp.swapaxes(wv, 2, 3).astype(BF16), jnp.swapaxes(wi, 1, 2).astype(BF16)


def kernel(x, w_in, w_br_sb, w_br_diff, w_br_dsa, w_br_moba, w_out, lambda_q1, lambda_k1, lambda_q2, lambda_k2,
           diff_subln_g, rel_bias, w_ffn_in, w_ffn_out, g_pre_mix, g_post_mix, g_pre_ffn, g_post_ffn):
    b, s, d = x.shape
    depth = w_in.shape[0]
    assert d == D_MODEL and s % TQ == 0 and s // MOBA_BLOCK >= 2
    t = b * s

    w_pack = _pack_weights(w_in)
    w_vt, w_it = _transposed_weights(w_in)
    w_gate = w_in.astype(BF16)[:, :, _OFF["gate"]:]
    w_br = jnp.stack([w_br_sb, w_br_diff, w_br_dsa, w_br_moba], axis=1).astype(BF16)
    w_o = w_out.astype(BF16)
    w_f1 = w_ffn_in.astype(BF16)
    w_f2 = w_ffn_out.astype(BF16)
    cs = jnp.asarray(_PACK_SCALE)[None, :]

    bw = _bias_rows(rel_bias)
    bw_df, bw_ds, bw_mb = bw[0:4], bw[4:8], bw[8:12]
    key = np.arange(TK)[:, None]
    qry = np.arange(TQ)[None, :]
    tri = jnp.asarray(key <= np.arange(TK)[None, :], BF16)
    tril = jnp.asarray(key >= np.arange(TK)[None, :], BF16)
    sb_mask = jnp.asarray(np.where(key < qry, 0.0, NEG), F32)

    xf = x.reshape(t, d)
    for l in range(depth):
        lam_init = 0.8 - 0.6 * math.exp(-0.3 * l)
        lamp = jnp.stack([lambda_q1[l], lambda_k1[l], lambda_q2[l], lambda_k2[l]]).astype(F32)
        cst = jnp.full((1, 128), lam_init, F32)
        zz, vt, wt = _proj(xf, g_pre_mix[l][None, :], w_pack, cs, w_vt, w_it, l)
        o_sb = _sb_attention(zz, vt, b, s, tri, sb_mask)
        o_df = _diff_attention(zz, vt, b, s, bw_df, lamp, cst, diff_subln_g[l][:, None])
        o_ds = _dsa_attention(zz, vt, wt, b, s, bw_ds, tril)
        o_mb = _moba_attention(zz, vt, b, s, bw_mb)
        xf = _merge(xf, o_sb, o_df, o_ds, o_mb, g_pre_mix[l][None, :], w_gate, w_br, w_o, g_post_mix[l][None, :], l)
        xf = _ffn(xf, g_pre_ffn[l][None, :], w_f1, w_f2, g_post_ffn[l][None, :], l)
    return xf.reshape(b, s, d)
```

```python
import functools
import math

import numpy as np
import jax
import jax.numpy as jnp
from jax import lax
from jax.experimental import pallas as pl
from jax.experimental.pallas import tpu as pltpu

F32 = jnp.float32
BF16 = jnp.bfloat16

D_MODEL = 1024
HEAD_DIM = 64
N_HEADS = 4
N_MIXERS = 4
MIXER_WIDTH = N_HEADS * HEAD_DIM
DIFF_QK_DIM = 32
IDX_HEADS = 8
DSA_TOPK_MAX = 256
MOBA_BLOCK = 256
MOBA_TOPK = 3
N_BUCKETS = 32
MAX_DISTANCE = 128
D_FF = 2816
NORM_EPS = 1e-6

TQ = 256
TK = 256
NEG = -1e30
HALF_NEG = -0.5e30
BIG = 3e38
BISECT_WARMUP = 20
BISECT_TRIP = 2
BISECT_MAX_TRIPS = 134
SB_TAIL_CUTOFF = 120.0
LOG2E = math.log2(math.e)
V_ROWS = 80
TM_MERGE = 512
TM_FFN = 512
N_GROUP = 11
N_SLAB = 4 * N_GROUP
N_PACK = (N_GROUP + N_MIXERS) * MIXER_WIDTH
VMEM_LIMIT = 56 * 1024 * 1024

G_SB_Q, G_SB_K, G_DF_Q, G_DF_K, G_DS_Q, G_DS_K, G_QI_A, G_QI_B, G_MB_Q, G_MB_K, G_KIDX = range(N_GROUP)
S_KIDX = 4 * G_KIDX
M_SB, M_DF, M_DS, M_MB = range(N_MIXERS)


def _layout():
    off = {}
    acc = 0
    for name, sz in (("q_sb", 256), ("k_sb", 256), ("v_sb", 256), ("q1", 128), ("q2", 128), ("k1", 128),
                     ("k2", 128), ("v_df", 256), ("q_ds", 256), ("k_ds", 256), ("v_ds", 256), ("qi", 512),
                     ("ki", 64), ("wi", 8), ("q_mb", 256), ("k_mb", 256), ("v_mb", 256), ("gate", 4096)):
        off[name] = acc
        acc += sz
    return off


_OFF = _layout()


def _pack_layout():
    off = _OFF
    cols, scale = [], []

    def add(start, n, s=1.0):
        cols.extend(range(start, start + n))
        scale.extend([s] * n)

    hd = HEAD_DIM ** -0.5
    hd2 = hd * LOG2E
    df2 = DIFF_QK_DIM ** -0.5 * LOG2E
    add(off["q_sb"], 256, hd); add(off["k_sb"], 256)
    for h in range(N_HEADS):
        add(off["q1"] + h * 32, 32, df2); add(off["q2"] + h * 32, 32, df2)
    for h in range(N_HEADS):
        add(off["k1"] + h * 32, 32); add(off["k2"] + h * 32, 32)
    add(off["q_ds"], 256, hd2); add(off["k_ds"], 256)
    add(off["qi"], 512, HEAD_DIM ** -0.5)
    add(off["q_mb"], 256, hd2); add(off["k_mb"], 256)
    add(off["ki"], 64); add(off["wi"], IDX_HEADS, IDX_HEADS ** -0.5)
    cols.extend([-1] * 184); scale.extend([1.0] * 184)
    for name in ("v_sb", "v_df", "v_ds", "v_mb"):
        add(off[name], MIXER_WIDTH)
    assert len(cols) == N_PACK
    return np.asarray(cols, np.int32), np.asarray(scale, np.float32)


_PACK_SRC, _PACK_SCALE = _pack_layout()


def _dot(a, b):
    return jnp.dot(a, b, preferred_element_type=F32)


def _dot_nt(a, b):
    return lax.dot_general(a, b, (((1,), (1,)), ((), ())), preferred_element_type=F32)


def _rms(x, g):
    return x * lax.rsqrt(jnp.mean(x * x, axis=-1, keepdims=True) + NORM_EPS) * g


def _cparams(n_axes):
    return pltpu.CompilerParams(dimension_semantics=("arbitrary",) * n_axes, vmem_limit_bytes=VMEM_LIMIT)


def _const_spec(shape):
    nd = len(shape)
    return pl.BlockSpec(shape, lambda *_: (0,) * nd, pipeline_mode=pl.Buffered(1))


def _layer_spec(shape, layer):
    nd = len(shape)
    return pl.BlockSpec((pl.Squeezed(),) + tuple(shape), lambda *_: (layer,) + (0,) * nd, pipeline_mode=pl.Buffered(1))


def _proj_kernel(x_ref, g_ref, w_ref, cs_ref, zz_ref, vt_ref, wt_ref):
    h = _rms(x_ref[...], g_ref[...]).astype(BF16)

    def group(c):
        cols = slice(c * MIXER_WIDTH, (c + 1) * MIXER_WIDTH)
        return _dot(h, w_ref[:, cols]) * cs_ref[:, cols]

    for c in range(N_GROUP):
        r = group(c)
        for s in range(4):
            zz_ref[4 * c + s] = r[:, s * HEAD_DIM:(s + 1) * HEAD_DIM].astype(BF16)
        if c == G_KIDX:
            wt_ref[...] = jnp.transpose(r)[HEAD_DIM:HEAD_DIM + IDX_HEADS]
    for m in range(N_MIXERS):
        vt = jnp.transpose(group(N_GROUP + m)).astype(BF16)
        for hh in range(N_HEADS):
            vt_ref[m, 0, hh * V_ROWS:hh * V_ROWS + HEAD_DIM, :] = vt[hh * HEAD_DIM:(hh + 1) * HEAD_DIM]
            vt_ref[m, 0, hh * V_ROWS + HEAD_DIM:(hh + 1) * V_ROWS, :] = jnp.ones((V_ROWS - HEAD_DIM, TQ), BF16)


def _proj(x, g, w, cs, layer):
    t = x.shape[0]
    tm = TQ
    return pl.pallas_call(
        _proj_kernel,
        grid=(t // tm,),
        in_specs=[pl.BlockSpec((tm, D_MODEL), lambda i: (i, 0)),
                  _const_spec((1, D_MODEL)),
                  _layer_spec((D_MODEL, N_PACK), layer),
                  _const_spec((1, N_PACK))],
        out_specs=[pl.BlockSpec((N_SLAB, tm, HEAD_DIM), lambda i: (0, i, 0)),
                   pl.BlockSpec((N_MIXERS, 1, N_HEADS * V_ROWS, tm), lambda i: (0, i, 0, 0)),
                   pl.BlockSpec((IDX_HEADS, tm), lambda i: (0, i))],
        out_shape=[jax.ShapeDtypeStruct((N_SLAB, t, HEAD_DIM), BF16),
                   jax.ShapeDtypeStruct((N_MIXERS, t // tm, N_HEADS * V_ROWS, tm), BF16),
                   jax.ShapeDtypeStruct((IDX_HEADS, t), F32)],
        compiler_params=_cparams(1),
        name="proj",
    )(x, g, w, cs)


def _k_block(ref, h, j):
    return ref[h, pl.ds(pl.multiple_of(j * TK, TK), TK), :]


def _fold_keys(a, op):
    n = a.shape[0]
    while n > 8:
        n //= 2
        a = op(a[:n], a[n:2 * n])
    return a


def _vt_block(ref, h, j, rows=V_ROWS):
    return ref[0, j, h * V_ROWS:h * V_ROWS + rows, :]


def _softmax_block(s_list, vt_list, carry):
    ms = [jnp.maximum(c[0], jnp.max(_fold_keys(s, jnp.maximum), axis=0, keepdims=True))
          for s, c in zip(s_list, carry)]
    pvs = [_dot(vt, jnp.exp2(s - m).astype(BF16)) for vt, s, m in zip(vt_list, s_list, ms)]
    return tuple((m_new, jnp.exp2(m - m_new) * acc + pv) for (m, acc), m_new, pv in zip(carry, ms, pvs))


def _softmax_loop(n_blocks, logits, values, carry):
    def pair(t, carry):
        j = 2 * t
        s_a, s_b = logits(j), logits(j + 1)
        return _softmax_block(s_b, values(j + 1), _softmax_block(s_a, values(j), carry))

    def single(j, carry):
        return _softmax_block(logits(j), values(j), carry)

    n_pairs = n_blocks // 2
    carry = lax.fori_loop(0, n_pairs, pair, carry)
    return lax.fori_loop(2 * n_pairs, n_blocks, single, carry)


def _softmax_init():
    return (jnp.full((1, TQ), NEG, F32), jnp.zeros((V_ROWS, TQ), F32))


def _softmax_out(carry):
    _, acc = carry
    return acc[:HEAD_DIM] / acc[HEAD_DIM:HEAD_DIM + 1]


def _store_heads(o_ref, heads_t):
    o_ref[...] = jnp.transpose(jnp.concatenate(heads_t, axis=0)).astype(BF16)


BIAS_ROWS_SHAPE = (N_HEADS, 3, 1, 2 * TQ)
BIAS_TILES_SHAPE = (N_HEADS, 3, TK, TQ)


def _fill_bias_tiles(bw_ref, bt_ref):
    @pl.when((pl.program_id(0) == 0) & (pl.program_id(1) == 0))
    def _():
        for h in range(N_HEADS):
            for o in range(3):
                rows = jnp.broadcast_to(bw_ref[h, o], (TK, 2 * TQ))
                bt_ref[h, o] = pltpu.roll(rows, TQ + 1, 1, stride=1, stride_axis=0)[:, :TQ]


def _attn_specs(nq, gq, gk, mixer, s):
    return [pl.BlockSpec((4, TQ, HEAD_DIM), lambda b, i: (gq, b * nq + i, 0)),
            pl.BlockSpec((4, s, HEAD_DIM), lambda b, i: (gk, b, 0)),
            pl.BlockSpec((1, s // TK, N_HEADS * V_ROWS, TK), lambda b, i: (mixer, b, 0, 0))]


def _out_spec(nq):
    return pl.BlockSpec((TQ, N_HEADS * HEAD_DIM), lambda b, i: (b * nq + i, 0))


def _sb_kernel(q_ref, k_ref, vt_ref, tri_ref, mask_ref, o_ref):
    i = pl.program_id(1)
    tri = tri_ref[...]

    def block(j, carry, masked):
        heads = range(N_HEADS)
        zs = [_dot_nt(_k_block(k_ref, h, j), q_ref[h]) for h in heads]
        if masked:
            zs = [z + mask_ref[...] for z in zs]
        sps = [jnp.maximum(z, 0.0) + jnp.log(1.0 + jnp.exp(-jnp.abs(z))) for z in zs]
        his = [sp.astype(BF16) for sp in sps]
        los = [(sp - hi.astype(F32)).astype(BF16) for sp, hi in zip(sps, his)]
        cs = [_dot(tri, hi) + _dot(tri, lo) + c[0] for hi, lo, c in zip(his, los, carry)]
        avs = [_dot(_vt_block(vt_ref, h, j, HEAD_DIM), jnp.exp(z - c).astype(BF16))
               for h, z, c in zip(heads, zs, cs)]
        return tuple((c[0:1, :], old[1] + av) for c, old, av in zip(cs, carry, avs))

    init = tuple((jnp.zeros((1, TQ), F32), jnp.zeros((HEAD_DIM, TQ), F32)) for _ in range(N_HEADS))
    carry = block(i, init, True)

    def weights_left(carry):
        tail = functools.reduce(jnp.minimum, [c[0] for c in carry])
        return (jnp.min(tail) < SB_TAIL_CUTOFF).astype(jnp.int32)

    def earlier_block(c):
        j, carry, _ = c
        carry = block(j, carry, False)
        return j - 1, carry, weights_left(carry)

    _, carry, _ = lax.while_loop(lambda c: (c[0] >= 0) & (c[2] > 0), earlier_block, (i - 1, carry, weights_left(carry)))
    _store_heads(o_ref, [c[1] for c in carry])


def _sb_attention(zz, vt, b, s, tri, mask):
    nq = s // TQ
    return pl.pallas_call(
        _sb_kernel,
        grid=(b, nq),
        in_specs=_attn_specs(nq, G_SB_Q, G_SB_K, M_SB, s) + [_const_spec((TK, TK)), _const_spec((TK, TQ))],
        out_specs=_out_spec(nq),
        out_shape=jax.ShapeDtypeStruct((b * s, N_HEADS * HEAD_DIM), BF16),
        compiler_params=_cparams(2),
        name="sb_attn",
    )(zz, zz, vt, tri, mask)


def _diff_kernel(q_ref, k_ref, vt_ref, bw_ref, lam_ref, cst_ref, g_ref, o_ref, bt_ref):
    i = pl.program_id(1)
    _fill_bias_tiles(bw_ref, bt_ref)
    lp = lam_ref[...]
    lam_init = cst_ref[:, 0:1]
    lam = (jnp.exp(jnp.sum(lp[0:1] * lp[1:2], axis=-1, keepdims=True))
           - jnp.exp(jnp.sum(lp[2:3] * lp[3:4], axis=-1, keepdims=True)) + lam_init)
    lane = lax.broadcasted_iota(jnp.int32, (TQ, HEAD_DIM), 1)
    qs = []
    for h in range(N_HEADS):
        q = q_ref[h]
        qs.append((jnp.where(lane < DIFF_QK_DIM, q, jnp.zeros_like(q)),
                   jnp.where(lane >= DIFF_QK_DIM, q, jnp.zeros_like(q))))

    def logits(j):
        s_list = []
        for h in range(N_HEADS):
            kj = _k_block(k_ref, h, j)
            bias = bt_ref[h, jnp.minimum(i - j, 2)]
            s_list += [_dot_nt(kj, qs[h][0]) + bias, _dot_nt(kj, qs[h][1]) + bias]
        return s_list

    def values(j):
        return [_vt_block(vt_ref, h, j) for h in range(N_HEADS) for _ in range(2)]

    carry = _softmax_loop(i + 1, logits, values, tuple(_softmax_init() for _ in range(2 * N_HEADS)))
    outs = []
    for h in range(N_HEADS):
        o = _softmax_out(carry[2 * h]) - lam * _softmax_out(carry[2 * h + 1])
        o = o * lax.rsqrt(jnp.mean(o * o, axis=0, keepdims=True) + NORM_EPS) * g_ref[...]
        outs.append(o * (1.0 - lam_init))
    _store_heads(o_ref, outs)


def _diff_attention(zz, vt, b, s, bw, lamp, cst, g):
    nq = s // TQ
    return pl.pallas_call(
        _diff_kernel,
        grid=(b, nq),
        in_specs=_attn_specs(nq, G_DF_Q, G_DF_K, M_DF, s) + [
            _const_spec(BIAS_ROWS_SHAPE), _const_spec((4, DIFF_QK_DIM)), _const_spec((1, 128)),
            _const_spec((HEAD_DIM, 1))],
        out_specs=_out_spec(nq),
        out_shape=jax.ShapeDtypeStruct((b * s, N_HEADS * HEAD_DIM), BF16),
        scratch_shapes=[pltpu.VMEM(BIAS_TILES_SHAPE, F32)],
        compiler_params=_cparams(2),
        name="diff_attn",
    )(zz, zz, vt, bw, lamp, cst, g)


def _dsa_kernel(q_ref, k_ref, vt_ref, qia_ref, qib_ref, ki_ref, wt_ref, bw_ref, tril_ref, o_ref, sc_ref, bt_ref, *,
                topk):
    i = pl.program_id(1)
    _fill_bias_tiles(bw_ref, bt_ref)
    nb = i + 1
    kf = float(topk)
    w = wt_ref[...]
    key = lax.broadcasted_iota(jnp.int32, (TK, TQ), 0)
    qry = lax.broadcasted_iota(jnp.int32, (TK, TQ), 1)

    def score(j):
        kij = _k_block(ki_ref, 0, j)
        sc = jnp.zeros((TK, TQ), F32)
        for hh in range(IDX_HEADS):
            qi = (qia_ref if hh < 4 else qib_ref)[hh % 4]
            sc = sc + w[hh:hh + 1, :] * jnp.maximum(_dot_nt(kij, qi), 0.0)
        return sc

    def extend(lo_src, hi_src, c):
        return (jnp.minimum(c[0], _fold_keys(lo_src, jnp.minimum)), jnp.maximum(c[1], _fold_keys(hi_src, jnp.maximum)))

    def earlier_block(j, c):
        sc = score(j)
        sc_ref[j] = sc
        return extend(sc, sc, c)

    def earlier_pair(t, c):
        sc_a, sc_b = score(2 * t), score(2 * t + 1)
        sc_ref[2 * t] = sc_a
        sc_ref[2 * t + 1] = sc_b
        return extend(sc_b, sc_b, extend(sc_a, sc_a, c))

    lo_part, hi_part = lax.fori_loop(0, i // 2, earlier_pair,
                                     (jnp.full((8, TQ), BIG, F32), jnp.full((8, TQ), -BIG, F32)))
    lo_part, hi_part = lax.fori_loop(2 * (i // 2), i, earlier_block, (lo_part, hi_part))
    sc = score(i)
    causal = key <= qry
    sc_ref[i] = jnp.where(causal, sc, NEG)
    lo_part, hi_part = extend(jnp.where(causal, sc, BIG), jnp.where(causal, sc, NEG), (lo_part, hi_part))

    def reduce_blocks(fn, init):
        def pair(t, c):
            return fn(sc_ref[2 * t + 1], 2 * t + 1, fn(sc_ref[2 * t], 2 * t, c))
        c = lax.fori_loop(0, nb // 2, pair, init)
        return lax.fori_loop(2 * (nb // 2), nb, lambda j, c: fn(sc_ref[j], j, c), c)

    def count_ge(t):
        part = reduce_blocks(lambda x, j, c: c + _fold_keys(jnp.where(x >= t, 1.0, 0.0), jnp.add),
                             jnp.zeros((8, TQ), F32))
        return jnp.sum(part, axis=0, keepdims=True)

    def minmax_blocks(lo_of, hi_of):
        def f(x, j, c):
            return (jnp.minimum(c[0], _fold_keys(lo_of(x), jnp.minimum)),
                    jnp.maximum(c[1], _fold_keys(hi_of(x), jnp.maximum)))
        lo_part, hi_part = reduce_blocks(f, (jnp.full((8, TQ), BIG, F32), jnp.full((8, TQ), -BIG, F32)))
        return jnp.min(lo_part, axis=0, keepdims=True), jnp.max(hi_part, axis=0, keepdims=True)

    n_valid = i * TQ + lax.broadcasted_iota(jnp.int32, (1, TQ), 1) + 1
    take_all = n_valid <= topk
    lo = jnp.min(lo_part, axis=0, keepdims=True)
    hi = jnp.max(hi_part, axis=0, keepdims=True)
    c_max = count_ge(hi)
    at_max = c_max >= kf
    state = (jnp.where(at_max, hi, lo), hi, jnp.where(at_max, c_max, n_valid.astype(F32)), c_max)

    def bisect(_, state):
        lo, hi, c_lo, c_hi = state
        mid = 0.5 * lo + 0.5 * hi
        c = count_ge(mid)
        ge = c >= kf
        return jnp.where(ge, mid, lo), jnp.where(ge, hi, mid), jnp.where(ge, c, c_lo), jnp.where(ge, c_hi, c)

    def unsettled(state):
        lo, hi, c_lo, _ = state
        open_q = jnp.where(take_all, 0.0, jnp.where(c_lo != kf, jnp.where(lo < hi, 1.0, 0.0), 0.0))

        def band_spread():
            b_min, b_max = minmax_blocks(lambda x: jnp.where(x >= lo, jnp.where(x < hi, x, BIG), BIG),
                                         lambda x: jnp.where(x >= lo, jnp.where(x < hi, x, -BIG), -BIG))
            return (jnp.max(jnp.where(b_max != b_min, open_q, 0.0)) > 0.0).astype(jnp.int32)

        return lax.cond(jnp.max(open_q) > 0.0, band_spread, lambda: jnp.int32(0))

    state = lax.fori_loop(0, jnp.where((i + 1) * TQ <= topk, 0, BISECT_WARMUP), bisect, state)

    def trip(c):
        n, state, _ = c
        state = lax.fori_loop(0, BISECT_TRIP, bisect, state)
        return n + 1, state, unsettled(state)

    _, state, _ = lax.while_loop(lambda c: (c[2] > 0) & (c[0] < BISECT_MAX_TRIPS), trip,
                                 (jnp.int32(0), state, unsettled(state)))
    lo, hi, c_lo, c_hi = state
    hi_ok = lo < hi
    c_above = jnp.where(hi_ok, c_hi, 0.0)
    hi_sel = jnp.where(hi_ok, hi, BIG)
    need = jnp.where(take_all, BIG, kf - c_above)
    lo_sel = jnp.where(take_all, HALF_NEG, lo)

    tied = jnp.max(jnp.where(take_all, 0.0, c_lo - kf)) > 0.0

    @pl.when(tied)
    def _():
        tril = tril_ref[...]

        def band_of(j):
            x = sc_ref[j]
            return x, jnp.where(x >= lo_sel, jnp.where(x < hi_sel, 1.0, 0.0), 0.0)

        def write(j, x, band, rank):
            sc_ref[j] = jnp.where(x >= hi_sel, 0.0,
                                  jnp.where(band * rank > 0.0, jnp.where(rank <= need, 0.0, NEG), NEG))
            return rank[TK - 1:TK, :]

        def write_mask(j, taken):
            x, band = band_of(j)
            return write(j, x, band, _dot(tril, band.astype(BF16)) + taken)

        def write_mask_pair(t, taken):
            (x_a, band_a), (x_b, band_b) = band_of(2 * t), band_of(2 * t + 1)
            in_a, in_b = _dot(tril, band_a.astype(BF16)), _dot(tril, band_b.astype(BF16))
            taken = write(2 * t, x_a, band_a, in_a + taken)
            return write(2 * t + 1, x_b, band_b, in_b + taken)

        taken = lax.fori_loop(0, nb // 2, write_mask_pair, jnp.zeros((1, TQ), F32))
        lax.fori_loop(2 * (nb // 2), nb, write_mask, taken)

    @pl.when(jnp.logical_not(tied))
    def _():
        def write_mask(j, _):
            sc_ref[j] = jnp.where(sc_ref[j] >= lo_sel, 0.0, NEG)
            return 0

        lax.fori_loop(0, nb, write_mask, 0)

    def logits(j):
        return [_dot_nt(_k_block(k_ref, h, j), q_ref[h]) + bt_ref[h, jnp.minimum(i - j, 2)] + sc_ref[j]
                for h in range(N_HEADS)]

    carry = _softmax_loop(nb, logits, lambda j: [_vt_block(vt_ref, h, j) for h in range(N_HEADS)],
                          tuple(_softmax_init() for _ in range(N_HEADS)))
    _store_heads(o_ref, [_softmax_out(c) for c in carry])


def _dsa_attention(zz, vt, wt, b, s, bw, tril):
    nq = s // TQ
    topk = min(DSA_TOPK_MAX, s // 4)
    return pl.pallas_call(
        functools.partial(_dsa_kernel, topk=topk),
        grid=(b, nq),
        in_specs=_attn_specs(nq, G_DS_Q, G_DS_K, M_DS, s) + [
            pl.BlockSpec((4, TQ, HEAD_DIM), lambda b_, i: (G_QI_A, b_ * nq + i, 0)),
            pl.BlockSpec((4, TQ, HEAD_DIM), lambda b_, i: (G_QI_B, b_ * nq + i, 0)),
            pl.BlockSpec((1, s, HEAD_DIM), lambda b_, i: (S_KIDX, b_, 0)),
            pl.BlockSpec((IDX_HEADS, TQ), lambda b_, i: (0, b_ * nq + i)),
            _const_spec(BIAS_ROWS_SHAPE), _const_spec((TK, TK))],
        out_specs=_out_spec(nq),
        out_shape=jax.ShapeDtypeStruct((b * s, N_HEADS * HEAD_DIM), BF16),
        scratch_shapes=[pltpu.VMEM((nq, TK, TQ), F32), pltpu.VMEM(BIAS_TILES_SHAPE, F32)],
        compiler_params=_cparams(2),
        name="dsa_attn",
    )(zz, zz, vt, zz, zz, zz, wt, bw, tril)


def _moba_kernel(q_ref, k_ref, vt_ref, bw_ref, o_ref, km_ref, bt_ref, *, nblk, topb):
    i = pl.program_id(1)
    _fill_bias_tiles(bw_ref, bt_ref)
    nrow = km_ref.shape[1]

    @pl.when(i == 0)
    def _():
        km_ref[...] = jnp.zeros_like(km_ref)
        for h in range(N_HEADS):
            for n in range(nblk):
                kb = k_ref[h, n * MOBA_BLOCK:(n + 1) * MOBA_BLOCK, :].astype(F32)
                km_ref[h, n:n + 1, :] = jnp.mean(kb, axis=0, keepdims=True)

    blk = lax.broadcasted_iota(jnp.int32, (nrow, TQ), 0)
    past = blk < i
    head_bits = []
    for h in range(N_HEADS):
        gate = _dot_nt(km_ref[h].astype(BF16), q_ref[h])
        bits = jnp.zeros((1, TQ), F32)
        for n in range(nblk):
            gn = gate[n:n + 1, :]
            beats = jnp.where(past, jnp.where(gate > gn, 1.0, jnp.where(gate == gn, jnp.where(blk < n, 1.0, 0.0), 0.0)), 0.0)
            rank = jnp.sum(beats, axis=0, keepdims=True)
            bits = bits + jnp.where(rank < float(topb), jnp.where(n < i, float(2 ** n), 0.0), 0.0)
        head_bits.append(bits.astype(jnp.int32) | lax.shift_left(jnp.int32(1), i))

    def values(n):
        return [_vt_block(vt_ref, h, n) for h in range(N_HEADS)]

    def logits(n):
        s_list = []
        for h in range(N_HEADS):
            picked = (lax.shift_right_logical(head_bits[h], jnp.full_like(head_bits[h], n)) & 1) == 1
            s_list.append(_dot_nt(_k_block(k_ref, h, n), q_ref[h]) + bt_ref[h, jnp.minimum(i - n, 2)]
                          + jnp.where(picked, 0.0, NEG))
        return s_list

    carry = _softmax_loop(i + 1, logits, values, tuple(_softmax_init() for _ in range(N_HEADS)))
    _store_heads(o_ref, [_softmax_out(c) for c in carry])


def _moba_attention(zz, vt, b, s, bw):
    nq = s // TQ
    nblk = s // MOBA_BLOCK
    topb = min(MOBA_TOPK, nblk - 1)
    return pl.pallas_call(
        functools.partial(_moba_kernel, nblk=nblk, topb=topb),
        grid=(b, nq),
        in_specs=_attn_specs(nq, G_MB_Q, G_MB_K, M_MB, s) + [_const_spec(BIAS_ROWS_SHAPE)],
        out_specs=_out_spec(nq),
        out_shape=jax.ShapeDtypeStruct((b * s, N_HEADS * HEAD_DIM), BF16),
        scratch_shapes=[pltpu.VMEM((N_HEADS, max(8, nblk), HEAD_DIM), F32), pltpu.VMEM(BIAS_TILES_SHAPE, F32)],
        compiler_params=_cparams(2),
        name="moba_attn",
    )(zz, zz, vt, bw)


def _merge_kernel(x_ref, osb_ref, odf_ref, ods_ref, omb_ref, gpre_ref, wg_ref, wbr_ref, wout_ref, gpost_ref, o_ref):
    x = x_ref[...]
    h = _rms(x, gpre_ref[...]).astype(BF16)
    y = jnp.zeros((x.shape[0], D_MODEL), F32)
    for r, o_r in enumerate((osb_ref, odf_ref, ods_ref, omb_ref)):
        gate = jax.nn.sigmoid(_dot(h, wg_ref[:, r * D_MODEL:(r + 1) * D_MODEL]))
        y = y + gate * _dot(o_r[...], wbr_ref[r])
    o_ref[...] = x + _rms(_dot(y.astype(BF16), wout_ref[...]), gpost_ref[...])


def _merge(x, o_sb, o_df, o_ds, o_mb, g_pre, w_gate, w_br, w_out, g_post, layer):
    t = x.shape[0]
    tm = TM_MERGE
    tok = lambda width: pl.BlockSpec((tm, width), lambda i: (i, 0))
    return pl.pallas_call(
        _merge_kernel,
        grid=(t // tm,),
        in_specs=[tok(D_MODEL)] + [tok(MIXER_WIDTH)] * N_MIXERS + [
            _const_spec((1, D_MODEL)), _layer_spec((D_MODEL, N_MIXERS * D_MODEL), layer),
            _layer_spec((N_MIXERS, MIXER_WIDTH, D_MODEL), layer), _layer_spec((D_MODEL, D_MODEL), layer),
            _const_spec((1, D_MODEL))],
        out_specs=tok(D_MODEL),
        out_shape=jax.ShapeDtypeStruct((t, D_MODEL), F32),
        compiler_params=_cparams(1),
        name="merge",
    )(x, o_sb, o_df, o_ds, o_mb, g_pre, w_gate, w_br, w_out, g_post)


def _ffn_kernel(x_ref, gpre_ref, win_ref, wout_ref, gpost_ref, o_ref):
    x = x_ref[...]
    h = _rms(x, gpre_ref[...]).astype(BF16)
    gate = _dot(h, win_ref[:, 0:D_FF])
    up = _dot(h, win_ref[:, D_FF:2 * D_FF])
    act = (gate * jax.nn.sigmoid(gate) * up).astype(BF16)
    o_ref[...] = x + _rms(_dot(act, wout_ref[...]), gpost_ref[...])


def _ffn(x, g_pre, w_in, w_out, g_post, layer):
    t = x.shape[0]
    tm = TM_FFN
    return pl.pallas_call(
        _ffn_kernel,
        grid=(t // tm,),
        in_specs=[pl.BlockSpec((tm, D_MODEL), lambda i: (i, 0)), _const_spec((1, D_MODEL)),
                  _layer_spec((D_MODEL, 2 * D_FF), layer), _layer_spec((D_FF, D_MODEL), layer),
                  _const_spec((1, D_MODEL))],
        out_specs=pl.BlockSpec((tm, D_MODEL), lambda i: (i, 0)),
        out_shape=jax.ShapeDtypeStruct((t, D_MODEL), F32),
        compiler_params=_cparams(1),
        name="ffn",
    )(x, g_pre, w_in, w_out, g_post)


def _t5_bucket(dist):
    max_exact = N_BUCKETS // 2
    d = jnp.maximum(dist, 0)
    log_ratio = jnp.log(jnp.maximum(d, 1).astype(F32) / max_exact) / math.log(MAX_DISTANCE / max_exact)
    large = jnp.minimum(max_exact + (log_ratio * (N_BUCKETS - max_exact)).astype(jnp.int32), N_BUCKETS - 1)
    return jnp.where(d < max_exact, d, large)


def _bias_rows(rel_bias):
    assert TQ == TK
    n = TK
    d = np.arange(-(n - 1), 3 * n + 1)
    by_dist = rel_bias.astype(F32).T[:, _t5_bucket(jnp.asarray(np.maximum(d, 0), jnp.int32))]
    by_dist = jnp.where(jnp.asarray(d >= 0)[None, :], by_dist * LOG2E, NEG)
    return jnp.stack([by_dist[:, o * n:o * n + 2 * n] for o in range(3)], axis=1)[:, :, None, :]


def _pack_weights(w_in):
    parts, start = [], 0
    for e in range(1, N_PACK + 1):
        if e == N_PACK or _PACK_SRC[e] != _PACK_SRC[e - 1] + (1 if _PACK_SRC[e - 1] >= 0 else 0):
            a = int(_PACK_SRC[start])
            n = e - start
            parts.append(w_in[:, :, a:a + n] if a >= 0 else jnp.zeros(w_in.shape[:2] + (n,), w_in.dtype))
            start = e
    return jnp.concatenate(parts, axis=2).astype(BF16)


def kernel(x, w_in, w_br_sb, w_br_diff, w_br_dsa, w_br_moba, w_out, lambda_q1, lambda_k1, lambda_q2, lambda_k2,
           diff_subln_g, rel_bias, w_ffn_in, w_ffn_out, g_pre_mix, g_post_mix, g_pre_ffn, g_post_ffn):
    b, s, d = x.shape
    depth = w_in.shape[0]
    assert d == D_MODEL and s % TQ == 0 and s // MOBA_BLOCK >= 2
    t = b * s

    w_pack = _pack_weights(w_in)
    w_gate = w_in.astype(BF16)[:, :, _OFF["gate"]:]
    w_br = jnp.stack([w_br_sb, w_br_diff, w_br_dsa, w_br_moba], axis=1).astype(BF16)
    w_o = w_out.astype(BF16)
    w_f1 = w_ffn_in.astype(BF16)
    w_f2 = w_ffn_out.astype(BF16)
    cs = jnp.asarray(_PACK_SCALE)[None, :]

    bw = _bias_rows(rel_bias)
    bw_df, bw_ds, bw_mb = bw[0:4], bw[4:8], bw[8:12]
    key = np.arange(TK)[:, None]
    qry = np.arange(TQ)[None, :]
    tri = jnp.asarray(key <= np.arange(TK)[None, :], BF16)
    tril = jnp.asarray(key >= np.arange(TK)[None, :], BF16)
    sb_mask = jnp.asarray(np.where(key < qry, 0.0, NEG), F32)

    xf = x.reshape(t, d)
    for l in range(depth):
        lam_init = 0.8 - 0.6 * math.exp(-0.3 * l)
        lamp = jnp.stack([lambda_q1[l], lambda_k1[l], lambda_q2[l], lambda_k2[l]]).astype(F32)
        cst = jnp.full((1, 128), lam_init, F32)
        zz, vt, wt = _proj(xf, g_pre_mix[l][None, :], w_pack, cs, l)
        o_sb = _sb_attention(zz, vt, b, s, tri, sb_mask)
        o_df = _diff_attention(zz, vt, b, s, bw_df, lamp, cst, diff_subln_g[l][:, None])
        o_ds = _dsa_attention(zz, vt, wt, b, s, bw_ds, tril)
        o_mb = _moba_attention(zz, vt, b, s, bw_mb)
        xf = _merge(xf, o_sb, o_df, o_ds, o_mb, g_pre_mix[l][None, :], w_gate, w_br, w_o, g_post_mix[l][None, :], l)
        xf = _ffn(xf, g_pre_ffn[l][None, :], w_f1, w_f2, g_post_ffn[l][None, :], l)
    return xf.reshape(b, s, d)
```

```python
import functools
import math

import numpy as np
import jax
import jax.numpy as jnp
from jax import lax
from jax.experimental import pallas as pl
from jax.experimental.pallas import tpu as pltpu

F32 = jnp.float32
BF16 = jnp.bfloat16

D_MODEL = 1024
HEAD_DIM = 64
N_HEADS = 4
N_MIXERS = 4
MIXER_WIDTH = N_HEADS * HEAD_DIM
DIFF_QK_DIM = 32
IDX_HEADS = 8
DSA_TOPK_MAX = 256
MOBA_BLOCK = 256
MOBA_TOPK = 3
N_BUCKETS = 32
MAX_DISTANCE = 128
D_FF = 2816
NORM_EPS = 1e-6

TQ = 256
TK = 256
NEG = -1e30
HALF_NEG = -0.5e30
BIG = 3e38
BISECT_WARMUP = 20
BISECT_TRIP = 2
BISECT_MAX_TRIPS = 134
LOG2E = math.log2(math.e)
SB_TAIL_CUTOFF = 120.0 * LOG2E
V_ROWS = 80
TM_MERGE = 512
TM_FFN = 512
N_GROUP = 11
N_SLAB = 4 * N_GROUP
N_PACK = (N_GROUP + N_MIXERS) * MIXER_WIDTH
VMEM_LIMIT = 56 * 1024 * 1024

G_SB_Q, G_SB_K, G_DF_Q, G_DF_K, G_DS_Q, G_DS_K, G_QI_A, G_QI_B, G_MB_Q, G_MB_K, G_KIDX = range(N_GROUP)
S_KIDX = 4 * G_KIDX
M_SB, M_DF, M_DS, M_MB = range(N_MIXERS)


def _layout():
    off = {}
    acc = 0
    for name, sz in (("q_sb", 256), ("k_sb", 256), ("v_sb", 256), ("q1", 128), ("q2", 128), ("k1", 128),
                     ("k2", 128), ("v_df", 256), ("q_ds", 256), ("k_ds", 256), ("v_ds", 256), ("qi", 512),
                     ("ki", 64), ("wi", 8), ("q_mb", 256), ("k_mb", 256), ("v_mb", 256), ("gate", 4096)):
        off[name] = acc
        acc += sz
    return off


_OFF = _layout()


def _pack_layout():
    off = _OFF
    cols, scale = [], []

    def add(start, n, s=1.0):
        cols.extend(range(start, start + n))
        scale.extend([s] * n)

    hd = HEAD_DIM ** -0.5
    hd2 = hd * LOG2E
    df2 = DIFF_QK_DIM ** -0.5 * LOG2E
    add(off["q_sb"], 256, hd2); add(off["k_sb"], 256)
    for h in range(N_HEADS):
        add(off["q1"] + h * 32, 32, df2); add(off["q2"] + h * 32, 32, df2)
    for h in range(N_HEADS):
        add(off["k1"] + h * 32, 32); add(off["k2"] + h * 32, 32)
    add(off["q_ds"], 256, hd2); add(off["k_ds"], 256)
    add(off["qi"], 512, HEAD_DIM ** -0.5)
    add(off["q_mb"], 256, hd2); add(off["k_mb"], 256)
    add(off["ki"], 64); add(off["wi"], IDX_HEADS, IDX_HEADS ** -0.5)
    cols.extend([-1] * 184); scale.extend([1.0] * 184)
    for name in ("v_sb", "v_df", "v_ds", "v_mb"):
        add(off[name], MIXER_WIDTH)
    assert len(cols) == N_PACK
    return np.asarray(cols, np.int32), np.asarray(scale, np.float32)


_PACK_SRC, _PACK_SCALE = _pack_layout()


def _dot(a, b):
    return jnp.dot(a, b, preferred_element_type=F32)


def _dot_nt(a, b):
    return lax.dot_general(a, b, (((1,), (1,)), ((), ())), preferred_element_type=F32)


def _rms(x, g):
    return x * lax.rsqrt(jnp.mean(x * x, axis=-1, keepdims=True) + NORM_EPS) * g


def _cparams(n_axes):
    return pltpu.CompilerParams(dimension_semantics=("arbitrary",) * n_axes, vmem_limit_bytes=VMEM_LIMIT)


def _const_spec(shape):
    nd = len(shape)
    return pl.BlockSpec(shape, lambda *_: (0,) * nd, pipeline_mode=pl.Buffered(1))


def _layer_spec(shape, layer):
    nd = len(shape)
    return pl.BlockSpec((pl.Squeezed(),) + tuple(shape), lambda *_: (layer,) + (0,) * nd, pipeline_mode=pl.Buffered(1))


def _proj_kernel(x_ref, g_ref, w_ref, cs_ref, zz_ref, vt_ref, wt_ref):
    h = _rms(x_ref[...], g_ref[...]).astype(BF16)

    def group(c):
        cols = slice(c * MIXER_WIDTH, (c + 1) * MIXER_WIDTH)
        return _dot(h, w_ref[:, cols]) * cs_ref[:, cols]

    for c in range(N_GROUP):
        r = group(c)
        for s in range(4):
            zz_ref[4 * c + s] = r[:, s * HEAD_DIM:(s + 1) * HEAD_DIM].astype(BF16)
        if c == G_KIDX:
            wt_ref[...] = jnp.transpose(r)[HEAD_DIM:HEAD_DIM + IDX_HEADS]
    for m in range(N_MIXERS):
        vt = jnp.transpose(group(N_GROUP + m)).astype(BF16)
        for hh in range(N_HEADS):
            vt_ref[m, 0, hh * V_ROWS:hh * V_ROWS + HEAD_DIM, :] = vt[hh * HEAD_DIM:(hh + 1) * HEAD_DIM]
            vt_ref[m, 0, hh * V_ROWS + HEAD_DIM:(hh + 1) * V_ROWS, :] = jnp.ones((V_ROWS - HEAD_DIM, TQ), BF16)


def _proj(x, g, w, cs, layer):
    t = x.shape[0]
    tm = TQ
    return pl.pallas_call(
        _proj_kernel,
        grid=(t // tm,),
        in_specs=[pl.BlockSpec((tm, D_MODEL), lambda i: (i, 0)),
                  _const_spec((1, D_MODEL)),
                  _layer_spec((D_MODEL, N_PACK), layer),
                  _const_spec((1, N_PACK))],
        out_specs=[pl.BlockSpec((N_SLAB, tm, HEAD_DIM), lambda i: (0, i, 0)),
                   pl.BlockSpec((N_MIXERS, 1, N_HEADS * V_ROWS, tm), lambda i: (0, i, 0, 0)),
                   pl.BlockSpec((IDX_HEADS, tm), lambda i: (0, i))],
        out_shape=[jax.ShapeDtypeStruct((N_SLAB, t, HEAD_DIM), BF16),
                   jax.ShapeDtypeStruct((N_MIXERS, t // tm, N_HEADS * V_ROWS, tm), BF16),
                   jax.ShapeDtypeStruct((IDX_HEADS, t), F32)],
        compiler_params=_cparams(1),
        name="proj",
    )(x, g, w, cs)


def _k_block(ref, h, j):
    return ref[h, pl.ds(pl.multiple_of(j * TK, TK), TK), :]


def _fold_keys(a, op):
    n = a.shape[0]
    while n > 8:
        n //= 2
        a = op(a[:n], a[n:2 * n])
    return a


def _vt_block(ref, h, j, rows=V_ROWS):
    return ref[0, j, h * V_ROWS:h * V_ROWS + rows, :]


def _softmax_block(s_list, vt_list, carry):
    ms = [jnp.maximum(c[0], jnp.max(_fold_keys(s, jnp.maximum), axis=0, keepdims=True))
          for s, c in zip(s_list, carry)]
    pvs = [_dot(vt, jnp.exp2(s - m).astype(BF16)) for vt, s, m in zip(vt_list, s_list, ms)]
    return tuple((m_new, jnp.exp2(m - m_new) * acc + pv) for (m, acc), m_new, pv in zip(carry, ms, pvs))


def _softmax_loop(n_blocks, logits, values, carry):
    def pair(t, carry):
        j = 2 * t
        s_a, s_b = logits(j), logits(j + 1)
        return _softmax_block(s_b, values(j + 1), _softmax_block(s_a, values(j), carry))

    def single(j, carry):
        return _softmax_block(logits(j), values(j), carry)

    n_pairs = n_blocks // 2
    carry = lax.fori_loop(0, n_pairs, pair, carry)
    return lax.fori_loop(2 * n_pairs, n_blocks, single, carry)


def _softmax_init():
    return (jnp.full((1, TQ), NEG, F32), jnp.zeros((V_ROWS, TQ), F32))


def _softmax_out(carry):
    _, acc = carry
    return acc[:HEAD_DIM] / acc[HEAD_DIM:HEAD_DIM + 1]


def _store_heads(o_ref, heads_t):
    o_ref[...] = jnp.transpose(jnp.concatenate(heads_t, axis=0)).astype(BF16)


BIAS_ROWS_SHAPE = (N_HEADS, 3, 1, 2 * TQ)
BIAS_TILES_SHAPE = (N_HEADS, 3, TK, TQ)


def _fill_bias_tiles(bw_ref, bt_ref):
    @pl.when((pl.program_id(0) == 0) & (pl.program_id(1) == 0))
    def _():
        for h in range(N_HEADS):
            for o in range(3):
                rows = jnp.broadcast_to(bw_ref[h, o], (TK, 2 * TQ))
                bt_ref[h, o] = pltpu.roll(rows, TQ + 1, 1, stride=1, stride_axis=0)[:, :TQ]


def _attn_specs(nq, gq, gk, mixer, s):
    return [pl.BlockSpec((4, TQ, HEAD_DIM), lambda b, i: (gq, b * nq + i, 0)),
            pl.BlockSpec((4, s, HEAD_DIM), lambda b, i: (gk, b, 0)),
            pl.BlockSpec((1, s // TK, N_HEADS * V_ROWS, TK), lambda b, i: (mixer, b, 0, 0))]


def _out_spec(nq):
    return pl.BlockSpec((TQ, N_HEADS * HEAD_DIM), lambda b, i: (b * nq + i, 0))


def _sb_kernel(q_ref, k_ref, vt_ref, tri_ref, mask_ref, o_ref):
    i = pl.program_id(1)
    tri = tri_ref[...]

    def block(j, carry, masked):
        heads = range(N_HEADS)
        zs = [_dot_nt(_k_block(k_ref, h, j), q_ref[h]) for h in heads]
        if masked:
            zs = [z + mask_ref[...] for z in zs]
        sps = [jnp.maximum(z, 0.0) + jnp.log2(1.0 + jnp.exp2(-jnp.abs(z))) for z in zs]
        his = [sp.astype(BF16) for sp in sps]
        los = [(sp - hi.astype(F32)).astype(BF16) for sp, hi in zip(sps, his)]
        cs = [_dot(tri, hi) + _dot(tri, lo) + c[0] for hi, lo, c in zip(his, los, carry)]
        avs = [_dot(_vt_block(vt_ref, h, j, HEAD_DIM), jnp.exp2(z - c).astype(BF16))
               for h, z, c in zip(heads, zs, cs)]
        return tuple((c[0:1, :], old[1] + av) for c, old, av in zip(cs, carry, avs))

    init = tuple((jnp.zeros((1, TQ), F32), jnp.zeros((HEAD_DIM, TQ), F32)) for _ in range(N_HEADS))
    carry = block(i, init, True)

    def weights_left(carry):
        tail = functools.reduce(jnp.minimum, [c[0] for c in carry])
        return (jnp.min(tail) < SB_TAIL_CUTOFF).astype(jnp.int32)

    def earlier_block(c):
        j, carry, _ = c
        carry = block(j, carry, False)
        return j - 1, carry, weights_left(carry)

    _, carry, _ = lax.while_loop(lambda c: (c[0] >= 0) & (c[2] > 0), earlier_block, (i - 1, carry, weights_left(carry)))
    _store_heads(o_ref, [c[1] for c in carry])


def _sb_attention(zz, vt, b, s, tri, mask):
    nq = s // TQ
    return pl.pallas_call(
        _sb_kernel,
        grid=(b, nq),
        in_specs=_attn_specs(nq, G_SB_Q, G_SB_K, M_SB, s) + [_const_spec((TK, TK)), _const_spec((TK, TQ))],
        out_specs=_out_spec(nq),
        out_shape=jax.ShapeDtypeStruct((b * s, N_HEADS * HEAD_DIM), BF16),
        compiler_params=_cparams(2),
        name="sb_attn",
    )(zz, zz, vt, tri, mask)


def _diff_kernel(q_ref, k_ref, vt_ref, bw_ref, lam_ref, cst_ref, g_ref, o_ref, bt_ref):
    i = pl.program_id(1)
    _fill_bias_tiles(bw_ref, bt_ref)
    lp = lam_ref[...]
    lam_init = cst_ref[:, 0:1]
    lam = (jnp.exp(jnp.sum(lp[0:1] * lp[1:2], axis=-1, keepdims=True))
           - jnp.exp(jnp.sum(lp[2:3] * lp[3:4], axis=-1, keepdims=True)) + lam_init)
    lane = lax.broadcasted_iota(jnp.int32, (TQ, HEAD_DIM), 1)
    qs = []
    for h in range(N_HEADS):
        q = q_ref[h]
        qs.append((jnp.where(lane < DIFF_QK_DIM, q, jnp.zeros_like(q)),
                   jnp.where(lane >= DIFF_QK_DIM, q, jnp.zeros_like(q))))

    def logits(j):
        s_list = []
        for h in range(N_HEADS):
            kj = _k_block(k_ref, h, j)
            bias = bt_ref[h, jnp.minimum(i - j, 2)]
            s_list += [_dot_nt(kj, qs[h][0]) + bias, _dot_nt(kj, qs[h][1]) + bias]
        return s_list

    def values(j):
        return [_vt_block(vt_ref, h, j) for h in range(N_HEADS) for _ in range(2)]

    carry = _softmax_loop(i + 1, logits, values, tuple(_softmax_init() for _ in range(2 * N_HEADS)))
    outs = []
    for h in range(N_HEADS):
        o = _softmax_out(carry[2 * h]) - lam * _softmax_out(carry[2 * h + 1])
        o = o * lax.rsqrt(jnp.mean(o * o, axis=0, keepdims=True) + NORM_EPS) * g_ref[...]
        outs.append(o * (1.0 - lam_init))
    _store_heads(o_ref, outs)


def _diff_attention(zz, vt, b, s, bw, lamp, cst, g):
    nq = s // TQ
    return pl.pallas_call(
        _diff_kernel,
        grid=(b, nq),
        in_specs=_attn_specs(nq, G_DF_Q, G_DF_K, M_DF, s) + [
            _const_spec(BIAS_ROWS_SHAPE), _const_spec((4, DIFF_QK_DIM)), _const_spec((1, 128)),
            _const_spec((HEAD_DIM, 1))],
        out_specs=_out_spec(nq),
        out_shape=jax.ShapeDtypeStruct((b * s, N_HEADS * HEAD_DIM), BF16),
        scratch_shapes=[pltpu.VMEM(BIAS_TILES_SHAPE, F32)],
        compiler_params=_cparams(2),
        name="diff_attn",
    )(zz, zz, vt, bw, lamp, cst, g)


def _dsa_kernel(q_ref, k_ref, vt_ref, qia_ref, qib_ref, ki_ref, wt_ref, bw_ref, tril_ref, o_ref, sc_ref, bt_ref, *,
                topk):
    i = pl.program_id(1)
    _fill_bias_tiles(bw_ref, bt_ref)
    nb = i + 1
    kf = float(topk)
    w = wt_ref[...]
    key = lax.broadcasted_iota(jnp.int32, (TK, TQ), 0)
    qry = lax.broadcasted_iota(jnp.int32, (TK, TQ), 1)

    def score(j):
        kij = _k_block(ki_ref, 0, j)
        sc = jnp.zeros((TK, TQ), F32)
        for hh in range(IDX_HEADS):
            qi = (qia_ref if hh < 4 else qib_ref)[hh % 4]
            sc = sc + w[hh:hh + 1, :] * jnp.maximum(_dot_nt(kij, qi), 0.0)
        return sc

    def extend(lo_src, hi_src, c):
        return (jnp.minimum(c[0], _fold_keys(lo_src, jnp.minimum)), jnp.maximum(c[1], _fold_keys(hi_src, jnp.maximum)))

    def earlier_block(j, c):
        sc = score(j)
        sc_ref[j] = sc
        return extend(sc, sc, c)

    def earlier_pair(t, c):
        sc_a, sc_b = score(2 * t), score(2 * t + 1)
        sc_ref[2 * t] = sc_a
        sc_ref[2 * t + 1] = sc_b
        return extend(sc_b, sc_b, extend(sc_a, sc_a, c))

    lo_part, hi_part = lax.fori_loop(0, i // 2, earlier_pair,
                                     (jnp.full((8, TQ), BIG, F32), jnp.full((8, TQ), -BIG, F32)))
    lo_part, hi_part = lax.fori_loop(2 * (i // 2), i, earlier_block, (lo_part, hi_part))
    sc = score(i)
    causal = key <= qry
    sc_ref[i] = jnp.where(causal, sc, NEG)
    lo_part, hi_part = extend(jnp.where(causal, sc, BIG), jnp.where(causal, sc, NEG), (lo_part, hi_part))

    def reduce_blocks(fn, init):
        def pair(t, c):
            return fn(sc_ref[2 * t + 1], 2 * t + 1, fn(sc_ref[2 * t], 2 * t, c))
        c = lax.fori_loop(0, nb // 2, pair, init)
        return lax.fori_loop(2 * (nb // 2), nb, lambda j, c: fn(sc_ref[j], j, c), c)

    def count_ge(t):
        part = reduce_blocks(lambda x, j, c: c + _fold_keys(jnp.where(x >= t, 1.0, 0.0), jnp.add),
                             jnp.zeros((8, TQ), F32))
        return jnp.sum(part, axis=0, keepdims=True)

    def minmax_blocks(lo_of, hi_of):
        def f(x, j, c):
            return (jnp.minimum(c[0], _fold_keys(lo_of(x), jnp.minimum)),
                    jnp.maximum(c[1], _fold_keys(hi_of(x), jnp.maximum)))
        lo_part, hi_part = reduce_blocks(f, (jnp.full((8, TQ), BIG, F32), jnp.full((8, TQ), -BIG, F32)))
        return jnp.min(lo_part, axis=0, keepdims=True), jnp.max(hi_part, axis=0, keepdims=True)

    n_valid = i * TQ + lax.broadcasted_iota(jnp.int32, (1, TQ), 1) + 1
    take_all = n_valid <= topk
    lo = jnp.min(lo_part, axis=0, keepdims=True)
    hi = jnp.max(hi_part, axis=0, keepdims=True)
    c_max = count_ge(hi)
    at_max = c_max >= kf
    state = (jnp.where(at_max, hi, lo), hi, jnp.where(at_max, c_max, n_valid.astype(F32)), c_max)

    def bisect(_, state):
        lo, hi, c_lo, c_hi = state
        mid = 0.5 * lo + 0.5 * hi
        c = count_ge(mid)
        ge = c >= kf
        return jnp.where(ge, mid, lo), jnp.where(ge, hi, mid), jnp.where(ge, c, c_lo), jnp.where(ge, c_hi, c)

    def unsettled(state):
        lo, hi, c_lo, _ = state
        open_q = jnp.where(take_all, 0.0, jnp.where(c_lo != kf, jnp.where(lo < hi, 1.0, 0.0), 0.0))

        def band_spread():
            b_min, b_max = minmax_blocks(lambda x: jnp.where(x >= lo, jnp.where(x < hi, x, BIG), BIG),
                                         lambda x: jnp.where(x >= lo, jnp.where(x < hi, x, -BIG), -BIG))
            return (jnp.max(jnp.where(b_max != b_min, open_q, 0.0)) > 0.0).astype(jnp.int32)

        return lax.cond(jnp.max(open_q) > 0.0, band_spread, lambda: jnp.int32(0))

    state = lax.fori_loop(0, jnp.where((i + 1) * TQ <= topk, 0, BISECT_WARMUP), bisect, state)

    def trip(c):
        n, state, _ = c
        state = lax.fori_loop(0, BISECT_TRIP, bisect, state)
        return n + 1, state, unsettled(state)

    _, state, _ = lax.while_loop(lambda c: (c[2] > 0) & (c[0] < BISECT_MAX_TRIPS), trip,
                                 (jnp.int32(0), state, unsettled(state)))
    lo, hi, c_lo, c_hi = state
    hi_ok = lo < hi
    c_above = jnp.where(hi_ok, c_hi, 0.0)
    hi_sel = jnp.where(hi_ok, hi, BIG)
    need = jnp.where(take_all, BIG, kf - c_above)
    lo_sel = jnp.where(take_all, HALF_NEG, lo)

    tied = jnp.max(jnp.where(take_all, 0.0, c_lo - kf)) > 0.0

    @pl.when(tied)
    def _():
        tril = tril_ref[...]

        def band_of(j):
            x = sc_ref[j]
            return x, jnp.where(x >= lo_sel, jnp.where(x < hi_sel, 1.0, 0.0), 0.0)

        def write(j, x, band, rank):
            sc_ref[j] = jnp.where(x >= hi_sel, 0.0,
                                  jnp.where(band * rank > 0.0, jnp.where(rank <= need, 0.0, NEG), NEG))
            return rank[TK - 1:TK, :]

        def write_mask(j, taken):
            x, band = band_of(j)
            return write(j, x, band, _dot(tril, band.astype(BF16)) + taken)

        def write_mask_pair(t, taken):
            (x_a, band_a), (x_b, band_b) = band_of(2 * t), band_of(2 * t + 1)
            in_a, in_b = _dot(tril, band_a.astype(BF16)), _dot(tril, band_b.astype(BF16))
            taken = write(2 * t, x_a, band_a, in_a + taken)
            return write(2 * t + 1, x_b, band_b, in_b + taken)

        taken = lax.fori_loop(0, nb // 2, write_mask_pair, jnp.zeros((1, TQ), F32))
        lax.fori_loop(2 * (nb // 2), nb, write_mask, taken)

    @pl.when(jnp.logical_not(tied))
    def _():
        def write_mask(j, _):
            sc_ref[j] = jnp.where(sc_ref[j] >= lo_sel, 0.0, NEG)
            return 0

        lax.fori_loop(0, nb, write_mask, 0)

    def logits(j):
        return [_dot_nt(_k_block(k_ref, h, j), q_ref[h]) + bt_ref[h, jnp.minimum(i - j, 2)] + sc_ref[j]
                for h in range(N_HEADS)]

    carry = _softmax_loop(nb, logits, lambda j: [_vt_block(vt_ref, h, j) for h in range(N_HEADS)],
                          tuple(_softmax_init() for _ in range(N_HEADS)))
    _store_heads(o_ref, [_softmax_out(c) for c in carry])


def _dsa_attention(zz, vt, wt, b, s, bw, tril):
    nq = s // TQ
    topk = min(DSA_TOPK_MAX, s // 4)
    return pl.pallas_call(
        functools.partial(_dsa_kernel, topk=topk),
        grid=(b, nq),
        in_specs=_attn_specs(nq, G_DS_Q, G_DS_K, M_DS, s) + [
            pl.BlockSpec((4, TQ, HEAD_DIM), lambda b_, i: (G_QI_A, b_ * nq + i, 0)),
            pl.BlockSpec((4, TQ, HEAD_DIM), lambda b_, i: (G_QI_B, b_ * nq + i, 0)),
            pl.BlockSpec((1, s, HEAD_DIM), lambda b_, i: (S_KIDX, b_, 0)),
            pl.BlockSpec((IDX_HEADS, TQ), lambda b_, i: (0, b_ * nq + i)),
            _const_spec(BIAS_ROWS_SHAPE), _const_spec((TK, TK))],
        out_specs=_out_spec(nq),
        out_shape=jax.ShapeDtypeStruct((b * s, N_HEADS * HEAD_DIM), BF16),
        scratch_shapes=[pltpu.VMEM((nq, TK, TQ), F32), pltpu.VMEM(BIAS_TILES_SHAPE, F32)],
        compiler_params=_cparams(2),
        name="dsa_attn",
    )(zz, zz, vt, zz, zz, zz, wt, bw, tril)


def _moba_kernel(q_ref, k_ref, vt_ref, bw_ref, o_ref, km_ref, bt_ref, *, nblk, topb):
    i = pl.program_id(1)
    _fill_bias_tiles(bw_ref, bt_ref)
    nrow = km_ref.shape[1]

    @pl.when(i == 0)
    def _():
        km_ref[...] = jnp.zeros_like(km_ref)
        for h in range(N_HEADS):
            for n in range(nblk):
                kb = k_ref[h, n * MOBA_BLOCK:(n + 1) * MOBA_BLOCK, :].astype(F32)
                km_ref[h, n:n + 1, :] = jnp.mean(kb, axis=0, keepdims=True)

    blk = lax.broadcasted_iota(jnp.int32, (nrow, TQ), 0)
    past = blk < i
    head_bits = []
    for h in range(N_HEADS):
        gate = _dot_nt(km_ref[h].astype(BF16), q_ref[h])
        bits = jnp.zeros((1, TQ), F32)
        for n in range(nblk):
            gn = gate[n:n + 1, :]
            beats = jnp.where(past, jnp.where(gate > gn, 1.0, jnp.where(gate == gn, jnp.where(blk < n, 1.0, 0.0), 0.0)), 0.0)
            rank = jnp.sum(beats, axis=0, keepdims=True)
            bits = bits + jnp.where(rank < float(topb), jnp.where(n < i, float(2 ** n), 0.0), 0.0)
        head_bits.append(bits.astype(jnp.int32) | lax.shift_left(jnp.int32(1), i))

    def values(n):
        return [_vt_block(vt_ref, h, n) for h in range(N_HEADS)]

    def logits(n):
        s_list = []
        for h in range(N_HEADS):
            picked = (lax.shift_right_logical(head_bits[h], jnp.full_like(head_bits[h], n)) & 1) == 1
            s_list.append(_dot_nt(_k_block(k_ref, h, n), q_ref[h]) + bt_ref[h, jnp.minimum(i - n, 2)]
                          + jnp.where(picked, 0.0, NEG))
        return s_list

    carry = _softmax_loop(i + 1, logits, values, tuple(_softmax_init() for _ in range(N_HEADS)))
    _store_heads(o_ref, [_softmax_out(c) for c in carry])


def _moba_attention(zz, vt, b, s, bw):
    nq = s // TQ
    nblk = s // MOBA_BLOCK
    topb = min(MOBA_TOPK, nblk - 1)
    return pl.pallas_call(
        functools.partial(_moba_kernel, nblk=nblk, topb=topb),
        grid=(b, nq),
        in_specs=_attn_specs(nq, G_MB_Q, G_MB_K, M_MB, s) + [_const_spec(BIAS_ROWS_SHAPE)],
        out_specs=_out_spec(nq),
        out_shape=jax.ShapeDtypeStruct((b * s, N_HEADS * HEAD_DIM), BF16),
        scratch_shapes=[pltpu.VMEM((N_HEADS, max(8, nblk), HEAD_DIM), F32), pltpu.VMEM(BIAS_TILES_SHAPE, F32)],
        compiler_params=_cparams(2),
        name="moba_attn",
    )(zz, zz, vt, bw)


def _merge_kernel(x_ref, osb_ref, odf_ref, ods_ref, omb_ref, gpre_ref, wg_ref, wbr_ref, wout_ref, gpost_ref, o_ref):
    x = x_ref[...]
    h = _rms(x, gpre_ref[...]).astype(BF16)
    y = jnp.zeros((x.shape[0], D_MODEL), F32)
    for r, o_r in enumerate((osb_ref, odf_ref, ods_ref, omb_ref)):
        gate = jax.nn.sigmoid(_dot(h, wg_ref[:, r * D_MODEL:(r + 1) * D_MODEL]))
        y = y + gate * _dot(o_r[...], wbr_ref[r])
    o_ref[...] = x + _rms(_dot(y.astype(BF16), wout_ref[...]), gpost_ref[...])


def _merge(x, o_sb, o_df, o_ds, o_mb, g_pre, w_gate, w_br, w_out, g_post, layer):
    t = x.shape[0]
    tm = TM_MERGE
    tok = lambda width: pl.BlockSpec((tm, width), lambda i: (i, 0))
    return pl.pallas_call(
        _merge_kernel,
        grid=(t // tm,),
        in_specs=[tok(D_MODEL)] + [tok(MIXER_WIDTH)] * N_MIXERS + [
            _const_spec((1, D_MODEL)), _layer_spec((D_MODEL, N_MIXERS * D_MODEL), layer),
            _layer_spec((N_MIXERS, MIXER_WIDTH, D_MODEL), layer), _layer_spec((D_MODEL, D_MODEL), layer),
            _const_spec((1, D_MODEL))],
        out_specs=tok(D_MODEL),
        out_shape=jax.ShapeDtypeStruct((t, D_MODEL), F32),
        compiler_params=_cparams(1),
        name="merge",
    )(x, o_sb, o_df, o_ds, o_mb, g_pre, w_gate, w_br, w_out, g_post)


def _ffn_kernel(x_ref, gpre_ref, win_ref, wout_ref, gpost_ref, o_ref):
    x = x_ref[...]
    h = _rms(x, gpre_ref[...]).astype(BF16)
    gate = _dot(h, win_ref[:, 0:D_FF])
    up = _dot(h, win_ref[:, D_FF:2 * D_FF])
    act = (gate * jax.nn.sigmoid(gate) * up).astype(BF16)
    o_ref[...] = x + _rms(_dot(act, wout_ref[...]), gpost_ref[...])


def _ffn(x, g_pre, w_in, w_out, g_post, layer):
    t = x.shape[0]
    tm = TM_FFN
    return pl.pallas_call(
        _ffn_kernel,
        grid=(t // tm,),
        in_specs=[pl.BlockSpec((tm, D_MODEL), lambda i: (i, 0)), _const_spec((1, D_MODEL)),
                  _layer_spec((D_MODEL, 2 * D_FF), layer), _layer_spec((D_FF, D_MODEL), layer),
                  _const_spec((1, D_MODEL))],
        out_specs=pl.BlockSpec((tm, D_MODEL), lambda i: (i, 0)),
        out_shape=jax.ShapeDtypeStruct((t, D_MODEL), F32),
        compiler_params=_cparams(1),
        name="ffn",
    )(x, g_pre, w_in, w_out, g_post)


def _t5_bucket(dist):
    max_exact = N_BUCKETS // 2
    d = jnp.maximum(dist, 0)
    log_ratio = jnp.log(jnp.maximum(d, 1).astype(F32) / max_exact) / math.log(MAX_DISTANCE / max_exact)
    large = jnp.minimum(max_exact + (log_ratio * (N_BUCKETS - max_exact)).astype(jnp.int32), N_BUCKETS - 1)
    return jnp.where(d < max_exact, d, large)


def _bias_rows(rel_bias):
    assert TQ == TK
    n = TK
    d = np.arange(-(n - 1), 3 * n + 1)
    by_dist = rel_bias.astype(F32).T[:, _t5_bucket(jnp.asarray(np.maximum(d, 0), jnp.int32))]
    by_dist = jnp.where(jnp.asarray(d >= 0)[None, :], by_dist * LOG2E, NEG)
    return jnp.stack([by_dist[:, o * n:o * n + 2 * n] for o in range(3)], axis=1)[:, :, None, :]


def _pack_weights(w_in):
    parts, start = [], 0
    for e in range(1, N_PACK + 1):
        if e == N_PACK or _PACK_SRC[e] != _PACK_SRC[e - 1] + (1 if _PACK_SRC[e - 1] >= 0 else 0):
            a = int(_PACK_SRC[start])
            n = e - start
            parts.append(w_in[:, :, a:a + n] if a >= 0 else jnp.zeros(w_in.shape[:2] + (n,), w_in.dtype))
            start = e
    return jnp.concatenate(parts, axis=2)


def kernel(x, w_in, w_br_sb, w_br_diff, w_br_dsa, w_br_moba, w_out, lambda_q1, lambda_k1, lambda_q2, lambda_k2,
           diff_subln_g, rel_bias, w_ffn_in, w_ffn_out, g_pre_mix, g_post_mix, g_pre_ffn, g_post_ffn):
    b, s, d = x.shape
    depth = w_in.shape[0]
    assert d == D_MODEL and s % TQ == 0 and s // MOBA_BLOCK >= 2
    t = b * s

    w_bf = w_in.astype(BF16)
    w_pack = _pack_weights(w_bf)
    w_gate = w_bf[:, :, _OFF["gate"]:]
    w_br = jnp.stack([w_br_sb, w_br_diff, w_br_dsa, w_br_moba], axis=1).astype(BF16)
    w_o = w_out.astype(BF16)
    w_f1 = w_ffn_in.astype(BF16)
    w_f2 = w_ffn_out.astype(BF16)
    cs = jnp.asarray(_PACK_SCALE)[None, :]

    bw = _bias_rows(rel_bias)
    bw_df, bw_ds, bw_mb = bw[0:4], bw[4:8], bw[8:12]
    key = np.arange(TK)[:, None]
    qry = np.arange(TQ)[None, :]
    tri = jnp.asarray(key <= np.arange(TK)[None, :], BF16)
    tril = jnp.asarray(key >= np.arange(TK)[None, :], BF16)
    sb_mask = jnp.asarray(np.where(key < qry, 0.0, NEG), F32)

    xf = x.reshape(t, d)
    for l in range(depth):
        lam_init = 0.8 - 0.6 * math.exp(-0.3 * l)
        lamp = jnp.stack([lambda_q1[l], lambda_k1[l], lambda_q2[l], lambda_k2[l]]).astype(F32)
        cst = jnp.full((1, 128), lam_init, F32)
        zz, vt, wt = _proj(xf, g_pre_mix[l][None, :], w_pack, cs, l)
        o_sb = _sb_attention(zz, vt, b, s, tri, sb_mask)
        o_df = _diff_attention(zz, vt, b, s, bw_df, lamp, cst, diff_subln_g[l][:, None])
        o_ds = _dsa_attention(zz, vt, wt, b, s, bw_ds, tril)
        o_mb = _moba_attention(zz, vt, b, s, bw_mb)
        xf = _merge(xf, o_sb, o_df, o_ds, o_mb, g_pre_mix[l][None, :], w_gate, w_br, w_o, g_post_mix[l][None, :], l)
        xf = _ffn(xf, g_pre_ffn[l][None, :], w_f1, w_f2, g_post_ffn[l][None, :], l)
    return xf.reshape(b, s, d)
```

```python
import functools
import math

import numpy as np
import jax
import jax.numpy as jnp
from jax import lax
from jax.experimental import pallas as pl
from jax.experimental.pallas import tpu as pltpu

F32 = jnp.float32
BF16 = jnp.bfloat16

D_MODEL = 1024
HEAD_DIM = 64
N_HEADS = 4
N_MIXERS = 4
MIXER_WIDTH = N_HEADS * HEAD_DIM
DIFF_QK_DIM = 32
IDX_HEADS = 8
DSA_TOPK_MAX = 256
MOBA_BLOCK = 256
MOBA_TOPK = 3
N_BUCKETS = 32
MAX_DISTANCE = 128
D_FF = 2816
NORM_EPS = 1e-6

TQ = 256
TK = 256
NEG = -1e30
HALF_NEG = -0.5e30
BIG = 3e38
BISECT_WARMUP = 20
BISECT_TRIP = 2
BISECT_MAX_TRIPS = 134
LOG2E = math.log2(math.e)
SB_TAIL_CUTOFF = 120.0 * LOG2E
V_ROWS = 80
TM_MERGE = 512
TM_FFN = 512
N_GROUP = 11
N_SLAB = 4 * N_GROUP
N_PACK = (N_GROUP + N_MIXERS) * MIXER_WIDTH
VMEM_LIMIT = 56 * 1024 * 1024

G_SB_Q, G_SB_K, G_DF_Q, G_DF_K, G_DS_Q, G_DS_K, G_QI_A, G_QI_B, G_MB_Q, G_MB_K, G_KIDX = range(N_GROUP)
S_KIDX = 4 * G_KIDX
M_SB, M_DF, M_DS, M_MB = range(N_MIXERS)


def _layout():
    off = {}
    acc = 0
    for name, sz in (("q_sb", 256), ("k_sb", 256), ("v_sb", 256), ("q1", 128), ("q2", 128), ("k1", 128),
                     ("k2", 128), ("v_df", 256), ("q_ds", 256), ("k_ds", 256), ("v_ds", 256), ("qi", 512),
                     ("ki", 64), ("wi", 8), ("q_mb", 256), ("k_mb", 256), ("v_mb", 256), ("gate", 4096)):
        off[name] = acc
        acc += sz
    return off


_OFF = _layout()


def _pack_layout():
    off = _OFF
    cols, scale = [], []

    def add(start, n, s=1.0):
        cols.extend(range(start, start + n))
        scale.extend([s] * n)

    hd = HEAD_DIM ** -0.5
    hd2 = hd * LOG2E
    df2 = DIFF_QK_DIM ** -0.5 * LOG2E
    add(off["q_sb"], 256, hd2); add(off["k_sb"], 256)
    for h in range(N_HEADS):
        add(off["q1"] + h * 32, 32, df2); add(off["q2"] + h * 32, 32, df2)
    for h in range(N_HEADS):
        add(off["k1"] + h * 32, 32); add(off["k2"] + h * 32, 32)
    add(off["q_ds"], 256, hd2); add(off["k_ds"], 256)
    add(off["qi"], 512, HEAD_DIM ** -0.5)
    add(off["q_mb"], 256, hd2); add(off["k_mb"], 256)
    add(off["ki"], 64); add(off["wi"], IDX_HEADS, IDX_HEADS ** -0.5)
    cols.extend([-1] * 184); scale.extend([1.0] * 184)
    for name in ("v_sb", "v_df", "v_ds", "v_mb"):
        add(off[name], MIXER_WIDTH)
    assert len(cols) == N_PACK
    return np.asarray(cols, np.int32), np.asarray(scale, np.float32)


_PACK_SRC, _PACK_SCALE = _pack_layout()


def _dot(a, b):
    return jnp.dot(a, b, preferred_element_type=F32)


def _dot_nt(a, b):
    return lax.dot_general(a, b, (((1,), (1,)), ((), ())), preferred_element_type=F32)


def _rms(x, g):
    return x * lax.rsqrt(jnp.mean(x * x, axis=-1, keepdims=True) + NORM_EPS) * g


def _cparams(n_axes):
    return pltpu.CompilerParams(dimension_semantics=("arbitrary",) * n_axes, vmem_limit_bytes=VMEM_LIMIT)


def _const_spec(shape):
    nd = len(shape)
    return pl.BlockSpec(shape, lambda *_: (0,) * nd, pipeline_mode=pl.Buffered(1))


def _layer_spec(shape, layer):
    nd = len(shape)
    return pl.BlockSpec((pl.Squeezed(),) + tuple(shape), lambda *_: (layer,) + (0,) * nd, pipeline_mode=pl.Buffered(1))


def _proj_kernel(x_ref, g_ref, w_ref, cs_ref, zz_ref, vt_ref, wt_ref):
    h = _rms(x_ref[...], g_ref[...]).astype(BF16)

    def group(c):
        cols = slice(c * MIXER_WIDTH, (c + 1) * MIXER_WIDTH)
        return _dot(h, w_ref[:, cols]) * cs_ref[:, cols]

    for c in range(N_GROUP):
        r = group(c)
        for s in range(4):
            zz_ref[4 * c + s] = r[:, s * HEAD_DIM:(s + 1) * HEAD_DIM].astype(BF16)
        if c == G_KIDX:
            wt_ref[...] = jnp.transpose(r)[HEAD_DIM:HEAD_DIM + IDX_HEADS]
    for m in range(N_MIXERS):
        vt = jnp.transpose(group(N_GROUP + m)).astype(BF16)
        for hh in range(N_HEADS):
            vt_ref[m, 0, hh * V_ROWS:hh * V_ROWS + HEAD_DIM, :] = vt[hh * HEAD_DIM:(hh + 1) * HEAD_DIM]
            vt_ref[m, 0, hh * V_ROWS + HEAD_DIM:(hh + 1) * V_ROWS, :] = jnp.ones((V_ROWS - HEAD_DIM, TQ), BF16)


def _proj(x, g, w, cs, layer):
    t = x.shape[0]
    tm = TQ
    return pl.pallas_call(
        _proj_kernel,
        grid=(t // tm,),
        in_specs=[pl.BlockSpec((tm, D_MODEL), lambda i: (i, 0)),
                  _const_spec((1, D_MODEL)),
                  _layer_spec((D_MODEL, N_PACK), layer),
                  _const_spec((1, N_PACK))],
        out_specs=[pl.BlockSpec((N_SLAB, tm, HEAD_DIM), lambda i: (0, i, 0)),
                   pl.BlockSpec((N_MIXERS, 1, N_HEADS * V_ROWS, tm), lambda i: (0, i, 0, 0)),
                   pl.BlockSpec((IDX_HEADS, tm), lambda i: (0, i))],
        out_shape=[jax.ShapeDtypeStruct((N_SLAB, t, HEAD_DIM), BF16),
                   jax.ShapeDtypeStruct((N_MIXERS, t // tm, N_HEADS * V_ROWS, tm), BF16),
                   jax.ShapeDtypeStruct((IDX_HEADS, t), F32)],
        compiler_params=_cparams(1),
        name="proj",
    )(x, g, w, cs)


def _k_block(ref, h, j):
    return ref[h, pl.ds(pl.multiple_of(j * TK, TK), TK), :]


def _fold_keys(a, op):
    n = a.shape[0]
    while n > 8:
        n //= 2
        a = op(a[:n], a[n:2 * n])
    return a


def _vt_block(ref, h, j, rows=V_ROWS):
    return ref[0, j, h * V_ROWS:h * V_ROWS + rows, :]


def _softmax_block(s_list, vt_list, carry):
    ms = [jnp.maximum(c[0], jnp.max(_fold_keys(s, jnp.maximum), axis=0, keepdims=True))
          for s, c in zip(s_list, carry)]
    pvs = [_dot(vt, jnp.exp2(s - m).astype(BF16)) for vt, s, m in zip(vt_list, s_list, ms)]
    return tuple((m_new, jnp.exp2(m - m_new) * acc + pv) for (m, acc), m_new, pv in zip(carry, ms, pvs))


def _softmax_loop(n_blocks, logits, values, carry, groups=(2, 1)):
    assert groups[-1] == 1
    start = 0
    for g in groups:
        def body(t, carry, g=g, start=start):
            j = start + g * t
            s_all = [logits(j + u) for u in range(g)]
            for u in range(g):
                carry = _softmax_block(s_all[u], values(j + u), carry)
            return carry

        trips = (n_blocks - start) // g
        carry = lax.fori_loop(0, trips, body, carry)
        start = start + g * trips
    return carry


def _softmax_init():
    return (jnp.full((1, TQ), NEG, F32), jnp.zeros((V_ROWS, TQ), F32))


def _softmax_out(carry):
    _, acc = carry
    return acc[:HEAD_DIM] / acc[HEAD_DIM:HEAD_DIM + 1]


def _store_heads(o_ref, heads_t):
    o_ref[...] = jnp.transpose(jnp.concatenate(heads_t, axis=0)).astype(BF16)


BIAS_ROWS_SHAPE = (N_HEADS, 3, 1, 2 * TQ)
BIAS_TILES_SHAPE = (N_HEADS, 3, TK, TQ)


def _fill_bias_tiles(bw_ref, bt_ref):
    @pl.when((pl.program_id(0) == 0) & (pl.program_id(1) == 0))
    def _():
        for h in range(N_HEADS):
            for o in range(3):
                rows = jnp.broadcast_to(bw_ref[h, o], (TK, 2 * TQ))
                bt_ref[h, o] = pltpu.roll(rows, TQ + 1, 1, stride=1, stride_axis=0)[:, :TQ]


def _attn_specs(nq, gq, gk, mixer, s):
    return [pl.BlockSpec((4, TQ, HEAD_DIM), lambda b, i: (gq, b * nq + i, 0)),
            pl.BlockSpec((4, s, HEAD_DIM), lambda b, i: (gk, b, 0)),
            pl.BlockSpec((1, s // TK, N_HEADS * V_ROWS, TK), lambda b, i: (mixer, b, 0, 0))]


def _out_spec(nq):
    return pl.BlockSpec((TQ, N_HEADS * HEAD_DIM), lambda b, i: (b * nq + i, 0))


def _sb_kernel(q_ref, k_ref, vt_ref, tri_ref, mask_ref, o_ref):
    i = pl.program_id(1)
    tri = tri_ref[...]

    def block(j, carry, masked):
        heads = range(N_HEADS)
        zs = [_dot_nt(_k_block(k_ref, h, j), q_ref[h]) for h in heads]
        if masked:
            zs = [z + mask_ref[...] for z in zs]
        sps = [jnp.maximum(z, 0.0) + jnp.log2(1.0 + jnp.exp2(-jnp.abs(z))) for z in zs]
        his = [sp.astype(BF16) for sp in sps]
        los = [(sp - hi.astype(F32)).astype(BF16) for sp, hi in zip(sps, his)]
        cs = [_dot(tri, hi) + _dot(tri, lo) + c[0] for hi, lo, c in zip(his, los, carry)]
        avs = [_dot(_vt_block(vt_ref, h, j, HEAD_DIM), jnp.exp2(z - c).astype(BF16))
               for h, z, c in zip(heads, zs, cs)]
        return tuple((c[0:1, :], old[1] + av) for c, old, av in zip(cs, carry, avs))

    init = tuple((jnp.zeros((1, TQ), F32), jnp.zeros((HEAD_DIM, TQ), F32)) for _ in range(N_HEADS))
    carry = block(i, init, True)

    def weights_left(carry):
        tail = functools.reduce(jnp.minimum, [c[0] for c in carry])
        return (jnp.min(tail) < SB_TAIL_CUTOFF).astype(jnp.int32)

    def earlier_block(c):
        j, carry, _ = c
        carry = block(j, carry, False)
        return j - 1, carry, weights_left(carry)

    _, carry, _ = lax.while_loop(lambda c: (c[0] >= 0) & (c[2] > 0), earlier_block, (i - 1, carry, weights_left(carry)))
    _store_heads(o_ref, [c[1] for c in carry])


def _sb_attention(zz, vt, b, s, tri, mask):
    nq = s // TQ
    return pl.pallas_call(
        _sb_kernel,
        grid=(b, nq),
        in_specs=_attn_specs(nq, G_SB_Q, G_SB_K, M_SB, s) + [_const_spec((TK, TK)), _const_spec((TK, TQ))],
        out_specs=_out_spec(nq),
        out_shape=jax.ShapeDtypeStruct((b * s, N_HEADS * HEAD_DIM), BF16),
        compiler_params=_cparams(2),
        name="sb_attn",
    )(zz, zz, vt, tri, mask)


def _diff_kernel(q_ref, k_ref, vt_ref, bw_ref, lam_ref, cst_ref, g_ref, o_ref, bt_ref):
    i = pl.program_id(1)
    _fill_bias_tiles(bw_ref, bt_ref)
    lp = lam_ref[...]
    lam_init = cst_ref[:, 0:1]
    lam = (jnp.exp(jnp.sum(lp[0:1] * lp[1:2], axis=-1, keepdims=True))
           - jnp.exp(jnp.sum(lp[2:3] * lp[3:4], axis=-1, keepdims=True)) + lam_init)
    lane = lax.broadcasted_iota(jnp.int32, (TQ, HEAD_DIM), 1)
    qs = []
    for h in range(N_HEADS):
        q = q_ref[h]
        qs.append((jnp.where(lane < DIFF_QK_DIM, q, jnp.zeros_like(q)),
                   jnp.where(lane >= DIFF_QK_DIM, q, jnp.zeros_like(q))))

    def logits(j):
        s_list = []
        for h in range(N_HEADS):
            kj = _k_block(k_ref, h, j)
            bias = bt_ref[h, jnp.minimum(i - j, 2)]
            s_list += [_dot_nt(kj, qs[h][0]) + bias, _dot_nt(kj, qs[h][1]) + bias]
        return s_list

    def values(j):
        return [_vt_block(vt_ref, h, j) for h in range(N_HEADS) for _ in range(2)]

    carry = _softmax_loop(i + 1, logits, values, tuple(_softmax_init() for _ in range(2 * N_HEADS)))
    outs = []
    for h in range(N_HEADS):
        o = _softmax_out(carry[2 * h]) - lam * _softmax_out(carry[2 * h + 1])
        o = o * lax.rsqrt(jnp.mean(o * o, axis=0, keepdims=True) + NORM_EPS) * g_ref[...]
        outs.append(o * (1.0 - lam_init))
    _store_heads(o_ref, outs)


def _diff_attention(zz, vt, b, s, bw, lamp, cst, g):
    nq = s // TQ
    return pl.pallas_call(
        _diff_kernel,
        grid=(b, nq),
        in_specs=_attn_specs(nq, G_DF_Q, G_DF_K, M_DF, s) + [
            _const_spec(BIAS_ROWS_SHAPE), _const_spec((4, DIFF_QK_DIM)), _const_spec((1, 128)),
            _const_spec((HEAD_DIM, 1))],
        out_specs=_out_spec(nq),
        out_shape=jax.ShapeDtypeStruct((b * s, N_HEADS * HEAD_DIM), BF16),
        scratch_shapes=[pltpu.VMEM(BIAS_TILES_SHAPE, F32)],
        compiler_params=_cparams(2),
        name="diff_attn",
    )(zz, zz, vt, bw, lamp, cst, g)


def _dsa_kernel(q_ref, k_ref, vt_ref, qia_ref, qib_ref, ki_ref, wt_ref, bw_ref, tril_ref, o_ref, sc_ref, bt_ref, *,
                topk):
    i = pl.program_id(1)
    _fill_bias_tiles(bw_ref, bt_ref)
    nb = i + 1
    kf = float(topk)
    w = wt_ref[...]
    key = lax.broadcasted_iota(jnp.int32, (TK, TQ), 0)
    qry = lax.broadcasted_iota(jnp.int32, (TK, TQ), 1)

    def score(j):
        kij = _k_block(ki_ref, 0, j)
        sc = jnp.zeros((TK, TQ), F32)
        for hh in range(IDX_HEADS):
            qi = (qia_ref if hh < 4 else qib_ref)[hh % 4]
            sc = sc + w[hh:hh + 1, :] * jnp.maximum(_dot_nt(kij, qi), 0.0)
        return sc

    def extend(lo_src, hi_src, c):
        return (jnp.minimum(c[0], _fold_keys(lo_src, jnp.minimum)), jnp.maximum(c[1], _fold_keys(hi_src, jnp.maximum)))

    def earlier_block(j, c):
        sc = score(j)
        sc_ref[j] = sc
        return extend(sc, sc, c)

    def earlier_pair(t, c):
        sc_a, sc_b = score(2 * t), score(2 * t + 1)
        sc_ref[2 * t] = sc_a
        sc_ref[2 * t + 1] = sc_b
        return extend(sc_b, sc_b, extend(sc_a, sc_a, c))

    lo_part, hi_part = lax.fori_loop(0, i // 2, earlier_pair,
                                     (jnp.full((8, TQ), BIG, F32), jnp.full((8, TQ), -BIG, F32)))
    lo_part, hi_part = lax.fori_loop(2 * (i // 2), i, earlier_block, (lo_part, hi_part))
    sc = score(i)
    causal = key <= qry
    sc_ref[i] = jnp.where(causal, sc, NEG)
    lo_part, hi_part = extend(jnp.where(causal, sc, BIG), jnp.where(causal, sc, NEG), (lo_part, hi_part))

    def reduce_blocks(fn, init):
        def pair(t, c):
            return fn(sc_ref[2 * t + 1], 2 * t + 1, fn(sc_ref[2 * t], 2 * t, c))
        c = lax.fori_loop(0, nb // 2, pair, init)
        return lax.fori_loop(2 * (nb // 2), nb, lambda j, c: fn(sc_ref[j], j, c), c)

    def count_ge(t):
        part = reduce_blocks(lambda x, j, c: c + _fold_keys(jnp.where(x >= t, 1.0, 0.0), jnp.add),
                             jnp.zeros((8, TQ), F32))
        return jnp.sum(part, axis=0, keepdims=True)

    def minmax_blocks(lo_of, hi_of):
        def f(x, j, c):
            return (jnp.minimum(c[0], _fold_keys(lo_of(x), jnp.minimum)),
                    jnp.maximum(c[1], _fold_keys(hi_of(x), jnp.maximum)))
        lo_part, hi_part = reduce_blocks(f, (jnp.full((8, TQ), BIG, F32), jnp.full((8, TQ), -BIG, F32)))
        return jnp.min(lo_part, axis=0, keepdims=True), jnp.max(hi_part, axis=0, keepdims=True)

    n_valid = i * TQ + lax.broadcasted_iota(jnp.int32, (1, TQ), 1) + 1
    take_all = n_valid <= topk
    lo = jnp.min(lo_part, axis=0, keepdims=True)
    hi = jnp.max(hi_part, axis=0, keepdims=True)
    c_max = count_ge(hi)
    at_max = c_max >= kf
    state = (jnp.where(at_max, hi, lo), hi, jnp.where(at_max, c_max, n_valid.astype(F32)), c_max)

    def bisect(_, state):
        lo, hi, c_lo, c_hi = state
        mid = 0.5 * lo + 0.5 * hi
        c = count_ge(mid)
        ge = c >= kf
        return jnp.where(ge, mid, lo), jnp.where(ge, hi, mid), jnp.where(ge, c, c_lo), jnp.where(ge, c_hi, c)

    def unsettled(state):
        lo, hi, c_lo, _ = state
        open_q = jnp.where(take_all, 0.0, jnp.where(c_lo != kf, jnp.where(lo < hi, 1.0, 0.0), 0.0))

        def band_spread():
            b_min, b_max = minmax_blocks(lambda x: jnp.where(x >= lo, jnp.where(x < hi, x, BIG), BIG),
                                         lambda x: jnp.where(x >= lo, jnp.where(x < hi, x, -BIG), -BIG))
            return (jnp.max(jnp.where(b_max != b_min, open_q, 0.0)) > 0.0).astype(jnp.int32)

        return lax.cond(jnp.max(open_q) > 0.0, band_spread, lambda: jnp.int32(0))

    state = lax.fori_loop(0, jnp.where((i + 1) * TQ <= topk, 0, BISECT_WARMUP), bisect, state)

    def trip(c):
        n, state, _ = c
        state = lax.fori_loop(0, BISECT_TRIP, bisect, state)
        return n + 1, state, unsettled(state)

    _, state, _ = lax.while_loop(lambda c: (c[2] > 0) & (c[0] < BISECT_MAX_TRIPS), trip,
                                 (jnp.int32(0), state, unsettled(state)))
    lo, hi, c_lo, c_hi = state
    hi_ok = lo < hi
    c_above = jnp.where(hi_ok, c_hi, 0.0)
    hi_sel = jnp.where(hi_ok, hi, BIG)
    need = jnp.where(take_all, BIG, kf - c_above)
    lo_sel = jnp.where(take_all, HALF_NEG, lo)

    tied = jnp.max(jnp.where(take_all, 0.0, c_lo - kf)) > 0.0

    @pl.when(tied)
    def _():
        tril = tril_ref[...]

        def band_of(j):
            x = sc_ref[j]
            return x, jnp.where(x >= lo_sel, jnp.where(x < hi_sel, 1.0, 0.0), 0.0)

        def write(j, x, band, rank):
            sc_ref[j] = jnp.where(x >= hi_sel, 0.0,
                                  jnp.where(band * rank > 0.0, jnp.where(rank <= need, 0.0, NEG), NEG))
            return rank[TK - 1:TK, :]

        def write_mask(j, taken):
            x, band = band_of(j)
            return write(j, x, band, _dot(tril, band.astype(BF16)) + taken)

        def write_mask_pair(t, taken):
            (x_a, band_a), (x_b, band_b) = band_of(2 * t), band_of(2 * t + 1)
            in_a, in_b = _dot(tril, band_a.astype(BF16)), _dot(tril, band_b.astype(BF16))
            taken = write(2 * t, x_a, band_a, in_a + taken)
            return write(2 * t + 1, x_b, band_b, in_b + taken)

        taken = lax.fori_loop(0, nb // 2, write_mask_pair, jnp.zeros((1, TQ), F32))
        lax.fori_loop(2 * (nb // 2), nb, write_mask, taken)

    @pl.when(jnp.logical_not(tied))
    def _():
        def write_mask(j, _):
            sc_ref[j] = jnp.where(sc_ref[j] >= lo_sel, 0.0, NEG)
            return 0

        lax.fori_loop(0, nb, write_mask, 0)

    def logits(j):
        return [_dot_nt(_k_block(k_ref, h, j), q_ref[h]) + bt_ref[h, jnp.minimum(i - j, 2)] + sc_ref[j]
                for h in range(N_HEADS)]

    carry = _softmax_loop(nb, logits, lambda j: [_vt_block(vt_ref, h, j) for h in range(N_HEADS)],
                          tuple(_softmax_init() for _ in range(N_HEADS)), groups=(4, 2, 1))
    _store_heads(o_ref, [_softmax_out(c) for c in carry])


def _dsa_attention(zz, vt, wt, b, s, bw, tril):
    nq = s // TQ
    topk = min(DSA_TOPK_MAX, s // 4)
    return pl.pallas_call(
        functools.partial(_dsa_kernel, topk=topk),
        grid=(b, nq),
        in_specs=_attn_specs(nq, G_DS_Q, G_DS_K, M_DS, s) + [
            pl.BlockSpec((4, TQ, HEAD_DIM), lambda b_, i: (G_QI_A, b_ * nq + i, 0)),
            pl.BlockSpec((4, TQ, HEAD_DIM), lambda b_, i: (G_QI_B, b_ * nq + i, 0)),
            pl.BlockSpec((1, s, HEAD_DIM), lambda b_, i: (S_KIDX, b_, 0)),
            pl.BlockSpec((IDX_HEADS, TQ), lambda b_, i: (0, b_ * nq + i)),
            _const_spec(BIAS_ROWS_SHAPE), _const_spec((TK, TK))],
        out_specs=_out_spec(nq),
        out_shape=jax.ShapeDtypeStruct((b * s, N_HEADS * HEAD_DIM), BF16),
        scratch_shapes=[pltpu.VMEM((nq, TK, TQ), F32), pltpu.VMEM(BIAS_TILES_SHAPE, F32)],
        compiler_params=_cparams(2),
        name="dsa_attn",
    )(zz, zz, vt, zz, zz, zz, wt, bw, tril)


def _moba_kernel(q_ref, k_ref, vt_ref, bw_ref, o_ref, km_ref, bt_ref, *, nblk, topb):
    i = pl.program_id(1)
    _fill_bias_tiles(bw_ref, bt_ref)
    nrow = km_ref.shape[1]

    @pl.when(i == 0)
    def _():
        km_ref[...] = jnp.zeros_like(km_ref)
        for h in range(N_HEADS):
            for n in range(nblk):
                kb = k_ref[h, n * MOBA_BLOCK:(n + 1) * MOBA_BLOCK, :].astype(F32)
                km_ref[h, n:n + 1, :] = jnp.mean(kb, axis=0, keepdims=True)

    blk = lax.broadcasted_iota(jnp.int32, (nrow, TQ), 0)
    past = blk < i
    head_bits = []
    for h in range(N_HEADS):
        gate = _dot_nt(km_ref[h].astype(BF16), q_ref[h])
        bits = jnp.zeros((1, TQ), F32)
        for n in range(nblk):
            gn = gate[n:n + 1, :]
            beats = jnp.where(past, jnp.where(gate > gn, 1.0, jnp.where(gate == gn, jnp.where(blk < n, 1.0, 0.0), 0.0)), 0.0)
            rank = jnp.sum(beats, axis=0, keepdims=True)
            bits = bits + jnp.where(rank < float(topb), jnp.where(n < i, float(2 ** n), 0.0), 0.0)
        head_bits.append(bits.astype(jnp.int32) | lax.shift_left(jnp.int32(1), i))

    def values(n):
        return [_vt_block(vt_ref, h, n) for h in range(N_HEADS)]

    def logits(n):
        s_list = []
        for h in range(N_HEADS):
            picked = (lax.shift_right_logical(head_bits[h], jnp.full_like(head_bits[h], n)) & 1) == 1
            s_list.append(_dot_nt(_k_block(k_ref, h, n), q_ref[h]) + bt_ref[h, jnp.minimum(i - n, 2)]
                          + jnp.where(picked, 0.0, NEG))
        return s_list

    carry = _softmax_loop(i + 1, logits, values, tuple(_softmax_init() for _ in range(N_HEADS)), groups=(4, 2, 1))
    _store_heads(o_ref, [_softmax_out(c) for c in carry])


def _moba_attention(zz, vt, b, s, bw):
    nq = s // TQ
    nblk = s // MOBA_BLOCK
    topb = min(MOBA_TOPK, nblk - 1)
    return pl.pallas_call(
        functools.partial(_moba_kernel, nblk=nblk, topb=topb),
        grid=(b, nq),
        in_specs=_attn_specs(nq, G_MB_Q, G_MB_K, M_MB, s) + [_const_spec(BIAS_ROWS_SHAPE)],
        out_specs=_out_spec(nq),
        out_shape=jax.ShapeDtypeStruct((b * s, N_HEADS * HEAD_DIM), BF16),
        scratch_shapes=[pltpu.VMEM((N_HEADS, max(8, nblk), HEAD_DIM), F32), pltpu.VMEM(BIAS_TILES_SHAPE, F32)],
        compiler_params=_cparams(2),
        name="moba_attn",
    )(zz, zz, vt, bw)


def _merge_kernel(x_ref, osb_ref, odf_ref, ods_ref, omb_ref, gpre_ref, wg_ref, wbr_ref, wout_ref, gpost_ref, o_ref):
    x = x_ref[...]
    h = _rms(x, gpre_ref[...]).astype(BF16)
    y = jnp.zeros((x.shape[0], D_MODEL), F32)
    for r, o_r in enumerate((osb_ref, odf_ref, ods_ref, omb_ref)):
        gate = jax.nn.sigmoid(_dot(h, wg_ref[:, r * D_MODEL:(r + 1) * D_MODEL]))
        y = y + gate * _dot(o_r[...], wbr_ref[r])
    o_ref[...] = x + _rms(_dot(y.astype(BF16), wout_ref[...]), gpost_ref[...])


def _merge(x, o_sb, o_df, o_ds, o_mb, g_pre, w_gate, w_br, w_out, g_post, layer):
    t = x.shape[0]
    tm = TM_MERGE
    tok = lambda width: pl.BlockSpec((tm, width), lambda i: (i, 0))
    return pl.pallas_call(
        _merge_kernel,
        grid=(t // tm,),
        in_specs=[tok(D_MODEL)] + [tok(MIXER_WIDTH)] * N_MIXERS + [
            _const_spec((1, D_MODEL)), _layer_spec((D_MODEL, N_MIXERS * D_MODEL), layer),
            _layer_spec((N_MIXERS, MIXER_WIDTH, D_MODEL), layer), _layer_spec((D_MODEL, D_MODEL), layer),
            _const_spec((1, D_MODEL))],
        out_specs=tok(D_MODEL),
        out_shape=jax.ShapeDtypeStruct((t, D_MODEL), F32),
        compiler_params=_cparams(1),
        name="merge",
    )(x, o_sb, o_df, o_ds, o_mb, g_pre, w_gate, w_br, w_out, g_post)


def _ffn_kernel(x_ref, gpre_ref, win_ref, wout_ref, gpost_ref, o_ref):
    x = x_ref[...]
    h = _rms(x, gpre_ref[...]).astype(BF16)
    gate = _dot(h, win_ref[:, 0:D_FF])
    up = _dot(h, win_ref[:, D_FF:2 * D_FF])
    act = (gate * jax.nn.sigmoid(gate) * up).astype(BF16)
    o_ref[...] = x + _rms(_dot(act, wout_ref[...]), gpost_ref[...])


def _ffn(x, g_pre, w_in, w_out, g_post, layer):
    t = x.shape[0]
    tm = TM_FFN
    return pl.pallas_call(
        _ffn_kernel,
        grid=(t // tm,),
        in_specs=[pl.BlockSpec((tm, D_MODEL), lambda i: (i, 0)), _const_spec((1, D_MODEL)),
                  _layer_spec((D_MODEL, 2 * D_FF), layer), _layer_spec((D_FF, D_MODEL), layer),
                  _const_spec((1, D_MODEL))],
        out_specs=pl.BlockSpec((tm, D_MODEL), lambda i: (i, 0)),
        out_shape=jax.ShapeDtypeStruct((t, D_MODEL), F32),
        compiler_params=_cparams(1),
        name="ffn",
    )(x, g_pre, w_in, w_out, g_post)


def _t5_bucket(dist):
    max_exact = N_BUCKETS // 2
    d = jnp.maximum(dist, 0)
    log_ratio = jnp.log(jnp.maximum(d, 1).astype(F32) / max_exact) / math.log(MAX_DISTANCE / max_exact)
    large = jnp.minimum(max_exact + (log_ratio * (N_BUCKETS - max_exact)).astype(jnp.int32), N_BUCKETS - 1)
    return jnp.where(d < max_exact, d, large)


def _bias_rows(rel_bias):
    assert TQ == TK
    n = TK
    d = np.arange(-(n - 1), 3 * n + 1)
    by_dist = rel_bias.astype(F32).T[:, _t5_bucket(jnp.asarray(np.maximum(d, 0), jnp.int32))]
    by_dist = jnp.where(jnp.asarray(d >= 0)[None, :], by_dist * LOG2E, NEG)
    return jnp.stack([by_dist[:, o * n:o * n + 2 * n] for o in range(3)], axis=1)[:, :, None, :]


def _pack_weights(w_in):
    parts, start = [], 0
    for e in range(1, N_PACK + 1):
        if e == N_PACK or _PACK_SRC[e] != _PACK_SRC[e - 1] + (1 if _PACK_SRC[e - 1] >= 0 else 0):
            a = int(_PACK_SRC[start])
            n = e - start
            parts.append(w_in[:, :, a:a + n] if a >= 0 else jnp.zeros(w_in.shape[:2] + (n,), w_in.dtype))
            start = e
    return jnp.concatenate(parts, axis=2)


def kernel(x, w_in, w_br_sb, w_br_diff, w_br_dsa, w_br_moba, w_out, lambda_q1, lambda_k1, lambda_q2, lambda_k2,
           diff_subln_g, rel_bias, w_ffn_in, w_ffn_out, g_pre_mix, g_post_mix, g_pre_ffn, g_post_ffn):
    b, s, d = x.shape
    depth = w_in.shape[0]
    assert d == D_MODEL and s % TQ == 0 and s // MOBA_BLOCK >= 2
    t = b * s

    w_bf = w_in.astype(BF16)
    w_pack = _pack_weights(w_bf)
    w_gate = w_bf[:, :, _OFF["gate"]:]
    w_br = jnp.stack([w_br_sb, w_br_diff, w_br_dsa, w_br_moba], axis=1).astype(BF16)
    w_o = w_out.astype(BF16)
    w_f1 = w_ffn_in.astype(BF16)
    w_f2 = w_ffn_out.astype(BF16)
    cs = jnp.asarray(_PACK_SCALE)[None, :]

    bw = _bias_rows(rel_bias)
    bw_df, bw_ds, bw_mb = bw[0:4], bw[4:8], bw[8:12]
    key = np.arange(TK)[:, None]
    qry = np.arange(TQ)[None, :]
    tri = jnp.asarray(key <= np.arange(TK)[None, :], BF16)
    tril = jnp.asarray(key >= np.arange(TK)[None, :], BF16)
    sb_mask = jnp.asarray(np.where(key < qry, 0.0, NEG), F32)

    xf = x.reshape(t, d)
    for l in range(depth):
        lam_init = 0.8 - 0.6 * math.exp(-0.3 * l)
        lamp = jnp.stack([lambda_q1[l], lambda_k1[l], lambda_q2[l], lambda_k2[l]]).astype(F32)
        cst = jnp.full((1, 128), lam_init, F32)
        zz, vt, wt = _proj(xf, g_pre_mix[l][None, :], w_pack, cs, l)
        o_sb = _sb_attention(zz, vt, b, s, tri, sb_mask)
        o_df = _diff_attention(zz, vt, b, s, bw_df, lamp, cst, diff_subln_g[l][:, None])
        o_ds = _dsa_attention(zz, vt, wt, b, s, bw_ds, tril)
        o_mb = _moba_attention(zz, vt, b, s, bw_mb)
        xf = _merge(xf, o_sb, o_df, o_ds, o_mb, g_pre_mix[l][None, :], w_gate, w_br, w_o, g_post_mix[l][None, :], l)
        xf = _ffn(xf, g_pre_ffn[l][None, :], w_f1, w_f2, g_post_ffn[l][None, :], l)
    return xf.reshape(b, s, d)
```

```python
import functools
import math

import numpy as np
import jax
import jax.numpy as jnp
from jax import lax
from jax.experimental import pallas as pl
from jax.experimental.pallas import tpu as pltpu

F32 = jnp.float32
BF16 = jnp.bfloat16

D_MODEL = 1024
HEAD_DIM = 64
N_HEADS = 4
N_MIXERS = 4
MIXER_WIDTH = N_HEADS * HEAD_DIM
DIFF_QK_DIM = 32
IDX_HEADS = 8
DSA_TOPK_MAX = 256
MOBA_BLOCK = 256
MOBA_TOPK = 3
N_BUCKETS = 32
MAX_DISTANCE = 128
D_FF = 2816
NORM_EPS = 1e-6

TQ = 256
TK = 256
NEG = -1e30
HALF_NEG = -0.5e30
BIG = 3e38
BISECT_WARMUP = 20
BISECT_TRIP = 2
BISECT_MAX_TRIPS = 134
LOG2E = math.log2(math.e)
SB_TAIL_CUTOFF = 120.0 * LOG2E
V_ROWS = 80
TM_MERGE = 512
TM_FFN = 512
N_GROUP = 11
N_SLAB = 4 * N_GROUP
N_PACK = (N_GROUP + N_MIXERS) * MIXER_WIDTH
VMEM_LIMIT = 56 * 1024 * 1024

G_SB_Q, G_SB_K, G_DF_Q, G_DF_K, G_DS_Q, G_DS_K, G_QI_A, G_QI_B, G_MB_Q, G_MB_K, G_KIDX = range(N_GROUP)
S_KIDX = 4 * G_KIDX
M_SB, M_DF, M_DS, M_MB = range(N_MIXERS)


def _layout():
    off = {}
    acc = 0
    for name, sz in (("q_sb", 256), ("k_sb", 256), ("v_sb", 256), ("q1", 128), ("q2", 128), ("k1", 128),
                     ("k2", 128), ("v_df", 256), ("q_ds", 256), ("k_ds", 256), ("v_ds", 256), ("qi", 512),
                     ("ki", 64), ("wi", 8), ("q_mb", 256), ("k_mb", 256), ("v_mb", 256), ("gate", 4096)):
        off[name] = acc
        acc += sz
    return off


_OFF = _layout()


def _pack_layout():
    off = _OFF
    cols, scale = [], []

    def add(start, n, s=1.0):
        cols.extend(range(start, start + n))
        scale.extend([s] * n)

    hd = HEAD_DIM ** -0.5
    hd2 = hd * LOG2E
    df2 = DIFF_QK_DIM ** -0.5 * LOG2E
    add(off["q_sb"], 256, hd2); add(off["k_sb"], 256)
    for h in range(N_HEADS):
        add(off["q1"] + h * 32, 32, df2); add(off["q2"] + h * 32, 32, df2)
    for h in range(N_HEADS):
        add(off["k1"] + h * 32, 32); add(off["k2"] + h * 32, 32)
    add(off["q_ds"], 256, hd2); add(off["k_ds"], 256)
    add(off["qi"], 512, HEAD_DIM ** -0.5)
    add(off["q_mb"], 256, hd2); add(off["k_mb"], 256)
    add(off["ki"], 64); add(off["wi"], IDX_HEADS, IDX_HEADS ** -0.5)
    cols.extend([-1] * 184); scale.extend([1.0] * 184)
    for name in ("v_sb", "v_df", "v_ds", "v_mb"):
        add(off[name], MIXER_WIDTH)
    assert len(cols) == N_PACK
    return np.asarray(cols, np.int32), np.asarray(scale, np.float32)


_PACK_SRC, _PACK_SCALE = _pack_layout()


def _dot(a, b):
    return jnp.dot(a, b, preferred_element_type=F32)


def _dot_nt(a, b):
    return lax.dot_general(a, b, (((1,), (1,)), ((), ())), preferred_element_type=F32)


def _rms(x, g):
    return x * lax.rsqrt(jnp.mean(x * x, axis=-1, keepdims=True) + NORM_EPS) * g


def _cparams(n_axes):
    return pltpu.CompilerParams(dimension_semantics=("arbitrary",) * n_axes, vmem_limit_bytes=VMEM_LIMIT)


def _const_spec(shape):
    nd = len(shape)
    return pl.BlockSpec(shape, lambda *_: (0,) * nd, pipeline_mode=pl.Buffered(1))


def _layer_spec(shape, layer):
    nd = len(shape)
    return pl.BlockSpec((pl.Squeezed(),) + tuple(shape), lambda *_: (layer,) + (0,) * nd, pipeline_mode=pl.Buffered(1))


def _proj_kernel(x_ref, g_ref, w_ref, cs_ref, zz_ref, vt_ref, wt_ref):
    h = _rms(x_ref[...], g_ref[...]).astype(BF16)

    def group(c):
        cols = slice(c * MIXER_WIDTH, (c + 1) * MIXER_WIDTH)
        return _dot(h, w_ref[:, cols]) * cs_ref[:, cols]

    for c in range(N_GROUP):
        r = group(c)
        for s in range(4):
            zz_ref[4 * c + s] = r[:, s * HEAD_DIM:(s + 1) * HEAD_DIM].astype(BF16)
        if c == G_KIDX:
            wt_ref[...] = jnp.transpose(r)[HEAD_DIM:HEAD_DIM + IDX_HEADS]
    for m in range(N_MIXERS):
        vt = jnp.transpose(group(N_GROUP + m)).astype(BF16)
        for hh in range(N_HEADS):
            vt_ref[m, 0, hh * V_ROWS:hh * V_ROWS + HEAD_DIM, :] = vt[hh * HEAD_DIM:(hh + 1) * HEAD_DIM]
            vt_ref[m, 0, hh * V_ROWS + HEAD_DIM:(hh + 1) * V_ROWS, :] = jnp.ones((V_ROWS - HEAD_DIM, TQ), BF16)


def _proj(x, g, w, cs, layer):
    t = x.shape[0]
    tm = TQ
    return pl.pallas_call(
        _proj_kernel,
        grid=(t // tm,),
        in_specs=[pl.BlockSpec((tm, D_MODEL), lambda i: (i, 0)),
                  _const_spec((1, D_MODEL)),
                  _layer_spec((D_MODEL, N_PACK), layer),
                  _const_spec((1, N_PACK))],
        out_specs=[pl.BlockSpec((N_SLAB, tm, HEAD_DIM), lambda i: (0, i, 0)),
                   pl.BlockSpec((N_MIXERS, 1, N_HEADS * V_ROWS, tm), lambda i: (0, i, 0, 0)),
                   pl.BlockSpec((IDX_HEADS, tm), lambda i: (0, i))],
        out_shape=[jax.ShapeDtypeStruct((N_SLAB, t, HEAD_DIM), BF16),
                   jax.ShapeDtypeStruct((N_MIXERS, t // tm, N_HEADS * V_ROWS, tm), BF16),
                   jax.ShapeDtypeStruct((IDX_HEADS, t), F32)],
        compiler_params=_cparams(1),
        name="proj",
    )(x, g, w, cs)


def _k_block(ref, h, j):
    return ref[h, pl.ds(pl.multiple_of(j * TK, TK), TK), :]


def _fold_keys(a, op):
    n = a.shape[0]
    while n > 8:
        n //= 2
        a = op(a[:n], a[n:2 * n])
    return a


def _vt_block(ref, h, j, rows=V_ROWS):
    return ref[0, j, h * V_ROWS:h * V_ROWS + rows, :]


def _softmax_block(s_list, vt_list, carry):
    ms = [jnp.maximum(c[0], jnp.max(_fold_keys(s, jnp.maximum), axis=0, keepdims=True))
          for s, c in zip(s_list, carry)]
    pvs = [_dot(vt, jnp.exp2(s - m).astype(BF16)) for vt, s, m in zip(vt_list, s_list, ms)]
    return tuple((m_new, jnp.exp2(m - m_new) * acc + pv) for (m, acc), m_new, pv in zip(carry, ms, pvs))


def _softmax_loop(n_blocks, logits, values, carry, groups=(2, 1)):
    assert groups[-1] == 1
    start = 0
    for g in groups:
        def body(t, carry, g=g, start=start):
            j = start + g * t
            s_all = [logits(j + u) for u in range(g)]
            for u in range(g):
                carry = _softmax_block(s_all[u], values(j + u), carry)
            return carry

        trips = (n_blocks - start) // g
        carry = lax.fori_loop(0, trips, body, carry)
        start = start + g * trips
    return carry


def _softmax_init():
    return (jnp.full((1, TQ), NEG, F32), jnp.zeros((V_ROWS, TQ), F32))


def _softmax_out(carry):
    _, acc = carry
    return acc[:HEAD_DIM] / acc[HEAD_DIM:HEAD_DIM + 1]


def _store_heads(o_ref, heads_t):
    o_ref[...] = jnp.transpose(jnp.concatenate(heads_t, axis=0)).astype(BF16)


BIAS_ROWS_SHAPE = (N_HEADS, 3, 1, 2 * TQ)
BIAS_TILES_SHAPE = (N_HEADS, 3, TK, TQ)


def _fill_bias_tiles(bw_ref, bt_ref):
    @pl.when((pl.program_id(0) == 0) & (pl.program_id(1) == 0))
    def _():
        for h in range(N_HEADS):
            for o in range(3):
                rows = jnp.broadcast_to(bw_ref[h, o], (TK, 2 * TQ))
                bt_ref[h, o] = pltpu.roll(rows, TQ + 1, 1, stride=1, stride_axis=0)[:, :TQ]


def _attn_specs(nq, gq, gk, mixer, s):
    return [pl.BlockSpec((4, TQ, HEAD_DIM), lambda b, i: (gq, b * nq + i, 0)),
            pl.BlockSpec((4, s, HEAD_DIM), lambda b, i: (gk, b, 0)),
            pl.BlockSpec((1, s // TK, N_HEADS * V_ROWS, TK), lambda b, i: (mixer, b, 0, 0))]


def _out_spec(nq):
    return pl.BlockSpec((TQ, N_HEADS * HEAD_DIM), lambda b, i: (b * nq + i, 0))


def _sb_kernel(q_ref, k_ref, vt_ref, tri_ref, mask_ref, o_ref):
    i = pl.program_id(1)
    tri = tri_ref[...]

    def block(j, carry, masked):
        heads = range(N_HEADS)
        zs = [_dot_nt(_k_block(k_ref, h, j), q_ref[h]) for h in heads]
        if masked:
            zs = [z + mask_ref[...] for z in zs]
        sps = [jnp.maximum(z, 0.0) + jnp.log2(1.0 + jnp.exp2(-jnp.abs(z))) for z in zs]
        his = [sp.astype(BF16) for sp in sps]
        los = [(sp - hi.astype(F32)).astype(BF16) for sp, hi in zip(sps, his)]
        cs = [_dot(tri, hi) + _dot(tri, lo) + c[0] for hi, lo, c in zip(his, los, carry)]
        avs = [_dot(_vt_block(vt_ref, h, j, HEAD_DIM), jnp.exp2(z - c).astype(BF16))
               for h, z, c in zip(heads, zs, cs)]
        return tuple((c[0:1, :], old[1] + av) for c, old, av in zip(cs, carry, avs))

    init = tuple((jnp.zeros((1, TQ), F32), jnp.zeros((HEAD_DIM, TQ), F32)) for _ in range(N_HEADS))
    carry = block(i, init, True)

    def weights_left(carry):
        tail = functools.reduce(jnp.minimum, [c[0] for c in carry])
        return (jnp.min(tail) < SB_TAIL_CUTOFF).astype(jnp.int32)

    def earlier_block(c):
        j, carry, _ = c
        carry = block(j, carry, False)
        return j - 1, carry, weights_left(carry)

    _, carry, _ = lax.while_loop(lambda c: (c[0] >= 0) & (c[2] > 0), earlier_block, (i - 1, carry, weights_left(carry)))
    _store_heads(o_ref, [c[1] for c in carry])


def _sb_attention(zz, vt, b, s, tri, mask):
    nq = s // TQ
    return pl.pallas_call(
        _sb_kernel,
        grid=(b, nq),
        in_specs=_attn_specs(nq, G_SB_Q, G_SB_K, M_SB, s) + [_const_spec((TK, TK)), _const_spec((TK, TQ))],
        out_specs=_out_spec(nq),
        out_shape=jax.ShapeDtypeStruct((b * s, N_HEADS * HEAD_DIM), BF16),
        compiler_params=_cparams(2),
        name="sb_attn",
    )(zz, zz, vt, tri, mask)


def _diff_kernel(q_ref, k_ref, vt_ref, bw_ref, lam_ref, cst_ref, g_ref, o_ref, bt_ref):
    i = pl.program_id(1)
    _fill_bias_tiles(bw_ref, bt_ref)
    lp = lam_ref[...]
    lam_init = cst_ref[:, 0:1]
    lam = (jnp.exp(jnp.sum(lp[0:1] * lp[1:2], axis=-1, keepdims=True))
           - jnp.exp(jnp.sum(lp[2:3] * lp[3:4], axis=-1, keepdims=True)) + lam_init)
    lane = lax.broadcasted_iota(jnp.int32, (TQ, HEAD_DIM), 1)
    qs = []
    for h in range(N_HEADS):
        q = q_ref[h]
        qs.append((jnp.where(lane < DIFF_QK_DIM, q, jnp.zeros_like(q)),
                   jnp.where(lane >= DIFF_QK_DIM, q, jnp.zeros_like(q))))

    def logits(j):
        s_list = []
        for h in range(N_HEADS):
            kj = _k_block(k_ref, h, j)
            bias = bt_ref[h, jnp.minimum(i - j, 2)]
            s_list += [_dot_nt(kj, qs[h][0]) + bias, _dot_nt(kj, qs[h][1]) + bias]
        return s_list

    def values(j):
        return [_vt_block(vt_ref, h, j) for h in range(N_HEADS) for _ in range(2)]

    carry = _softmax_loop(i + 1, logits, values, tuple(_softmax_init() for _ in range(2 * N_HEADS)), groups=(4, 2, 1))
    outs = []
    for h in range(N_HEADS):
        o = _softmax_out(carry[2 * h]) - lam * _softmax_out(carry[2 * h + 1])
        o = o * lax.rsqrt(jnp.mean(o * o, axis=0, keepdims=True) + NORM_EPS) * g_ref[...]
        outs.append(o * (1.0 - lam_init))
    _store_heads(o_ref, outs)


def _diff_attention(zz, vt, b, s, bw, lamp, cst, g):
    nq = s // TQ
    return pl.pallas_call(
        _diff_kernel,
        grid=(b, nq),
        in_specs=_attn_specs(nq, G_DF_Q, G_DF_K, M_DF, s) + [
            _const_spec(BIAS_ROWS_SHAPE), _const_spec((4, DIFF_QK_DIM)), _const_spec((1, 128)),
            _const_spec((HEAD_DIM, 1))],
        out_specs=_out_spec(nq),
        out_shape=jax.ShapeDtypeStruct((b * s, N_HEADS * HEAD_DIM), BF16),
        scratch_shapes=[pltpu.VMEM(BIAS_TILES_SHAPE, F32)],
        compiler_params=_cparams(2),
        name="diff_attn",
    )(zz, zz, vt, bw, lamp, cst, g)


def _dsa_kernel(q_ref, k_ref, vt_ref, qia_ref, qib_ref, ki_ref, wt_ref, bw_ref, tril_ref, o_ref, sc_ref, bt_ref, *,
                topk):
    i = pl.program_id(1)
    _fill_bias_tiles(bw_ref, bt_ref)
    nb = i + 1
    kf = float(topk)
    w = wt_ref[...]
    key = lax.broadcasted_iota(jnp.int32, (TK, TQ), 0)
    qry = lax.broadcasted_iota(jnp.int32, (TK, TQ), 1)

    def score(j):
        kij = _k_block(ki_ref, 0, j)
        sc = jnp.zeros((TK, TQ), F32)
        for hh in range(IDX_HEADS):
            qi = (qia_ref if hh < 4 else qib_ref)[hh % 4]
            sc = sc + w[hh:hh + 1, :] * jnp.maximum(_dot_nt(kij, qi), 0.0)
        return sc

    def extend(lo_src, hi_src, c):
        return (jnp.minimum(c[0], _fold_keys(lo_src, jnp.minimum)), jnp.maximum(c[1], _fold_keys(hi_src, jnp.maximum)))

    def earlier_block(j, c):
        sc = score(j)
        sc_ref[j] = sc
        return extend(sc, sc, c)

    def earlier_pair(t, c):
        sc_a, sc_b = score(2 * t), score(2 * t + 1)
        sc_ref[2 * t] = sc_a
        sc_ref[2 * t + 1] = sc_b
        return extend(sc_b, sc_b, extend(sc_a, sc_a, c))

    lo_part, hi_part = lax.fori_loop(0, i // 2, earlier_pair,
                                     (jnp.full((8, TQ), BIG, F32), jnp.full((8, TQ), -BIG, F32)))
    lo_part, hi_part = lax.fori_loop(2 * (i // 2), i, earlier_block, (lo_part, hi_part))
    sc = score(i)
    causal = key <= qry
    sc_ref[i] = jnp.where(causal, sc, NEG)
    lo_part, hi_part = extend(jnp.where(causal, sc, BIG), jnp.where(causal, sc, NEG), (lo_part, hi_part))

    def reduce_blocks(fn, init):
        def pair(t, c):
            return fn(sc_ref[2 * t + 1], 2 * t + 1, fn(sc_ref[2 * t], 2 * t, c))
        c = lax.fori_loop(0, nb // 2, pair, init)
        return lax.fori_loop(2 * (nb // 2), nb, lambda j, c: fn(sc_ref[j], j, c), c)

    def count_ge(t):
        part = reduce_blocks(lambda x, j, c: c + _fold_keys(jnp.where(x >= t, 1.0, 0.0), jnp.add),
                             jnp.zeros((8, TQ), F32))
        return jnp.sum(part, axis=0, keepdims=True)

    def minmax_blocks(lo_of, hi_of):
        def f(x, j, c):
            return (jnp.minimum(c[0], _fold_keys(lo_of(x), jnp.minimum)),
                    jnp.maximum(c[1], _fold_keys(hi_of(x), jnp.maximum)))
        lo_part, hi_part = reduce_blocks(f, (jnp.full((8, TQ), BIG, F32), jnp.full((8, TQ), -BIG, F32)))
        return jnp.min(lo_part, axis=0, keepdims=True), jnp.max(hi_part, axis=0, keepdims=True)

    n_valid = i * TQ + lax.broadcasted_iota(jnp.int32, (1, TQ), 1) + 1
    take_all = n_valid <= topk
    lo = jnp.min(lo_part, axis=0, keepdims=True)
    hi = jnp.max(hi_part, axis=0, keepdims=True)
    c_max = count_ge(hi)
    at_max = c_max >= kf
    state = (jnp.where(at_max, hi, lo), hi, jnp.where(at_max, c_max, n_valid.astype(F32)), c_max)

    def bisect(_, state):
        lo, hi, c_lo, c_hi = state
        mid = 0.5 * lo + 0.5 * hi
        c = count_ge(mid)
        ge = c >= kf
        return jnp.where(ge, mid, lo), jnp.where(ge, hi, mid), jnp.where(ge, c, c_lo), jnp.where(ge, c_hi, c)

    def unsettled(state):
        lo, hi, c_lo, _ = state
        open_q = jnp.where(take_all, 0.0, jnp.where(c_lo != kf, jnp.where(lo < hi, 1.0, 0.0), 0.0))

        def band_spread():
            b_min, b_max = minmax_blocks(lambda x: jnp.where(x >= lo, jnp.where(x < hi, x, BIG), BIG),
                                         lambda x: jnp.where(x >= lo, jnp.where(x < hi, x, -BIG), -BIG))
            return (jnp.max(jnp.where(b_max != b_min, open_q, 0.0)) > 0.0).astype(jnp.int32)

        return lax.cond(jnp.max(open_q) > 0.0, band_spread, lambda: jnp.int32(0))

    state = lax.fori_loop(0, jnp.where((i + 1) * TQ <= topk, 0, BISECT_WARMUP), bisect, state)

    def trip(c):
        n, state, _ = c
        state = lax.fori_loop(0, BISECT_TRIP, bisect, state)
        return n + 1, state, unsettled(state)

    _, state, _ = lax.while_loop(lambda c: (c[2] > 0) & (c[0] < BISECT_MAX_TRIPS), trip,
                                 (jnp.int32(0), state, unsettled(state)))
    lo, hi, c_lo, c_hi = state
    hi_ok = lo < hi
    c_above = jnp.where(hi_ok, c_hi, 0.0)
    hi_sel = jnp.where(hi_ok, hi, BIG)
    need = jnp.where(take_all, BIG, kf - c_above)
    lo_sel = jnp.where(take_all, HALF_NEG, lo)

    tied = jnp.max(jnp.where(take_all, 0.0, c_lo - kf)) > 0.0

    @pl.when(tied)
    def _():
        tril = tril_ref[...]

        def band_of(j):
            x = sc_ref[j]
            return x, jnp.where(x >= lo_sel, jnp.where(x < hi_sel, 1.0, 0.0), 0.0)

        def write(j, x, band, rank):
            sc_ref[j] = jnp.where(x >= hi_sel, 0.0,
                                  jnp.where(band * rank > 0.0, jnp.where(rank <= need, 0.0, NEG), NEG))
            return rank[TK - 1:TK, :]

        def write_mask(j, taken):
            x, band = band_of(j)
            return write(j, x, band, _dot(tril, band.astype(BF16)) + taken)

        def write_mask_pair(t, taken):
            (x_a, band_a), (x_b, band_b) = band_of(2 * t), band_of(2 * t + 1)
            in_a, in_b = _dot(tril, band_a.astype(BF16)), _dot(tril, band_b.astype(BF16))
            taken = write(2 * t, x_a, band_a, in_a + taken)
            return write(2 * t + 1, x_b, band_b, in_b + taken)

        taken = lax.fori_loop(0, nb // 2, write_mask_pair, jnp.zeros((1, TQ), F32))
        lax.fori_loop(2 * (nb // 2), nb, write_mask, taken)

    @pl.when(jnp.logical_not(tied))
    def _():
        def write_mask(j, _):
            sc_ref[j] = jnp.where(sc_ref[j] >= lo_sel, 0.0, NEG)
            return 0

        lax.fori_loop(0, nb, write_mask, 0)

    def logits(j):
        return [_dot_nt(_k_block(k_ref, h, j), q_ref[h]) + bt_ref[h, jnp.minimum(i - j, 2)] + sc_ref[j]
                for h in range(N_HEADS)]

    carry = _softmax_loop(nb, logits, lambda j: [_vt_block(vt_ref, h, j) for h in range(N_HEADS)],
                          tuple(_softmax_init() for _ in range(N_HEADS)), groups=(4, 2, 1))
    _store_heads(o_ref, [_softmax_out(c) for c in carry])


def _dsa_attention(zz, vt, wt, b, s, bw, tril):
    nq = s // TQ
    topk = min(DSA_TOPK_MAX, s // 4)
    return pl.pallas_call(
        functools.partial(_dsa_kernel, topk=topk),
        grid=(b, nq),
        in_specs=_attn_specs(nq, G_DS_Q, G_DS_K, M_DS, s) + [
            pl.BlockSpec((4, TQ, HEAD_DIM), lambda b_, i: (G_QI_A, b_ * nq + i, 0)),
            pl.BlockSpec((4, TQ, HEAD_DIM), lambda b_, i: (G_QI_B, b_ * nq + i, 0)),
            pl.BlockSpec((1, s, HEAD_DIM), lambda b_, i: (S_KIDX, b_, 0)),
            pl.BlockSpec((IDX_HEADS, TQ), lambda b_, i: (0, b_ * nq + i)),
            _const_spec(BIAS_ROWS_SHAPE), _const_spec((TK, TK))],
        out_specs=_out_spec(nq),
        out_shape=jax.ShapeDtypeStruct((b * s, N_HEADS * HEAD_DIM), BF16),
        scratch_shapes=[pltpu.VMEM((nq, TK, TQ), F32), pltpu.VMEM(BIAS_TILES_SHAPE, F32)],
        compiler_params=_cparams(2),
        name="dsa_attn",
    )(zz, zz, vt, zz, zz, zz, wt, bw, tril)


def _moba_kernel(q_ref, k_ref, vt_ref, bw_ref, o_ref, km_ref, bt_ref, *, nblk, topb):
    i = pl.program_id(1)
    _fill_bias_tiles(bw_ref, bt_ref)
    nrow = km_ref.shape[1]

    @pl.when(i == 0)
    def _():
        km_ref[...] = jnp.zeros_like(km_ref)
        for h in range(N_HEADS):
            for n in range(nblk):
                kb = k_ref[h, n * MOBA_BLOCK:(n + 1) * MOBA_BLOCK, :].astype(F32)
                km_ref[h, n:n + 1, :] = jnp.mean(kb, axis=0, keepdims=True)

    blk = lax.broadcasted_iota(jnp.int32, (nrow, TQ), 0)
    past = blk < i
    head_bits = []
    for h in range(N_HEADS):
        gate = _dot_nt(km_ref[h].astype(BF16), q_ref[h])
        bits = jnp.zeros((1, TQ), F32)
        for n in range(nblk):
            gn = gate[n:n + 1, :]
            beats = jnp.where(past, jnp.where(gate > gn, 1.0, jnp.where(gate == gn, jnp.where(blk < n, 1.0, 0.0), 0.0)), 0.0)
            rank = jnp.sum(beats, axis=0, keepdims=True)
            bits = bits + jnp.where(rank < float(topb), jnp.where(n < i, float(2 ** n), 0.0), 0.0)
        head_bits.append(bits.astype(jnp.int32) | lax.shift_left(jnp.int32(1), i))

    def values(n):
        return [_vt_block(vt_ref, h, n) for h in range(N_HEADS)]

    def logits(n):
        s_list = []
        for h in range(N_HEADS):
            picked = (lax.shift_right_logical(head_bits[h], jnp.full_like(head_bits[h], n)) & 1) == 1
            s_list.append(_dot_nt(_k_block(k_ref, h, n), q_ref[h]) + bt_ref[h, jnp.minimum(i - n, 2)]
                          + jnp.where(picked, 0.0, NEG))
        return s_list

    carry = _softmax_loop(i + 1, logits, values, tuple(_softmax_init() for _ in range(N_HEADS)), groups=(4, 2, 1))
    _store_heads(o_ref, [_softmax_out(c) for c in carry])


def _moba_attention(zz, vt, b, s, bw):
    nq = s // TQ
    nblk = s // MOBA_BLOCK
    topb = min(MOBA_TOPK, nblk - 1)
    return pl.pallas_call(
        functools.partial(_moba_kernel, nblk=nblk, topb=topb),
        grid=(b, nq),
        in_specs=_attn_specs(nq, G_MB_Q, G_MB_K, M_MB, s) + [_const_spec(BIAS_ROWS_SHAPE)],
        out_specs=_out_spec(nq),
        out_shape=jax.ShapeDtypeStruct((b * s, N_HEADS * HEAD_DIM), BF16),
        scratch_shapes=[pltpu.VMEM((N_HEADS, max(8, nblk), HEAD_DIM), F32), pltpu.VMEM(BIAS_TILES_SHAPE, F32)],
        compiler_params=_cparams(2),
        name="moba_attn",
    )(zz, zz, vt, bw)


def _merge_kernel(x_ref, osb_ref, odf_ref, ods_ref, omb_ref, gpre_ref, wg_ref, wbr_ref, wout_ref, gpost_ref, o_ref):
    x = x_ref[...]
    h = _rms(x, gpre_ref[...]).astype(BF16)
    y = jnp.zeros((x.shape[0], D_MODEL), F32)
    for r, o_r in enumerate((osb_ref, odf_ref, ods_ref, omb_ref)):
        gate = jax.nn.sigmoid(_dot(h, wg_ref[:, r * D_MODEL:(r + 1) * D_MODEL]))
        y = y + gate * _dot(o_r[...], wbr_ref[r])
    o_ref[...] = x + _rms(_dot(y.astype(BF16), wout_ref[...]), gpost_ref[...])


def _merge(x, o_sb, o_df, o_ds, o_mb, g_pre, w_gate, w_br, w_out, g_post, layer):
    t = x.shape[0]
    tm = TM_MERGE
    tok = lambda width: pl.BlockSpec((tm, width), lambda i: (i, 0))
    return pl.pallas_call(
        _merge_kernel,
        grid=(t // tm,),
        in_specs=[tok(D_MODEL)] + [tok(MIXER_WIDTH)] * N_MIXERS + [
            _const_spec((1, D_MODEL)), _layer_spec((D_MODEL, N_MIXERS * D_MODEL), layer),
            _layer_spec((N_MIXERS, MIXER_WIDTH, D_MODEL), layer), _layer_spec((D_MODEL, D_MODEL), layer),
            _const_spec((1, D_MODEL))],
        out_specs=tok(D_MODEL),
        out_shape=jax.ShapeDtypeStruct((t, D_MODEL), F32),
        compiler_params=_cparams(1),
        name="merge",
    )(x, o_sb, o_df, o_ds, o_mb, g_pre, w_gate, w_br, w_out, g_post)


def _ffn_kernel(x_ref, gpre_ref, win_ref, wout_ref, gpost_ref, o_ref):
    x = x_ref[...]
    h = _rms(x, gpre_ref[...]).astype(BF16)
    gate = _dot(h, win_ref[:, 0:D_FF])
    up = _dot(h, win_ref[:, D_FF:2 * D_FF])
    act = (gate * jax.nn.sigmoid(gate) * up).astype(BF16)
    o_ref[...] = x + _rms(_dot(act, wout_ref[...]), gpost_ref[...])


def _ffn(x, g_pre, w_in, w_out, g_post, layer):
    t = x.shape[0]
    tm = TM_FFN
    return pl.pallas_call(
        _ffn_kernel,
        grid=(t // tm,),
        in_specs=[pl.BlockSpec((tm, D_MODEL), lambda i: (i, 0)), _const_spec((1, D_MODEL)),
                  _layer_spec((D_MODEL, 2 * D_FF), layer), _layer_spec((D_FF, D_MODEL), layer),
                  _const_spec((1, D_MODEL))],
        out_specs=pl.BlockSpec((tm, D_MODEL), lambda i: (i, 0)),
        out_shape=jax.ShapeDtypeStruct((t, D_MODEL), F32),
        compiler_params=_cparams(1),
        name="ffn",
    )(x, g_pre, w_in, w_out, g_post)


def _t5_bucket(dist):
    max_exact = N_BUCKETS // 2
    d = jnp.maximum(dist, 0)
    log_ratio = jnp.log(jnp.maximum(d, 1).astype(F32) / max_exact) / math.log(MAX_DISTANCE / max_exact)
    large = jnp.minimum(max_exact + (log_ratio * (N_BUCKETS - max_exact)).astype(jnp.int32), N_BUCKETS - 1)
    return jnp.where(d < max_exact, d, large)


def _bias_rows(rel_bias):
    assert TQ == TK
    n = TK
    d = np.arange(-(n - 1), 3 * n + 1)
    by_dist = rel_bias.astype(F32).T[:, _t5_bucket(jnp.asarray(np.maximum(d, 0), jnp.int32))]
    by_dist = jnp.where(jnp.asarray(d >= 0)[None, :], by_dist * LOG2E, NEG)
    return jnp.stack([by_dist[:, o * n:o * n + 2 * n] for o in range(3)], axis=1)[:, :, None, :]


def _pack_weights(w_in):
    parts, start = [], 0
    for e in range(1, N_PACK + 1):
        if e == N_PACK or _PACK_SRC[e] != _PACK_SRC[e - 1] + (1 if _PACK_SRC[e - 1] >= 0 else 0):
            a = int(_PACK_SRC[start])
            n = e - start
            parts.append(w_in[:, :, a:a + n] if a >= 0 else jnp.zeros(w_in.shape[:2] + (n,), w_in.dtype))
            start = e
    return jnp.concatenate(parts, axis=2)


def kernel(x, w_in, w_br_sb, w_br_diff, w_br_dsa, w_br_moba, w_out, lambda_q1, lambda_k1, lambda_q2, lambda_k2,
           diff_subln_g, rel_bias, w_ffn_in, w_ffn_out, g_pre_mix, g_post_mix, g_pre_ffn, g_post_ffn):
    b, s, d = x.shape
    depth = w_in.shape[0]
    assert d == D_MODEL and s % TQ == 0 and s // MOBA_BLOCK >= 2
    t = b * s

    w_bf = w_in.astype(BF16)
    w_pack = _pack_weights(w_bf)
    w_gate = w_bf[:, :, _OFF["gate"]:]
    w_br = jnp.stack([w_br_sb, w_br_diff, w_br_dsa, w_br_moba], axis=1).astype(BF16)
    w_o = w_out.astype(BF16)
    w_f1 = w_ffn_in.astype(BF16)
    w_f2 = w_ffn_out.astype(BF16)
    cs = jnp.asarray(_PACK_SCALE)[None, :]

    bw = _bias_rows(rel_bias)
    bw_df, bw_ds, bw_mb = bw[0:4], bw[4:8], bw[8:12]
    key = np.arange(TK)[:, None]
    qry = np.arange(TQ)[None, :]
    tri = jnp.asarray(key <= np.arange(TK)[None, :], BF16)
    tril = jnp.asarray(key >= np.arange(TK)[None, :], BF16)
    sb_mask = jnp.asarray(np.where(key < qry, 0.0, NEG), F32)

    xf = x.reshape(t, d)
    for l in range(depth):
        lam_init = 0.8 - 0.6 * math.exp(-0.3 * l)
        lamp = jnp.stack([lambda_q1[l], lambda_k1[l], lambda_q2[l], lambda_k2[l]]).astype(F32)
        cst = jnp.full((1, 128), lam_init, F32)
        zz, vt, wt = _proj(xf, g_pre_mix[l][None, :], w_pack, cs, l)
        o_sb = _sb_attention(zz, vt, b, s, tri, sb_mask)
        o_df = _diff_attention(zz, vt, b, s, bw_df, lamp, cst, diff_subln_g[l][:, None])
        o_ds = _dsa_attention(zz, vt, wt, b, s, bw_ds, tril)
        o_mb = _moba_attention(zz, vt, b, s, bw_mb)
        xf = _merge(xf, o_sb, o_df, o_ds, o_mb, g_pre_mix[l][None, :], w_gate, w_br, w_o, g_post_mix[l][None, :], l)
        xf = _ffn(xf, g_pre_ffn[l][None, :], w_f1, w_f2, g_post_ffn[l][None, :], l)
    return xf.reshape(b, s, d)
```

```python
import functools
import math

import numpy as np
import jax
import jax.numpy as jnp
from jax import lax
from jax.experimental import pallas as pl
from jax.experimental.pallas import tpu as pltpu

F32 = jnp.float32
BF16 = jnp.bfloat16

D_MODEL = 1024
HEAD_DIM = 64
N_HEADS = 4
N_MIXERS = 4
MIXER_WIDTH = N_HEADS * HEAD_DIM
DIFF_QK_DIM = 32
IDX_HEADS = 8
DSA_TOPK_MAX = 256
MOBA_BLOCK = 256
MOBA_TOPK = 3
N_BUCKETS = 32
MAX_DISTANCE = 128
D_FF = 2816
NORM_EPS = 1e-6

TQ = 256
TK = 256
NEG = -1e30
HALF_NEG = -0.5e30
BIG = 3e38
BISECT_WARMUP = 20
BISECT_TRIP = 2
BISECT_MAX_TRIPS = 134
LOG2E = math.log2(math.e)
SB_TAIL_CUTOFF = 120.0 * LOG2E
V_ROWS = 80
TM_MERGE = 512
TM_FFN = 512
N_GROUP = 11
N_SLAB = 4 * N_GROUP
N_PACK = (N_GROUP + N_MIXERS) * MIXER_WIDTH
VMEM_LIMIT = 56 * 1024 * 1024

G_SB_Q, G_SB_K, G_DF_Q, G_DF_K, G_DS_Q, G_DS_K, G_QI_A, G_QI_B, G_MB_Q, G_MB_K, G_KIDX = range(N_GROUP)
S_KIDX = 4 * G_KIDX
M_SB, M_DF, M_DS, M_MB = range(N_MIXERS)


def _layout():
    off = {}
    acc = 0
    for name, sz in (("q_sb", 256), ("k_sb", 256), ("v_sb", 256), ("q1", 128), ("q2", 128), ("k1", 128),
                     ("k2", 128), ("v_df", 256), ("q_ds", 256), ("k_ds", 256), ("v_ds", 256), ("qi", 512),
                     ("ki", 64), ("wi", 8), ("q_mb", 256), ("k_mb", 256), ("v_mb", 256), ("gate", 4096)):
        off[name] = acc
        acc += sz
    return off


_OFF = _layout()


def _pack_layout():
    off = _OFF
    cols, scale = [], []

    def add(start, n, s=1.0):
        cols.extend(range(start, start + n))
        scale.extend([s] * n)

    hd = HEAD_DIM ** -0.5
    hd2 = hd * LOG2E
    df2 = DIFF_QK_DIM ** -0.5 * LOG2E
    add(off["q_sb"], 256, hd2); add(off["k_sb"], 256)
    for h in range(N_HEADS):
        add(off["q1"] + h * 32, 32, df2); add(off["q2"] + h * 32, 32, df2)
    for h in range(N_HEADS):
        add(off["k1"] + h * 32, 32); add(off["k2"] + h * 32, 32)
    add(off["q_ds"], 256, hd2); add(off["k_ds"], 256)
    add(off["qi"], 512, HEAD_DIM ** -0.5)
    add(off["q_mb"], 256, hd2); add(off["k_mb"], 256)
    add(off["ki"], 64); add(off["wi"], IDX_HEADS, IDX_HEADS ** -0.5)
    cols.extend([-1] * 184); scale.extend([1.0] * 184)
    for name in ("v_sb", "v_df", "v_ds", "v_mb"):
        add(off[name], MIXER_WIDTH)
    assert len(cols) == N_PACK
    return np.asarray(cols, np.int32), np.asarray(scale, np.float32)


_PACK_SRC, _PACK_SCALE = _pack_layout()


def _dot(a, b):
    return jnp.dot(a, b, preferred_element_type=F32)


def _dot_nt(a, b):
    return lax.dot_general(a, b, (((1,), (1,)), ((), ())), preferred_element_type=F32)


def _rms(x, g):
    return x * lax.rsqrt(jnp.mean(x * x, axis=-1, keepdims=True) + NORM_EPS) * g


def _cparams(n_axes):
    return pltpu.CompilerParams(dimension_semantics=("arbitrary",) * n_axes, vmem_limit_bytes=VMEM_LIMIT)


def _const_spec(shape):
    nd = len(shape)
    return pl.BlockSpec(shape, lambda *_: (0,) * nd, pipeline_mode=pl.Buffered(1))


def _layer_spec(shape, layer):
    nd = len(shape)
    return pl.BlockSpec((pl.Squeezed(),) + tuple(shape), lambda *_: (layer,) + (0,) * nd, pipeline_mode=pl.Buffered(1))


def _proj_kernel(x_ref, g_ref, w_ref, cs_ref, zz_ref, vt_ref, wt_ref):
    h = _rms(x_ref[...], g_ref[...]).astype(BF16)

    def group(c):
        cols = slice(c * MIXER_WIDTH, (c + 1) * MIXER_WIDTH)
        return _dot(h, w_ref[:, cols]) * cs_ref[:, cols]

    for c in range(N_GROUP):
        r = group(c)
        for s in range(4):
            zz_ref[4 * c + s] = r[:, s * HEAD_DIM:(s + 1) * HEAD_DIM].astype(BF16)
        if c == G_KIDX:
            wt_ref[...] = jnp.transpose(r)[HEAD_DIM:HEAD_DIM + IDX_HEADS]
    for m in range(N_MIXERS):
        vt = jnp.transpose(group(N_GROUP + m)).astype(BF16)
        for hh in range(N_HEADS):
            vt_ref[m, 0, hh * V_ROWS:hh * V_ROWS + HEAD_DIM, :] = vt[hh * HEAD_DIM:(hh + 1) * HEAD_DIM]
            vt_ref[m, 0, hh * V_ROWS + HEAD_DIM:(hh + 1) * V_ROWS, :] = jnp.ones((V_ROWS - HEAD_DIM, TQ), BF16)


def _proj(x, g, w, cs, layer):
    t = x.shape[0]
    tm = TQ
    return pl.pallas_call(
        _proj_kernel,
        grid=(t // tm,),
        in_specs=[pl.BlockSpec((tm, D_MODEL), lambda i: (i, 0)),
                  _const_spec((1, D_MODEL)),
                  _layer_spec((D_MODEL, N_PACK), layer),
                  _const_spec((1, N_PACK))],
        out_specs=[pl.BlockSpec((N_SLAB, tm, HEAD_DIM), lambda i: (0, i, 0)),
                   pl.BlockSpec((N_MIXERS, 1, N_HEADS * V_ROWS, tm), lambda i: (0, i, 0, 0)),
                   pl.BlockSpec((IDX_HEADS, tm), lambda i: (0, i))],
        out_shape=[jax.ShapeDtypeStruct((N_SLAB, t, HEAD_DIM), BF16),
                   jax.ShapeDtypeStruct((N_MIXERS, t // tm, N_HEADS * V_ROWS, tm), BF16),
                   jax.ShapeDtypeStruct((IDX_HEADS, t), F32)],
        compiler_params=_cparams(1),
        name="proj",
    )(x, g, w, cs)


def _k_block(ref, h, j):
    return ref[h, pl.ds(pl.multiple_of(j * TK, TK), TK), :]


def _fold_keys(a, op):
    n = a.shape[0]
    while n > 8:
        n //= 2
        a = op(a[:n], a[n:2 * n])
    return a


def _vt_block(ref, h, j, rows=V_ROWS):
    return ref[0, j, h * V_ROWS:h * V_ROWS + rows, :]


def _softmax_block(s_list, vt_list, carry):
    ms = [jnp.maximum(c[0], jnp.max(_fold_keys(s, jnp.maximum), axis=0, keepdims=True))
          for s, c in zip(s_list, carry)]
    pvs = [_dot(vt, jnp.exp2(s - m).astype(BF16)) for vt, s, m in zip(vt_list, s_list, ms)]
    return tuple((m_new, jnp.exp2(m - m_new) * acc + pv) for (m, acc), m_new, pv in zip(carry, ms, pvs))


def _softmax_loop(n_blocks, logits, values, carry, groups=(2, 1)):
    assert groups[-1] == 1
    start = 0
    for g in groups:
        def body(t, carry, g=g, start=start):
            j = start + g * t
            s_all = [logits(j + u) for u in range(g)]
            for u in range(g):
                carry = _softmax_block(s_all[u], values(j + u), carry)
            return carry

        trips = (n_blocks - start) // g
        carry = lax.fori_loop(0, trips, body, carry)
        start = start + g * trips
    return carry


def _softmax_init():
    return (jnp.full((1, TQ), NEG, F32), jnp.zeros((V_ROWS, TQ), F32))


def _softmax_out(carry):
    _, acc = carry
    return acc[:HEAD_DIM] / acc[HEAD_DIM:HEAD_DIM + 1]


def _store_heads(o_ref, heads_t):
    o_ref[...] = jnp.transpose(jnp.concatenate(heads_t, axis=0)).astype(BF16)


BIAS_ROWS_SHAPE = (N_HEADS, 3, 1, 2 * TQ)
BIAS_TILES_SHAPE = (N_HEADS, 3, TK, TQ)


def _fill_bias_tiles(bw_ref, bt_ref):
    @pl.when((pl.program_id(0) == 0) & (pl.program_id(1) == 0))
    def _():
        for h in range(N_HEADS):
            for o in range(3):
                rows = jnp.broadcast_to(bw_ref[h, o], (TK, 2 * TQ))
                bt_ref[h, o] = pltpu.roll(rows, TQ + 1, 1, stride=1, stride_axis=0)[:, :TQ]


def _attn_specs(nq, gq, gk, mixer, s):
    return [pl.BlockSpec((4, TQ, HEAD_DIM), lambda b, i: (gq, b * nq + i, 0)),
            pl.BlockSpec((4, s, HEAD_DIM), lambda b, i: (gk, b, 0)),
            pl.BlockSpec((1, s // TK, N_HEADS * V_ROWS, TK), lambda b, i: (mixer, b, 0, 0))]


def _out_spec(nq):
    return pl.BlockSpec((TQ, N_HEADS * HEAD_DIM), lambda b, i: (b * nq + i, 0))


def _sb_kernel(q_ref, k_ref, vt_ref, tri_ref, mask_ref, o_ref):
    i = pl.program_id(1)
    tri = tri_ref[...]

    def block(j, carry, masked):
        heads = range(N_HEADS)
        zs = [_dot_nt(_k_block(k_ref, h, j), q_ref[h]) for h in heads]
        if masked:
            zs = [z + mask_ref[...] for z in zs]
        sps = [jnp.maximum(z, 0.0) + jnp.log2(1.0 + jnp.exp2(-jnp.abs(z))) for z in zs]
        his = [sp.astype(BF16) for sp in sps]
        los = [(sp - hi.astype(F32)).astype(BF16) for sp, hi in zip(sps, his)]
        cs = [_dot(tri, hi) + _dot(tri, lo) + c[0] for hi, lo, c in zip(his, los, carry)]
        avs = [_dot(_vt_block(vt_ref, h, j, HEAD_DIM), jnp.exp2(z - c).astype(BF16))
               for h, z, c in zip(heads, zs, cs)]
        return tuple((c[0:1, :], old[1] + av) for c, old, av in zip(cs, carry, avs))

    init = tuple((jnp.zeros((1, TQ), F32), jnp.zeros((HEAD_DIM, TQ), F32)) for _ in range(N_HEADS))
    carry = block(i, init, True)

    def weights_left(carry):
        tail = functools.reduce(jnp.minimum, [c[0] for c in carry])
        return (jnp.min(tail) < SB_TAIL_CUTOFF).astype(jnp.int32)

    def earlier_block(c):
        j, carry, _ = c
        carry = block(j, carry, False)
        return j - 1, carry, weights_left(carry)

    _, carry, _ = lax.while_loop(lambda c: (c[0] >= 0) & (c[2] > 0), earlier_block, (i - 1, carry, weights_left(carry)))
    _store_heads(o_ref, [c[1] for c in carry])


def _sb_attention(zz, vt, b, s, tri, mask):
    nq = s // TQ
    return pl.pallas_call(
        _sb_kernel,
        grid=(b, nq),
        in_specs=_attn_specs(nq, G_SB_Q, G_SB_K, M_SB, s) + [_const_spec((TK, TK)), _const_spec((TK, TQ))],
        out_specs=_out_spec(nq),
        out_shape=jax.ShapeDtypeStruct((b * s, N_HEADS * HEAD_DIM), BF16),
        compiler_params=_cparams(2),
        name="sb_attn",
    )(zz, zz, vt, tri, mask)


def _diff_kernel(q_ref, k_ref, vt_ref, bw_ref, lam_ref, cst_ref, g_ref, o_ref, bt_ref):
    i = pl.program_id(1)
    _fill_bias_tiles(bw_ref, bt_ref)
    lp = lam_ref[...]
    lam_init = cst_ref[:, 0:1]
    lam = (jnp.exp(jnp.sum(lp[0:1] * lp[1:2], axis=-1, keepdims=True))
           - jnp.exp(jnp.sum(lp[2:3] * lp[3:4], axis=-1, keepdims=True)) + lam_init)
    lane = lax.broadcasted_iota(jnp.int32, (TQ, HEAD_DIM), 1)
    qs = []
    for h in range(N_HEADS):
        q = q_ref[h]
        qs.append((jnp.where(lane < DIFF_QK_DIM, q, jnp.zeros_like(q)),
                   jnp.where(lane >= DIFF_QK_DIM, q, jnp.zeros_like(q))))

    def logits(j):
        s_list = []
        for h in range(N_HEADS):
            kj = _k_block(k_ref, h, j)
            bias = bt_ref[h, jnp.minimum(i - j, 2)]
            s_list += [_dot_nt(kj, qs[h][0]) + bias, _dot_nt(kj, qs[h][1]) + bias]
        return s_list

    def values(j):
        return [_vt_block(vt_ref, h, j) for h in range(N_HEADS) for _ in range(2)]

    carry = _softmax_loop(i + 1, logits, values, tuple(_softmax_init() for _ in range(2 * N_HEADS)), groups=(4, 2, 1))
    outs = []
    for h in range(N_HEADS):
        o = _softmax_out(carry[2 * h]) - lam * _softmax_out(carry[2 * h + 1])
        o = o * lax.rsqrt(jnp.mean(o * o, axis=0, keepdims=True) + NORM_EPS) * g_ref[...]
        outs.append(o * (1.0 - lam_init))
    _store_heads(o_ref, outs)


def _diff_attention(zz, vt, b, s, bw, lamp, cst, g):
    nq = s // TQ
    return pl.pallas_call(
        _diff_kernel,
        grid=(b, nq),
        in_specs=_attn_specs(nq, G_DF_Q, G_DF_K, M_DF, s) + [
            _const_spec(BIAS_ROWS_SHAPE), _const_spec((4, DIFF_QK_DIM)), _const_spec((1, 128)),
            _const_spec((HEAD_DIM, 1))],
        out_specs=_out_spec(nq),
        out_shape=jax.ShapeDtypeStruct((b * s, N_HEADS * HEAD_DIM), BF16),
        scratch_shapes=[pltpu.VMEM(BIAS_TILES_SHAPE, F32)],
        compiler_params=_cparams(2),
        name="diff_attn",
    )(zz, zz, vt, bw, lamp, cst, g)


def _dsa_kernel(q_ref, k_ref, vt_ref, qia_ref, qib_ref, ki_ref, wt_ref, bw_ref, tril_ref, o_ref, sc_ref, bt_ref, *,
                topk):
    i = pl.program_id(1)
    _fill_bias_tiles(bw_ref, bt_ref)
    nb = i + 1
    kf = float(topk)
    w = wt_ref[...]
    key = lax.broadcasted_iota(jnp.int32, (TK, TQ), 0)
    qry = lax.broadcasted_iota(jnp.int32, (TK, TQ), 1)

    def score(j):
        kij = _k_block(ki_ref, 0, j)
        sc = jnp.zeros((TK, TQ), F32)
        for hh in range(IDX_HEADS):
            qi = (qia_ref if hh < 4 else qib_ref)[hh % 4]
            sc = sc + w[hh:hh + 1, :] * jnp.maximum(_dot_nt(kij, qi), 0.0)
        return sc

    def extend(lo_src, hi_src, c):
        return (jnp.minimum(c[0], _fold_keys(lo_src, jnp.minimum)), jnp.maximum(c[1], _fold_keys(hi_src, jnp.maximum)))

    def earlier_block(j, c):
        sc = score(j)
        sc_ref[j] = sc
        return extend(sc, sc, c)

    def earlier_pair(t, c):
        sc_a, sc_b = score(2 * t), score(2 * t + 1)
        sc_ref[2 * t] = sc_a
        sc_ref[2 * t + 1] = sc_b
        return extend(sc_b, sc_b, extend(sc_a, sc_a, c))

    lo_part, hi_part = lax.fori_loop(0, i // 2, earlier_pair,
                                     (jnp.full((8, TQ), BIG, F32), jnp.full((8, TQ), -BIG, F32)))
    lo_part, hi_part = lax.fori_loop(2 * (i // 2), i, earlier_block, (lo_part, hi_part))
    sc = score(i)
    causal = key <= qry
    sc_ref[i] = jnp.where(causal, sc, NEG)
    lo_part, hi_part = extend(jnp.where(causal, sc, BIG), jnp.where(causal, sc, NEG), (lo_part, hi_part))

    def reduce_blocks(fn, init):
        def pair(t, c):
            return fn(sc_ref[2 * t + 1], 2 * t + 1, fn(sc_ref[2 * t], 2 * t, c))
        c = lax.fori_loop(0, nb // 2, pair, init)
        return lax.fori_loop(2 * (nb // 2), nb, lambda j, c: fn(sc_ref[j], j, c), c)

    def count_ge(t):
        part = reduce_blocks(lambda x, j, c: c + _fold_keys(jnp.where(x >= t, 1.0, 0.0), jnp.add),
                             jnp.zeros((8, TQ), F32))
        return jnp.sum(part, axis=0, keepdims=True)

    def minmax_blocks(lo_of, hi_of):
        def f(x, j, c):
            return (jnp.minimum(c[0], _fold_keys(lo_of(x), jnp.minimum)),
                    jnp.maximum(c[1], _fold_keys(hi_of(x), jnp.maximum)))
        lo_part, hi_part = reduce_blocks(f, (jnp.full((8, TQ), BIG, F32), jnp.full((8, TQ), -BIG, F32)))
        return jnp.min(lo_part, axis=0, keepdims=True), jnp.max(hi_part, axis=0, keepdims=True)

    n_valid = i * TQ + lax.broadcasted_iota(jnp.int32, (1, TQ), 1) + 1
    take_all = n_valid <= topk
    lo = jnp.min(lo_part, axis=0, keepdims=True)
    hi = jnp.max(hi_part, axis=0, keepdims=True)
    c_max = count_ge(hi)
    at_max = c_max >= kf
    state = (jnp.where(at_max, hi, lo), hi, jnp.where(at_max, c_max, n_valid.astype(F32)), c_max)

    def bisect(_, state):
        lo, hi, c_lo, c_hi = state
        mid = 0.5 * lo + 0.5 * hi
        c = count_ge(mid)
        ge = c >= kf
        return jnp.where(ge, mid, lo), jnp.where(ge, hi, mid), jnp.where(ge, c, c_lo), jnp.where(ge, c_hi, c)

    def unsettled(state):
        lo, hi, c_lo, _ = state
        open_q = jnp.where(take_all, 0.0, jnp.where(c_lo != kf, jnp.where(lo < hi, 1.0, 0.0), 0.0))

        def band_spread():
            b_min, b_max = minmax_blocks(lambda x: jnp.where(x >= lo, jnp.where(x < hi, x, BIG), BIG),
                                         lambda x: jnp.where(x >= lo, jnp.where(x < hi, x, -BIG), -BIG))
            return (jnp.max(jnp.where(b_max != b_min, open_q, 0.0)) > 0.0).astype(jnp.int32)

        return lax.cond(jnp.max(open_q) > 0.0, band_spread, lambda: jnp.int32(0))

    state = lax.fori_loop(0, jnp.where((i + 1) * TQ <= topk, 0, BISECT_WARMUP), bisect, state)

    def trip(c):
        n, state, _ = c
        state = lax.fori_loop(0, BISECT_TRIP, bisect, state)
        return n + 1, state, unsettled(state)

    _, state, _ = lax.while_loop(lambda c: (c[2] > 0) & (c[0] < BISECT_MAX_TRIPS), trip,
                                 (jnp.int32(0), state, unsettled(state)))
    lo, hi, c_lo, c_hi = state
    hi_ok = lo < hi
    c_above = jnp.where(hi_ok, c_hi, 0.0)
    hi_sel = jnp.where(hi_ok, hi, BIG)
    need = jnp.where(take_all, BIG, kf - c_above)
    lo_sel = jnp.where(take_all, HALF_NEG, lo)

    tied = jnp.max(jnp.where(take_all, 0.0, c_lo - kf)) > 0.0

    @pl.when(tied)
    def _():
        tril = tril_ref[...]

        def band_of(j):
            x = sc_ref[j]
            return x, jnp.where(x >= lo_sel, jnp.where(x < hi_sel, 1.0, 0.0), 0.0)

        def write(j, x, band, rank):
            sc_ref[j] = jnp.where(x >= hi_sel, 0.0,
                                  jnp.where(band * rank > 0.0, jnp.where(rank <= need, 0.0, NEG), NEG))
            return rank[TK - 1:TK, :]

        def write_mask(j, taken):
            x, band = band_of(j)
            return write(j, x, band, _dot(tril, band.astype(BF16)) + taken)

        def write_mask_pair(t, taken):
            (x_a, band_a), (x_b, band_b) = band_of(2 * t), band_of(2 * t + 1)
            in_a, in_b = _dot(tril, band_a.astype(BF16)), _dot(tril, band_b.astype(BF16))
            taken = write(2 * t, x_a, band_a, in_a + taken)
            return write(2 * t + 1, x_b, band_b, in_b + taken)

        taken = lax.fori_loop(0, nb // 2, write_mask_pair, jnp.zeros((1, TQ), F32))
        lax.fori_loop(2 * (nb // 2), nb, write_mask, taken)

    @pl.when(jnp.logical_not(tied))
    def _():
        def write_mask(j, _):
            sc_ref[j] = jnp.where(sc_ref[j] >= lo_sel, 0.0, NEG)
            return 0

        lax.fori_loop(0, nb, write_mask, 0)

    def logits(j):
        return [_dot_nt(_k_block(k_ref, h, j), q_ref[h]) + bt_ref[h, jnp.minimum(i - j, 2)] + sc_ref[j]
                for h in range(N_HEADS)]

    carry = _softmax_loop(nb, logits, lambda j: [_vt_block(vt_ref, h, j) for h in range(N_HEADS)],
                          tuple(_softmax_init() for _ in range(N_HEADS)), groups=(4, 2, 1))
    _store_heads(o_ref, [_softmax_out(c) for c in carry])


def _dsa_attention(zz, vt, wt, b, s, bw, tril):
    nq = s // TQ
    topk = min(DSA_TOPK_MAX, s // 4)
    return pl.pallas_call(
        functools.partial(_dsa_kernel, topk=topk),
        grid=(b, nq),
        in_specs=_attn_specs(nq, G_DS_Q, G_DS_K, M_DS, s) + [
            pl.BlockSpec((4, TQ, HEAD_DIM), lambda b_, i: (G_QI_A, b_ * nq + i, 0)),
            pl.BlockSpec((4, TQ, HEAD_DIM), lambda b_, i: (G_QI_B, b_ * nq + i, 0)),
            pl.BlockSpec((1, s, HEAD_DIM), lambda b_, i: (S_KIDX, b_, 0)),
            pl.BlockSpec((IDX_HEADS, TQ), lambda b_, i: (0, b_ * nq + i)),
            _const_spec(BIAS_ROWS_SHAPE), _const_spec((TK, TK))],
        out_specs=_out_spec(nq),
        out_shape=jax.ShapeDtypeStruct((b * s, N_HEADS * HEAD_DIM), BF16),
        scratch_shapes=[pltpu.VMEM((nq, TK, TQ), F32), pltpu.VMEM(BIAS_TILES_SHAPE, F32)],
        compiler_params=_cparams(2),
        name="dsa_attn",
    )(zz, zz, vt, zz, zz, zz, wt, bw, tril)


def _moba_kernel(q_ref, k_ref, vt_ref, bw_ref, o_ref, km_ref, bt_ref, *, nblk, topb):
    i = pl.program_id(1)
    _fill_bias_tiles(bw_ref, bt_ref)
    nrow = km_ref.shape[1]

    @pl.when(i == 0)
    def _():
        km_ref[...] = jnp.zeros_like(km_ref)
        for h in range(N_HEADS):
            for n in range(nblk):
                kb = k_ref[h, n * MOBA_BLOCK:(n + 1) * MOBA_BLOCK, :].astype(F32)
                km_ref[h, n:n + 1, :] = jnp.mean(kb, axis=0, keepdims=True)

    blk = lax.broadcasted_iota(jnp.int32, (nrow, TQ), 0)
    past = blk < i
    head_bits = []
    for h in range(N_HEADS):
        gate = _dot_nt(km_ref[h].astype(BF16), q_ref[h])
        bits = jnp.zeros((1, TQ), F32)
        for n in range(nblk):
            gn = gate[n:n + 1, :]
            beats = jnp.where(past, jnp.where(gate > gn, 1.0, jnp.where(gate == gn, jnp.where(blk < n, 1.0, 0.0), 0.0)), 0.0)
            rank = jnp.sum(beats, axis=0, keepdims=True)
            bits = bits + jnp.where(rank < float(topb), jnp.where(n < i, float(2 ** n), 0.0), 0.0)
        head_bits.append(bits.astype(jnp.int32) | lax.shift_left(jnp.int32(1), i))

    def values(n):
        return [_vt_block(vt_ref, h, n) for h in range(N_HEADS)]

    def logits(n):
        s_list = []
        for h in range(N_HEADS):
            picked = (lax.shift_right_logical(head_bits[h], jnp.full_like(head_bits[h], n)) & 1) == 1
            s_list.append(_dot_nt(_k_block(k_ref, h, n), q_ref[h]) + bt_ref[h, jnp.minimum(i - n, 2)]
                          + jnp.where(picked, 0.0, NEG))
        return s_list

    carry = _softmax_loop(i + 1, logits, values, tuple(_softmax_init() for _ in range(N_HEADS)), groups=(4, 2, 1))
    _store_heads(o_ref, [_softmax_out(c) for c in carry])


def _moba_attention(zz, vt, b, s, bw):
    nq = s // TQ
    nblk = s // MOBA_BLOCK
    topb = min(MOBA_TOPK, nblk - 1)
    return pl.pallas_call(
        functools.partial(_moba_kernel, nblk=nblk, topb=topb),
        grid=(b, nq),
        in_specs=_attn_specs(nq, G_MB_Q, G_MB_K, M_MB, s) + [_const_spec(BIAS_ROWS_SHAPE)],
        out_specs=_out_spec(nq),
        out_shape=jax.ShapeDtypeStruct((b * s, N_HEADS * HEAD_DIM), BF16),
        scratch_shapes=[pltpu.VMEM((N_HEADS, max(8, nblk), HEAD_DIM), F32), pltpu.VMEM(BIAS_TILES_SHAPE, F32)],
        compiler_params=_cparams(2),
        name="moba_attn",
    )(zz, zz, vt, bw)


def _merge_kernel(x_ref, osb_ref, odf_ref, ods_ref, omb_ref, gpre_ref, wg_ref, wbr_ref, wout_ref, gpost_ref, o_ref):
    x = x_ref[...]
    h = _rms(x, gpre_ref[...]).astype(BF16)
    y = jnp.zeros((x.shape[0], D_MODEL), F32)
    for r, o_r in enumerate((osb_ref, odf_ref, ods_ref, omb_ref)):
        gate = jax.nn.sigmoid(_dot(h, wg_ref[:, r * D_MODEL:(r + 1) * D_MODEL]))
        y = y + gate * _dot(o_r[...], wbr_ref[r])
    o_ref[...] = x + _rms(_dot(y.astype(BF16), wout_ref[...]), gpost_ref[...])


def _merge(x, o_sb, o_df, o_ds, o_mb, g_pre, w_gate, w_br, w_out, g_post, layer):
    t = x.shape[0]
    tm = TM_MERGE
    tok = lambda width: pl.BlockSpec((tm, width), lambda i: (i, 0))
    return pl.pallas_call(
        _merge_kernel,
        grid=(t // tm,),
        in_specs=[tok(D_MODEL)] + [tok(MIXER_WIDTH)] * N_MIXERS + [
            _const_spec((1, D_MODEL)), _layer_spec((D_MODEL, N_MIXERS * D_MODEL), layer),
            _layer_spec((N_MIXERS, MIXER_WIDTH, D_MODEL), layer), _layer_spec((D_MODEL, D_MODEL), layer),
            _const_spec((1, D_MODEL))],
        out_specs=tok(D_MODEL),
        out_shape=jax.ShapeDtypeStruct((t, D_MODEL), F32),
        compiler_params=_cparams(1),
        name="merge",
    )(x, o_sb, o_df, o_ds, o_mb, g_pre, w_gate, w_br, w_out, g_post)


def _ffn_kernel(x_ref, gpre_ref, win_ref, wout_ref, gpost_ref, o_ref):
    x = x_ref[...]
    h = _rms(x, gpre_ref[...]).astype(BF16)
    gate = _dot(h, win_ref[:, 0:D_FF])
    up = _dot(h, win_ref[:, D_FF:2 * D_FF])
    act = (gate * jax.nn.sigmoid(gate) * up).astype(BF16)
    o_ref[...] = x + _rms(_dot(act, wout_ref[...]), gpost_ref[...])


def _ffn(x, g_pre, w_in, w_out, g_post, layer):
    t = x.shape[0]
    tm = TM_FFN
    return pl.pallas_call(
        _ffn_kernel,
        grid=(t // tm,),
        in_specs=[pl.BlockSpec((tm, D_MODEL), lambda i: (i, 0)), _const_spec((1, D_MODEL)),
                  _layer_spec((D_MODEL, 2 * D_FF), layer), _layer_spec((D_FF, D_MODEL), layer),
                  _const_spec((1, D_MODEL))],
        out_specs=pl.BlockSpec((tm, D_MODEL), lambda i: (i, 0)),
        out_shape=jax.ShapeDtypeStruct((t, D_MODEL), F32),
        compiler_params=_cparams(1),
        name="ffn",
    )(x, g_pre, w_in, w_out, g_post)


def _t5_bucket(dist):
    max_exact = N_BUCKETS // 2
    d = jnp.maximum(dist, 0)
    log_ratio = jnp.log(jnp.maximum(d, 1).astype(F32) / max_exact) / math.log(MAX_DISTANCE / max_exact)
    large = jnp.minimum(max_exact + (log_ratio * (N_BUCKETS - max_exact)).astype(jnp.int32), N_BUCKETS - 1)
    return jnp.where(d < max_exact, d, large)


def _bias_rows(rel_bias):
    assert TQ == TK
    n = TK
    d = np.arange(-(n - 1), 3 * n + 1)
    by_dist = rel_bias.astype(F32).T[:, _t5_bucket(jnp.asarray(np.maximum(d, 0), jnp.int32))]
    by_dist = jnp.where(jnp.asarray(d >= 0)[None, :], by_dist * LOG2E, NEG)
    return jnp.stack([by_dist[:, o * n:o * n + 2 * n] for o in range(3)], axis=1)[:, :, None, :]


def _pack_runs():
    runs, start = [], 0
    for e in range(1, N_PACK + 1):
        if e == N_PACK or _PACK_SRC[e] != _PACK_SRC[e - 1] + (1 if _PACK_SRC[e - 1] >= 0 else 0):
            runs.append((int(_PACK_SRC[start]), e - start))
            start = e
    return runs


_PACK_RUNS = _pack_runs()
PACK_ROWS = 128


def _pack_kernel(w_ref, pack_ref, gate_ref):
    w = w_ref[...]
    parts = [w[:, a:a + n] if a >= 0 else jnp.zeros((w.shape[0], n), F32) for a, n in _PACK_RUNS]
    pack_ref[...] = jnp.concatenate(parts, axis=1).astype(BF16)
    gate_ref[...] = w[:, _OFF["gate"]:].astype(BF16)


def _pack_weights(w_in):
    depth, d, n_in = w_in.shape
    row_block = lambda width: pl.BlockSpec((pl.Squeezed(), PACK_ROWS, width), lambda l, r: (l, r, 0))
    return pl.pallas_call(
        _pack_kernel,
        grid=(depth, d // PACK_ROWS),
        in_specs=[row_block(n_in)],
        out_specs=[row_block(N_PACK), row_block(N_MIXERS * D_MODEL)],
        out_shape=[jax.ShapeDtypeStruct((depth, d, N_PACK), BF16),
                   jax.ShapeDtypeStruct((depth, d, N_MIXERS * D_MODEL), BF16)],
        compiler_params=_cparams(2),
        name="pack_weights",
    )(w_in)


def kernel(x, w_in, w_br_sb, w_br_diff, w_br_dsa, w_br_moba, w_out, lambda_q1, lambda_k1, lambda_q2, lambda_k2,
           diff_subln_g, rel_bias, w_ffn_in, w_ffn_out, g_pre_mix, g_post_mix, g_pre_ffn, g_post_ffn):
    b, s, d = x.shape
    depth = w_in.shape[0]
    assert d == D_MODEL and s % TQ == 0 and s // MOBA_BLOCK >= 2
    t = b * s

    w_pack, w_gate = _pack_weights(w_in)
    w_br = jnp.stack([w_br_sb, w_br_diff, w_br_dsa, w_br_moba], axis=1).astype(BF16)
    w_o = w_out.astype(BF16)
    w_f1 = w_ffn_in.astype(BF16)
    w_f2 = w_ffn_out.astype(BF16)
    cs = jnp.asarray(_PACK_SCALE)[None, :]

    bw = _bias_rows(rel_bias)
    bw_df, bw_ds, bw_mb = bw[0:4], bw[4:8], bw[8:12]
    key = np.arange(TK)[:, None]
    qry = np.arange(TQ)[None, :]
    tri = jnp.asarray(key <= np.arange(TK)[None, :], BF16)
    tril = jnp.asarray(key >= np.arange(TK)[None, :], BF16)
    sb_mask = jnp.asarray(np.where(key < qry, 0.0, NEG), F32)

    xf = x.reshape(t, d)
    for l in range(depth):
        lam_init = 0.8 - 0.6 * math.exp(-0.3 * l)
        lamp = jnp.stack([lambda_q1[l], lambda_k1[l], lambda_q2[l], lambda_k2[l]]).astype(F32)
        cst = jnp.full((1, 128), lam_init, F32)
        zz, vt, wt = _proj(xf, g_pre_mix[l][None, :], w_pack, cs, l)
        o_sb = _sb_attention(zz, vt, b, s, tri, sb_mask)
        o_df = _diff_attention(zz, vt, b, s, bw_df, lamp, cst, diff_subln_g[l][:, None])
        o_ds = _dsa_attention(zz, vt, wt, b, s, bw_ds, tril)
        o_mb = _moba_attention(zz, vt, b, s, bw_mb)
        xf = _merge(xf, o_sb, o_df, o_ds, o_mb, g_pre_mix[l][None, :], w_gate, w_br, w_o, g_post_mix[l][None, :], l)
        xf = _ffn(xf, g_pre_ffn[l][None, :], w_f1, w_f2, g_post_ffn[l][None, :], l)
    return xf.reshape(b, s, d)
```

```python
import functools
import math

import numpy as np
import jax
import jax.numpy as jnp
from jax import lax
from jax.experimental import pallas as pl
from jax.experimental.pallas import tpu as pltpu

F32 = jnp.float32
BF16 = jnp.bfloat16

D_MODEL = 1024
HEAD_DIM = 64
N_HEADS = 4
N_MIXERS = 4
MIXER_WIDTH = N_HEADS * HEAD_DIM
DIFF_QK_DIM = 32
IDX_HEADS = 8
DSA_TOPK_MAX = 256
MOBA_BLOCK = 256
MOBA_TOPK = 3
N_BUCKETS = 32
MAX_DISTANCE = 128
D_FF = 2816
NORM_EPS = 1e-6

TQ = 256
TK = 256
NEG = -1e30
HALF_NEG = -0.5e30
BIG = 3e38
BISECT_WARMUP = 20
BISECT_TRIP = 2
BISECT_MAX_TRIPS = 134
LOG2E = math.log2(math.e)
SB_TAIL_CUTOFF = 120.0 * LOG2E
V_ROWS = 80
TILES_PER_STEP = 2
TM_MERGE = 512
TM_FFN = 512
N_GROUP = 11
N_SLAB = 4 * N_GROUP
N_PACK = (N_GROUP + N_MIXERS) * MIXER_WIDTH
VMEM_LIMIT = 56 * 1024 * 1024

G_SB_Q, G_SB_K, G_DF_Q, G_DF_K, G_DS_Q, G_DS_K, G_QI_A, G_QI_B, G_MB_Q, G_MB_K, G_KIDX = range(N_GROUP)
S_KIDX = 4 * G_KIDX
M_SB, M_DF, M_DS, M_MB = range(N_MIXERS)


def _layout():
    off = {}
    acc = 0
    for name, sz in (("q_sb", 256), ("k_sb", 256), ("v_sb", 256), ("q1", 128), ("q2", 128), ("k1", 128),
                     ("k2", 128), ("v_df", 256), ("q_ds", 256), ("k_ds", 256), ("v_ds", 256), ("qi", 512),
                     ("ki", 64), ("wi", 8), ("q_mb", 256), ("k_mb", 256), ("v_mb", 256), ("gate", 4096)):
        off[name] = acc
        acc += sz
    return off


_OFF = _layout()


def _pack_layout():
    off = _OFF
    cols, scale = [], []

    def add(start, n, s=1.0):
        cols.extend(range(start, start + n))
        scale.extend([s] * n)

    hd = HEAD_DIM ** -0.5
    hd2 = hd * LOG2E
    df2 = DIFF_QK_DIM ** -0.5 * LOG2E
    add(off["q_sb"], 256, hd2); add(off["k_sb"], 256)
    for h in range(N_HEADS):
        add(off["q1"] + h * 32, 32, df2); add(off["q2"] + h * 32, 32, df2)
    for h in range(N_HEADS):
        add(off["k1"] + h * 32, 32); add(off["k2"] + h * 32, 32)
    add(off["q_ds"], 256, hd2); add(off["k_ds"], 256)
    add(off["qi"], 512, HEAD_DIM ** -0.5)
    add(off["q_mb"], 256, hd2); add(off["k_mb"], 256)
    add(off["ki"], 64); add(off["wi"], IDX_HEADS, IDX_HEADS ** -0.5)
    cols.extend([-1] * 184); scale.extend([1.0] * 184)
    for name in ("v_sb", "v_df", "v_ds", "v_mb"):
        add(off[name], MIXER_WIDTH)
    assert len(cols) == N_PACK
    return np.asarray(cols, np.int32), np.asarray(scale, np.float32)


_PACK_SRC, _PACK_SCALE = _pack_layout()


def _dot(a, b):
    return jnp.dot(a, b, preferred_element_type=F32)


def _dot_nt(a, b):
    return lax.dot_general(a, b, (((1,), (1,)), ((), ())), preferred_element_type=F32)


def _rms(x, g):
    return x * lax.rsqrt(jnp.mean(x * x, axis=-1, keepdims=True) + NORM_EPS) * g


def _cparams(n_axes):
    return pltpu.CompilerParams(dimension_semantics=("arbitrary",) * n_axes, vmem_limit_bytes=VMEM_LIMIT)


def _const_spec(shape):
    nd = len(shape)
    return pl.BlockSpec(shape, lambda *_: (0,) * nd, pipeline_mode=pl.Buffered(1))


def _layer_spec(shape, layer):
    nd = len(shape)
    return pl.BlockSpec((pl.Squeezed(),) + tuple(shape), lambda *_: (layer,) + (0,) * nd, pipeline_mode=pl.Buffered(1))


def _proj_kernel(x_ref, g_ref, w_ref, cs_ref, zz_ref, vt_ref, wt_ref):
    h = _rms(x_ref[...], g_ref[...]).astype(BF16)

    def group(c):
        cols = slice(c * MIXER_WIDTH, (c + 1) * MIXER_WIDTH)
        return _dot(h, w_ref[:, cols]) * cs_ref[:, cols]

    for c in range(N_GROUP):
        r = group(c)
        for s in range(4):
            zz_ref[4 * c + s] = r[:, s * HEAD_DIM:(s + 1) * HEAD_DIM].astype(BF16)
        if c == G_KIDX:
            wt_ref[...] = jnp.transpose(r)[HEAD_DIM:HEAD_DIM + IDX_HEADS]
    for m in range(N_MIXERS):
        vt = jnp.transpose(group(N_GROUP + m)).astype(BF16)
        for hh in range(N_HEADS):
            vt_ref[m, 0, hh * V_ROWS:hh * V_ROWS + HEAD_DIM, :] = vt[hh * HEAD_DIM:(hh + 1) * HEAD_DIM]
            vt_ref[m, 0, hh * V_ROWS + HEAD_DIM:(hh + 1) * V_ROWS, :] = jnp.ones((V_ROWS - HEAD_DIM, TQ), BF16)


def _proj(x, g, w, cs, layer):
    t = x.shape[0]
    tm = TQ
    return pl.pallas_call(
        _proj_kernel,
        grid=(t // tm,),
        in_specs=[pl.BlockSpec((tm, D_MODEL), lambda i: (i, 0)),
                  _const_spec((1, D_MODEL)),
                  _layer_spec((D_MODEL, N_PACK), layer),
                  _const_spec((1, N_PACK))],
        out_specs=[pl.BlockSpec((N_SLAB, tm, HEAD_DIM), lambda i: (0, i, 0)),
                   pl.BlockSpec((N_MIXERS, 1, N_HEADS * V_ROWS, tm), lambda i: (0, i, 0, 0)),
                   pl.BlockSpec((IDX_HEADS, tm), lambda i: (0, i))],
        out_shape=[jax.ShapeDtypeStruct((N_SLAB, t, HEAD_DIM), BF16),
                   jax.ShapeDtypeStruct((N_MIXERS, t // tm, N_HEADS * V_ROWS, tm), BF16),
                   jax.ShapeDtypeStruct((IDX_HEADS, t), F32)],
        compiler_params=_cparams(1),
        name="proj",
    )(x, g, w, cs)


def _k_block(ref, h, j):
    return ref[h, pl.ds(pl.multiple_of(j * TK, TK), TK), :]


def _fold_keys(a, op):
    n = a.shape[0]
    while n > 8:
        n //= 2
        a = op(a[:n], a[n:2 * n])
    return a


def _vt_block(ref, h, j, rows=V_ROWS):
    return ref[0, j, h * V_ROWS:h * V_ROWS + rows, :]


def _softmax_block(s_list, vt_list, carry):
    ms = [jnp.maximum(c[0], jnp.max(_fold_keys(s, jnp.maximum), axis=0, keepdims=True))
          for s, c in zip(s_list, carry)]
    pvs = [_dot(vt, jnp.exp2(s - m).astype(BF16)) for vt, s, m in zip(vt_list, s_list, ms)]
    return tuple((m_new, jnp.exp2(m - m_new) * acc + pv) for (m, acc), m_new, pv in zip(carry, ms, pvs))


def _softmax_loop(n_blocks, logits, values, carry, groups=(2, 1)):
    assert groups[-1] == 1
    start = 0
    for g in groups:
        def body(t, carry, g=g, start=start):
            j = start + g * t
            s_all = [logits(j + u) for u in range(g)]
            for u in range(g):
                carry = _softmax_block(s_all[u], values(j + u), carry)
            return carry

        trips = (n_blocks - start) // g
        carry = lax.fori_loop(0, trips, body, carry)
        start = start + g * trips
    return carry


def _softmax_init():
    return (jnp.full((1, TQ), NEG, F32), jnp.zeros((V_ROWS, TQ), F32))


def _softmax_out(carry):
    _, acc = carry
    return acc[:HEAD_DIM] / acc[HEAD_DIM:HEAD_DIM + 1]


def _store_heads(o_ref, heads_t):
    o_ref[...] = jnp.transpose(jnp.concatenate(heads_t, axis=0)).astype(BF16)


BIAS_ROWS_SHAPE = (N_HEADS, 3, 1, 2 * TQ)
BIAS_TILES_SHAPE = (N_HEADS, 3, TK, TQ)


def _fill_bias_tiles(bw_ref, bt_ref):
    @pl.when((pl.program_id(0) == 0) & (pl.program_id(1) == 0))
    def _():
        for h in range(N_HEADS):
            for o in range(3):
                rows = jnp.broadcast_to(bw_ref[h, o], (TK, 2 * TQ))
                bt_ref[h, o] = pltpu.roll(rows, TQ + 1, 1, stride=1, stride_axis=0)[:, :TQ]


def _attn_specs(nq, gq, gk, mixer, s, tiles=1):
    steps = nq // tiles
    return [pl.BlockSpec((4, tiles * TQ, HEAD_DIM), lambda b, i: (gq, b * steps + i, 0)),
            pl.BlockSpec((4, s, HEAD_DIM), lambda b, i: (gk, b, 0)),
            pl.BlockSpec((1, s // TK, N_HEADS * V_ROWS, TK), lambda b, i: (mixer, b, 0, 0))]


def _out_spec(nq, tiles=1):
    steps = nq // tiles
    return pl.BlockSpec((tiles * TQ, N_HEADS * HEAD_DIM), lambda b, i: (b * steps + i, 0))


def _for_query_tiles(q_ref, o_ref, tile_fn):
    def body(t, _):
        rows = pl.ds(pl.multiple_of(t * TQ, TQ), TQ)
        tile_fn(pl.program_id(1) * TILES_PER_STEP + t, q_ref.at[:, rows, :], o_ref.at[rows, :])
        return 0

    lax.fori_loop(0, TILES_PER_STEP, body, 0)


def _sb_kernel(q_ref, k_ref, vt_ref, tri_ref, mask_ref, o_ref):
    i = pl.program_id(1)
    tri = tri_ref[...]

    def block(j, carry, masked):
        heads = range(N_HEADS)
        zs = [_dot_nt(_k_block(k_ref, h, j), q_ref[h]) for h in heads]
        if masked:
            zs = [z + mask_ref[...] for z in zs]
        sps = [jnp.maximum(z, 0.0) + jnp.log2(1.0 + jnp.exp2(-jnp.abs(z))) for z in zs]
        his = [sp.astype(BF16) for sp in sps]
        los = [(sp - hi.astype(F32)).astype(BF16) for sp, hi in zip(sps, his)]
        cs = [_dot(tri, hi) + _dot(tri, lo) + c[0] for hi, lo, c in zip(his, los, carry)]
        avs = [_dot(_vt_block(vt_ref, h, j, HEAD_DIM), jnp.exp2(z - c).astype(BF16))
               for h, z, c in zip(heads, zs, cs)]
        return tuple((c[0:1, :], old[1] + av) for c, old, av in zip(cs, carry, avs))

    init = tuple((jnp.zeros((1, TQ), F32), jnp.zeros((HEAD_DIM, TQ), F32)) for _ in range(N_HEADS))
    carry = block(i, init, True)

    def weights_left(carry):
        tail = functools.reduce(jnp.minimum, [c[0] for c in carry])
        return (jnp.min(tail) < SB_TAIL_CUTOFF).astype(jnp.int32)

    def earlier_block(c):
        j, carry, _ = c
        carry = block(j, carry, False)
        return j - 1, carry, weights_left(carry)

    _, carry, _ = lax.while_loop(lambda c: (c[0] >= 0) & (c[2] > 0), earlier_block, (i - 1, carry, weights_left(carry)))
    _store_heads(o_ref, [c[1] for c in carry])


def _sb_attention(zz, vt, b, s, tri, mask):
    nq = s // TQ
    return pl.pallas_call(
        _sb_kernel,
        grid=(b, nq),
        in_specs=_attn_specs(nq, G_SB_Q, G_SB_K, M_SB, s) + [_const_spec((TK, TK)), _const_spec((TK, TQ))],
        out_specs=_out_spec(nq),
        out_shape=jax.ShapeDtypeStruct((b * s, N_HEADS * HEAD_DIM), BF16),
        compiler_params=_cparams(2),
        name="sb_attn",
    )(zz, zz, vt, tri, mask)


def _diff_kernel(q_ref, k_ref, vt_ref, bw_ref, lam_ref, cst_ref, g_ref, o_ref, bt_ref):
    i = pl.program_id(1)
    _fill_bias_tiles(bw_ref, bt_ref)
    lp = lam_ref[...]
    lam_init = cst_ref[:, 0:1]
    lam = (jnp.exp(jnp.sum(lp[0:1] * lp[1:2], axis=-1, keepdims=True))
           - jnp.exp(jnp.sum(lp[2:3] * lp[3:4], axis=-1, keepdims=True)) + lam_init)
    lane = lax.broadcasted_iota(jnp.int32, (TQ, HEAD_DIM), 1)
    qs = []
    for h in range(N_HEADS):
        q = q_ref[h]
        qs.append((jnp.where(lane < DIFF_QK_DIM, q, jnp.zeros_like(q)),
                   jnp.where(lane >= DIFF_QK_DIM, q, jnp.zeros_like(q))))

    def logits(j):
        s_list = []
        for h in range(N_HEADS):
            kj = _k_block(k_ref, h, j)
            bias = bt_ref[h, jnp.minimum(i - j, 2)]
            s_list += [_dot_nt(kj, qs[h][0]) + bias, _dot_nt(kj, qs[h][1]) + bias]
        return s_list

    def values(j):
        return [_vt_block(vt_ref, h, j) for h in range(N_HEADS) for _ in range(2)]

    carry = _softmax_loop(i + 1, logits, values, tuple(_softmax_init() for _ in range(2 * N_HEADS)), groups=(4, 2, 1))
    outs = []
    for h in range(N_HEADS):
        o = _softmax_out(carry[2 * h]) - lam * _softmax_out(carry[2 * h + 1])
        o = o * lax.rsqrt(jnp.mean(o * o, axis=0, keepdims=True) + NORM_EPS) * g_ref[...]
        outs.append(o * (1.0 - lam_init))
    _store_heads(o_ref, outs)


def _diff_attention(zz, vt, b, s, bw, lamp, cst, g):
    nq = s // TQ
    return pl.pallas_call(
        _diff_kernel,
        grid=(b, nq),
        in_specs=_attn_specs(nq, G_DF_Q, G_DF_K, M_DF, s) + [
            _const_spec(BIAS_ROWS_SHAPE), _const_spec((4, DIFF_QK_DIM)), _const_spec((1, 128)),
            _const_spec((HEAD_DIM, 1))],
        out_specs=_out_spec(nq),
        out_shape=jax.ShapeDtypeStruct((b * s, N_HEADS * HEAD_DIM), BF16),
        scratch_shapes=[pltpu.VMEM(BIAS_TILES_SHAPE, F32)],
        compiler_params=_cparams(2),
        name="diff_attn",
    )(zz, zz, vt, bw, lamp, cst, g)


def _dsa_kernel(q_ref, k_ref, vt_ref, qia_ref, qib_ref, ki_ref, wt_ref, bw_ref, tril_ref, o_ref, sc_ref, bt_ref, *,
                topk):
    i = pl.program_id(1)
    _fill_bias_tiles(bw_ref, bt_ref)
    nb = i + 1
    kf = float(topk)
    w = wt_ref[...]
    key = lax.broadcasted_iota(jnp.int32, (TK, TQ), 0)
    qry = lax.broadcasted_iota(jnp.int32, (TK, TQ), 1)

    def score(j):
        kij = _k_block(ki_ref, 0, j)
        sc = jnp.zeros((TK, TQ), F32)
        for hh in range(IDX_HEADS):
            qi = (qia_ref if hh < 4 else qib_ref)[hh % 4]
            sc = sc + w[hh:hh + 1, :] * jnp.maximum(_dot_nt(kij, qi), 0.0)
        return sc

    def extend(lo_src, hi_src, c):
        return (jnp.minimum(c[0], _fold_keys(lo_src, jnp.minimum)), jnp.maximum(c[1], _fold_keys(hi_src, jnp.maximum)))

    def earlier_block(j, c):
        sc = score(j)
        sc_ref[j] = sc
        return extend(sc, sc, c)

    def earlier_pair(t, c):
        sc_a, sc_b = score(2 * t), score(2 * t + 1)
        sc_ref[2 * t] = sc_a
        sc_ref[2 * t + 1] = sc_b
        return extend(sc_b, sc_b, extend(sc_a, sc_a, c))

    lo_part, hi_part = lax.fori_loop(0, i // 2, earlier_pair,
                                     (jnp.full((8, TQ), BIG, F32), jnp.full((8, TQ), -BIG, F32)))
    lo_part, hi_part = lax.fori_loop(2 * (i // 2), i, earlier_block, (lo_part, hi_part))
    sc = score(i)
    causal = key <= qry
    sc_ref[i] = jnp.where(causal, sc, NEG)
    lo_part, hi_part = extend(jnp.where(causal, sc, BIG), jnp.where(causal, sc, NEG), (lo_part, hi_part))

    def reduce_blocks(fn, init):
        def pair(t, c):
            return fn(sc_ref[2 * t + 1], 2 * t + 1, fn(sc_ref[2 * t], 2 * t, c))
        c = lax.fori_loop(0, nb // 2, pair, init)
        return lax.fori_loop(2 * (nb // 2), nb, lambda j, c: fn(sc_ref[j], j, c), c)

    def count_ge(t):
        part = reduce_blocks(lambda x, j, c: c + _fold_keys(jnp.where(x >= t, 1.0, 0.0), jnp.add),
                             jnp.zeros((8, TQ), F32))
        return jnp.sum(part, axis=0, keepdims=True)

    def minmax_blocks(lo_of, hi_of):
        def f(x, j, c):
            return (jnp.minimum(c[0], _fold_keys(lo_of(x), jnp.minimum)),
                    jnp.maximum(c[1], _fold_keys(hi_of(x), jnp.maximum)))
        lo_part, hi_part = reduce_blocks(f, (jnp.full((8, TQ), BIG, F32), jnp.full((8, TQ), -BIG, F32)))
        return jnp.min(lo_part, axis=0, keepdims=True), jnp.max(hi_part, axis=0, keepdims=True)

    n_valid = i * TQ + lax.broadcasted_iota(jnp.int32, (1, TQ), 1) + 1
    take_all = n_valid <= topk
    lo = jnp.min(lo_part, axis=0, keepdims=True)
    hi = jnp.max(hi_part, axis=0, keepdims=True)
    c_max = count_ge(hi)
    at_max = c_max >= kf
    state = (jnp.where(at_max, hi, lo), hi, jnp.where(at_max, c_max, n_valid.astype(F32)), c_max)

    def bisect(_, state):
        lo, hi, c_lo, c_hi = state
        mid = 0.5 * lo + 0.5 * hi
        c = count_ge(mid)
        ge = c >= kf
        return jnp.where(ge, mid, lo), jnp.where(ge, hi, mid), jnp.where(ge, c, c_lo), jnp.where(ge, c_hi, c)

    def unsettled(state):
        lo, hi, c_lo, _ = state
        open_q = jnp.where(take_all, 0.0, jnp.where(c_lo != kf, jnp.where(lo < hi, 1.0, 0.0), 0.0))

        def band_spread():
            b_min, b_max = minmax_blocks(lambda x: jnp.where(x >= lo, jnp.where(x < hi, x, BIG), BIG),
                                         lambda x: jnp.where(x >= lo, jnp.where(x < hi, x, -BIG), -BIG))
            return (jnp.max(jnp.where(b_max != b_min, open_q, 0.0)) > 0.0).astype(jnp.int32)

        return lax.cond(jnp.max(open_q) > 0.0, band_spread, lambda: jnp.int32(0))

    state = lax.fori_loop(0, jnp.where((i + 1) * TQ <= topk, 0, BISECT_WARMUP), bisect, state)

    def trip(c):
        n, state, _ = c
        state = lax.fori_loop(0, BISECT_TRIP, bisect, state)
        return n + 1, state, unsettled(state)

    _, state, _ = lax.while_loop(lambda c: (c[2] > 0) & (c[0] < BISECT_MAX_TRIPS), trip,
                                 (jnp.int32(0), state, unsettled(state)))
    lo, hi, c_lo, c_hi = state
    hi_ok = lo < hi
    c_above = jnp.where(hi_ok, c_hi, 0.0)
    hi_sel = jnp.where(hi_ok, hi, BIG)
    need = jnp.where(take_all, BIG, kf - c_above)
    lo_sel = jnp.where(take_all, HALF_NEG, lo)

    tied = jnp.max(jnp.where(take_all, 0.0, c_lo - kf)) > 0.0

    @pl.when(tied)
    def _():
        tril = tril_ref[...]

        def band_of(j):
            x = sc_ref[j]
            return x, jnp.where(x >= lo_sel, jnp.where(x < hi_sel, 1.0, 0.0), 0.0)

        def write(j, x, band, rank):
            sc_ref[j] = jnp.where(x >= hi_sel, 0.0,
                                  jnp.where(band * rank > 0.0, jnp.where(rank <= need, 0.0, NEG), NEG))
            return rank[TK - 1:TK, :]

        def write_mask(j, taken):
            x, band = band_of(j)
            return write(j, x, band, _dot(tril, band.astype(BF16)) + taken)

        def write_mask_pair(t, taken):
            (x_a, band_a), (x_b, band_b) = band_of(2 * t), band_of(2 * t + 1)
            in_a, in_b = _dot(tril, band_a.astype(BF16)), _dot(tril, band_b.astype(BF16))
            taken = write(2 * t, x_a, band_a, in_a + taken)
            return write(2 * t + 1, x_b, band_b, in_b + taken)

        taken = lax.fori_loop(0, nb // 2, write_mask_pair, jnp.zeros((1, TQ), F32))
        lax.fori_loop(2 * (nb // 2), nb, write_mask, taken)

    @pl.when(jnp.logical_not(tied))
    def _():
        def write_mask(j, _):
            sc_ref[j] = jnp.where(sc_ref[j] >= lo_sel, 0.0, NEG)
            return 0

        lax.fori_loop(0, nb, write_mask, 0)

    def logits(j):
        return [_dot_nt(_k_block(k_ref, h, j), q_ref[h]) + bt_ref[h, jnp.minimum(i - j, 2)] + sc_ref[j]
                for h in range(N_HEADS)]

    carry = _softmax_loop(nb, logits, lambda j: [_vt_block(vt_ref, h, j) for h in range(N_HEADS)],
                          tuple(_softmax_init() for _ in range(N_HEADS)), groups=(4, 2, 1))
    _store_heads(o_ref, [_softmax_out(c) for c in carry])


def _dsa_attention(zz, vt, wt, b, s, bw, tril):
    nq = s // TQ
    topk = min(DSA_TOPK_MAX, s // 4)
    return pl.pallas_call(
        functools.partial(_dsa_kernel, topk=topk),
        grid=(b, nq),
        in_specs=_attn_specs(nq, G_DS_Q, G_DS_K, M_DS, s) + [
            pl.BlockSpec((4, TQ, HEAD_DIM), lambda b_, i: (G_QI_A, b_ * nq + i, 0)),
            pl.BlockSpec((4, TQ, HEAD_DIM), lambda b_, i: (G_QI_B, b_ * nq + i, 0)),
            pl.BlockSpec((1, s, HEAD_DIM), lambda b_, i: (S_KIDX, b_, 0)),
            pl.BlockSpec((IDX_HEADS, TQ), lambda b_, i: (0, b_ * nq + i)),
            _const_spec(BIAS_ROWS_SHAPE), _const_spec((TK, TK))],
        out_specs=_out_spec(nq),
        out_shape=jax.ShapeDtypeStruct((b * s, N_HEADS * HEAD_DIM), BF16),
        scratch_shapes=[pltpu.VMEM((nq, TK, TQ), F32), pltpu.VMEM(BIAS_TILES_SHAPE, F32)],
        compiler_params=_cparams(2),
        name="dsa_attn",
    )(zz, zz, vt, zz, zz, zz, wt, bw, tril)


def _moba_kernel(q_ref, k_ref, vt_ref, bw_ref, o_ref, km_ref, bt_ref, *, nblk, topb):
    _fill_bias_tiles(bw_ref, bt_ref)
    _for_query_tiles(q_ref, o_ref, functools.partial(_moba_tile, k_ref=k_ref, vt_ref=vt_ref, km_ref=km_ref,
                                                      bt_ref=bt_ref, nblk=nblk, topb=topb))


def _moba_tile(i, q_ref, o_ref, *, k_ref, vt_ref, km_ref, bt_ref, nblk, topb):
    nrow = km_ref.shape[1]

    @pl.when(i == 0)
    def _():
        km_ref[...] = jnp.zeros_like(km_ref)
        for h in range(N_HEADS):
            for n in range(nblk):
                kb = k_ref[h, n * MOBA_BLOCK:(n + 1) * MOBA_BLOCK, :].astype(F32)
                km_ref[h, n:n + 1, :] = jnp.mean(kb, axis=0, keepdims=True)

    blk = lax.broadcasted_iota(jnp.int32, (nrow, TQ), 0)
    past = blk < i
    head_bits = []
    for h in range(N_HEADS):
        gate = _dot_nt(km_ref[h].astype(BF16), q_ref[h])
        bits = jnp.zeros((1, TQ), F32)
        for n in range(nblk):
            gn = gate[n:n + 1, :]
            beats = jnp.where(past, jnp.where(gate > gn, 1.0, jnp.where(gate == gn, jnp.where(blk < n, 1.0, 0.0), 0.0)), 0.0)
            rank = jnp.sum(beats, axis=0, keepdims=True)
            bits = bits + jnp.where(rank < float(topb), jnp.where(n < i, float(2 ** n), 0.0), 0.0)
        head_bits.append(bits.astype(jnp.int32) | lax.shift_left(jnp.int32(1), i))

    def values(n):
        return [_vt_block(vt_ref, h, n) for h in range(N_HEADS)]

    def logits(n):
        s_list = []
        for h in range(N_HEADS):
            picked = (lax.shift_right_logical(head_bits[h], jnp.full_like(head_bits[h], n)) & 1) == 1
            s_list.append(_dot_nt(_k_block(k_ref, h, n), q_ref[h]) + bt_ref[h, jnp.minimum(i - n, 2)]
                          + jnp.where(picked, 0.0, NEG))
        return s_list

    carry = _softmax_loop(i + 1, logits, values, tuple(_softmax_init() for _ in range(N_HEADS)), groups=(4, 2, 1))
    _store_heads(o_ref, [_softmax_out(c) for c in carry])


def _moba_attention(zz, vt, b, s, bw):
    nq = s // TQ
    nblk = s // MOBA_BLOCK
    topb = min(MOBA_TOPK, nblk - 1)
    return pl.pallas_call(
        functools.partial(_moba_kernel, nblk=nblk, topb=topb),
        grid=(b, nq // TILES_PER_STEP),
        in_specs=_attn_specs(nq, G_MB_Q, G_MB_K, M_MB, s, TILES_PER_STEP) + [_const_spec(BIAS_ROWS_SHAPE)],
        out_specs=_out_spec(nq, TILES_PER_STEP),
        out_shape=jax.ShapeDtypeStruct((b * s, N_HEADS * HEAD_DIM), BF16),
        scratch_shapes=[pltpu.VMEM((N_HEADS, max(8, nblk), HEAD_DIM), F32), pltpu.VMEM(BIAS_TILES_SHAPE, F32)],
        compiler_params=_cparams(2),
        name="moba_attn",
    )(zz, zz, vt, bw)


def _merge_kernel(x_ref, osb_ref, odf_ref, ods_ref, omb_ref, gpre_ref, wg_ref, wbr_ref, wout_ref, gpost_ref, o_ref):
    x = x_ref[...]
    h = _rms(x, gpre_ref[...]).astype(BF16)
    y = jnp.zeros((x.shape[0], D_MODEL), F32)
    for r, o_r in enumerate((osb_ref, odf_ref, ods_ref, omb_ref)):
        gate = jax.nn.sigmoid(_dot(h, wg_ref[:, r * D_MODEL:(r + 1) * D_MODEL]))
        y = y + gate * _dot(o_r[...], wbr_ref[r])
    o_ref[...] = x + _rms(_dot(y.astype(BF16), wout_ref[...]), gpost_ref[...])


def _merge(x, o_sb, o_df, o_ds, o_mb, g_pre, w_gate, w_br, w_out, g_post, layer):
    t = x.shape[0]
    tm = TM_MERGE
    tok = lambda width: pl.BlockSpec((tm, width), lambda i: (i, 0))
    return pl.pallas_call(
        _merge_kernel,
        grid=(t // tm,),
        in_specs=[tok(D_MODEL)] + [tok(MIXER_WIDTH)] * N_MIXERS + [
            _const_spec((1, D_MODEL)), _layer_spec((D_MODEL, N_MIXERS * D_MODEL), layer),
            _layer_spec((N_MIXERS, MIXER_WIDTH, D_MODEL), layer), _layer_spec((D_MODEL, D_MODEL), layer),
            _const_spec((1, D_MODEL))],
        out_specs=tok(D_MODEL),
        out_shape=jax.ShapeDtypeStruct((t, D_MODEL), F32),
        compiler_params=_cparams(1),
        name="merge",
    )(x, o_sb, o_df, o_ds, o_mb, g_pre, w_gate, w_br, w_out, g_post)


def _ffn_kernel(x_ref, gpre_ref, win_ref, wout_ref, gpost_ref, o_ref):
    x = x_ref[...]
    h = _rms(x, gpre_ref[...]).astype(BF16)
    gate = _dot(h, win_ref[:, 0:D_FF])
    up = _dot(h, win_ref[:, D_FF:2 * D_FF])
    act = (gate * jax.nn.sigmoid(gate) * up).astype(BF16)
    o_ref[...] = x + _rms(_dot(act, wout_ref[...]), gpost_ref[...])


def _ffn(x, g_pre, w_in, w_out, g_post, layer):
    t = x.shape[0]
    tm = TM_FFN
    return pl.pallas_call(
        _ffn_kernel,
        grid=(t // tm,),
        in_specs=[pl.BlockSpec((tm, D_MODEL), lambda i: (i, 0)), _const_spec((1, D_MODEL)),
                  _layer_spec((D_MODEL, 2 * D_FF), layer), _layer_spec((D_FF, D_MODEL), layer),
                  _const_spec((1, D_MODEL))],
        out_specs=pl.BlockSpec((tm, D_MODEL), lambda i: (i, 0)),
        out_shape=jax.ShapeDtypeStruct((t, D_MODEL), F32),
        compiler_params=_cparams(1),
        name="ffn",
    )(x, g_pre, w_in, w_out, g_post)


def _t5_bucket(dist):
    max_exact = N_BUCKETS // 2
    d = jnp.maximum(dist, 0)
    log_ratio = jnp.log(jnp.maximum(d, 1).astype(F32) / max_exact) / math.log(MAX_DISTANCE / max_exact)
    large = jnp.minimum(max_exact + (log_ratio * (N_BUCKETS - max_exact)).astype(jnp.int32), N_BUCKETS - 1)
    return jnp.where(d < max_exact, d, large)


def _bias_rows(rel_bias):
    assert TQ == TK
    n = TK
    d = np.arange(-(n - 1), 3 * n + 1)
    by_dist = rel_bias.astype(F32).T[:, _t5_bucket(jnp.asarray(np.maximum(d, 0), jnp.int32))]
    by_dist = jnp.where(jnp.asarray(d >= 0)[None, :], by_dist * LOG2E, NEG)
    return jnp.stack([by_dist[:, o * n:o * n + 2 * n] for o in range(3)], axis=1)[:, :, None, :]


def _pack_runs():
    runs, start = [], 0
    for e in range(1, N_PACK + 1):
        if e == N_PACK or _PACK_SRC[e] != _PACK_SRC[e - 1] + (1 if _PACK_SRC[e - 1] >= 0 else 0):
            runs.append((int(_PACK_SRC[start]), e - start))
            start = e
    return runs


_PACK_RUNS = _pack_runs()
PACK_ROWS = 128


def _pack_kernel(w_ref, pack_ref, gate_ref):
    w = w_ref[...]
    parts = [w[:, a:a + n] if a >= 0 else jnp.zeros((w.shape[0], n), F32) for a, n in _PACK_RUNS]
    pack_ref[...] = jnp.concatenate(parts, axis=1).astype(BF16)
    gate_ref[...] = w[:, _OFF["gate"]:].astype(BF16)


def _pack_weights(w_in):
    depth, d, n_in = w_in.shape
    row_block = lambda width: pl.BlockSpec((pl.Squeezed(), PACK_ROWS, width), lambda l, r: (l, r, 0))
    return pl.pallas_call(
        _pack_kernel,
        grid=(depth, d // PACK_ROWS),
        in_specs=[row_block(n_in)],
        out_specs=[row_block(N_PACK), row_block(N_MIXERS * D_MODEL)],
        out_shape=[jax.ShapeDtypeStruct((depth, d, N_PACK), BF16),
                   jax.ShapeDtypeStruct((depth, d, N_MIXERS * D_MODEL), BF16)],
        compiler_params=_cparams(2),
        name="pack_weights",
    )(w_in)


def kernel(x, w_in, w_br_sb, w_br_diff, w_br_dsa, w_br_moba, w_out, lambda_q1, lambda_k1, lambda_q2, lambda_k2,
           diff_subln_g, rel_bias, w_ffn_in, w_ffn_out, g_pre_mix, g_post_mix, g_pre_ffn, g_post_ffn):
    b, s, d = x.shape
    depth = w_in.shape[0]
    assert d == D_MODEL and s % TQ == 0 and s // MOBA_BLOCK >= 2
    t = b * s

    w_pack, w_gate = _pack_weights(w_in)
    w_br = jnp.stack([w_br_sb, w_br_diff, w_br_dsa, w_br_moba], axis=1).astype(BF16)
    w_o = w_out.astype(BF16)
    w_f1 = w_ffn_in.astype(BF16)
    w_f2 = w_ffn_out.astype(BF16)
    cs = jnp.asarray(_PACK_SCALE)[None, :]

    bw = _bias_rows(rel_bias)
    bw_df, bw_ds, bw_mb = bw[0:4], bw[4:8], bw[8:12]
    key = np.arange(TK)[:, None]
    qry = np.arange(TQ)[None, :]
    tri = jnp.asarray(key <= np.arange(TK)[None, :], BF16)
    tril = jnp.asarray(key >= np.arange(TK)[None, :], BF16)
    sb_mask = jnp.asarray(np.where(key < qry, 0.0, NEG), F32)

    xf = x.reshape(t, d)
    for l in range(depth):
        lam_init = 0.8 - 0.6 * math.exp(-0.3 * l)
        lamp = jnp.stack([lambda_q1[l], lambda_k1[l], lambda_q2[l], lambda_k2[l]]).astype(F32)
        cst = jnp.full((1, 128), lam_init, F32)
        zz, vt, wt = _proj(xf, g_pre_mix[l][None, :], w_pack, cs, l)
        o_sb = _sb_attention(zz, vt, b, s, tri, sb_mask)
        o_df = _diff_attention(zz, vt, b, s, bw_df, lamp, cst, diff_subln_g[l][:, None])
        o_ds = _dsa_attention(zz, vt, wt, b, s, bw_ds, tril)
        o_mb = _moba_attention(zz, vt, b, s, bw_mb)
        xf = _merge(xf, o_sb, o_df, o_ds, o_mb, g_pre_mix[l][None, :], w_gate, w_br, w_o, g_post_mix[l][None, :], l)
        xf = _ffn(xf, g_pre_ffn[l][None, :], w_f1, w_f2, g_post_ffn[l][None, :], l)
    return xf.reshape(b, s, d)
```

```python
import functools
import math

import numpy as np
import jax
import jax.numpy as jnp
from jax import lax
from jax.experimental import pallas as pl
from jax.experimental.pallas import tpu as pltpu

F32 = jnp.float32
BF16 = jnp.bfloat16

D_MODEL = 1024
HEAD_DIM = 64
N_HEADS = 4
N_MIXERS = 4
MIXER_WIDTH = N_HEADS * HEAD_DIM
DIFF_QK_DIM = 32
IDX_HEADS = 8
DSA_TOPK_MAX = 256
MOBA_BLOCK = 256
MOBA_TOPK = 3
N_BUCKETS = 32
MAX_DISTANCE = 128
D_FF = 2816
NORM_EPS = 1e-6

TQ = 256
TK = 256
NEG = -1e30
HALF_NEG = -0.5e30
BIG = 3e38
BISECT_WARMUP = 20
BISECT_TRIP = 2
BISECT_MAX_TRIPS = 134
LOG2E = math.log2(math.e)
SB_TAIL_CUTOFF = 120.0 * LOG2E
V_ROWS = 80
TM_PROJ = 512
TM_MERGE = 512
TM_FFN = 512
N_GROUP = 11
N_SLAB = 4 * N_GROUP
N_PACK = (N_GROUP + N_MIXERS) * MIXER_WIDTH
VMEM_LIMIT = 56 * 1024 * 1024

G_SB_Q, G_SB_K, G_DF_Q, G_DF_K, G_DS_Q, G_DS_K, G_QI_A, G_QI_B, G_MB_Q, G_MB_K, G_KIDX = range(N_GROUP)
S_KIDX = 4 * G_KIDX
M_SB, M_DF, M_DS, M_MB = range(N_MIXERS)


def _layout():
    off = {}
    acc = 0
    for name, sz in (("q_sb", 256), ("k_sb", 256), ("v_sb", 256), ("q1", 128), ("q2", 128), ("k1", 128),
                     ("k2", 128), ("v_df", 256), ("q_ds", 256), ("k_ds", 256), ("v_ds", 256), ("qi", 512),
                     ("ki", 64), ("wi", 8), ("q_mb", 256), ("k_mb", 256), ("v_mb", 256), ("gate", 4096)):
        off[name] = acc
        acc += sz
    return off


_OFF = _layout()


def _pack_layout():
    off = _OFF
    cols, scale = [], []

    def add(start, n, s=1.0):
        cols.extend(range(start, start + n))
        scale.extend([s] * n)

    hd = HEAD_DIM ** -0.5
    hd2 = hd * LOG2E
    df2 = DIFF_QK_DIM ** -0.5 * LOG2E
    add(off["q_sb"], 256, hd2); add(off["k_sb"], 256)
    for h in range(N_HEADS):
        add(off["q1"] + h * 32, 32, df2); add(off["q2"] + h * 32, 32, df2)
    for h in range(N_HEADS):
        add(off["k1"] + h * 32, 32); add(off["k2"] + h * 32, 32)
    add(off["q_ds"], 256, hd2); add(off["k_ds"], 256)
    add(off["qi"], 512, HEAD_DIM ** -0.5)
    add(off["q_mb"], 256, hd2); add(off["k_mb"], 256)
    add(off["ki"], 64); add(off["wi"], IDX_HEADS, IDX_HEADS ** -0.5)
    cols.extend([-1] * 184); scale.extend([1.0] * 184)
    for name in ("v_sb", "v_df", "v_ds", "v_mb"):
        add(off[name], MIXER_WIDTH)
    assert len(cols) == N_PACK
    return np.asarray(cols, np.int32), np.asarray(scale, np.float32)


_PACK_SRC, _PACK_SCALE = _pack_layout()


def _dot(a, b):
    return jnp.dot(a, b, preferred_element_type=F32)


def _dot_nt(a, b):
    return lax.dot_general(a, b, (((1,), (1,)), ((), ())), preferred_element_type=F32)


def _rms(x, g):
    return x * lax.rsqrt(jnp.mean(x * x, axis=-1, keepdims=True) + NORM_EPS) * g


def _cparams(n_axes):
    return pltpu.CompilerParams(dimension_semantics=("arbitrary",) * n_axes, vmem_limit_bytes=VMEM_LIMIT)


def _const_spec(shape):
    nd = len(shape)
    return pl.BlockSpec(shape, lambda *_: (0,) * nd, pipeline_mode=pl.Buffered(1))


def _layer_spec(shape, layer):
    nd = len(shape)
    return pl.BlockSpec((pl.Squeezed(),) + tuple(shape), lambda *_: (layer,) + (0,) * nd, pipeline_mode=pl.Buffered(1))


def _proj_kernel(x_ref, g_ref, w_ref, cs_ref, zz_ref, vt_ref, wt_ref):
    h = _rms(x_ref[...], g_ref[...]).astype(BF16)

    def group(c):
        cols = slice(c * MIXER_WIDTH, (c + 1) * MIXER_WIDTH)
        return _dot(h, w_ref[:, cols]) * cs_ref[:, cols]

    for c in range(N_GROUP):
        r = group(c)
        for s in range(4):
            zz_ref[4 * c + s] = r[:, s * HEAD_DIM:(s + 1) * HEAD_DIM].astype(BF16)
        if c == G_KIDX:
            wt_ref[...] = jnp.transpose(r)[HEAD_DIM:HEAD_DIM + IDX_HEADS]
    for m in range(N_MIXERS):
        v = group(N_GROUP + m)
        for u in range(TM_PROJ // TK):
            vt = jnp.transpose(v[u * TK:(u + 1) * TK]).astype(BF16)
            for hh in range(N_HEADS):
                vt_ref[m, u, hh * V_ROWS:hh * V_ROWS + HEAD_DIM, :] = vt[hh * HEAD_DIM:(hh + 1) * HEAD_DIM]
                vt_ref[m, u, hh * V_ROWS + HEAD_DIM:(hh + 1) * V_ROWS, :] = jnp.ones((V_ROWS - HEAD_DIM, TK), BF16)


def _proj(x, g, w, cs, layer):
    t = x.shape[0]
    tm = TM_PROJ
    return pl.pallas_call(
        _proj_kernel,
        grid=(t // tm,),
        in_specs=[pl.BlockSpec((tm, D_MODEL), lambda i: (i, 0)),
                  _const_spec((1, D_MODEL)),
                  _layer_spec((D_MODEL, N_PACK), layer),
                  _const_spec((1, N_PACK))],
        out_specs=[pl.BlockSpec((N_SLAB, tm, HEAD_DIM), lambda i: (0, i, 0)),
                   pl.BlockSpec((N_MIXERS, tm // TK, N_HEADS * V_ROWS, TK), lambda i: (0, i, 0, 0)),
                   pl.BlockSpec((IDX_HEADS, tm), lambda i: (0, i))],
        out_shape=[jax.ShapeDtypeStruct((N_SLAB, t, HEAD_DIM), BF16),
                   jax.ShapeDtypeStruct((N_MIXERS, t // TK, N_HEADS * V_ROWS, TK), BF16),
                   jax.ShapeDtypeStruct((IDX_HEADS, t), F32)],
        compiler_params=_cparams(1),
        name="proj",
    )(x, g, w, cs)


def _k_block(ref, h, j):
    return ref[h, pl.ds(pl.multiple_of(j * TK, TK), TK), :]


def _fold_keys(a, op):
    n = a.shape[0]
    while n > 8:
        n //= 2
        a = op(a[:n], a[n:2 * n])
    return a


def _vt_block(ref, h, j, rows=V_ROWS):
    return ref[0, j, h * V_ROWS:h * V_ROWS + rows, :]


def _softmax_block(s_list, vt_list, carry):
    ms = [jnp.maximum(c[0], jnp.max(_fold_keys(s, jnp.maximum), axis=0, keepdims=True))
          for s, c in zip(s_list, carry)]
    pvs = [_dot(vt, jnp.exp2(s - m).astype(BF16)) for vt, s, m in zip(vt_list, s_list, ms)]
    return tuple((m_new, jnp.exp2(m - m_new) * acc + pv) for (m, acc), m_new, pv in zip(carry, ms, pvs))


def _softmax_loop(n_blocks, logits, values, carry, groups=(2, 1)):
    assert groups[-1] == 1
    start = 0
    for g in groups:
        def body(t, carry, g=g, start=start):
            j = start + g * t
            s_all = [logits(j + u) for u in range(g)]
            for u in range(g):
                carry = _softmax_block(s_all[u], values(j + u), carry)
            return carry

        trips = (n_blocks - start) // g
        carry = lax.fori_loop(0, trips, body, carry)
        start = start + g * trips
    return carry


def _softmax_init():
    return (jnp.full((1, TQ), NEG, F32), jnp.zeros((V_ROWS, TQ), F32))


def _softmax_out(carry):
    _, acc = carry
    return acc[:HEAD_DIM] / acc[HEAD_DIM:HEAD_DIM + 1]


def _store_heads(o_ref, heads_t):
    o_ref[...] = jnp.transpose(jnp.concatenate(heads_t, axis=0)).astype(BF16)


BIAS_ROWS_SHAPE = (N_HEADS, 3, 1, 2 * TQ)
BIAS_TILES_SHAPE = (N_HEADS, 3, TK, TQ)


def _fill_bias_tiles(bw_ref, bt_ref):
    @pl.when((pl.program_id(0) == 0) & (pl.program_id(1) == 0))
    def _():
        for h in range(N_HEADS):
            for o in range(3):
                rows = jnp.broadcast_to(bw_ref[h, o], (TK, 2 * TQ))
                bt_ref[h, o] = pltpu.roll(rows, TQ + 1, 1, stride=1, stride_axis=0)[:, :TQ]


def _attn_specs(nq, gq, gk, mixer, s):
    return [pl.BlockSpec((4, TQ, HEAD_DIM), lambda b, i: (gq, b * nq + i, 0)),
            pl.BlockSpec((4, s, HEAD_DIM), lambda b, i: (gk, b, 0)),
            pl.BlockSpec((1, s // TK, N_HEADS * V_ROWS, TK), lambda b, i: (mixer, b, 0, 0))]


def _out_spec(nq):
    return pl.BlockSpec((TQ, N_HEADS * HEAD_DIM), lambda b, i: (b * nq + i, 0))


def _sb_kernel(q_ref, k_ref, vt_ref, tri_ref, mask_ref, o_ref):
    i = pl.program_id(1)
    tri = tri_ref[...]

    def block(j, carry, masked):
        heads = range(N_HEADS)
        zs = [_dot_nt(_k_block(k_ref, h, j), q_ref[h]) for h in heads]
        if masked:
            zs = [z + mask_ref[...] for z in zs]
        sps = [jnp.maximum(z, 0.0) + jnp.log2(1.0 + jnp.exp2(-jnp.abs(z))) for z in zs]
        his = [sp.astype(BF16) for sp in sps]
        los = [(sp - hi.astype(F32)).astype(BF16) for sp, hi in zip(sps, his)]
        cs = [_dot(tri, hi) + _dot(tri, lo) + c[0] for hi, lo, c in zip(his, los, carry)]
        avs = [_dot(_vt_block(vt_ref, h, j, HEAD_DIM), jnp.exp2(z - c).astype(BF16))
               for h, z, c in zip(heads, zs, cs)]
        return tuple((c[0:1, :], old[1] + av) for c, old, av in zip(cs, carry, avs))

    init = tuple((jnp.zeros((1, TQ), F32), jnp.zeros((HEAD_DIM, TQ), F32)) for _ in range(N_HEADS))
    carry = block(i, init, True)

    def weights_left(carry):
        tail = functools.reduce(jnp.minimum, [c[0] for c in carry])
        return (jnp.min(tail) < SB_TAIL_CUTOFF).astype(jnp.int32)

    def earlier_block(c):
        j, carry, _ = c
        carry = block(j, carry, False)
        return j - 1, carry, weights_left(carry)

    _, carry, _ = lax.while_loop(lambda c: (c[0] >= 0) & (c[2] > 0), earlier_block, (i - 1, carry, weights_left(carry)))
    _store_heads(o_ref, [c[1] for c in carry])


def _sb_attention(zz, vt, b, s, tri, mask):
    nq = s // TQ
    return pl.pallas_call(
        _sb_kernel,
        grid=(b, nq),
        in_specs=_attn_specs(nq, G_SB_Q, G_SB_K, M_SB, s) + [_const_spec((TK, TK)), _const_spec((TK, TQ))],
        out_specs=_out_spec(nq),
        out_shape=jax.ShapeDtypeStruct((b * s, N_HEADS * HEAD_DIM), BF16),
        compiler_params=_cparams(2),
        name="sb_attn",
    )(zz, zz, vt, tri, mask)


def _diff_kernel(q_ref, k_ref, vt_ref, bw_ref, lam_ref, cst_ref, g_ref, o_ref, bt_ref):
    i = pl.program_id(1)
    _fill_bias_tiles(bw_ref, bt_ref)
    lp = lam_ref[...]
    lam_init = cst_ref[:, 0:1]
    lam = (jnp.exp(jnp.sum(lp[0:1] * lp[1:2], axis=-1, keepdims=True))
           - jnp.exp(jnp.sum(lp[2:3] * lp[3:4], axis=-1, keepdims=True)) + lam_init)
    lane = lax.broadcasted_iota(jnp.int32, (TQ, HEAD_DIM), 1)
    qs = []
    for h in range(N_HEADS):
        q = q_ref[h]
        qs.append((jnp.where(lane < DIFF_QK_DIM, q, jnp.zeros_like(q)),
                   jnp.where(lane >= DIFF_QK_DIM, q, jnp.zeros_like(q))))

    def logits(j):
        s_list = []
        for h in range(N_HEADS):
            kj = _k_block(k_ref, h, j)
            bias = bt_ref[h, jnp.minimum(i - j, 2)]
            s_list += [_dot_nt(kj, qs[h][0]) + bias, _dot_nt(kj, qs[h][1]) + bias]
        return s_list

    def values(j):
        return [_vt_block(vt_ref, h, j) for h in range(N_HEADS) for _ in range(2)]

    carry = _softmax_loop(i + 1, logits, values, tuple(_softmax_init() for _ in range(2 * N_HEADS)), groups=(4, 2, 1))
    outs = []
    for h in range(N_HEADS):
        o = _softmax_out(carry[2 * h]) - lam * _softmax_out(carry[2 * h + 1])
        o = o * lax.rsqrt(jnp.mean(o * o, axis=0, keepdims=True) + NORM_EPS) * g_ref[...]
        outs.append(o * (1.0 - lam_init))
    _store_heads(o_ref, outs)


def _diff_attention(zz, vt, b, s, bw, lamp, cst, g):
    nq = s // TQ
    return pl.pallas_call(
        _diff_kernel,
        grid=(b, nq),
        in_specs=_attn_specs(nq, G_DF_Q, G_DF_K, M_DF, s) + [
            _const_spec(BIAS_ROWS_SHAPE), _const_spec((4, DIFF_QK_DIM)), _const_spec((1, 128)),
            _const_spec((HEAD_DIM, 1))],
        out_specs=_out_spec(nq),
        out_shape=jax.ShapeDtypeStruct((b * s, N_HEADS * HEAD_DIM), BF16),
        scratch_shapes=[pltpu.VMEM(BIAS_TILES_SHAPE, F32)],
        compiler_params=_cparams(2),
        name="diff_attn",
    )(zz, zz, vt, bw, lamp, cst, g)


def _dsa_kernel(q_ref, k_ref, vt_ref, qia_ref, qib_ref, ki_ref, wt_ref, bw_ref, tril_ref, o_ref, sc_ref, bt_ref, *,
                topk):
    i = pl.program_id(1)
    _fill_bias_tiles(bw_ref, bt_ref)
    nb = i + 1
    kf = float(topk)
    w = wt_ref[...]
    key = lax.broadcasted_iota(jnp.int32, (TK, TQ), 0)
    qry = lax.broadcasted_iota(jnp.int32, (TK, TQ), 1)

    def score(j):
        kij = _k_block(ki_ref, 0, j)
        sc = jnp.zeros((TK, TQ), F32)
        for hh in range(IDX_HEADS):
            qi = (qia_ref if hh < 4 else qib_ref)[hh % 4]
            sc = sc + w[hh:hh + 1, :] * jnp.maximum(_dot_nt(kij, qi), 0.0)
        return sc

    def extend(lo_src, hi_src, c):
        return (jnp.minimum(c[0], _fold_keys(lo_src, jnp.minimum)), jnp.maximum(c[1], _fold_keys(hi_src, jnp.maximum)))

    def earlier_block(j, c):
        sc = score(j)
        sc_ref[j] = sc
        return extend(sc, sc, c)

    def earlier_pair(t, c):
        sc_a, sc_b = score(2 * t), score(2 * t + 1)
        sc_ref[2 * t] = sc_a
        sc_ref[2 * t + 1] = sc_b
        return extend(sc_b, sc_b, extend(sc_a, sc_a, c))

    lo_part, hi_part = lax.fori_loop(0, i // 2, earlier_pair,
                                     (jnp.full((8, TQ), BIG, F32), jnp.full((8, TQ), -BIG, F32)))
    lo_part, hi_part = lax.fori_loop(2 * (i // 2), i, earlier_block, (lo_part, hi_part))
    sc = score(i)
    causal = key <= qry
    sc_ref[i] = jnp.where(causal, sc, NEG)
    lo_part, hi_part = extend(jnp.where(causal, sc, BIG), jnp.where(causal, sc, NEG), (lo_part, hi_part))

    def reduce_blocks(fn, init):
        def pair(t, c):
            return fn(sc_ref[2 * t + 1], 2 * t + 1, fn(sc_ref[2 * t], 2 * t, c))
        c = lax.fori_loop(0, nb // 2, pair, init)
        return lax.fori_loop(2 * (nb // 2), nb, lambda j, c: fn(sc_ref[j], j, c), c)

    def count_ge(t):
        part = reduce_blocks(lambda x, j, c: c + _fold_keys(jnp.where(x >= t, 1.0, 0.0), jnp.add),
                             jnp.zeros((8, TQ), F32))
        return jnp.sum(part, axis=0, keepdims=True)

    def minmax_blocks(lo_of, hi_of):
        def f(x, j, c):
            return (jnp.minimum(c[0], _fold_keys(lo_of(x), jnp.minimum)),
                    jnp.maximum(c[1], _fold_keys(hi_of(x), jnp.maximum)))
        lo_part, hi_part = reduce_blocks(f, (jnp.full((8, TQ), BIG, F32), jnp.full((8, TQ), -BIG, F32)))
        return jnp.min(lo_part, axis=0, keepdims=True), jnp.max(hi_part, axis=0, keepdims=True)

    n_valid = i * TQ + lax.broadcasted_iota(jnp.int32, (1, TQ), 1) + 1
    take_all = n_valid <= topk
    lo = jnp.min(lo_part, axis=0, keepdims=True)
    hi = jnp.max(hi_part, axis=0, keepdims=True)
    c_max = count_ge(hi)
    at_max = c_max >= kf
    state = (jnp.where(at_max, hi, lo), hi, jnp.where(at_max, c_max, n_valid.astype(F32)), c_max)

    def bisect(_, state):
        lo, hi, c_lo, c_hi = state
        mid = 0.5 * lo + 0.5 * hi
        c = count_ge(mid)
        ge = c >= kf
        return jnp.where(ge, mid, lo), jnp.where(ge, hi, mid), jnp.where(ge, c, c_lo), jnp.where(ge, c_hi, c)

    def unsettled(state):
        lo, hi, c_lo, _ = state
        open_q = jnp.where(take_all, 0.0, jnp.where(c_lo != kf, jnp.where(lo < hi, 1.0, 0.0), 0.0))

        def band_spread():
            b_min, b_max = minmax_blocks(lambda x: jnp.where(x >= lo, jnp.where(x < hi, x, BIG), BIG),
                                         lambda x: jnp.where(x >= lo, jnp.where(x < hi, x, -BIG), -BIG))
            return (jnp.max(jnp.where(b_max != b_min, open_q, 0.0)) > 0.0).astype(jnp.int32)

        return lax.cond(jnp.max(open_q) > 0.0, band_spread, lambda: jnp.int32(0))

    state = lax.fori_loop(0, jnp.where((i + 1) * TQ <= topk, 0, BISECT_WARMUP), bisect, state)

    def trip(c):
        n, state, _ = c
        state = lax.fori_loop(0, BISECT_TRIP, bisect, state)
        return n + 1, state, unsettled(state)

    _, state, _ = lax.while_loop(lambda c: (c[2] > 0) & (c[0] < BISECT_MAX_TRIPS), trip,
                                 (jnp.int32(0), state, unsettled(state)))
    lo, hi, c_lo, c_hi = state
    hi_ok = lo < hi
    c_above = jnp.where(hi_ok, c_hi, 0.0)
    hi_sel = jnp.where(hi_ok, hi, BIG)
    need = jnp.where(take_all, BIG, kf - c_above)
    lo_sel = jnp.where(take_all, HALF_NEG, lo)

    tied = jnp.max(jnp.where(take_all, 0.0, c_lo - kf)) > 0.0

    @pl.when(tied)
    def _():
        tril = tril_ref[...]

        def band_of(j):
            x = sc_ref[j]
            return x, jnp.where(x >= lo_sel, jnp.where(x < hi_sel, 1.0, 0.0), 0.0)

        def write(j, x, band, rank):
            sc_ref[j] = jnp.where(x >= hi_sel, 0.0,
                                  jnp.where(band * rank > 0.0, jnp.where(rank <= need, 0.0, NEG), NEG))
            return rank[TK - 1:TK, :]

        def write_mask(j, taken):
            x, band = band_of(j)
            return write(j, x, band, _dot(tril, band.astype(BF16)) + taken)

        def write_mask_pair(t, taken):
            (x_a, band_a), (x_b, band_b) = band_of(2 * t), band_of(2 * t + 1)
            in_a, in_b = _dot(tril, band_a.astype(BF16)), _dot(tril, band_b.astype(BF16))
            taken = write(2 * t, x_a, band_a, in_a + taken)
            return write(2 * t + 1, x_b, band_b, in_b + taken)

        taken = lax.fori_loop(0, nb // 2, write_mask_pair, jnp.zeros((1, TQ), F32))
        lax.fori_loop(2 * (nb // 2), nb, write_mask, taken)

    @pl.when(jnp.logical_not(tied))
    def _():
        def write_mask(j, _):
            sc_ref[j] = jnp.where(sc_ref[j] >= lo_sel, 0.0, NEG)
            return 0

        lax.fori_loop(0, nb, write_mask, 0)

    def logits(j):
        return [_dot_nt(_k_block(k_ref, h, j), q_ref[h]) + bt_ref[h, jnp.minimum(i - j, 2)] + sc_ref[j]
                for h in range(N_HEADS)]

    carry = _softmax_loop(nb, logits, lambda j: [_vt_block(vt_ref, h, j) for h in range(N_HEADS)],
                          tuple(_softmax_init() for _ in range(N_HEADS)), groups=(4, 2, 1))
    _store_heads(o_ref, [_softmax_out(c) for c in carry])


def _dsa_attention(zz, vt, wt, b, s, bw, tril):
    nq = s // TQ
    topk = min(DSA_TOPK_MAX, s // 4)
    return pl.pallas_call(
        functools.partial(_dsa_kernel, topk=topk),
        grid=(b, nq),
        in_specs=_attn_specs(nq, G_DS_Q, G_DS_K, M_DS, s) + [
            pl.BlockSpec((4, TQ, HEAD_DIM), lambda b_, i: (G_QI_A, b_ * nq + i, 0)),
            pl.BlockSpec((4, TQ, HEAD_DIM), lambda b_, i: (G_QI_B, b_ * nq + i, 0)),
            pl.BlockSpec((1, s, HEAD_DIM), lambda b_, i: (S_KIDX, b_, 0)),
            pl.BlockSpec((IDX_HEADS, TQ), lambda b_, i: (0, b_ * nq + i)),
            _const_spec(BIAS_ROWS_SHAPE), _const_spec((TK, TK))],
        out_specs=_out_spec(nq),
        out_shape=jax.ShapeDtypeStruct((b * s, N_HEADS * HEAD_DIM), BF16),
        scratch_shapes=[pltpu.VMEM((nq, TK, TQ), F32), pltpu.VMEM(BIAS_TILES_SHAPE, F32)],
        compiler_params=_cparams(2),
        name="dsa_attn",
    )(zz, zz, vt, zz, zz, zz, wt, bw, tril)


def _moba_kernel(q_ref, k_ref, vt_ref, bw_ref, o_ref, km_ref, bt_ref, *, nblk, topb):
    i = pl.program_id(1)
    _fill_bias_tiles(bw_ref, bt_ref)
    nrow = km_ref.shape[1]

    @pl.when(i == 0)
    def _():
        km_ref[...] = jnp.zeros_like(km_ref)
        for h in range(N_HEADS):
            for n in range(nblk):
                kb = k_ref[h, n * MOBA_BLOCK:(n + 1) * MOBA_BLOCK, :].astype(F32)
                km_ref[h, n:n + 1, :] = jnp.mean(kb, axis=0, keepdims=True)

    blk = lax.broadcasted_iota(jnp.int32, (nrow, TQ), 0)
    past = blk < i
    head_bits = []
    for h in range(N_HEADS):
        gate = _dot_nt(km_ref[h].astype(BF16), q_ref[h])
        bits = jnp.zeros((1, TQ), F32)
        for n in range(nblk):
            gn = gate[n:n + 1, :]
            beats = jnp.where(past, jnp.where(gate > gn, 1.0, jnp.where(gate == gn, jnp.where(blk < n, 1.0, 0.0), 0.0)), 0.0)
            rank = jnp.sum(beats, axis=0, keepdims=True)
            bits = bits + jnp.where(rank < float(topb), jnp.where(n < i, float(2 ** n), 0.0), 0.0)
        head_bits.append(bits.astype(jnp.int32) | lax.shift_left(jnp.int32(1), i))

    def values(n):
        return [_vt_block(vt_ref, h, n) for h in range(N_HEADS)]

    def logits(n):
        s_list = []
        for h in range(N_HEADS):
            picked = (lax.shift_right_logical(head_bits[h], jnp.full_like(head_bits[h], n)) & 1) == 1
            s_list.append(_dot_nt(_k_block(k_ref, h, n), q_ref[h]) + bt_ref[h, jnp.minimum(i - n, 2)]
                          + jnp.where(picked, 0.0, NEG))
        return s_list

    carry = _softmax_loop(i + 1, logits, values, tuple(_softmax_init() for _ in range(N_HEADS)), groups=(4, 2, 1))
    _store_heads(o_ref, [_softmax_out(c) for c in carry])


def _moba_attention(zz, vt, b, s, bw):
    nq = s // TQ
    nblk = s // MOBA_BLOCK
    topb = min(MOBA_TOPK, nblk - 1)
    return pl.pallas_call(
        functools.partial(_moba_kernel, nblk=nblk, topb=topb),
        grid=(b, nq),
        in_specs=_attn_specs(nq, G_MB_Q, G_MB_K, M_MB, s) + [_const_spec(BIAS_ROWS_SHAPE)],
        out_specs=_out_spec(nq),
        out_shape=jax.ShapeDtypeStruct((b * s, N_HEADS * HEAD_DIM), BF16),
        scratch_shapes=[pltpu.VMEM((N_HEADS, max(8, nblk), HEAD_DIM), F32), pltpu.VMEM(BIAS_TILES_SHAPE, F32)],
        compiler_params=_cparams(2),
        name="moba_attn",
    )(zz, zz, vt, bw)


def _merge_kernel(x_ref, osb_ref, odf_ref, ods_ref, omb_ref, gpre_ref, wg_ref, wbr_ref, wout_ref, gpost_ref, o_ref):
    x = x_ref[...]
    h = _rms(x, gpre_ref[...]).astype(BF16)
    y = jnp.zeros((x.shape[0], D_MODEL), F32)
    for r, o_r in enumerate((osb_ref, odf_ref, ods_ref, omb_ref)):
        gate = jax.nn.sigmoid(_dot(h, wg_ref[:, r * D_MODEL:(r + 1) * D_MODEL]))
        y = y + gate * _dot(o_r[...], wbr_ref[r])
    o_ref[...] = x + _rms(_dot(y.astype(BF16), wout_ref[...]), gpost_ref[...])


def _merge(x, o_sb, o_df, o_ds, o_mb, g_pre, w_gate, w_br, w_out, g_post, layer):
    t = x.shape[0]
    tm = TM_MERGE
    tok = lambda width: pl.BlockSpec((tm, width), lambda i: (i, 0))
    return pl.pallas_call(
        _merge_kernel,
        grid=(t // tm,),
        in_specs=[tok(D_MODEL)] + [tok(MIXER_WIDTH)] * N_MIXERS + [
            _const_spec((1, D_MODEL)), _layer_spec((D_MODEL, N_MIXERS * D_MODEL), layer),
            _layer_spec((N_MIXERS, MIXER_WIDTH, D_MODEL), layer), _layer_spec((D_MODEL, D_MODEL), layer),
            _const_spec((1, D_MODEL))],
        out_specs=tok(D_MODEL),
        out_shape=jax.ShapeDtypeStruct((t, D_MODEL), F32),
        compiler_params=_cparams(1),
        name="merge",
    )(x, o_sb, o_df, o_ds, o_mb, g_pre, w_gate, w_br, w_out, g_post)


def _ffn_kernel(x_ref, gpre_ref, win_ref, wout_ref, gpost_ref, o_ref):
    x = x_ref[...]
    h = _rms(x, gpre_ref[...]).astype(BF16)
    gate = _dot(h, win_ref[:, 0:D_FF])
    up = _dot(h, win_ref[:, D_FF:2 * D_FF])
    act = (gate * jax.nn.sigmoid(gate) * up).astype(BF16)
    o_ref[...] = x + _rms(_dot(act, wout_ref[...]), gpost_ref[...])


def _ffn(x, g_pre, w_in, w_out, g_post, layer):
    t = x.shape[0]
    tm = TM_FFN
    return pl.pallas_call(
        _ffn_kernel,
        grid=(t // tm,),
        in_specs=[pl.BlockSpec((tm, D_MODEL), lambda i: (i, 0)), _const_spec((1, D_MODEL)),
                  _layer_spec((D_MODEL, 2 * D_FF), layer), _layer_spec((D_FF, D_MODEL), layer),
                  _const_spec((1, D_MODEL))],
        out_specs=pl.BlockSpec((tm, D_MODEL), lambda i: (i, 0)),
        out_shape=jax.ShapeDtypeStruct((t, D_MODEL), F32),
        compiler_params=_cparams(1),
        name="ffn",
    )(x, g_pre, w_in, w_out, g_post)


def _t5_bucket(dist):
    max_exact = N_BUCKETS // 2
    d = jnp.maximum(dist, 0)
    log_ratio = jnp.log(jnp.maximum(d, 1).astype(F32) / max_exact) / math.log(MAX_DISTANCE / max_exact)
    large = jnp.minimum(max_exact + (log_ratio * (N_BUCKETS - max_exact)).astype(jnp.int32), N_BUCKETS - 1)
    return jnp.where(d < max_exact, d, large)


def _bias_rows(rel_bias):
    assert TQ == TK
    n = TK
    d = np.arange(-(n - 1), 3 * n + 1)
    by_dist = rel_bias.astype(F32).T[:, _t5_bucket(jnp.asarray(np.maximum(d, 0), jnp.int32))]
    by_dist = jnp.where(jnp.asarray(d >= 0)[None, :], by_dist * LOG2E, NEG)
    return jnp.stack([by_dist[:, o * n:o * n + 2 * n] for o in range(3)], axis=1)[:, :, None, :]


def _pack_runs():
    runs, start = [], 0
    for e in range(1, N_PACK + 1):
        if e == N_PACK or _PACK_SRC[e] != _PACK_SRC[e - 1] + (1 if _PACK_SRC[e - 1] >= 0 else 0):
            runs.append((int(_PACK_SRC[start]), e - start))
            start = e
    return runs


_PACK_RUNS = _pack_runs()
PACK_ROWS = 128


def _pack_kernel(w_ref, pack_ref, gate_ref):
    w = w_ref[...]
    parts = [w[:, a:a + n] if a >= 0 else jnp.zeros((w.shape[0], n), F32) for a, n in _PACK_RUNS]
    pack_ref[...] = jnp.concatenate(parts, axis=1).astype(BF16)
    gate_ref[...] = w[:, _OFF["gate"]:].astype(BF16)


def _pack_weights(w_in):
    depth, d, n_in = w_in.shape
    row_block = lambda width: pl.BlockSpec((pl.Squeezed(), PACK_ROWS, width), lambda l, r: (l, r, 0))
    return pl.pallas_call(
        _pack_kernel,
        grid=(depth, d // PACK_ROWS),
        in_specs=[row_block(n_in)],
        out_specs=[row_block(N_PACK), row_block(N_MIXERS * D_MODEL)],
        out_shape=[jax.ShapeDtypeStruct((depth, d, N_PACK), BF16),
                   jax.ShapeDtypeStruct((depth, d, N_MIXERS * D_MODEL), BF16)],
        compiler_params=_cparams(2),
        name="pack_weights",
    )(w_in)


def kernel(x, w_in, w_br_sb, w_br_diff, w_br_dsa, w_br_moba, w_out, lambda_q1, lambda_k1, lambda_q2, lambda_k2,
           diff_subln_g, rel_bias, w_ffn_in, w_ffn_out, g_pre_mix, g_post_mix, g_pre_ffn, g_post_ffn):
    b, s, d = x.shape
    depth = w_in.shape[0]
    assert d == D_MODEL and s % TQ == 0 and s // MOBA_BLOCK >= 2
    t = b * s

    w_pack, w_gate = _pack_weights(w_in)
    w_br = jnp.stack([w_br_sb, w_br_diff, w_br_dsa, w_br_moba], axis=1).astype(BF16)
    w_o = w_out.astype(BF16)
    w_f1 = w_ffn_in.astype(BF16)
    w_f2 = w_ffn_out.astype(BF16)
    cs = jnp.asarray(_PACK_SCALE)[None, :]

    bw = _bias_rows(rel_bias)
    bw_df, bw_ds, bw_mb = bw[0:4], bw[4:8], bw[8:12]
    key = np.arange(TK)[:, None]
    qry = np.arange(TQ)[None, :]
    tri = jnp.asarray(key <= np.arange(TK)[None, :], BF16)
    tril = jnp.asarray(key >= np.arange(TK)[None, :], BF16)
    sb_mask = jnp.asarray(np.where(key < qry, 0.0, NEG), F32)

    xf = x.reshape(t, d)
    for l in range(depth):
        lam_init = 0.8 - 0.6 * math.exp(-0.3 * l)
        lamp = jnp.stack([lambda_q1[l], lambda_k1[l], lambda_q2[l], lambda_k2[l]]).astype(F32)
        cst = jnp.full((1, 128), lam_init, F32)
        zz, vt, wt = _proj(xf, g_pre_mix[l][None, :], w_pack, cs, l)
        o_sb = _sb_attention(zz, vt, b, s, tri, sb_mask)
        o_df = _diff_attention(zz, vt, b, s, bw_df, lamp, cst, diff_subln_g[l][:, None])
        o_ds = _dsa_attention(zz, vt, wt, b, s, bw_ds, tril)
        o_mb = _moba_attention(zz, vt, b, s, bw_mb)
        xf = _merge(xf, o_sb, o_df, o_ds, o_mb, g_pre_mix[l][None, :], w_gate, w_br, w_o, g_post_mix[l][None, :], l)
        xf = _ffn(xf, g_pre_ffn[l][None, :], w_f1, w_f2, g_post_ffn[l][None, :], l)
    return xf.reshape(b, s, d)
```

```python
import functools
import math

import numpy as np
import jax
import jax.numpy as jnp
from jax import lax
from jax.experimental import pallas as pl
from jax.experimental.pallas import tpu as pltpu

F32 = jnp.float32
BF16 = jnp.bfloat16

D_MODEL = 1024
HEAD_DIM = 64
N_HEADS = 4
N_MIXERS = 4
MIXER_WIDTH = N_HEADS * HEAD_DIM
DIFF_QK_DIM = 32
IDX_HEADS = 8
DSA_TOPK_MAX = 256
MOBA_BLOCK = 256
MOBA_TOPK = 3
N_BUCKETS = 32
MAX_DISTANCE = 128
D_FF = 2816
NORM_EPS = 1e-6

TQ = 256
TK = 256
NEG = -1e30
HALF_NEG = -0.5e30
BIG = 3e38
BISECT_WARMUP = 20
BISECT_TRIP = 2
BISECT_MAX_TRIPS = 134
LOG2E = math.log2(math.e)
SB_TAIL_CUTOFF = 120.0 * LOG2E
V_ROWS = 80
TM_PROJ = 512
TM_MERGE = 512
TM_FFN = 512
N_GROUP = 11
N_SLAB = 4 * N_GROUP
N_PACK = (N_GROUP + N_MIXERS) * MIXER_WIDTH
VMEM_LIMIT = 56 * 1024 * 1024

G_SB_Q, G_SB_K, G_DF_Q, G_DF_K, G_DS_Q, G_DS_K, G_QI_A, G_QI_B, G_MB_Q, G_MB_K, G_KIDX = range(N_GROUP)
S_KIDX = 4 * G_KIDX
M_SB, M_DF, M_DS, M_MB = range(N_MIXERS)


def _layout():
    off = {}
    acc = 0
    for name, sz in (("q_sb", 256), ("k_sb", 256), ("v_sb", 256), ("q1", 128), ("q2", 128), ("k1", 128),
                     ("k2", 128), ("v_df", 256), ("q_ds", 256), ("k_ds", 256), ("v_ds", 256), ("qi", 512),
                     ("ki", 64), ("wi", 8), ("q_mb", 256), ("k_mb", 256), ("v_mb", 256), ("gate", 4096)):
        off[name] = acc
        acc += sz
    return off


_OFF = _layout()


def _pack_layout():
    off = _OFF
    cols, scale = [], []

    def add(start, n, s=1.0):
        cols.extend(range(start, start + n))
        scale.extend([s] * n)

    hd = HEAD_DIM ** -0.5
    hd2 = hd * LOG2E
    df2 = DIFF_QK_DIM ** -0.5 * LOG2E
    add(off["q_sb"], 256, hd2); add(off["k_sb"], 256)
    for h in range(N_HEADS):
        add(off["q1"] + h * 32, 32, df2); add(off["q2"] + h * 32, 32, df2)
    for h in range(N_HEADS):
        add(off["k1"] + h * 32, 32); add(off["k2"] + h * 32, 32)
    add(off["q_ds"], 256, hd2); add(off["k_ds"], 256)
    add(off["qi"], 512, HEAD_DIM ** -0.5)
    add(off["q_mb"], 256, hd2); add(off["k_mb"], 256)
    add(off["ki"], 64); add(off["wi"], IDX_HEADS, IDX_HEADS ** -0.5)
    cols.extend([-1] * 184); scale.extend([1.0] * 184)
    for name in ("v_sb", "v_df", "v_ds", "v_mb"):
        add(off[name], MIXER_WIDTH)
    assert len(cols) == N_PACK
    return np.asarray(cols, np.int32), np.asarray(scale, np.float32)


_PACK_SRC, _PACK_SCALE = _pack_layout()


def _dot(a, b):
    return jnp.dot(a, b, preferred_element_type=F32)


def _dot_nt(a, b):
    return lax.dot_general(a, b, (((1,), (1,)), ((), ())), preferred_element_type=F32)


def _rms(x, g):
    return x * lax.rsqrt(jnp.mean(x * x, axis=-1, keepdims=True) + NORM_EPS) * g


def _cparams(n_axes):
    return pltpu.CompilerParams(dimension_semantics=("arbitrary",) * n_axes, vmem_limit_bytes=VMEM_LIMIT)


def _const_spec(shape):
    nd = len(shape)
    return pl.BlockSpec(shape, lambda *_: (0,) * nd, pipeline_mode=pl.Buffered(1))


def _layer_spec(shape, layer):
    nd = len(shape)
    return pl.BlockSpec((pl.Squeezed(),) + tuple(shape), lambda *_: (layer,) + (0,) * nd, pipeline_mode=pl.Buffered(1))


def _proj_kernel(x_ref, g_ref, w_ref, cs_ref, zz_ref, vt_ref, wt_ref):
    h = _rms(x_ref[...], g_ref[...]).astype(BF16)

    def group(c):
        cols = slice(c * MIXER_WIDTH, (c + 1) * MIXER_WIDTH)
        return _dot(h, w_ref[:, cols]) * cs_ref[:, cols]

    for c in range(N_GROUP):
        r = group(c)
        for s in range(4):
            zz_ref[4 * c + s] = r[:, s * HEAD_DIM:(s + 1) * HEAD_DIM].astype(BF16)
        if c == G_KIDX:
            wt_ref[...] = jnp.transpose(r)[HEAD_DIM:HEAD_DIM + IDX_HEADS]
    for m in range(N_MIXERS):
        v = group(N_GROUP + m)
        for u in range(TM_PROJ // TK):
            vt = jnp.transpose(v[u * TK:(u + 1) * TK]).astype(BF16)
            for hh in range(N_HEADS):
                vt_ref[m, u, hh * V_ROWS:hh * V_ROWS + HEAD_DIM, :] = vt[hh * HEAD_DIM:(hh + 1) * HEAD_DIM]
                vt_ref[m, u, hh * V_ROWS + HEAD_DIM:(hh + 1) * V_ROWS, :] = jnp.ones((V_ROWS - HEAD_DIM, TK), BF16)


def _proj(x, g, w, cs, layer):
    t = x.shape[0]
    tm = TM_PROJ
    return pl.pallas_call(
        _proj_kernel,
        grid=(t // tm,),
        in_specs=[pl.BlockSpec((tm, D_MODEL), lambda i: (i, 0)),
                  _const_spec((1, D_MODEL)),
                  _layer_spec((D_MODEL, N_PACK), layer),
                  _const_spec((1, N_PACK))],
        out_specs=[pl.BlockSpec((N_SLAB, tm, HEAD_DIM), lambda i: (0, i, 0)),
                   pl.BlockSpec((N_MIXERS, tm // TK, N_HEADS * V_ROWS, TK), lambda i: (0, i, 0, 0)),
                   pl.BlockSpec((IDX_HEADS, tm), lambda i: (0, i))],
        out_shape=[jax.ShapeDtypeStruct((N_SLAB, t, HEAD_DIM), BF16),
                   jax.ShapeDtypeStruct((N_MIXERS, t // TK, N_HEADS * V_ROWS, TK), BF16),
                   jax.ShapeDtypeStruct((IDX_HEADS, t), F32)],
        compiler_params=_cparams(1),
        name="proj",
    )(x, g, w, cs)


def _k_block(ref, h, j):
    return ref[h, pl.ds(pl.multiple_of(j * TK, TK), TK), :]


def _fold_keys(a, op):
    n = a.shape[0]
    while n > 8:
        n //= 2
        a = op(a[:n], a[n:2 * n])
    return a


def _vt_block(ref, h, j, rows=V_ROWS):
    return ref[0, j, h * V_ROWS:h * V_ROWS + rows, :]


def _softmax_block(s_list, vt_list, carry):
    ms = [jnp.maximum(c[0], jnp.max(_fold_keys(s, jnp.maximum), axis=0, keepdims=True))
          for s, c in zip(s_list, carry)]
    pvs = [_dot(vt, jnp.exp2(s - m).astype(BF16)) for vt, s, m in zip(vt_list, s_list, ms)]
    return tuple((m_new, jnp.exp2(m - m_new) * acc + pv) for (m, acc), m_new, pv in zip(carry, ms, pvs))


def _softmax_loop(n_blocks, logits, values, carry, groups=(2, 1)):
    assert groups[-1] == 1
    start = 0
    for g in groups:
        def body(t, carry, g=g, start=start):
            j = start + g * t
            s_all = [logits(j + u) for u in range(g)]
            for u in range(g):
                carry = _softmax_block(s_all[u], values(j + u), carry)
            return carry

        trips = (n_blocks - start) // g
        carry = lax.fori_loop(0, trips, body, carry)
        start = start + g * trips
    return carry


def _softmax_init():
    return (jnp.full((1, TQ), NEG, F32), jnp.zeros((V_ROWS, TQ), F32))


def _softmax_out(carry):
    _, acc = carry
    return acc[:HEAD_DIM] / acc[HEAD_DIM:HEAD_DIM + 1]


def _store_heads(o_ref, heads_t):
    o_ref[...] = jnp.transpose(jnp.concatenate(heads_t, axis=0)).astype(BF16)


BIAS_ROWS_SHAPE = (N_HEADS, 3, 1, 2 * TQ)
BIAS_TILES_SHAPE = (N_HEADS, 3, TK, TQ)


def _fill_bias_tiles(bw_ref, bt_ref):
    @pl.when((pl.program_id(0) == 0) & (pl.program_id(1) == 0))
    def _():
        for h in range(N_HEADS):
            for o in range(3):
                rows = jnp.broadcast_to(bw_ref[h, o], (TK, 2 * TQ))
                bt_ref[h, o] = pltpu.roll(rows, TQ + 1, 1, stride=1, stride_axis=0)[:, :TQ]


def _attn_specs(nq, gq, gk, mixer, s):
    return [pl.BlockSpec((4, TQ, HEAD_DIM), lambda b, i: (gq, b * nq + i, 0)),
            pl.BlockSpec((4, s, HEAD_DIM), lambda b, i: (gk, b, 0)),
            pl.BlockSpec((1, s // TK, N_HEADS * V_ROWS, TK), lambda b, i: (mixer, b, 0, 0))]


def _out_spec(nq):
    return pl.BlockSpec((TQ, N_HEADS * HEAD_DIM), lambda b, i: (b * nq + i, 0))


def _sb_kernel(q_ref, k_ref, vt_ref, tri_ref, mask_ref, o_ref):
    i = pl.program_id(1)
    tri = tri_ref[...]

    def block(j, carry, masked):
        heads = range(N_HEADS)
        zs = [_dot_nt(_k_block(k_ref, h, j), q_ref[h]) for h in heads]
        if masked:
            zs = [z + mask_ref[...] for z in zs]
        sps = [jnp.maximum(z, 0.0) + jnp.log2(1.0 + jnp.exp2(-jnp.abs(z))) for z in zs]
        his = [sp.astype(BF16) for sp in sps]
        los = [(sp - hi.astype(F32)).astype(BF16) for sp, hi in zip(sps, his)]
        cs = [_dot(tri, hi) + _dot(tri, lo) + c[0] for hi, lo, c in zip(his, los, carry)]
        avs = [_dot(_vt_block(vt_ref, h, j, HEAD_DIM), jnp.exp2(z - c).astype(BF16))
               for h, z, c in zip(heads, zs, cs)]
        return tuple((c[0:1, :], old[1] + av) for c, old, av in zip(cs, carry, avs))

    init = tuple((jnp.zeros((1, TQ), F32), jnp.zeros((HEAD_DIM, TQ), F32)) for _ in range(N_HEADS))
    carry = block(i, init, True)

    def weights_left(carry):
        tail = functools.reduce(jnp.minimum, [c[0] for c in carry])
        return (jnp.min(tail) < SB_TAIL_CUTOFF).astype(jnp.int32)

    def earlier_block(c):
        j, carry, _ = c
        carry = block(j, carry, False)
        return j - 1, carry, weights_left(carry)

    _, carry, _ = lax.while_loop(lambda c: (c[0] >= 0) & (c[2] > 0), earlier_block, (i - 1, carry, weights_left(carry)))
    _store_heads(o_ref, [c[1] for c in carry])


def _sb_attention(zz, vt, b, s, tri, mask):
    nq = s // TQ
    return pl.pallas_call(
        _sb_kernel,
        grid=(b, nq),
        in_specs=_attn_specs(nq, G_SB_Q, G_SB_K, M_SB, s) + [_const_spec((TK, TK)), _const_spec((TK, TQ))],
        out_specs=_out_spec(nq),
        out_shape=jax.ShapeDtypeStruct((b * s, N_HEADS * HEAD_DIM), BF16),
        compiler_params=_cparams(2),
        name="sb_attn",
    )(zz, zz, vt, tri, mask)


def _diff_kernel(q_ref, k_ref, vt_ref, bw_ref, lam_ref, cst_ref, g_ref, o_ref, bt_ref):
    i = pl.program_id(1)
    _fill_bias_tiles(bw_ref, bt_ref)
    lp = lam_ref[...]
    lam_init = cst_ref[:, 0:1]
    lam = (jnp.exp(jnp.sum(lp[0:1] * lp[1:2], axis=-1, keepdims=True))
           - jnp.exp(jnp.sum(lp[2:3] * lp[3:4], axis=-1, keepdims=True)) + lam_init)
    lane = lax.broadcasted_iota(jnp.int32, (TQ, HEAD_DIM), 1)
    qs = []
    for h in range(N_HEADS):
        q = q_ref[h]
        qs.append((jnp.where(lane < DIFF_QK_DIM, q, jnp.zeros_like(q)),
                   jnp.where(lane >= DIFF_QK_DIM, q, jnp.zeros_like(q))))

    def logits(j):
        s_list = []
        for h in range(N_HEADS):
            kj = _k_block(k_ref, h, j)
            bias = bt_ref[h, jnp.minimum(i - j, 2)]
            s_list += [_dot_nt(kj, qs[h][0]) + bias, _dot_nt(kj, qs[h][1]) + bias]
        return s_list

    def values(j):
        return [_vt_block(vt_ref, h, j) for h in range(N_HEADS) for _ in range(2)]

    carry = _softmax_loop(i + 1, logits, values, tuple(_softmax_init() for _ in range(2 * N_HEADS)), groups=(4, 2, 1))
    outs = []
    for h in range(N_HEADS):
        o = _softmax_out(carry[2 * h]) - lam * _softmax_out(carry[2 * h + 1])
        o = o * lax.rsqrt(jnp.mean(o * o, axis=0, keepdims=True) + NORM_EPS) * g_ref[...]
        outs.append(o * (1.0 - lam_init))
    _store_heads(o_ref, outs)


def _diff_attention(zz, vt, b, s, bw, lamp, cst, g):
    nq = s // TQ
    return pl.pallas_call(
        _diff_kernel,
        grid=(b, nq),
        in_specs=_attn_specs(nq, G_DF_Q, G_DF_K, M_DF, s) + [
            _const_spec(BIAS_ROWS_SHAPE), _const_spec((4, DIFF_QK_DIM)), _const_spec((1, 128)),
            _const_spec((HEAD_DIM, 1))],
        out_specs=_out_spec(nq),
        out_shape=jax.ShapeDtypeStruct((b * s, N_HEADS * HEAD_DIM), BF16),
        scratch_shapes=[pltpu.VMEM(BIAS_TILES_SHAPE, F32)],
        compiler_params=_cparams(2),
        name="diff_attn",
    )(zz, zz, vt, bw, lamp, cst, g)


def _dsa_kernel(q_ref, k_ref, vt_ref, qia_ref, qib_ref, ki_ref, wt_ref, bw_ref, tril_ref, o_ref, sc_ref, bt_ref, *,
                topk):
    i = pl.program_id(1)
    _fill_bias_tiles(bw_ref, bt_ref)
    nb = i + 1
    kf = float(topk)
    w = wt_ref[...]
    key = lax.broadcasted_iota(jnp.int32, (TK, TQ), 0)
    qry = lax.broadcasted_iota(jnp.int32, (TK, TQ), 1)

    def score(j):
        kij = _k_block(ki_ref, 0, j)
        sc = jnp.zeros((TK, TQ), F32)
        for hh in range(IDX_HEADS):
            qi = (qia_ref if hh < 4 else qib_ref)[hh % 4]
            sc = sc + w[hh:hh + 1, :] * jnp.maximum(_dot_nt(kij, qi), 0.0)
        return sc

    def extend(lo_src, hi_src, c):
        return (jnp.minimum(c[0], _fold_keys(lo_src, jnp.minimum)), jnp.maximum(c[1], _fold_keys(hi_src, jnp.maximum)))

    def earlier_block(j, c):
        sc = score(j)
        sc_ref[j] = sc
        return extend(sc, sc, c)

    def earlier_pair(t, c):
        sc_a, sc_b = score(2 * t), score(2 * t + 1)
        sc_ref[2 * t] = sc_a
        sc_ref[2 * t + 1] = sc_b
        return extend(sc_b, sc_b, extend(sc_a, sc_a, c))

    lo_part, hi_part = lax.fori_loop(0, i // 2, earlier_pair,
                                     (jnp.full((8, TQ), BIG, F32), jnp.full((8, TQ), -BIG, F32)))
    lo_part, hi_part = lax.fori_loop(2 * (i // 2), i, earlier_block, (lo_part, hi_part))
    sc = score(i)
    causal = key <= qry
    sc_ref[i] = jnp.where(causal, sc, NEG)
    lo_part, hi_part = extend(jnp.where(causal, sc, BIG), jnp.where(causal, sc, NEG), (lo_part, hi_part))

    def reduce_blocks(fn, init):
        def pair(t, c):
            return fn(sc_ref[2 * t + 1], 2 * t + 1, fn(sc_ref[2 * t], 2 * t, c))
        c = lax.fori_loop(0, nb // 2, pair, init)
        return lax.fori_loop(2 * (nb // 2), nb, lambda j, c: fn(sc_ref[j], j, c), c)

    def count_ge(t):
        part = reduce_blocks(lambda x, j, c: c + _fold_keys(jnp.where(x >= t, 1.0, 0.0), jnp.add),
                             jnp.zeros((8, TQ), F32))
        return jnp.sum(part, axis=0, keepdims=True)

    def minmax_blocks(lo_of, hi_of):
        def f(x, j, c):
            return (jnp.minimum(c[0], _fold_keys(lo_of(x), jnp.minimum)),
                    jnp.maximum(c[1], _fold_keys(hi_of(x), jnp.maximum)))
        lo_part, hi_part = reduce_blocks(f, (jnp.full((8, TQ), BIG, F32), jnp.full((8, TQ), -BIG, F32)))
        return jnp.min(lo_part, axis=0, keepdims=True), jnp.max(hi_part, axis=0, keepdims=True)

    n_valid = i * TQ + lax.broadcasted_iota(jnp.int32, (1, TQ), 1) + 1
    take_all = n_valid <= topk
    lo = jnp.min(lo_part, axis=0, keepdims=True)
    hi = jnp.max(hi_part, axis=0, keepdims=True)
    c_max = count_ge(hi)
    at_max = c_max >= kf
    state = (jnp.where(at_max, hi, lo), hi, jnp.where(at_max, c_max, n_valid.astype(F32)), c_max)

    def bisect(_, state):
        lo, hi, c_lo, c_hi = state
        mid = 0.5 * lo + 0.5 * hi
        c = count_ge(mid)
        ge = c >= kf
        return jnp.where(ge, mid, lo), jnp.where(ge, hi, mid), jnp.where(ge, c, c_lo), jnp.where(ge, c_hi, c)

    def unsettled(state):
        lo, hi, c_lo, _ = state
        open_q = jnp.where(take_all, 0.0, jnp.where(c_lo != kf, jnp.where(lo < hi, 1.0, 0.0), 0.0))

        def band_spread():
            b_min, b_max = minmax_blocks(lambda x: jnp.where(x >= lo, jnp.where(x < hi, x, BIG), BIG),
                                         lambda x: jnp.where(x >= lo, jnp.where(x < hi, x, -BIG), -BIG))
            return (jnp.max(jnp.where(b_max != b_min, open_q, 0.0)) > 0.0).astype(jnp.int32)

        return lax.cond(jnp.max(open_q) > 0.0, band_spread, lambda: jnp.int32(0))

    state = lax.fori_loop(0, jnp.where((i + 1) * TQ <= topk, 0, BISECT_WARMUP), bisect, state)

    def trip(c):
        n, state, _ = c
        state = lax.fori_loop(0, BISECT_TRIP, bisect, state)
        return n + 1, state, unsettled(state)

    _, state, _ = lax.while_loop(lambda c: (c[2] > 0) & (c[0] < BISECT_MAX_TRIPS), trip,
                                 (jnp.int32(0), state, unsettled(state)))
    lo, hi, c_lo, c_hi = state
    hi_ok = lo < hi
    c_above = jnp.where(hi_ok, c_hi, 0.0)
    hi_sel = jnp.where(hi_ok, hi, BIG)
    need = jnp.where(take_all, BIG, kf - c_above)
    lo_sel = jnp.where(take_all, HALF_NEG, lo)

    tied = jnp.max(jnp.where(take_all, 0.0, c_lo - kf)) > 0.0

    @pl.when(tied)
    def _():
        tril = tril_ref[...]

        def band_of(j):
            x = sc_ref[j]
            return x, jnp.where(x >= lo_sel, jnp.where(x < hi_sel, 1.0, 0.0), 0.0)

        def write(j, x, band, rank):
            sc_ref[j] = jnp.where(x >= hi_sel, 0.0,
                                  jnp.where(band * rank > 0.0, jnp.where(rank <= need, 0.0, NEG), NEG))
            return rank[TK - 1:TK, :]

        def write_mask(j, taken):
            x, band = band_of(j)
            return write(j, x, band, _dot(tril, band.astype(BF16)) + taken)

        def write_mask_pair(t, taken):
            (x_a, band_a), (x_b, band_b) = band_of(2 * t), band_of(2 * t + 1)
            in_a, in_b = _dot(tril, band_a.astype(BF16)), _dot(tril, band_b.astype(BF16))
            taken = write(2 * t, x_a, band_a, in_a + taken)
            return write(2 * t + 1, x_b, band_b, in_b + taken)

        taken = lax.fori_loop(0, nb // 2, write_mask_pair, jnp.zeros((1, TQ), F32))
        lax.fori_loop(2 * (nb // 2), nb, write_mask, taken)

    @pl.when(jnp.logical_not(tied))
    def _():
        def write_mask(j, _):
            sc_ref[j] = jnp.where(sc_ref[j] >= lo_sel, 0.0, NEG)
            return 0

        lax.fori_loop(0, nb, write_mask, 0)

    def logits(j):
        return [_dot_nt(_k_block(k_ref, h, j), q_ref[h]) + bt_ref[h, jnp.minimum(i - j, 2)] + sc_ref[j]
                for h in range(N_HEADS)]

    carry = _softmax_loop(nb, logits, lambda j: [_vt_block(vt_ref, h, j) for h in range(N_HEADS)],
                          tuple(_softmax_init() for _ in range(N_HEADS)), groups=(4, 2, 1))
    _store_heads(o_ref, [_softmax_out(c) for c in carry])


def _dsa_attention(zz, vt, wt, b, s, bw, tril):
    nq = s // TQ
    topk = min(DSA_TOPK_MAX, s // 4)
    return pl.pallas_call(
        functools.partial(_dsa_kernel, topk=topk),
        grid=(b, nq),
        in_specs=_attn_specs(nq, G_DS_Q, G_DS_K, M_DS, s) + [
            pl.BlockSpec((4, TQ, HEAD_DIM), lambda b_, i: (G_QI_A, b_ * nq + i, 0)),
            pl.BlockSpec((4, TQ, HEAD_DIM), lambda b_, i: (G_QI_B, b_ * nq + i, 0)),
            pl.BlockSpec((1, s, HEAD_DIM), lambda b_, i: (S_KIDX, b_, 0)),
            pl.BlockSpec((IDX_HEADS, TQ), lambda b_, i: (0, b_ * nq + i)),
            _const_spec(BIAS_ROWS_SHAPE), _const_spec((TK, TK))],
        out_specs=_out_spec(nq),
        out_shape=jax.ShapeDtypeStruct((b * s, N_HEADS * HEAD_DIM), BF16),
        scratch_shapes=[pltpu.VMEM((nq, TK, TQ), F32), pltpu.VMEM(BIAS_TILES_SHAPE, F32)],
        compiler_params=_cparams(2),
        name="dsa_attn",
    )(zz, zz, vt, zz, zz, zz, wt, bw, tril)


def _moba_kernel(q_ref, k_ref, vt_ref, bw_ref, o_ref, km_ref, bt_ref, *, nblk, topb):
    i = pl.program_id(1)
    _fill_bias_tiles(bw_ref, bt_ref)
    nrow = km_ref.shape[1]

    @pl.when(i == 0)
    def _():
        km_ref[...] = jnp.zeros_like(km_ref)
        for h in range(N_HEADS):
            for n in range(nblk):
                kb = k_ref[h, n * MOBA_BLOCK:(n + 1) * MOBA_BLOCK, :].astype(F32)
                km_ref[h, n:n + 1, :] = jnp.mean(kb, axis=0, keepdims=True)

    blk = lax.broadcasted_iota(jnp.int32, (nrow, TQ), 0)
    past = blk < i
    head_bits = []
    for h in range(N_HEADS):
        gate = _dot_nt(km_ref[h].astype(BF16), q_ref[h])
        bits = jnp.zeros((1, TQ), F32)
        for n in range(nblk):
            gn = gate[n:n + 1, :]
            beats = jnp.where(past, jnp.where(gate > gn, 1.0, jnp.where(gate == gn, jnp.where(blk < n, 1.0, 0.0), 0.0)), 0.0)
            rank = jnp.sum(beats, axis=0, keepdims=True)
            bits = bits + jnp.where(rank < float(topb), jnp.where(n < i, float(2 ** n), 0.0), 0.0)
        head_bits.append(bits.astype(jnp.int32) | lax.shift_left(jnp.int32(1), i))

    def values(n):
        return [_vt_block(vt_ref, h, n) for h in range(N_HEADS)]

    def logits(n):
        s_list = []
        for h in range(N_HEADS):
            picked = (lax.shift_right_logical(head_bits[h], jnp.full_like(head_bits[h], n)) & 1) == 1
            s_list.append(_dot_nt(_k_block(k_ref, h, n), q_ref[h]) + bt_ref[h, jnp.minimum(i - n, 2)]
                          + jnp.where(picked, 0.0, NEG))
        return s_list

    carry = _softmax_loop(i + 1, logits, values, tuple(_softmax_init() for _ in range(N_HEADS)), groups=(4, 2, 1))
    _store_heads(o_ref, [_softmax_out(c) for c in carry])


def _moba_attention(zz, vt, b, s, bw):
    nq = s // TQ
    nblk = s // MOBA_BLOCK
    topb = min(MOBA_TOPK, nblk - 1)
    return pl.pallas_call(
        functools.partial(_moba_kernel, nblk=nblk, topb=topb),
        grid=(b, nq),
        in_specs=_attn_specs(nq, G_MB_Q, G_MB_K, M_MB, s) + [_const_spec(BIAS_ROWS_SHAPE)],
        out_specs=_out_spec(nq),
        out_shape=jax.ShapeDtypeStruct((b * s, N_HEADS * HEAD_DIM), BF16),
        scratch_shapes=[pltpu.VMEM((N_HEADS, max(8, nblk), HEAD_DIM), F32), pltpu.VMEM(BIAS_TILES_SHAPE, F32)],
        compiler_params=_cparams(2),
        name="moba_attn",
    )(zz, zz, vt, bw)


def _merge_kernel(x_ref, osb_ref, odf_ref, ods_ref, omb_ref, gpre_ref, wg_ref, wbr_ref, wout_ref, gpost_ref, o_ref):
    x = x_ref[...]
    h = _rms(x, gpre_ref[...]).astype(BF16)
    y = jnp.zeros((x.shape[0], D_MODEL), F32)
    for r, o_r in enumerate((osb_ref, odf_ref, ods_ref, omb_ref)):
        gate = jax.nn.sigmoid(_dot(h, wg_ref[:, r * D_MODEL:(r + 1) * D_MODEL]))
        y = y + gate * _dot(o_r[...], wbr_ref[r])
    o_ref[...] = x + _rms(_dot(y.astype(BF16), wout_ref[...]), gpost_ref[...])


def _merge(x, o_sb, o_df, o_ds, o_mb, g_pre, w_gate, w_br, w_out, g_post, layer):
    t = x.shape[0]
    tm = TM_MERGE
    tok = lambda width: pl.BlockSpec((tm, width), lambda i: (i, 0))
    return pl.pallas_call(
        _merge_kernel,
        grid=(t // tm,),
        in_specs=[tok(D_MODEL)] + [tok(MIXER_WIDTH)] * N_MIXERS + [
            _const_spec((1, D_MODEL)), _layer_spec((D_MODEL, N_MIXERS * D_MODEL), layer),
            _layer_spec((N_MIXERS, MIXER_WIDTH, D_MODEL), layer), _layer_spec((D_MODEL, D_MODEL), layer),
            _const_spec((1, D_MODEL))],
        out_specs=tok(D_MODEL),
        out_shape=jax.ShapeDtypeStruct((t, D_MODEL), F32),
        compiler_params=_cparams(1),
        name="merge",
    )(x, o_sb, o_df, o_ds, o_mb, g_pre, w_gate, w_br, w_out, g_post)


def _ffn_kernel(x_ref, gpre_ref, win_ref, wout_ref, gpost_ref, o_ref):
    x = x_ref[...]
    h = _rms(x, gpre_ref[...]).astype(BF16)
    gate = _dot(h, win_ref[:, 0:D_FF])
    up = _dot(h, win_ref[:, D_FF:2 * D_FF])
    act = (gate * jax.nn.sigmoid(gate) * up).astype(BF16)
    o_ref[...] = x + _rms(_dot(act, wout_ref[...]), gpost_ref[...])


def _ffn(x, g_pre, w_in, w_out, g_post, layer):
    t = x.shape[0]
    tm = TM_FFN
    return pl.pallas_call(
        _ffn_kernel,
        grid=(t // tm,),
        in_specs=[pl.BlockSpec((tm, D_MODEL), lambda i: (i, 0)), _const_spec((1, D_MODEL)),
                  _layer_spec((D_MODEL, 2 * D_FF), layer), _layer_spec((D_FF, D_MODEL), layer),
                  _const_spec((1, D_MODEL))],
        out_specs=pl.BlockSpec((tm, D_MODEL), lambda i: (i, 0)),
        out_shape=jax.ShapeDtypeStruct((t, D_MODEL), F32),
        compiler_params=_cparams(1),
        name="ffn",
    )(x, g_pre, w_in, w_out, g_post)


def _t5_bucket(dist):
    max_exact = N_BUCKETS // 2
    d = jnp.maximum(dist, 0)
    log_ratio = jnp.log(jnp.maximum(d, 1).astype(F32) / max_exact) / math.log(MAX_DISTANCE / max_exact)
    large = jnp.minimum(max_exact + (log_ratio * (N_BUCKETS - max_exact)).astype(jnp.int32), N_BUCKETS - 1)
    return jnp.where(d < max_exact, d, large)


def _bias_rows(rel_bias):
    assert TQ == TK
    n = TK
    d = np.arange(-(n - 1), 3 * n + 1)
    by_dist = rel_bias.astype(F32).T[:, _t5_bucket(jnp.asarray(np.maximum(d, 0), jnp.int32))]
    by_dist = jnp.where(jnp.asarray(d >= 0)[None, :], by_dist * LOG2E, NEG)
    return jnp.stack([by_dist[:, o * n:o * n + 2 * n] for o in range(3)], axis=1)[:, :, None, :]


def _pack_runs():
    runs, start = [], 0
    for e in range(1, N_PACK + 1):
        if e == N_PACK or _PACK_SRC[e] != _PACK_SRC[e - 1] + (1 if _PACK_SRC[e - 1] >= 0 else 0):
            runs.append((int(_PACK_SRC[start]), e - start))
            start = e
    return runs


_PACK_RUNS = _pack_runs()
PACK_ROWS = 128


def _pack_kernel(w_ref, pack_ref, gate_ref):
    w = w_ref[...]
    parts = [w[:, a:a + n] if a >= 0 else jnp.zeros((w.shape[0], n), F32) for a, n in _PACK_RUNS]
    pack_ref[...] = jnp.concatenate(parts, axis=1).astype(BF16)
    gate_ref[...] = w[:, _OFF["gate"]:].astype(BF16)


def _pack_weights(w_in):
    depth, d, n_in = w_in.shape
    row_block = lambda width: pl.BlockSpec((pl.Squeezed(), PACK_ROWS, width), lambda l, r: (l, r, 0))
    return pl.pallas_call(
        _pack_kernel,
        grid=(depth, d // PACK_ROWS),
        in_specs=[row_block(n_in)],
        out_specs=[row_block(N_PACK), row_block(N_MIXERS * D_MODEL)],
        out_shape=[jax.ShapeDtypeStruct((depth, d, N_PACK), BF16),
                   jax.ShapeDtypeStruct((depth, d, N_MIXERS * D_MODEL), BF16)],
        compiler_params=_cparams(2),
        name="pack_weights",
    )(w_in)


def kernel(x, w_in, w_br_sb, w_br_diff, w_br_dsa, w_br_moba, w_out, lambda_q1, lambda_k1, lambda_q2, lambda_k2,
           diff_subln_g, rel_bias, w_ffn_in, w_ffn_out, g_pre_mix, g_post_mix, g_pre_ffn, g_post_ffn):
    b, s, d = x.shape
    depth = w_in.shape[0]
    assert d == D_MODEL and s % TQ == 0 and s // MOBA_BLOCK >= 2
    t = b * s

    w_br = jnp.stack([w_br_sb, w_br_diff, w_br_dsa, w_br_moba], axis=1).astype(BF16)
    w_o = w_out.astype(BF16)
    w_f1 = w_ffn_in.astype(BF16)
    w_f2 = w_ffn_out.astype(BF16)
    cs = jnp.asarray(_PACK_SCALE)[None, :]

    bw = _bias_rows(rel_bias)
    bw_df, bw_ds, bw_mb = bw[0:4], bw[4:8], bw[8:12]
    key = np.arange(TK)[:, None]
    qry = np.arange(TQ)[None, :]
    tri = jnp.asarray(key <= np.arange(TK)[None, :], BF16)
    tril = jnp.asarray(key >= np.arange(TK)[None, :], BF16)
    sb_mask = jnp.asarray(np.where(key < qry, 0.0, NEG), F32)

    xf = x.reshape(t, d)
    for l in range(depth):
        lam_init = 0.8 - 0.6 * math.exp(-0.3 * l)
        lamp = jnp.stack([lambda_q1[l], lambda_k1[l], lambda_q2[l], lambda_k2[l]]).astype(F32)
        cst = jnp.full((1, 128), lam_init, F32)
        w_pack, w_gate = _pack_weights(w_in[l:l + 1])
        zz, vt, wt = _proj(xf, g_pre_mix[l][None, :], w_pack, cs, 0)
        o_sb = _sb_attention(zz, vt, b, s, tri, sb_mask)
        o_df = _diff_attention(zz, vt, b, s, bw_df, lamp, cst, diff_subln_g[l][:, None])
        o_ds = _dsa_attention(zz, vt, wt, b, s, bw_ds, tril)
        o_mb = _moba_attention(zz, vt, b, s, bw_mb)
        xf = _merge(xf, o_sb, o_df, o_ds, o_mb, g_pre_mix[l][None, :], w_gate, w_br[l:l + 1], w_o[l:l + 1],
                    g_post_mix[l][None, :], 0)
        xf = _ffn(xf, g_pre_ffn[l][None, :], w_f1, w_f2, g_post_ffn[l][None, :], l)
    return xf.reshape(b, s, d)
```

```python
import functools
import math

import numpy as np
import jax
import jax.numpy as jnp
from jax import lax
from jax.experimental import pallas as pl
from jax.experimental.pallas import tpu as pltpu

F32 = jnp.float32
BF16 = jnp.bfloat16

D_MODEL = 1024
HEAD_DIM = 64
N_HEADS = 4
N_MIXERS = 4
MIXER_WIDTH = N_HEADS * HEAD_DIM
DIFF_QK_DIM = 32
IDX_HEADS = 8
DSA_TOPK_MAX = 256
MOBA_BLOCK = 256
MOBA_TOPK = 3
N_BUCKETS = 32
MAX_DISTANCE = 128
D_FF = 2816
NORM_EPS = 1e-6

TQ = 256
TK = 256
NEG = -1e30
HALF_NEG = -0.5e30
BIG = 3e38
BISECT_WARMUP = 20
BISECT_TRIP = 2
BISECT_MAX_TRIPS = 134
LOG2E = math.log2(math.e)
SB_TAIL_CUTOFF = 120.0 * LOG2E
V_ROWS = 80
TM_PROJ = 512
TM_MERGE = 512
TM_FFN = 512
N_GROUP = 11
N_SLAB = 4 * N_GROUP
N_PACK = (N_GROUP + N_MIXERS) * MIXER_WIDTH
VMEM_LIMIT = 56 * 1024 * 1024

G_SB_Q, G_SB_K, G_DF_Q, G_DF_K, G_DS_Q, G_DS_K, G_QI_A, G_QI_B, G_MB_Q, G_MB_K, G_KIDX = range(N_GROUP)
S_KIDX = 4 * G_KIDX
M_SB, M_DF, M_DS, M_MB = range(N_MIXERS)


def _layout():
    off = {}
    acc = 0
    for name, sz in (("q_sb", 256), ("k_sb", 256), ("v_sb", 256), ("q1", 128), ("q2", 128), ("k1", 128),
                     ("k2", 128), ("v_df", 256), ("q_ds", 256), ("k_ds", 256), ("v_ds", 256), ("qi", 512),
                     ("ki", 64), ("wi", 8), ("q_mb", 256), ("k_mb", 256), ("v_mb", 256), ("gate", 4096)):
        off[name] = acc
        acc += sz
    return off


_OFF = _layout()


def _pack_layout():
    off = _OFF
    cols, scale = [], []

    def add(start, n, s=1.0):
        cols.extend(range(start, start + n))
        scale.extend([s] * n)

    hd = HEAD_DIM ** -0.5
    hd2 = hd * LOG2E
    df2 = DIFF_QK_DIM ** -0.5 * LOG2E
    add(off["q_sb"], 256, hd2); add(off["k_sb"], 256)
    for h in range(N_HEADS):
        add(off["q1"] + h * 32, 32, df2); add(off["q2"] + h * 32, 32, df2)
    for h in range(N_HEADS):
        add(off["k1"] + h * 32, 32); add(off["k2"] + h * 32, 32)
    add(off["q_ds"], 256, hd2); add(off["k_ds"], 256)
    add(off["qi"], 512, HEAD_DIM ** -0.5)
    add(off["q_mb"], 256, hd2); add(off["k_mb"], 256)
    add(off["ki"], 64); add(off["wi"], IDX_HEADS, IDX_HEADS ** -0.5)
    cols.extend([-1] * 184); scale.extend([1.0] * 184)
    for name in ("v_sb", "v_df", "v_ds", "v_mb"):
        add(off[name], MIXER_WIDTH)
    assert len(cols) == N_PACK
    return np.asarray(cols, np.int32), np.asarray(scale, np.float32)


_PACK_SRC, _PACK_SCALE = _pack_layout()


def _dot(a, b):
    return jnp.dot(a, b, preferred_element_type=F32)


def _dot_nt(a, b):
    return lax.dot_general(a, b, (((1,), (1,)), ((), ())), preferred_element_type=F32)


def _rms(x, g):
    return x * lax.rsqrt(jnp.mean(x * x, axis=-1, keepdims=True) + NORM_EPS) * g


def _cparams(n_axes):
    return pltpu.CompilerParams(dimension_semantics=("arbitrary",) * n_axes, vmem_limit_bytes=VMEM_LIMIT)


def _const_spec(shape):
    nd = len(shape)
    return pl.BlockSpec(shape, lambda *_: (0,) * nd, pipeline_mode=pl.Buffered(1))


def _layer_spec(shape, layer):
    nd = len(shape)
    return pl.BlockSpec((pl.Squeezed(),) + tuple(shape), lambda *_: (layer,) + (0,) * nd, pipeline_mode=pl.Buffered(1))


def _proj_kernel(x_ref, g_ref, w_ref, cs_ref, zz_ref, vt_ref, wt_ref):
    h = _rms(x_ref[...], g_ref[...]).astype(BF16)

    def group(c):
        cols = slice(c * MIXER_WIDTH, (c + 1) * MIXER_WIDTH)
        return _dot(h, w_ref[:, cols]) * cs_ref[:, cols]

    for c in range(N_GROUP):
        r = group(c)
        for s in range(4):
            zz_ref[4 * c + s] = r[:, s * HEAD_DIM:(s + 1) * HEAD_DIM].astype(BF16)
        if c == G_KIDX:
            wt_ref[...] = jnp.transpose(r)[HEAD_DIM:HEAD_DIM + IDX_HEADS]
    for m in range(N_MIXERS):
        v = group(N_GROUP + m)
        for u in range(TM_PROJ // TK):
            vt = jnp.transpose(v[u * TK:(u + 1) * TK]).astype(BF16)
            for hh in range(N_HEADS):
                vt_ref[m, u, hh * V_ROWS:hh * V_ROWS + HEAD_DIM, :] = vt[hh * HEAD_DIM:(hh + 1) * HEAD_DIM]
                vt_ref[m, u, hh * V_ROWS + HEAD_DIM:(hh + 1) * V_ROWS, :] = jnp.ones((V_ROWS - HEAD_DIM, TK), BF16)


def _proj(x, g, w, cs, layer):
    t = x.shape[0]
    tm = TM_PROJ
    return pl.pallas_call(
        _proj_kernel,
        grid=(t // tm,),
        in_specs=[pl.BlockSpec((tm, D_MODEL), lambda i: (i, 0)),
                  _const_spec((1, D_MODEL)),
                  _layer_spec((D_MODEL, N_PACK), layer),
                  _const_spec((1, N_PACK))],
        out_specs=[pl.BlockSpec((N_SLAB, tm, HEAD_DIM), lambda i: (0, i, 0)),
                   pl.BlockSpec((N_MIXERS, tm // TK, N_HEADS * V_ROWS, TK), lambda i: (0, i, 0, 0)),
                   pl.BlockSpec((IDX_HEADS, tm), lambda i: (0, i))],
        out_shape=[jax.ShapeDtypeStruct((N_SLAB, t, HEAD_DIM), BF16),
                   jax.ShapeDtypeStruct((N_MIXERS, t // TK, N_HEADS * V_ROWS, TK), BF16),
                   jax.ShapeDtypeStruct((IDX_HEADS, t), F32)],
        compiler_params=_cparams(1),
        name="proj",
    )(x, g, w, cs)


def _k_block(ref, h, j):
    return ref[h, pl.ds(pl.multiple_of(j * TK, TK), TK), :]


def _fold_keys(a, op):
    n = a.shape[0]
    while n > 8:
        n //= 2
        a = op(a[:n], a[n:2 * n])
    return a


def _vt_block(ref, h, j, rows=V_ROWS):
    return ref[0, j, h * V_ROWS:h * V_ROWS + rows, :]


def _softmax_block(s_list, vt_list, carry):
    ms = [jnp.maximum(c[0], jnp.max(_fold_keys(s, jnp.maximum), axis=0, keepdims=True))
          for s, c in zip(s_list, carry)]
    pvs = [_dot(vt, jnp.exp2(s - m).astype(BF16)) for vt, s, m in zip(vt_list, s_list, ms)]
    return tuple((m_new, jnp.exp2(m - m_new) * acc + pv) for (m, acc), m_new, pv in zip(carry, ms, pvs))


def _softmax_loop(n_blocks, logits, values, carry, groups=(2, 1)):
    assert groups[-1] == 1
    start = 0
    for g in groups:
        def body(t, carry, g=g, start=start):
            j = start + g * t
            s_all = [logits(j + u) for u in range(g)]
            for u in range(g):
                carry = _softmax_block(s_all[u], values(j + u), carry)
            return carry

        trips = (n_blocks - start) // g
        carry = lax.fori_loop(0, trips, body, carry)
        start = start + g * trips
    return carry


def _softmax_init():
    return (jnp.full((1, TQ), NEG, F32), jnp.zeros((V_ROWS, TQ), F32))


def _softmax_out(carry):
    _, acc = carry
    return acc[:HEAD_DIM] / acc[HEAD_DIM:HEAD_DIM + 1]


def _store_heads(o_ref, heads_t):
    o_ref[...] = jnp.transpose(jnp.concatenate(heads_t, axis=0)).astype(BF16)


BIAS_ROWS_SHAPE = (N_HEADS, 3, 1, 2 * TQ)
BIAS_TILES_SHAPE = (N_HEADS, 3, TK, TQ)


def _fill_bias_tiles(bw_ref, bt_ref):
    @pl.when((pl.program_id(0) == 0) & (pl.program_id(1) == 0))
    def _():
        for h in range(N_HEADS):
            for o in range(3):
                rows = jnp.broadcast_to(bw_ref[h, o], (TK, 2 * TQ))
                bt_ref[h, o] = pltpu.roll(rows, TQ + 1, 1, stride=1, stride_axis=0)[:, :TQ]


def _attn_specs(nq, gq, gk, mixer, s):
    return [pl.BlockSpec((4, TQ, HEAD_DIM), lambda b, i: (gq, b * nq + i, 0)),
            pl.BlockSpec((4, s, HEAD_DIM), lambda b, i: (gk, b, 0)),
            pl.BlockSpec((1, s // TK, N_HEADS * V_ROWS, TK), lambda b, i: (mixer, b, 0, 0))]


def _out_spec(nq):
    return pl.BlockSpec((TQ, N_HEADS * HEAD_DIM), lambda b, i: (b * nq + i, 0))


def _sb_kernel(q_ref, k_ref, vt_ref, tri_ref, mask_ref, o_ref):
    i = pl.program_id(1)
    tri = tri_ref[...]

    def blocks(js, carry, first_masked):
        heads = range(N_HEADS)
        zs = [[_dot_nt(_k_block(k_ref, h, j), q_ref[h]) for h in heads] for j in js]
        if first_masked:
            zs[0] = [z + mask_ref[...] for z in zs[0]]
        sps = [[jnp.maximum(z, 0.0) + jnp.log2(1.0 + jnp.exp2(-jnp.abs(z))) for z in zb] for zb in zs]
        his = [[sp.astype(BF16) for sp in sb] for sb in sps]
        los = [[(sp - hi.astype(F32)).astype(BF16) for sp, hi in zip(sb, hb)] for sb, hb in zip(sps, his)]
        cums = [[_dot(tri, hi) + _dot(tri, lo) for hi, lo in zip(hb, lb)] for hb, lb in zip(his, los)]
        out = []
        for h in heads:
            tail, acc = carry[h]
            for u, j in enumerate(js):
                c = cums[u][h] + tail
                acc = acc + _dot(_vt_block(vt_ref, h, j, HEAD_DIM), jnp.exp2(zs[u][h] - c).astype(BF16))
                tail = c[0:1, :]
            out.append((tail, acc))
        return tuple(out)

    init = tuple((jnp.zeros((1, TQ), F32), jnp.zeros((HEAD_DIM, TQ), F32)) for _ in range(N_HEADS))
    carry = lax.cond(i >= 1, lambda c: blocks([i, i - 1], c, True), lambda c: blocks([i], c, True), init)

    def weights_left(carry):
        tail = functools.reduce(jnp.minimum, [c[0] for c in carry])
        return (jnp.min(tail) < SB_TAIL_CUTOFF).astype(jnp.int32)

    def earlier_block(c):
        j, carry, _ = c
        carry = blocks([j], carry, False)
        return j - 1, carry, weights_left(carry)

    _, carry, _ = lax.while_loop(lambda c: (c[0] >= 0) & (c[2] > 0), earlier_block, (i - 2, carry, weights_left(carry)))
    _store_heads(o_ref, [c[1] for c in carry])


def _sb_attention(zz, vt, b, s, tri, mask):
    nq = s // TQ
    return pl.pallas_call(
        _sb_kernel,
        grid=(b, nq),
        in_specs=_attn_specs(nq, G_SB_Q, G_SB_K, M_SB, s) + [_const_spec((TK, TK)), _const_spec((TK, TQ))],
        out_specs=_out_spec(nq),
        out_shape=jax.ShapeDtypeStruct((b * s, N_HEADS * HEAD_DIM), BF16),
        compiler_params=_cparams(2),
        name="sb_attn",
    )(zz, zz, vt, tri, mask)


def _diff_kernel(q_ref, k_ref, vt_ref, bw_ref, lam_ref, cst_ref, g_ref, o_ref, bt_ref):
    i = pl.program_id(1)
    _fill_bias_tiles(bw_ref, bt_ref)
    lp = lam_ref[...]
    lam_init = cst_ref[:, 0:1]
    lam = (jnp.exp(jnp.sum(lp[0:1] * lp[1:2], axis=-1, keepdims=True))
           - jnp.exp(jnp.sum(lp[2:3] * lp[3:4], axis=-1, keepdims=True)) + lam_init)
    lane = lax.broadcasted_iota(jnp.int32, (TQ, HEAD_DIM), 1)
    qs = []
    for h in range(N_HEADS):
        q = q_ref[h]
        qs.append((jnp.where(lane < DIFF_QK_DIM, q, jnp.zeros_like(q)),
                   jnp.where(lane >= DIFF_QK_DIM, q, jnp.zeros_like(q))))

    def logits(j):
        s_list = []
        for h in range(N_HEADS):
            kj = _k_block(k_ref, h, j)
            bias = bt_ref[h, jnp.minimum(i - j, 2)]
            s_list += [_dot_nt(kj, qs[h][0]) + bias, _dot_nt(kj, qs[h][1]) + bias]
        return s_list

    def values(j):
        return [_vt_block(vt_ref, h, j) for h in range(N_HEADS) for _ in range(2)]

    carry = _softmax_loop(i + 1, logits, values, tuple(_softmax_init() for _ in range(2 * N_HEADS)), groups=(4, 2, 1))
    outs = []
    for h in range(N_HEADS):
        o = _softmax_out(carry[2 * h]) - lam * _softmax_out(carry[2 * h + 1])
        o = o * lax.rsqrt(jnp.mean(o * o, axis=0, keepdims=True) + NORM_EPS) * g_ref[...]
        outs.append(o * (1.0 - lam_init))
    _store_heads(o_ref, outs)


def _diff_attention(zz, vt, b, s, bw, lamp, cst, g):
    nq = s // TQ
    return pl.pallas_call(
        _diff_kernel,
        grid=(b, nq),
        in_specs=_attn_specs(nq, G_DF_Q, G_DF_K, M_DF, s) + [
            _const_spec(BIAS_ROWS_SHAPE), _const_spec((4, DIFF_QK_DIM)), _const_spec((1, 128)),
            _const_spec((HEAD_DIM, 1))],
        out_specs=_out_spec(nq),
        out_shape=jax.ShapeDtypeStruct((b * s, N_HEADS * HEAD_DIM), BF16),
        scratch_shapes=[pltpu.VMEM(BIAS_TILES_SHAPE, F32)],
        compiler_params=_cparams(2),
        name="diff_attn",
    )(zz, zz, vt, bw, lamp, cst, g)


def _dsa_kernel(q_ref, k_ref, vt_ref, qia_ref, qib_ref, ki_ref, wt_ref, bw_ref, tril_ref, o_ref, sc_ref, bt_ref, *,
                topk):
    i = pl.program_id(1)
    _fill_bias_tiles(bw_ref, bt_ref)
    nb = i + 1
    kf = float(topk)
    w = wt_ref[...]
    key = lax.broadcasted_iota(jnp.int32, (TK, TQ), 0)
    qry = lax.broadcasted_iota(jnp.int32, (TK, TQ), 1)

    def score(j):
        kij = _k_block(ki_ref, 0, j)
        sc = jnp.zeros((TK, TQ), F32)
        for hh in range(IDX_HEADS):
            qi = (qia_ref if hh < 4 else qib_ref)[hh % 4]
            sc = sc + w[hh:hh + 1, :] * jnp.maximum(_dot_nt(kij, qi), 0.0)
        return sc

    def extend(lo_src, hi_src, c):
        return (jnp.minimum(c[0], _fold_keys(lo_src, jnp.minimum)), jnp.maximum(c[1], _fold_keys(hi_src, jnp.maximum)))

    def earlier_block(j, c):
        sc = score(j)
        sc_ref[j] = sc
        return extend(sc, sc, c)

    def earlier_pair(t, c):
        sc_a, sc_b = score(2 * t), score(2 * t + 1)
        sc_ref[2 * t] = sc_a
        sc_ref[2 * t + 1] = sc_b
        return extend(sc_b, sc_b, extend(sc_a, sc_a, c))

    lo_part, hi_part = lax.fori_loop(0, i // 2, earlier_pair,
                                     (jnp.full((8, TQ), BIG, F32), jnp.full((8, TQ), -BIG, F32)))
    lo_part, hi_part = lax.fori_loop(2 * (i // 2), i, earlier_block, (lo_part, hi_part))
    sc = score(i)
    causal = key <= qry
    sc_ref[i] = jnp.where(causal, sc, NEG)
    lo_part, hi_part = extend(jnp.where(causal, sc, BIG), jnp.where(causal, sc, NEG), (lo_part, hi_part))

    def reduce_blocks(fn, init):
        def pair(t, c):
            return fn(sc_ref[2 * t + 1], 2 * t + 1, fn(sc_ref[2 * t], 2 * t, c))
        c = lax.fori_loop(0, nb // 2, pair, init)
        return lax.fori_loop(2 * (nb // 2), nb, lambda j, c: fn(sc_ref[j], j, c), c)

    def count_ge(t):
        part = reduce_blocks(lambda x, j, c: c + _fold_keys(jnp.where(x >= t, 1.0, 0.0), jnp.add),
                             jnp.zeros((8, TQ), F32))
        return jnp.sum(part, axis=0, keepdims=True)

    def minmax_blocks(lo_of, hi_of):
        def f(x, j, c):
            return (jnp.minimum(c[0], _fold_keys(lo_of(x), jnp.minimum)),
                    jnp.maximum(c[1], _fold_keys(hi_of(x), jnp.maximum)))
        lo_part, hi_part = reduce_blocks(f, (jnp.full((8, TQ), BIG, F32), jnp.full((8, TQ), -BIG, F32)))
        return jnp.min(lo_part, axis=0, keepdims=True), jnp.max(hi_part, axis=0, keepdims=True)

    n_valid = i * TQ + lax.broadcasted_iota(jnp.int32, (1, TQ), 1) + 1
    take_all = n_valid <= topk
    lo = jnp.min(lo_part, axis=0, keepdims=True)
    hi = jnp.max(hi_part, axis=0, keepdims=True)
    c_max = count_ge(hi)
    at_max = c_max >= kf
    state = (jnp.where(at_max, hi, lo), hi, jnp.where(at_max, c_max, n_valid.astype(F32)), c_max)

    def bisect(_, state):
        lo, hi, c_lo, c_hi = state
        mid = 0.5 * lo + 0.5 * hi
        c = count_ge(mid)
        ge = c >= kf
        return jnp.where(ge, mid, lo), jnp.where(ge, hi, mid), jnp.where(ge, c, c_lo), jnp.where(ge, c_hi, c)

    def unsettled(state):
        lo, hi, c_lo, _ = state
        open_q = jnp.where(take_all, 0.0, jnp.where(c_lo != kf, jnp.where(lo < hi, 1.0, 0.0), 0.0))

        def band_spread():
            b_min, b_max = minmax_blocks(lambda x: jnp.where(x >= lo, jnp.where(x < hi, x, BIG), BIG),
                                         lambda x: jnp.where(x >= lo, jnp.where(x < hi, x, -BIG), -BIG))
            return (jnp.max(jnp.where(b_max != b_min, open_q, 0.0)) > 0.0).astype(jnp.int32)

        return lax.cond(jnp.max(open_q) > 0.0, band_spread, lambda: jnp.int32(0))

    state = lax.fori_loop(0, jnp.where((i + 1) * TQ <= topk, 0, BISECT_WARMUP), bisect, state)

    def trip(c):
        n, state, _ = c
        state = lax.fori_loop(0, BISECT_TRIP, bisect, state)
        return n + 1, state, unsettled(state)

    _, state, _ = lax.while_loop(lambda c: (c[2] > 0) & (c[0] < BISECT_MAX_TRIPS), trip,
                                 (jnp.int32(0), state, unsettled(state)))
    lo, hi, c_lo, c_hi = state
    hi_ok = lo < hi
    c_above = jnp.where(hi_ok, c_hi, 0.0)
    hi_sel = jnp.where(hi_ok, hi, BIG)
    need = jnp.where(take_all, BIG, kf - c_above)
    lo_sel = jnp.where(take_all, HALF_NEG, lo)

    tied = jnp.max(jnp.where(take_all, 0.0, c_lo - kf)) > 0.0

    @pl.when(tied)
    def _():
        tril = tril_ref[...]

        def band_of(j):
            x = sc_ref[j]
            return x, jnp.where(x >= lo_sel, jnp.where(x < hi_sel, 1.0, 0.0), 0.0)

        def write(j, x, band, rank):
            sc_ref[j] = jnp.where(x >= hi_sel, 0.0,
                                  jnp.where(band * rank > 0.0, jnp.where(rank <= need, 0.0, NEG), NEG))
            return rank[TK - 1:TK, :]

        def write_mask(j, taken):
            x, band = band_of(j)
            return write(j, x, band, _dot(tril, band.astype(BF16)) + taken)

        def write_mask_pair(t, taken):
            (x_a, band_a), (x_b, band_b) = band_of(2 * t), band_of(2 * t + 1)
            in_a, in_b = _dot(tril, band_a.astype(BF16)), _dot(tril, band_b.astype(BF16))
            taken = write(2 * t, x_a, band_a, in_a + taken)
            return write(2 * t + 1, x_b, band_b, in_b + taken)

        taken = lax.fori_loop(0, nb // 2, write_mask_pair, jnp.zeros((1, TQ), F32))
        lax.fori_loop(2 * (nb // 2), nb, write_mask, taken)

    @pl.when(jnp.logical_not(tied))
    def _():
        def write_mask(j, _):
            sc_ref[j] = jnp.where(sc_ref[j] >= lo_sel, 0.0, NEG)
            return 0

        lax.fori_loop(0, nb, write_mask, 0)

    def logits(j):
        return [_dot_nt(_k_block(k_ref, h, j), q_ref[h]) + bt_ref[h, jnp.minimum(i - j, 2)] + sc_ref[j]
                for h in range(N_HEADS)]

    carry = _softmax_loop(nb, logits, lambda j: [_vt_block(vt_ref, h, j) for h in range(N_HEADS)],
                          tuple(_softmax_init() for _ in range(N_HEADS)), groups=(4, 2, 1))
    _store_heads(o_ref, [_softmax_out(c) for c in carry])


def _dsa_attention(zz, vt, wt, b, s, bw, tril):
    nq = s // TQ
    topk = min(DSA_TOPK_MAX, s // 4)
    return pl.pallas_call(
        functools.partial(_dsa_kernel, topk=topk),
        grid=(b, nq),
        in_specs=_attn_specs(nq, G_DS_Q, G_DS_K, M_DS, s) + [
            pl.BlockSpec((4, TQ, HEAD_DIM), lambda b_, i: (G_QI_A, b_ * nq + i, 0)),
            pl.BlockSpec((4, TQ, HEAD_DIM), lambda b_, i: (G_QI_B, b_ * nq + i, 0)),
            pl.BlockSpec((1, s, HEAD_DIM), lambda b_, i: (S_KIDX, b_, 0)),
            pl.BlockSpec((IDX_HEADS, TQ), lambda b_, i: (0, b_ * nq + i)),
            _const_spec(BIAS_ROWS_SHAPE), _const_spec((TK, TK))],
        out_specs=_out_spec(nq),
        out_shape=jax.ShapeDtypeStruct((b * s, N_HEADS * HEAD_DIM), BF16),
        scratch_shapes=[pltpu.VMEM((nq, TK, TQ), F32), pltpu.VMEM(BIAS_TILES_SHAPE, F32)],
        compiler_params=_cparams(2),
        name="dsa_attn",
    )(zz, zz, vt, zz, zz, zz, wt, bw, tril)


def _moba_kernel(q_ref, k_ref, vt_ref, bw_ref, o_ref, km_ref, bt_ref, *, nblk, topb):
    i = pl.program_id(1)
    _fill_bias_tiles(bw_ref, bt_ref)
    nrow = km_ref.shape[1]

    @pl.when(i == 0)
    def _():
        km_ref[...] = jnp.zeros_like(km_ref)
        for h in range(N_HEADS):
            for n in range(nblk):
                kb = k_ref[h, n * MOBA_BLOCK:(n + 1) * MOBA_BLOCK, :].astype(F32)
                km_ref[h, n:n + 1, :] = jnp.mean(kb, axis=0, keepdims=True)

    blk = lax.broadcasted_iota(jnp.int32, (nrow, TQ), 0)
    past = blk < i
    head_bits = []
    for h in range(N_HEADS):
        gate = _dot_nt(km_ref[h].astype(BF16), q_ref[h])
        bits = jnp.zeros((1, TQ), F32)
        for n in range(nblk):
            gn = gate[n:n + 1, :]
            beats = jnp.where(past, jnp.where(gate > gn, 1.0, jnp.where(gate == gn, jnp.where(blk < n, 1.0, 0.0), 0.0)), 0.0)
            rank = jnp.sum(beats, axis=0, keepdims=True)
            bits = bits + jnp.where(rank < float(topb), jnp.where(n < i, float(2 ** n), 0.0), 0.0)
        head_bits.append(bits.astype(jnp.int32) | lax.shift_left(jnp.int32(1), i))

    def values(n):
        return [_vt_block(vt_ref, h, n) for h in range(N_HEADS)]

    def logits(n):
        s_list = []
        for h in range(N_HEADS):
            picked = (lax.shift_right_logical(head_bits[h], jnp.full_like(head_bits[h], n)) & 1) == 1
            s_list.append(_dot_nt(_k_block(k_ref, h, n), q_ref[h]) + bt_ref[h, jnp.minimum(i - n, 2)]
                          + jnp.where(picked, 0.0, NEG))
        return s_list

    carry = _softmax_loop(i + 1, logits, values, tuple(_softmax_init() for _ in range(N_HEADS)), groups=(4, 2, 1))
    _store_heads(o_ref, [_softmax_out(c) for c in carry])


def _moba_attention(zz, vt, b, s, bw):
    nq = s // TQ
    nblk = s // MOBA_BLOCK
    topb = min(MOBA_TOPK, nblk - 1)
    return pl.pallas_call(
        functools.partial(_moba_kernel, nblk=nblk, topb=topb),
        grid=(b, nq),
        in_specs=_attn_specs(nq, G_MB_Q, G_MB_K, M_MB, s) + [_const_spec(BIAS_ROWS_SHAPE)],
        out_specs=_out_spec(nq),
        out_shape=jax.ShapeDtypeStruct((b * s, N_HEADS * HEAD_DIM), BF16),
        scratch_shapes=[pltpu.VMEM((N_HEADS, max(8, nblk), HEAD_DIM), F32), pltpu.VMEM(BIAS_TILES_SHAPE, F32)],
        compiler_params=_cparams(2),
        name="moba_attn",
    )(zz, zz, vt, bw)


def _merge_kernel(x_ref, osb_ref, odf_ref, ods_ref, omb_ref, gpre_ref, wg_ref, wbr_ref, wout_ref, gpost_ref, o_ref):
    x = x_ref[...]
    h = _rms(x, gpre_ref[...]).astype(BF16)
    y = jnp.zeros((x.shape[0], D_MODEL), F32)
    for r, o_r in enumerate((osb_ref, odf_ref, ods_ref, omb_ref)):
        gate = jax.nn.sigmoid(_dot(h, wg_ref[:, r * D_MODEL:(r + 1) * D_MODEL]))
        y = y + gate * _dot(o_r[...], wbr_ref[r])
    o_ref[...] = x + _rms(_dot(y.astype(BF16), wout_ref[...]), gpost_ref[...])


def _merge(x, o_sb, o_df, o_ds, o_mb, g_pre, w_gate, w_br, w_out, g_post, layer):
    t = x.shape[0]
    tm = TM_MERGE
    tok = lambda width: pl.BlockSpec((tm, width), lambda i: (i, 0))
    return pl.pallas_call(
        _merge_kernel,
        grid=(t // tm,),
        in_specs=[tok(D_MODEL)] + [tok(MIXER_WIDTH)] * N_MIXERS + [
            _const_spec((1, D_MODEL)), _layer_spec((D_MODEL, N_MIXERS * D_MODEL), layer),
            _layer_spec((N_MIXERS, MIXER_WIDTH, D_MODEL), layer), _layer_spec((D_MODEL, D_MODEL), layer),
            _const_spec((1, D_MODEL))],
        out_specs=tok(D_MODEL),
        out_shape=jax.ShapeDtypeStruct((t, D_MODEL), F32),
        compiler_params=_cparams(1),
        name="merge",
    )(x, o_sb, o_df, o_ds, o_mb, g_pre, w_gate, w_br, w_out, g_post)


def _ffn_kernel(x_ref, gpre_ref, win_ref, wout_ref, gpost_ref, o_ref):
    x = x_ref[...]
    h = _rms(x, gpre_ref[...]).astype(BF16)
    gate = _dot(h, win_ref[:, 0:D_FF])
    up = _dot(h, win_ref[:, D_FF:2 * D_FF])
    act = (gate * jax.nn.sigmoid(gate) * up).astype(BF16)
    o_ref[...] = x + _rms(_dot(act, wout_ref[...]), gpost_ref[...])


def _ffn(x, g_pre, w_in, w_out, g_post, layer):
    t = x.shape[0]
    tm = TM_FFN
    return pl.pallas_call(
        _ffn_kernel,
        grid=(t // tm,),
        in_specs=[pl.BlockSpec((tm, D_MODEL), lambda i: (i, 0)), _const_spec((1, D_MODEL)),
                  _layer_spec((D_MODEL, 2 * D_FF), layer), _layer_spec((D_FF, D_MODEL), layer),
                  _const_spec((1, D_MODEL))],
        out_specs=pl.BlockSpec((tm, D_MODEL), lambda i: (i, 0)),
        out_shape=jax.ShapeDtypeStruct((t, D_MODEL), F32),
        compiler_params=_cparams(1),
        name="ffn",
    )(x, g_pre, w_in, w_out, g_post)


def _t5_bucket(dist):
    max_exact = N_BUCKETS // 2
    d = jnp.maximum(dist, 0)
    log_ratio = jnp.log(jnp.maximum(d, 1).astype(F32) / max_exact) / math.log(MAX_DISTANCE / max_exact)
    large = jnp.minimum(max_exact + (log_ratio * (N_BUCKETS - max_exact)).astype(jnp.int32), N_BUCKETS - 1)
    return jnp.where(d < max_exact, d, large)


def _bias_rows(rel_bias):
    assert TQ == TK
    n = TK
    d = np.arange(-(n - 1), 3 * n + 1)
    by_dist = rel_bias.astype(F32).T[:, _t5_bucket(jnp.asarray(np.maximum(d, 0), jnp.int32))]
    by_dist = jnp.where(jnp.asarray(d >= 0)[None, :], by_dist * LOG2E, NEG)
    return jnp.stack([by_dist[:, o * n:o * n + 2 * n] for o in range(3)], axis=1)[:, :, None, :]


def _pack_runs():
    runs, start = [], 0
    for e in range(1, N_PACK + 1):
        if e == N_PACK or _PACK_SRC[e] != _PACK_SRC[e - 1] + (1 if _PACK_SRC[e - 1] >= 0 else 0):
            runs.append((int(_PACK_SRC[start]), e - start))
            start = e
    return runs


_PACK_RUNS = _pack_runs()
PACK_ROWS = 128


def _pack_kernel(w_ref, pack_ref, gate_ref):
    w = w_ref[...]
    parts = [w[:, a:a + n] if a >= 0 else jnp.zeros((w.shape[0], n), F32) for a, n in _PACK_RUNS]
    pack_ref[...] = jnp.concatenate(parts, axis=1).astype(BF16)
    gate_ref[...] = w[:, _OFF["gate"]:].astype(BF16)


def _pack_weights(w_in):
    depth, d, n_in = w_in.shape
    row_block = lambda width: pl.BlockSpec((pl.Squeezed(), PACK_ROWS, width), lambda l, r: (l, r, 0))
    return pl.pallas_call(
        _pack_kernel,
        grid=(depth, d // PACK_ROWS),
        in_specs=[row_block(n_in)],
        out_specs=[row_block(N_PACK), row_block(N_MIXERS * D_MODEL)],
        out_shape=[jax.ShapeDtypeStruct((depth, d, N_PACK), BF16),
                   jax.ShapeDtypeStruct((depth, d, N_MIXERS * D_MODEL), BF16)],
        compiler_params=_cparams(2),
        name="pack_weights",
    )(w_in)


def kernel(x, w_in, w_br_sb, w_br_diff, w_br_dsa, w_br_moba, w_out, lambda_q1, lambda_k1, lambda_q2, lambda_k2,
           diff_subln_g, rel_bias, w_ffn_in, w_ffn_out, g_pre_mix, g_post_mix, g_pre_ffn, g_post_ffn):
    b, s, d = x.shape
    depth = w_in.shape[0]
    assert d == D_MODEL and s % TQ == 0 and s // MOBA_BLOCK >= 2
    t = b * s

    w_pack, w_gate = _pack_weights(w_in)
    w_br = jnp.stack([w_br_sb, w_br_diff, w_br_dsa, w_br_moba], axis=1).astype(BF16)
    w_o = w_out.astype(BF16)
    w_f1 = w_ffn_in.astype(BF16)
    w_f2 = w_ffn_out.astype(BF16)
    cs = jnp.asarray(_PACK_SCALE)[None, :]

    bw = _bias_rows(rel_bias)
    bw_df, bw_ds, bw_mb = bw[0:4], bw[4:8], bw[8:12]
    key = np.arange(TK)[:, None]
    qry = np.arange(TQ)[None, :]
    tri = jnp.asarray(key <= np.arange(TK)[None, :], BF16)
    tril = jnp.asarray(key >= np.arange(TK)[None, :], BF16)
    sb_mask = jnp.asarray(np.where(key < qry, 0.0, NEG), F32)

    xf = x.reshape(t, d)
    for l in range(depth):
        lam_init = 0.8 - 0.6 * math.exp(-0.3 * l)
        lamp = jnp.stack([lambda_q1[l], lambda_k1[l], lambda_q2[l], lambda_k2[l]]).astype(F32)
        cst = jnp.full((1, 128), lam_init, F32)
        zz, vt, wt = _proj(xf, g_pre_mix[l][None, :], w_pack, cs, l)
        o_sb = _sb_attention(zz, vt, b, s, tri, sb_mask)
        o_df = _diff_attention(zz, vt, b, s, bw_df, lamp, cst, diff_subln_g[l][:, None])
        o_ds = _dsa_attention(zz, vt, wt, b, s, bw_ds, tril)
        o_mb = _moba_attention(zz, vt, b, s, bw_mb)
        xf = _merge(xf, o_sb, o_df, o_ds, o_mb, g_pre_mix[l][None, :], w_gate, w_br, w_o, g_post_mix[l][None, :], l)
        xf = _ffn(xf, g_pre_ffn[l][None, :], w_f1, w_f2, g_post_ffn[l][None, :], l)
    return xf.reshape(b, s, d)
```

```python
import functools
import math

import numpy as np
import jax
import jax.numpy as jnp
from jax import lax
from jax.experimental import pallas as pl
from jax.experimental.pallas import tpu as pltpu

F32 = jnp.float32
BF16 = jnp.bfloat16

D_MODEL = 1024
HEAD_DIM = 64
N_HEADS = 4
N_MIXERS = 4
MIXER_WIDTH = N_HEADS * HEAD_DIM
DIFF_QK_DIM = 32
IDX_HEADS = 8
DSA_TOPK_MAX = 256
MOBA_BLOCK = 256
MOBA_TOPK = 3
N_BUCKETS = 32
MAX_DISTANCE = 128
D_FF = 2816
NORM_EPS = 1e-6

TQ = 256
TK = 256
NEG = -1e30
HALF_NEG = -0.5e30
BIG = 3e38
BISECT_WARMUP = 20
BISECT_TRIP = 2
BISECT_MAX_TRIPS = 134
LOG2E = math.log2(math.e)
SB_TAIL_CUTOFF = 120.0 * LOG2E
V_ROWS = 80
TM_PROJ = 512
TM_MERGE = 512
TM_FFN = 512
N_GROUP = 11
N_SLAB = 4 * N_GROUP
N_PACK = (N_GROUP + N_MIXERS) * MIXER_WIDTH
VMEM_LIMIT = 56 * 1024 * 1024

G_SB_Q, G_SB_K, G_DF_Q, G_DF_K, G_DS_Q, G_DS_K, G_QI_A, G_QI_B, G_MB_Q, G_MB_K, G_KIDX = range(N_GROUP)
S_KIDX = 4 * G_KIDX
M_SB, M_DF, M_DS, M_MB = range(N_MIXERS)


def _layout():
    off = {}
    acc = 0
    for name, sz in (("q_sb", 256), ("k_sb", 256), ("v_sb", 256), ("q1", 128), ("q2", 128), ("k1", 128),
                     ("k2", 128), ("v_df", 256), ("q_ds", 256), ("k_ds", 256), ("v_ds", 256), ("qi", 512),
                     ("ki", 64), ("wi", 8), ("q_mb", 256), ("k_mb", 256), ("v_mb", 256), ("gate", 4096)):
        off[name] = acc
        acc += sz
    return off


_OFF = _layout()


def _pack_layout():
    off = _OFF
    cols, scale = [], []

    def add(start, n, s=1.0):
        cols.extend(range(start, start + n))
        scale.extend([s] * n)

    hd = HEAD_DIM ** -0.5
    hd2 = hd * LOG2E
    df2 = DIFF_QK_DIM ** -0.5 * LOG2E
    add(off["q_sb"], 256, hd2); add(off["k_sb"], 256)
    for h in range(N_HEADS):
        add(off["q1"] + h * 32, 32, df2); add(off["q2"] + h * 32, 32, df2)
    for h in range(N_HEADS):
        add(off["k1"] + h * 32, 32); add(off["k2"] + h * 32, 32)
    add(off["q_ds"], 256, hd2); add(off["k_ds"], 256)
    add(off["qi"], 512, HEAD_DIM ** -0.5)
    add(off["q_mb"], 256, hd2); add(off["k_mb"], 256)
    add(off["ki"], 64); add(off["wi"], IDX_HEADS, IDX_HEADS ** -0.5)
    cols.extend([-1] * 184); scale.extend([1.0] * 184)
    for name in ("v_sb", "v_df", "v_ds", "v_mb"):
        add(off[name], MIXER_WIDTH)
    assert len(cols) == N_PACK
    return np.asarray(cols, np.int32), np.asarray(scale, np.float32)


_PACK_SRC, _PACK_SCALE = _pack_layout()


def _dot(a, b):
    return jnp.dot(a, b, preferred_element_type=F32)


def _dot_nt(a, b):
    return lax.dot_general(a, b, (((1,), (1,)), ((), ())), preferred_element_type=F32)


def _rms(x, g):
    return x * lax.rsqrt(jnp.mean(x * x, axis=-1, keepdims=True) + NORM_EPS) * g


def _cparams(n_axes):
    return pltpu.CompilerParams(dimension_semantics=("arbitrary",) * n_axes, vmem_limit_bytes=VMEM_LIMIT)


def _const_spec(shape):
    nd = len(shape)
    return pl.BlockSpec(shape, lambda *_: (0,) * nd, pipeline_mode=pl.Buffered(1))


def _layer_spec(shape, layer):
    nd = len(shape)
    return pl.BlockSpec((pl.Squeezed(),) + tuple(shape), lambda *_: (layer,) + (0,) * nd, pipeline_mode=pl.Buffered(1))


def _proj_kernel(x_ref, g_ref, w_ref, cs_ref, zz_ref, vt_ref, wt_ref):
    h = _rms(x_ref[...], g_ref[...]).astype(BF16)

    def group(c):
        cols = slice(c * MIXER_WIDTH, (c + 1) * MIXER_WIDTH)
        return _dot(h, w_ref[:, cols]) * cs_ref[:, cols]

    for c in range(N_GROUP):
        r = group(c)
        for s in range(4):
            zz_ref[4 * c + s] = r[:, s * HEAD_DIM:(s + 1) * HEAD_DIM].astype(BF16)
        if c == G_KIDX:
            wt_ref[...] = jnp.transpose(r)[HEAD_DIM:HEAD_DIM + IDX_HEADS]
    for m in range(N_MIXERS):
        v = group(N_GROUP + m)
        for u in range(TM_PROJ // TK):
            vt = jnp.transpose(v[u * TK:(u + 1) * TK]).astype(BF16)
            for hh in range(N_HEADS):
                vt_ref[m, u, hh * V_ROWS:hh * V_ROWS + HEAD_DIM, :] = vt[hh * HEAD_DIM:(hh + 1) * HEAD_DIM]
                vt_ref[m, u, hh * V_ROWS + HEAD_DIM:(hh + 1) * V_ROWS, :] = jnp.ones((V_ROWS - HEAD_DIM, TK), BF16)


def _proj(x, g, w, cs, layer):
    t = x.shape[0]
    tm = TM_PROJ
    return pl.pallas_call(
        _proj_kernel,
        grid=(t // tm,),
        in_specs=[pl.BlockSpec((tm, D_MODEL), lambda i: (i, 0)),
                  _const_spec((1, D_MODEL)),
                  _layer_spec((D_MODEL, N_PACK), layer),
                  _const_spec((1, N_PACK))],
        out_specs=[pl.BlockSpec((N_SLAB, tm, HEAD_DIM), lambda i: (0, i, 0)),
                   pl.BlockSpec((N_MIXERS, tm // TK, N_HEADS * V_ROWS, TK), lambda i: (0, i, 0, 0)),
                   pl.BlockSpec((IDX_HEADS, tm), lambda i: (0, i))],
        out_shape=[jax.ShapeDtypeStruct((N_SLAB, t, HEAD_DIM), BF16),
                   jax.ShapeDtypeStruct((N_MIXERS, t // TK, N_HEADS * V_ROWS, TK), BF16),
                   jax.ShapeDtypeStruct((IDX_HEADS, t), F32)],
        compiler_params=_cparams(1),
        name="proj",
    )(x, g, w, cs)


def _k_block(ref, h, j):
    return ref[h, pl.ds(pl.multiple_of(j * TK, TK), TK), :]


def _fold_keys(a, op):
    n = a.shape[0]
    while n > 8:
        n //= 2
        a = op(a[:n], a[n:2 * n])
    return a


def _vt_block(ref, h, j, rows=V_ROWS):
    return ref[0, j, h * V_ROWS:h * V_ROWS + rows, :]


def _softmax_block(s_list, vt_list, carry):
    ms = [jnp.maximum(c[0], jnp.max(_fold_keys(s, jnp.maximum), axis=0, keepdims=True))
          for s, c in zip(s_list, carry)]
    pvs = [_dot(vt, jnp.exp2(s - m).astype(BF16)) for vt, s, m in zip(vt_list, s_list, ms)]
    return tuple((m_new, jnp.exp2(m - m_new) * acc + pv) for (m, acc), m_new, pv in zip(carry, ms, pvs))


def _softmax_loop(n_blocks, logits, values, carry, groups=(2, 1)):
    assert groups[-1] == 1
    start = 0
    for g in groups:
        def body(t, carry, g=g, start=start):
            j = start + g * t
            s_all = [logits(j + u) for u in range(g)]
            for u in range(g):
                carry = _softmax_block(s_all[u], values(j + u), carry)
            return carry

        trips = (n_blocks - start) // g
        carry = lax.fori_loop(0, trips, body, carry)
        start = start + g * trips
    return carry


def _softmax_init():
    return (jnp.full((1, TQ), NEG, F32), jnp.zeros((V_ROWS, TQ), F32))


def _softmax_out(carry):
    _, acc = carry
    return acc[:HEAD_DIM] / acc[HEAD_DIM:HEAD_DIM + 1]


def _store_heads(o_ref, heads_t):
    o_ref[...] = jnp.transpose(jnp.concatenate(heads_t, axis=0)).astype(BF16)


BIAS_ROWS_SHAPE = (N_HEADS, 3, 1, 2 * TQ)
BIAS_TILES_SHAPE = (N_HEADS, 3, TK, TQ)


def _fill_bias_tiles(bw_ref, bt_ref):
    @pl.when((pl.program_id(0) == 0) & (pl.program_id(1) == 0))
    def _():
        for h in range(N_HEADS):
            for o in range(3):
                rows = jnp.broadcast_to(bw_ref[h, o], (TK, 2 * TQ))
                bt_ref[h, o] = pltpu.roll(rows, TQ + 1, 1, stride=1, stride_axis=0)[:, :TQ]


def _attn_specs(nq, gq, gk, mixer, s):
    return [pl.BlockSpec((4, TQ, HEAD_DIM), lambda b, i: (gq, b * nq + i, 0)),
            pl.BlockSpec((4, s, HEAD_DIM), lambda b, i: (gk, b, 0)),
            pl.BlockSpec((1, s // TK, N_HEADS * V_ROWS, TK), lambda b, i: (mixer, b, 0, 0))]


def _out_spec(nq):
    return pl.BlockSpec((TQ, N_HEADS * HEAD_DIM), lambda b, i: (b * nq + i, 0))


def _sb_kernel(q_ref, k_ref, vt_ref, tri_ref, mask_ref, o_ref):
    i = pl.program_id(1)
    tri = tri_ref[...]

    def blocks(js, carry, first_masked):
        heads = range(N_HEADS)
        zs = [[_dot_nt(_k_block(k_ref, h, j), q_ref[h]) for h in heads] for j in js]
        if first_masked:
            zs[0] = [z + mask_ref[...] for z in zs[0]]
        sps = [[jnp.maximum(z, 0.0) + jnp.log2(1.0 + jnp.exp2(-jnp.abs(z))) for z in zb] for zb in zs]
        his = [[sp.astype(BF16) for sp in sb] for sb in sps]
        los = [[(sp - hi.astype(F32)).astype(BF16) for sp, hi in zip(sb, hb)] for sb, hb in zip(sps, his)]
        cums = [[_dot(tri, hi) + _dot(tri, lo) for hi, lo in zip(hb, lb)] for hb, lb in zip(his, los)]
        out = []
        for h in heads:
            tail, acc = carry[h]
            for u, j in enumerate(js):
                c = cums[u][h] + tail
                acc = acc + _dot(_vt_block(vt_ref, h, j, HEAD_DIM), jnp.exp2(zs[u][h] - c).astype(BF16))
                tail = c[0:1, :]
            out.append((tail, acc))
        return tuple(out)

    init = tuple((jnp.zeros((1, TQ), F32), jnp.zeros((HEAD_DIM, TQ), F32)) for _ in range(N_HEADS))
    carry = lax.cond(i >= 1, lambda c: blocks([i, i - 1], c, True), lambda c: blocks([i], c, True), init)

    def weights_left(carry):
        tail = functools.reduce(jnp.minimum, [c[0] for c in carry])
        return (jnp.min(tail) < SB_TAIL_CUTOFF).astype(jnp.int32)

    def earlier_block(c):
        j, carry, _ = c
        carry = blocks([j], carry, False)
        return j - 1, carry, weights_left(carry)

    _, carry, _ = lax.while_loop(lambda c: (c[0] >= 0) & (c[2] > 0), earlier_block, (i - 2, carry, weights_left(carry)))
    _store_heads(o_ref, [c[1] for c in carry])


def _sb_attention(zz, vt, b, s, tri, mask):
    nq = s // TQ
    return pl.pallas_call(
        _sb_kernel,
        grid=(b, nq),
        in_specs=_attn_specs(nq, G_SB_Q, G_SB_K, M_SB, s) + [_const_spec((TK, TK)), _const_spec((TK, TQ))],
        out_specs=_out_spec(nq),
        out_shape=jax.ShapeDtypeStruct((b * s, N_HEADS * HEAD_DIM), BF16),
        compiler_params=_cparams(2),
        name="sb_attn",
    )(zz, zz, vt, tri, mask)


def _diff_kernel(q_ref, k_ref, vt_ref, bw_ref, lam_ref, cst_ref, g_ref, o_ref, bt_ref):
    i = pl.program_id(1)
    _fill_bias_tiles(bw_ref, bt_ref)
    lp = lam_ref[...]
    lam_init = cst_ref[:, 0:1]
    lam = (jnp.exp(jnp.sum(lp[0:1] * lp[1:2], axis=-1, keepdims=True))
           - jnp.exp(jnp.sum(lp[2:3] * lp[3:4], axis=-1, keepdims=True)) + lam_init)
    lane = lax.broadcasted_iota(jnp.int32, (TQ, HEAD_DIM), 1)
    qs = []
    for h in range(N_HEADS):
        q = q_ref[h]
        qs.append((jnp.where(lane < DIFF_QK_DIM, q, jnp.zeros_like(q)),
                   jnp.where(lane >= DIFF_QK_DIM, q, jnp.zeros_like(q))))

    def logits(j):
        s_list = []
        for h in range(N_HEADS):
            kj = _k_block(k_ref, h, j)
            bias = bt_ref[h, jnp.minimum(i - j, 2)]
            s_list += [_dot_nt(kj, qs[h][0]) + bias, _dot_nt(kj, qs[h][1]) + bias]
        return s_list

    def values(j):
        return [_vt_block(vt_ref, h, j) for h in range(N_HEADS) for _ in range(2)]

    carry = _softmax_loop(i + 1, logits, values, tuple(_softmax_init() for _ in range(2 * N_HEADS)), groups=(4, 2, 1))
    outs = []
    for h in range(N_HEADS):
        o = _softmax_out(carry[2 * h]) - lam * _softmax_out(carry[2 * h + 1])
        o = o * lax.rsqrt(jnp.mean(o * o, axis=0, keepdims=True) + NORM_EPS) * g_ref[...]
        outs.append(o * (1.0 - lam_init))
    _store_heads(o_ref, outs)


def _diff_attention(zz, vt, b, s, bw, lamp, cst, g):
    nq = s // TQ
    return pl.pallas_call(
        _diff_kernel,
        grid=(b, nq),
        in_specs=_attn_specs(nq, G_DF_Q, G_DF_K, M_DF, s) + [
            _const_spec(BIAS_ROWS_SHAPE), _const_spec((4, DIFF_QK_DIM)), _const_spec((1, 128)),
            _const_spec((HEAD_DIM, 1))],
        out_specs=_out_spec(nq),
        out_shape=jax.ShapeDtypeStruct((b * s, N_HEADS * HEAD_DIM), BF16),
        scratch_shapes=[pltpu.VMEM(BIAS_TILES_SHAPE, F32)],
        compiler_params=_cparams(2),
        name="diff_attn",
    )(zz, zz, vt, bw, lamp, cst, g)


def _dsa_kernel(q_ref, k_ref, vt_ref, qia_ref, qib_ref, ki_ref, wt_ref, bw_ref, tril_ref, o_ref, sc_ref, bt_ref, *,
                topk):
    i = pl.program_id(1)
    _fill_bias_tiles(bw_ref, bt_ref)
    nb = i + 1
    kf = float(topk)
    w = wt_ref[...]
    key = lax.broadcasted_iota(jnp.int32, (TK, TQ), 0)
    qry = lax.broadcasted_iota(jnp.int32, (TK, TQ), 1)

    def score(j):
        kij = _k_block(ki_ref, 0, j)
        sc = jnp.zeros((TK, TQ), F32)
        for hh in range(IDX_HEADS):
            qi = (qia_ref if hh < 4 else qib_ref)[hh % 4]
            sc = sc + w[hh:hh + 1, :] * jnp.maximum(_dot_nt(kij, qi), 0.0)
        return sc

    def scored(j):
        sc = score(j)
        valid = key - qry <= (i - j) * TQ
        masked = jnp.where(valid, sc, NEG)
        sc_ref[j] = masked
        return masked, jnp.where(valid, sc, BIG)

    c = (jnp.full((8, TQ), BIG, F32), jnp.full((8, TQ), -BIG, F32))
    start = 0
    for g in (4, 2, 1):
        def body(t, c, g=g, start=start):
            tiles = [scored(start + g * t + u) for u in range(g)]
            for masked, lo_src in tiles:
                c = (jnp.minimum(c[0], _fold_keys(lo_src, jnp.minimum)),
                     jnp.maximum(c[1], _fold_keys(masked, jnp.maximum)))
            return c

        trips = (nb - start) // g
        c = lax.fori_loop(0, trips, body, c)
        start = start + g * trips
    lo_part, hi_part = c

    def reduce_blocks(fn, init):
        def pair(t, c):
            return fn(sc_ref[2 * t + 1], 2 * t + 1, fn(sc_ref[2 * t], 2 * t, c))
        c = lax.fori_loop(0, nb // 2, pair, init)
        return lax.fori_loop(2 * (nb // 2), nb, lambda j, c: fn(sc_ref[j], j, c), c)

    def count_ge(t):
        part = reduce_blocks(lambda x, j, c: c + _fold_keys(jnp.where(x >= t, 1.0, 0.0), jnp.add),
                             jnp.zeros((8, TQ), F32))
        return jnp.sum(part, axis=0, keepdims=True)

    def minmax_blocks(lo_of, hi_of):
        def f(x, j, c):
            return (jnp.minimum(c[0], _fold_keys(lo_of(x), jnp.minimum)),
                    jnp.maximum(c[1], _fold_keys(hi_of(x), jnp.maximum)))
        lo_part, hi_part = reduce_blocks(f, (jnp.full((8, TQ), BIG, F32), jnp.full((8, TQ), -BIG, F32)))
        return jnp.min(lo_part, axis=0, keepdims=True), jnp.max(hi_part, axis=0, keepdims=True)

    n_valid = i * TQ + lax.broadcasted_iota(jnp.int32, (1, TQ), 1) + 1
    take_all = n_valid <= topk
    lo = jnp.min(lo_part, axis=0, keepdims=True)
    hi = jnp.max(hi_part, axis=0, keepdims=True)
    c_max = count_ge(hi)
    at_max = c_max >= kf
    state = (jnp.where(at_max, hi, lo), hi, jnp.where(at_max, c_max, n_valid.astype(F32)), c_max)

    def bisect(_, state):
        lo, hi, c_lo, c_hi = state
        mid = 0.5 * lo + 0.5 * hi
        c = count_ge(mid)
        ge = c >= kf
        return jnp.where(ge, mid, lo), jnp.where(ge, hi, mid), jnp.where(ge, c, c_lo), jnp.where(ge, c_hi, c)

    def unsettled(state):
        lo, hi, c_lo, _ = state
        open_q = jnp.where(take_all, 0.0, jnp.where(c_lo != kf, jnp.where(lo < hi, 1.0, 0.0), 0.0))

        def band_spread():
            b_min, b_max = minmax_blocks(lambda x: jnp.where(x >= lo, jnp.where(x < hi, x, BIG), BIG),
                                         lambda x: jnp.where(x >= lo, jnp.where(x < hi, x, -BIG), -BIG))
            return (jnp.max(jnp.where(b_max != b_min, open_q, 0.0)) > 0.0).astype(jnp.int32)

        return lax.cond(jnp.max(open_q) > 0.0, band_spread, lambda: jnp.int32(0))

    state = lax.fori_loop(0, jnp.where((i + 1) * TQ <= topk, 0, BISECT_WARMUP), bisect, state)

    def trip(c):
        n, state, _ = c
        state = lax.fori_loop(0, BISECT_TRIP, bisect, state)
        return n + 1, state, unsettled(state)

    _, state, _ = lax.while_loop(lambda c: (c[2] > 0) & (c[0] < BISECT_MAX_TRIPS), trip,
                                 (jnp.int32(0), state, unsettled(state)))
    lo, hi, c_lo, c_hi = state
    hi_ok = lo < hi
    c_above = jnp.where(hi_ok, c_hi, 0.0)
    hi_sel = jnp.where(hi_ok, hi, BIG)
    need = jnp.where(take_all, BIG, kf - c_above)
    lo_sel = jnp.where(take_all, HALF_NEG, lo)

    tied = jnp.max(jnp.where(take_all, 0.0, c_lo - kf)) > 0.0

    @pl.when(tied)
    def _():
        tril = tril_ref[...]

        def band_of(j):
            x = sc_ref[j]
            return x, jnp.where(x >= lo_sel, jnp.where(x < hi_sel, 1.0, 0.0), 0.0)

        def write(j, x, band, rank):
            sc_ref[j] = jnp.where(x >= hi_sel, 0.0,
                                  jnp.where(band * rank > 0.0, jnp.where(rank <= need, 0.0, NEG), NEG))
            return rank[TK - 1:TK, :]

        def write_mask(j, taken):
            x, band = band_of(j)
            return write(j, x, band, _dot(tril, band.astype(BF16)) + taken)

        def write_mask_pair(t, taken):
            (x_a, band_a), (x_b, band_b) = band_of(2 * t), band_of(2 * t + 1)
            in_a, in_b = _dot(tril, band_a.astype(BF16)), _dot(tril, band_b.astype(BF16))
            taken = write(2 * t, x_a, band_a, in_a + taken)
            return write(2 * t + 1, x_b, band_b, in_b + taken)

        taken = lax.fori_loop(0, nb // 2, write_mask_pair, jnp.zeros((1, TQ), F32))
        lax.fori_loop(2 * (nb // 2), nb, write_mask, taken)

    @pl.when(jnp.logical_not(tied))
    def _():
        def write_mask(j, _):
            sc_ref[j] = jnp.where(sc_ref[j] >= lo_sel, 0.0, NEG)
            return 0

        lax.fori_loop(0, nb, write_mask, 0)

    def logits(j):
        return [_dot_nt(_k_block(k_ref, h, j), q_ref[h]) + bt_ref[h, jnp.minimum(i - j, 2)] + sc_ref[j]
                for h in range(N_HEADS)]

    carry = _softmax_loop(nb, logits, lambda j: [_vt_block(vt_ref, h, j) for h in range(N_HEADS)],
                          tuple(_softmax_init() for _ in range(N_HEADS)), groups=(4, 2, 1))
    _store_heads(o_ref, [_softmax_out(c) for c in carry])


def _dsa_attention(zz, vt, wt, b, s, bw, tril):
    nq = s // TQ
    topk = min(DSA_TOPK_MAX, s // 4)
    return pl.pallas_call(
        functools.partial(_dsa_kernel, topk=topk),
        grid=(b, nq),
        in_specs=_attn_specs(nq, G_DS_Q, G_DS_K, M_DS, s) + [
            pl.BlockSpec((4, TQ, HEAD_DIM), lambda b_, i: (G_QI_A, b_ * nq + i, 0)),
            pl.BlockSpec((4, TQ, HEAD_DIM), lambda b_, i: (G_QI_B, b_ * nq + i, 0)),
            pl.BlockSpec((1, s, HEAD_DIM), lambda b_, i: (S_KIDX, b_, 0)),
            pl.BlockSpec((IDX_HEADS, TQ), lambda b_, i: (0, b_ * nq + i)),
            _const_spec(BIAS_ROWS_SHAPE), _const_spec((TK, TK))],
        out_specs=_out_spec(nq),
        out_shape=jax.ShapeDtypeStruct((b * s, N_HEADS * HEAD_DIM), BF16),
        scratch_shapes=[pltpu.VMEM((nq, TK, TQ), F32), pltpu.VMEM(BIAS_TILES_SHAPE, F32)],
        compiler_params=_cparams(2),
        name="dsa_attn",
    )(zz, zz, vt, zz, zz, zz, wt, bw, tril)


def _moba_kernel(q_ref, k_ref, vt_ref, bw_ref, o_ref, km_ref, bt_ref, *, nblk, topb):
    i = pl.program_id(1)
    _fill_bias_tiles(bw_ref, bt_ref)
    nrow = km_ref.shape[1]

    @pl.when(i == 0)
    def _():
        km_ref[...] = jnp.zeros_like(km_ref)
        for h in range(N_HEADS):
            for n in range(nblk):
                kb = k_ref[h, n * MOBA_BLOCK:(n + 1) * MOBA_BLOCK, :].astype(F32)
                km_ref[h, n:n + 1, :] = jnp.mean(kb, axis=0, keepdims=True)

    blk = lax.broadcasted_iota(jnp.int32, (nrow, TQ), 0)
    past = blk < i
    head_bits = []
    for h in range(N_HEADS):
        gate = _dot_nt(km_ref[h].astype(BF16), q_ref[h])
        bits = jnp.zeros((1, TQ), F32)
        for n in range(nblk):
            gn = gate[n:n + 1, :]
            beats = jnp.where(past, jnp.where(gate > gn, 1.0, jnp.where(gate == gn, jnp.where(blk < n, 1.0, 0.0), 0.0)), 0.0)
            rank = jnp.sum(beats, axis=0, keepdims=True)
            bits = bits + jnp.where(rank < float(topb), jnp.where(n < i, float(2 ** n), 0.0), 0.0)
        head_bits.append(bits.astype(jnp.int32) | lax.shift_left(jnp.int32(1), i))

    def values(n):
        return [_vt_block(vt_ref, h, n) for h in range(N_HEADS)]

    def logits(n):
        s_list = []
        for h in range(N_HEADS):
            picked = (lax.shift_right_logical(head_bits[h], jnp.full_like(head_bits[h], n)) & 1) == 1
            s_list.append(_dot_nt(_k_block(k_ref, h, n), q_ref[h]) + bt_ref[h, jnp.minimum(i - n, 2)]
                          + jnp.where(picked, 0.0, NEG))
        return s_list

    carry = _softmax_loop(i + 1, logits, values, tuple(_softmax_init() for _ in range(N_HEADS)), groups=(4, 2, 1))
    _store_heads(o_ref, [_softmax_out(c) for c in carry])


def _moba_attention(zz, vt, b, s, bw):
    nq = s // TQ
    nblk = s // MOBA_BLOCK
    topb = min(MOBA_TOPK, nblk - 1)
    return pl.pallas_call(
        functools.partial(_moba_kernel, nblk=nblk, topb=topb),
        grid=(b, nq),
        in_specs=_attn_specs(nq, G_MB_Q, G_MB_K, M_MB, s) + [_const_spec(BIAS_ROWS_SHAPE)],
        out_specs=_out_spec(nq),
        out_shape=jax.ShapeDtypeStruct((b * s, N_HEADS * HEAD_DIM), BF16),
        scratch_shapes=[pltpu.VMEM((N_HEADS, max(8, nblk), HEAD_DIM), F32), pltpu.VMEM(BIAS_TILES_SHAPE, F32)],
        compiler_params=_cparams(2),
        name="moba_attn",
    )(zz, zz, vt, bw)


def _merge_kernel(x_ref, osb_ref, odf_ref, ods_ref, omb_ref, gpre_ref, wg_ref, wbr_ref, wout_ref, gpost_ref, o_ref):
    x = x_ref[...]
    h = _rms(x, gpre_ref[...]).astype(BF16)
    y = jnp.zeros((x.shape[0], D_MODEL), F32)
    for r, o_r in enumerate((osb_ref, odf_ref, ods_ref, omb_ref)):
        gate = jax.nn.sigmoid(_dot(h, wg_ref[:, r * D_MODEL:(r + 1) * D_MODEL]))
        y = y + gate * _dot(o_r[...], wbr_ref[r])
    o_ref[...] = x + _rms(_dot(y.astype(BF16), wout_ref[...]), gpost_ref[...])


def _merge(x, o_sb, o_df, o_ds, o_mb, g_pre, w_gate, w_br, w_out, g_post, layer):
    t = x.shape[0]
    tm = TM_MERGE
    tok = lambda width: pl.BlockSpec((tm, width), lambda i: (i, 0))
    return pl.pallas_call(
        _merge_kernel,
        grid=(t // tm,),
        in_specs=[tok(D_MODEL)] + [tok(MIXER_WIDTH)] * N_MIXERS + [
            _const_spec((1, D_MODEL)), _layer_spec((D_MODEL, N_MIXERS * D_MODEL), layer),
            _layer_spec((N_MIXERS, MIXER_WIDTH, D_MODEL), layer), _layer_spec((D_MODEL, D_MODEL), layer),
            _const_spec((1, D_MODEL))],
        out_specs=tok(D_MODEL),
        out_shape=jax.ShapeDtypeStruct((t, D_MODEL), F32),
        compiler_params=_cparams(1),
        name="merge",
    )(x, o_sb, o_df, o_ds, o_mb, g_pre, w_gate, w_br, w_out, g_post)


def _ffn_kernel(x_ref, gpre_ref, win_ref, wout_ref, gpost_ref, o_ref):
    x = x_ref[...]
    h = _rms(x, gpre_ref[...]).astype(BF16)
    gate = _dot(h, win_ref[:, 0:D_FF])
    up = _dot(h, win_ref[:, D_FF:2 * D_FF])
    act = (gate * jax.nn.sigmoid(gate) * up).astype(BF16)
    o_ref[...] = x + _rms(_dot(act, wout_ref[...]), gpost_ref[...])


def _ffn(x, g_pre, w_in, w_out, g_post, layer):
    t = x.shape[0]
    tm = TM_FFN
    return pl.pallas_call(
        _ffn_kernel,
        grid=(t // tm,),
        in_specs=[pl.BlockSpec((tm, D_MODEL), lambda i: (i, 0)), _const_spec((1, D_MODEL)),
                  _layer_spec((D_MODEL, 2 * D_FF), layer), _layer_spec((D_FF, D_MODEL), layer),
                  _const_spec((1, D_MODEL))],
        out_specs=pl.BlockSpec((tm, D_MODEL), lambda i: (i, 0)),
        out_shape=jax.ShapeDtypeStruct((t, D_MODEL), F32),
        compiler_params=_cparams(1),
        name="ffn",
    )(x, g_pre, w_in, w_out, g_post)


def _t5_bucket(dist):
    max_exact = N_BUCKETS // 2
    d = jnp.maximum(dist, 0)
    log_ratio = jnp.log(jnp.maximum(d, 1).astype(F32) / max_exact) / math.log(MAX_DISTANCE / max_exact)
    large = jnp.minimum(max_exact + (log_ratio * (N_BUCKETS - max_exact)).astype(jnp.int32), N_BUCKETS - 1)
    return jnp.where(d < max_exact, d, large)


def _bias_rows(rel_bias):
    assert TQ == TK
    n = TK
    d = np.arange(-(n - 1), 3 * n + 1)
    by_dist = rel_bias.astype(F32).T[:, _t5_bucket(jnp.asarray(np.maximum(d, 0), jnp.int32))]
    by_dist = jnp.where(jnp.asarray(d >= 0)[None, :], by_dist * LOG2E, NEG)
    return jnp.stack([by_dist[:, o * n:o * n + 2 * n] for o in range(3)], axis=1)[:, :, None, :]


def _pack_runs():
    runs, start = [], 0
    for e in range(1, N_PACK + 1):
        if e == N_PACK or _PACK_SRC[e] != _PACK_SRC[e - 1] + (1 if _PACK_SRC[e - 1] >= 0 else 0):
            runs.append((int(_PACK_SRC[start]), e - start))
            start = e
    return runs


_PACK_RUNS = _pack_runs()
PACK_ROWS = 128


def _pack_kernel(w_ref, pack_ref, gate_ref):
    w = w_ref[...]
    parts = [w[:, a:a + n] if a >= 0 else jnp.zeros((w.shape[0], n), F32) for a, n in _PACK_RUNS]
    pack_ref[...] = jnp.concatenate(parts, axis=1).astype(BF16)
    gate_ref[...] = w[:, _OFF["gate"]:].astype(BF16)


def _pack_weights(w_in):
    depth, d, n_in = w_in.shape
    row_block = lambda width: pl.BlockSpec((pl.Squeezed(), PACK_ROWS, width), lambda l, r: (l, r, 0))
    return pl.pallas_call(
        _pack_kernel,
        grid=(depth, d // PACK_ROWS),
        in_specs=[row_block(n_in)],
        out_specs=[row_block(N_PACK), row_block(N_MIXERS * D_MODEL)],
        out_shape=[jax.ShapeDtypeStruct((depth, d, N_PACK), BF16),
                   jax.ShapeDtypeStruct((depth, d, N_MIXERS * D_MODEL), BF16)],
        compiler_params=_cparams(2),
        name="pack_weights",
    )(w_in)


def kernel(x, w_in, w_br_sb, w_br_diff, w_br_dsa, w_br_moba, w_out, lambda_q1, lambda_k1, lambda_q2, lambda_k2,
           diff_subln_g, rel_bias, w_ffn_in, w_ffn_out, g_pre_mix, g_post_mix, g_pre_ffn, g_post_ffn):
    b, s, d = x.shape
    depth = w_in.shape[0]
    assert d == D_MODEL and s % TQ == 0 and s // MOBA_BLOCK >= 2
    t = b * s

    w_pack, w_gate = _pack_weights(w_in)
    w_br = jnp.stack([w_br_sb, w_br_diff, w_br_dsa, w_br_moba], axis=1).astype(BF16)
    w_o = w_out.astype(BF16)
    w_f1 = w_ffn_in.astype(BF16)
    w_f2 = w_ffn_out.astype(BF16)
    cs = jnp.asarray(_PACK_SCALE)[None, :]

    bw = _bias_rows(rel_bias)
    bw_df, bw_ds, bw_mb = bw[0:4], bw[4:8], bw[8:12]
    key = np.arange(TK)[:, None]
    qry = np.arange(TQ)[None, :]
    tri = jnp.asarray(key <= np.arange(TK)[None, :], BF16)
    tril = jnp.asarray(key >= np.arange(TK)[None, :], BF16)
    sb_mask = jnp.asarray(np.where(key < qry, 0.0, NEG), F32)

    xf = x.reshape(t, d)
    for l in range(depth):
        lam_init = 0.8 - 0.6 * math.exp(-0.3 * l)
        lamp = jnp.stack([lambda_q1[l], lambda_k1[l], lambda_q2[l], lambda_k2[l]]).astype(F32)
        cst = jnp.full((1, 128), lam_init, F32)
        zz, vt, wt = _proj(xf, g_pre_mix[l][None, :], w_pack, cs, l)
        o_sb = _sb_attention(zz, vt, b, s, tri, sb_mask)
        o_df = _diff_attention(zz, vt, b, s, bw_df, lamp, cst, diff_subln_g[l][:, None])
        o_ds = _dsa_attention(zz, vt, wt, b, s, bw_ds, tril)
        o_mb = _moba_attention(zz, vt, b, s, bw_mb)
        xf = _merge(xf, o_sb, o_df, o_ds, o_mb, g_pre_mix[l][None, :], w_gate, w_br, w_o, g_post_mix[l][None, :], l)
        xf = _ffn(xf, g_pre_ffn[l][None, :], w_f1, w_f2, g_post_ffn[l][None, :], l)
    return xf.reshape(b, s, d)
```

```python
import functools
import math

import numpy as np
import jax
import jax.numpy as jnp
from jax import lax
from jax.experimental import pallas as pl
from jax.experimental.pallas import tpu as pltpu

F32 = jnp.float32
BF16 = jnp.bfloat16

D_MODEL = 1024
HEAD_DIM = 64
N_HEADS = 4
N_MIXERS = 4
MIXER_WIDTH = N_HEADS * HEAD_DIM
DIFF_QK_DIM = 32
IDX_HEADS = 8
DSA_TOPK_MAX = 256
MOBA_BLOCK = 256
MOBA_TOPK = 3
N_BUCKETS = 32
MAX_DISTANCE = 128
D_FF = 2816
NORM_EPS = 1e-6

TQ = 256
TK = 256
NEG = -1e30
HALF_NEG = -0.5e30
BIG = 3e38
BISECT_WARMUP = 20
BISECT_TRIP = 2
BISECT_MAX_TRIPS = 134
LOG2E = math.log2(math.e)
SB_TAIL_CUTOFF = 120.0 * LOG2E
V_ROWS = 80
TM_PROJ = 512
TM_MERGE = 512
TM_FFN = 512
N_GROUP = 11
N_SLAB = 4 * N_GROUP
N_PACK = (N_GROUP + N_MIXERS) * MIXER_WIDTH
VMEM_LIMIT = 56 * 1024 * 1024

G_SB_Q, G_SB_K, G_DF_Q, G_DF_K, G_DS_Q, G_DS_K, G_QI_A, G_QI_B, G_MB_Q, G_MB_K, G_KIDX = range(N_GROUP)
S_KIDX = 4 * G_KIDX
M_SB, M_DF, M_DS, M_MB = range(N_MIXERS)


def _layout():
    off = {}
    acc = 0
    for name, sz in (("q_sb", 256), ("k_sb", 256), ("v_sb", 256), ("q1", 128), ("q2", 128), ("k1", 128),
                     ("k2", 128), ("v_df", 256), ("q_ds", 256), ("k_ds", 256), ("v_ds", 256), ("qi", 512),
                     ("ki", 64), ("wi", 8), ("q_mb", 256), ("k_mb", 256), ("v_mb", 256), ("gate", 4096)):
        off[name] = acc
        acc += sz
    return off


_OFF = _layout()


def _pack_layout():
    off = _OFF
    cols, scale = [], []

    def add(start, n, s=1.0):
        cols.extend(range(start, start + n))
        scale.extend([s] * n)

    hd = HEAD_DIM ** -0.5
    hd2 = hd * LOG2E
    df2 = DIFF_QK_DIM ** -0.5 * LOG2E
    add(off["q_sb"], 256, hd2); add(off["k_sb"], 256)
    for h in range(N_HEADS):
        add(off["q1"] + h * 32, 32, df2); add(off["q2"] + h * 32, 32, df2)
    for h in range(N_HEADS):
        add(off["k1"] + h * 32, 32); add(off["k2"] + h * 32, 32)
    add(off["q_ds"], 256, hd2); add(off["k_ds"], 256)
    add(off["qi"], 512, HEAD_DIM ** -0.5)
    add(off["q_mb"], 256, hd2); add(off["k_mb"], 256)
    add(off["ki"], 64); add(off["wi"], IDX_HEADS, IDX_HEADS ** -0.5)
    cols.extend([-1] * 184); scale.extend([1.0] * 184)
    for name in ("v_sb", "v_df", "v_ds", "v_mb"):
        add(off[name], MIXER_WIDTH)
    assert len(cols) == N_PACK
    return np.asarray(cols, np.int32), np.asarray(scale, np.float32)


_PACK_SRC, _PACK_SCALE = _pack_layout()


def _dot(a, b):
    return jnp.dot(a, b, preferred_element_type=F32)


def _dot_nt(a, b):
    return lax.dot_general(a, b, (((1,), (1,)), ((), ())), preferred_element_type=F32)


def _rms(x, g):
    return x * lax.rsqrt(jnp.mean(x * x, axis=-1, keepdims=True) + NORM_EPS) * g


def _cparams(n_axes):
    return pltpu.CompilerParams(dimension_semantics=("arbitrary",) * n_axes, vmem_limit_bytes=VMEM_LIMIT)


def _const_spec(shape):
    nd = len(shape)
    return pl.BlockSpec(shape, lambda *_: (0,) * nd, pipeline_mode=pl.Buffered(1))


def _layer_spec(shape, layer):
    nd = len(shape)
    return pl.BlockSpec((pl.Squeezed(),) + tuple(shape), lambda *_: (layer,) + (0,) * nd, pipeline_mode=pl.Buffered(1))


def _proj_kernel(x_ref, g_ref, w_ref, cs_ref, zz_ref, vt_ref, wt_ref):
    h = _rms(x_ref[...], g_ref[...]).astype(BF16)

    def group(c):
        cols = slice(c * MIXER_WIDTH, (c + 1) * MIXER_WIDTH)
        return _dot(h, w_ref[:, cols]) * cs_ref[:, cols]

    for c in range(N_GROUP):
        r = group(c)
        for s in range(4):
            zz_ref[4 * c + s] = r[:, s * HEAD_DIM:(s + 1) * HEAD_DIM].astype(BF16)
        if c == G_KIDX:
            wt_ref[...] = jnp.transpose(r)[HEAD_DIM:HEAD_DIM + IDX_HEADS]
    for m in range(N_MIXERS):
        v = group(N_GROUP + m)
        for u in range(TM_PROJ // TK):
            vt = jnp.transpose(v[u * TK:(u + 1) * TK]).astype(BF16)
            for hh in range(N_HEADS):
                vt_ref[m, u, hh * V_ROWS:hh * V_ROWS + HEAD_DIM, :] = vt[hh * HEAD_DIM:(hh + 1) * HEAD_DIM]
                vt_ref[m, u, hh * V_ROWS + HEAD_DIM:(hh + 1) * V_ROWS, :] = jnp.ones((V_ROWS - HEAD_DIM, TK), BF16)


def _proj(x, g, w, cs, layer):
    t = x.shape[0]
    tm = TM_PROJ
    return pl.pallas_call(
        _proj_kernel,
        grid=(t // tm,),
        in_specs=[pl.BlockSpec((tm, D_MODEL), lambda i: (i, 0)),
                  _const_spec((1, D_MODEL)),
                  _layer_spec((D_MODEL, N_PACK), layer),
                  _const_spec((1, N_PACK))],
        out_specs=[pl.BlockSpec((N_SLAB, tm, HEAD_DIM), lambda i: (0, i, 0)),
                   pl.BlockSpec((N_MIXERS, tm // TK, N_HEADS * V_ROWS, TK), lambda i: (0, i, 0, 0)),
                   pl.BlockSpec((IDX_HEADS, tm), lambda i: (0, i))],
        out_shape=[jax.ShapeDtypeStruct((N_SLAB, t, HEAD_DIM), BF16),
                   jax.ShapeDtypeStruct((N_MIXERS, t // TK, N_HEADS * V_ROWS, TK), BF16),
                   jax.ShapeDtypeStruct((IDX_HEADS, t), F32)],
        compiler_params=_cparams(1),
        name="proj",
    )(x, g, w, cs)


def _k_block(ref, h, j):
    return ref[h, pl.ds(pl.multiple_of(j * TK, TK), TK), :]


def _fold_keys(a, op):
    n = a.shape[0]
    while n > 8:
        n //= 2
        a = op(a[:n], a[n:2 * n])
    return a


def _vt_block(ref, h, j, rows=V_ROWS):
    return ref[0, j, h * V_ROWS:h * V_ROWS + rows, :]


def _softmax_block(s_list, vt_list, carry):
    ms = [jnp.maximum(c[0], jnp.max(_fold_keys(s, jnp.maximum), axis=0, keepdims=True))
          for s, c in zip(s_list, carry)]
    pvs = [_dot(vt, jnp.exp2(s - m).astype(BF16)) for vt, s, m in zip(vt_list, s_list, ms)]
    return tuple((m_new, jnp.exp2(m - m_new) * acc + pv) for (m, acc), m_new, pv in zip(carry, ms, pvs))


def _softmax_loop(n_blocks, logits, values, carry, groups=(2, 1)):
    assert groups[-1] == 1
    start = 0
    for g in groups:
        def body(t, carry, g=g, start=start):
            j = start + g * t
            s_all = [logits(j + u) for u in range(g)]
            for u in range(g):
                carry = _softmax_block(s_all[u], values(j + u), carry)
            return carry

        trips = (n_blocks - start) // g
        carry = lax.fori_loop(0, trips, body, carry)
        start = start + g * trips
    return carry


def _softmax_init():
    return (jnp.full((1, TQ), NEG, F32), jnp.zeros((V_ROWS, TQ), F32))


def _softmax_out(carry):
    _, acc = carry
    return acc[:HEAD_DIM] / acc[HEAD_DIM:HEAD_DIM + 1]


def _store_heads(o_ref, heads_t):
    o_ref[...] = jnp.transpose(jnp.concatenate(heads_t, axis=0)).astype(BF16)


BIAS_ROWS_SHAPE = (N_HEADS, 3, 1, 2 * TQ)
BIAS_TILES_SHAPE = (N_HEADS, 3, TK, TQ)


def _fill_bias_tiles(bw_ref, bt_ref):
    @pl.when((pl.program_id(0) == 0) & (pl.program_id(1) == 0))
    def _():
        for h in range(N_HEADS):
            for o in range(3):
                rows = jnp.broadcast_to(bw_ref[h, o], (TK, 2 * TQ))
                bt_ref[h, o] = pltpu.roll(rows, TQ + 1, 1, stride=1, stride_axis=0)[:, :TQ]


def _attn_specs(nq, gq, gk, mixer, s):
    return [pl.BlockSpec((4, TQ, HEAD_DIM), lambda b, i: (gq, b * nq + i, 0)),
            pl.BlockSpec((4, s, HEAD_DIM), lambda b, i: (gk, b, 0)),
            pl.BlockSpec((1, s // TK, N_HEADS * V_ROWS, TK), lambda b, i: (mixer, b, 0, 0))]


def _out_spec(nq):
    return pl.BlockSpec((TQ, N_HEADS * HEAD_DIM), lambda b, i: (b * nq + i, 0))


def _sb_kernel(q_ref, k_ref, vt_ref, tri_ref, mask_ref, o_ref):
    i = pl.program_id(1)
    tri = tri_ref[...]

    def blocks(js, carry, first_masked):
        heads = range(N_HEADS)
        zs = [[_dot_nt(_k_block(k_ref, h, j), q_ref[h]) for h in heads] for j in js]
        if first_masked:
            zs[0] = [z + mask_ref[...] for z in zs[0]]
        sps = [[jnp.maximum(z, 0.0) + jnp.log2(1.0 + jnp.exp2(-jnp.abs(z))) for z in zb] for zb in zs]
        his = [[sp.astype(BF16) for sp in sb] for sb in sps]
        los = [[(sp - hi.astype(F32)).astype(BF16) for sp, hi in zip(sb, hb)] for sb, hb in zip(sps, his)]
        cums = [[_dot(tri, hi) + _dot(tri, lo) for hi, lo in zip(hb, lb)] for hb, lb in zip(his, los)]
        out = []
        for h in heads:
            tail, acc = carry[h]
            for u, j in enumerate(js):
                c = cums[u][h] + tail
                acc = acc + _dot(_vt_block(vt_ref, h, j, HEAD_DIM), jnp.exp2(zs[u][h] - c).astype(BF16))
                tail = c[0:1, :]
            out.append((tail, acc))
        return tuple(out)

    init = tuple((jnp.zeros((1, TQ), F32), jnp.zeros((HEAD_DIM, TQ), F32)) for _ in range(N_HEADS))
    carry = lax.cond(i >= 1, lambda c: blocks([i, i - 1], c, True), lambda c: blocks([i], c, True), init)

    def weights_left(carry):
        tail = functools.reduce(jnp.minimum, [c[0] for c in carry])
        return (jnp.min(tail) < SB_TAIL_CUTOFF).astype(jnp.int32)

    def earlier_block(c):
        j, carry, _ = c
        carry = blocks([j], carry, False)
        return j - 1, carry, weights_left(carry)

    _, carry, _ = lax.while_loop(lambda c: (c[0] >= 0) & (c[2] > 0), earlier_block, (i - 2, carry, weights_left(carry)))
    _store_heads(o_ref, [c[1] for c in carry])


def _sb_attention(zz, vt, b, s, tri, mask):
    nq = s // TQ
    return pl.pallas_call(
        _sb_kernel,
        grid=(b, nq),
        in_specs=_attn_specs(nq, G_SB_Q, G_SB_K, M_SB, s) + [_const_spec((TK, TK)), _const_spec((TK, TQ))],
        out_specs=_out_spec(nq),
        out_shape=jax.ShapeDtypeStruct((b * s, N_HEADS * HEAD_DIM), BF16),
        compiler_params=_cparams(2),
        name="sb_attn",
    )(zz, zz, vt, tri, mask)


def _diff_kernel(q_ref, k_ref, vt_ref, bw_ref, lam_ref, cst_ref, g_ref, o_ref, bt_ref):
    i = pl.program_id(1)
    _fill_bias_tiles(bw_ref, bt_ref)
    lp = lam_ref[...]
    lam_init = cst_ref[:, 0:1]
    lam = (jnp.exp(jnp.sum(lp[0:1] * lp[1:2], axis=-1, keepdims=True))
           - jnp.exp(jnp.sum(lp[2:3] * lp[3:4], axis=-1, keepdims=True)) + lam_init)
    lane = lax.broadcasted_iota(jnp.int32, (TQ, HEAD_DIM), 1)
    qs = []
    for h in range(N_HEADS):
        q = q_ref[h]
        qs.append((jnp.where(lane < DIFF_QK_DIM, q, jnp.zeros_like(q)),
                   jnp.where(lane >= DIFF_QK_DIM, q, jnp.zeros_like(q))))

    def logits(j):
        s_list = []
        for h in range(N_HEADS):
            kj = _k_block(k_ref, h, j)
            bias = bt_ref[h, jnp.minimum(i - j, 2)]
            s_list += [_dot_nt(kj, qs[h][0]) + bias, _dot_nt(kj, qs[h][1]) + bias]
        return s_list

    def values(j):
        return [_vt_block(vt_ref, h, j) for h in range(N_HEADS) for _ in range(2)]

    carry = _softmax_loop(i + 1, logits, values, tuple(_softmax_init() for _ in range(2 * N_HEADS)), groups=(4, 2, 1))
    outs = []
    for h in range(N_HEADS):
        o = _softmax_out(carry[2 * h]) - lam * _softmax_out(carry[2 * h + 1])
        o = o * lax.rsqrt(jnp.mean(o * o, axis=0, keepdims=True) + NORM_EPS) * g_ref[...]
        outs.append(o * (1.0 - lam_init))
    _store_heads(o_ref, outs)


def _diff_attention(zz, vt, b, s, bw, lamp, cst, g):
    nq = s // TQ
    return pl.pallas_call(
        _diff_kernel,
        grid=(b, nq),
        in_specs=_attn_specs(nq, G_DF_Q, G_DF_K, M_DF, s) + [
            _const_spec(BIAS_ROWS_SHAPE), _const_spec((4, DIFF_QK_DIM)), _const_spec((1, 128)),
            _const_spec((HEAD_DIM, 1))],
        out_specs=_out_spec(nq),
        out_shape=jax.ShapeDtypeStruct((b * s, N_HEADS * HEAD_DIM), BF16),
        scratch_shapes=[pltpu.VMEM(BIAS_TILES_SHAPE, F32)],
        compiler_params=_cparams(2),
        name="diff_attn",
    )(zz, zz, vt, bw, lamp, cst, g)


def _dsa_moba_kernel(q_ref, k_ref, vt_ref, qia_ref, qib_ref, ki_ref, wt_ref, bw_ref, tril_ref,
                     mq_ref, mk_ref, mvt_ref, mbw_ref, o_ref, omb_ref, sc_ref, bt_ref, km_ref, mbt_ref, *,
                     topk, nblk, topb):
    i = pl.program_id(1)
    _fill_bias_tiles(bw_ref, bt_ref)
    _fill_bias_tiles(mbw_ref, mbt_ref)
    nb = i + 1
    kf = float(topk)
    w = wt_ref[...]
    key = lax.broadcasted_iota(jnp.int32, (TK, TQ), 0)
    qry = lax.broadcasted_iota(jnp.int32, (TK, TQ), 1)

    def score(j):
        kij = _k_block(ki_ref, 0, j)
        sc = jnp.zeros((TK, TQ), F32)
        for hh in range(IDX_HEADS):
            qi = (qia_ref if hh < 4 else qib_ref)[hh % 4]
            sc = sc + w[hh:hh + 1, :] * jnp.maximum(_dot_nt(kij, qi), 0.0)
        return sc

    def scored(j):
        sc = score(j)
        valid = key - qry <= (i - j) * TQ
        masked = jnp.where(valid, sc, NEG)
        sc_ref[j] = masked
        return masked, jnp.where(valid, sc, BIG)

    c = (jnp.full((8, TQ), BIG, F32), jnp.full((8, TQ), -BIG, F32))
    start = 0
    for g in (4, 2, 1):
        def body(t, c, g=g, start=start):
            tiles = [scored(start + g * t + u) for u in range(g)]
            for masked, lo_src in tiles:
                c = (jnp.minimum(c[0], _fold_keys(lo_src, jnp.minimum)),
                     jnp.maximum(c[1], _fold_keys(masked, jnp.maximum)))
            return c

        trips = (nb - start) // g
        c = lax.fori_loop(0, trips, body, c)
        start = start + g * trips
    lo_part, hi_part = c

    def reduce_blocks(fn, init):
        def pair(t, c):
            return fn(sc_ref[2 * t + 1], 2 * t + 1, fn(sc_ref[2 * t], 2 * t, c))
        c = lax.fori_loop(0, nb // 2, pair, init)
        return lax.fori_loop(2 * (nb // 2), nb, lambda j, c: fn(sc_ref[j], j, c), c)

    def count_ge(t):
        part = reduce_blocks(lambda x, j, c: c + _fold_keys(jnp.where(x >= t, 1.0, 0.0), jnp.add),
                             jnp.zeros((8, TQ), F32))
        return jnp.sum(part, axis=0, keepdims=True)

    def minmax_blocks(lo_of, hi_of):
        def f(x, j, c):
            return (jnp.minimum(c[0], _fold_keys(lo_of(x), jnp.minimum)),
                    jnp.maximum(c[1], _fold_keys(hi_of(x), jnp.maximum)))
        lo_part, hi_part = reduce_blocks(f, (jnp.full((8, TQ), BIG, F32), jnp.full((8, TQ), -BIG, F32)))
        return jnp.min(lo_part, axis=0, keepdims=True), jnp.max(hi_part, axis=0, keepdims=True)

    n_valid = i * TQ + lax.broadcasted_iota(jnp.int32, (1, TQ), 1) + 1
    take_all = n_valid <= topk
    lo = jnp.min(lo_part, axis=0, keepdims=True)
    hi = jnp.max(hi_part, axis=0, keepdims=True)
    c_max = count_ge(hi)
    at_max = c_max >= kf
    state = (jnp.where(at_max, hi, lo), hi, jnp.where(at_max, c_max, n_valid.astype(F32)), c_max)

    def bisect(_, state):
        lo, hi, c_lo, c_hi = state
        mid = 0.5 * lo + 0.5 * hi
        c = count_ge(mid)
        ge = c >= kf
        return jnp.where(ge, mid, lo), jnp.where(ge, hi, mid), jnp.where(ge, c, c_lo), jnp.where(ge, c_hi, c)

    def unsettled(state):
        lo, hi, c_lo, _ = state
        open_q = jnp.where(take_all, 0.0, jnp.where(c_lo != kf, jnp.where(lo < hi, 1.0, 0.0), 0.0))

        def band_spread():
            b_min, b_max = minmax_blocks(lambda x: jnp.where(x >= lo, jnp.where(x < hi, x, BIG), BIG),
                                         lambda x: jnp.where(x >= lo, jnp.where(x < hi, x, -BIG), -BIG))
            return (jnp.max(jnp.where(b_max != b_min, open_q, 0.0)) > 0.0).astype(jnp.int32)

        return lax.cond(jnp.max(open_q) > 0.0, band_spread, lambda: jnp.int32(0))

    state = lax.fori_loop(0, jnp.where((i + 1) * TQ <= topk, 0, BISECT_WARMUP), bisect, state)

    def trip(c):
        n, state, _ = c
        state = lax.fori_loop(0, BISECT_TRIP, bisect, state)
        return n + 1, state, unsettled(state)

    _, state, _ = lax.while_loop(lambda c: (c[2] > 0) & (c[0] < BISECT_MAX_TRIPS), trip,
                                 (jnp.int32(0), state, unsettled(state)))
    lo, hi, c_lo, c_hi = state
    hi_ok = lo < hi
    c_above = jnp.where(hi_ok, c_hi, 0.0)
    hi_sel = jnp.where(hi_ok, hi, BIG)
    need = jnp.where(take_all, BIG, kf - c_above)
    lo_sel = jnp.where(take_all, HALF_NEG, lo)

    tied = jnp.max(jnp.where(take_all, 0.0, c_lo - kf)) > 0.0

    @pl.when(tied)
    def _():
        tril = tril_ref[...]

        def band_of(j):
            x = sc_ref[j]
            return x, jnp.where(x >= lo_sel, jnp.where(x < hi_sel, 1.0, 0.0), 0.0)

        def write(j, x, band, rank):
            sc_ref[j] = jnp.where(x >= hi_sel, 0.0,
                                  jnp.where(band * rank > 0.0, jnp.where(rank <= need, 0.0, NEG), NEG))
            return rank[TK - 1:TK, :]

        def write_mask(j, taken):
            x, band = band_of(j)
            return write(j, x, band, _dot(tril, band.astype(BF16)) + taken)

        def write_mask_pair(t, taken):
            (x_a, band_a), (x_b, band_b) = band_of(2 * t), band_of(2 * t + 1)
            in_a, in_b = _dot(tril, band_a.astype(BF16)), _dot(tril, band_b.astype(BF16))
            taken = write(2 * t, x_a, band_a, in_a + taken)
            return write(2 * t + 1, x_b, band_b, in_b + taken)

        taken = lax.fori_loop(0, nb // 2, write_mask_pair, jnp.zeros((1, TQ), F32))
        lax.fori_loop(2 * (nb // 2), nb, write_mask, taken)

    @pl.when(jnp.logical_not(tied))
    def _():
        def write_mask(j, _):
            sc_ref[j] = jnp.where(sc_ref[j] >= lo_sel, 0.0, NEG)
            return 0

        lax.fori_loop(0, nb, write_mask, 0)

    head_bits = _moba_block_choice(i, mq_ref, mk_ref, km_ref, nblk, topb)

    def logits(j):
        dsa = [_dot_nt(_k_block(k_ref, h, j), q_ref[h]) + bt_ref[h, jnp.minimum(i - j, 2)] + sc_ref[j]
               for h in range(N_HEADS)]
        moba = []
        for h in range(N_HEADS):
            picked = (lax.shift_right_logical(head_bits[h], jnp.full_like(head_bits[h], j)) & 1) == 1
            moba.append(_dot_nt(_k_block(mk_ref, h, j), mq_ref[h]) + mbt_ref[h, jnp.minimum(i - j, 2)]
                        + jnp.where(picked, 0.0, NEG))
        return dsa + moba

    def values(j):
        return [_vt_block(vt_ref, h, j) for h in range(N_HEADS)] + [_vt_block(mvt_ref, h, j) for h in range(N_HEADS)]

    carry = _softmax_loop(nb, logits, values, tuple(_softmax_init() for _ in range(2 * N_HEADS)), groups=(4, 2, 1))
    _store_heads(o_ref, [_softmax_out(c) for c in carry[:N_HEADS]])
    _store_heads(omb_ref, [_softmax_out(c) for c in carry[N_HEADS:]])


def _moba_block_choice(i, q_ref, k_ref, km_ref, nblk, topb):
    nrow = km_ref.shape[1]

    @pl.when(i == 0)
    def _():
        km_ref[...] = jnp.zeros_like(km_ref)
        for h in range(N_HEADS):
            for n in range(nblk):
                kb = k_ref[h, n * MOBA_BLOCK:(n + 1) * MOBA_BLOCK, :].astype(F32)
                km_ref[h, n:n + 1, :] = jnp.mean(kb, axis=0, keepdims=True)

    blk = lax.broadcasted_iota(jnp.int32, (nrow, TQ), 0)
    past = blk < i
    head_bits = []
    for h in range(N_HEADS):
        gate = _dot_nt(km_ref[h].astype(BF16), q_ref[h])
        bits = jnp.zeros((1, TQ), F32)
        for n in range(nblk):
            gn = gate[n:n + 1, :]
            beats = jnp.where(past, jnp.where(gate > gn, 1.0, jnp.where(gate == gn, jnp.where(blk < n, 1.0, 0.0), 0.0)), 0.0)
            rank = jnp.sum(beats, axis=0, keepdims=True)
            bits = bits + jnp.where(rank < float(topb), jnp.where(n < i, float(2 ** n), 0.0), 0.0)
        head_bits.append(bits.astype(jnp.int32) | lax.shift_left(jnp.int32(1), i))
    return head_bits


def _dsa_moba_attention(zz, vt, wt, b, s, bw_ds, bw_mb, tril):
    nq = s // TQ
    topk = min(DSA_TOPK_MAX, s // 4)
    nblk = s // MOBA_BLOCK
    topb = min(MOBA_TOPK, nblk - 1)
    out = jax.ShapeDtypeStruct((b * s, N_HEADS * HEAD_DIM), BF16)
    return pl.pallas_call(
        functools.partial(_dsa_moba_kernel, topk=topk, nblk=nblk, topb=topb),
        grid=(b, nq),
        in_specs=_attn_specs(nq, G_DS_Q, G_DS_K, M_DS, s) + [
            pl.BlockSpec((4, TQ, HEAD_DIM), lambda b_, i: (G_QI_A, b_ * nq + i, 0)),
            pl.BlockSpec((4, TQ, HEAD_DIM), lambda b_, i: (G_QI_B, b_ * nq + i, 0)),
            pl.BlockSpec((1, s, HEAD_DIM), lambda b_, i: (S_KIDX, b_, 0)),
            pl.BlockSpec((IDX_HEADS, TQ), lambda b_, i: (0, b_ * nq + i)),
            _const_spec(BIAS_ROWS_SHAPE), _const_spec((TK, TK))]
        + _attn_specs(nq, G_MB_Q, G_MB_K, M_MB, s) + [_const_spec(BIAS_ROWS_SHAPE)],
        out_specs=[_out_spec(nq), _out_spec(nq)],
        out_shape=[out, out],
        scratch_shapes=[pltpu.VMEM((nq, TK, TQ), F32), pltpu.VMEM(BIAS_TILES_SHAPE, F32),
                        pltpu.VMEM((N_HEADS, max(8, nblk), HEAD_DIM), F32), pltpu.VMEM(BIAS_TILES_SHAPE, F32)],
        compiler_params=_cparams(2),
        name="dsa_moba_attn",
    )(zz, zz, vt, zz, zz, zz, wt, bw_ds, tril, zz, zz, vt, bw_mb)


def _merge_kernel(x_ref, osb_ref, odf_ref, ods_ref, omb_ref, gpre_ref, wg_ref, wbr_ref, wout_ref, gpost_ref, o_ref):
    x = x_ref[...]
    h = _rms(x, gpre_ref[...]).astype(BF16)
    y = jnp.zeros((x.shape[0], D_MODEL), F32)
    for r, o_r in enumerate((osb_ref, odf_ref, ods_ref, omb_ref)):
        gate = jax.nn.sigmoid(_dot(h, wg_ref[:, r * D_MODEL:(r + 1) * D_MODEL]))
        y = y + gate * _dot(o_r[...], wbr_ref[r])
    o_ref[...] = x + _rms(_dot(y.astype(BF16), wout_ref[...]), gpost_ref[...])


def _merge(x, o_sb, o_df, o_ds, o_mb, g_pre, w_gate, w_br, w_out, g_post, layer):
    t = x.shape[0]
    tm = TM_MERGE
    tok = lambda width: pl.BlockSpec((tm, width), lambda i: (i, 0))
    return pl.pallas_call(
        _merge_kernel,
        grid=(t // tm,),
        in_specs=[tok(D_MODEL)] + [tok(MIXER_WIDTH)] * N_MIXERS + [
            _const_spec((1, D_MODEL)), _layer_spec((D_MODEL, N_MIXERS * D_MODEL), layer),
            _layer_spec((N_MIXERS, MIXER_WIDTH, D_MODEL), layer), _layer_spec((D_MODEL, D_MODEL), layer),
            _const_spec((1, D_MODEL))],
        out_specs=tok(D_MODEL),
        out_shape=jax.ShapeDtypeStruct((t, D_MODEL), F32),
        compiler_params=_cparams(1),
        name="merge",
    )(x, o_sb, o_df, o_ds, o_mb, g_pre, w_gate, w_br, w_out, g_post)


def _ffn_kernel(x_ref, gpre_ref, win_ref, wout_ref, gpost_ref, o_ref):
    x = x_ref[...]
    h = _rms(x, gpre_ref[...]).astype(BF16)
    gate = _dot(h, win_ref[:, 0:D_FF])
    up = _dot(h, win_ref[:, D_FF:2 * D_FF])
    act = (gate * jax.nn.sigmoid(gate) * up).astype(BF16)
    o_ref[...] = x + _rms(_dot(act, wout_ref[...]), gpost_ref[...])


def _ffn(x, g_pre, w_in, w_out, g_post, layer):
    t = x.shape[0]
    tm = TM_FFN
    return pl.pallas_call(
        _ffn_kernel,
        grid=(t // tm,),
        in_specs=[pl.BlockSpec((tm, D_MODEL), lambda i: (i, 0)), _const_spec((1, D_MODEL)),
                  _layer_spec((D_MODEL, 2 * D_FF), layer), _layer_spec((D_FF, D_MODEL), layer),
                  _const_spec((1, D_MODEL))],
        out_specs=pl.BlockSpec((tm, D_MODEL), lambda i: (i, 0)),
        out_shape=jax.ShapeDtypeStruct((t, D_MODEL), F32),
        compiler_params=_cparams(1),
        name="ffn",
    )(x, g_pre, w_in, w_out, g_post)


def _t5_bucket(dist):
    max_exact = N_BUCKETS // 2
    d = jnp.maximum(dist, 0)
    log_ratio = jnp.log(jnp.maximum(d, 1).astype(F32) / max_exact) / math.log(MAX_DISTANCE / max_exact)
    large = jnp.minimum(max_exact + (log_ratio * (N_BUCKETS - max_exact)).astype(jnp.int32), N_BUCKETS - 1)
    return jnp.where(d < max_exact, d, large)


def _bias_rows(rel_bias):
    assert TQ == TK
    n = TK
    d = np.arange(-(n - 1), 3 * n + 1)
    by_dist = rel_bias.astype(F32).T[:, _t5_bucket(jnp.asarray(np.maximum(d, 0), jnp.int32))]
    by_dist = jnp.where(jnp.asarray(d >= 0)[None, :], by_dist * LOG2E, NEG)
    return jnp.stack([by_dist[:, o * n:o * n + 2 * n] for o in range(3)], axis=1)[:, :, None, :]


def _pack_runs():
    runs, start = [], 0
    for e in range(1, N_PACK + 1):
        if e == N_PACK or _PACK_SRC[e] != _PACK_SRC[e - 1] + (1 if _PACK_SRC[e - 1] >= 0 else 0):
            runs.append((int(_PACK_SRC[start]), e - start))
            start = e
    return runs


_PACK_RUNS = _pack_runs()
PACK_ROWS = 128


def _pack_kernel(w_ref, pack_ref, gate_ref):
    w = w_ref[...]
    parts = [w[:, a:a + n] if a >= 0 else jnp.zeros((w.shape[0], n), F32) for a, n in _PACK_RUNS]
    pack_ref[...] = jnp.concatenate(parts, axis=1).astype(BF16)
    gate_ref[...] = w[:, _OFF["gate"]:].astype(BF16)


def _pack_weights(w_in):
    depth, d, n_in = w_in.shape
    row_block = lambda width: pl.BlockSpec((pl.Squeezed(), PACK_ROWS, width), lambda l, r: (l, r, 0))
    return pl.pallas_call(
        _pack_kernel,
        grid=(depth, d // PACK_ROWS),
        in_specs=[row_block(n_in)],
        out_specs=[row_block(N_PACK), row_block(N_MIXERS * D_MODEL)],
        out_shape=[jax.ShapeDtypeStruct((depth, d, N_PACK), BF16),
                   jax.ShapeDtypeStruct((depth, d, N_MIXERS * D_MODEL), BF16)],
        compiler_params=_cparams(2),
        name="pack_weights",
    )(w_in)


def kernel(x, w_in, w_br_sb, w_br_diff, w_br_dsa, w_br_moba, w_out, lambda_q1, lambda_k1, lambda_q2, lambda_k2,
           diff_subln_g, rel_bias, w_ffn_in, w_ffn_out, g_pre_mix, g_post_mix, g_pre_ffn, g_post_ffn):
    b, s, d = x.shape
    depth = w_in.shape[0]
    assert d == D_MODEL and s % TQ == 0 and s // MOBA_BLOCK >= 2
    t = b * s

    w_pack, w_gate = _pack_weights(w_in)
    w_br = jnp.stack([w_br_sb, w_br_diff, w_br_dsa, w_br_moba], axis=1).astype(BF16)
    w_o = w_out.astype(BF16)
    w_f1 = w_ffn_in.astype(BF16)
    w_f2 = w_ffn_out.astype(BF16)
    cs = jnp.asarray(_PACK_SCALE)[None, :]

    bw = _bias_rows(rel_bias)
    bw_df, bw_ds, bw_mb = bw[0:4], bw[4:8], bw[8:12]
    key = np.arange(TK)[:, None]
    qry = np.arange(TQ)[None, :]
    tri = jnp.asarray(key <= np.arange(TK)[None, :], BF16)
    tril = jnp.asarray(key >= np.arange(TK)[None, :], BF16)
    sb_mask = jnp.asarray(np.where(key < qry, 0.0, NEG), F32)

    xf = x.reshape(t, d)
    for l in range(depth):
        lam_init = 0.8 - 0.6 * math.exp(-0.3 * l)
        lamp = jnp.stack([lambda_q1[l], lambda_k1[l], lambda_q2[l], lambda_k2[l]]).astype(F32)
        cst = jnp.full((1, 128), lam_init, F32)
        zz, vt, wt = _proj(xf, g_pre_mix[l][None, :], w_pack, cs, l)
        o_sb = _sb_attention(zz, vt, b, s, tri, sb_mask)
        o_df = _diff_attention(zz, vt, b, s, bw_df, lamp, cst, diff_subln_g[l][:, None])
        o_ds, o_mb = _dsa_moba_attention(zz, vt, wt, b, s, bw_ds, bw_mb, tril)
        xf = _merge(xf, o_sb, o_df, o_ds, o_mb, g_pre_mix[l][None, :], w_gate, w_br, w_o, g_post_mix[l][None, :], l)
        xf = _ffn(xf, g_pre_ffn[l][None, :], w_f1, w_f2, g_post_ffn[l][None, :], l)
    return xf.reshape(b, s, d)
```

```python
import functools
import math

import numpy as np
import jax
import jax.numpy as jnp
from jax import lax
from jax.experimental import pallas as pl
from jax.experimental.pallas import tpu as pltpu

F32 = jnp.float32
BF16 = jnp.bfloat16

D_MODEL = 1024
HEAD_DIM = 64
N_HEADS = 4
N_MIXERS = 4
MIXER_WIDTH = N_HEADS * HEAD_DIM
DIFF_QK_DIM = 32
IDX_HEADS = 8
DSA_TOPK_MAX = 256
MOBA_BLOCK = 256
MOBA_TOPK = 3
N_BUCKETS = 32
MAX_DISTANCE = 128
D_FF = 2816
NORM_EPS = 1e-6

TQ = 256
TK = 256
NEG = -1e30
HALF_NEG = -0.5e30
BIG = 3e38
BISECT_WARMUP = 24
BISECT_TRIP = 2
BISECT_MAX_TRIPS = 134
LOG2E = math.log2(math.e)
SB_TAIL_CUTOFF = 120.0 * LOG2E
V_ROWS = 80
TM_PROJ = 512
TM_MERGE = 512
TM_FFN = 512
N_GROUP = 11
N_SLAB = 4 * N_GROUP
N_PACK = (N_GROUP + N_MIXERS) * MIXER_WIDTH
VMEM_LIMIT = 56 * 1024 * 1024

G_SB_Q, G_SB_K, G_DF_Q, G_DF_K, G_DS_Q, G_DS_K, G_QI_A, G_QI_B, G_MB_Q, G_MB_K, G_KIDX = range(N_GROUP)
S_KIDX = 4 * G_KIDX
M_SB, M_DF, M_DS, M_MB = range(N_MIXERS)


def _layout():
    off = {}
    acc = 0
    for name, sz in (("q_sb", 256), ("k_sb", 256), ("v_sb", 256), ("q1", 128), ("q2", 128), ("k1", 128),
                     ("k2", 128), ("v_df", 256), ("q_ds", 256), ("k_ds", 256), ("v_ds", 256), ("qi", 512),
                     ("ki", 64), ("wi", 8), ("q_mb", 256), ("k_mb", 256), ("v_mb", 256), ("gate", 4096)):
        off[name] = acc
        acc += sz
    return off


_OFF = _layout()


def _pack_layout():
    off = _OFF
    cols, scale = [], []

    def add(start, n, s=1.0):
        cols.extend(range(start, start + n))
        scale.extend([s] * n)

    hd = HEAD_DIM ** -0.5
    hd2 = hd * LOG2E
    df2 = DIFF_QK_DIM ** -0.5 * LOG2E
    add(off["q_sb"], 256, hd2); add(off["k_sb"], 256)
    for h in range(N_HEADS):
        add(off["q1"] + h * 32, 32, df2); add(off["q2"] + h * 32, 32, df2)
    for h in range(N_HEADS):
        add(off["k1"] + h * 32, 32); add(off["k2"] + h * 32, 32)
    add(off["q_ds"], 256, hd2); add(off["k_ds"], 256)
    add(off["qi"], 512, HEAD_DIM ** -0.5)
    add(off["q_mb"], 256, hd2); add(off["k_mb"], 256)
    add(off["ki"], 64); add(off["wi"], IDX_HEADS, IDX_HEADS ** -0.5)
    cols.extend([-1] * 184); scale.extend([1.0] * 184)
    for name in ("v_sb", "v_df", "v_ds", "v_mb"):
        add(off[name], MIXER_WIDTH)
    assert len(cols) == N_PACK
    return np.asarray(cols, np.int32), np.asarray(scale, np.float32)


_PACK_SRC, _PACK_SCALE = _pack_layout()


def _dot(a, b):
    return jnp.dot(a, b, preferred_element_type=F32)


def _dot_nt(a, b):
    return lax.dot_general(a, b, (((1,), (1,)), ((), ())), preferred_element_type=F32)


def _rms(x, g):
    return x * lax.rsqrt(jnp.mean(x * x, axis=-1, keepdims=True) + NORM_EPS) * g


def _cparams(n_axes):
    return pltpu.CompilerParams(dimension_semantics=("arbitrary",) * n_axes, vmem_limit_bytes=VMEM_LIMIT)


def _const_spec(shape):
    nd = len(shape)
    return pl.BlockSpec(shape, lambda *_: (0,) * nd, pipeline_mode=pl.Buffered(1))


def _layer_spec(shape, layer):
    nd = len(shape)
    return pl.BlockSpec((pl.Squeezed(),) + tuple(shape), lambda *_: (layer,) + (0,) * nd, pipeline_mode=pl.Buffered(1))


def _proj_kernel(x_ref, g_ref, w_ref, cs_ref, zz_ref, vt_ref, wt_ref):
    h = _rms(x_ref[...], g_ref[...]).astype(BF16)

    def group(c):
        cols = slice(c * MIXER_WIDTH, (c + 1) * MIXER_WIDTH)
        return _dot(h, w_ref[:, cols]) * cs_ref[:, cols]

    for c in range(N_GROUP):
        r = group(c)
        for s in range(4):
            zz_ref[4 * c + s] = r[:, s * HEAD_DIM:(s + 1) * HEAD_DIM].astype(BF16)
        if c == G_KIDX:
            wt_ref[...] = jnp.transpose(r)[HEAD_DIM:HEAD_DIM + IDX_HEADS]
    for m in range(N_MIXERS):
        v = group(N_GROUP + m)
        for u in range(TM_PROJ // TK):
            vt = jnp.transpose(v[u * TK:(u + 1) * TK]).astype(BF16)
            for hh in range(N_HEADS):
                vt_ref[m, u, hh * V_ROWS:hh * V_ROWS + HEAD_DIM, :] = vt[hh * HEAD_DIM:(hh + 1) * HEAD_DIM]
                vt_ref[m, u, hh * V_ROWS + HEAD_DIM:(hh + 1) * V_ROWS, :] = jnp.ones((V_ROWS - HEAD_DIM, TK), BF16)


def _proj(x, g, w, cs, layer):
    t = x.shape[0]
    tm = TM_PROJ
    return pl.pallas_call(
        _proj_kernel,
        grid=(t // tm,),
        in_specs=[pl.BlockSpec((tm, D_MODEL), lambda i: (i, 0)),
                  _const_spec((1, D_MODEL)),
                  _layer_spec((D_MODEL, N_PACK), layer),
                  _const_spec((1, N_PACK))],
        out_specs=[pl.BlockSpec((N_SLAB, tm, HEAD_DIM), lambda i: (0, i, 0)),
                   pl.BlockSpec((N_MIXERS, tm // TK, N_HEADS * V_ROWS, TK), lambda i: (0, i, 0, 0)),
                   pl.BlockSpec((IDX_HEADS, tm), lambda i: (0, i))],
        out_shape=[jax.ShapeDtypeStruct((N_SLAB, t, HEAD_DIM), BF16),
                   jax.ShapeDtypeStruct((N_MIXERS, t // TK, N_HEADS * V_ROWS, TK), BF16),
                   jax.ShapeDtypeStruct((IDX_HEADS, t), F32)],
        compiler_params=_cparams(1),
        name="proj",
    )(x, g, w, cs)


def _k_block(ref, h, j):
    return ref[h, pl.ds(pl.multiple_of(j * TK, TK), TK), :]


def _fold_keys(a, op):
    n = a.shape[0]
    while n > 8:
        n //= 2
        a = op(a[:n], a[n:2 * n])
    return a


def _vt_block(ref, h, j, rows=V_ROWS):
    return ref[0, j, h * V_ROWS:h * V_ROWS + rows, :]


def _softmax_block(s_list, vt_list, carry):
    ms = [jnp.maximum(c[0], jnp.max(_fold_keys(s, jnp.maximum), axis=0, keepdims=True))
          for s, c in zip(s_list, carry)]
    pvs = [_dot(vt, jnp.exp2(s - m).astype(BF16)) for vt, s, m in zip(vt_list, s_list, ms)]
    return tuple((m_new, jnp.exp2(m - m_new) * acc + pv) for (m, acc), m_new, pv in zip(carry, ms, pvs))


def _softmax_loop(n_blocks, logits, values, carry, groups=(2, 1)):
    assert groups[-1] == 1
    start = 0
    for g in groups:
        def body(t, carry, g=g, start=start):
            j = start + g * t
            s_all = [logits(j + u) for u in range(g)]
            for u in range(g):
                carry = _softmax_block(s_all[u], values(j + u), carry)
            return carry

        trips = (n_blocks - start) // g
        carry = lax.fori_loop(0, trips, body, carry)
        start = start + g * trips
    return carry


def _softmax_init():
    return (jnp.full((1, TQ), NEG, F32), jnp.zeros((V_ROWS, TQ), F32))


def _softmax_out(carry):
    _, acc = carry
    return acc[:HEAD_DIM] / acc[HEAD_DIM:HEAD_DIM + 1]


def _store_heads(o_ref, heads_t):
    o_ref[...] = jnp.transpose(jnp.concatenate(heads_t, axis=0)).astype(BF16)


BIAS_ROWS_SHAPE = (N_HEADS, 3, 1, 2 * TQ)
BIAS_TILES_SHAPE = (N_HEADS, 3, TK, TQ)


def _fill_bias_tiles(bw_ref, bt_ref):
    @pl.when((pl.program_id(0) == 0) & (pl.program_id(1) == 0))
    def _():
        for h in range(N_HEADS):
            for o in range(3):
                rows = jnp.broadcast_to(bw_ref[h, o], (TK, 2 * TQ))
                bt_ref[h, o] = pltpu.roll(rows, TQ + 1, 1, stride=1, stride_axis=0)[:, :TQ]


def _attn_specs(nq, gq, gk, mixer, s):
    return [pl.BlockSpec((4, TQ, HEAD_DIM), lambda b, i: (gq, b * nq + i, 0)),
            pl.BlockSpec((4, s, HEAD_DIM), lambda b, i: (gk, b, 0)),
            pl.BlockSpec((1, s // TK, N_HEADS * V_ROWS, TK), lambda b, i: (mixer, b, 0, 0))]


def _out_spec(nq):
    return pl.BlockSpec((TQ, N_HEADS * HEAD_DIM), lambda b, i: (b * nq + i, 0))


def _sb_kernel(q_ref, k_ref, vt_ref, tri_ref, mask_ref, o_ref):
    i = pl.program_id(1)
    tri = tri_ref[...]

    def blocks(js, carry, first_masked):
        heads = range(N_HEADS)
        zs = [[_dot_nt(_k_block(k_ref, h, j), q_ref[h]) for h in heads] for j in js]
        if first_masked:
            zs[0] = [z + mask_ref[...] for z in zs[0]]
        sps = [[jnp.maximum(z, 0.0) + jnp.log2(1.0 + jnp.exp2(-jnp.abs(z))) for z in zb] for zb in zs]
        his = [[sp.astype(BF16) for sp in sb] for sb in sps]
        los = [[(sp - hi.astype(F32)).astype(BF16) for sp, hi in zip(sb, hb)] for sb, hb in zip(sps, his)]
        cums = [[_dot(tri, hi) + _dot(tri, lo) for hi, lo in zip(hb, lb)] for hb, lb in zip(his, los)]
        out = []
        for h in heads:
            tail, acc = carry[h]
            for u, j in enumerate(js):
                c = cums[u][h] + tail
                acc = acc + _dot(_vt_block(vt_ref, h, j, HEAD_DIM), jnp.exp2(zs[u][h] - c).astype(BF16))
                tail = c[0:1, :]
            out.append((tail, acc))
        return tuple(out)

    init = tuple((jnp.zeros((1, TQ), F32), jnp.zeros((HEAD_DIM, TQ), F32)) for _ in range(N_HEADS))
    carry = lax.cond(i >= 1, lambda c: blocks([i, i - 1], c, True), lambda c: blocks([i], c, True), init)

    def weights_left(carry):
        tail = functools.reduce(jnp.minimum, [c[0] for c in carry])
        return (jnp.min(tail) < SB_TAIL_CUTOFF).astype(jnp.int32)

    def earlier_block(c):
        j, carry, _ = c
        carry = blocks([j], carry, False)
        return j - 1, carry, weights_left(carry)

    _, carry, _ = lax.while_loop(lambda c: (c[0] >= 0) & (c[2] > 0), earlier_block, (i - 2, carry, weights_left(carry)))
    _store_heads(o_ref, [c[1] for c in carry])


def _sb_attention(zz, vt, b, s, tri, mask):
    nq = s // TQ
    return pl.pallas_call(
        _sb_kernel,
        grid=(b, nq),
        in_specs=_attn_specs(nq, G_SB_Q, G_SB_K, M_SB, s) + [_const_spec((TK, TK)), _const_spec((TK, TQ))],
        out_specs=_out_spec(nq),
        out_shape=jax.ShapeDtypeStruct((b * s, N_HEADS * HEAD_DIM), BF16),
        compiler_params=_cparams(2),
        name="sb_attn",
    )(zz, zz, vt, tri, mask)


def _diff_kernel(q_ref, k_ref, vt_ref, bw_ref, lam_ref, cst_ref, g_ref, o_ref, bt_ref):
    i = pl.program_id(1)
    _fill_bias_tiles(bw_ref, bt_ref)
    lp = lam_ref[...]
    lam_init = cst_ref[:, 0:1]
    lam = (jnp.exp(jnp.sum(lp[0:1] * lp[1:2], axis=-1, keepdims=True))
           - jnp.exp(jnp.sum(lp[2:3] * lp[3:4], axis=-1, keepdims=True)) + lam_init)
    lane = lax.broadcasted_iota(jnp.int32, (TQ, HEAD_DIM), 1)
    qs = []
    for h in range(N_HEADS):
        q = q_ref[h]
        qs.append((jnp.where(lane < DIFF_QK_DIM, q, jnp.zeros_like(q)),
                   jnp.where(lane >= DIFF_QK_DIM, q, jnp.zeros_like(q))))

    def logits(j):
        s_list = []
        for h in range(N_HEADS):
            kj = _k_block(k_ref, h, j)
            bias = bt_ref[h, jnp.minimum(i - j, 2)]
            s_list += [_dot_nt(kj, qs[h][0]) + bias, _dot_nt(kj, qs[h][1]) + bias]
        return s_list

    def values(j):
        return [_vt_block(vt_ref, h, j) for h in range(N_HEADS) for _ in range(2)]

    carry = _softmax_loop(i + 1, logits, values, tuple(_softmax_init() for _ in range(2 * N_HEADS)), groups=(4, 2, 1))
    outs = []
    for h in range(N_HEADS):
        o = _softmax_out(carry[2 * h]) - lam * _softmax_out(carry[2 * h + 1])
        o = o * lax.rsqrt(jnp.mean(o * o, axis=0, keepdims=True) + NORM_EPS) * g_ref[...]
        outs.append(o * (1.0 - lam_init))
    _store_heads(o_ref, outs)


def _diff_attention(zz, vt, b, s, bw, lamp, cst, g):
    nq = s // TQ
    return pl.pallas_call(
        _diff_kernel,
        grid=(b, nq),
        in_specs=_attn_specs(nq, G_DF_Q, G_DF_K, M_DF, s) + [
            _const_spec(BIAS_ROWS_SHAPE), _const_spec((4, DIFF_QK_DIM)), _const_spec((1, 128)),
            _const_spec((HEAD_DIM, 1))],
        out_specs=_out_spec(nq),
        out_shape=jax.ShapeDtypeStruct((b * s, N_HEADS * HEAD_DIM), BF16),
        scratch_shapes=[pltpu.VMEM(BIAS_TILES_SHAPE, F32)],
        compiler_params=_cparams(2),
        name="diff_attn",
    )(zz, zz, vt, bw, lamp, cst, g)


def _dsa_moba_kernel(q_ref, k_ref, vt_ref, qia_ref, qib_ref, ki_ref, wt_ref, bw_ref, tril_ref,
                     mq_ref, mk_ref, mvt_ref, mbw_ref, o_ref, omb_ref, sc_ref, bt_ref, km_ref, mbt_ref, *,
                     topk, nblk, topb):
    i = pl.program_id(1)
    _fill_bias_tiles(bw_ref, bt_ref)
    _fill_bias_tiles(mbw_ref, mbt_ref)
    nb = i + 1
    kf = float(topk)
    w = wt_ref[...]
    key = lax.broadcasted_iota(jnp.int32, (TK, TQ), 0)
    qry = lax.broadcasted_iota(jnp.int32, (TK, TQ), 1)

    def score(j):
        kij = _k_block(ki_ref, 0, j)
        sc = jnp.zeros((TK, TQ), F32)
        for hh in range(IDX_HEADS):
            qi = (qia_ref if hh < 4 else qib_ref)[hh % 4]
            sc = sc + w[hh:hh + 1, :] * jnp.maximum(_dot_nt(kij, qi), 0.0)
        return sc

    def scored(j):
        sc = score(j)
        valid = key - qry <= (i - j) * TQ
        masked = jnp.where(valid, sc, NEG)
        sc_ref[j] = masked
        return masked, jnp.where(valid, sc, BIG)

    c = (jnp.full((8, TQ), BIG, F32), jnp.full((8, TQ), -BIG, F32))
    start = 0
    for g in (4, 2, 1):
        def body(t, c, g=g, start=start):
            tiles = [scored(start + g * t + u) for u in range(g)]
            for masked, lo_src in tiles:
                c = (jnp.minimum(c[0], _fold_keys(lo_src, jnp.minimum)),
                     jnp.maximum(c[1], _fold_keys(masked, jnp.maximum)))
            return c

        trips = (nb - start) // g
        c = lax.fori_loop(0, trips, body, c)
        start = start + g * trips
    lo_part, hi_part = c

    def reduce_blocks(fn, init):
        def pair(t, c):
            return fn(sc_ref[2 * t + 1], 2 * t + 1, fn(sc_ref[2 * t], 2 * t, c))
        c = lax.fori_loop(0, nb // 2, pair, init)
        return lax.fori_loop(2 * (nb // 2), nb, lambda j, c: fn(sc_ref[j], j, c), c)

    def count_ge(t):
        part = reduce_blocks(lambda x, j, c: c + _fold_keys(jnp.where(x >= t, 1.0, 0.0), jnp.add),
                             jnp.zeros((8, TQ), F32))
        return jnp.sum(part, axis=0, keepdims=True)

    def minmax_blocks(lo_of, hi_of):
        def f(x, j, c):
            return (jnp.minimum(c[0], _fold_keys(lo_of(x), jnp.minimum)),
                    jnp.maximum(c[1], _fold_keys(hi_of(x), jnp.maximum)))
        lo_part, hi_part = reduce_blocks(f, (jnp.full((8, TQ), BIG, F32), jnp.full((8, TQ), -BIG, F32)))
        return jnp.min(lo_part, axis=0, keepdims=True), jnp.max(hi_part, axis=0, keepdims=True)

    n_valid = i * TQ + lax.broadcasted_iota(jnp.int32, (1, TQ), 1) + 1
    take_all = n_valid <= topk
    lo = jnp.min(lo_part, axis=0, keepdims=True)
    hi = jnp.max(hi_part, axis=0, keepdims=True)
    c_max = count_ge(hi)
    at_max = c_max >= kf
    state = (jnp.where(at_max, hi, lo), hi, jnp.where(at_max, c_max, n_valid.astype(F32)), c_max)

    def bisect(_, state):
        lo, hi, c_lo, c_hi = state
        mid = 0.5 * lo + 0.5 * hi
        c = count_ge(mid)
        ge = c >= kf
        return jnp.where(ge, mid, lo), jnp.where(ge, hi, mid), jnp.where(ge, c, c_lo), jnp.where(ge, c_hi, c)

    def unsettled(state):
        lo, hi, c_lo, _ = state
        open_q = jnp.where(take_all, 0.0, jnp.where(c_lo != kf, jnp.where(lo < hi, 1.0, 0.0), 0.0))

        def band_spread():
            b_min, b_max = minmax_blocks(lambda x: jnp.where(x >= lo, jnp.where(x < hi, x, BIG), BIG),
                                         lambda x: jnp.where(x >= lo, jnp.where(x < hi, x, -BIG), -BIG))
            return (jnp.max(jnp.where(b_max != b_min, open_q, 0.0)) > 0.0).astype(jnp.int32)

        return lax.cond(jnp.max(open_q) > 0.0, band_spread, lambda: jnp.int32(0))

    state = lax.fori_loop(0, jnp.where((i + 1) * TQ <= topk, 0, BISECT_WARMUP), bisect, state)

    def trip(c):
        n, state, _ = c
        state = lax.fori_loop(0, BISECT_TRIP, bisect, state)
        return n + 1, state, unsettled(state)

    _, state, _ = lax.while_loop(lambda c: (c[2] > 0) & (c[0] < BISECT_MAX_TRIPS), trip,
                                 (jnp.int32(0), state, unsettled(state)))
    lo, hi, c_lo, c_hi = state
    hi_ok = lo < hi
    c_above = jnp.where(hi_ok, c_hi, 0.0)
    hi_sel = jnp.where(hi_ok, hi, BIG)
    need = jnp.where(take_all, BIG, kf - c_above)
    lo_sel = jnp.where(take_all, HALF_NEG, lo)

    tied = jnp.max(jnp.where(take_all, 0.0, c_lo - kf)) > 0.0

    @pl.when(tied)
    def _():
        tril = tril_ref[...]

        def band_of(j):
            x = sc_ref[j]
            return x, jnp.where(x >= lo_sel, jnp.where(x < hi_sel, 1.0, 0.0), 0.0)

        def write(j, x, band, rank):
            sc_ref[j] = jnp.where(x >= hi_sel, 0.0,
                                  jnp.where(band * rank > 0.0, jnp.where(rank <= need, 0.0, NEG), NEG))
            return rank[TK - 1:TK, :]

        def write_mask(j, taken):
            x, band = band_of(j)
            return write(j, x, band, _dot(tril, band.astype(BF16)) + taken)

        def write_mask_pair(t, taken):
            (x_a, band_a), (x_b, band_b) = band_of(2 * t), band_of(2 * t + 1)
            in_a, in_b = _dot(tril, band_a.astype(BF16)), _dot(tril, band_b.astype(BF16))
            taken = write(2 * t, x_a, band_a, in_a + taken)
            return write(2 * t + 1, x_b, band_b, in_b + taken)

        taken = lax.fori_loop(0, nb // 2, write_mask_pair, jnp.zeros((1, TQ), F32))
        lax.fori_loop(2 * (nb // 2), nb, write_mask, taken)

    @pl.when(jnp.logical_not(tied))
    def _():
        def write_mask(j, _):
            sc_ref[j] = jnp.where(sc_ref[j] >= lo_sel, 0.0, NEG)
            return 0

        lax.fori_loop(0, nb, write_mask, 0)

    head_bits = _moba_block_choice(i, mq_ref, mk_ref, km_ref, nblk, topb)

    def logits(j):
        dsa = [_dot_nt(_k_block(k_ref, h, j), q_ref[h]) + bt_ref[h, jnp.minimum(i - j, 2)] + sc_ref[j]
               for h in range(N_HEADS)]
        moba = []
        for h in range(N_HEADS):
            picked = (lax.shift_right_logical(head_bits[h], jnp.full_like(head_bits[h], j)) & 1) == 1
            moba.append(_dot_nt(_k_block(mk_ref, h, j), mq_ref[h]) + mbt_ref[h, jnp.minimum(i - j, 2)]
                        + jnp.where(picked, 0.0, NEG))
        return dsa + moba

    def values(j):
        return [_vt_block(vt_ref, h, j) for h in range(N_HEADS)] + [_vt_block(mvt_ref, h, j) for h in range(N_HEADS)]

    carry = _softmax_loop(nb, logits, values, tuple(_softmax_init() for _ in range(2 * N_HEADS)), groups=(4, 2, 1))
    _store_heads(o_ref, [_softmax_out(c) for c in carry[:N_HEADS]])
    _store_heads(omb_ref, [_softmax_out(c) for c in carry[N_HEADS:]])


def _moba_block_choice(i, q_ref, k_ref, km_ref, nblk, topb):
    nrow = km_ref.shape[1]

    @pl.when(i == 0)
    def _():
        km_ref[...] = jnp.zeros_like(km_ref)
        for h in range(N_HEADS):
            for n in range(nblk):
                kb = k_ref[h, n * MOBA_BLOCK:(n + 1) * MOBA_BLOCK, :].astype(F32)
                km_ref[h, n:n + 1, :] = jnp.mean(kb, axis=0, keepdims=True)

    blk = lax.broadcasted_iota(jnp.int32, (nrow, TQ), 0)
    past = blk < i
    head_bits = []
    for h in range(N_HEADS):
        gate = _dot_nt(km_ref[h].astype(BF16), q_ref[h])
        bits = jnp.zeros((1, TQ), F32)
        for n in range(nblk):
            gn = gate[n:n + 1, :]
            beats = jnp.where(past, jnp.where(gate > gn, 1.0, jnp.where(gate == gn, jnp.where(blk < n, 1.0, 0.0), 0.0)), 0.0)
            rank = jnp.sum(beats, axis=0, keepdims=True)
            bits = bits + jnp.where(rank < float(topb), jnp.where(n < i, float(2 ** n), 0.0), 0.0)
        head_bits.append(bits.astype(jnp.int32) | lax.shift_left(jnp.int32(1), i))
    return head_bits


def _dsa_moba_attention(zz, vt, wt, b, s, bw_ds, bw_mb, tril):
    nq = s // TQ
    topk = min(DSA_TOPK_MAX, s // 4)
    nblk = s // MOBA_BLOCK
    topb = min(MOBA_TOPK, nblk - 1)
    out = jax.ShapeDtypeStruct((b * s, N_HEADS * HEAD_DIM), BF16)
    return pl.pallas_call(
        functools.partial(_dsa_moba_kernel, topk=topk, nblk=nblk, topb=topb),
        grid=(b, nq),
        in_specs=_attn_specs(nq, G_DS_Q, G_DS_K, M_DS, s) + [
            pl.BlockSpec((4, TQ, HEAD_DIM), lambda b_, i: (G_QI_A, b_ * nq + i, 0)),
            pl.BlockSpec((4, TQ, HEAD_DIM), lambda b_, i: (G_QI_B, b_ * nq + i, 0)),
            pl.BlockSpec((1, s, HEAD_DIM), lambda b_, i: (S_KIDX, b_, 0)),
            pl.BlockSpec((IDX_HEADS, TQ), lambda b_, i: (0, b_ * nq + i)),
            _const_spec(BIAS_ROWS_SHAPE), _const_spec((TK, TK))]
        + _attn_specs(nq, G_MB_Q, G_MB_K, M_MB, s) + [_const_spec(BIAS_ROWS_SHAPE)],
        out_specs=[_out_spec(nq), _out_spec(nq)],
        out_shape=[out, out],
        scratch_shapes=[pltpu.VMEM((nq, TK, TQ), F32), pltpu.VMEM(BIAS_TILES_SHAPE, F32),
                        pltpu.VMEM((N_HEADS, max(8, nblk), HEAD_DIM), F32), pltpu.VMEM(BIAS_TILES_SHAPE, F32)],
        compiler_params=_cparams(2),
        name="dsa_moba_attn",
    )(zz, zz, vt, zz, zz, zz, wt, bw_ds, tril, zz, zz, vt, bw_mb)


def _merge_kernel(x_ref, osb_ref, odf_ref, ods_ref, omb_ref, gpre_ref, wg_ref, wbr_ref, wout_ref, gpost_ref, o_ref):
    x = x_ref[...]
    h = _rms(x, gpre_ref[...]).astype(BF16)
    y = jnp.zeros((x.shape[0], D_MODEL), F32)
    for r, o_r in enumerate((osb_ref, odf_ref, ods_ref, omb_ref)):
        gate = jax.nn.sigmoid(_dot(h, wg_ref[:, r * D_MODEL:(r + 1) * D_MODEL]))
        y = y + gate * _dot(o_r[...], wbr_ref[r])
    o_ref[...] = x + _rms(_dot(y.astype(BF16), wout_ref[...]), gpost_ref[...])


def _merge(x, o_sb, o_df, o_ds, o_mb, g_pre, w_gate, w_br, w_out, g_post, layer):
    t = x.shape[0]
    tm = TM_MERGE
    tok = lambda width: pl.BlockSpec((tm, width), lambda i: (i, 0))
    return pl.pallas_call(
        _merge_kernel,
        grid=(t // tm,),
        in_specs=[tok(D_MODEL)] + [tok(MIXER_WIDTH)] * N_MIXERS + [
            _const_spec((1, D_MODEL)), _layer_spec((D_MODEL, N_MIXERS * D_MODEL), layer),
            _layer_spec((N_MIXERS, MIXER_WIDTH, D_MODEL), layer), _layer_spec((D_MODEL, D_MODEL), layer),
            _const_spec((1, D_MODEL))],
        out_specs=tok(D_MODEL),
        out_shape=jax.ShapeDtypeStruct((t, D_MODEL), F32),
        compiler_params=_cparams(1),
        name="merge",
    )(x, o_sb, o_df, o_ds, o_mb, g_pre, w_gate, w_br, w_out, g_post)


def _ffn_kernel(x_ref, gpre_ref, win_ref, wout_ref, gpost_ref, o_ref):
    x = x_ref[...]
    h = _rms(x, gpre_ref[...]).astype(BF16)
    gate = _dot(h, win_ref[:, 0:D_FF])
    up = _dot(h, win_ref[:, D_FF:2 * D_FF])
    act = (gate * jax.nn.sigmoid(gate) * up).astype(BF16)
    o_ref[...] = x + _rms(_dot(act, wout_ref[...]), gpost_ref[...])


def _ffn(x, g_pre, w_in, w_out, g_post, layer):
    t = x.shape[0]
    tm = TM_FFN
    return pl.pallas_call(
        _ffn_kernel,
        grid=(t // tm,),
        in_specs=[pl.BlockSpec((tm, D_MODEL), lambda i: (i, 0)), _const_spec((1, D_MODEL)),
                  _layer_spec((D_MODEL, 2 * D_FF), layer), _layer_spec((D_FF, D_MODEL), layer),
                  _const_spec((1, D_MODEL))],
        out_specs=pl.BlockSpec((tm, D_MODEL), lambda i: (i, 0)),
        out_shape=jax.ShapeDtypeStruct((t, D_MODEL), F32),
        compiler_params=_cparams(1),
        name="ffn",
    )(x, g_pre, w_in, w_out, g_post)


def _t5_bucket(dist):
    max_exact = N_BUCKETS // 2
    d = jnp.maximum(dist, 0)
    log_ratio = jnp.log(jnp.maximum(d, 1).astype(F32) / max_exact) / math.log(MAX_DISTANCE / max_exact)
    large = jnp.minimum(max_exact + (log_ratio * (N_BUCKETS - max_exact)).astype(jnp.int32), N_BUCKETS - 1)
    return jnp.where(d < max_exact, d, large)


def _bias_rows(rel_bias):
    assert TQ == TK
    n = TK
    d = np.arange(-(n - 1), 3 * n + 1)
    by_dist = rel_bias.astype(F32).T[:, _t5_bucket(jnp.asarray(np.maximum(d, 0), jnp.int32))]
    by_dist = jnp.where(jnp.asarray(d >= 0)[None, :], by_dist * LOG2E, NEG)
    return jnp.stack([by_dist[:, o * n:o * n + 2 * n] for o in range(3)], axis=1)[:, :, None, :]


def _pack_runs():
    runs, start = [], 0
    for e in range(1, N_PACK + 1):
        if e == N_PACK or _PACK_SRC[e] != _PACK_SRC[e - 1] + (1 if _PACK_SRC[e - 1] >= 0 else 0):
            runs.append((int(_PACK_SRC[start]), e - start))
            start = e
    return runs


_PACK_RUNS = _pack_runs()
PACK_ROWS = 128


def _pack_kernel(w_ref, pack_ref, gate_ref):
    w = w_ref[...]
    parts = [w[:, a:a + n] if a >= 0 else jnp.zeros((w.shape[0], n), F32) for a, n in _PACK_RUNS]
    pack_ref[...] = jnp.concatenate(parts, axis=1).astype(BF16)
    gate_ref[...] = w[:, _OFF["gate"]:].astype(BF16)


def _pack_weights(w_in):
    depth, d, n_in = w_in.shape
    row_block = lambda width: pl.BlockSpec((pl.Squeezed(), PACK_ROWS, width), lambda l, r: (l, r, 0))
    return pl.pallas_call(
        _pack_kernel,
        grid=(depth, d // PACK_ROWS),
        in_specs=[row_block(n_in)],
        out_specs=[row_block(N_PACK), row_block(N_MIXERS * D_MODEL)],
        out_shape=[jax.ShapeDtypeStruct((depth, d, N_PACK), BF16),
                   jax.ShapeDtypeStruct((depth, d, N_MIXERS * D_MODEL), BF16)],
        compiler_params=_cparams(2),
        name="pack_weights",
    )(w_in)


def kernel(x, w_in, w_br_sb, w_br_diff, w_br_dsa, w_br_moba, w_out, lambda_q1, lambda_k1, lambda_q2, lambda_k2,
           diff_subln_g, rel_bias, w_ffn_in, w_ffn_out, g_pre_mix, g_post_mix, g_pre_ffn, g_post_ffn):
    b, s, d = x.shape
    depth = w_in.shape[0]
    assert d == D_MODEL and s % TQ == 0 and s // MOBA_BLOCK >= 2
    t = b * s

    w_pack, w_gate = _pack_weights(w_in)
    w_br = jnp.stack([w_br_sb, w_br_diff, w_br_dsa, w_br_moba], axis=1).astype(BF16)
    w_o = w_out.astype(BF16)
    w_f1 = w_ffn_in.astype(BF16)
    w_f2 = w_ffn_out.astype(BF16)
    cs = jnp.asarray(_PACK_SCALE)[None, :]

    bw = _bias_rows(rel_bias)
    bw_df, bw_ds, bw_mb = bw[0:4], bw[4:8], bw[8:12]
    key = np.arange(TK)[:, None]
    qry = np.arange(TQ)[None, :]
    tri = jnp.asarray(key <= np.arange(TK)[None, :], BF16)
    tril = jnp.asarray(key >= np.arange(TK)[None, :], BF16)
    sb_mask = jnp.asarray(np.where(key < qry, 0.0, NEG), F32)

    xf = x.reshape(t, d)
    for l in range(depth):
        lam_init = 0.8 - 0.6 * math.exp(-0.3 * l)
        lamp = jnp.stack([lambda_q1[l], lambda_k1[l], lambda_q2[l], lambda_k2[l]]).astype(F32)
        cst = jnp.full((1, 128), lam_init, F32)
        zz, vt, wt = _proj(xf, g_pre_mix[l][None, :], w_pack, cs, l)
        o_sb = _sb_attention(zz, vt, b, s, tri, sb_mask)
        o_df = _diff_attention(zz, vt, b, s, bw_df, lamp, cst, diff_subln_g[l][:, None])
        o_ds, o_mb = _dsa_moba_attention(zz, vt, wt, b, s, bw_ds, bw_mb, tril)
        xf = _merge(xf, o_sb, o_df, o_ds, o_mb, g_pre_mix[l][None, :], w_gate, w_br, w_o, g_post_mix[l][None, :], l)
        xf = _ffn(xf, g_pre_ffn[l][None, :], w_f1, w_f2, g_post_ffn[l][None, :], l)
    return xf.reshape(b, s, d)
```

```python
import functools
import math

import numpy as np
import jax
import jax.numpy as jnp
from jax import lax
from jax.experimental import pallas as pl
from jax.experimental.pallas import tpu as pltpu

F32 = jnp.float32
BF16 = jnp.bfloat16

D_MODEL = 1024
HEAD_DIM = 64
N_HEADS = 4
N_MIXERS = 4
MIXER_WIDTH = N_HEADS * HEAD_DIM
DIFF_QK_DIM = 32
IDX_HEADS = 8
DSA_TOPK_MAX = 256
MOBA_BLOCK = 256
MOBA_TOPK = 3
N_BUCKETS = 32
MAX_DISTANCE = 128
D_FF = 2816
NORM_EPS = 1e-6

TQ = 256
TK = 256
NEG = -1e30
HALF_NEG = -0.5e30
BIG = 3e38
BISECT_WARMUP = 20
BISECT_TRIP = 2
BISECT_MAX_TRIPS = 134
BISECT_COUNT_ONLY_TRIPS = 6
LOG2E = math.log2(math.e)
SB_TAIL_CUTOFF = 120.0 * LOG2E
V_ROWS = 80
TM_PROJ = 512
TM_MERGE = 512
TM_FFN = 512
N_GROUP = 11
N_SLAB = 4 * N_GROUP
N_PACK = (N_GROUP + N_MIXERS) * MIXER_WIDTH
VMEM_LIMIT = 56 * 1024 * 1024

G_SB_Q, G_SB_K, G_DF_Q, G_DF_K, G_DS_Q, G_DS_K, G_QI_A, G_QI_B, G_MB_Q, G_MB_K, G_KIDX = range(N_GROUP)
S_KIDX = 4 * G_KIDX
M_SB, M_DF, M_DS, M_MB = range(N_MIXERS)


def _layout():
    off = {}
    acc = 0
    for name, sz in (("q_sb", 256), ("k_sb", 256), ("v_sb", 256), ("q1", 128), ("q2", 128), ("k1", 128),
                     ("k2", 128), ("v_df", 256), ("q_ds", 256), ("k_ds", 256), ("v_ds", 256), ("qi", 512),
                     ("ki", 64), ("wi", 8), ("q_mb", 256), ("k_mb", 256), ("v_mb", 256), ("gate", 4096)):
        off[name] = acc
        acc += sz
    return off


_OFF = _layout()


def _pack_layout():
    off = _OFF
    cols, scale = [], []

    def add(start, n, s=1.0):
        cols.extend(range(start, start + n))
        scale.extend([s] * n)

    hd = HEAD_DIM ** -0.5
    hd2 = hd * LOG2E
    df2 = DIFF_QK_DIM ** -0.5 * LOG2E
    add(off["q_sb"], 256, hd2); add(off["k_sb"], 256)
    for h in range(N_HEADS):
        add(off["q1"] + h * 32, 32, df2); add(off["q2"] + h * 32, 32, df2)
    for h in range(N_HEADS):
        add(off["k1"] + h * 32, 32); add(off["k2"] + h * 32, 32)
    add(off["q_ds"], 256, hd2); add(off["k_ds"], 256)
    add(off["qi"], 512, HEAD_DIM ** -0.5)
    add(off["q_mb"], 256, hd2); add(off["k_mb"], 256)
    add(off["ki"], 64); add(off["wi"], IDX_HEADS, IDX_HEADS ** -0.5)
    cols.extend([-1] * 184); scale.extend([1.0] * 184)
    for name in ("v_sb", "v_df", "v_ds", "v_mb"):
        add(off[name], MIXER_WIDTH)
    assert len(cols) == N_PACK
    return np.asarray(cols, np.int32), np.asarray(scale, np.float32)


_PACK_SRC, _PACK_SCALE = _pack_layout()


def _dot(a, b):
    return jnp.dot(a, b, preferred_element_type=F32)


def _dot_nt(a, b):
    return lax.dot_general(a, b, (((1,), (1,)), ((), ())), preferred_element_type=F32)


def _rms(x, g):
    return x * lax.rsqrt(jnp.mean(x * x, axis=-1, keepdims=True) + NORM_EPS) * g


def _cparams(n_axes):
    return pltpu.CompilerParams(dimension_semantics=("arbitrary",) * n_axes, vmem_limit_bytes=VMEM_LIMIT)


def _const_spec(shape):
    nd = len(shape)
    return pl.BlockSpec(shape, lambda *_: (0,) * nd, pipeline_mode=pl.Buffered(1))


def _layer_spec(shape, layer):
    nd = len(shape)
    return pl.BlockSpec((pl.Squeezed(),) + tuple(shape), lambda *_: (layer,) + (0,) * nd, pipeline_mode=pl.Buffered(1))


def _proj_kernel(x_ref, g_ref, w_ref, cs_ref, zz_ref, vt_ref, wt_ref):
    h = _rms(x_ref[...], g_ref[...]).astype(BF16)

    def group(c):
        cols = slice(c * MIXER_WIDTH, (c + 1) * MIXER_WIDTH)
        return _dot(h, w_ref[:, cols]) * cs_ref[:, cols]

    for c in range(N_GROUP):
        r = group(c)
        for s in range(4):
            zz_ref[4 * c + s] = r[:, s * HEAD_DIM:(s + 1) * HEAD_DIM].astype(BF16)
        if c == G_KIDX:
            wt_ref[...] = jnp.transpose(r)[HEAD_DIM:HEAD_DIM + IDX_HEADS]
    for m in range(N_MIXERS):
        v = group(N_GROUP + m)
        for u in range(TM_PROJ // TK):
            vt = jnp.transpose(v[u * TK:(u + 1) * TK]).astype(BF16)
            for hh in range(N_HEADS):
                vt_ref[m, u, hh * V_ROWS:hh * V_ROWS + HEAD_DIM, :] = vt[hh * HEAD_DIM:(hh + 1) * HEAD_DIM]
                vt_ref[m, u, hh * V_ROWS + HEAD_DIM:(hh + 1) * V_ROWS, :] = jnp.ones((V_ROWS - HEAD_DIM, TK), BF16)


def _proj(x, g, w, cs, layer):
    t = x.shape[0]
    tm = TM_PROJ
    return pl.pallas_call(
        _proj_kernel,
        grid=(t // tm,),
        in_specs=[pl.BlockSpec((tm, D_MODEL), lambda i: (i, 0)),
                  _const_spec((1, D_MODEL)),
                  _layer_spec((D_MODEL, N_PACK), layer),
                  _const_spec((1, N_PACK))],
        out_specs=[pl.BlockSpec((N_SLAB, tm, HEAD_DIM), lambda i: (0, i, 0)),
                   pl.BlockSpec((N_MIXERS, tm // TK, N_HEADS * V_ROWS, TK), lambda i: (0, i, 0, 0)),
                   pl.BlockSpec((IDX_HEADS, tm), lambda i: (0, i))],
        out_shape=[jax.ShapeDtypeStruct((N_SLAB, t, HEAD_DIM), BF16),
                   jax.ShapeDtypeStruct((N_MIXERS, t // TK, N_HEADS * V_ROWS, TK), BF16),
                   jax.ShapeDtypeStruct((IDX_HEADS, t), F32)],
        compiler_params=_cparams(1),
        name="proj",
    )(x, g, w, cs)


def _k_block(ref, h, j):
    return ref[h, pl.ds(pl.multiple_of(j * TK, TK), TK), :]


def _fold_keys(a, op):
    n = a.shape[0]
    while n > 8:
        n //= 2
        a = op(a[:n], a[n:2 * n])
    return a


def _vt_block(ref, h, j, rows=V_ROWS):
    return ref[0, j, h * V_ROWS:h * V_ROWS + rows, :]


def _softmax_block(s_list, vt_list, carry):
    ms = [jnp.maximum(c[0], jnp.max(_fold_keys(s, jnp.maximum), axis=0, keepdims=True))
          for s, c in zip(s_list, carry)]
    pvs = [_dot(vt, jnp.exp2(s - m).astype(BF16)) for vt, s, m in zip(vt_list, s_list, ms)]
    return tuple((m_new, jnp.exp2(m - m_new) * acc + pv) for (m, acc), m_new, pv in zip(carry, ms, pvs))


def _softmax_loop(n_blocks, logits, values, carry, groups=(2, 1)):
    assert groups[-1] == 1
    start = 0
    for g in groups:
        def body(t, carry, g=g, start=start):
            j = start + g * t
            s_all = [logits(j + u) for u in range(g)]
            for u in range(g):
                carry = _softmax_block(s_all[u], values(j + u), carry)
            return carry

        trips = (n_blocks - start) // g
        carry = lax.fori_loop(0, trips, body, carry)
        start = start + g * trips
    return carry


def _softmax_init():
    return (jnp.full((1, TQ), NEG, F32), jnp.zeros((V_ROWS, TQ), F32))


def _softmax_out(carry):
    _, acc = carry
    return acc[:HEAD_DIM] / acc[HEAD_DIM:HEAD_DIM + 1]


def _store_heads(o_ref, heads_t):
    o_ref[...] = jnp.transpose(jnp.concatenate(heads_t, axis=0)).astype(BF16)


BIAS_ROWS_SHAPE = (N_HEADS, 3, 1, 2 * TQ)
BIAS_TILES_SHAPE = (N_HEADS, 3, TK, TQ)


def _fill_bias_tiles(bw_ref, bt_ref):
    @pl.when((pl.program_id(0) == 0) & (pl.program_id(1) == 0))
    def _():
        for h in range(N_HEADS):
            for o in range(3):
                rows = jnp.broadcast_to(bw_ref[h, o], (TK, 2 * TQ))
                bt_ref[h, o] = pltpu.roll(rows, TQ + 1, 1, stride=1, stride_axis=0)[:, :TQ]


def _attn_specs(nq, gq, gk, mixer, s):
    return [pl.BlockSpec((4, TQ, HEAD_DIM), lambda b, i: (gq, b * nq + i, 0)),
            pl.BlockSpec((4, s, HEAD_DIM), lambda b, i: (gk, b, 0)),
            pl.BlockSpec((1, s // TK, N_HEADS * V_ROWS, TK), lambda b, i: (mixer, b, 0, 0))]


def _out_spec(nq):
    return pl.BlockSpec((TQ, N_HEADS * HEAD_DIM), lambda b, i: (b * nq + i, 0))


def _sb_kernel(q_ref, k_ref, vt_ref, tri_ref, mask_ref, o_ref):
    i = pl.program_id(1)
    tri = tri_ref[...]

    def blocks(js, carry, first_masked):
        heads = range(N_HEADS)
        zs = [[_dot_nt(_k_block(k_ref, h, j), q_ref[h]) for h in heads] for j in js]
        if first_masked:
            zs[0] = [z + mask_ref[...] for z in zs[0]]
        sps = [[jnp.maximum(z, 0.0) + jnp.log2(1.0 + jnp.exp2(-jnp.abs(z))) for z in zb] for zb in zs]
        his = [[sp.astype(BF16) for sp in sb] for sb in sps]
        los = [[(sp - hi.astype(F32)).astype(BF16) for sp, hi in zip(sb, hb)] for sb, hb in zip(sps, his)]
        cums = [[_dot(tri, hi) + _dot(tri, lo) for hi, lo in zip(hb, lb)] for hb, lb in zip(his, los)]
        out = []
        for h in heads:
            tail, acc = carry[h]
            for u, j in enumerate(js):
                c = cums[u][h] + tail
                acc = acc + _dot(_vt_block(vt_ref, h, j, HEAD_DIM), jnp.exp2(zs[u][h] - c).astype(BF16))
                tail = c[0:1, :]
            out.append((tail, acc))
        return tuple(out)

    init = tuple((jnp.zeros((1, TQ), F32), jnp.zeros((HEAD_DIM, TQ), F32)) for _ in range(N_HEADS))
    carry = lax.cond(i >= 1, lambda c: blocks([i, i - 1], c, True), lambda c: blocks([i], c, True), init)

    def weights_left(carry):
        tail = functools.reduce(jnp.minimum, [c[0] for c in carry])
        return (jnp.min(tail) < SB_TAIL_CUTOFF).astype(jnp.int32)

    def earlier_block(c):
        j, carry, _ = c
        carry = blocks([j], carry, False)
        return j - 1, carry, weights_left(carry)

    _, carry, _ = lax.while_loop(lambda c: (c[0] >= 0) & (c[2] > 0), earlier_block, (i - 2, carry, weights_left(carry)))
    _store_heads(o_ref, [c[1] for c in carry])


def _sb_attention(zz, vt, b, s, tri, mask):
    nq = s // TQ
    return pl.pallas_call(
        _sb_kernel,
        grid=(b, nq),
        in_specs=_attn_specs(nq, G_SB_Q, G_SB_K, M_SB, s) + [_const_spec((TK, TK)), _const_spec((TK, TQ))],
        out_specs=_out_spec(nq),
        out_shape=jax.ShapeDtypeStruct((b * s, N_HEADS * HEAD_DIM), BF16),
        compiler_params=_cparams(2),
        name="sb_attn",
    )(zz, zz, vt, tri, mask)


def _diff_kernel(q_ref, k_ref, vt_ref, bw_ref, lam_ref, cst_ref, g_ref, o_ref, bt_ref):
    i = pl.program_id(1)
    _fill_bias_tiles(bw_ref, bt_ref)
    lp = lam_ref[...]
    lam_init = cst_ref[:, 0:1]
    lam = (jnp.exp(jnp.sum(lp[0:1] * lp[1:2], axis=-1, keepdims=True))
           - jnp.exp(jnp.sum(lp[2:3] * lp[3:4], axis=-1, keepdims=True)) + lam_init)
    lane = lax.broadcasted_iota(jnp.int32, (TQ, HEAD_DIM), 1)
    qs = []
    for h in range(N_HEADS):
        q = q_ref[h]
        qs.append((jnp.where(lane < DIFF_QK_DIM, q, jnp.zeros_like(q)),
                   jnp.where(lane >= DIFF_QK_DIM, q, jnp.zeros_like(q))))

    def logits(j):
        s_list = []
        for h in range(N_HEADS):
            kj = _k_block(k_ref, h, j)
            bias = bt_ref[h, jnp.minimum(i - j, 2)]
            s_list += [_dot_nt(kj, qs[h][0]) + bias, _dot_nt(kj, qs[h][1]) + bias]
        return s_list

    def values(j):
        return [_vt_block(vt_ref, h, j) for h in range(N_HEADS) for _ in range(2)]

    carry = _softmax_loop(i + 1, logits, values, tuple(_softmax_init() for _ in range(2 * N_HEADS)), groups=(4, 2, 1))
    outs = []
    for h in range(N_HEADS):
        o = _softmax_out(carry[2 * h]) - lam * _softmax_out(carry[2 * h + 1])
        o = o * lax.rsqrt(jnp.mean(o * o, axis=0, keepdims=True) + NORM_EPS) * g_ref[...]
        outs.append(o * (1.0 - lam_init))
    _store_heads(o_ref, outs)


def _diff_attention(zz, vt, b, s, bw, lamp, cst, g):
    nq = s // TQ
    return pl.pallas_call(
        _diff_kernel,
        grid=(b, nq),
        in_specs=_attn_specs(nq, G_DF_Q, G_DF_K, M_DF, s) + [
            _const_spec(BIAS_ROWS_SHAPE), _const_spec((4, DIFF_QK_DIM)), _const_spec((1, 128)),
            _const_spec((HEAD_DIM, 1))],
        out_specs=_out_spec(nq),
        out_shape=jax.ShapeDtypeStruct((b * s, N_HEADS * HEAD_DIM), BF16),
        scratch_shapes=[pltpu.VMEM(BIAS_TILES_SHAPE, F32)],
        compiler_params=_cparams(2),
        name="diff_attn",
    )(zz, zz, vt, bw, lamp, cst, g)


def _dsa_moba_kernel(q_ref, k_ref, vt_ref, qia_ref, qib_ref, ki_ref, wt_ref, bw_ref, tril_ref,
                     mq_ref, mk_ref, mvt_ref, mbw_ref, o_ref, omb_ref, sc_ref, bt_ref, km_ref, mbt_ref, *,
                     topk, nblk, topb):
    i = pl.program_id(1)
    _fill_bias_tiles(bw_ref, bt_ref)
    _fill_bias_tiles(mbw_ref, mbt_ref)
    nb = i + 1
    kf = float(topk)
    w = wt_ref[...]
    key = lax.broadcasted_iota(jnp.int32, (TK, TQ), 0)
    qry = lax.broadcasted_iota(jnp.int32, (TK, TQ), 1)

    def score(j):
        kij = _k_block(ki_ref, 0, j)
        sc = jnp.zeros((TK, TQ), F32)
        for hh in range(IDX_HEADS):
            qi = (qia_ref if hh < 4 else qib_ref)[hh % 4]
            sc = sc + w[hh:hh + 1, :] * jnp.maximum(_dot_nt(kij, qi), 0.0)
        return sc

    def scored(j):
        sc = score(j)
        valid = key - qry <= (i - j) * TQ
        masked = jnp.where(valid, sc, NEG)
        sc_ref[j] = masked
        return masked, jnp.where(valid, sc, BIG)

    c = (jnp.full((8, TQ), BIG, F32), jnp.full((8, TQ), -BIG, F32))
    start = 0
    for g in (4, 2, 1):
        def body(t, c, g=g, start=start):
            tiles = [scored(start + g * t + u) for u in range(g)]
            for masked, lo_src in tiles:
                c = (jnp.minimum(c[0], _fold_keys(lo_src, jnp.minimum)),
                     jnp.maximum(c[1], _fold_keys(masked, jnp.maximum)))
            return c

        trips = (nb - start) // g
        c = lax.fori_loop(0, trips, body, c)
        start = start + g * trips
    lo_part, hi_part = c

    def reduce_blocks(fn, init):
        def pair(t, c):
            return fn(sc_ref[2 * t + 1], 2 * t + 1, fn(sc_ref[2 * t], 2 * t, c))
        c = lax.fori_loop(0, nb // 2, pair, init)
        return lax.fori_loop(2 * (nb // 2), nb, lambda j, c: fn(sc_ref[j], j, c), c)

    def count_ge(t):
        part = reduce_blocks(lambda x, j, c: c + _fold_keys(jnp.where(x >= t, 1.0, 0.0), jnp.add),
                             jnp.zeros((8, TQ), F32))
        return jnp.sum(part, axis=0, keepdims=True)

    def minmax_blocks(lo_of, hi_of):
        def f(x, j, c):
            return (jnp.minimum(c[0], _fold_keys(lo_of(x), jnp.minimum)),
                    jnp.maximum(c[1], _fold_keys(hi_of(x), jnp.maximum)))
        lo_part, hi_part = reduce_blocks(f, (jnp.full((8, TQ), BIG, F32), jnp.full((8, TQ), -BIG, F32)))
        return jnp.min(lo_part, axis=0, keepdims=True), jnp.max(hi_part, axis=0, keepdims=True)

    n_valid = i * TQ + lax.broadcasted_iota(jnp.int32, (1, TQ), 1) + 1
    take_all = n_valid <= topk
    lo = jnp.min(lo_part, axis=0, keepdims=True)
    hi = jnp.max(hi_part, axis=0, keepdims=True)
    c_max = count_ge(hi)
    at_max = c_max >= kf
    state = (jnp.where(at_max, hi, lo), hi, jnp.where(at_max, c_max, n_valid.astype(F32)), c_max)

    def bisect(_, state):
        lo, hi, c_lo, c_hi = state
        mid = 0.5 * lo + 0.5 * hi
        c = count_ge(mid)
        ge = c >= kf
        return jnp.where(ge, mid, lo), jnp.where(ge, hi, mid), jnp.where(ge, c, c_lo), jnp.where(ge, c_hi, c)

    def unsettled(state, n):
        lo, hi, c_lo, _ = state
        open_q = jnp.where(take_all, 0.0, jnp.where(c_lo != kf, jnp.where(lo < hi, 1.0, 0.0), 0.0))

        def band_spread():
            b_min, b_max = minmax_blocks(lambda x: jnp.where(x >= lo, jnp.where(x < hi, x, BIG), BIG),
                                         lambda x: jnp.where(x >= lo, jnp.where(x < hi, x, -BIG), -BIG))
            return (jnp.max(jnp.where(b_max != b_min, open_q, 0.0)) > 0.0).astype(jnp.int32)

        any_open = jnp.max(open_q) > 0.0
        return lax.cond(any_open & (n >= BISECT_COUNT_ONLY_TRIPS), band_spread, lambda: any_open.astype(jnp.int32))

    state = lax.fori_loop(0, jnp.where((i + 1) * TQ <= topk, 0, BISECT_WARMUP), bisect, state)

    def trip(c):
        n, state, _ = c
        state = lax.fori_loop(0, BISECT_TRIP, bisect, state)
        return n + 1, state, unsettled(state, n + 1)

    _, state, _ = lax.while_loop(lambda c: (c[2] > 0) & (c[0] < BISECT_MAX_TRIPS), trip,
                                 (jnp.int32(0), state, unsettled(state, jnp.int32(0))))
    lo, hi, c_lo, c_hi = state
    hi_ok = lo < hi
    c_above = jnp.where(hi_ok, c_hi, 0.0)
    hi_sel = jnp.where(hi_ok, hi, BIG)
    need = jnp.where(take_all, BIG, kf - c_above)
    lo_sel = jnp.where(take_all, HALF_NEG, lo)

    tied = jnp.max(jnp.where(take_all, 0.0, c_lo - kf)) > 0.0

    @pl.when(tied)
    def _():
        tril = tril_ref[...]

        def band_of(j):
            x = sc_ref[j]
            return x, jnp.where(x >= lo_sel, jnp.where(x < hi_sel, 1.0, 0.0), 0.0)

        def write(j, x, band, rank):
            sc_ref[j] = jnp.where(x >= hi_sel, 0.0,
                                  jnp.where(band * rank > 0.0, jnp.where(rank <= need, 0.0, NEG), NEG))
            return rank[TK - 1:TK, :]

        def write_mask(j, taken):
            x, band = band_of(j)
            return write(j, x, band, _dot(tril, band.astype(BF16)) + taken)

        def write_mask_pair(t, taken):
            (x_a, band_a), (x_b, band_b) = band_of(2 * t), band_of(2 * t + 1)
            in_a, in_b = _dot(tril, band_a.astype(BF16)), _dot(tril, band_b.astype(BF16))
            taken = write(2 * t, x_a, band_a, in_a + taken)
            return write(2 * t + 1, x_b, band_b, in_b + taken)

        taken = lax.fori_loop(0, nb // 2, write_mask_pair, jnp.zeros((1, TQ), F32))
        lax.fori_loop(2 * (nb // 2), nb, write_mask, taken)

    @pl.when(jnp.logical_not(tied))
    def _():
        def write_mask(j, _):
            sc_ref[j] = jnp.where(sc_ref[j] >= lo_sel, 0.0, NEG)
            return 0

        lax.fori_loop(0, nb, write_mask, 0)

    head_bits = _moba_block_choice(i, mq_ref, mk_ref, km_ref, nblk, topb)

    def logits(j):
        dsa = [_dot_nt(_k_block(k_ref, h, j), q_ref[h]) + bt_ref[h, jnp.minimum(i - j, 2)] + sc_ref[j]
               for h in range(N_HEADS)]
        moba = []
        for h in range(N_HEADS):
            picked = (lax.shift_right_logical(head_bits[h], jnp.full_like(head_bits[h], j)) & 1) == 1
            moba.append(_dot_nt(_k_block(mk_ref, h, j), mq_ref[h]) + mbt_ref[h, jnp.minimum(i - j, 2)]
                        + jnp.where(picked, 0.0, NEG))
        return dsa + moba

    def values(j):
        return [_vt_block(vt_ref, h, j) for h in range(N_HEADS)] + [_vt_block(mvt_ref, h, j) for h in range(N_HEADS)]

    carry = _softmax_loop(nb, logits, values, tuple(_softmax_init() for _ in range(2 * N_HEADS)), groups=(4, 2, 1))
    _store_heads(o_ref, [_softmax_out(c) for c in carry[:N_HEADS]])
    _store_heads(omb_ref, [_softmax_out(c) for c in carry[N_HEADS:]])


def _moba_block_choice(i, q_ref, k_ref, km_ref, nblk, topb):
    nrow = km_ref.shape[1]

    @pl.when(i == 0)
    def _():
        km_ref[...] = jnp.zeros_like(km_ref)
        for h in range(N_HEADS):
            for n in range(nblk):
                kb = k_ref[h, n * MOBA_BLOCK:(n + 1) * MOBA_BLOCK, :].astype(F32)
                km_ref[h, n:n + 1, :] = jnp.mean(kb, axis=0, keepdims=True)

    blk = lax.broadcasted_iota(jnp.int32, (nrow, TQ), 0)
    past = blk < i
    head_bits = []
    for h in range(N_HEADS):
        gate = _dot_nt(km_ref[h].astype(BF16), q_ref[h])
        bits = jnp.zeros((1, TQ), F32)
        for n in range(nblk):
            gn = gate[n:n + 1, :]
            beats = jnp.where(past, jnp.where(gate > gn, 1.0, jnp.where(gate == gn, jnp.where(blk < n, 1.0, 0.0), 0.0)), 0.0)
            rank = jnp.sum(beats, axis=0, keepdims=True)
            bits = bits + jnp.where(rank < float(topb), jnp.where(n < i, float(2 ** n), 0.0), 0.0)
        head_bits.append(bits.astype(jnp.int32) | lax.shift_left(jnp.int32(1), i))
    return head_bits


def _dsa_moba_attention(zz, vt, wt, b, s, bw_ds, bw_mb, tril):
    nq = s // TQ
    topk = min(DSA_TOPK_MAX, s // 4)
    nblk = s // MOBA_BLOCK
    topb = min(MOBA_TOPK, nblk - 1)
    out = jax.ShapeDtypeStruct((b * s, N_HEADS * HEAD_DIM), BF16)
    return pl.pallas_call(
        functools.partial(_dsa_moba_kernel, topk=topk, nblk=nblk, topb=topb),
        grid=(b, nq),
        in_specs=_attn_specs(nq, G_DS_Q, G_DS_K, M_DS, s) + [
            pl.BlockSpec((4, TQ, HEAD_DIM), lambda b_, i: (G_QI_A, b_ * nq + i, 0)),
            pl.BlockSpec((4, TQ, HEAD_DIM), lambda b_, i: (G_QI_B, b_ * nq + i, 0)),
            pl.BlockSpec((1, s, HEAD_DIM), lambda b_, i: (S_KIDX, b_, 0)),
            pl.BlockSpec((IDX_HEADS, TQ), lambda b_, i: (0, b_ * nq + i)),
            _const_spec(BIAS_ROWS_SHAPE), _const_spec((TK, TK))]
        + _attn_specs(nq, G_MB_Q, G_MB_K, M_MB, s) + [_const_spec(BIAS_ROWS_SHAPE)],
        out_specs=[_out_spec(nq), _out_spec(nq)],
        out_shape=[out, out],
        scratch_shapes=[pltpu.VMEM((nq, TK, TQ), F32), pltpu.VMEM(BIAS_TILES_SHAPE, F32),
                        pltpu.VMEM((N_HEADS, max(8, nblk), HEAD_DIM), F32), pltpu.VMEM(BIAS_TILES_SHAPE, F32)],
        compiler_params=_cparams(2),
        name="dsa_moba_attn",
    )(zz, zz, vt, zz, zz, zz, wt, bw_ds, tril, zz, zz, vt, bw_mb)


def _merge_kernel(x_ref, osb_ref, odf_ref, ods_ref, omb_ref, gpre_ref, wg_ref, wbr_ref, wout_ref, gpost_ref, o_ref):
    x = x_ref[...]
    h = _rms(x, gpre_ref[...]).astype(BF16)
    y = jnp.zeros((x.shape[0], D_MODEL), F32)
    for r, o_r in enumerate((osb_ref, odf_ref, ods_ref, omb_ref)):
        gate = jax.nn.sigmoid(_dot(h, wg_ref[:, r * D_MODEL:(r + 1) * D_MODEL]))
        y = y + gate * _dot(o_r[...], wbr_ref[r])
    o_ref[...] = x + _rms(_dot(y.astype(BF16), wout_ref[...]), gpost_ref[...])


def _merge(x, o_sb, o_df, o_ds, o_mb, g_pre, w_gate, w_br, w_out, g_post, layer):
    t = x.shape[0]
    tm = TM_MERGE
    tok = lambda width: pl.BlockSpec((tm, width), lambda i: (i, 0))
    return pl.pallas_call(
        _merge_kernel,
        grid=(t // tm,),
        in_specs=[tok(D_MODEL)] + [tok(MIXER_WIDTH)] * N_MIXERS + [
            _const_spec((1, D_MODEL)), _layer_spec((D_MODEL, N_MIXERS * D_MODEL), layer),
            _layer_spec((N_MIXERS, MIXER_WIDTH, D_MODEL), layer), _layer_spec((D_MODEL, D_MODEL), layer),
            _const_spec((1, D_MODEL))],
        out_specs=tok(D_MODEL),
        out_shape=jax.ShapeDtypeStruct((t, D_MODEL), F32),
        compiler_params=_cparams(1),
        name="merge",
    )(x, o_sb, o_df, o_ds, o_mb, g_pre, w_gate, w_br, w_out, g_post)


def _ffn_kernel(x_ref, gpre_ref, win_ref, wout_ref, gpost_ref, o_ref):
    x = x_ref[...]
    h = _rms(x, gpre_ref[...]).astype(BF16)
    gate = _dot(h, win_ref[:, 0:D_FF])
    up = _dot(h, win_ref[:, D_FF:2 * D_FF])
    act = (gate * jax.nn.sigmoid(gate) * up).astype(BF16)
    o_ref[...] = x + _rms(_dot(act, wout_ref[...]), gpost_ref[...])


def _ffn(x, g_pre, w_in, w_out, g_post, layer):
    t = x.shape[0]
    tm = TM_FFN
    return pl.pallas_call(
        _ffn_kernel,
        grid=(t // tm,),
        in_specs=[pl.BlockSpec((tm, D_MODEL), lambda i: (i, 0)), _const_spec((1, D_MODEL)),
                  _layer_spec((D_MODEL, 2 * D_FF), layer), _layer_spec((D_FF, D_MODEL), layer),
                  _const_spec((1, D_MODEL))],
        out_specs=pl.BlockSpec((tm, D_MODEL), lambda i: (i, 0)),
        out_shape=jax.ShapeDtypeStruct((t, D_MODEL), F32),
        compiler_params=_cparams(1),
        name="ffn",
    )(x, g_pre, w_in, w_out, g_post)


def _t5_bucket(dist):
    max_exact = N_BUCKETS // 2
    d = jnp.maximum(dist, 0)
    log_ratio = jnp.log(jnp.maximum(d, 1).astype(F32) / max_exact) / math.log(MAX_DISTANCE / max_exact)
    large = jnp.minimum(max_exact + (log_ratio * (N_BUCKETS - max_exact)).astype(jnp.int32), N_BUCKETS - 1)
    return jnp.where(d < max_exact, d, large)


def _bias_rows(rel_bias):
    assert TQ == TK
    n = TK
    d = np.arange(-(n - 1), 3 * n + 1)
    by_dist = rel_bias.astype(F32).T[:, _t5_bucket(jnp.asarray(np.maximum(d, 0), jnp.int32))]
    by_dist = jnp.where(jnp.asarray(d >= 0)[None, :], by_dist * LOG2E, NEG)
    return jnp.stack([by_dist[:, o * n:o * n + 2 * n] for o in range(3)], axis=1)[:, :, None, :]


def _pack_runs():
    runs, start = [], 0
    for e in range(1, N_PACK + 1):
        if e == N_PACK or _PACK_SRC[e] != _PACK_SRC[e - 1] + (1 if _PACK_SRC[e - 1] >= 0 else 0):
            runs.append((int(_PACK_SRC[start]), e - start))
            start = e
    return runs


_PACK_RUNS = _pack_runs()
PACK_ROWS = 128


def _pack_kernel(w_ref, pack_ref, gate_ref):
    w = w_ref[...]
    parts = [w[:, a:a + n] if a >= 0 else jnp.zeros((w.shape[0], n), F32) for a, n in _PACK_RUNS]
    pack_ref[...] = jnp.concatenate(parts, axis=1).astype(BF16)
    gate_ref[...] = w[:, _OFF["gate"]:].astype(BF16)


def _pack_weights(w_in):
    depth, d, n_in = w_in.shape
    row_block = lambda width: pl.BlockSpec((pl.Squeezed(), PACK_ROWS, width), lambda l, r: (l, r, 0))
    return pl.pallas_call(
        _pack_kernel,
        grid=(depth, d // PACK_ROWS),
        in_specs=[row_block(n_in)],
        out_specs=[row_block(N_PACK), row_block(N_MIXERS * D_MODEL)],
        out_shape=[jax.ShapeDtypeStruct((depth, d, N_PACK), BF16),
                   jax.ShapeDtypeStruct((depth, d, N_MIXERS * D_MODEL), BF16)],
        compiler_params=_cparams(2),
        name="pack_weights",
    )(w_in)


def kernel(x, w_in, w_br_sb, w_br_diff, w_br_dsa, w_br_moba, w_out, lambda_q1, lambda_k1, lambda_q2, lambda_k2,
           diff_subln_g, rel_bias, w_ffn_in, w_ffn_out, g_pre_mix, g_post_mix, g_pre_ffn, g_post_ffn):
    b, s, d = x.shape
    depth = w_in.shape[0]
    assert d == D_MODEL and s % TQ == 0 and s // MOBA_BLOCK >= 2
    t = b * s

    w_pack, w_gate = _pack_weights(w_in)
    w_br = jnp.stack([w_br_sb, w_br_diff, w_br_dsa, w_br_moba], axis=1).astype(BF16)
    w_o = w_out.astype(BF16)
    w_f1 = w_ffn_in.astype(BF16)
    w_f2 = w_ffn_out.astype(BF16)
    cs = jnp.asarray(_PACK_SCALE)[None, :]

    bw = _bias_rows(rel_bias)
    bw_df, bw_ds, bw_mb = bw[0:4], bw[4:8], bw[8:12]
    key = np.arange(TK)[:, None]
    qry = np.arange(TQ)[None, :]
    tri = jnp.asarray(key <= np.arange(TK)[None, :], BF16)
    tril = jnp.asarray(key >= np.arange(TK)[None, :], BF16)
    sb_mask = jnp.asarray(np.where(key < qry, 0.0, NEG), F32)

    xf = x.reshape(t, d)
    for l in range(depth):
        lam_init = 0.8 - 0.6 * math.exp(-0.3 * l)
        lamp = jnp.stack([lambda_q1[l], lambda_k1[l], lambda_q2[l], lambda_k2[l]]).astype(F32)
        cst = jnp.full((1, 128), lam_init, F32)
        zz, vt, wt = _proj(xf, g_pre_mix[l][None, :], w_pack, cs, l)
        o_sb = _sb_attention(zz, vt, b, s, tri, sb_mask)
        o_df = _diff_attention(zz, vt, b, s, bw_df, lamp, cst, diff_subln_g[l][:, None])
        o_ds, o_mb = _dsa_moba_attention(zz, vt, wt, b, s, bw_ds, bw_mb, tril)
        xf = _merge(xf, o_sb, o_df, o_ds, o_mb, g_pre_mix[l][None, :], w_gate, w_br, w_o, g_post_mix[l][None, :], l)
        xf = _ffn(xf, g_pre_ffn[l][None, :], w_f1, w_f2, g_post_ffn[l][None, :], l)
    return xf.reshape(b, s, d)
```

```python
import functools
import math

import numpy as np
import jax
import jax.numpy as jnp
from jax import lax
from jax.experimental import pallas as pl
from jax.experimental.pallas import tpu as pltpu

F32 = jnp.float32
BF16 = jnp.bfloat16

D_MODEL = 1024
HEAD_DIM = 64
N_HEADS = 4
N_MIXERS = 4
MIXER_WIDTH = N_HEADS * HEAD_DIM
DIFF_QK_DIM = 32
IDX_HEADS = 8
DSA_TOPK_MAX = 256
MOBA_BLOCK = 256
MOBA_TOPK = 3
N_BUCKETS = 32
MAX_DISTANCE = 128
D_FF = 2816
NORM_EPS = 1e-6

TQ = 256
TK = 256
NEG = -1e30
HALF_NEG = -0.5e30
BIG = 3e38
BISECT_WARMUP = 16
BISECT_TRIP = 2
BISECT_MAX_TRIPS = 134
LOG2E = math.log2(math.e)
SB_TAIL_CUTOFF = 120.0 * LOG2E
V_ROWS = 80
TM_PROJ = 512
TM_MERGE = 512
TM_FFN = 512
N_GROUP = 11
N_SLAB = 4 * N_GROUP
N_PACK = (N_GROUP + N_MIXERS) * MIXER_WIDTH
VMEM_LIMIT = 56 * 1024 * 1024

G_SB_Q, G_SB_K, G_DF_Q, G_DF_K, G_DS_Q, G_DS_K, G_QI_A, G_QI_B, G_MB_Q, G_MB_K, G_KIDX = range(N_GROUP)
S_KIDX = 4 * G_KIDX
M_SB, M_DF, M_DS, M_MB = range(N_MIXERS)


def _layout():
    off = {}
    acc = 0
    for name, sz in (("q_sb", 256), ("k_sb", 256), ("v_sb", 256), ("q1", 128), ("q2", 128), ("k1", 128),
                     ("k2", 128), ("v_df", 256), ("q_ds", 256), ("k_ds", 256), ("v_ds", 256), ("qi", 512),
                     ("ki", 64), ("wi", 8), ("q_mb", 256), ("k_mb", 256), ("v_mb", 256), ("gate", 4096)):
        off[name] = acc
        acc += sz
    return off


_OFF = _layout()


def _pack_layout():
    off = _OFF
    cols, scale = [], []

    def add(start, n, s=1.0):
        cols.extend(range(start, start + n))
        scale.extend([s] * n)

    hd = HEAD_DIM ** -0.5
    hd2 = hd * LOG2E
    df2 = DIFF_QK_DIM ** -0.5 * LOG2E
    add(off["q_sb"], 256, hd2); add(off["k_sb"], 256)
    for h in range(N_HEADS):
        add(off["q1"] + h * 32, 32, df2); add(off["q2"] + h * 32, 32, df2)
    for h in range(N_HEADS):
        add(off["k1"] + h * 32, 32); add(off["k2"] + h * 32, 32)
    add(off["q_ds"], 256, hd2); add(off["k_ds"], 256)
    add(off["qi"], 512, HEAD_DIM ** -0.5)
    add(off["q_mb"], 256, hd2); add(off["k_mb"], 256)
    add(off["ki"], 64); add(off["wi"], IDX_HEADS, IDX_HEADS ** -0.5)
    cols.extend([-1] * 184); scale.extend([1.0] * 184)
    for name in ("v_sb", "v_df", "v_ds", "v_mb"):
        add(off[name], MIXER_WIDTH)
    assert len(cols) == N_PACK
    return np.asarray(cols, np.int32), np.asarray(scale, np.float32)


_PACK_SRC, _PACK_SCALE = _pack_layout()


def _dot(a, b):
    return jnp.dot(a, b, preferred_element_type=F32)


def _dot_nt(a, b):
    return lax.dot_general(a, b, (((1,), (1,)), ((), ())), preferred_element_type=F32)


def _rms(x, g):
    return x * lax.rsqrt(jnp.mean(x * x, axis=-1, keepdims=True) + NORM_EPS) * g


def _cparams(n_axes):
    return pltpu.CompilerParams(dimension_semantics=("arbitrary",) * n_axes, vmem_limit_bytes=VMEM_LIMIT)


def _const_spec(shape):
    nd = len(shape)
    return pl.BlockSpec(shape, lambda *_: (0,) * nd, pipeline_mode=pl.Buffered(1))


def _layer_spec(shape, layer):
    nd = len(shape)
    return pl.BlockSpec((pl.Squeezed(),) + tuple(shape), lambda *_: (layer,) + (0,) * nd, pipeline_mode=pl.Buffered(1))


def _proj_kernel(x_ref, g_ref, w_ref, cs_ref, zz_ref, vt_ref, wt_ref):
    h = _rms(x_ref[...], g_ref[...]).astype(BF16)

    def group(c):
        cols = slice(c * MIXER_WIDTH, (c + 1) * MIXER_WIDTH)
        return _dot(h, w_ref[:, cols]) * cs_ref[:, cols]

    for c in range(N_GROUP):
        r = group(c)
        for s in range(4):
            zz_ref[4 * c + s] = r[:, s * HEAD_DIM:(s + 1) * HEAD_DIM].astype(BF16)
        if c == G_KIDX:
            wt_ref[...] = jnp.transpose(r)[HEAD_DIM:HEAD_DIM + IDX_HEADS]
    for m in range(N_MIXERS):
        v = group(N_GROUP + m)
        for u in range(TM_PROJ // TK):
            vt = jnp.transpose(v[u * TK:(u + 1) * TK]).astype(BF16)
            for hh in range(N_HEADS):
                vt_ref[m, u, hh * V_ROWS:hh * V_ROWS + HEAD_DIM, :] = vt[hh * HEAD_DIM:(hh + 1) * HEAD_DIM]
                vt_ref[m, u, hh * V_ROWS + HEAD_DIM:(hh + 1) * V_ROWS, :] = jnp.ones((V_ROWS - HEAD_DIM, TK), BF16)


def _proj(x, g, w, cs, layer):
    t = x.shape[0]
    tm = TM_PROJ
    return pl.pallas_call(
        _proj_kernel,
        grid=(t // tm,),
        in_specs=[pl.BlockSpec((tm, D_MODEL), lambda i: (i, 0)),
                  _const_spec((1, D_MODEL)),
                  _layer_spec((D_MODEL, N_PACK), layer),
                  _const_spec((1, N_PACK))],
        out_specs=[pl.BlockSpec((N_SLAB, tm, HEAD_DIM), lambda i: (0, i, 0)),
                   pl.BlockSpec((N_MIXERS, tm // TK, N_HEADS * V_ROWS, TK), lambda i: (0, i, 0, 0)),
                   pl.BlockSpec((IDX_HEADS, tm), lambda i: (0, i))],
        out_shape=[jax.ShapeDtypeStruct((N_SLAB, t, HEAD_DIM), BF16),
                   jax.ShapeDtypeStruct((N_MIXERS, t // TK, N_HEADS * V_ROWS, TK), BF16),
                   jax.ShapeDtypeStruct((IDX_HEADS, t), F32)],
        compiler_params=_cparams(1),
        name="proj",
    )(x, g, w, cs)


def _k_block(ref, h, j):
    return ref[h, pl.ds(pl.multiple_of(j * TK, TK), TK), :]


def _fold_keys(a, op):
    n = a.shape[0]
    while n > 8:
        n //= 2
        a = op(a[:n], a[n:2 * n])
    return a


def _vt_block(ref, h, j, rows=V_ROWS):
    return ref[0, j, h * V_ROWS:h * V_ROWS + rows, :]


def _softmax_block(s_list, vt_list, carry):
    ms = [jnp.maximum(c[0], jnp.max(_fold_keys(s, jnp.maximum), axis=0, keepdims=True))
          for s, c in zip(s_list, carry)]
    pvs = [_dot(vt, jnp.exp2(s - m).astype(BF16)) for vt, s, m in zip(vt_list, s_list, ms)]
    return tuple((m_new, jnp.exp2(m - m_new) * acc + pv) for (m, acc), m_new, pv in zip(carry, ms, pvs))


def _softmax_loop(n_blocks, logits, values, carry, groups=(2, 1)):
    assert groups[-1] == 1
    start = 0
    for g in groups:
        def body(t, carry, g=g, start=start):
            j = start + g * t
            s_all = [logits(j + u) for u in range(g)]
            for u in range(g):
                carry = _softmax_block(s_all[u], values(j + u), carry)
            return carry

        trips = (n_blocks - start) // g
        carry = lax.fori_loop(0, trips, body, carry)
        start = start + g * trips
    return carry


def _softmax_init():
    return (jnp.full((1, TQ), NEG, F32), jnp.zeros((V_ROWS, TQ), F32))


def _softmax_out(carry):
    _, acc = carry
    return acc[:HEAD_DIM] / acc[HEAD_DIM:HEAD_DIM + 1]


def _store_heads(o_ref, heads_t):
    o_ref[...] = jnp.transpose(jnp.concatenate(heads_t, axis=0)).astype(BF16)


BIAS_ROWS_SHAPE = (N_HEADS, 3, 1, 2 * TQ)
BIAS_TILES_SHAPE = (N_HEADS, 3, TK, TQ)


def _fill_bias_tiles(bw_ref, bt_ref):
    @pl.when((pl.program_id(0) == 0) & (pl.program_id(1) == 0))
    def _():
        for h in range(N_HEADS):
            for o in range(3):
                rows = jnp.broadcast_to(bw_ref[h, o], (TK, 2 * TQ))
                bt_ref[h, o] = pltpu.roll(rows, TQ + 1, 1, stride=1, stride_axis=0)[:, :TQ]


def _attn_specs(nq, gq, gk, mixer, s):
    return [pl.BlockSpec((4, TQ, HEAD_DIM), lambda b, i: (gq, b * nq + i, 0)),
            pl.BlockSpec((4, s, HEAD_DIM), lambda b, i: (gk, b, 0)),
            pl.BlockSpec((1, s // TK, N_HEADS * V_ROWS, TK), lambda b, i: (mixer, b, 0, 0))]


def _out_spec(nq):
    return pl.BlockSpec((TQ, N_HEADS * HEAD_DIM), lambda b, i: (b * nq + i, 0))


def _sb_kernel(q_ref, k_ref, vt_ref, tri_ref, mask_ref, o_ref):
    i = pl.program_id(1)
    tri = tri_ref[...]

    def blocks(js, carry, first_masked):
        heads = range(N_HEADS)
        zs = [[_dot_nt(_k_block(k_ref, h, j), q_ref[h]) for h in heads] for j in js]
        if first_masked:
            zs[0] = [z + mask_ref[...] for z in zs[0]]
        sps = [[jnp.maximum(z, 0.0) + jnp.log2(1.0 + jnp.exp2(-jnp.abs(z))) for z in zb] for zb in zs]
        his = [[sp.astype(BF16) for sp in sb] for sb in sps]
        los = [[(sp - hi.astype(F32)).astype(BF16) for sp, hi in zip(sb, hb)] for sb, hb in zip(sps, his)]
        cums = [[_dot(tri, hi) + _dot(tri, lo) for hi, lo in zip(hb, lb)] for hb, lb in zip(his, los)]
        out = []
        for h in heads:
            tail, acc = carry[h]
            for u, j in enumerate(js):
                c = cums[u][h] + tail
                acc = acc + _dot(_vt_block(vt_ref, h, j, HEAD_DIM), jnp.exp2(zs[u][h] - c).astype(BF16))
                tail = c[0:1, :]
            out.append((tail, acc))
        return tuple(out)

    init = tuple((jnp.zeros((1, TQ), F32), jnp.zeros((HEAD_DIM, TQ), F32)) for _ in range(N_HEADS))
    carry = lax.cond(i >= 1, lambda c: blocks([i, i - 1], c, True), lambda c: blocks([i], c, True), init)

    def weights_left(carry):
        tail = functools.reduce(jnp.minimum, [c[0] for c in carry])
        return (jnp.min(tail) < SB_TAIL_CUTOFF).astype(jnp.int32)

    def earlier_block(c):
        j, carry, _ = c
        carry = blocks([j], carry, False)
        return j - 1, carry, weights_left(carry)

    _, carry, _ = lax.while_loop(lambda c: (c[0] >= 0) & (c[2] > 0), earlier_block, (i - 2, carry, weights_left(carry)))
    _store_heads(o_ref, [c[1] for c in carry])


def _sb_attention(zz, vt, b, s, tri, mask):
    nq = s // TQ
    return pl.pallas_call(
        _sb_kernel,
        grid=(b, nq),
        in_specs=_attn_specs(nq, G_SB_Q, G_SB_K, M_SB, s) + [_const_spec((TK, TK)), _const_spec((TK, TQ))],
        out_specs=_out_spec(nq),
        out_shape=jax.ShapeDtypeStruct((b * s, N_HEADS * HEAD_DIM), BF16),
        compiler_params=_cparams(2),
        name="sb_attn",
    )(zz, zz, vt, tri, mask)


def _diff_kernel(q_ref, k_ref, vt_ref, bw_ref, lam_ref, cst_ref, g_ref, o_ref, bt_ref):
    i = pl.program_id(1)
    _fill_bias_tiles(bw_ref, bt_ref)
    lp = lam_ref[...]
    lam_init = cst_ref[:, 0:1]
    lam = (jnp.exp(jnp.sum(lp[0:1] * lp[1:2], axis=-1, keepdims=True))
           - jnp.exp(jnp.sum(lp[2:3] * lp[3:4], axis=-1, keepdims=True)) + lam_init)
    lane = lax.broadcasted_iota(jnp.int32, (TQ, HEAD_DIM), 1)
    qs = []
    for h in range(N_HEADS):
        q = q_ref[h]
        qs.append((jnp.where(lane < DIFF_QK_DIM, q, jnp.zeros_like(q)),
                   jnp.where(lane >= DIFF_QK_DIM, q, jnp.zeros_like(q))))

    def logits(j):
        s_list = []
        for h in range(N_HEADS):
            kj = _k_block(k_ref, h, j)
            bias = bt_ref[h, jnp.minimum(i - j, 2)]
            s_list += [_dot_nt(kj, qs[h][0]) + bias, _dot_nt(kj, qs[h][1]) + bias]
        return s_list

    def values(j):
        return [_vt_block(vt_ref, h, j) for h in range(N_HEADS) for _ in range(2)]

    carry = _softmax_loop(i + 1, logits, values, tuple(_softmax_init() for _ in range(2 * N_HEADS)), groups=(4, 2, 1))
    outs = []
    for h in range(N_HEADS):
        o = _softmax_out(carry[2 * h]) - lam * _softmax_out(carry[2 * h + 1])
        o = o * lax.rsqrt(jnp.mean(o * o, axis=0, keepdims=True) + NORM_EPS) * g_ref[...]
        outs.append(o * (1.0 - lam_init))
    _store_heads(o_ref, outs)


def _diff_attention(zz, vt, b, s, bw, lamp, cst, g):
    nq = s // TQ
    return pl.pallas_call(
        _diff_kernel,
        grid=(b, nq),
        in_specs=_attn_specs(nq, G_DF_Q, G_DF_K, M_DF, s) + [
            _const_spec(BIAS_ROWS_SHAPE), _const_spec((4, DIFF_QK_DIM)), _const_spec((1, 128)),
            _const_spec((HEAD_DIM, 1))],
        out_specs=_out_spec(nq),
        out_shape=jax.ShapeDtypeStruct((b * s, N_HEADS * HEAD_DIM), BF16),
        scratch_shapes=[pltpu.VMEM(BIAS_TILES_SHAPE, F32)],
        compiler_params=_cparams(2),
        name="diff_attn",
    )(zz, zz, vt, bw, lamp, cst, g)


def _dsa_moba_kernel(q_ref, k_ref, vt_ref, qia_ref, qib_ref, ki_ref, wt_ref, bw_ref, tril_ref,
                     mq_ref, mk_ref, mvt_ref, mbw_ref, o_ref, omb_ref, sc_ref, bt_ref, km_ref, mbt_ref, *,
                     topk, nblk, topb):
    i = pl.program_id(1)
    _fill_bias_tiles(bw_ref, bt_ref)
    _fill_bias_tiles(mbw_ref, mbt_ref)
    nb = i + 1
    kf = float(topk)
    w = wt_ref[...]
    key = lax.broadcasted_iota(jnp.int32, (TK, TQ), 0)
    qry = lax.broadcasted_iota(jnp.int32, (TK, TQ), 1)

    def score(j):
        kij = _k_block(ki_ref, 0, j)
        sc = jnp.zeros((TK, TQ), F32)
        for hh in range(IDX_HEADS):
            qi = (qia_ref if hh < 4 else qib_ref)[hh % 4]
            sc = sc + w[hh:hh + 1, :] * jnp.maximum(_dot_nt(kij, qi), 0.0)
        return sc

    def scored(j):
        sc = score(j)
        valid = key - qry <= (i - j) * TQ
        masked = jnp.where(valid, sc, NEG)
        sc_ref[j] = masked
        return masked, jnp.where(valid, sc, BIG)

    c = (jnp.full((8, TQ), BIG, F32), jnp.full((8, TQ), -BIG, F32))
    start = 0
    for g in (4, 2, 1):
        def body(t, c, g=g, start=start):
            tiles = [scored(start + g * t + u) for u in range(g)]
            for masked, lo_src in tiles:
                c = (jnp.minimum(c[0], _fold_keys(lo_src, jnp.minimum)),
                     jnp.maximum(c[1], _fold_keys(masked, jnp.maximum)))
            return c

        trips = (nb - start) // g
        c = lax.fori_loop(0, trips, body, c)
        start = start + g * trips
    lo_part, hi_part = c

    def reduce_blocks(fn, init):
        def pair(t, c):
            return fn(sc_ref[2 * t + 1], 2 * t + 1, fn(sc_ref[2 * t], 2 * t, c))
        c = lax.fori_loop(0, nb // 2, pair, init)
        return lax.fori_loop(2 * (nb // 2), nb, lambda j, c: fn(sc_ref[j], j, c), c)

    def count_ge(t):
        part = reduce_blocks(lambda x, j, c: c + _fold_keys(jnp.where(x >= t, 1.0, 0.0), jnp.add),
                             jnp.zeros((8, TQ), F32))
        return jnp.sum(part, axis=0, keepdims=True)

    def minmax_blocks(lo_of, hi_of):
        def f(x, j, c):
            return (jnp.minimum(c[0], _fold_keys(lo_of(x), jnp.minimum)),
                    jnp.maximum(c[1], _fold_keys(hi_of(x), jnp.maximum)))
        lo_part, hi_part = reduce_blocks(f, (jnp.full((8, TQ), BIG, F32), jnp.full((8, TQ), -BIG, F32)))
        return jnp.min(lo_part, axis=0, keepdims=True), jnp.max(hi_part, axis=0, keepdims=True)

    n_valid = i * TQ + lax.broadcasted_iota(jnp.int32, (1, TQ), 1) + 1
    take_all = n_valid <= topk
    lo = jnp.min(lo_part, axis=0, keepdims=True)
    hi = jnp.max(hi_part, axis=0, keepdims=True)
    c_max = count_ge(hi)
    at_max = c_max >= kf
    state = (jnp.where(at_max, hi, lo), hi, jnp.where(at_max, c_max, n_valid.astype(F32)), c_max)

    def bisect(_, state):
        lo, hi, c_lo, c_hi = state
        mid = 0.5 * lo + 0.5 * hi
        c = count_ge(mid)
        ge = c >= kf
        return jnp.where(ge, mid, lo), jnp.where(ge, hi, mid), jnp.where(ge, c, c_lo), jnp.where(ge, c_hi, c)

    def unsettled(state):
        lo, hi, c_lo, _ = state
        open_q = jnp.where(take_all, 0.0, jnp.where(c_lo != kf, jnp.where(lo < hi, 1.0, 0.0), 0.0))

        def band_spread():
            b_min, b_max = minmax_blocks(lambda x: jnp.where(x >= lo, jnp.where(x < hi, x, BIG), BIG),
                                         lambda x: jnp.where(x >= lo, jnp.where(x < hi, x, -BIG), -BIG))
            return (jnp.max(jnp.where(b_max != b_min, open_q, 0.0)) > 0.0).astype(jnp.int32)

        return lax.cond(jnp.max(open_q) > 0.0, band_spread, lambda: jnp.int32(0))

    state = lax.fori_loop(0, jnp.where((i + 1) * TQ <= topk, 0, BISECT_WARMUP), bisect, state)

    def trip(c):
        n, state, _ = c
        state = lax.fori_loop(0, BISECT_TRIP, bisect, state)
        return n + 1, state, unsettled(state)

    _, state, _ = lax.while_loop(lambda c: (c[2] > 0) & (c[0] < BISECT_MAX_TRIPS), trip,
                                 (jnp.int32(0), state, unsettled(state)))
    lo, hi, c_lo, c_hi = state
    hi_ok = lo < hi
    c_above = jnp.where(hi_ok, c_hi, 0.0)
    hi_sel = jnp.where(hi_ok, hi, BIG)
    need = jnp.where(take_all, BIG, kf - c_above)
    lo_sel = jnp.where(take_all, HALF_NEG, lo)

    tied = jnp.max(jnp.where(take_all, 0.0, c_lo - kf)) > 0.0

    @pl.when(tied)
    def _():
        tril = tril_ref[...]

        def band_of(j):
            x = sc_ref[j]
            return x, jnp.where(x >= lo_sel, jnp.where(x < hi_sel, 1.0, 0.0), 0.0)

        def write(j, x, band, rank):
            sc_ref[j] = jnp.where(x >= hi_sel, 0.0,
                                  jnp.where(band * rank > 0.0, jnp.where(rank <= need, 0.0, NEG), NEG))
            return rank[TK - 1:TK, :]

        def write_mask(j, taken):
            x, band = band_of(j)
            return write(j, x, band, _dot(tril, band.astype(BF16)) + taken)

        def write_mask_pair(t, taken):
            (x_a, band_a), (x_b, band_b) = band_of(2 * t), band_of(2 * t + 1)
            in_a, in_b = _dot(tril, band_a.astype(BF16)), _dot(tril, band_b.astype(BF16))
            taken = write(2 * t, x_a, band_a, in_a + taken)
            return write(2 * t + 1, x_b, band_b, in_b + taken)

        taken = lax.fori_loop(0, nb // 2, write_mask_pair, jnp.zeros((1, TQ), F32))
        lax.fori_loop(2 * (nb // 2), nb, write_mask, taken)

    @pl.when(jnp.logical_not(tied))
    def _():
        def write_mask(j, _):
            sc_ref[j] = jnp.where(sc_ref[j] >= lo_sel, 0.0, NEG)
            return 0

        lax.fori_loop(0, nb, write_mask, 0)

    head_bits = _moba_block_choice(i, mq_ref, mk_ref, km_ref, nblk, topb)

    def logits(j):
        dsa = [_dot_nt(_k_block(k_ref, h, j), q_ref[h]) + bt_ref[h, jnp.minimum(i - j, 2)] + sc_ref[j]
               for h in range(N_HEADS)]
        moba = []
        for h in range(N_HEADS):
            picked = (lax.shift_right_logical(head_bits[h], jnp.full_like(head_bits[h], j)) & 1) == 1
            moba.append(_dot_nt(_k_block(mk_ref, h, j), mq_ref[h]) + mbt_ref[h, jnp.minimum(i - j, 2)]
                        + jnp.where(picked, 0.0, NEG))
        return dsa + moba

    def values(j):
        return [_vt_block(vt_ref, h, j) for h in range(N_HEADS)] + [_vt_block(mvt_ref, h, j) for h in range(N_HEADS)]

    carry = _softmax_loop(nb, logits, values, tuple(_softmax_init() for _ in range(2 * N_HEADS)), groups=(4, 2, 1))
    _store_heads(o_ref, [_softmax_out(c) for c in carry[:N_HEADS]])
    _store_heads(omb_ref, [_softmax_out(c) for c in carry[N_HEADS:]])


def _moba_block_choice(i, q_ref, k_ref, km_ref, nblk, topb):
    nrow = km_ref.shape[1]

    @pl.when(i == 0)
    def _():
        km_ref[...] = jnp.zeros_like(km_ref)
        for h in range(N_HEADS):
            for n in range(nblk):
                kb = k_ref[h, n * MOBA_BLOCK:(n + 1) * MOBA_BLOCK, :].astype(F32)
                km_ref[h, n:n + 1, :] = jnp.mean(kb, axis=0, keepdims=True)

    blk = lax.broadcasted_iota(jnp.int32, (nrow, TQ), 0)
    past = blk < i
    head_bits = []
    for h in range(N_HEADS):
        gate = _dot_nt(km_ref[h].astype(BF16), q_ref[h])
        bits = jnp.zeros((1, TQ), F32)
        for n in range(nblk):
            gn = gate[n:n + 1, :]
            beats = jnp.where(past, jnp.where(gate > gn, 1.0, jnp.where(gate == gn, jnp.where(blk < n, 1.0, 0.0), 0.0)), 0.0)
            rank = jnp.sum(beats, axis=0, keepdims=True)
            bits = bits + jnp.where(rank < float(topb), jnp.where(n < i, float(2 ** n), 0.0), 0.0)
        head_bits.append(bits.astype(jnp.int32) | lax.shift_left(jnp.int32(1), i))
    return head_bits


def _dsa_moba_attention(zz, vt, wt, b, s, bw_ds, bw_mb, tril):
    nq = s // TQ
    topk = min(DSA_TOPK_MAX, s // 4)
    nblk = s // MOBA_BLOCK
    topb = min(MOBA_TOPK, nblk - 1)
    out = jax.ShapeDtypeStruct((b * s, N_HEADS * HEAD_DIM), BF16)
    return pl.pallas_call(
        functools.partial(_dsa_moba_kernel, topk=topk, nblk=nblk, topb=topb),
        grid=(b, nq),
        in_specs=_attn_specs(nq, G_DS_Q, G_DS_K, M_DS, s) + [
            pl.BlockSpec((4, TQ, HEAD_DIM), lambda b_, i: (G_QI_A, b_ * nq + i, 0)),
            pl.BlockSpec((4, TQ, HEAD_DIM), lambda b_, i: (G_QI_B, b_ * nq + i, 0)),
            pl.BlockSpec((1, s, HEAD_DIM), lambda b_, i: (S_KIDX, b_, 0)),
            pl.BlockSpec((IDX_HEADS, TQ), lambda b_, i: (0, b_ * nq + i)),
            _const_spec(BIAS_ROWS_SHAPE), _const_spec((TK, TK))]
        + _attn_specs(nq, G_MB_Q, G_MB_K, M_MB, s) + [_const_spec(BIAS_ROWS_SHAPE)],
        out_specs=[_out_spec(nq), _out_spec(nq)],
        out_shape=[out, out],
        scratch_shapes=[pltpu.VMEM((nq, TK, TQ), F32), pltpu.VMEM(BIAS_TILES_SHAPE, F32),
                        pltpu.VMEM((N_HEADS, max(8, nblk), HEAD_DIM), F32), pltpu.VMEM(BIAS_TILES_SHAPE, F32)],
        compiler_params=_cparams(2),
        name="dsa_moba_attn",
    )(zz, zz, vt, zz, zz, zz, wt, bw_ds, tril, zz, zz, vt, bw_mb)


def _merge_kernel(x_ref, osb_ref, odf_ref, ods_ref, omb_ref, gpre_ref, wg_ref, wbr_ref, wout_ref, gpost_ref, o_ref):
    x = x_ref[...]
    h = _rms(x, gpre_ref[...]).astype(BF16)
    y = jnp.zeros((x.shape[0], D_MODEL), F32)
    for r, o_r in enumerate((osb_ref, odf_ref, ods_ref, omb_ref)):
        gate = jax.nn.sigmoid(_dot(h, wg_ref[:, r * D_MODEL:(r + 1) * D_MODEL]))
        y = y + gate * _dot(o_r[...], wbr_ref[r])
    o_ref[...] = x + _rms(_dot(y.astype(BF16), wout_ref[...]), gpost_ref[...])


def _merge(x, o_sb, o_df, o_ds, o_mb, g_pre, w_gate, w_br, w_out, g_post, layer):
    t = x.shape[0]
    tm = TM_MERGE
    tok = lambda width: pl.BlockSpec((tm, width), lambda i: (i, 0))
    return pl.pallas_call(
        _merge_kernel,
        grid=(t // tm,),
        in_specs=[tok(D_MODEL)] + [tok(MIXER_WIDTH)] * N_MIXERS + [
            _const_spec((1, D_MODEL)), _layer_spec((D_MODEL, N_MIXERS * D_MODEL), layer),
            _layer_spec((N_MIXERS, MIXER_WIDTH, D_MODEL), layer), _layer_spec((D_MODEL, D_MODEL), layer),
            _const_spec((1, D_MODEL))],
        out_specs=tok(D_MODEL),
        out_shape=jax.ShapeDtypeStruct((t, D_MODEL), F32),
        compiler_params=_cparams(1),
        name="merge",
    )(x, o_sb, o_df, o_ds, o_mb, g_pre, w_gate, w_br, w_out, g_post)


def _ffn_kernel(x_ref, gpre_ref, win_ref, wout_ref, gpost_ref, o_ref):
    x = x_ref[...]
    h = _rms(x, gpre_ref[...]).astype(BF16)
    gate = _dot(h, win_ref[:, 0:D_FF])
    up = _dot(h, win_ref[:, D_FF:2 * D_FF])
    act = (gate * jax.nn.sigmoid(gate) * up).astype(BF16)
    o_ref[...] = x + _rms(_dot(act, wout_ref[...]), gpost_ref[...])


def _ffn(x, g_pre, w_in, w_out, g_post, layer):
    t = x.shape[0]
    tm = TM_FFN
    return pl.pallas_call(
        _ffn_kernel,
        grid=(t // tm,),
        in_specs=[pl.BlockSpec((tm, D_MODEL), lambda i: (i, 0)), _const_spec((1, D_MODEL)),
                  _layer_spec((D_MODEL, 2 * D_FF), layer), _layer_spec((D_FF, D_MODEL), layer),
                  _const_spec((1, D_MODEL))],
        out_specs=pl.BlockSpec((tm, D_MODEL), lambda i: (i, 0)),
        out_shape=jax.ShapeDtypeStruct((t, D_MODEL), F32),
        compiler_params=_cparams(1),
        name="ffn",
    )(x, g_pre, w_in, w_out, g_post)


def _t5_bucket(dist):
    max_exact = N_BUCKETS // 2
    d = jnp.maximum(dist, 0)
    log_ratio = jnp.log(jnp.maximum(d, 1).astype(F32) / max_exact) / math.log(MAX_DISTANCE / max_exact)
    large = jnp.minimum(max_exact + (log_ratio * (N_BUCKETS - max_exact)).astype(jnp.int32), N_BUCKETS - 1)
    return jnp.where(d < max_exact, d, large)


def _bias_rows(rel_bias):
    assert TQ == TK
    n = TK
    d = np.arange(-(n - 1), 3 * n + 1)
    by_dist = rel_bias.astype(F32).T[:, _t5_bucket(jnp.asarray(np.maximum(d, 0), jnp.int32))]
    by_dist = jnp.where(jnp.asarray(d >= 0)[None, :], by_dist * LOG2E, NEG)
    return jnp.stack([by_dist[:, o * n:o * n + 2 * n] for o in range(3)], axis=1)[:, :, None, :]


def _pack_runs():
    runs, start = [], 0
    for e in range(1, N_PACK + 1):
        if e == N_PACK or _PACK_SRC[e] != _PACK_SRC[e - 1] + (1 if _PACK_SRC[e - 1] >= 0 else 0):
            runs.append((int(_PACK_SRC[start]), e - start))
            start = e
    return runs


_PACK_RUNS = _pack_runs()
PACK_ROWS = 128


def _pack_kernel(w_ref, pack_ref, gate_ref):
    w = w_ref[...]
    parts = [w[:, a:a + n] if a >= 0 else jnp.zeros((w.shape[0], n), F32) for a, n in _PACK_RUNS]
    pack_ref[...] = jnp.concatenate(parts, axis=1).astype(BF16)
    gate_ref[...] = w[:, _OFF["gate"]:].astype(BF16)


def _pack_weights(w_in):
    depth, d, n_in = w_in.shape
    row_block = lambda width: pl.BlockSpec((pl.Squeezed(), PACK_ROWS, width), lambda l, r: (l, r, 0))
    return pl.pallas_call(
        _pack_kernel,
        grid=(depth, d // PACK_ROWS),
        in_specs=[row_block(n_in)],
        out_specs=[row_block(N_PACK), row_block(N_MIXERS * D_MODEL)],
        out_shape=[jax.ShapeDtypeStruct((depth, d, N_PACK), BF16),
                   jax.ShapeDtypeStruct((depth, d, N_MIXERS * D_MODEL), BF16)],
        compiler_params=_cparams(2),
        name="pack_weights",
    )(w_in)


def kernel(x, w_in, w_br_sb, w_br_diff, w_br_dsa, w_br_moba, w_out, lambda_q1, lambda_k1, lambda_q2, lambda_k2,
           diff_subln_g, rel_bias, w_ffn_in, w_ffn_out, g_pre_mix, g_post_mix, g_pre_ffn, g_post_ffn):
    b, s, d = x.shape
    depth = w_in.shape[0]
    assert d == D_MODEL and s % TQ == 0 and s // MOBA_BLOCK >= 2
    t = b * s

    w_pack, w_gate = _pack_weights(w_in)
    w_br = jnp.stack([w_br_sb, w_br_diff, w_br_dsa, w_br_moba], axis=1).astype(BF16)
    w_o = w_out.astype(BF16)
    w_f1 = w_ffn_in.astype(BF16)
    w_f2 = w_ffn_out.astype(BF16)
    cs = jnp.asarray(_PACK_SCALE)[None, :]

    bw = _bias_rows(rel_bias)
    bw_df, bw_ds, bw_mb = bw[0:4], bw[4:8], bw[8:12]
    key = np.arange(TK)[:, None]
    qry = np.arange(TQ)[None, :]
    tri = jnp.asarray(key <= np.arange(TK)[None, :], BF16)
    tril = jnp.asarray(key >= np.arange(TK)[None, :], BF16)
    sb_mask = jnp.asarray(np.where(key < qry, 0.0, NEG), F32)

    xf = x.reshape(t, d)
    for l in range(depth):
        lam_init = 0.8 - 0.6 * math.exp(-0.3 * l)
        lamp = jnp.stack([lambda_q1[l], lambda_k1[l], lambda_q2[l], lambda_k2[l]]).astype(F32)
        cst = jnp.full((1, 128), lam_init, F32)
        zz, vt, wt = _proj(xf, g_pre_mix[l][None, :], w_pack, cs, l)
        o_sb = _sb_attention(zz, vt, b, s, tri, sb_mask)
        o_df = _diff_attention(zz, vt, b, s, bw_df, lamp, cst, diff_subln_g[l][:, None])
        o_ds, o_mb = _dsa_moba_attention(zz, vt, wt, b, s, bw_ds, bw_mb, tril)
        xf = _merge(xf, o_sb, o_df, o_ds, o_mb, g_pre_mix[l][None, :], w_gate, w_br, w_o, g_post_mix[l][None, :], l)
        xf = _ffn(xf, g_pre_ffn[l][None, :], w_f1, w_f2, g_post_ffn[l][None, :], l)
    return xf.reshape(b, s, d)
```

```python
import functools
import math

import numpy as np
import jax
import jax.numpy as jnp
from jax import lax
from jax.experimental import pallas as pl
from jax.experimental.pallas import tpu as pltpu

F32 = jnp.float32
BF16 = jnp.bfloat16

D_MODEL = 1024
HEAD_DIM = 64
N_HEADS = 4
N_MIXERS = 4
MIXER_WIDTH = N_HEADS * HEAD_DIM
DIFF_QK_DIM = 32
IDX_HEADS = 8
DSA_TOPK_MAX = 256
MOBA_BLOCK = 256
MOBA_TOPK = 3
N_BUCKETS = 32
MAX_DISTANCE = 128
D_FF = 2816
NORM_EPS = 1e-6

TQ = 256
TK = 256
NEG = -1e30
HALF_NEG = -0.5e30
BIG = 3e38
BISECT_WARMUP = 22
BISECT_TRIP = 2
BISECT_MAX_TRIPS = 134
LOG2E = math.log2(math.e)
SB_TAIL_CUTOFF = 120.0 * LOG2E
V_ROWS = 80
TM_PROJ = 512
TM_MERGE = 512
TM_FFN = 512
N_GROUP = 11
N_SLAB = 4 * N_GROUP
N_PACK = (N_GROUP + N_MIXERS) * MIXER_WIDTH
VMEM_LIMIT = 56 * 1024 * 1024

G_SB_Q, G_SB_K, G_DF_Q, G_DF_K, G_DS_Q, G_DS_K, G_QI_A, G_QI_B, G_MB_Q, G_MB_K, G_KIDX = range(N_GROUP)
S_KIDX = 4 * G_KIDX
M_SB, M_DF, M_DS, M_MB = range(N_MIXERS)


def _layout():
    off = {}
    acc = 0
    for name, sz in (("q_sb", 256), ("k_sb", 256), ("v_sb", 256), ("q1", 128), ("q2", 128), ("k1", 128),
                     ("k2", 128), ("v_df", 256), ("q_ds", 256), ("k_ds", 256), ("v_ds", 256), ("qi", 512),
                     ("ki", 64), ("wi", 8), ("q_mb", 256), ("k_mb", 256), ("v_mb", 256), ("gate", 4096)):
        off[name] = acc
        acc += sz
    return off


_OFF = _layout()


def _pack_layout():
    off = _OFF
    cols, scale = [], []

    def add(start, n, s=1.0):
        cols.extend(range(start, start + n))
        scale.extend([s] * n)

    hd = HEAD_DIM ** -0.5
    hd2 = hd * LOG2E
    df2 = DIFF_QK_DIM ** -0.5 * LOG2E
    add(off["q_sb"], 256, hd2); add(off["k_sb"], 256)
    for h in range(N_HEADS):
        add(off["q1"] + h * 32, 32, df2); add(off["q2"] + h * 32, 32, df2)
    for h in range(N_HEADS):
        add(off["k1"] + h * 32, 32); add(off["k2"] + h * 32, 32)
    add(off["q_ds"], 256, hd2); add(off["k_ds"], 256)
    add(off["qi"], 512, HEAD_DIM ** -0.5)
    add(off["q_mb"], 256, hd2); add(off["k_mb"], 256)
    add(off["ki"], 64); add(off["wi"], IDX_HEADS, IDX_HEADS ** -0.5)
    cols.extend([-1] * 184); scale.extend([1.0] * 184)
    for name in ("v_sb", "v_df", "v_ds", "v_mb"):
        add(off[name], MIXER_WIDTH)
    assert len(cols) == N_PACK
    return np.asarray(cols, np.int32), np.asarray(scale, np.float32)


_PACK_SRC, _PACK_SCALE = _pack_layout()


def _dot(a, b):
    return jnp.dot(a, b, preferred_element_type=F32)


def _dot_nt(a, b):
    return lax.dot_general(a, b, (((1,), (1,)), ((), ())), preferred_element_type=F32)


def _rms(x, g):
    return x * lax.rsqrt(jnp.mean(x * x, axis=-1, keepdims=True) + NORM_EPS) * g


def _cparams(n_axes):
    return pltpu.CompilerParams(dimension_semantics=("arbitrary",) * n_axes, vmem_limit_bytes=VMEM_LIMIT)


def _const_spec(shape):
    nd = len(shape)
    return pl.BlockSpec(shape, lambda *_: (0,) * nd, pipeline_mode=pl.Buffered(1))


def _layer_spec(shape, layer):
    nd = len(shape)
    return pl.BlockSpec((pl.Squeezed(),) + tuple(shape), lambda *_: (layer,) + (0,) * nd, pipeline_mode=pl.Buffered(1))


def _proj_kernel(x_ref, g_ref, w_ref, cs_ref, zz_ref, vt_ref, wt_ref):
    h = _rms(x_ref[...], g_ref[...]).astype(BF16)

    def group(c):
        cols = slice(c * MIXER_WIDTH, (c + 1) * MIXER_WIDTH)
        return _dot(h, w_ref[:, cols]) * cs_ref[:, cols]

    for c in range(N_GROUP):
        r = group(c)
        for s in range(4):
            zz_ref[4 * c + s] = r[:, s * HEAD_DIM:(s + 1) * HEAD_DIM].astype(BF16)
        if c == G_KIDX:
            wt_ref[...] = jnp.transpose(r)[HEAD_DIM:HEAD_DIM + IDX_HEADS]
    for m in range(N_MIXERS):
        v = group(N_GROUP + m)
        for u in range(TM_PROJ // TK):
            vt = jnp.transpose(v[u * TK:(u + 1) * TK]).astype(BF16)
            for hh in range(N_HEADS):
                vt_ref[m, u, hh * V_ROWS:hh * V_ROWS + HEAD_DIM, :] = vt[hh * HEAD_DIM:(hh + 1) * HEAD_DIM]
                vt_ref[m, u, hh * V_ROWS + HEAD_DIM:(hh + 1) * V_ROWS, :] = jnp.ones((V_ROWS - HEAD_DIM, TK), BF16)


def _proj(x, g, w, cs, layer):
    t = x.shape[0]
    tm = TM_PROJ
    return pl.pallas_call(
        _proj_kernel,
        grid=(t // tm,),
        in_specs=[pl.BlockSpec((tm, D_MODEL), lambda i: (i, 0)),
                  _const_spec((1, D_MODEL)),
                  _layer_spec((D_MODEL, N_PACK), layer),
                  _const_spec((1, N_PACK))],
        out_specs=[pl.BlockSpec((N_SLAB, tm, HEAD_DIM), lambda i: (0, i, 0)),
                   pl.BlockSpec((N_MIXERS, tm // TK, N_HEADS * V_ROWS, TK), lambda i: (0, i, 0, 0)),
                   pl.BlockSpec((IDX_HEADS, tm), lambda i: (0, i))],
        out_shape=[jax.ShapeDtypeStruct((N_SLAB, t, HEAD_DIM), BF16),
                   jax.ShapeDtypeStruct((N_MIXERS, t // TK, N_HEADS * V_ROWS, TK), BF16),
                   jax.ShapeDtypeStruct((IDX_HEADS, t), F32)],
        compiler_params=_cparams(1),
        name="proj",
    )(x, g, w, cs)


def _k_block(ref, h, j):
    return ref[h, pl.ds(pl.multiple_of(j * TK, TK), TK), :]


def _fold_keys(a, op):
    n = a.shape[0]
    while n > 8:
        n //= 2
        a = op(a[:n], a[n:2 * n])
    return a


def _vt_block(ref, h, j, rows=V_ROWS):
    return ref[0, j, h * V_ROWS:h * V_ROWS + rows, :]


def _softmax_block(s_list, vt_list, carry):
    ms = [jnp.maximum(c[0], jnp.max(_fold_keys(s, jnp.maximum), axis=0, keepdims=True))
          for s, c in zip(s_list, carry)]
    pvs = [_dot(vt, jnp.exp2(s - m).astype(BF16)) for vt, s, m in zip(vt_list, s_list, ms)]
    return tuple((m_new, jnp.exp2(m - m_new) * acc + pv) for (m, acc), m_new, pv in zip(carry, ms, pvs))


def _softmax_loop(n_blocks, logits, values, carry, groups=(2, 1)):
    assert groups[-1] == 1
    start = 0
    for g in groups:
        def body(t, carry, g=g, start=start):
            j = start + g * t
            s_all = [logits(j + u) for u in range(g)]
            for u in range(g):
                carry = _softmax_block(s_all[u], values(j + u), carry)
            return carry

        trips = (n_blocks - start) // g
        carry = lax.fori_loop(0, trips, body, carry)
        start = start + g * trips
    return carry


def _softmax_init():
    return (jnp.full((1, TQ), NEG, F32), jnp.zeros((V_ROWS, TQ), F32))


def _softmax_out(carry):
    _, acc = carry
    return acc[:HEAD_DIM] / acc[HEAD_DIM:HEAD_DIM + 1]


def _store_heads(o_ref, heads_t):
    o_ref[...] = jnp.transpose(jnp.concatenate(heads_t, axis=0)).astype(BF16)


BIAS_ROWS_SHAPE = (N_HEADS, 3, 1, 2 * TQ)
BIAS_TILES_SHAPE = (N_HEADS, 3, TK, TQ)


def _fill_bias_tiles(bw_ref, bt_ref):
    @pl.when((pl.program_id(0) == 0) & (pl.program_id(1) == 0))
    def _():
        for h in range(N_HEADS):
            for o in range(3):
                rows = jnp.broadcast_to(bw_ref[h, o], (TK, 2 * TQ))
                bt_ref[h, o] = pltpu.roll(rows, TQ + 1, 1, stride=1, stride_axis=0)[:, :TQ]


def _attn_specs(nq, gq, gk, mixer, s):
    return [pl.BlockSpec((4, TQ, HEAD_DIM), lambda b, i: (gq, b * nq + i, 0)),
            pl.BlockSpec((4, s, HEAD_DIM), lambda b, i: (gk, b, 0)),
            pl.BlockSpec((1, s // TK, N_HEADS * V_ROWS, TK), lambda b, i: (mixer, b, 0, 0))]


def _out_spec(nq):
    return pl.BlockSpec((TQ, N_HEADS * HEAD_DIM), lambda b, i: (b * nq + i, 0))


def _sb_kernel(q_ref, k_ref, vt_ref, tri_ref, mask_ref, o_ref):
    i = pl.program_id(1)
    tri = tri_ref[...]

    def blocks(js, carry, first_masked):
        heads = range(N_HEADS)
        zs = [[_dot_nt(_k_block(k_ref, h, j), q_ref[h]) for h in heads] for j in js]
        if first_masked:
            zs[0] = [z + mask_ref[...] for z in zs[0]]
        sps = [[jnp.maximum(z, 0.0) + jnp.log2(1.0 + jnp.exp2(-jnp.abs(z))) for z in zb] for zb in zs]
        his = [[sp.astype(BF16) for sp in sb] for sb in sps]
        los = [[(sp - hi.astype(F32)).astype(BF16) for sp, hi in zip(sb, hb)] for sb, hb in zip(sps, his)]
        cums = [[_dot(tri, hi) + _dot(tri, lo) for hi, lo in zip(hb, lb)] for hb, lb in zip(his, los)]
        out = []
        for h in heads:
            tail, acc = carry[h]
            for u, j in enumerate(js):
                c = cums[u][h] + tail
                acc = acc + _dot(_vt_block(vt_ref, h, j, HEAD_DIM), jnp.exp2(zs[u][h] - c).astype(BF16))
                tail = c[0:1, :]
            out.append((tail, acc))
        return tuple(out)

    init = tuple((jnp.zeros((1, TQ), F32), jnp.zeros((HEAD_DIM, TQ), F32)) for _ in range(N_HEADS))
    carry = lax.cond(i >= 1, lambda c: blocks([i, i - 1], c, True), lambda c: blocks([i], c, True), init)

    def weights_left(carry):
        tail = functools.reduce(jnp.minimum, [c[0] for c in carry])
        return (jnp.min(tail) < SB_TAIL_CUTOFF).astype(jnp.int32)

    def earlier_block(c):
        j, carry, _ = c
        carry = blocks([j], carry, False)
        return j - 1, carry, weights_left(carry)

    _, carry, _ = lax.while_loop(lambda c: (c[0] >= 0) & (c[2] > 0), earlier_block, (i - 2, carry, weights_left(carry)))
    _store_heads(o_ref, [c[1] for c in carry])


def _sb_attention(zz, vt, b, s, tri, mask):
    nq = s // TQ
    return pl.pallas_call(
        _sb_kernel,
        grid=(b, nq),
        in_specs=_attn_specs(nq, G_SB_Q, G_SB_K, M_SB, s) + [_const_spec((TK, TK)), _const_spec((TK, TQ))],
        out_specs=_out_spec(nq),
        out_shape=jax.ShapeDtypeStruct((b * s, N_HEADS * HEAD_DIM), BF16),
        compiler_params=_cparams(2),
        name="sb_attn",
    )(zz, zz, vt, tri, mask)


def _diff_kernel(q_ref, k_ref, vt_ref, bw_ref, lam_ref, cst_ref, g_ref, o_ref, bt_ref):
    i = pl.program_id(1)
    _fill_bias_tiles(bw_ref, bt_ref)
    lp = lam_ref[...]
    lam_init = cst_ref[:, 0:1]
    lam = (jnp.exp(jnp.sum(lp[0:1] * lp[1:2], axis=-1, keepdims=True))
           - jnp.exp(jnp.sum(lp[2:3] * lp[3:4], axis=-1, keepdims=True)) + lam_init)
    lane = lax.broadcasted_iota(jnp.int32, (TQ, HEAD_DIM), 1)
    qs = []
    for h in range(N_HEADS):
        q = q_ref[h]
        qs.append((jnp.where(lane < DIFF_QK_DIM, q, jnp.zeros_like(q)),
                   jnp.where(lane >= DIFF_QK_DIM, q, jnp.zeros_like(q))))

    def logits(j):
        s_list = []
        for h in range(N_HEADS):
            kj = _k_block(k_ref, h, j)
            bias = bt_ref[h, jnp.minimum(i - j, 2)]
            s_list += [_dot_nt(kj, qs[h][0]) + bias, _dot_nt(kj, qs[h][1]) + bias]
        return s_list

    def values(j):
        return [_vt_block(vt_ref, h, j) for h in range(N_HEADS) for _ in range(2)]

    carry = _softmax_loop(i + 1, logits, values, tuple(_softmax_init() for _ in range(2 * N_HEADS)), groups=(4, 2, 1))
    outs = []
    for h in range(N_HEADS):
        o = _softmax_out(carry[2 * h]) - lam * _softmax_out(carry[2 * h + 1])
        o = o * lax.rsqrt(jnp.mean(o * o, axis=0, keepdims=True) + NORM_EPS) * g_ref[...]
        outs.append(o * (1.0 - lam_init))
    _store_heads(o_ref, outs)


def _diff_attention(zz, vt, b, s, bw, lamp, cst, g):
    nq = s // TQ
    return pl.pallas_call(
        _diff_kernel,
        grid=(b, nq),
        in_specs=_attn_specs(nq, G_DF_Q, G_DF_K, M_DF, s) + [
            _const_spec(BIAS_ROWS_SHAPE), _const_spec((4, DIFF_QK_DIM)), _const_spec((1, 128)),
            _const_spec((HEAD_DIM, 1))],
        out_specs=_out_spec(nq),
        out_shape=jax.ShapeDtypeStruct((b * s, N_HEADS * HEAD_DIM), BF16),
        scratch_shapes=[pltpu.VMEM(BIAS_TILES_SHAPE, F32)],
        compiler_params=_cparams(2),
        name="diff_attn",
    )(zz, zz, vt, bw, lamp, cst, g)


def _dsa_moba_kernel(q_ref, k_ref, vt_ref, qia_ref, qib_ref, ki_ref, wt_ref, bw_ref, tril_ref,
                     mq_ref, mk_ref, mvt_ref, mbw_ref, o_ref, omb_ref, sc_ref, bt_ref, km_ref, mbt_ref, *,
                     topk, nblk, topb):
    i = pl.program_id(1)
    _fill_bias_tiles(bw_ref, bt_ref)
    _fill_bias_tiles(mbw_ref, mbt_ref)
    nb = i + 1
    kf = float(topk)
    w = wt_ref[...]
    key = lax.broadcasted_iota(jnp.int32, (TK, TQ), 0)
    qry = lax.broadcasted_iota(jnp.int32, (TK, TQ), 1)

    def score(j):
        kij = _k_block(ki_ref, 0, j)
        sc = jnp.zeros((TK, TQ), F32)
        for hh in range(IDX_HEADS):
            qi = (qia_ref if hh < 4 else qib_ref)[hh % 4]
            sc = sc + w[hh:hh + 1, :] * jnp.maximum(_dot_nt(kij, qi), 0.0)
        return sc

    def scored(j):
        sc = score(j)
        valid = key - qry <= (i - j) * TQ
        masked = jnp.where(valid, sc, NEG)
        sc_ref[j] = masked
        return masked, jnp.where(valid, sc, BIG)

    c = (jnp.full((8, TQ), BIG, F32), jnp.full((8, TQ), -BIG, F32))
    start = 0
    for g in (4, 2, 1):
        def body(t, c, g=g, start=start):
            tiles = [scored(start + g * t + u) for u in range(g)]
            for masked, lo_src in tiles:
                c = (jnp.minimum(c[0], _fold_keys(lo_src, jnp.minimum)),
                     jnp.maximum(c[1], _fold_keys(masked, jnp.maximum)))
            return c

        trips = (nb - start) // g
        c = lax.fori_loop(0, trips, body, c)
        start = start + g * trips
    lo_part, hi_part = c

    def reduce_blocks(fn, init):
        def pair(t, c):
            return fn(sc_ref[2 * t + 1], 2 * t + 1, fn(sc_ref[2 * t], 2 * t, c))
        c = lax.fori_loop(0, nb // 2, pair, init)
        return lax.fori_loop(2 * (nb // 2), nb, lambda j, c: fn(sc_ref[j], j, c), c)

    def count_ge(t):
        part = reduce_blocks(lambda x, j, c: c + _fold_keys(jnp.where(x >= t, 1.0, 0.0), jnp.add),
                             jnp.zeros((8, TQ), F32))
        return jnp.sum(part, axis=0, keepdims=True)

    def minmax_blocks(lo_of, hi_of):
        def f(x, j, c):
            return (jnp.minimum(c[0], _fold_keys(lo_of(x), jnp.minimum)),
                    jnp.maximum(c[1], _fold_keys(hi_of(x), jnp.maximum)))
        lo_part, hi_part = reduce_blocks(f, (jnp.full((8, TQ), BIG, F32), jnp.full((8, TQ), -BIG, F32)))
        return jnp.min(lo_part, axis=0, keepdims=True), jnp.max(hi_part, axis=0, keepdims=True)

    n_valid = i * TQ + lax.broadcasted_iota(jnp.int32, (1, TQ), 1) + 1
    take_all = n_valid <= topk
    lo = jnp.min(lo_part, axis=0, keepdims=True)
    hi = jnp.max(hi_part, axis=0, keepdims=True)
    c_max = count_ge(hi)
    at_max = c_max >= kf
    state = (jnp.where(at_max, hi, lo), hi, jnp.where(at_max, c_max, n_valid.astype(F32)), c_max)

    def bisect(_, state):
        lo, hi, c_lo, c_hi = state
        mid = 0.5 * lo + 0.5 * hi
        c = count_ge(mid)
        ge = c >= kf
        return jnp.where(ge, mid, lo), jnp.where(ge, hi, mid), jnp.where(ge, c, c_lo), jnp.where(ge, c_hi, c)

    def unsettled(state):
        lo, hi, c_lo, _ = state
        open_q = jnp.where(take_all, 0.0, jnp.where(c_lo != kf, jnp.where(lo < hi, 1.0, 0.0), 0.0))

        def band_spread():
            b_min, b_max = minmax_blocks(lambda x: jnp.where(x >= lo, jnp.where(x < hi, x, BIG), BIG),
                                         lambda x: jnp.where(x >= lo, jnp.where(x < hi, x, -BIG), -BIG))
            return (jnp.max(jnp.where(b_max != b_min, open_q, 0.0)) > 0.0).astype(jnp.int32)

        return lax.cond(jnp.max(open_q) > 0.0, band_spread, lambda: jnp.int32(0))

    state = lax.fori_loop(0, jnp.where((i + 1) * TQ <= topk, 0, BISECT_WARMUP), bisect, state)

    def trip(c):
        n, state, _ = c
        state = lax.fori_loop(0, BISECT_TRIP, bisect, state)
        return n + 1, state, unsettled(state)

    _, state, _ = lax.while_loop(lambda c: (c[2] > 0) & (c[0] < BISECT_MAX_TRIPS), trip,
                                 (jnp.int32(0), state, unsettled(state)))
    lo, hi, c_lo, c_hi = state
    hi_ok = lo < hi
    c_above = jnp.where(hi_ok, c_hi, 0.0)
    hi_sel = jnp.where(hi_ok, hi, BIG)
    need = jnp.where(take_all, BIG, kf - c_above)
    lo_sel = jnp.where(take_all, HALF_NEG, lo)

    tied = jnp.max(jnp.where(take_all, 0.0, c_lo - kf)) > 0.0

    @pl.when(tied)
    def _():
        tril = tril_ref[...]

        def band_of(j):
            x = sc_ref[j]
            return x, jnp.where(x >= lo_sel, jnp.where(x < hi_sel, 1.0, 0.0), 0.0)

        def write(j, x, band, rank):
            sc_ref[j] = jnp.where(x >= hi_sel, 0.0,
                                  jnp.where(band * rank > 0.0, jnp.where(rank <= need, 0.0, NEG), NEG))
            return rank[TK - 1:TK, :]

        def write_mask(j, taken):
            x, band = band_of(j)
            return write(j, x, band, _dot(tril, band.astype(BF16)) + taken)

        def write_mask_pair(t, taken):
            (x_a, band_a), (x_b, band_b) = band_of(2 * t), band_of(2 * t + 1)
            in_a, in_b = _dot(tril, band_a.astype(BF16)), _dot(tril, band_b.astype(BF16))
            taken = write(2 * t, x_a, band_a, in_a + taken)
            return write(2 * t + 1, x_b, band_b, in_b + taken)

        taken = lax.fori_loop(0, nb // 2, write_mask_pair, jnp.zeros((1, TQ), F32))
        lax.fori_loop(2 * (nb // 2), nb, write_mask, taken)

    @pl.when(jnp.logical_not(tied))
    def _():
        def write_mask(j, _):
            sc_ref[j] = jnp.where(sc_ref[j] >= lo_sel, 0.0, NEG)
            return 0

        lax.fori_loop(0, nb, write_mask, 0)

    head_bits = _moba_block_choice(i, mq_ref, mk_ref, km_ref, nblk, topb)

    def logits(j):
        dsa = [_dot_nt(_k_block(k_ref, h, j), q_ref[h]) + bt_ref[h, jnp.minimum(i - j, 2)] + sc_ref[j]
               for h in range(N_HEADS)]
        moba = []
        for h in range(N_HEADS):
            picked = (lax.shift_right_logical(head_bits[h], jnp.full_like(head_bits[h], j)) & 1) == 1
            moba.append(_dot_nt(_k_block(mk_ref, h, j), mq_ref[h]) + mbt_ref[h, jnp.minimum(i - j, 2)]
                        + jnp.where(picked, 0.0, NEG))
        return dsa + moba

    def values(j):
        return [_vt_block(vt_ref, h, j) for h in range(N_HEADS)] + [_vt_block(mvt_ref, h, j) for h in range(N_HEADS)]

    carry = _softmax_loop(nb, logits, values, tuple(_softmax_init() for _ in range(2 * N_HEADS)), groups=(4, 2, 1))
    _store_heads(o_ref, [_softmax_out(c) for c in carry[:N_HEADS]])
    _store_heads(omb_ref, [_softmax_out(c) for c in carry[N_HEADS:]])


def _moba_block_choice(i, q_ref, k_ref, km_ref, nblk, topb):
    nrow = km_ref.shape[1]

    @pl.when(i == 0)
    def _():
        km_ref[...] = jnp.zeros_like(km_ref)
        for h in range(N_HEADS):
            for n in range(nblk):
                kb = k_ref[h, n * MOBA_BLOCK:(n + 1) * MOBA_BLOCK, :].astype(F32)
                km_ref[h, n:n + 1, :] = jnp.mean(kb, axis=0, keepdims=True)

    blk = lax.broadcasted_iota(jnp.int32, (nrow, TQ), 0)
    past = blk < i
    head_bits = []
    for h in range(N_HEADS):
        gate = _dot_nt(km_ref[h].astype(BF16), q_ref[h])
        bits = jnp.zeros((1, TQ), F32)
        for n in range(nblk):
            gn = gate[n:n + 1, :]
            beats = jnp.where(past, jnp.where(gate > gn, 1.0, jnp.where(gate == gn, jnp.where(blk < n, 1.0, 0.0), 0.0)), 0.0)
            rank = jnp.sum(beats, axis=0, keepdims=True)
            bits = bits + jnp.where(rank < float(topb), jnp.where(n < i, float(2 ** n), 0.0), 0.0)
        head_bits.append(bits.astype(jnp.int32) | lax.shift_left(jnp.int32(1), i))
    return head_bits


def _dsa_moba_attention(zz, vt, wt, b, s, bw_ds, bw_mb, tril):
    nq = s // TQ
    topk = min(DSA_TOPK_MAX, s // 4)
    nblk = s // MOBA_BLOCK
    topb = min(MOBA_TOPK, nblk - 1)
    out = jax.ShapeDtypeStruct((b * s, N_HEADS * HEAD_DIM), BF16)
    return pl.pallas_call(
        functools.partial(_dsa_moba_kernel, topk=topk, nblk=nblk, topb=topb),
        grid=(b, nq),
        in_specs=_attn_specs(nq, G_DS_Q, G_DS_K, M_DS, s) + [
            pl.BlockSpec((4, TQ, HEAD_DIM), lambda b_, i: (G_QI_A, b_ * nq + i, 0)),
            pl.BlockSpec((4, TQ, HEAD_DIM), lambda b_, i: (G_QI_B, b_ * nq + i, 0)),
            pl.BlockSpec((1, s, HEAD_DIM), lambda b_, i: (S_KIDX, b_, 0)),
            pl.BlockSpec((IDX_HEADS, TQ), lambda b_, i: (0, b_ * nq + i)),
            _const_spec(BIAS_ROWS_SHAPE), _const_spec((TK, TK))]
        + _attn_specs(nq, G_MB_Q, G_MB_K, M_MB, s) + [_const_spec(BIAS_ROWS_SHAPE)],
        out_specs=[_out_spec(nq), _out_spec(nq)],
        out_shape=[out, out],
        scratch_shapes=[pltpu.VMEM((nq, TK, TQ), F32), pltpu.VMEM(BIAS_TILES_SHAPE, F32),
                        pltpu.VMEM((N_HEADS, max(8, nblk), HEAD_DIM), F32), pltpu.VMEM(BIAS_TILES_SHAPE, F32)],
        compiler_params=_cparams(2),
        name="dsa_moba_attn",
    )(zz, zz, vt, zz, zz, zz, wt, bw_ds, tril, zz, zz, vt, bw_mb)


def _merge_kernel(x_ref, osb_ref, odf_ref, ods_ref, omb_ref, gpre_ref, wg_ref, wbr_ref, wout_ref, gpost_ref, o_ref):
    x = x_ref[...]
    h = _rms(x, gpre_ref[...]).astype(BF16)
    y = jnp.zeros((x.shape[0], D_MODEL), F32)
    for r, o_r in enumerate((osb_ref, odf_ref, ods_ref, omb_ref)):
        gate = jax.nn.sigmoid(_dot(h, wg_ref[:, r * D_MODEL:(r + 1) * D_MODEL]))
        y = y + gate * _dot(o_r[...], wbr_ref[r])
    o_ref[...] = x + _rms(_dot(y.astype(BF16), wout_ref[...]), gpost_ref[...])


def _merge(x, o_sb, o_df, o_ds, o_mb, g_pre, w_gate, w_br, w_out, g_post, layer):
    t = x.shape[0]
    tm = TM_MERGE
    tok = lambda width: pl.BlockSpec((tm, width), lambda i: (i, 0))
    return pl.pallas_call(
        _merge_kernel,
        grid=(t // tm,),
        in_specs=[tok(D_MODEL)] + [tok(MIXER_WIDTH)] * N_MIXERS + [
            _const_spec((1, D_MODEL)), _layer_spec((D_MODEL, N_MIXERS * D_MODEL), layer),
            _layer_spec((N_MIXERS, MIXER_WIDTH, D_MODEL), layer), _layer_spec((D_MODEL, D_MODEL), layer),
            _const_spec((1, D_MODEL))],
        out_specs=tok(D_MODEL),
        out_shape=jax.ShapeDtypeStruct((t, D_MODEL), F32),
        compiler_params=_cparams(1),
        name="merge",
    )(x, o_sb, o_df, o_ds, o_mb, g_pre, w_gate, w_br, w_out, g_post)


def _ffn_kernel(x_ref, gpre_ref, win_ref, wout_ref, gpost_ref, o_ref):
    x = x_ref[...]
    h = _rms(x, gpre_ref[...]).astype(BF16)
    gate = _dot(h, win_ref[:, 0:D_FF])
    up = _dot(h, win_ref[:, D_FF:2 * D_FF])
    act = (gate * jax.nn.sigmoid(gate) * up).astype(BF16)
    o_ref[...] = x + _rms(_dot(act, wout_ref[...]), gpost_ref[...])


def _ffn(x, g_pre, w_in, w_out, g_post, layer):
    t = x.shape[0]
    tm = TM_FFN
    return pl.pallas_call(
        _ffn_kernel,
        grid=(t // tm,),
        in_specs=[pl.BlockSpec((tm, D_MODEL), lambda i: (i, 0)), _const_spec((1, D_MODEL)),
                  _layer_spec((D_MODEL, 2 * D_FF), layer), _layer_spec((D_FF, D_MODEL), layer),
                  _const_spec((1, D_MODEL))],
        out_specs=pl.BlockSpec((tm, D_MODEL), lambda i: (i, 0)),
        out_shape=jax.ShapeDtypeStruct((t, D_MODEL), F32),
        compiler_params=_cparams(1),
        name="ffn",
    )(x, g_pre, w_in, w_out, g_post)


def _t5_bucket(dist):
    max_exact = N_BUCKETS // 2
    d = jnp.maximum(dist, 0)
    log_ratio = jnp.log(jnp.maximum(d, 1).astype(F32) / max_exact) / math.log(MAX_DISTANCE / max_exact)
    large = jnp.minimum(max_exact + (log_ratio * (N_BUCKETS - max_exact)).astype(jnp.int32), N_BUCKETS - 1)
    return jnp.where(d < max_exact, d, large)


def _bias_rows(rel_bias):
    assert TQ == TK
    n = TK
    d = np.arange(-(n - 1), 3 * n + 1)
    by_dist = rel_bias.astype(F32).T[:, _t5_bucket(jnp.asarray(np.maximum(d, 0), jnp.int32))]
    by_dist = jnp.where(jnp.asarray(d >= 0)[None, :], by_dist * LOG2E, NEG)
    return jnp.stack([by_dist[:, o * n:o * n + 2 * n] for o in range(3)], axis=1)[:, :, None, :]


def _pack_runs():
    runs, start = [], 0
    for e in range(1, N_PACK + 1):
        if e == N_PACK or _PACK_SRC[e] != _PACK_SRC[e - 1] + (1 if _PACK_SRC[e - 1] >= 0 else 0):
            runs.append((int(_PACK_SRC[start]), e - start))
            start = e
    return runs


_PACK_RUNS = _pack_runs()
PACK_ROWS = 128


def _pack_kernel(w_ref, pack_ref, gate_ref):
    w = w_ref[...]
    parts = [w[:, a:a + n] if a >= 0 else jnp.zeros((w.shape[0], n), F32) for a, n in _PACK_RUNS]
    pack_ref[...] = jnp.concatenate(parts, axis=1).astype(BF16)
    gate_ref[...] = w[:, _OFF["gate"]:].astype(BF16)


def _pack_weights(w_in):
    depth, d, n_in = w_in.shape
    row_block = lambda width: pl.BlockSpec((pl.Squeezed(), PACK_ROWS, width), lambda l, r: (l, r, 0))
    return pl.pallas_call(
        _pack_kernel,
        grid=(depth, d // PACK_ROWS),
        in_specs=[row_block(n_in)],
        out_specs=[row_block(N_PACK), row_block(N_MIXERS * D_MODEL)],
        out_shape=[jax.ShapeDtypeStruct((depth, d, N_PACK), BF16),
                   jax.ShapeDtypeStruct((depth, d, N_MIXERS * D_MODEL), BF16)],
        compiler_params=_cparams(2),
        name="pack_weights",
    )(w_in)


def kernel(x, w_in, w_br_sb, w_br_diff, w_br_dsa, w_br_moba, w_out, lambda_q1, lambda_k1, lambda_q2, lambda_k2,
           diff_subln_g, rel_bias, w_ffn_in, w_ffn_out, g_pre_mix, g_post_mix, g_pre_ffn, g_post_ffn):
    b, s, d = x.shape
    depth = w_in.shape[0]
    assert d == D_MODEL and s % TQ == 0 and s // MOBA_BLOCK >= 2
    t = b * s

    w_pack, w_gate = _pack_weights(w_in)
    w_br = jnp.stack([w_br_sb, w_br_diff, w_br_dsa, w_br_moba], axis=1).astype(BF16)
    w_o = w_out.astype(BF16)
    w_f1 = w_ffn_in.astype(BF16)
    w_f2 = w_ffn_out.astype(BF16)
    cs = jnp.asarray(_PACK_SCALE)[None, :]

    bw = _bias_rows(rel_bias)
    bw_df, bw_ds, bw_mb = bw[0:4], bw[4:8], bw[8:12]
    key = np.arange(TK)[:, None]
    qry = np.arange(TQ)[None, :]
    tri = jnp.asarray(key <= np.arange(TK)[None, :], BF16)
    tril = jnp.asarray(key >= np.arange(TK)[None, :], BF16)
    sb_mask = jnp.asarray(np.where(key < qry, 0.0, NEG), F32)

    xf = x.reshape(t, d)
    for l in range(depth):
        lam_init = 0.8 - 0.6 * math.exp(-0.3 * l)
        lamp = jnp.stack([lambda_q1[l], lambda_k1[l], lambda_q2[l], lambda_k2[l]]).astype(F32)
        cst = jnp.full((1, 128), lam_init, F32)
        zz, vt, wt = _proj(xf, g_pre_mix[l][None, :], w_pack, cs, l)
        o_sb = _sb_attention(zz, vt, b, s, tri, sb_mask)
        o_df = _diff_attention(zz, vt, b, s, bw_df, lamp, cst, diff_subln_g[l][:, None])
        o_ds, o_mb = _dsa_moba_attention(zz, vt, wt, b, s, bw_ds, bw_mb, tril)
        xf = _merge(xf, o_sb, o_df, o_ds, o_mb, g_pre_mix[l][None, :], w_gate, w_br, w_o, g_post_mix[l][None, :], l)
        xf = _ffn(xf, g_pre_ffn[l][None, :], w_f1, w_f2, g_post_ffn[l][None, :], l)
    return xf.reshape(b, s, d)
```

```python
import functools
import math

import numpy as np
import jax
import jax.numpy as jnp
from jax import lax
from jax.experimental import pallas as pl
from jax.experimental.pallas import tpu as pltpu

F32 = jnp.float32
BF16 = jnp.bfloat16

D_MODEL = 1024
HEAD_DIM = 64
N_HEADS = 4
N_MIXERS = 4
MIXER_WIDTH = N_HEADS * HEAD_DIM
DIFF_QK_DIM = 32
IDX_HEADS = 8
DSA_TOPK_MAX = 256
MOBA_BLOCK = 256
MOBA_TOPK = 3
N_BUCKETS = 32
MAX_DISTANCE = 128
D_FF = 2816
NORM_EPS = 1e-6

TQ = 256
TK = 256
NEG = -1e30
HALF_NEG = -0.5e30
BIG = 3e38
BISECT_WARMUP = 22
BISECT_TRIP = 2
BISECT_MAX_TRIPS = 134
LOG2E = math.log2(math.e)
SB_TAIL_CUTOFF = 120.0 * LOG2E
V_ROWS = 80
TM_PROJ = 512
TM_MERGE = 512
TM_FFN = 512
N_GROUP = 11
N_SLAB = 4 * N_GROUP
N_PACK = (N_GROUP + N_MIXERS) * MIXER_WIDTH
VMEM_LIMIT = 56 * 1024 * 1024

G_SB_Q, G_SB_K, G_DF_Q, G_DF_K, G_DS_Q, G_DS_K, G_QI_A, G_QI_B, G_MB_Q, G_MB_K, G_KIDX = range(N_GROUP)
S_KIDX = 4 * G_KIDX
M_SB, M_DF, M_DS, M_MB = range(N_MIXERS)


def _layout():
    off = {}
    acc = 0
    for name, sz in (("q_sb", 256), ("k_sb", 256), ("v_sb", 256), ("q1", 128), ("q2", 128), ("k1", 128),
                     ("k2", 128), ("v_df", 256), ("q_ds", 256), ("k_ds", 256), ("v_ds", 256), ("qi", 512),
                     ("ki", 64), ("wi", 8), ("q_mb", 256), ("k_mb", 256), ("v_mb", 256), ("gate", 4096)):
        off[name] = acc
        acc += sz
    return off


_OFF = _layout()


def _pack_layout():
    off = _OFF
    cols, scale = [], []

    def add(start, n, s=1.0):
        cols.extend(range(start, start + n))
        scale.extend([s] * n)

    hd = HEAD_DIM ** -0.5
    hd2 = hd * LOG2E
    df2 = DIFF_QK_DIM ** -0.5 * LOG2E
    add(off["q_sb"], 256, hd2); add(off["k_sb"], 256)
    for h in range(N_HEADS):
        add(off["q1"] + h * 32, 32, df2); add(off["q2"] + h * 32, 32, df2)
    for h in range(N_HEADS):
        add(off["k1"] + h * 32, 32); add(off["k2"] + h * 32, 32)
    add(off["q_ds"], 256, hd2); add(off["k_ds"], 256)
    add(off["qi"], 512, HEAD_DIM ** -0.5)
    add(off["q_mb"], 256, hd2); add(off["k_mb"], 256)
    add(off["ki"], 64); add(off["wi"], IDX_HEADS, IDX_HEADS ** -0.5)
    cols.extend([-1] * 184); scale.extend([1.0] * 184)
    for name in ("v_sb", "v_df", "v_ds", "v_mb"):
        add(off[name], MIXER_WIDTH)
    assert len(cols) == N_PACK
    return np.asarray(cols, np.int32), np.asarray(scale, np.float32)


_PACK_SRC, _PACK_SCALE = _pack_layout()


def _dot(a, b):
    return jnp.dot(a, b, preferred_element_type=F32)


def _dot_nt(a, b):
    return lax.dot_general(a, b, (((1,), (1,)), ((), ())), preferred_element_type=F32)


def _rms(x, g):
    return x * lax.rsqrt(jnp.mean(x * x, axis=-1, keepdims=True) + NORM_EPS) * g


def _cparams(n_axes):
    return pltpu.CompilerParams(dimension_semantics=("arbitrary",) * n_axes, vmem_limit_bytes=VMEM_LIMIT)


def _const_spec(shape):
    nd = len(shape)
    return pl.BlockSpec(shape, lambda *_: (0,) * nd, pipeline_mode=pl.Buffered(1))


def _layer_spec(shape, layer):
    nd = len(shape)
    return pl.BlockSpec((pl.Squeezed(),) + tuple(shape), lambda *_: (layer,) + (0,) * nd, pipeline_mode=pl.Buffered(1))


def _proj_kernel(x_ref, g_ref, w_ref, cs_ref, zz_ref, vt_ref, wt_ref):
    h = _rms(x_ref[...], g_ref[...]).astype(BF16)

    def group(c):
        cols = slice(c * MIXER_WIDTH, (c + 1) * MIXER_WIDTH)
        return _dot(h, w_ref[:, cols]) * cs_ref[:, cols]

    for c in range(N_GROUP):
        r = group(c)
        for s in range(4):
            zz_ref[4 * c + s] = r[:, s * HEAD_DIM:(s + 1) * HEAD_DIM].astype(BF16)
        if c == G_KIDX:
            wt_ref[...] = jnp.transpose(r)[HEAD_DIM:HEAD_DIM + IDX_HEADS]
    for m in range(N_MIXERS):
        v = group(N_GROUP + m)
        for u in range(TM_PROJ // TK):
            vt = jnp.transpose(v[u * TK:(u + 1) * TK]).astype(BF16)
            for hh in range(N_HEADS):
                vt_ref[m, u, hh * V_ROWS:hh * V_ROWS + HEAD_DIM, :] = vt[hh * HEAD_DIM:(hh + 1) * HEAD_DIM]
                vt_ref[m, u, hh * V_ROWS + HEAD_DIM:(hh + 1) * V_ROWS, :] = jnp.ones((V_ROWS - HEAD_DIM, TK), BF16)


def _proj(x, g, w, cs, layer):
    t = x.shape[0]
    tm = TM_PROJ
    return pl.pallas_call(
        _proj_kernel,
        grid=(t // tm,),
        in_specs=[pl.BlockSpec((tm, D_MODEL), lambda i: (i, 0)),
                  _const_spec((1, D_MODEL)),
                  _layer_spec((D_MODEL, N_PACK), layer),
                  _const_spec((1, N_PACK))],
        out_specs=[pl.BlockSpec((N_SLAB, tm, HEAD_DIM), lambda i: (0, i, 0)),
                   pl.BlockSpec((N_MIXERS, tm // TK, N_HEADS * V_ROWS, TK), lambda i: (0, i, 0, 0)),
                   pl.BlockSpec((IDX_HEADS, tm), lambda i: (0, i))],
        out_shape=[jax.ShapeDtypeStruct((N_SLAB, t, HEAD_DIM), BF16),
                   jax.ShapeDtypeStruct((N_MIXERS, t // TK, N_HEADS * V_ROWS, TK), BF16),
                   jax.ShapeDtypeStruct((IDX_HEADS, t), F32)],
        compiler_params=_cparams(1),
        name="proj",
    )(x, g, w, cs)


def _k_block(ref, h, j):
    return ref[h, pl.ds(pl.multiple_of(j * TK, TK), TK), :]


def _fold_keys(a, op):
    n = a.shape[0]
    while n > 8:
        n //= 2
        a = op(a[:n], a[n:2 * n])
    return a


def _vt_block(ref, h, j, rows=V_ROWS):
    return ref[0, j, h * V_ROWS:h * V_ROWS + rows, :]


def _softmax_block(s_list, vt_list, carry):
    ms = [jnp.maximum(c[0], jnp.max(_fold_keys(s, jnp.maximum), axis=0, keepdims=True))
          for s, c in zip(s_list, carry)]
    pvs = [_dot(vt, jnp.exp2(s - m).astype(BF16)) for vt, s, m in zip(vt_list, s_list, ms)]
    return tuple((m_new, jnp.exp2(m - m_new) * acc + pv) for (m, acc), m_new, pv in zip(carry, ms, pvs))


def _softmax_loop(n_blocks, logits, values, carry, groups=(2, 1)):
    assert groups[-1] == 1
    start = 0
    for g in groups:
        def body(t, carry, g=g, start=start):
            j = start + g * t
            s_all = [logits(j + u) for u in range(g)]
            for u in range(g):
                carry = _softmax_block(s_all[u], values(j + u), carry)
            return carry

        trips = (n_blocks - start) // g
        carry = lax.fori_loop(0, trips, body, carry)
        start = start + g * trips
    return carry


def _softmax_init():
    return (jnp.full((1, TQ), NEG, F32), jnp.zeros((V_ROWS, TQ), F32))


def _softmax_out(carry):
    _, acc = carry
    return acc[:HEAD_DIM] / acc[HEAD_DIM:HEAD_DIM + 1]


def _store_heads(o_ref, heads_t):
    o_ref[...] = jnp.transpose(jnp.concatenate(heads_t, axis=0)).astype(BF16)


BIAS_ROWS_SHAPE = (N_HEADS, 3, 1, 2 * TQ)
BIAS_TILES_SHAPE = (N_HEADS, 3, TK, TQ)


def _fill_bias_tiles(bw_ref, bt_ref):
    @pl.when((pl.program_id(0) == 0) & (pl.program_id(1) == 0))
    def _():
        for h in range(N_HEADS):
            for o in range(3):
                rows = jnp.broadcast_to(bw_ref[h, o], (TK, 2 * TQ))
                bt_ref[h, o] = pltpu.roll(rows, TQ + 1, 1, stride=1, stride_axis=0)[:, :TQ]


def _attn_specs(nq, gq, gk, mixer, s):
    return [pl.BlockSpec((4, TQ, HEAD_DIM), lambda b, i: (gq, b * nq + i, 0)),
            pl.BlockSpec((4, s, HEAD_DIM), lambda b, i: (gk, b, 0)),
            pl.BlockSpec((1, s // TK, N_HEADS * V_ROWS, TK), lambda b, i: (mixer, b, 0, 0))]


def _out_spec(nq):
    return pl.BlockSpec((TQ, N_HEADS * HEAD_DIM), lambda b, i: (b * nq + i, 0))


def _sb_kernel(q_ref, k_ref, vt_ref, tri_ref, mask_ref, o_ref):
    i = pl.program_id(1)
    tri = tri_ref[...]

    def blocks(js, carry, first_masked):
        heads = range(N_HEADS)
        zs = [[_dot_nt(_k_block(k_ref, h, j), q_ref[h]) for h in heads] for j in js]
        if first_masked:
            zs[0] = [z + mask_ref[...] for z in zs[0]]
        sps = [[jnp.maximum(z, 0.0) + jnp.log2(1.0 + jnp.exp2(-jnp.abs(z))) for z in zb] for zb in zs]
        his = [[sp.astype(BF16) for sp in sb] for sb in sps]
        los = [[(sp - hi.astype(F32)).astype(BF16) for sp, hi in zip(sb, hb)] for sb, hb in zip(sps, his)]
        cums = [[_dot(tri, hi) + _dot(tri, lo) for hi, lo in zip(hb, lb)] for hb, lb in zip(his, los)]
        out = []
        for h in heads:
            tail, acc = carry[h]
            for u, j in enumerate(js):
                c = cums[u][h] + tail
                acc = acc + _dot(_vt_block(vt_ref, h, j, HEAD_DIM), jnp.exp2(zs[u][h] - c).astype(BF16))
                tail = c[0:1, :]
            out.append((tail, acc))
        return tuple(out)

    init = tuple((jnp.zeros((1, TQ), F32), jnp.zeros((HEAD_DIM, TQ), F32)) for _ in range(N_HEADS))
    carry = lax.cond(i >= 1, lambda c: blocks([i, i - 1], c, True), lambda c: blocks([i], c, True), init)

    def weights_left(carry):
        tail = functools.reduce(jnp.minimum, [c[0] for c in carry])
        return (jnp.min(tail) < SB_TAIL_CUTOFF).astype(jnp.int32)

    def earlier_block(c):
        j, carry, _ = c
        carry = blocks([j], carry, False)
        return j - 1, carry, weights_left(carry)

    _, carry, _ = lax.while_loop(lambda c: (c[0] >= 0) & (c[2] > 0), earlier_block, (i - 2, carry, weights_left(carry)))
    _store_heads(o_ref, [c[1] for c in carry])


def _sb_attention(zz, vt, b, s, tri, mask):
    nq = s // TQ
    return pl.pallas_call(
        _sb_kernel,
        grid=(b, nq),
        in_specs=_attn_specs(nq, G_SB_Q, G_SB_K, M_SB, s) + [_const_spec((TK, TK)), _const_spec((TK, TQ))],
        out_specs=_out_spec(nq),
        out_shape=jax.ShapeDtypeStruct((b * s, N_HEADS * HEAD_DIM), BF16),
        compiler_params=_cparams(2),
        name="sb_attn",
    )(zz, zz, vt, tri, mask)


def _diff_kernel(q_ref, k_ref, vt_ref, bw_ref, lam_ref, cst_ref, g_ref, o_ref, bt_ref):
    i = pl.program_id(1)
    _fill_bias_tiles(bw_ref, bt_ref)
    lp = lam_ref[...]
    lam_init = cst_ref[:, 0:1]
    lam = (jnp.exp(jnp.sum(lp[0:1] * lp[1:2], axis=-1, keepdims=True))
           - jnp.exp(jnp.sum(lp[2:3] * lp[3:4], axis=-1, keepdims=True)) + lam_init)
    lane = lax.broadcasted_iota(jnp.int32, (TQ, HEAD_DIM), 1)
    qs = []
    for h in range(N_HEADS):
        q = q_ref[h]
        qs.append((jnp.where(lane < DIFF_QK_DIM, q, jnp.zeros_like(q)),
                   jnp.where(lane >= DIFF_QK_DIM, q, jnp.zeros_like(q))))

    def logits(j):
        s_list = []
        for h in range(N_HEADS):
            kj = _k_block(k_ref, h, j)
            bias = bt_ref[h, jnp.minimum(i - j, 2)]
            s_list += [_dot_nt(kj, qs[h][0]) + bias, _dot_nt(kj, qs[h][1]) + bias]
        return s_list

    def values(j):
        return [_vt_block(vt_ref, h, j) for h in range(N_HEADS) for _ in range(2)]

    carry = _softmax_loop(i + 1, logits, values, tuple(_softmax_init() for _ in range(2 * N_HEADS)), groups=(4, 2, 1))
    outs = []
    for h in range(N_HEADS):
        o = _softmax_out(carry[2 * h]) - lam * _softmax_out(carry[2 * h + 1])
        o = o * lax.rsqrt(jnp.mean(o * o, axis=0, keepdims=True) + NORM_EPS) * g_ref[...]
        outs.append(o * (1.0 - lam_init))
    _store_heads(o_ref, outs)


def _diff_attention(zz, vt, b, s, bw, lamp, cst, g):
    nq = s // TQ
    return pl.pallas_call(
        _diff_kernel,
        grid=(b, nq),
        in_specs=_attn_specs(nq, G_DF_Q, G_DF_K, M_DF, s) + [
            _const_spec(BIAS_ROWS_SHAPE), _const_spec((4, DIFF_QK_DIM)), _const_spec((1, 128)),
            _const_spec((HEAD_DIM, 1))],
        out_specs=_out_spec(nq),
        out_shape=jax.ShapeDtypeStruct((b * s, N_HEADS * HEAD_DIM), BF16),
        scratch_shapes=[pltpu.VMEM(BIAS_TILES_SHAPE, F32)],
        compiler_params=_cparams(2),
        name="diff_attn",
    )(zz, zz, vt, bw, lamp, cst, g)


def _dsa_moba_kernel(q_ref, k_ref, vt_ref, qia_ref, qib_ref, ki_ref, wt_ref, bw_ref, tril_ref,
                     mq_ref, mk_ref, mvt_ref, mbw_ref, o_ref, omb_ref, sc_ref, bt_ref, km_ref, mbt_ref, *,
                     topk, nblk, topb):
    i = pl.program_id(1)
    _fill_bias_tiles(bw_ref, bt_ref)
    _fill_bias_tiles(mbw_ref, mbt_ref)
    nb = i + 1
    kf = float(topk)
    w = wt_ref[...]
    key = lax.broadcasted_iota(jnp.int32, (TK, TQ), 0)
    qry = lax.broadcasted_iota(jnp.int32, (TK, TQ), 1)

    def score(j):
        kij = _k_block(ki_ref, 0, j)
        sc = jnp.zeros((TK, TQ), F32)
        for hh in range(IDX_HEADS):
            qi = (qia_ref if hh < 4 else qib_ref)[hh % 4]
            sc = sc + w[hh:hh + 1, :] * jnp.maximum(_dot_nt(kij, qi), 0.0)
        return sc

    def scored(j):
        sc = score(j)
        valid = key - qry <= (i - j) * TQ
        masked = jnp.where(valid, sc, NEG)
        sc_ref[j] = masked
        return masked, jnp.where(valid, sc, BIG)

    c = (jnp.full((8, TQ), BIG, F32), jnp.full((8, TQ), -BIG, F32))
    start = 0
    for g in (4, 2, 1):
        def body(t, c, g=g, start=start):
            tiles = [scored(start + g * t + u) for u in range(g)]
            for masked, lo_src in tiles:
                c = (jnp.minimum(c[0], _fold_keys(lo_src, jnp.minimum)),
                     jnp.maximum(c[1], _fold_keys(masked, jnp.maximum)))
            return c

        trips = (nb - start) // g
        c = lax.fori_loop(0, trips, body, c)
        start = start + g * trips
    lo_part, hi_part = c

    def reduce_blocks(fn, init):
        def pair(t, c):
            return fn(sc_ref[2 * t + 1], 2 * t + 1, fn(sc_ref[2 * t], 2 * t, c))
        c = lax.fori_loop(0, nb // 2, pair, init)
        return lax.fori_loop(2 * (nb // 2), nb, lambda j, c: fn(sc_ref[j], j, c), c)

    def count_ge(t):
        part = reduce_blocks(lambda x, j, c: c + _fold_keys(jnp.where(x >= t, 1.0, 0.0), jnp.add),
                             jnp.zeros((8, TQ), F32))
        return jnp.sum(part, axis=0, keepdims=True)

    def minmax_blocks(lo_of, hi_of):
        def f(x, j, c):
            return (jnp.minimum(c[0], _fold_keys(lo_of(x), jnp.minimum)),
                    jnp.maximum(c[1], _fold_keys(hi_of(x), jnp.maximum)))
        lo_part, hi_part = reduce_blocks(f, (jnp.full((8, TQ), BIG, F32), jnp.full((8, TQ), -BIG, F32)))
        return jnp.min(lo_part, axis=0, keepdims=True), jnp.max(hi_part, axis=0, keepdims=True)

    n_valid = i * TQ + lax.broadcasted_iota(jnp.int32, (1, TQ), 1) + 1
    take_all = n_valid <= topk
    lo = jnp.min(lo_part, axis=0, keepdims=True)
    hi = jnp.max(hi_part, axis=0, keepdims=True)
    c_max = count_ge(hi)
    at_max = c_max >= kf
    state = (jnp.where(at_max, hi, lo), hi, jnp.where(at_max, c_max, n_valid.astype(F32)), c_max)

    def bisect(_, state):
        lo, hi, c_lo, c_hi = state
        mid = 0.5 * lo + 0.5 * hi
        c = count_ge(mid)
        ge = c >= kf
        return jnp.where(ge, mid, lo), jnp.where(ge, hi, mid), jnp.where(ge, c, c_lo), jnp.where(ge, c_hi, c)

    def unsettled(state):
        lo, hi, c_lo, _ = state
        open_q = jnp.where(take_all, 0.0, jnp.where(c_lo != kf, jnp.where(lo < hi, 1.0, 0.0), 0.0))

        def band_spread():
            b_min, b_max = minmax_blocks(lambda x: jnp.where((x >= lo) & (x < hi), x, BIG),
                                         lambda x: jnp.where((x >= lo) & (x < hi), x, -BIG))
            return (jnp.max(jnp.where(b_max != b_min, open_q, 0.0)) > 0.0).astype(jnp.int32)

        return lax.cond(jnp.max(open_q) > 0.0, band_spread, lambda: jnp.int32(0))

    state = lax.fori_loop(0, jnp.where((i + 1) * TQ <= topk, 0, BISECT_WARMUP), bisect, state)

    def trip(c):
        n, state, _ = c
        state = lax.fori_loop(0, BISECT_TRIP, bisect, state)
        return n + 1, state, unsettled(state)

    _, state, _ = lax.while_loop(lambda c: (c[2] > 0) & (c[0] < BISECT_MAX_TRIPS), trip,
                                 (jnp.int32(0), state, unsettled(state)))
    lo, hi, c_lo, c_hi = state
    hi_ok = lo < hi
    c_above = jnp.where(hi_ok, c_hi, 0.0)
    hi_sel = jnp.where(hi_ok, hi, BIG)
    need = jnp.where(take_all, BIG, kf - c_above)
    lo_sel = jnp.where(take_all, HALF_NEG, lo)

    tied = jnp.max(jnp.where(take_all, 0.0, c_lo - kf)) > 0.0

    @pl.when(tied)
    def _():
        tril = tril_ref[...]

        def band_of(j):
            x = sc_ref[j]
            return x, jnp.where(x >= lo_sel, jnp.where(x < hi_sel, 1.0, 0.0), 0.0)

        def write(j, x, band, rank):
            sc_ref[j] = jnp.where(x >= hi_sel, 0.0,
                                  jnp.where(band * rank > 0.0, jnp.where(rank <= need, 0.0, NEG), NEG))
            return rank[TK - 1:TK, :]

        def write_mask(j, taken):
            x, band = band_of(j)
            return write(j, x, band, _dot(tril, band.astype(BF16)) + taken)

        def write_mask_pair(t, taken):
            (x_a, band_a), (x_b, band_b) = band_of(2 * t), band_of(2 * t + 1)
            in_a, in_b = _dot(tril, band_a.astype(BF16)), _dot(tril, band_b.astype(BF16))
            taken = write(2 * t, x_a, band_a, in_a + taken)
            return write(2 * t + 1, x_b, band_b, in_b + taken)

        taken = lax.fori_loop(0, nb // 2, write_mask_pair, jnp.zeros((1, TQ), F32))
        lax.fori_loop(2 * (nb // 2), nb, write_mask, taken)

    @pl.when(jnp.logical_not(tied))
    def _():
        def write_mask(j, _):
            sc_ref[j] = jnp.where(sc_ref[j] >= lo_sel, 0.0, NEG)
            return 0

        lax.fori_loop(0, nb, write_mask, 0)

    head_bits = _moba_block_choice(i, mq_ref, mk_ref, km_ref, nblk, topb)

    def logits(j):
        dsa = [_dot_nt(_k_block(k_ref, h, j), q_ref[h]) + bt_ref[h, jnp.minimum(i - j, 2)] + sc_ref[j]
               for h in range(N_HEADS)]
        moba = []
        for h in range(N_HEADS):
            picked = (lax.shift_right_logical(head_bits[h], jnp.full_like(head_bits[h], j)) & 1) == 1
            moba.append(_dot_nt(_k_block(mk_ref, h, j), mq_ref[h]) + mbt_ref[h, jnp.minimum(i - j, 2)]
                        + jnp.where(picked, 0.0, NEG))
        return dsa + moba

    def values(j):
        return [_vt_block(vt_ref, h, j) for h in range(N_HEADS)] + [_vt_block(mvt_ref, h, j) for h in range(N_HEADS)]

    carry = _softmax_loop(nb, logits, values, tuple(_softmax_init() for _ in range(2 * N_HEADS)), groups=(4, 2, 1))
    _store_heads(o_ref, [_softmax_out(c) for c in carry[:N_HEADS]])
    _store_heads(omb_ref, [_softmax_out(c) for c in carry[N_HEADS:]])


def _moba_block_choice(i, q_ref, k_ref, km_ref, nblk, topb):
    nrow = km_ref.shape[1]

    @pl.when(i == 0)
    def _():
        km_ref[...] = jnp.zeros_like(km_ref)
        for h in range(N_HEADS):
            for n in range(nblk):
                kb = k_ref[h, n * MOBA_BLOCK:(n + 1) * MOBA_BLOCK, :].astype(F32)
                km_ref[h, n:n + 1, :] = jnp.mean(kb, axis=0, keepdims=True)

    blk = lax.broadcasted_iota(jnp.int32, (nrow, TQ), 0)
    past = blk < i
    head_bits = []
    for h in range(N_HEADS):
        gate = _dot_nt(km_ref[h].astype(BF16), q_ref[h])
        bits = jnp.zeros((1, TQ), F32)
        for n in range(nblk):
            gn = gate[n:n + 1, :]
            beats = jnp.where(past, jnp.where(gate > gn, 1.0, jnp.where(gate == gn, jnp.where(blk < n, 1.0, 0.0), 0.0)), 0.0)
            rank = jnp.sum(beats, axis=0, keepdims=True)
            bits = bits + jnp.where(rank < float(topb), jnp.where(n < i, float(2 ** n), 0.0), 0.0)
        head_bits.append(bits.astype(jnp.int32) | lax.shift_left(jnp.int32(1), i))
    return head_bits


def _dsa_moba_attention(zz, vt, wt, b, s, bw_ds, bw_mb, tril):
    nq = s // TQ
    topk = min(DSA_TOPK_MAX, s // 4)
    nblk = s // MOBA_BLOCK
    topb = min(MOBA_TOPK, nblk - 1)
    out = jax.ShapeDtypeStruct((b * s, N_HEADS * HEAD_DIM), BF16)
    return pl.pallas_call(
        functools.partial(_dsa_moba_kernel, topk=topk, nblk=nblk, topb=topb),
        grid=(b, nq),
        in_specs=_attn_specs(nq, G_DS_Q, G_DS_K, M_DS, s) + [
            pl.BlockSpec((4, TQ, HEAD_DIM), lambda b_, i: (G_QI_A, b_ * nq + i, 0)),
            pl.BlockSpec((4, TQ, HEAD_DIM), lambda b_, i: (G_QI_B, b_ * nq + i, 0)),
            pl.BlockSpec((1, s, HEAD_DIM), lambda b_, i: (S_KIDX, b_, 0)),
            pl.BlockSpec((IDX_HEADS, TQ), lambda b_, i: (0, b_ * nq + i)),
            _const_spec(BIAS_ROWS_SHAPE), _const_spec((TK, TK))]
        + _attn_specs(nq, G_MB_Q, G_MB_K, M_MB, s) + [_const_spec(BIAS_ROWS_SHAPE)],
        out_specs=[_out_spec(nq), _out_spec(nq)],
        out_shape=[out, out],
        scratch_shapes=[pltpu.VMEM((nq, TK, TQ), F32), pltpu.VMEM(BIAS_TILES_SHAPE, F32),
                        pltpu.VMEM((N_HEADS, max(8, nblk), HEAD_DIM), F32), pltpu.VMEM(BIAS_TILES_SHAPE, F32)],
        compiler_params=_cparams(2),
        name="dsa_moba_attn",
    )(zz, zz, vt, zz, zz, zz, wt, bw_ds, tril, zz, zz, vt, bw_mb)


def _merge_kernel(x_ref, osb_ref, odf_ref, ods_ref, omb_ref, gpre_ref, wg_ref, wbr_ref, wout_ref, gpost_ref, o_ref):
    x = x_ref[...]
    h = _rms(x, gpre_ref[...]).astype(BF16)
    y = jnp.zeros((x.shape[0], D_MODEL), F32)
    for r, o_r in enumerate((osb_ref, odf_ref, ods_ref, omb_ref)):
        gate = jax.nn.sigmoid(_dot(h, wg_ref[:, r * D_MODEL:(r + 1) * D_MODEL]))
        y = y + gate * _dot(o_r[...], wbr_ref[r])
    o_ref[...] = x + _rms(_dot(y.astype(BF16), wout_ref[...]), gpost_ref[...])


def _merge(x, o_sb, o_df, o_ds, o_mb, g_pre, w_gate, w_br, w_out, g_post, layer):
    t = x.shape[0]
    tm = TM_MERGE
    tok = lambda width: pl.BlockSpec((tm, width), lambda i: (i, 0))
    return pl.pallas_call(
        _merge_kernel,
        grid=(t // tm,),
        in_specs=[tok(D_MODEL)] + [tok(MIXER_WIDTH)] * N_MIXERS + [
            _const_spec((1, D_MODEL)), _layer_spec((D_MODEL, N_MIXERS * D_MODEL), layer),
            _layer_spec((N_MIXERS, MIXER_WIDTH, D_MODEL), layer), _layer_spec((D_MODEL, D_MODEL), layer),
            _const_spec((1, D_MODEL))],
        out_specs=tok(D_MODEL),
        out_shape=jax.ShapeDtypeStruct((t, D_MODEL), F32),
        compiler_params=_cparams(1),
        name="merge",
    )(x, o_sb, o_df, o_ds, o_mb, g_pre, w_gate, w_br, w_out, g_post)


def _ffn_kernel(x_ref, gpre_ref, win_ref, wout_ref, gpost_ref, o_ref):
    x = x_ref[...]
    h = _rms(x, gpre_ref[...]).astype(BF16)
    gate = _dot(h, win_ref[:, 0:D_FF])
    up = _dot(h, win_ref[:, D_FF:2 * D_FF])
    act = (gate * jax.nn.sigmoid(gate) * up).astype(BF16)
    o_ref[...] = x + _rms(_dot(act, wout_ref[...]), gpost_ref[...])


def _ffn(x, g_pre, w_in, w_out, g_post, layer):
    t = x.shape[0]
    tm = TM_FFN
    return pl.pallas_call(
        _ffn_kernel,
        grid=(t // tm,),
        in_specs=[pl.BlockSpec((tm, D_MODEL), lambda i: (i, 0)), _const_spec((1, D_MODEL)),
                  _layer_spec((D_MODEL, 2 * D_FF), layer), _layer_spec((D_FF, D_MODEL), layer),
                  _const_spec((1, D_MODEL))],
        out_specs=pl.BlockSpec((tm, D_MODEL), lambda i: (i, 0)),
        out_shape=jax.ShapeDtypeStruct((t, D_MODEL), F32),
        compiler_params=_cparams(1),
        name="ffn",
    )(x, g_pre, w_in, w_out, g_post)


def _t5_bucket(dist):
    max_exact = N_BUCKETS // 2
    d = jnp.maximum(dist, 0)
    log_ratio = jnp.log(jnp.maximum(d, 1).astype(F32) / max_exact) / math.log(MAX_DISTANCE / max_exact)
    large = jnp.minimum(max_exact + (log_ratio * (N_BUCKETS - max_exact)).astype(jnp.int32), N_BUCKETS - 1)
    return jnp.where(d < max_exact, d, large)


def _bias_rows(rel_bias):
    assert TQ == TK
    n = TK
    d = np.arange(-(n - 1), 3 * n + 1)
    by_dist = rel_bias.astype(F32).T[:, _t5_bucket(jnp.asarray(np.maximum(d, 0), jnp.int32))]
    by_dist = jnp.where(jnp.asarray(d >= 0)[None, :], by_dist * LOG2E, NEG)
    return jnp.stack([by_dist[:, o * n:o * n + 2 * n] for o in range(3)], axis=1)[:, :, None, :]


def _pack_runs():
    runs, start = [], 0
    for e in range(1, N_PACK + 1):
        if e == N_PACK or _PACK_SRC[e] != _PACK_SRC[e - 1] + (1 if _PACK_SRC[e - 1] >= 0 else 0):
            runs.append((int(_PACK_SRC[start]), e - start))
            start = e
    return runs


_PACK_RUNS = _pack_runs()
PACK_ROWS = 128


def _pack_kernel(w_ref, pack_ref, gate_ref):
    w = w_ref[...]
    parts = [w[:, a:a + n] if a >= 0 else jnp.zeros((w.shape[0], n), F32) for a, n in _PACK_RUNS]
    pack_ref[...] = jnp.concatenate(parts, axis=1).astype(BF16)
    gate_ref[...] = w[:, _OFF["gate"]:].astype(BF16)


def _pack_weights(w_in):
    depth, d, n_in = w_in.shape
    row_block = lambda width: pl.BlockSpec((pl.Squeezed(), PACK_ROWS, width), lambda l, r: (l, r, 0))
    return pl.pallas_call(
        _pack_kernel,
        grid=(depth, d // PACK_ROWS),
        in_specs=[row_block(n_in)],
        out_specs=[row_block(N_PACK), row_block(N_MIXERS * D_MODEL)],
        out_shape=[jax.ShapeDtypeStruct((depth, d, N_PACK), BF16),
                   jax.ShapeDtypeStruct((depth, d, N_MIXERS * D_MODEL), BF16)],
        compiler_params=_cparams(2),
        name="pack_weights",
    )(w_in)


def kernel(x, w_in, w_br_sb, w_br_diff, w_br_dsa, w_br_moba, w_out, lambda_q1, lambda_k1, lambda_q2, lambda_k2,
           diff_subln_g, rel_bias, w_ffn_in, w_ffn_out, g_pre_mix, g_post_mix, g_pre_ffn, g_post_ffn):
    b, s, d = x.shape
    depth = w_in.shape[0]
    assert d == D_MODEL and s % TQ == 0 and s // MOBA_BLOCK >= 2
    t = b * s

    w_pack, w_gate = _pack_weights(w_in)
    w_br = jnp.stack([w_br_sb, w_br_diff, w_br_dsa, w_br_moba], axis=1).astype(BF16)
    w_o = w_out.astype(BF16)
    w_f1 = w_ffn_in.astype(BF16)
    w_f2 = w_ffn_out.astype(BF16)
    cs = jnp.asarray(_PACK_SCALE)[None, :]

    bw = _bias_rows(rel_bias)
    bw_df, bw_ds, bw_mb = bw[0:4], bw[4:8], bw[8:12]
    key = np.arange(TK)[:, None]
    qry = np.arange(TQ)[None, :]
    tri = jnp.asarray(key <= np.arange(TK)[None, :], BF16)
    tril = jnp.asarray(key >= np.arange(TK)[None, :], BF16)
    sb_mask = jnp.asarray(np.where(key < qry, 0.0, NEG), F32)

    xf = x.reshape(t, d)
    for l in range(depth):
        lam_init = 0.8 - 0.6 * math.exp(-0.3 * l)
        lamp = jnp.stack([lambda_q1[l], lambda_k1[l], lambda_q2[l], lambda_k2[l]]).astype(F32)
        cst = jnp.full((1, 128), lam_init, F32)
        zz, vt, wt = _proj(xf, g_pre_mix[l][None, :], w_pack, cs, l)
        o_sb = _sb_attention(zz, vt, b, s, tri, sb_mask)
        o_df = _diff_attention(zz, vt, b, s, bw_df, lamp, cst, diff_subln_g[l][:, None])
        o_ds, o_mb = _dsa_moba_attention(zz, vt, wt, b, s, bw_ds, bw_mb, tril)
        xf = _merge(xf, o_sb, o_df, o_ds, o_mb, g_pre_mix[l][None, :], w_gate, w_br, w_o, g_post_mix[l][None, :], l)
        xf = _ffn(xf, g_pre_ffn[l][None, :], w_f1, w_f2, g_post_ffn[l][None, :], l)
    return xf.reshape(b, s, d)
```

```python
import functools
import math

import numpy as np
import jax
import jax.numpy as jnp
from jax import lax
from jax.experimental import pallas as pl
from jax.experimental.pallas import tpu as pltpu

F32 = jnp.float32
BF16 = jnp.bfloat16

D_MODEL = 1024
HEAD_DIM = 64
N_HEADS = 4
N_MIXERS = 4
MIXER_WIDTH = N_HEADS * HEAD_DIM
DIFF_QK_DIM = 32
IDX_HEADS = 8
DSA_TOPK_MAX = 256
MOBA_BLOCK = 256
MOBA_TOPK = 3
N_BUCKETS = 32
MAX_DISTANCE = 128
D_FF = 2816
NORM_EPS = 1e-6

TQ = 256
TK = 256
NEG = -1e30
HALF_NEG = -0.5e30
BIG = 3e38
BISECT_WARMUP = 22
BISECT_TRIP = 2
BISECT_MAX_TRIPS = 134
LOG2E = math.log2(math.e)
SB_TAIL_CUTOFF = 120.0 * LOG2E
V_ROWS = 80
TM_PROJ = 512
TM_MERGE = 512
TM_FFN = 512
N_GROUP = 11
N_SLAB = 4 * N_GROUP
N_PACK = (N_GROUP + N_MIXERS) * MIXER_WIDTH
VMEM_LIMIT = 56 * 1024 * 1024

G_SB_Q, G_SB_K, G_DF_Q, G_DF_K, G_DS_Q, G_DS_K, G_QI_A, G_QI_B, G_MB_Q, G_MB_K, G_KIDX = range(N_GROUP)
S_KIDX = 4 * G_KIDX
M_SB, M_DF, M_DS, M_MB = range(N_MIXERS)


def _layout():
    off = {}
    acc = 0
    for name, sz in (("q_sb", 256), ("k_sb", 256), ("v_sb", 256), ("q1", 128), ("q2", 128), ("k1", 128),
                     ("k2", 128), ("v_df", 256), ("q_ds", 256), ("k_ds", 256), ("v_ds", 256), ("qi", 512),
                     ("ki", 64), ("wi", 8), ("q_mb", 256), ("k_mb", 256), ("v_mb", 256), ("gate", 4096)):
        off[name] = acc
        acc += sz
    return off


_OFF = _layout()


def _pack_layout():
    off = _OFF
    cols, scale = [], []

    def add(start, n, s=1.0):
        cols.extend(range(start, start + n))
        scale.extend([s] * n)

    hd = HEAD_DIM ** -0.5
    hd2 = hd * LOG2E
    df2 = DIFF_QK_DIM ** -0.5 * LOG2E
    add(off["q_sb"], 256, hd2); add(off["k_sb"], 256)
    for h in range(N_HEADS):
        add(off["q1"] + h * 32, 32, df2); add(off["q2"] + h * 32, 32, df2)
    for h in range(N_HEADS):
        add(off["k1"] + h * 32, 32); add(off["k2"] + h * 32, 32)
    add(off["q_ds"], 256, hd2); add(off["k_ds"], 256)
    add(off["qi"], 512, HEAD_DIM ** -0.5)
    add(off["q_mb"], 256, hd2); add(off["k_mb"], 256)
    add(off["ki"], 64); add(off["wi"], IDX_HEADS, IDX_HEADS ** -0.5)
    cols.extend([-1] * 184); scale.extend([1.0] * 184)
    for name in ("v_sb", "v_df", "v_ds", "v_mb"):
        add(off[name], MIXER_WIDTH)
    assert len(cols) == N_PACK
    return np.asarray(cols, np.int32), np.asarray(scale, np.float32)


_PACK_SRC, _PACK_SCALE = _pack_layout()


def _dot(a, b):
    return jnp.dot(a, b, preferred_element_type=F32)


def _dot_nt(a, b):
    return lax.dot_general(a, b, (((1,), (1,)), ((), ())), preferred_element_type=F32)


def _rms(x, g):
    return x * lax.rsqrt(jnp.mean(x * x, axis=-1, keepdims=True) + NORM_EPS) * g


def _cparams(n_axes):
    return pltpu.CompilerParams(dimension_semantics=("arbitrary",) * n_axes, vmem_limit_bytes=VMEM_LIMIT)


def _const_spec(shape):
    nd = len(shape)
    return pl.BlockSpec(shape, lambda *_: (0,) * nd, pipeline_mode=pl.Buffered(1))


def _layer_spec(shape, layer):
    nd = len(shape)
    return pl.BlockSpec((pl.Squeezed(),) + tuple(shape), lambda *_: (layer,) + (0,) * nd, pipeline_mode=pl.Buffered(1))


def _proj_kernel(x_ref, g_ref, w_ref, cs_ref, zz_ref, vt_ref, wt_ref):
    h = _rms(x_ref[...], g_ref[...]).astype(BF16)

    def group(c):
        cols = slice(c * MIXER_WIDTH, (c + 1) * MIXER_WIDTH)
        return _dot(h, w_ref[:, cols]) * cs_ref[:, cols]

    for c in range(N_GROUP):
        r = group(c)
        for s in range(4):
            zz_ref[4 * c + s] = r[:, s * HEAD_DIM:(s + 1) * HEAD_DIM].astype(BF16)
        if c == G_KIDX:
            wt_ref[...] = jnp.transpose(r)[HEAD_DIM:HEAD_DIM + IDX_HEADS]
    for m in range(N_MIXERS):
        v = group(N_GROUP + m)
        for u in range(TM_PROJ // TK):
            vt = jnp.transpose(v[u * TK:(u + 1) * TK]).astype(BF16)
            for hh in range(N_HEADS):
                vt_ref[m, u, hh * V_ROWS:hh * V_ROWS + HEAD_DIM, :] = vt[hh * HEAD_DIM:(hh + 1) * HEAD_DIM]
                vt_ref[m, u, hh * V_ROWS + HEAD_DIM:(hh + 1) * V_ROWS, :] = jnp.ones((V_ROWS - HEAD_DIM, TK), BF16)


def _proj(x, g, w, cs, layer):
    t = x.shape[0]
    tm = TM_PROJ
    return pl.pallas_call(
        _proj_kernel,
        grid=(t // tm,),
        in_specs=[pl.BlockSpec((tm, D_MODEL), lambda i: (i, 0)),
                  _const_spec((1, D_MODEL)),
                  _layer_spec((D_MODEL, N_PACK), layer),
                  _const_spec((1, N_PACK))],
        out_specs=[pl.BlockSpec((N_SLAB, tm, HEAD_DIM), lambda i: (0, i, 0)),
                   pl.BlockSpec((N_MIXERS, tm // TK, N_HEADS * V_ROWS, TK), lambda i: (0, i, 0, 0)),
                   pl.BlockSpec((IDX_HEADS, tm), lambda i: (0, i))],
        out_shape=[jax.ShapeDtypeStruct((N_SLAB, t, HEAD_DIM), BF16),
                   jax.ShapeDtypeStruct((N_MIXERS, t // TK, N_HEADS * V_ROWS, TK), BF16),
                   jax.ShapeDtypeStruct((IDX_HEADS, t), F32)],
        compiler_params=_cparams(1),
        name="proj",
    )(x, g, w, cs)


def _k_block(ref, h, j):
    return ref[h, pl.ds(pl.multiple_of(j * TK, TK), TK), :]


def _fold_keys(a, op):
    n = a.shape[0]
    while n > 8:
        n //= 2
        a = op(a[:n], a[n:2 * n])
    return a


def _vt_block(ref, h, j, rows=V_ROWS):
    return ref[0, j, h * V_ROWS:h * V_ROWS + rows, :]


def _softmax_block(s_list, vt_list, carry):
    ms = [jnp.maximum(c[0], jnp.max(_fold_keys(s, jnp.maximum), axis=0, keepdims=True))
          for s, c in zip(s_list, carry)]
    pvs = [_dot(vt, jnp.exp2(s - m).astype(BF16)) for vt, s, m in zip(vt_list, s_list, ms)]
    return tuple((m_new, jnp.exp2(m - m_new) * acc + pv) for (m, acc), m_new, pv in zip(carry, ms, pvs))


def _softmax_loop(n_blocks, logits, values, carry, groups=(2, 1), first=0):
    assert groups[-1] == 1
    start = first
    for g in groups:
        def body(t, carry, g=g, start=start):
            j = start + g * t
            s_all = [logits(j + u) for u in range(g)]
            for u in range(g):
                carry = _softmax_block(s_all[u], values(j + u), carry)
            return carry

        trips = (n_blocks - start) // g
        carry = lax.fori_loop(0, trips, body, carry)
        start = start + g * trips
    return carry


def _softmax_init():
    return (jnp.full((1, TQ), NEG, F32), jnp.zeros((V_ROWS, TQ), F32))


def _softmax_out(carry):
    _, acc = carry
    return acc[:HEAD_DIM] / acc[HEAD_DIM:HEAD_DIM + 1]


def _store_heads(o_ref, heads_t):
    o_ref[...] = jnp.transpose(jnp.concatenate(heads_t, axis=0)).astype(BF16)


BIAS_ROWS_SHAPE = (N_HEADS, 3, 1, 2 * TQ)
BIAS_TILES_SHAPE = (N_HEADS, 3, TK, TQ)


def _fill_bias_tiles(bw_ref, bt_ref):
    @pl.when((pl.program_id(0) == 0) & (pl.program_id(1) == 0))
    def _():
        for h in range(N_HEADS):
            for o in range(3):
                rows = jnp.broadcast_to(bw_ref[h, o], (TK, 2 * TQ))
                bt_ref[h, o] = pltpu.roll(rows, TQ + 1, 1, stride=1, stride_axis=0)[:, :TQ]


def _attn_specs(nq, gq, gk, mixer, s):
    return [pl.BlockSpec((4, TQ, HEAD_DIM), lambda b, i: (gq, b * nq + i, 0)),
            pl.BlockSpec((4, s, HEAD_DIM), lambda b, i: (gk, b, 0)),
            pl.BlockSpec((1, s // TK, N_HEADS * V_ROWS, TK), lambda b, i: (mixer, b, 0, 0))]


def _out_spec(nq):
    return pl.BlockSpec((TQ, N_HEADS * HEAD_DIM), lambda b, i: (b * nq + i, 0))


def _sb_kernel(q_ref, k_ref, vt_ref, tri_ref, mask_ref, o_ref):
    i = pl.program_id(1)
    tri = tri_ref[...]

    def blocks(js, carry, first_masked):
        heads = range(N_HEADS)
        zs = [[_dot_nt(_k_block(k_ref, h, j), q_ref[h]) for h in heads] for j in js]
        if first_masked:
            zs[0] = [z + mask_ref[...] for z in zs[0]]
        sps = [[jnp.maximum(z, 0.0) + jnp.log2(1.0 + jnp.exp2(-jnp.abs(z))) for z in zb] for zb in zs]
        his = [[sp.astype(BF16) for sp in sb] for sb in sps]
        los = [[(sp - hi.astype(F32)).astype(BF16) for sp, hi in zip(sb, hb)] for sb, hb in zip(sps, his)]
        cums = [[_dot(tri, hi) + _dot(tri, lo) for hi, lo in zip(hb, lb)] for hb, lb in zip(his, los)]
        out = []
        for h in heads:
            tail, acc = carry[h]
            for u, j in enumerate(js):
                c = cums[u][h] + tail
                acc = acc + _dot(_vt_block(vt_ref, h, j, HEAD_DIM), jnp.exp2(zs[u][h] - c).astype(BF16))
                tail = c[0:1, :]
            out.append((tail, acc))
        return tuple(out)

    init = tuple((jnp.zeros((1, TQ), F32), jnp.zeros((HEAD_DIM, TQ), F32)) for _ in range(N_HEADS))
    carry = lax.cond(i >= 1, lambda c: blocks([i, i - 1], c, True), lambda c: blocks([i], c, True), init)

    def weights_left(carry):
        tail = functools.reduce(jnp.minimum, [c[0] for c in carry])
        return (jnp.min(tail) < SB_TAIL_CUTOFF).astype(jnp.int32)

    def earlier_block(c):
        j, carry, _ = c
        carry = blocks([j], carry, False)
        return j - 1, carry, weights_left(carry)

    _, carry, _ = lax.while_loop(lambda c: (c[0] >= 0) & (c[2] > 0), earlier_block, (i - 2, carry, weights_left(carry)))
    _store_heads(o_ref, [c[1] for c in carry])


def _sb_attention(zz, vt, b, s, tri, mask):
    nq = s // TQ
    return pl.pallas_call(
        _sb_kernel,
        grid=(b, nq),
        in_specs=_attn_specs(nq, G_SB_Q, G_SB_K, M_SB, s) + [_const_spec((TK, TK)), _const_spec((TK, TQ))],
        out_specs=_out_spec(nq),
        out_shape=jax.ShapeDtypeStruct((b * s, N_HEADS * HEAD_DIM), BF16),
        compiler_params=_cparams(2),
        name="sb_attn",
    )(zz, zz, vt, tri, mask)


def _diff_kernel(q_ref, k_ref, vt_ref, bw_ref, lam_ref, cst_ref, g_ref, o_ref, bt_ref):
    i = pl.program_id(1)
    _fill_bias_tiles(bw_ref, bt_ref)
    lp = lam_ref[...]
    lam_init = cst_ref[:, 0:1]
    lam = (jnp.exp(jnp.sum(lp[0:1] * lp[1:2], axis=-1, keepdims=True))
           - jnp.exp(jnp.sum(lp[2:3] * lp[3:4], axis=-1, keepdims=True)) + lam_init)
    lane = lax.broadcasted_iota(jnp.int32, (TQ, HEAD_DIM), 1)
    qs = []
    for h in range(N_HEADS):
        q = q_ref[h]
        qs.append((jnp.where(lane < DIFF_QK_DIM, q, jnp.zeros_like(q)),
                   jnp.where(lane >= DIFF_QK_DIM, q, jnp.zeros_like(q))))

    def logits(j):
        s_list = []
        for h in range(N_HEADS):
            kj = _k_block(k_ref, h, j)
            bias = bt_ref[h, jnp.minimum(i - j, 2)]
            s_list += [_dot_nt(kj, qs[h][0]) + bias, _dot_nt(kj, qs[h][1]) + bias]
        return s_list

    def values(j):
        return [_vt_block(vt_ref, h, j) for h in range(N_HEADS) for _ in range(2)]

    carry = _softmax_loop(i + 1, logits, values, tuple(_softmax_init() for _ in range(2 * N_HEADS)), groups=(4, 2, 1))
    outs = []
    for h in range(N_HEADS):
        o = _softmax_out(carry[2 * h]) - lam * _softmax_out(carry[2 * h + 1])
        o = o * lax.rsqrt(jnp.mean(o * o, axis=0, keepdims=True) + NORM_EPS) * g_ref[...]
        outs.append(o * (1.0 - lam_init))
    _store_heads(o_ref, outs)


def _diff_attention(zz, vt, b, s, bw, lamp, cst, g):
    nq = s // TQ
    return pl.pallas_call(
        _diff_kernel,
        grid=(b, nq),
        in_specs=_attn_specs(nq, G_DF_Q, G_DF_K, M_DF, s) + [
            _const_spec(BIAS_ROWS_SHAPE), _const_spec((4, DIFF_QK_DIM)), _const_spec((1, 128)),
            _const_spec((HEAD_DIM, 1))],
        out_specs=_out_spec(nq),
        out_shape=jax.ShapeDtypeStruct((b * s, N_HEADS * HEAD_DIM), BF16),
        scratch_shapes=[pltpu.VMEM(BIAS_TILES_SHAPE, F32)],
        compiler_params=_cparams(2),
        name="diff_attn",
    )(zz, zz, vt, bw, lamp, cst, g)


def _dsa_moba_kernel(q_ref, k_ref, vt_ref, qia_ref, qib_ref, ki_ref, wt_ref, bw_ref, tril_ref,
                     mq_ref, mk_ref, mvt_ref, mbw_ref, o_ref, omb_ref, sc_ref, bt_ref, km_ref, mbt_ref, *,
                     topk, nblk, topb):
    i = pl.program_id(1)
    _fill_bias_tiles(bw_ref, bt_ref)
    _fill_bias_tiles(mbw_ref, mbt_ref)
    nb = i + 1
    kf = float(topk)
    w = wt_ref[...]
    key = lax.broadcasted_iota(jnp.int32, (TK, TQ), 0)
    qry = lax.broadcasted_iota(jnp.int32, (TK, TQ), 1)

    def score(j):
        kij = _k_block(ki_ref, 0, j)
        sc = jnp.zeros((TK, TQ), F32)
        for hh in range(IDX_HEADS):
            qi = (qia_ref if hh < 4 else qib_ref)[hh % 4]
            sc = sc + w[hh:hh + 1, :] * jnp.maximum(_dot_nt(kij, qi), 0.0)
        return sc

    def scored(j):
        sc = score(j)
        valid = key - qry <= (i - j) * TQ
        masked = jnp.where(valid, sc, NEG)
        sc_ref[j] = masked
        return masked, jnp.where(valid, sc, BIG)

    c = (jnp.full((8, TQ), BIG, F32), jnp.full((8, TQ), -BIG, F32))
    start = 0
    for g in (4, 2, 1):
        def body(t, c, g=g, start=start):
            tiles = [scored(start + g * t + u) for u in range(g)]
            for masked, lo_src in tiles:
                c = (jnp.minimum(c[0], _fold_keys(lo_src, jnp.minimum)),
                     jnp.maximum(c[1], _fold_keys(masked, jnp.maximum)))
            return c

        trips = (nb - start) // g
        c = lax.fori_loop(0, trips, body, c)
        start = start + g * trips
    lo_part, hi_part = c

    def reduce_blocks(fn, init):
        def pair(t, c):
            return fn(sc_ref[2 * t + 1], 2 * t + 1, fn(sc_ref[2 * t], 2 * t, c))
        c = lax.fori_loop(0, nb // 2, pair, init)
        return lax.fori_loop(2 * (nb // 2), nb, lambda j, c: fn(sc_ref[j], j, c), c)

    def count_ge(t):
        part = reduce_blocks(lambda x, j, c: c + _fold_keys(jnp.where(x >= t, 1.0, 0.0), jnp.add),
                             jnp.zeros((8, TQ), F32))
        return jnp.sum(part, axis=0, keepdims=True)

    def minmax_blocks(lo_of, hi_of):
        def f(x, j, c):
            return (jnp.minimum(c[0], _fold_keys(lo_of(x), jnp.minimum)),
                    jnp.maximum(c[1], _fold_keys(hi_of(x), jnp.maximum)))
        lo_part, hi_part = reduce_blocks(f, (jnp.full((8, TQ), BIG, F32), jnp.full((8, TQ), -BIG, F32)))
        return jnp.min(lo_part, axis=0, keepdims=True), jnp.max(hi_part, axis=0, keepdims=True)

    n_valid = i * TQ + lax.broadcasted_iota(jnp.int32, (1, TQ), 1) + 1
    take_all = n_valid <= topk
    lo = jnp.min(lo_part, axis=0, keepdims=True)
    hi = jnp.max(hi_part, axis=0, keepdims=True)
    c_max = count_ge(hi)
    at_max = c_max >= kf
    state = (jnp.where(at_max, hi, lo), hi, jnp.where(at_max, c_max, n_valid.astype(F32)), c_max)

    def bisect(_, state):
        lo, hi, c_lo, c_hi = state
        mid = 0.5 * lo + 0.5 * hi
        c = count_ge(mid)
        ge = c >= kf
        return jnp.where(ge, mid, lo), jnp.where(ge, hi, mid), jnp.where(ge, c, c_lo), jnp.where(ge, c_hi, c)

    def unsettled(state):
        lo, hi, c_lo, _ = state
        open_q = jnp.where(take_all, 0.0, jnp.where(c_lo != kf, jnp.where(lo < hi, 1.0, 0.0), 0.0))

        def band_spread():
            b_min, b_max = minmax_blocks(lambda x: jnp.where((x >= lo) & (x < hi), x, BIG),
                                         lambda x: jnp.where((x >= lo) & (x < hi), x, -BIG))
            return (jnp.max(jnp.where(b_max != b_min, open_q, 0.0)) > 0.0).astype(jnp.int32)

        return lax.cond(jnp.max(open_q) > 0.0, band_spread, lambda: jnp.int32(0))

    state = lax.fori_loop(0, jnp.where((i + 1) * TQ <= topk, 0, BISECT_WARMUP), bisect, state)

    def trip(c):
        n, state, _ = c
        state = lax.fori_loop(0, BISECT_TRIP, bisect, state)
        return n + 1, state, unsettled(state)

    _, state, _ = lax.while_loop(lambda c: (c[2] > 0) & (c[0] < BISECT_MAX_TRIPS), trip,
                                 (jnp.int32(0), state, unsettled(state)))
    lo, hi, c_lo, c_hi = state
    hi_ok = lo < hi
    c_above = jnp.where(hi_ok, c_hi, 0.0)
    hi_sel = jnp.where(hi_ok, hi, BIG)
    need = jnp.where(take_all, BIG, kf - c_above)
    lo_sel = jnp.where(take_all, HALF_NEG, lo)

    tied = jnp.max(jnp.where(take_all, 0.0, c_lo - kf)) > 0.0

    @pl.when(tied)
    def _():
        tril = tril_ref[...]

        def band_of(j):
            x = sc_ref[j]
            return x, jnp.where(x >= lo_sel, jnp.where(x < hi_sel, 1.0, 0.0), 0.0)

        def write(j, x, band, rank):
            sc_ref[j] = jnp.where(x >= hi_sel, 0.0,
                                  jnp.where(band * rank > 0.0, jnp.where(rank <= need, 0.0, NEG), NEG))
            return rank[TK - 1:TK, :]

        def write_mask(j, taken):
            x, band = band_of(j)
            return write(j, x, band, _dot(tril, band.astype(BF16)) + taken)

        def write_mask_pair(t, taken):
            (x_a, band_a), (x_b, band_b) = band_of(2 * t), band_of(2 * t + 1)
            in_a, in_b = _dot(tril, band_a.astype(BF16)), _dot(tril, band_b.astype(BF16))
            taken = write(2 * t, x_a, band_a, in_a + taken)
            return write(2 * t + 1, x_b, band_b, in_b + taken)

        taken = lax.fori_loop(0, nb // 2, write_mask_pair, jnp.zeros((1, TQ), F32))
        lax.fori_loop(2 * (nb // 2), nb, write_mask, taken)

    @pl.when(jnp.logical_not(tied))
    def _():
        def write_mask(j, _):
            sc_ref[j] = jnp.where(sc_ref[j] >= lo_sel, 0.0, NEG)
            return 0

        lax.fori_loop(0, nb, write_mask, 0)

    head_bits = _moba_block_choice(i, mq_ref, mk_ref, km_ref, nblk, topb)

    def logits(j, far):
        dsa = [_dot_nt(_k_block(k_ref, h, j), q_ref[h]) + (bt_ref[h, 2, 0:1, :] if far else bt_ref[h, i - j])
               + sc_ref[j] for h in range(N_HEADS)]
        moba = []
        for h in range(N_HEADS):
            picked = (lax.shift_right_logical(head_bits[h], jnp.full_like(head_bits[h], j)) & 1) == 1
            gate = jnp.where(picked, 0.0, NEG)
            bias = mbt_ref[h, 2, 0:1, :] + gate if far else mbt_ref[h, i - j] + gate
            moba.append(_dot_nt(_k_block(mk_ref, h, j), mq_ref[h]) + bias)
        return dsa + moba

    def values(j):
        return [_vt_block(vt_ref, h, j) for h in range(N_HEADS)] + [_vt_block(mvt_ref, h, j) for h in range(N_HEADS)]

    n_far = jnp.maximum(i - 1, 0)
    carry = _softmax_loop(n_far, lambda j: logits(j, True), values,
                          tuple(_softmax_init() for _ in range(2 * N_HEADS)), groups=(4, 2, 1))
    carry = _softmax_loop(nb, lambda j: logits(j, False), values, carry, groups=(2, 1), first=n_far)
    _store_heads(o_ref, [_softmax_out(c) for c in carry[:N_HEADS]])
    _store_heads(omb_ref, [_softmax_out(c) for c in carry[N_HEADS:]])


def _moba_block_choice(i, q_ref, k_ref, km_ref, nblk, topb):
    nrow = km_ref.shape[1]

    @pl.when(i == 0)
    def _():
        km_ref[...] = jnp.zeros_like(km_ref)
        for h in range(N_HEADS):
            for n in range(nblk):
                kb = k_ref[h, n * MOBA_BLOCK:(n + 1) * MOBA_BLOCK, :].astype(F32)
                km_ref[h, n:n + 1, :] = jnp.mean(kb, axis=0, keepdims=True)

    blk = lax.broadcasted_iota(jnp.int32, (nrow, TQ), 0)
    past = blk < i
    head_bits = []
    for h in range(N_HEADS):
        gate = _dot_nt(km_ref[h].astype(BF16), q_ref[h])
        bits = jnp.zeros((1, TQ), F32)
        for n in range(nblk):
            gn = gate[n:n + 1, :]
            beats = jnp.where(past, jnp.where(gate > gn, 1.0, jnp.where(gate == gn, jnp.where(blk < n, 1.0, 0.0), 0.0)), 0.0)
            rank = jnp.sum(beats, axis=0, keepdims=True)
            bits = bits + jnp.where(rank < float(topb), jnp.where(n < i, float(2 ** n), 0.0), 0.0)
        head_bits.append(bits.astype(jnp.int32) | lax.shift_left(jnp.int32(1), i))
    return head_bits


def _dsa_moba_attention(zz, vt, wt, b, s, bw_ds, bw_mb, tril):
    nq = s // TQ
    topk = min(DSA_TOPK_MAX, s // 4)
    nblk = s // MOBA_BLOCK
    topb = min(MOBA_TOPK, nblk - 1)
    out = jax.ShapeDtypeStruct((b * s, N_HEADS * HEAD_DIM), BF16)
    return pl.pallas_call(
        functools.partial(_dsa_moba_kernel, topk=topk, nblk=nblk, topb=topb),
        grid=(b, nq),
        in_specs=_attn_specs(nq, G_DS_Q, G_DS_K, M_DS, s) + [
            pl.BlockSpec((4, TQ, HEAD_DIM), lambda b_, i: (G_QI_A, b_ * nq + i, 0)),
            pl.BlockSpec((4, TQ, HEAD_DIM), lambda b_, i: (G_QI_B, b_ * nq + i, 0)),
            pl.BlockSpec((1, s, HEAD_DIM), lambda b_, i: (S_KIDX, b_, 0)),
            pl.BlockSpec((IDX_HEADS, TQ), lambda b_, i: (0, b_ * nq + i)),
            _const_spec(BIAS_ROWS_SHAPE), _const_spec((TK, TK))]
        + _attn_specs(nq, G_MB_Q, G_MB_K, M_MB, s) + [_const_spec(BIAS_ROWS_SHAPE)],
        out_specs=[_out_spec(nq), _out_spec(nq)],
        out_shape=[out, out],
        scratch_shapes=[pltpu.VMEM((nq, TK, TQ), F32), pltpu.VMEM(BIAS_TILES_SHAPE, F32),
                        pltpu.VMEM((N_HEADS, max(8, nblk), HEAD_DIM), F32), pltpu.VMEM(BIAS_TILES_SHAPE, F32)],
        compiler_params=_cparams(2),
        name="dsa_moba_attn",
    )(zz, zz, vt, zz, zz, zz, wt, bw_ds, tril, zz, zz, vt, bw_mb)


def _merge_kernel(x_ref, osb_ref, odf_ref, ods_ref, omb_ref, gpre_ref, wg_ref, wbr_ref, wout_ref, gpost_ref, o_ref):
    x = x_ref[...]
    h = _rms(x, gpre_ref[...]).astype(BF16)
    y = jnp.zeros((x.shape[0], D_MODEL), F32)
    for r, o_r in enumerate((osb_ref, odf_ref, ods_ref, omb_ref)):
        gate = jax.nn.sigmoid(_dot(h, wg_ref[:, r * D_MODEL:(r + 1) * D_MODEL]))
        y = y + gate * _dot(o_r[...], wbr_ref[r])
    o_ref[...] = x + _rms(_dot(y.astype(BF16), wout_ref[...]), gpost_ref[...])


def _merge(x, o_sb, o_df, o_ds, o_mb, g_pre, w_gate, w_br, w_out, g_post, layer):
    t = x.shape[0]
    tm = TM_MERGE
    tok = lambda width: pl.BlockSpec((tm, width), lambda i: (i, 0))
    return pl.pallas_call(
        _merge_kernel,
        grid=(t // tm,),
        in_specs=[tok(D_MODEL)] + [tok(MIXER_WIDTH)] * N_MIXERS + [
            _const_spec((1, D_MODEL)), _layer_spec((D_MODEL, N_MIXERS * D_MODEL), layer),
            _layer_spec((N_MIXERS, MIXER_WIDTH, D_MODEL), layer), _layer_spec((D_MODEL, D_MODEL), layer),
            _const_spec((1, D_MODEL))],
        out_specs=tok(D_MODEL),
        out_shape=jax.ShapeDtypeStruct((t, D_MODEL), F32),
        compiler_params=_cparams(1),
        name="merge",
    )(x, o_sb, o_df, o_ds, o_mb, g_pre, w_gate, w_br, w_out, g_post)


def _ffn_kernel(x_ref, gpre_ref, win_ref, wout_ref, gpost_ref, o_ref):
    x = x_ref[...]
    h = _rms(x, gpre_ref[...]).astype(BF16)
    gate = _dot(h, win_ref[:, 0:D_FF])
    up = _dot(h, win_ref[:, D_FF:2 * D_FF])
    act = (gate * jax.nn.sigmoid(gate) * up).astype(BF16)
    o_ref[...] = x + _rms(_dot(act, wout_ref[...]), gpost_ref[...])


def _ffn(x, g_pre, w_in, w_out, g_post, layer):
    t = x.shape[0]
    tm = TM_FFN
    return pl.pallas_call(
        _ffn_kernel,
        grid=(t // tm,),
        in_specs=[pl.BlockSpec((tm, D_MODEL), lambda i: (i, 0)), _const_spec((1, D_MODEL)),
                  _layer_spec((D_MODEL, 2 * D_FF), layer), _layer_spec((D_FF, D_MODEL), layer),
                  _const_spec((1, D_MODEL))],
        out_specs=pl.BlockSpec((tm, D_MODEL), lambda i: (i, 0)),
        out_shape=jax.ShapeDtypeStruct((t, D_MODEL), F32),
        compiler_params=_cparams(1),
        name="ffn",
    )(x, g_pre, w_in, w_out, g_post)


def _t5_bucket(dist):
    max_exact = N_BUCKETS // 2
    d = jnp.maximum(dist, 0)
    log_ratio = jnp.log(jnp.maximum(d, 1).astype(F32) / max_exact) / math.log(MAX_DISTANCE / max_exact)
    large = jnp.minimum(max_exact + (log_ratio * (N_BUCKETS - max_exact)).astype(jnp.int32), N_BUCKETS - 1)
    return jnp.where(d < max_exact, d, large)


def _bias_rows(rel_bias):
    assert TQ == TK
    n = TK
    d = np.arange(-(n - 1), 3 * n + 1)
    by_dist = rel_bias.astype(F32).T[:, _t5_bucket(jnp.asarray(np.maximum(d, 0), jnp.int32))]
    by_dist = jnp.where(jnp.asarray(d >= 0)[None, :], by_dist * LOG2E, NEG)
    return jnp.stack([by_dist[:, o * n:o * n + 2 * n] for o in range(3)], axis=1)[:, :, None, :]


def _pack_runs():
    runs, start = [], 0
    for e in range(1, N_PACK + 1):
        if e == N_PACK or _PACK_SRC[e] != _PACK_SRC[e - 1] + (1 if _PACK_SRC[e - 1] >= 0 else 0):
            runs.append((int(_PACK_SRC[start]), e - start))
            start = e
    return runs


_PACK_RUNS = _pack_runs()
PACK_ROWS = 128


def _pack_kernel(w_ref, pack_ref, gate_ref):
    w = w_ref[...]
    parts = [w[:, a:a + n] if a >= 0 else jnp.zeros((w.shape[0], n), F32) for a, n in _PACK_RUNS]
    pack_ref[...] = jnp.concatenate(parts, axis=1).astype(BF16)
    gate_ref[...] = w[:, _OFF["gate"]:].astype(BF16)


def _pack_weights(w_in):
    depth, d, n_in = w_in.shape
    row_block = lambda width: pl.BlockSpec((pl.Squeezed(), PACK_ROWS, width), lambda l, r: (l, r, 0))
    return pl.pallas_call(
        _pack_kernel,
        grid=(depth, d // PACK_ROWS),
        in_specs=[row_block(n_in)],
        out_specs=[row_block(N_PACK), row_block(N_MIXERS * D_MODEL)],
        out_shape=[jax.ShapeDtypeStruct((depth, d, N_PACK), BF16),
                   jax.ShapeDtypeStruct((depth, d, N_MIXERS * D_MODEL), BF16)],
        compiler_params=_cparams(2),
        name="pack_weights",
    )(w_in)


def kernel(x, w_in, w_br_sb, w_br_diff, w_br_dsa, w_br_moba, w_out, lambda_q1, lambda_k1, lambda_q2, lambda_k2,
           diff_subln_g, rel_bias, w_ffn_in, w_ffn_out, g_pre_mix, g_post_mix, g_pre_ffn, g_post_ffn):
    b, s, d = x.shape
    depth = w_in.shape[0]
    assert d == D_MODEL and s % TQ == 0 and s // MOBA_BLOCK >= 2
    t = b * s

    w_pack, w_gate = _pack_weights(w_in)
    w_br = jnp.stack([w_br_sb, w_br_diff, w_br_dsa, w_br_moba], axis=1).astype(BF16)
    w_o = w_out.astype(BF16)
    w_f1 = w_ffn_in.astype(BF16)
    w_f2 = w_ffn_out.astype(BF16)
    cs = jnp.asarray(_PACK_SCALE)[None, :]

    bw = _bias_rows(rel_bias)
    bw_df, bw_ds, bw_mb = bw[0:4], bw[4:8], bw[8:12]
    key = np.arange(TK)[:, None]
    qry = np.arange(TQ)[None, :]
    tri = jnp.asarray(key <= np.arange(TK)[None, :], BF16)
    tril = jnp.asarray(key >= np.arange(TK)[None, :], BF16)
    sb_mask = jnp.asarray(np.where(key < qry, 0.0, NEG), F32)

    xf = x.reshape(t, d)
    for l in range(depth):
        lam_init = 0.8 - 0.6 * math.exp(-0.3 * l)
        lamp = jnp.stack([lambda_q1[l], lambda_k1[l], lambda_q2[l], lambda_k2[l]]).astype(F32)
        cst = jnp.full((1, 128), lam_init, F32)
        zz, vt, wt = _proj(xf, g_pre_mix[l][None, :], w_pack, cs, l)
        o_sb = _sb_attention(zz, vt, b, s, tri, sb_mask)
        o_df = _diff_attention(zz, vt, b, s, bw_df, lamp, cst, diff_subln_g[l][:, None])
        o_ds, o_mb = _dsa_moba_attention(zz, vt, wt, b, s, bw_ds, bw_mb, tril)
        xf = _merge(xf, o_sb, o_df, o_ds, o_mb, g_pre_mix[l][None, :], w_gate, w_br, w_o, g_post_mix[l][None, :], l)
        xf = _ffn(xf, g_pre_ffn[l][None, :], w_f1, w_f2, g_post_ffn[l][None, :], l)
    return xf.reshape(b, s, d)
```

```python
import functools
import math

import numpy as np
import jax
import jax.numpy as jnp
from jax import lax
from jax.experimental import pallas as pl
from jax.experimental.pallas import tpu as pltpu

F32 = jnp.float32
BF16 = jnp.bfloat16

D_MODEL = 1024
HEAD_DIM = 64
N_HEADS = 4
N_MIXERS = 4
MIXER_WIDTH = N_HEADS * HEAD_DIM
DIFF_QK_DIM = 32
IDX_HEADS = 8
DSA_TOPK_MAX = 256
MOBA_BLOCK = 256
MOBA_TOPK = 3
N_BUCKETS = 32
MAX_DISTANCE = 128
D_FF = 2816
NORM_EPS = 1e-6

TQ = 256
TK = 256
NEG = -1e30
HALF_NEG = -0.5e30
BIG = 3e38
BISECT_WARMUP = 20
BISECT_TRIP = 2
BISECT_MAX_TRIPS = 134
LOG2E = math.log2(math.e)
SB_TAIL_CUTOFF = 120.0 * LOG2E
V_ROWS = 80
TM_PROJ = 512
TM_MERGE = 512
TM_FFN = 512
N_GROUP = 11
N_SLAB = 4 * N_GROUP
N_PACK = (N_GROUP + N_MIXERS) * MIXER_WIDTH
VMEM_LIMIT = 56 * 1024 * 1024

G_SB_Q, G_SB_K, G_DF_Q, G_DF_K, G_DS_Q, G_DS_K, G_QI_A, G_QI_B, G_MB_Q, G_MB_K, G_KIDX = range(N_GROUP)
S_KIDX = 4 * G_KIDX
M_SB, M_DF, M_DS, M_MB = range(N_MIXERS)


def _layout():
    off = {}
    acc = 0
    for name, sz in (("q_sb", 256), ("k_sb", 256), ("v_sb", 256), ("q1", 128), ("q2", 128), ("k1", 128),
                     ("k2", 128), ("v_df", 256), ("q_ds", 256), ("k_ds", 256), ("v_ds", 256), ("qi", 512),
                     ("ki", 64), ("wi", 8), ("q_mb", 256), ("k_mb", 256), ("v_mb", 256), ("gate", 4096)):
        off[name] = acc
        acc += sz
    return off


_OFF = _layout()


def _pack_layout():
    off = _OFF
    cols, scale = [], []

    def add(start, n, s=1.0):
        cols.extend(range(start, start + n))
        scale.extend([s] * n)

    hd = HEAD_DIM ** -0.5
    hd2 = hd * LOG2E
    df2 = DIFF_QK_DIM ** -0.5 * LOG2E
    add(off["q_sb"], 256, hd2); add(off["k_sb"], 256)
    for h in range(N_HEADS):
        add(off["q1"] + h * 32, 32, df2); add(off["q2"] + h * 32, 32, df2)
    for h in range(N_HEADS):
        add(off["k1"] + h * 32, 32); add(off["k2"] + h * 32, 32)
    add(off["q_ds"], 256, hd2); add(off["k_ds"], 256)
    add(off["qi"], 512, HEAD_DIM ** -0.5)
    add(off["q_mb"], 256, hd2); add(off["k_mb"], 256)
    add(off["ki"], 64); add(off["wi"], IDX_HEADS, IDX_HEADS ** -0.5)
    cols.extend([-1] * 184); scale.extend([1.0] * 184)
    for name in ("v_sb", "v_df", "v_ds", "v_mb"):
        add(off[name], MIXER_WIDTH)
    assert len(cols) == N_PACK
    return np.asarray(cols, np.int32), np.asarray(scale, np.float32)


_PACK_SRC, _PACK_SCALE = _pack_layout()


def _dot(a, b):
    return jnp.dot(a, b, preferred_element_type=F32)


def _dot_nt(a, b):
    return lax.dot_general(a, b, (((1,), (1,)), ((), ())), preferred_element_type=F32)


def _rms(x, g):
    return x * lax.rsqrt(jnp.mean(x * x, axis=-1, keepdims=True) + NORM_EPS) * g


def _cparams(n_axes):
    return pltpu.CompilerParams(dimension_semantics=("arbitrary",) * n_axes, vmem_limit_bytes=VMEM_LIMIT)


def _const_spec(shape):
    nd = len(shape)
    return pl.BlockSpec(shape, lambda *_: (0,) * nd, pipeline_mode=pl.Buffered(1))


def _layer_spec(shape, layer):
    nd = len(shape)
    return pl.BlockSpec((pl.Squeezed(),) + tuple(shape), lambda *_: (layer,) + (0,) * nd, pipeline_mode=pl.Buffered(1))


def _proj_kernel(x_ref, g_ref, w_ref, cs_ref, zz_ref, vt_ref, wt_ref):
    h = _rms(x_ref[...], g_ref[...]).astype(BF16)

    def group(c):
        cols = slice(c * MIXER_WIDTH, (c + 1) * MIXER_WIDTH)
        return _dot(h, w_ref[:, cols]) * cs_ref[:, cols]

    for c in range(N_GROUP):
        r = group(c)
        for s in range(4):
            zz_ref[4 * c + s] = r[:, s * HEAD_DIM:(s + 1) * HEAD_DIM].astype(BF16)
        if c == G_KIDX:
            wt_ref[...] = jnp.transpose(r)[HEAD_DIM:HEAD_DIM + IDX_HEADS]
    for m in range(N_MIXERS):
        v = group(N_GROUP + m)
        for u in range(TM_PROJ // TK):
            vt = jnp.transpose(v[u * TK:(u + 1) * TK]).astype(BF16)
            for hh in range(N_HEADS):
                vt_ref[m, u, hh * V_ROWS:hh * V_ROWS + HEAD_DIM, :] = vt[hh * HEAD_DIM:(hh + 1) * HEAD_DIM]
                vt_ref[m, u, hh * V_ROWS + HEAD_DIM:(hh + 1) * V_ROWS, :] = jnp.ones((V_ROWS - HEAD_DIM, TK), BF16)


def _proj(x, g, w, cs, layer):
    t = x.shape[0]
    tm = TM_PROJ
    return pl.pallas_call(
        _proj_kernel,
        grid=(t // tm,),
        in_specs=[pl.BlockSpec((tm, D_MODEL), lambda i: (i, 0)),
                  _const_spec((1, D_MODEL)),
                  _layer_spec((D_MODEL, N_PACK), layer),
                  _const_spec((1, N_PACK))],
        out_specs=[pl.BlockSpec((N_SLAB, tm, HEAD_DIM), lambda i: (0, i, 0)),
                   pl.BlockSpec((N_MIXERS, tm // TK, N_HEADS * V_ROWS, TK), lambda i: (0, i, 0, 0)),
                   pl.BlockSpec((IDX_HEADS, tm), lambda i: (0, i))],
        out_shape=[jax.ShapeDtypeStruct((N_SLAB, t, HEAD_DIM), BF16),
                   jax.ShapeDtypeStruct((N_MIXERS, t // TK, N_HEADS * V_ROWS, TK), BF16),
                   jax.ShapeDtypeStruct((IDX_HEADS, t), F32)],
        compiler_params=_cparams(1),
        name="proj",
    )(x, g, w, cs)


def _k_block(ref, h, j):
    return ref[h, pl.ds(pl.multiple_of(j * TK, TK), TK), :]


def _fold_keys(a, op):
    n = a.shape[0]
    while n > 8:
        n //= 2
        a = op(a[:n], a[n:2 * n])
    return a


def _vt_block(ref, h, j, rows=V_ROWS):
    return ref[0, j, h * V_ROWS:h * V_ROWS + rows, :]


def _softmax_block(s_list, vt_list, carry):
    ms = [jnp.maximum(c[0], jnp.max(_fold_keys(s, jnp.maximum), axis=0, keepdims=True))
          for s, c in zip(s_list, carry)]
    pvs = [_dot(vt, jnp.exp2(s - m).astype(BF16)) for vt, s, m in zip(vt_list, s_list, ms)]
    return tuple((m_new, jnp.exp2(m - m_new) * acc + pv) for (m, acc), m_new, pv in zip(carry, ms, pvs))


def _softmax_loop(n_blocks, logits, values, carry, groups=(2, 1)):
    assert groups[-1] == 1
    start = 0
    for g in groups:
        def body(t, carry, g=g, start=start):
            j = start + g * t
            s_all = [logits(j + u) for u in range(g)]
            for u in range(g):
                carry = _softmax_block(s_all[u], values(j + u), carry)
            return carry

        trips = (n_blocks - start) // g
        carry = lax.fori_loop(0, trips, body, carry)
        start = start + g * trips
    return carry


def _softmax_init():
    return (jnp.full((1, TQ), NEG, F32), jnp.zeros((V_ROWS, TQ), F32))


def _softmax_out(carry):
    _, acc = carry
    return acc[:HEAD_DIM] / acc[HEAD_DIM:HEAD_DIM + 1]


def _store_heads(o_ref, heads_t):
    o_ref[...] = jnp.transpose(jnp.concatenate(heads_t, axis=0)).astype(BF16)


BIAS_ROWS_SHAPE = (N_HEADS, 3, 1, 2 * TQ)
BIAS_TILES_SHAPE = (N_HEADS, 3, TK, TQ)


def _fill_bias_tiles(bw_ref, bt_ref):
    @pl.when((pl.program_id(0) == 0) & (pl.program_id(1) == 0))
    def _():
        for h in range(N_HEADS):
            for o in range(3):
                rows = jnp.broadcast_to(bw_ref[h, o], (TK, 2 * TQ))
                bt_ref[h, o] = pltpu.roll(rows, TQ + 1, 1, stride=1, stride_axis=0)[:, :TQ]


def _attn_specs(nq, gq, gk, mixer, s):
    return [pl.BlockSpec((4, TQ, HEAD_DIM), lambda b, i: (gq, b * nq + i, 0)),
            pl.BlockSpec((4, s, HEAD_DIM), lambda b, i: (gk, b, 0)),
            pl.BlockSpec((1, s // TK, N_HEADS * V_ROWS, TK), lambda b, i: (mixer, b, 0, 0))]


def _out_spec(nq):
    return pl.BlockSpec((TQ, N_HEADS * HEAD_DIM), lambda b, i: (b * nq + i, 0))


def _sb_kernel(q_ref, k_ref, vt_ref, tri_ref, mask_ref, o_ref):
    i = pl.program_id(1)
    tri = tri_ref[...]

    def blocks(js, carry, first_masked):
        heads = range(N_HEADS)
        zs = [[_dot_nt(_k_block(k_ref, h, j), q_ref[h]) for h in heads] for j in js]
        if first_masked:
            zs[0] = [z + mask_ref[...] for z in zs[0]]
        sps = [[jnp.maximum(z, 0.0) + jnp.log2(1.0 + jnp.exp2(-jnp.abs(z))) for z in zb] for zb in zs]
        his = [[sp.astype(BF16) for sp in sb] for sb in sps]
        los = [[(sp - hi.astype(F32)).astype(BF16) for sp, hi in zip(sb, hb)] for sb, hb in zip(sps, his)]
        cums = [[_dot(tri, hi) + _dot(tri, lo) for hi, lo in zip(hb, lb)] for hb, lb in zip(his, los)]
        out = []
        for h in heads:
            tail, acc = carry[h]
            for u, j in enumerate(js):
                c = cums[u][h] + tail
                acc = acc + _dot(_vt_block(vt_ref, h, j, HEAD_DIM), jnp.exp2(zs[u][h] - c).astype(BF16))
                tail = c[0:1, :]
            out.append((tail, acc))
        return tuple(out)

    init = tuple((jnp.zeros((1, TQ), F32), jnp.zeros((HEAD_DIM, TQ), F32)) for _ in range(N_HEADS))
    carry = lax.cond(i >= 1, lambda c: blocks([i, i - 1], c, True), lambda c: blocks([i], c, True), init)

    def weights_left(carry):
        tail = functools.reduce(jnp.minimum, [c[0] for c in carry])
        return (jnp.min(tail) < SB_TAIL_CUTOFF).astype(jnp.int32)

    def earlier_block(c):
        j, carry, _ = c
        carry = blocks([j], carry, False)
        return j - 1, carry, weights_left(carry)

    _, carry, _ = lax.while_loop(lambda c: (c[0] >= 0) & (c[2] > 0), earlier_block, (i - 2, carry, weights_left(carry)))
    _store_heads(o_ref, [c[1] for c in carry])


def _sb_attention(zz, vt, b, s, tri, mask):
    nq = s // TQ
    return pl.pallas_call(
        _sb_kernel,
        grid=(b, nq),
        in_specs=_attn_specs(nq, G_SB_Q, G_SB_K, M_SB, s) + [_const_spec((TK, TK)), _const_spec((TK, TQ))],
        out_specs=_out_spec(nq),
        out_shape=jax.ShapeDtypeStruct((b * s, N_HEADS * HEAD_DIM), BF16),
        compiler_params=_cparams(2),
        name="sb_attn",
    )(zz, zz, vt, tri, mask)


def _diff_kernel(q_ref, k_ref, vt_ref, bw_ref, lam_ref, cst_ref, g_ref, o_ref, bt_ref):
    i = pl.program_id(1)
    _fill_bias_tiles(bw_ref, bt_ref)
    lp = lam_ref[...]
    lam_init = cst_ref[:, 0:1]
    lam = (jnp.exp(jnp.sum(lp[0:1] * lp[1:2], axis=-1, keepdims=True))
           - jnp.exp(jnp.sum(lp[2:3] * lp[3:4], axis=-1, keepdims=True)) + lam_init)
    lane = lax.broadcasted_iota(jnp.int32, (TQ, HEAD_DIM), 1)
    qs = []
    for h in range(N_HEADS):
        q = q_ref[h]
        qs.append((jnp.where(lane < DIFF_QK_DIM, q, jnp.zeros_like(q)),
                   jnp.where(lane >= DIFF_QK_DIM, q, jnp.zeros_like(q))))

    def logits(j):
        s_list = []
        for h in range(N_HEADS):
            kj = _k_block(k_ref, h, j)
            bias = bt_ref[h, jnp.minimum(i - j, 2)]
            s_list += [_dot_nt(kj, qs[h][0]) + bias, _dot_nt(kj, qs[h][1]) + bias]
        return s_list

    def values(j):
        return [_vt_block(vt_ref, h, j) for h in range(N_HEADS) for _ in range(2)]

    carry = _softmax_loop(i + 1, logits, values, tuple(_softmax_init() for _ in range(2 * N_HEADS)), groups=(4, 2, 1))
    outs = []
    for h in range(N_HEADS):
        o = _softmax_out(carry[2 * h]) - lam * _softmax_out(carry[2 * h + 1])
        o = o * lax.rsqrt(jnp.mean(o * o, axis=0, keepdims=True) + NORM_EPS) * g_ref[...]
        outs.append(o * (1.0 - lam_init))
    _store_heads(o_ref, outs)


def _diff_attention(zz, vt, b, s, bw, lamp, cst, g):
    nq = s // TQ
    return pl.pallas_call(
        _diff_kernel,
        grid=(b, nq),
        in_specs=_attn_specs(nq, G_DF_Q, G_DF_K, M_DF, s) + [
            _const_spec(BIAS_ROWS_SHAPE), _const_spec((4, DIFF_QK_DIM)), _const_spec((1, 128)),
            _const_spec((HEAD_DIM, 1))],
        out_specs=_out_spec(nq),
        out_shape=jax.ShapeDtypeStruct((b * s, N_HEADS * HEAD_DIM), BF16),
        scratch_shapes=[pltpu.VMEM(BIAS_TILES_SHAPE, F32)],
        compiler_params=_cparams(2),
        name="diff_attn",
    )(zz, zz, vt, bw, lamp, cst, g)


def _dsa_moba_kernel(q_ref, k_ref, vt_ref, qia_ref, qib_ref, ki_ref, wt_ref, bw_ref, tril_ref,
                     mq_ref, mk_ref, mvt_ref, mbw_ref, o_ref, omb_ref, sc_ref, bt_ref, km_ref, mbt_ref, *,
                     topk, nblk, topb):
    i = pl.program_id(1)
    _fill_bias_tiles(bw_ref, bt_ref)
    _fill_bias_tiles(mbw_ref, mbt_ref)
    nb = i + 1
    kf = float(topk)
    w = wt_ref[...]
    key = lax.broadcasted_iota(jnp.int32, (TK, TQ), 0)
    qry = lax.broadcasted_iota(jnp.int32, (TK, TQ), 1)

    def score(j):
        kij = _k_block(ki_ref, 0, j)
        sc = jnp.zeros((TK, TQ), F32)
        for hh in range(IDX_HEADS):
            qi = (qia_ref if hh < 4 else qib_ref)[hh % 4]
            sc = sc + w[hh:hh + 1, :] * jnp.maximum(_dot_nt(kij, qi), 0.0)
        return sc

    def scored(j):
        sc = score(j)
        valid = key - qry <= (i - j) * TQ
        masked = jnp.where(valid, sc, NEG)
        sc_ref[j] = masked
        return masked, jnp.where(valid, sc, BIG)

    c = (jnp.full((8, TQ), BIG, F32), jnp.full((8, TQ), -BIG, F32))
    start = 0
    for g in (4, 2, 1):
        def body(t, c, g=g, start=start):
            tiles = [scored(start + g * t + u) for u in range(g)]
            for masked, lo_src in tiles:
                c = (jnp.minimum(c[0], _fold_keys(lo_src, jnp.minimum)),
                     jnp.maximum(c[1], _fold_keys(masked, jnp.maximum)))
            return c

        trips = (nb - start) // g
        c = lax.fori_loop(0, trips, body, c)
        start = start + g * trips
    lo_part, hi_part = c

    def reduce_blocks(fn, init):
        def pair(t, c):
            return fn(sc_ref[2 * t + 1], 2 * t + 1, fn(sc_ref[2 * t], 2 * t, c))
        c = lax.fori_loop(0, nb // 2, pair, init)
        return lax.fori_loop(2 * (nb // 2), nb, lambda j, c: fn(sc_ref[j], j, c), c)

    def count_ge(t):
        part = reduce_blocks(lambda x, j, c: c + _fold_keys(jnp.where(x >= t, 1.0, 0.0), jnp.add),
                             jnp.zeros((8, TQ), F32))
        return jnp.sum(part, axis=0, keepdims=True)

    def minmax_blocks(lo_of, hi_of):
        def f(x, j, c):
            return (jnp.minimum(c[0], _fold_keys(lo_of(x), jnp.minimum)),
                    jnp.maximum(c[1], _fold_keys(hi_of(x), jnp.maximum)))
        lo_part, hi_part = reduce_blocks(f, (jnp.full((8, TQ), BIG, F32), jnp.full((8, TQ), -BIG, F32)))
        return jnp.min(lo_part, axis=0, keepdims=True), jnp.max(hi_part, axis=0, keepdims=True)

    n_valid = i * TQ + lax.broadcasted_iota(jnp.int32, (1, TQ), 1) + 1
    take_all = n_valid <= topk
    lo = jnp.min(lo_part, axis=0, keepdims=True)
    hi = jnp.max(hi_part, axis=0, keepdims=True)
    c_max = count_ge(hi)
    at_max = c_max >= kf
    state = (jnp.where(at_max, hi, lo), hi, jnp.where(at_max, c_max, n_valid.astype(F32)), c_max)

    def bisect(_, state):
        lo, hi, c_lo, c_hi = state
        mid = 0.5 * lo + 0.5 * hi
        c = count_ge(mid)
        ge = c >= kf
        return jnp.where(ge, mid, lo), jnp.where(ge, hi, mid), jnp.where(ge, c, c_lo), jnp.where(ge, c_hi, c)

    def unsettled(state):
        lo, hi, c_lo, _ = state
        open_q = jnp.where(take_all, 0.0, jnp.where(c_lo != kf, jnp.where(lo < hi, 1.0, 0.0), 0.0))

        def band_spread():
            b_min, b_max = minmax_blocks(lambda x: jnp.where((x >= lo) & (x < hi), x, BIG),
                                         lambda x: jnp.where((x >= lo) & (x < hi), x, -BIG))
            return (jnp.max(jnp.where(b_max != b_min, open_q, 0.0)) > 0.0).astype(jnp.int32)

        return lax.cond(jnp.max(open_q) > 0.0, band_spread, lambda: jnp.int32(0))

    state = lax.fori_loop(0, jnp.where((i + 1) * TQ <= topk, 0, BISECT_WARMUP), bisect, state)

    def trip(c):
        n, state, _ = c
        state = lax.fori_loop(0, BISECT_TRIP, bisect, state)
        return n + 1, state, unsettled(state)

    _, state, _ = lax.while_loop(lambda c: (c[2] > 0) & (c[0] < BISECT_MAX_TRIPS), trip,
                                 (jnp.int32(0), state, unsettled(state)))
    lo, hi, c_lo, c_hi = state
    hi_ok = lo < hi
    c_above = jnp.where(hi_ok, c_hi, 0.0)
    hi_sel = jnp.where(hi_ok, hi, BIG)
    need = jnp.where(take_all, BIG, kf - c_above)
    lo_sel = jnp.where(take_all, HALF_NEG, lo)

    tied = jnp.max(jnp.where(take_all, 0.0, c_lo - kf)) > 0.0

    @pl.when(tied)
    def _():
        tril = tril_ref[...]

        def band_of(j):
            x = sc_ref[j]
            return x, jnp.where(x >= lo_sel, jnp.where(x < hi_sel, 1.0, 0.0), 0.0)

        def write(j, x, band, rank):
            sc_ref[j] = jnp.where(x >= hi_sel, 0.0,
                                  jnp.where(band * rank > 0.0, jnp.where(rank <= need, 0.0, NEG), NEG))
            return rank[TK - 1:TK, :]

        def write_mask(j, taken):
            x, band = band_of(j)
            return write(j, x, band, _dot(tril, band.astype(BF16)) + taken)

        def write_mask_pair(t, taken):
            (x_a, band_a), (x_b, band_b) = band_of(2 * t), band_of(2 * t + 1)
            in_a, in_b = _dot(tril, band_a.astype(BF16)), _dot(tril, band_b.astype(BF16))
            taken = write(2 * t, x_a, band_a, in_a + taken)
            return write(2 * t + 1, x_b, band_b, in_b + taken)

        taken = lax.fori_loop(0, nb // 2, write_mask_pair, jnp.zeros((1, TQ), F32))
        lax.fori_loop(2 * (nb // 2), nb, write_mask, taken)

    @pl.when(jnp.logical_not(tied))
    def _():
        def write_mask(j, _):
            sc_ref[j] = jnp.where(sc_ref[j] >= lo_sel, 0.0, NEG)
            return 0

        lax.fori_loop(0, nb, write_mask, 0)

    head_bits = _moba_block_choice(i, mq_ref, mk_ref, km_ref, nblk, topb)

    def logits(j):
        dsa = [_dot_nt(_k_block(k_ref, h, j), q_ref[h]) + bt_ref[h, jnp.minimum(i - j, 2)] + sc_ref[j]
               for h in range(N_HEADS)]
        moba = []
        for h in range(N_HEADS):
            picked = (lax.shift_right_logical(head_bits[h], jnp.full_like(head_bits[h], j)) & 1) == 1
            moba.append(_dot_nt(_k_block(mk_ref, h, j), mq_ref[h]) + mbt_ref[h, jnp.minimum(i - j, 2)]
                        + jnp.where(picked, 0.0, NEG))
        return dsa + moba

    def values(j):
        return [_vt_block(vt_ref, h, j) for h in range(N_HEADS)] + [_vt_block(mvt_ref, h, j) for h in range(N_HEADS)]

    carry = _softmax_loop(nb, logits, values, tuple(_softmax_init() for _ in range(2 * N_HEADS)), groups=(4, 2, 1))
    _store_heads(o_ref, [_softmax_out(c) for c in carry[:N_HEADS]])
    _store_heads(omb_ref, [_softmax_out(c) for c in carry[N_HEADS:]])


def _moba_block_choice(i, q_ref, k_ref, km_ref, nblk, topb):
    nrow = km_ref.shape[1]

    @pl.when(i == 0)
    def _():
        km_ref[...] = jnp.zeros_like(km_ref)
        for h in range(N_HEADS):
            for n in range(nblk):
                kb = k_ref[h, n * MOBA_BLOCK:(n + 1) * MOBA_BLOCK, :].astype(F32)
                km_ref[h, n:n + 1, :] = jnp.mean(kb, axis=0, keepdims=True)

    blk = lax.broadcasted_iota(jnp.int32, (nrow, TQ), 0)
    past = blk < i
    head_bits = []
    for h in range(N_HEADS):
        gate = _dot_nt(km_ref[h].astype(BF16), q_ref[h])
        bits = jnp.zeros((1, TQ), F32)
        for n in range(nblk):
            gn = gate[n:n + 1, :]
            beats = jnp.where(past, jnp.where(gate > gn, 1.0, jnp.where(gate == gn, jnp.where(blk < n, 1.0, 0.0), 0.0)), 0.0)
            rank = jnp.sum(beats, axis=0, keepdims=True)
            bits = bits + jnp.where(rank < float(topb), jnp.where(n < i, float(2 ** n), 0.0), 0.0)
        head_bits.append(bits.astype(jnp.int32) | lax.shift_left(jnp.int32(1), i))
    return head_bits


def _dsa_moba_attention(zz, vt, wt, b, s, bw_ds, bw_mb, tril):
    nq = s // TQ
    topk = min(DSA_TOPK_MAX, s // 4)
    nblk = s // MOBA_BLOCK
    topb = min(MOBA_TOPK, nblk - 1)
    out = jax.ShapeDtypeStruct((b * s, N_HEADS * HEAD_DIM), BF16)
    return pl.pallas_call(
        functools.partial(_dsa_moba_kernel, topk=topk, nblk=nblk, topb=topb),
        grid=(b, nq),
        in_specs=_attn_specs(nq, G_DS_Q, G_DS_K, M_DS, s) + [
            pl.BlockSpec((4, TQ, HEAD_DIM), lambda b_, i: (G_QI_A, b_ * nq + i, 0)),
            pl.BlockSpec((4, TQ, HEAD_DIM), lambda b_, i: (G_QI_B, b_ * nq + i, 0)),
            pl.BlockSpec((1, s, HEAD_DIM), lambda b_, i: (S_KIDX, b_, 0)),
            pl.BlockSpec((IDX_HEADS, TQ), lambda b_, i: (0, b_ * nq + i)),
            _const_spec(BIAS_ROWS_SHAPE), _const_spec((TK, TK))]
        + _attn_specs(nq, G_MB_Q, G_MB_K, M_MB, s) + [_const_spec(BIAS_ROWS_SHAPE)],
        out_specs=[_out_spec(nq), _out_spec(nq)],
        out_shape=[out, out],
        scratch_shapes=[pltpu.VMEM((nq, TK, TQ), F32), pltpu.VMEM(BIAS_TILES_SHAPE, F32),
                        pltpu.VMEM((N_HEADS, max(8, nblk), HEAD_DIM), F32), pltpu.VMEM(BIAS_TILES_SHAPE, F32)],
        compiler_params=_cparams(2),
        name="dsa_moba_attn",
    )(zz, zz, vt, zz, zz, zz, wt, bw_ds, tril, zz, zz, vt, bw_mb)


def _merge_kernel(x_ref, osb_ref, odf_ref, ods_ref, omb_ref, gpre_ref, wg_ref, wbr_ref, wout_ref, gpost_ref, o_ref):
    x = x_ref[...]
    h = _rms(x, gpre_ref[...]).astype(BF16)
    y = jnp.zeros((x.shape[0], D_MODEL), F32)
    for r, o_r in enumerate((osb_ref, odf_ref, ods_ref, omb_ref)):
        gate = jax.nn.sigmoid(_dot(h, wg_ref[:, r * D_MODEL:(r + 1) * D_MODEL]))
        y = y + gate * _dot(o_r[...], wbr_ref[r])
    o_ref[...] = x + _rms(_dot(y.astype(BF16), wout_ref[...]), gpost_ref[...])


def _merge(x, o_sb, o_df, o_ds, o_mb, g_pre, w_gate, w_br, w_out, g_post, layer):
    t = x.shape[0]
    tm = TM_MERGE
    tok = lambda width: pl.BlockSpec((tm, width), lambda i: (i, 0))
    return pl.pallas_call(
        _merge_kernel,
        grid=(t // tm,),
        in_specs=[tok(D_MODEL)] + [tok(MIXER_WIDTH)] * N_MIXERS + [
            _const_spec((1, D_MODEL)), _layer_spec((D_MODEL, N_MIXERS * D_MODEL), layer),
            _layer_spec((N_MIXERS, MIXER_WIDTH, D_MODEL), layer), _layer_spec((D_MODEL, D_MODEL), layer),
            _const_spec((1, D_MODEL))],
        out_specs=tok(D_MODEL),
        out_shape=jax.ShapeDtypeStruct((t, D_MODEL), F32),
        compiler_params=_cparams(1),
        name="merge",
    )(x, o_sb, o_df, o_ds, o_mb, g_pre, w_gate, w_br, w_out, g_post)


def _ffn_kernel(x_ref, gpre_ref, win_ref, wout_ref, gpost_ref, o_ref):
    x = x_ref[...]
    h = _rms(x, gpre_ref[...]).astype(BF16)
    gate = _dot(h, win_ref[:, 0:D_FF])
    up = _dot(h, win_ref[:, D_FF:2 * D_FF])
    act = (gate * jax.nn.sigmoid(gate) * up).astype(BF16)
    o_ref[...] = x + _rms(_dot(act, wout_ref[...]), gpost_ref[...])


def _ffn(x, g_pre, w_in, w_out, g_post, layer):
    t = x.shape[0]
    tm = TM_FFN
    return pl.pallas_call(
        _ffn_kernel,
        grid=(t // tm,),
        in_specs=[pl.BlockSpec((tm, D_MODEL), lambda i: (i, 0)), _const_spec((1, D_MODEL)),
                  _layer_spec((D_MODEL, 2 * D_FF), layer), _layer_spec((D_FF, D_MODEL), layer),
                  _const_spec((1, D_MODEL))],
        out_specs=pl.BlockSpec((tm, D_MODEL), lambda i: (i, 0)),
        out_shape=jax.ShapeDtypeStruct((t, D_MODEL), F32),
        compiler_params=_cparams(1),
        name="ffn",
    )(x, g_pre, w_in, w_out, g_post)


def _t5_bucket(dist):
    max_exact = N_BUCKETS // 2
    d = jnp.maximum(dist, 0)
    log_ratio = jnp.log(jnp.maximum(d, 1).astype(F32) / max_exact) / math.log(MAX_DISTANCE / max_exact)
    large = jnp.minimum(max_exact + (log_ratio * (N_BUCKETS - max_exact)).astype(jnp.int32), N_BUCKETS - 1)
    return jnp.where(d < max_exact, d, large)


def _bias_rows(rel_bias):
    assert TQ == TK
    n = TK
    d = np.arange(-(n - 1), 3 * n + 1)
    by_dist = rel_bias.astype(F32).T[:, _t5_bucket(jnp.asarray(np.maximum(d, 0), jnp.int32))]
    by_dist = jnp.where(jnp.asarray(d >= 0)[None, :], by_dist * LOG2E, NEG)
    return jnp.stack([by_dist[:, o * n:o * n + 2 * n] for o in range(3)], axis=1)[:, :, None, :]


def _pack_runs():
    runs, start = [], 0
    for e in range(1, N_PACK + 1):
        if e == N_PACK or _PACK_SRC[e] != _PACK_SRC[e - 1] + (1 if _PACK_SRC[e - 1] >= 0 else 0):
            runs.append((int(_PACK_SRC[start]), e - start))
            start = e
    return runs


_PACK_RUNS = _pack_runs()
PACK_ROWS = 128


def _pack_kernel(w_ref, pack_ref, gate_ref):
    w = w_ref[...]
    parts = [w[:, a:a + n] if a >= 0 else jnp.zeros((w.shape[0], n), F32) for a, n in _PACK_RUNS]
    pack_ref[...] = jnp.concatenate(parts, axis=1).astype(BF16)
    gate_ref[...] = w[:, _OFF["gate"]:].astype(BF16)


def _pack_weights(w_in):
    depth, d, n_in = w_in.shape
    row_block = lambda width: pl.BlockSpec((pl.Squeezed(), PACK_ROWS, width), lambda l, r: (l, r, 0))
    return pl.pallas_call(
        _pack_kernel,
        grid=(depth, d // PACK_ROWS),
        in_specs=[row_block(n_in)],
        out_specs=[row_block(N_PACK), row_block(N_MIXERS * D_MODEL)],
        out_shape=[jax.ShapeDtypeStruct((depth, d, N_PACK), BF16),
                   jax.ShapeDtypeStruct((depth, d, N_MIXERS * D_MODEL), BF16)],
        compiler_params=_cparams(2),
        name="pack_weights",
    )(w_in)


def kernel(x, w_in, w_br_sb, w_br_diff, w_br_dsa, w_br_moba, w_out, lambda_q1, lambda_k1, lambda_q2, lambda_k2,
           diff_subln_g, rel_bias, w_ffn_in, w_ffn_out, g_pre_mix, g_post_mix, g_pre_ffn, g_post_ffn):
    b, s, d = x.shape
    depth = w_in.shape[0]
    assert d == D_MODEL and s % TQ == 0 and s // MOBA_BLOCK >= 2
    t = b * s

    w_pack, w_gate = _pack_weights(w_in)
    w_br = jnp.stack([w_br_sb, w_br_diff, w_br_dsa, w_br_moba], axis=1).astype(BF16)
    w_o = w_out.astype(BF16)
    w_f1 = w_ffn_in.astype(BF16)
    w_f2 = w_ffn_out.astype(BF16)
    cs = jnp.asarray(_PACK_SCALE)[None, :]

    bw = _bias_rows(rel_bias)
    bw_df, bw_ds, bw_mb = bw[0:4], bw[4:8], bw[8:12]
    key = np.arange(TK)[:, None]
    qry = np.arange(TQ)[None, :]
    tri = jnp.asarray(key <= np.arange(TK)[None, :], BF16)
    tril = jnp.asarray(key >= np.arange(TK)[None, :], BF16)
    sb_mask = jnp.asarray(np.where(key < qry, 0.0, NEG), F32)

    xf = x.reshape(t, d)
    for l in range(depth):
        lam_init = 0.8 - 0.6 * math.exp(-0.3 * l)
        lamp = jnp.stack([lambda_q1[l], lambda_k1[l], lambda_q2[l], lambda_k2[l]]).astype(F32)
        cst = jnp.full((1, 128), lam_init, F32)
        zz, vt, wt = _proj(xf, g_pre_mix[l][None, :], w_pack, cs, l)
        o_sb = _sb_attention(zz, vt, b, s, tri, sb_mask)
        o_df = _diff_attention(zz, vt, b, s, bw_df, lamp, cst, diff_subln_g[l][:, None])
        o_ds, o_mb = _dsa_moba_attention(zz, vt, wt, b, s, bw_ds, bw_mb, tril)
        xf = _merge(xf, o_sb, o_df, o_ds, o_mb, g_pre_mix[l][None, :], w_gate, w_br, w_o, g_post_mix[l][None, :], l)
        xf = _ffn(xf, g_pre_ffn[l][None, :], w_f1, w_f2, g_post_ffn[l][None, :], l)
    return xf.reshape(b, s, d)
```
